```python
import jax, jax.numpy as jnp
from jax import lax
import numpy as np

D_MODEL = 1024
BATCH = 4
SEQ = 8192
DEPTH = 2

CHUNK = 64
EPS = 1e-6
ROPE_THETA = 10000.0
A_HEADS = 4
A_DK = 128
A_DV = 128
A_WIDTH = A_HEADS * A_DV
B_HEADS = 4
B_DH = 128
B_WIDTH = B_HEADS * B_DH
Q_LORA = 256
IDX_HEADS = 16
IDX_DIM = 64
TOPK_MAX = 256
Q_BLOCK = 128
D_MIX = A_WIDTH + B_WIDTH
IN_SIZES = (A_WIDTH, A_WIDTH, A_WIDTH, A_WIDTH, Q_LORA, B_WIDTH, B_WIDTH, IDX_DIM, IDX_HEADS)
D_IN = 4 * A_WIDTH + Q_LORA + 2 * B_WIDTH + IDX_DIM + IDX_HEADS
N_MEM = 256
X_HEADS = 4
X_DH = 128
X_WIDTH = X_HEADS * X_DH
N_GROUPS = 4
EXP_PER_GROUP = 4
TOP_K_IN_GROUP = 2
D_EXPERT = 256

kernel_name = "hybrid_hgrn2_dsa_hmoe_block"


def rms_norm(x, gain):
    xf = x.astype(jnp.float32)
    y = xf * lax.rsqrt(jnp.mean(xf * xf, axis=-1, keepdims=True) + EPS)
    return (y * gain.astype(jnp.float32)).astype(x.dtype)


def rope_tables(positions, dim):
    inv = ROPE_THETA ** (-jnp.arange(0, dim, 2, dtype=jnp.float32) / dim)
    ang = positions.astype(jnp.float32)[..., None] * inv
    return jnp.cos(ang), jnp.sin(ang)


def apply_rope(x, cos, sin):
    d2 = x.shape[-1] // 2
    xf = x.astype(jnp.float32)
    x1, x2 = xf[..., :d2], xf[..., d2:]
    c, s = cos[:, :, None, :], sin[:, :, None, :]
    return jnp.concatenate([x1 * c - x2 * s, x2 * c + x1 * s], axis=-1).astype(x.dtype)


def hgrn2_mixer(q_raw, f_raw, i_raw, g_raw, lb, gnorm_gain):
    f32 = jnp.float32
    bsz, seq = q_raw.shape[:2]
    n_chunks = seq // CHUNK
    q = jax.nn.silu(q_raw.astype(f32))
    z = f_raw.astype(f32)
    log_f = jnp.logaddexp(jnp.log(lb), jnp.log1p(-lb) + jax.nn.log_sigmoid(z))
    k = (1.0 - lb) * jax.nn.sigmoid(-z)
    v = i_raw.astype(f32)

    def to_chunks(t):
        return t.reshape(bsz, n_chunks, CHUNK, A_HEADS, t.shape[-1]).transpose(1, 0, 3, 2, 4)

    causal = jnp.tril(jnp.ones((CHUNK, CHUNK), dtype=bool))[:, :, None]

    def step(state, inp):
        qb, kb, vb, lfb = inp
        b = jnp.cumsum(lfb, axis=2)
        o_inter = jnp.einsum('bhtk,bhkv->bhtv', qb * jnp.exp(b), state)
        diff = b[:, :, :, None, :] - b[:, :, None, :, :]
        decay = jnp.exp(jnp.where(causal, diff, -jnp.inf))
        scores = jnp.einsum('bhtk,bhsk,bhtsk->bhts', qb, kb, decay)
        o_intra = jnp.einsum('bhts,bhsv->bhtv', scores, vb)
        b_last = b[:, :, -1:, :]
        new_state = (jnp.exp(b_last[:, :, 0, :])[..., None] * state
                     + jnp.einsum('bhsk,bhsv->bhkv', kb * jnp.exp(b_last - b), vb))
        return new_state, o_inter + o_intra

    s0 = jnp.zeros((bsz, A_HEADS, A_DK, A_DV), f32)
    _, o = lax.scan(step, s0, tuple(map(to_chunks, (q, k, v, log_f))))
    o = o.transpose(1, 0, 3, 2, 4).reshape(bsz, seq, A_HEADS, A_DV)
    g = g_raw.reshape(bsz, seq, A_HEADS, A_DV).astype(f32)
    o = rms_norm(o, gnorm_gain) * jax.nn.silu(g)
    return o.reshape(bsz, seq, A_WIDTH).astype(q_raw.dtype)


def dsa_mixer(q_lat, k_raw, v_raw, kidx_raw, widx, cos_m, sin_m, cos_i, sin_i,
              q_lat_gain, w_qb, w_qidx, qn_gain, kn_gain, kidx_gain):
    f32 = jnp.float32
    bsz, seq = q_lat.shape[:2]
    c = rms_norm(q_lat, q_lat_gain)
    q = rms_norm((c @ w_qb).reshape(bsz, seq, B_HEADS, B_DH), qn_gain)
    q = apply_rope(q, cos_m, sin_m)
    k = apply_rope(rms_norm(k_raw.reshape(bsz, seq, B_HEADS, B_DH), kn_gain), cos_m, sin_m)
    v = v_raw.reshape(bsz, seq, B_HEADS, B_DH)
    q_idx = apply_rope((c @ w_qidx).reshape(bsz, seq, IDX_HEADS, IDX_DIM), cos_i, sin_i)
    k_idx = apply_rope(rms_norm(kidx_raw, kidx_gain)[:, :, None, :], cos_i, sin_i)[:, :, 0, :]
    w_h = widx.astype(f32) * (IDX_HEADS ** -0.5 * IDX_DIM ** -0.5)
    top_k = min(TOPK_MAX, seq // 4)
    n_blocks = seq // Q_BLOCK
    key_pos = jnp.arange(seq, dtype=jnp.int32)
    scale = B_DH ** -0.5
    gather = jax.vmap(lambda kb, ib: kb[ib])

    def block(i):
        start = i * Q_BLOCK
        qi = lax.dynamic_slice_in_dim(q_idx, start, Q_BLOCK, axis=1)
        wi = lax.dynamic_slice_in_dim(w_h, start, Q_BLOCK, axis=1)
        qb = lax.dynamic_slice_in_dim(q, start, Q_BLOCK, axis=1)
        qpos = start + jnp.arange(Q_BLOCK, dtype=jnp.int32)
        limit = (qpos // CHUNK + 1) * CHUNK
        admissible = key_pos[None, :] < limit[:, None]
        idx_logits = jnp.einsum('bqhd,bsd->bqhs', qi, k_idx).astype(f32)
        score = jnp.einsum('bqhs,bqh->bqs', jax.nn.relu(idx_logits), wi)
        score = jnp.where(admissible[None], score, -jnp.inf)
        _, sel = lax.top_k(score, top_k)
        sel_ok = sel < limit[None, :, None]
        k_sel = gather(k, sel)
        v_sel = gather(v, sel)
        logits = jnp.einsum('bqhd,bqkhd->bhqk', qb, k_sel).astype(f32) * scale
        logits = jnp.where(sel_ok[:, None], logits, -jnp.inf)
        p = jax.nn.softmax(logits, axis=-1)
        out = jnp.einsum('bhqk,bqkhd->bqhd', p.astype(v.dtype), v_sel)
        return out.reshape(bsz, Q_BLOCK, B_WIDTH)

    out = lax.map(block, jnp.arange(n_blocks, dtype=jnp.int32))
    return out.transpose(1, 0, 2, 3).reshape(bsz, seq, B_WIDTH)


def memory_cross_attention(h, m, w_q, w_k, w_v, w_o, qn_gain, kn_gain):
    bsz, seq = h.shape[:2]
    n_mem = m.shape[1]
    q = rms_norm((h @ w_q).reshape(bsz, seq, X_HEADS, X_DH), qn_gain)
    k = rms_norm((m @ w_k).reshape(bsz, n_mem, X_HEADS, X_DH), kn_gain)
    v = (m @ w_v).reshape(bsz, n_mem, X_HEADS, X_DH)
    logits = jnp.einsum('bshd,bmhd->bhsm', q, k).astype(jnp.float32) * (X_DH ** -0.5)
    p = jax.nn.softmax(logits, axis=-1)
    o = jnp.einsum('bhsm,bmhd->bshd', p.astype(v.dtype), v).reshape(bsz, seq, X_WIDTH)
    return o @ w_o


def hierarchical_moe(h, w_rg, b_rg, w_re, b_re, w_gu, w_dn):
    f32 = jnp.float32
    bsz, seq, d = h.shape
    t = h.reshape(-1, d)
    g_logits = (t @ w_rg).astype(f32) + b_rg.astype(f32)
    g_prob = jax.nn.softmax(g_logits, axis=-1)
    g_sel = jnp.argmax(g_logits, axis=-1)
    g_w = jnp.take_along_axis(g_prob, g_sel[:, None], axis=-1)
    e_all = jnp.einsum('nd,dge->nge', t, w_re).astype(f32) + b_re.astype(f32)
    e_logits = jnp.take_along_axis(e_all, g_sel[:, None, None], axis=1)[:, 0]
    top_v, top_i = lax.top_k(e_logits, TOP_K_IN_GROUP)
    top_w = jax.nn.softmax(top_v, axis=-1)
    gate_e = jnp.sum(jax.nn.one_hot(top_i, EXP_PER_GROUP, dtype=f32) * top_w[..., None], axis=1)
    gate = (jax.nn.one_hot(g_sel, N_GROUPS, dtype=f32)[:, :, None]
            * gate_e[:, None, :] * g_w[:, :, None])
    out = jnp.zeros_like(t)
    for g in range(N_GROUPS):
        gu = jnp.einsum('nd,edf->nef', t, w_gu[g])
        act = jax.nn.silu(gu[..., :D_EXPERT]) * gu[..., D_EXPERT:]
        act = act * gate[:, g, :, None].astype(act.dtype)
        out = out + jnp.einsum('nef,efd->nd', act, w_dn[g])
    return out.reshape(bsz, seq, d)


def setup_inputs(seed: int = 0) -> dict:
    key = jax.random.key(seed)
    ks = iter(jax.random.split(key, 48))
    f32 = jnp.float32

    def nrm(shape, fan_in):
        return jax.random.normal(next(ks), shape, f32) * (fan_in ** -0.5)

    def gain(shape):
        return 1.0 + 0.02 * jax.random.normal(next(ks), shape, f32)

    L = DEPTH
    x = jax.random.normal(next(ks), (BATCH, SEQ, D_MODEL), f32)
    mem = jax.random.normal(next(ks), (BATCH, N_MEM, D_MODEL), f32)
    offsets = jax.random.randint(next(ks), (BATCH, 1), 0, 16, dtype=jnp.int32) * CHUNK
    positions = (offsets + jnp.arange(SEQ, dtype=jnp.int32)[None, :]).astype(jnp.int32)
    return {
        "x": x,
        "mem": mem,
        "positions": positions,
        "norm_mix": gain((L, D_MODEL)),
        "w_in": nrm((L, D_MODEL, D_IN), D_MODEL),
        "lb_raw": 0.5 * jax.random.normal(next(ks), (L, A_WIDTH), f32),
        "a_gnorm": gain((L, A_DV)),
        "b_qlat_gain": gain((L, Q_LORA)),
        "b_wqb": nrm((L, Q_LORA, B_WIDTH), Q_LORA),
        "b_wqidx": nrm((L, Q_LORA, IDX_HEADS * IDX_DIM), Q_LORA),
        "b_qnorm": gain((L, B_DH)),
        "b_knorm": gain((L, B_DH)),
        "b_kidx_norm": gain((L, IDX_DIM)),
        "w_out": nrm((L, D_MIX, D_MODEL), D_MIX),
        "norm_x": gain((L, D_MODEL)),
        "norm_mem": gain((L, D_MODEL)),
        "x_wq": nrm((L, D_MODEL, X_WIDTH), D_MODEL),
        "x_wk": nrm((L, D_MODEL, X_WIDTH), D_MODEL),
        "x_wv": nrm((L, D_MODEL, X_WIDTH), D_MODEL),
        "x_wo": nrm((L, X_WIDTH, D_MODEL), X_WIDTH),
        "x_qnorm": gain((L, X_DH)),
        "x_knorm": gain((L, X_DH)),
        "norm_ffn": gain((L, D_MODEL)),
        "w_rg": nrm((L, D_MODEL, N_GROUPS), D_MODEL),
        "b_rg": 0.01 * jax.random.normal(next(ks), (L, N_GROUPS), f32),
        "w_re": nrm((L, D_MODEL, N_GROUPS, EXP_PER_GROUP), D_MODEL),
        "b_re": 0.01 * jax.random.normal(next(ks), (L, N_GROUPS, EXP_PER_GROUP), f32),
        "w_gu": nrm((L, N_GROUPS, EXP_PER_GROUP, D_MODEL, 2 * D_EXPERT), D_MODEL),
        "w_dn": nrm((L, N_GROUPS, EXP_PER_GROUP, D_EXPERT, D_MODEL), D_EXPERT),
    }


def reference(x, mem, positions, norm_mix, w_in, lb_raw, a_gnorm, b_qlat_gain, b_wqb,
              b_wqidx, b_qnorm, b_knorm, b_kidx_norm, w_out, norm_x, norm_mem, x_wq, x_wk,
              x_wv, x_wo, x_qnorm, x_knorm, norm_ffn, w_rg, b_rg, w_re, b_re, w_gu, w_dn):
    bsz, seq, _ = x.shape
    cos_m, sin_m = rope_tables(positions, B_DH)
    cos_i, sin_i = rope_tables(positions, IDX_DIM)
    lb_all = jnp.cumsum(jax.nn.softmax(lb_raw.astype(jnp.float32), axis=0), axis=0)
    lb_all = lb_all - lb_all[0:1]
    split_at = np.cumsum(IN_SIZES)[:-1].tolist()
    for l in range(DEPTH):
        h = rms_norm(x, norm_mix[l])
        u = h @ w_in[l]
        a_q, a_f, a_i, a_g, b_ql, b_k, b_v, b_kidx, b_widx = jnp.split(u, split_at, axis=-1)
        heads4 = lambda t: t.reshape(bsz, seq, A_HEADS, A_DK)
        o_a = hgrn2_mixer(heads4(a_q), heads4(a_f), heads4(a_i), a_g,
                          lb_all[l].reshape(A_HEADS, A_DK), a_gnorm[l])
        o_b = dsa_mixer(b_ql, b_k, b_v, b_kidx, b_widx, cos_m, sin_m, cos_i, sin_i,
                        b_qlat_gain[l], b_wqb[l], b_wqidx[l], b_qnorm[l], b_knorm[l],
                        b_kidx_norm[l])
        x = x + jnp.concatenate([o_a, o_b], axis=-1) @ w_out[l]
        h = rms_norm(x, norm_x[l])
        m = rms_norm(mem, norm_mem[l])
        x = x + memory_cross_attention(h, m, x_wq[l], x_wk[l], x_wv[l], x_wo[l],
                                       x_qnorm[l], x_knorm[l])
        h = rms_norm(x, norm_ffn[l])
        x = x + hierarchical_moe(h, w_rg[l], b_rg[l], w_re[l], b_re[l], w_gu[l], w_dn[l])
    return x
```

```python
import functools

import numpy as np
import jax
import jax.numpy as jnp
from jax import lax
from jax.experimental import pallas as pl
from jax.experimental.pallas import tpu as pltpu

F32 = jnp.float32
BF16 = jnp.bfloat16
I32 = jnp.int32

EPS = 1e-6
ROPE_THETA = 10000.0
CHUNK = 64
CHUNK_SHIFT = 6
SUB = 16
SUB_SHIFT = 4
N_HEADS = 4
HEAD_DIM = 128
WIDTH = N_HEADS * HEAD_DIM
Q_LORA = 256
IDX_HEADS = 16
IDX_DIM = 64
TOPK_MAX = 256
N_GROUPS = 4
EXP_PER_GROUP = 4
N_EXPERTS = N_GROUPS * EXP_PER_GROUP
D_EXPERT = 256
LANES = 128
INT_MIN = -2 ** 31
NEG_INF_BITS = -8388608

U_AQ, U_AF, U_AI, U_AG = 0, 512, 1024, 1536
U_K, U_V, U_QLAT, U_MISC = 2048, 2560, 3072, 3328
U_COLS = 3456

VMEM_LIMIT = 56 * 1024 * 1024


def _rms(x, gain):
    return x * lax.rsqrt(jnp.mean(x * x, axis=-1, keepdims=True) + EPS) * gain


def _dot(a, b):
    return jnp.dot(a, b, preferred_element_type=F32)


def _dot_nt(a, b):
    return lax.dot_general(a, b, (((1,), (1,)), ((), ())), preferred_element_type=F32)


def _params(*sem):
    return pltpu.CompilerParams(dimension_semantics=sem, vmem_limit_bytes=VMEM_LIMIT)


def _in_proj_kernel(x_ref, g_ref, w_ref, u_ref, *, col_chunk):
    hb = _rms(x_ref[...], g_ref[...]).astype(BF16)
    for c0 in range(0, U_COLS, col_chunk):
        u_ref[:, c0:c0 + col_chunk] = _dot(hb, w_ref[:, c0:c0 + col_chunk])


def _in_proj(x2, gain, w_bf, tm=256):
    n, d = x2.shape
    return pl.pallas_call(
        functools.partial(_in_proj_kernel, col_chunk=1152),
        out_shape=jax.ShapeDtypeStruct((n, U_COLS), F32),
        grid=(n // tm,),
        in_specs=[pl.BlockSpec((tm, d), lambda i: (i, 0)),
                  pl.BlockSpec((1, d), lambda i: (0, 0)),
                  pl.BlockSpec((d, U_COLS), lambda i: (0, 0))],
        out_specs=pl.BlockSpec((tm, U_COLS), lambda i: (i, 0)),
        compiler_params=_params("parallel"),
        name="in_proj",
    )(x2, gain, w_bf)


def _split3(x):
    hi = x.astype(BF16)
    r1 = x - hi.astype(F32)
    mid = r1.astype(BF16)
    lo = (r1 - mid.astype(F32)).astype(BF16)
    return hi, mid, lo


def _hgrn_kernel(q_ref, f_ref, i_ref, g_ref, lbraw_ref, gn_ref, e_ref, o_ref, st_ref,
                 *, layer, n_chunks):
    @pl.when(pl.program_id(2) == 0)
    def _():
        st_ref[...] = jnp.zeros_like(st_ref)

    lr = lbraw_ref[...]
    ex = jnp.exp(lr - jnp.max(lr, axis=0, keepdims=True))
    sm = ex / jnp.sum(ex, axis=0, keepdims=True)
    lb = jnp.zeros((1, HEAD_DIM), F32)
    for r in range(1, layer + 1):
        lb = lb + sm[r:r + 1, :]
    log_lb = jnp.log(lb)
    log_1mlb = jnp.log(1.0 - lb)

    row = lax.broadcasted_iota(I32, (CHUNK, HEAD_DIM), 0)
    col = lax.broadcasted_iota(I32, (CHUNK, HEAD_DIM), 1)
    row_l = row & (SUB - 1)
    row_b = row >> SUB_SHIFT
    col_b = col >> SUB_SHIFT
    tri = (lax.broadcasted_iota(I32, (CHUNK, CHUNK), 0)
           >= lax.broadcasted_iota(I32, (CHUNK, CHUNK), 1)).astype(BF16)
    n_sub = CHUNK // SUB
    zeros_c = jnp.zeros((CHUNK, HEAD_DIM), F32)

    def sub_bcast(t, s):
        t4 = t.reshape(n_sub, SUB, HEAD_DIM)
        return jnp.broadcast_to(t4[:, s:s + 1, :], (n_sub, SUB, HEAD_DIM)).reshape(CHUNK, HEAD_DIM)

    def chunk(c, carry):
        rows = pl.ds(pl.multiple_of(c * CHUNK, CHUNK), CHUNK)
        qr = q_ref[rows, :]
        z = f_ref[rows, :]
        v = i_ref[rows, :]
        g = g_ref[rows, :]
        q = qr * jax.nn.sigmoid(qr)
        ls = jnp.minimum(z, 0.0) - jnp.log(1.0 + jnp.exp(-jnp.abs(z)))
        cc = log_1mlb + ls
        log_f = jnp.maximum(log_lb, cc) + jnp.log(1.0 + jnp.exp(-jnp.abs(log_lb - cc)))
        kg = (1.0 - lb) * jax.nn.sigmoid(-z)
        hi, mid, lo = _split3(log_f)
        b = _dot(tri, hi) + _dot(tri, mid) + _dot(tri, lo)
        b_last = b[CHUNK - 1:CHUNK, :]
        st = st_ref[...]
        o_inter = _dot_nt((q * jnp.exp(b)).astype(BF16), st.astype(BF16))
        kk = kg * jnp.exp(b_last - b)
        st_ref[...] = st * jnp.exp(b_last) + _dot(v.T.astype(BF16), kk.astype(BF16))

        pieces = []
        for s in range(SUB):
            d = jnp.where(row_l >= s, b - sub_bcast(b, s), -jnp.inf)
            pieces.append((q * sub_bcast(kg, s) * jnp.exp(d)).astype(BF16))
        a2 = _dot(jnp.concatenate(pieces, axis=1), e_ref[...])
        a = jnp.where(col_b == row_b, a2, 0.0)
        qt = (q * jnp.exp(b - sub_bcast(b, 0))).astype(BF16)
        kts = []
        for i in range(1, n_sub):
            r_i = b[i * SUB:i * SUB + 1, :]
            kts.append(kg * jnp.exp(jnp.where(row < i * SUB, r_i - b, -jnp.inf)))
            kts.append(zeros_c)
        r = _dot_nt(qt, jnp.concatenate(kts, axis=0).astype(BF16))
        for i in range(1, n_sub):
            a = a + jnp.where(row_b == i, r[:, (i - 1) * LANES:i * LANES], 0.0)
        o = o_inter + _dot(a[:, :CHUNK].astype(BF16), v.astype(BF16))
        o = _rms(o, gn_ref[...]) * (g * jax.nn.sigmoid(g))
        o_ref[rows, :] = o
        return carry

    lax.fori_loop(0, n_chunks, chunk, 0)


def _hgrn_selector():
    e = np.zeros((SUB * HEAD_DIM, LANES), np.float32)
    s_of_row = np.arange(SUB * HEAD_DIM) // HEAD_DIM
    cols = np.arange(LANES)
    e[:, :] = ((cols[None, :] % SUB) == s_of_row[:, None]) & (cols[None, :] < CHUNK)
    return jnp.asarray(e, BF16)


def _hgrn(u, lb_raw, gn, layer, bsz, seq, tb=512):
    n = bsz * seq
    nb = seq // tb
    blk = lambda k: pl.BlockSpec((tb, HEAD_DIM), lambda b, h, j, k=k: (b * nb + j, h + N_HEADS * k))
    n_layers = lb_raw.shape[0]
    return pl.pallas_call(
        functools.partial(_hgrn_kernel, layer=layer, n_chunks=tb // CHUNK),
        out_shape=jax.ShapeDtypeStruct((n, WIDTH), F32),
        grid=(bsz, N_HEADS, nb),
        in_specs=[blk(0), blk(1), blk(2), blk(3),
                  pl.BlockSpec((n_layers, HEAD_DIM), lambda b, h, j: (0, h)),
                  pl.BlockSpec((1, HEAD_DIM), lambda b, h, j: (0, 0)),
                  pl.BlockSpec((SUB * HEAD_DIM, LANES), lambda b, h, j: (0, 0))],
        out_specs=pl.BlockSpec((tb, HEAD_DIM), lambda b, h, j: (b * nb + j, h)),
        scratch_shapes=[pltpu.VMEM((HEAD_DIM, HEAD_DIM), F32)],
        compiler_params=_params("parallel", "parallel", "arbitrary"),
        name="hgrn",
    )(u, u, u, u, lb_raw, gn, _hgrn_selector())


def _rope128(x, cos, sin_signed):
    return x * cos + pltpu.roll(x, 64, 1) * sin_signed


def _rope64(x, cos, sin_a, sin_b):
    return x * cos + pltpu.roll(x, 96, 1) * sin_a + pltpu.roll(x, 32, 1) * sin_b


def _dsa_prep_kernel(k_ref, v_ref, ql_ref, misc_ref, cm_ref, sm_ref, ci_ref, sia_ref, sib_ref,
                     wqb_ref, wqi_ref, gql_ref, gqn_ref, gkn_ref, gki_ref,
                     q_out, k_out, vt_out, qi_out, ki_out, wt_out):
    cm, sm = cm_ref[...], sm_ref[...]
    ci, sia, sib = ci_ref[...], sia_ref[...], sib_ref[...]
    cb = _rms(ql_ref[...], gql_ref[...]).astype(BF16)
    qf = _dot(cb, wqb_ref[...])
    scale = HEAD_DIM ** -0.5
    for h in range(N_HEADS):
        sl = slice(h * HEAD_DIM, (h + 1) * HEAD_DIM)
        qh = _rope128(_rms(qf[:, sl], gqn_ref[...]), cm, sm)
        q_out[:, sl] = (qh * scale).astype(BF16)
        kh = _rope128(_rms(k_ref[:, sl], gkn_ref[...]), cm, sm)
        k_out[:, sl] = kh.astype(BF16)
    vt_out[0] = v_ref[...].T.astype(BF16)
    qi = _dot(cb, wqi_ref[...])
    for p in range(IDX_HEADS * IDX_DIM // LANES):
        sl = slice(p * LANES, (p + 1) * LANES)
        qi_out[:, sl] = _rope64(qi[:, sl], ci, sia, sib).astype(BF16)
    misc = misc_ref[...]
    lane = lax.broadcasted_iota(I32, misc.shape, 1)
    kraw = jnp.where(lane < IDX_DIM, misc, 0.0)
    kn = kraw * lax.rsqrt(jnp.sum(kraw * kraw, axis=-1, keepdims=True) * (1.0 / IDX_DIM) + EPS)
    kn = _rope64(kn * gki_ref[...], ci, sia, sib)
    ki_out[...] = (kn + pltpu.roll(kn, IDX_DIM, 1)).astype(BF16)
    wt = (misc * (IDX_HEADS ** -0.5 * IDX_DIM ** -0.5)).T
    wt_out[...] = wt[IDX_DIM:IDX_DIM + IDX_HEADS, :]


def _dsa_prep(u, tabs, wqb, wqi, gql, gqn, gkn, gki, tm=256):
    n = u.shape[0]
    nt = n // tm
    ublk = lambda width, col: pl.BlockSpec((tm, width), lambda i: (i, col // width))
    tab = pl.BlockSpec((tm, LANES), lambda i: (i, 0))
    full = lambda a: pl.BlockSpec(a.shape, lambda i: (0,) * a.ndim)
    return pl.pallas_call(
        _dsa_prep_kernel,
        out_shape=(jax.ShapeDtypeStruct((n, WIDTH), BF16),
                   jax.ShapeDtypeStruct((n, WIDTH), BF16),
                   jax.ShapeDtypeStruct((nt, WIDTH, tm), BF16),
                   jax.ShapeDtypeStruct((n, IDX_HEADS * IDX_DIM), BF16),
                   jax.ShapeDtypeStruct((n, LANES), BF16),
                   jax.ShapeDtypeStruct((IDX_HEADS, n), F32)),
        grid=(nt,),
        in_specs=[ublk(WIDTH, U_K), ublk(WIDTH, U_V), ublk(Q_LORA, U_QLAT), ublk(LANES, U_MISC),
                  tab, tab, tab, tab, tab,
                  full(wqb), full(wqi), full(gql), full(gqn), full(gkn), full(gki)],
        out_specs=(pl.BlockSpec((tm, WIDTH), lambda i: (i, 0)),
                   pl.BlockSpec((tm, WIDTH), lambda i: (i, 0)),
                   pl.BlockSpec((1, WIDTH, tm), lambda i: (i, 0, 0)),
                   pl.BlockSpec((tm, IDX_HEADS * IDX_DIM), lambda i: (i, 0)),
                   pl.BlockSpec((tm, LANES), lambda i: (i, 0)),
                   pl.BlockSpec((IDX_HEADS, tm), lambda i: (0, i))),
        compiler_params=_params("parallel"),
        name="dsa_prep",
    )(u, u, u, u, *tabs, wqb, wqi, gql, gqn, gkn, gki)


def _sortable(x):
    bits = lax.bitcast_convert_type(x, I32)
    return bits ^ ((bits >> 31) & 0x7FFFFFFF)


def _dsa_attn_kernel(qi_ref, wt_ref, q_ref, ki_ref, k_ref, vt_ref, o_ref,
                     qm_ref, strip_ref, acc_ref, m_ref, l_ref, *, tq, kt, top_k):
    q0 = pl.program_id(1) * tq
    n_tiles = (q0 + tq + kt - 1) // kt

    lane_q = lax.broadcasted_iota(I32, (tq, LANES), 1)
    for h in range(IDX_HEADS):
        src = qi_ref[:, (h // 2) * LANES:(h // 2 + 1) * LANES].astype(F32)
        keep = (lane_q < IDX_DIM) if h % 2 == 0 else (lane_q >= IDX_DIM)
        qm_ref[h * tq:(h + 1) * tq, :] = jnp.where(keep, src, 0.0).astype(BF16)

    qpos = q0 + lax.broadcasted_iota(I32, (1, tq), 1)
    limit = ((qpos >> CHUNK_SHIFT) + 1) << CHUNK_SHIFT
    key_iota = lax.broadcasted_iota(I32, (kt, tq), 0)

    def tile_rows(jt):
        return pl.ds(pl.multiple_of(jt * kt, kt), kt)

    def score_tile(jt, carry):
        rows = tile_rows(jt)
        x = _dot_nt(ki_ref[rows, :], qm_ref[...])
        acc = jnp.zeros((kt, tq), F32)
        for h in range(IDX_HEADS):
            acc = acc + wt_ref[h:h + 1, :] * jnp.maximum(x[:, h * tq:(h + 1) * tq], 0.0)
        key = jnp.where(jt * kt + key_iota < limit, _sortable(acc), INT_MIN)
        strip_ref[rows, :] = key
        return carry

    lax.fori_loop(0, n_tiles, score_tile, 0)

    def count(pred_fn):
        def body(jt, c):
            hit = pred_fn(strip_ref[tile_rows(jt), :]).astype(I32)
            return c + jnp.sum(hit.reshape(kt // 8, 8, tq), axis=0)
        c8 = lax.fori_loop(0, n_tiles, body, jnp.zeros((8, tq), I32))
        return jnp.sum(c8, axis=0, keepdims=True)

    def bisect(it, lo):
        trial = lo + lax.shift_left(jnp.int32(1), 31 - it)
        return jnp.where(count(lambda kv: kv >= trial) >= top_k, trial, lo)

    tau = lax.fori_loop(0, 32, bisect, jnp.full((1, tq), INT_MIN, I32))
    n_ge = count(lambda kv: kv >= tau)
    live = tau > INT_MIN
    excess = jnp.max(jnp.where(live, n_ge - top_k, 0).astype(F32))

    @pl.when(excess <= 0)
    def _():
        thr = jnp.maximum(tau, INT_MIN + 1)

        def body(jt, carry):
            rows = tile_rows(jt)
            strip_ref[rows, :] = jnp.where(strip_ref[rows, :] >= thr, 0, NEG_INF_BITS)
            return carry
        lax.fori_loop(0, n_tiles, body, 0)

    @pl.when(excess > 0)
    def _():
        need = (top_k - count(lambda kv: kv > tau)).astype(F32)
        tri = (lax.broadcasted_iota(I32, (kt, kt), 0)
               >= lax.broadcasted_iota(I32, (kt, kt), 1)).astype(BF16)

        def body(jt, seen):
            rows = tile_rows(jt)
            kv = strip_ref[rows, :]
            eq = kv == tau
            rank = _dot(tri, jnp.where(eq, 1.0, 0.0).astype(BF16)) + seen
            sel = (kv > tau) | (eq & (rank <= need) & live)
            strip_ref[rows, :] = jnp.where(sel, 0, NEG_INF_BITS)
            return rank[kt - 1:kt, :]
        lax.fori_loop(0, n_tiles, body, jnp.zeros((1, tq), F32))

    m_ref[...] = jnp.full_like(m_ref, -jnp.inf)
    l_ref[...] = jnp.zeros_like(l_ref)
    acc_ref[...] = jnp.zeros_like(acc_ref)

    def attn_tile(jt, carry):
        rows = tile_rows(jt)
        bias = lax.bitcast_convert_type(strip_ref[rows, :], F32)
        for h in range(N_HEADS):
            sl = slice(h * HEAD_DIM, (h + 1) * HEAD_DIM)
            lt = _dot_nt(k_ref[rows, sl], q_ref[:, sl]) + bias
            m_old = m_ref[h:h + 1, :]
            m_new = jnp.maximum(m_old, jnp.max(lt, axis=0, keepdims=True))
            m_safe = jnp.where(m_new == -jnp.inf, 0.0, m_new)
            p = jnp.exp(lt - m_safe)
            alpha = jnp.exp(m_old - m_safe)
            l_ref[h:h + 1, :] = alpha * l_ref[h:h + 1, :] + jnp.sum(p, axis=0, keepdims=True)
            acc_ref[h] = acc_ref[h] * alpha + _dot(vt_ref[jt, sl, :], p.astype(BF16))
            m_ref[h:h + 1, :] = m_new
        return carry

    lax.fori_loop(0, n_tiles, attn_tile, 0)
    for h in range(N_HEADS):
        out_t = acc_ref[h] * (1.0 / l_ref[h:h + 1, :])
        o_ref[:, h * HEAD_DIM:(h + 1) * HEAD_DIM] = out_t.T


def _dsa_attn(q, k, vt, qi, ki, wt, bsz, seq, tq=128, kt=256):
    n = bsz * seq
    nq = seq // tq
    top_k = min(TOPK_MAX, seq // 4)
    once = dict(pipeline_mode=pl.Buffered(1))
    return pl.pallas_call(
        functools.partial(_dsa_attn_kernel, tq=tq, kt=kt, top_k=top_k),
        out_shape=jax.ShapeDtypeStruct((n, WIDTH), F32),
        grid=(bsz, nq),
        in_specs=[pl.BlockSpec((tq, IDX_HEADS * IDX_DIM), lambda b, i: (b * nq + i, 0)),
                  pl.BlockSpec((IDX_HEADS, tq), lambda b, i: (0, b * nq + i)),
                  pl.BlockSpec((tq, WIDTH), lambda b, i: (b * nq + i, 0)),
                  pl.BlockSpec((seq, LANES), lambda b, i: (b, 0), **once),
                  pl.BlockSpec((seq, WIDTH), lambda b, i: (b, 0), **once),
                  pl.BlockSpec((seq // kt, WIDTH, kt), lambda b, i: (b, 0, 0), **once)],
        out_specs=pl.BlockSpec((tq, WIDTH), lambda b, i: (b * nq + i, 0)),
        scratch_shapes=[pltpu.VMEM((IDX_HEADS * tq, LANES), BF16),
                        pltpu.VMEM((seq, tq), I32),
                        pltpu.VMEM((N_HEADS, HEAD_DIM, tq), F32),
                        pltpu.VMEM((N_HEADS, tq), F32),
                        pltpu.VMEM((N_HEADS, tq), F32)],
        compiler_params=_params("parallel", "arbitrary"),
        name="dsa_attn",
    )(qi, wt, q, ki, k, vt)


def _mem_kv_kernel(mem_ref, g_ref, wk_ref, wv_ref, gk_ref, k_out, v_out):
    mb = _rms(mem_ref[...], g_ref[...]).astype(BF16)
    kf = _dot(mb, wk_ref[...])
    for h in range(N_HEADS):
        sl = slice(h * HEAD_DIM, (h + 1) * HEAD_DIM)
        k_out[:, sl] = _rms(kf[:, sl], gk_ref[...]).astype(BF16)
    v_out[...] = _dot(mb, wv_ref[...]).astype(BF16)


def _mem_kv(mem2, gain, wk, wv, gk, n_mem):
    n, d = mem2.shape
    full = lambda a: pl.BlockSpec(a.shape, lambda b: (0,) * a.ndim)
    return pl.pallas_call(
        _mem_kv_kernel,
        out_shape=(jax.ShapeDtypeStruct((n, WIDTH), BF16), jax.ShapeDtypeStruct((n, WIDTH), BF16)),
        grid=(n // n_mem,),
        in_specs=[pl.BlockSpec((n_mem, d), lambda b: (b, 0)), full(gain), full(wk), full(wv), full(gk)],
        out_specs=(pl.BlockSpec((n_mem, WIDTH), lambda b: (b, 0)),
                   pl.BlockSpec((n_mem, WIDTH), lambda b: (b, 0))),
        compiler_params=_params("parallel"),
        name="mem_kv",
    )(mem2, gain, wk, wv, gk)


def _xattn_kernel(x_ref, oa_ref, ob_ref, wout_ref, g_ref, wq_ref, gq_ref, km_ref, vm_ref, wo_ref, o_ref):
    x1 = (x_ref[...] + _dot(oa_ref[...].astype(BF16), wout_ref[:WIDTH, :])
          + _dot(ob_ref[...].astype(BF16), wout_ref[WIDTH:, :]))
    hb = _rms(x1, g_ref[...]).astype(BF16)
    qf = _dot(hb, wq_ref[...])
    scale = HEAD_DIM ** -0.5
    outs = []
    for h in range(N_HEADS):
        sl = slice(h * HEAD_DIM, (h + 1) * HEAD_DIM)
        qh = (_rms(qf[:, sl], gq_ref[...]) * scale).astype(BF16)
        logits = _dot_nt(qh, km_ref[:, sl])
        p = jnp.exp(logits - jnp.max(logits, axis=-1, keepdims=True))
        oh = _dot(p.astype(BF16), vm_ref[:, sl])
        outs.append((oh * (1.0 / jnp.sum(p, axis=-1, keepdims=True))).astype(BF16))
    o_ref[...] = x1 + _dot(jnp.concatenate(outs, axis=1), wo_ref[...])


def _xattn(x2, oa, ob, wout, gain, wq, gq, km, vm, wo, seq, n_mem, tm=512):
    n, d = x2.shape
    per_b = seq // tm
    row = lambda w: pl.BlockSpec((tm, w), lambda i: (i, 0))
    full = lambda a: pl.BlockSpec(a.shape, lambda i: (0,) * a.ndim)
    memb = pl.BlockSpec((n_mem, WIDTH), lambda i: (i // per_b, 0))
    return pl.pallas_call(
        _xattn_kernel,
        out_shape=jax.ShapeDtypeStruct((n, d), F32),
        grid=(n // tm,),
        in_specs=[row(d), row(WIDTH), row(WIDTH), full(wout), full(gain), full(wq), full(gq),
                  memb, memb, full(wo)],
        out_specs=row(d),
        compiler_params=_params("parallel"),
        name="xattn",
    )(x2, oa, ob, wout, gain, wq, gq, km, vm, wo)


def _moe_kernel(x_ref, g_ref, wr_hi_ref, wr_lo_ref, br_ref, wgu_ref, wdn_ref, o_ref, hb_ref, gate_ref):
    e = pl.program_id(1)
    lane = lax.broadcasted_iota(I32, gate_ref.shape, 1)
    lane_f = lane.astype(F32)

    @pl.when(e == 0)
    def _():
        x = x_ref[...]
        h = _rms(x, g_ref[...])
        h_hi = h.astype(BF16)
        h_lo = (h - h_hi.astype(F32)).astype(BF16)
        hb_ref[...] = h_hi
        lg = (_dot(h_hi, wr_hi_ref[...]) + _dot(h_lo, wr_hi_ref[...]) + _dot(h_hi, wr_lo_ref[...])
              + br_ref[...])
        first = lambda cond: jnp.min(jnp.where(cond, lane_f, 1e9), axis=-1, keepdims=True)
        gl = jnp.where(lane < N_GROUPS, lg, -jnp.inf)
        gmax = jnp.max(gl, axis=-1, keepdims=True)
        gsel = first(gl == gmax)
        g_w = 1.0 / jnp.sum(jnp.exp(gl - gmax), axis=-1, keepdims=True)
        grp_of_lane = ((lane - N_GROUPS) >> 2).astype(F32)
        in_grp = (lane >= N_GROUPS) & (lane < N_GROUPS + N_EXPERTS) & (grp_of_lane == gsel)
        el = jnp.where(in_grp, lg, -jnp.inf)
        v1 = jnp.max(el, axis=-1, keepdims=True)
        i1 = first(el == v1)
        el2 = jnp.where(lane_f == i1, -jnp.inf, el)
        v2 = jnp.max(el2, axis=-1, keepdims=True)
        i2 = first(el2 == v2)
        r = jnp.exp(v2 - v1)
        w1 = 1.0 / (1.0 + r)
        gate_ref[...] = jnp.where(lane_f == i1, w1 * g_w, jnp.where(lane_f == i2, r * w1 * g_w, 0.0))
        o_ref[...] = x

    ge = jnp.sum(jnp.where(lane == e + N_GROUPS, gate_ref[...], 0.0), axis=-1, keepdims=True)
    gu = _dot(hb_ref[...], wgu_ref[0])
    up = gu[:, :D_EXPERT]
    act = up * jax.nn.sigmoid(up) * gu[:, D_EXPERT:] * ge
    o_ref[...] += _dot(act.astype(BF16), wdn_ref[0])


def _moe(x2, gain, wr_hi, wr_lo, br, wgu, wdn, tm=1024):
    n, d = x2.shape
    full = lambda a: pl.BlockSpec(a.shape, lambda i, e: (0,) * a.ndim)
    return pl.pallas_call(
        _moe_kernel,
        out_shape=jax.ShapeDtypeStruct((n, d), F32),
        grid=(n // tm, N_EXPERTS),
        in_specs=[pl.BlockSpec((tm, d), lambda i, e: (i, 0)), full(gain), full(wr_hi), full(wr_lo), full(br),
                  pl.BlockSpec((1, d, 2 * D_EXPERT), lambda i, e: (e, 0, 0)),
                  pl.BlockSpec((1, D_EXPERT, d), lambda i, e: (e, 0, 0))],
        out_specs=pl.BlockSpec((tm, d), lambda i, e: (i, 0)),
        scratch_shapes=[pltpu.VMEM((tm, d), BF16), pltpu.VMEM((tm, LANES), F32)],
        compiler_params=_params("parallel", "arbitrary"),
        name="moe",
    )(x2, gain, wr_hi, wr_lo, br, wgu, wdn)


def _rope_tables(positions):
    pos = positions.reshape(-1).astype(F32)[:, None]

    def cs(dim):
        inv = ROPE_THETA ** (-jnp.arange(0, dim, 2, dtype=F32) / dim)
        ang = pos * inv
        return jnp.cos(ang), jnp.sin(ang)

    c, s = cs(HEAD_DIM)
    ci, si = cs(IDX_DIM)
    z = jnp.zeros_like(si)
    return (jnp.concatenate([c, c], 1), jnp.concatenate([-s, s], 1),
            jnp.concatenate([ci, ci, ci, ci], 1), jnp.concatenate([-si, z, -si, z], 1),
            jnp.concatenate([z, si, z, si], 1))


def _reorder_in_weight(w):
    a = w[:, :4 * WIDTH]
    o = 4 * WIDTH
    qlat = w[:, o:o + Q_LORA]; o += Q_LORA
    k = w[:, o:o + WIDTH]; o += WIDTH
    v = w[:, o:o + WIDTH]; o += WIDTH
    misc = w[:, o:]
    pad = jnp.zeros((w.shape[0], LANES - misc.shape[1]), w.dtype)
    return jnp.concatenate([a, k, v, qlat, misc, pad], axis=1).astype(BF16)


def _row(v):
    return v.reshape(1, -1).astype(F32)


def kernel(x, mem, positions, norm_mix, w_in, lb_raw, a_gnorm, b_qlat_gain, b_wqb, b_wqidx, b_qnorm,
           b_knorm, b_kidx_norm, w_out, norm_x, norm_mem, x_wq, x_wk, x_wv, x_wo, x_qnorm, x_knorm,
           norm_ffn, w_rg, b_rg, w_re, b_re, w_gu, w_dn):
    bsz, seq, d = x.shape
    n_mem = mem.shape[1]
    depth = w_in.shape[0]
    x2 = x.reshape(bsz * seq, d)
    mem2 = mem.reshape(bsz * n_mem, d)
    tabs = _rope_tables(positions)
    pad_lanes = lambda v: jnp.pad(v, ((0, 0), (0, LANES - v.shape[1])))
    for l in range(depth):
        u = _in_proj(x2, _row(norm_mix[l]), _reorder_in_weight(w_in[l]))
        o_a = _hgrn(u, lb_raw.astype(F32), _row(a_gnorm[l]), l, bsz, seq)
        q, k, vt, qi, ki, wt = _dsa_prep(
            u, tabs, b_wqb[l].astype(BF16), b_wqidx[l].astype(BF16), _row(b_qlat_gain[l]),
            _row(b_qnorm[l]), _row(b_knorm[l]), pad_lanes(_row(b_kidx_norm[l])))
        o_b = _dsa_attn(q, k, vt, qi, ki, wt, bsz, seq)
        km, vm = _mem_kv(mem2, _row(norm_mem[l]), x_wk[l].astype(BF16), x_wv[l].astype(BF16),
                         _row(x_knorm[l]), n_mem)
        x2 = _xattn(x2, o_a, o_b, w_out[l].astype(BF16), _row(norm_x[l]), x_wq[l].astype(BF16),
                    _row(x_qnorm[l]), km, vm, x_wo[l].astype(BF16), seq, n_mem)
        wr = pad_lanes(jnp.concatenate([w_rg[l], w_re[l].reshape(d, N_EXPERTS)], axis=1).astype(F32))
        wr_hi = wr.astype(BF16)
        wr_lo = (wr - wr_hi.astype(F32)).astype(BF16)
        br = pad_lanes(jnp.concatenate([b_rg[l], b_re[l].reshape(-1)]).reshape(1, -1).astype(F32))
        x2 = _moe(x2, _row(norm_ffn[l]), wr_hi, wr_lo, br,
                  w_gu[l].reshape(N_EXPERTS, d, 2 * D_EXPERT).astype(BF16),
                  w_dn[l].reshape(N_EXPERTS, D_EXPERT, d).astype(BF16))
    return x2.reshape(bsz, seq, d)
```

```python
import functools

import numpy as np
import jax
import jax.numpy as jnp
from jax import lax
from jax.experimental import pallas as pl
from jax.experimental.pallas import tpu as pltpu

F32 = jnp.float32
BF16 = jnp.bfloat16
I32 = jnp.int32
I16 = jnp.int16

EPS = 1e-6
ROPE_THETA = 10000.0
CHUNK = 64
CHUNK_SHIFT = 6
SUB = 16
SUB_SHIFT = 4
N_HEADS = 4
HEAD_DIM = 128
WIDTH = N_HEADS * HEAD_DIM
Q_LORA = 256
IDX_HEADS = 16
IDX_DIM = 64
TOPK_MAX = 256
N_GROUPS = 4
EXP_PER_GROUP = 4
N_EXPERTS = N_GROUPS * EXP_PER_GROUP
D_EXPERT = 256
LANES = 128
INT_MIN = -2 ** 31
INT16_MIN = -2 ** 15
PACK = 16
LOG2E = 1.4426950408889634

U_AQ, U_AF, U_AI, U_AG = 0, 512, 1024, 1536
U_K, U_V, U_QLAT, U_MISC = 2048, 2560, 3072, 3328
U_COLS = 3456

VMEM_LIMIT = 56 * 1024 * 1024


def _rms(x, gain):
    return x * lax.rsqrt(jnp.mean(x * x, axis=-1, keepdims=True) + EPS) * gain


def _dot(a, b):
    return jnp.dot(a, b, preferred_element_type=F32)


def _dot_nt(a, b):
    return lax.dot_general(a, b, (((1,), (1,)), ((), ())), preferred_element_type=F32)


def _params(*sem):
    return pltpu.CompilerParams(dimension_semantics=sem, vmem_limit_bytes=VMEM_LIMIT)


def _in_proj_kernel(x_ref, g_ref, w_ref, u_ref, *, col_chunk):
    hb = _rms(x_ref[...], g_ref[...]).astype(BF16)
    for c0 in range(0, U_COLS, col_chunk):
        u_ref[:, c0:c0 + col_chunk] = _dot(hb, w_ref[:, c0:c0 + col_chunk])


def _in_proj(x2, gain, w_bf, tm=256):
    n, d = x2.shape
    return pl.pallas_call(
        functools.partial(_in_proj_kernel, col_chunk=1152),
        out_shape=jax.ShapeDtypeStruct((n, U_COLS), F32),
        grid=(n // tm,),
        in_specs=[pl.BlockSpec((tm, d), lambda i: (i, 0)),
                  pl.BlockSpec((1, d), lambda i: (0, 0)),
                  pl.BlockSpec((d, U_COLS), lambda i: (0, 0))],
        out_specs=pl.BlockSpec((tm, U_COLS), lambda i: (i, 0)),
        compiler_params=_params("parallel"),
        name="in_proj",
    )(x2, gain, w_bf)


def _split3(x):
    hi = x.astype(BF16)
    r1 = x - hi.astype(F32)
    mid = r1.astype(BF16)
    lo = (r1 - mid.astype(F32)).astype(BF16)
    return hi, mid, lo


def _hgrn_kernel(q_ref, f_ref, i_ref, g_ref, lbraw_ref, gn_ref, e_ref, o_ref, st_ref,
                 *, layer, n_chunks):
    @pl.when(pl.program_id(2) == 0)
    def _():
        st_ref[...] = jnp.zeros_like(st_ref)

    lr = lbraw_ref[...]
    ex = jnp.exp(lr - jnp.max(lr, axis=0, keepdims=True))
    sm = ex / jnp.sum(ex, axis=0, keepdims=True)
    lb = jnp.zeros((1, HEAD_DIM), F32)
    for r in range(1, layer + 1):
        lb = lb + sm[r:r + 1, :]
    log_lb = jnp.log(lb)
    log_1mlb = jnp.log(1.0 - lb)

    row = lax.broadcasted_iota(I32, (CHUNK, HEAD_DIM), 0)
    col = lax.broadcasted_iota(I32, (CHUNK, HEAD_DIM), 1)
    row_l = row & (SUB - 1)
    row_b = row >> SUB_SHIFT
    col_b = col >> SUB_SHIFT
    tri = (lax.broadcasted_iota(I32, (CHUNK, CHUNK), 0)
           >= lax.broadcasted_iota(I32, (CHUNK, CHUNK), 1)).astype(BF16)
    n_sub = CHUNK // SUB
    zeros_c = jnp.zeros((CHUNK, HEAD_DIM), F32)

    def sub_bcast(t, s):
        t4 = t.reshape(n_sub, SUB, HEAD_DIM)
        return jnp.broadcast_to(t4[:, s:s + 1, :], (n_sub, SUB, HEAD_DIM)).reshape(CHUNK, HEAD_DIM)

    def chunk(c, carry):
        rows = pl.ds(pl.multiple_of(c * CHUNK, CHUNK), CHUNK)
        qr = q_ref[rows, :]
        z = f_ref[rows, :]
        v = i_ref[rows, :]
        g = g_ref[rows, :]
        q = qr * jax.nn.sigmoid(qr)
        ls = jnp.minimum(z, 0.0) - jnp.log(1.0 + jnp.exp(-jnp.abs(z)))
        cc = log_1mlb + ls
        log_f = jnp.maximum(log_lb, cc) + jnp.log(1.0 + jnp.exp(-jnp.abs(log_lb - cc)))
        kg = (1.0 - lb) * jax.nn.sigmoid(-z)
        hi, mid, lo = _split3(log_f)
        b = _dot(tri, hi) + _dot(tri, mid) + _dot(tri, lo)
        b_last = b[CHUNK - 1:CHUNK, :]
        st = st_ref[...]
        o_inter = _dot_nt((q * jnp.exp(b)).astype(BF16), st.astype(BF16))
        kk = kg * jnp.exp(b_last - b)
        st_ref[...] = st * jnp.exp(b_last) + _dot(v.T.astype(BF16), kk.astype(BF16))

        pieces = []
        for s in range(SUB):
            d = jnp.where(row_l >= s, b - sub_bcast(b, s), -jnp.inf)
            pieces.append((q * sub_bcast(kg, s) * jnp.exp(d)).astype(BF16))
        a2 = _dot(jnp.concatenate(pieces, axis=1), e_ref[...])
        a = jnp.where(col_b == row_b, a2, 0.0)
        qt = (q * jnp.exp(b - sub_bcast(b, 0))).astype(BF16)
        kts = []
        for i in range(1, n_sub):
            r_i = b[i * SUB:i * SUB + 1, :]
            kts.append(kg * jnp.exp(jnp.where(row < i * SUB, r_i - b, -jnp.inf)))
            kts.append(zeros_c)
        r = _dot_nt(qt, jnp.concatenate(kts, axis=0).astype(BF16))
        for i in range(1, n_sub):
            a = a + jnp.where(row_b == i, r[:, (i - 1) * LANES:i * LANES], 0.0)
        o = o_inter + _dot(a[:, :CHUNK].astype(BF16), v.astype(BF16))
        o = _rms(o, gn_ref[...]) * (g * jax.nn.sigmoid(g))
        o_ref[rows, :] = o
        return carry

    lax.fori_loop(0, n_chunks, chunk, 0)


def _hgrn_selector():
    e = np.zeros((SUB * HEAD_DIM, LANES), np.float32)
    s_of_row = np.arange(SUB * HEAD_DIM) // HEAD_DIM
    cols = np.arange(LANES)
    e[:, :] = ((cols[None, :] % SUB) == s_of_row[:, None]) & (cols[None, :] < CHUNK)
    return jnp.asarray(e, BF16)


def _hgrn(u, lb_raw, gn, layer, bsz, seq, tb=512):
    n = bsz * seq
    nb = seq // tb
    blk = lambda k: pl.BlockSpec((tb, HEAD_DIM), lambda b, h, j, k=k: (b * nb + j, h + N_HEADS * k))
    n_layers = lb_raw.shape[0]
    return pl.pallas_call(
        functools.partial(_hgrn_kernel, layer=layer, n_chunks=tb // CHUNK),
        out_shape=jax.ShapeDtypeStruct((n, WIDTH), F32),
        grid=(bsz, N_HEADS, nb),
        in_specs=[blk(0), blk(1), blk(2), blk(3),
                  pl.BlockSpec((n_layers, HEAD_DIM), lambda b, h, j: (0, h)),
                  pl.BlockSpec((1, HEAD_DIM), lambda b, h, j: (0, 0)),
                  pl.BlockSpec((SUB * HEAD_DIM, LANES), lambda b, h, j: (0, 0))],
        out_specs=pl.BlockSpec((tb, HEAD_DIM), lambda b, h, j: (b * nb + j, h)),
        scratch_shapes=[pltpu.VMEM((HEAD_DIM, HEAD_DIM), F32)],
        compiler_params=_params("parallel", "parallel", "arbitrary"),
        name="hgrn",
    )(u, u, u, u, lb_raw, gn, _hgrn_selector())


def _rope128(x, cos, sin_signed):
    return x * cos + pltpu.roll(x, 64, 1) * sin_signed


def _rope64(x, cos, sin_a, sin_b):
    return x * cos + pltpu.roll(x, 96, 1) * sin_a + pltpu.roll(x, 32, 1) * sin_b


def _dsa_prep_kernel(k_ref, v_ref, ql_ref, misc_ref, cm_ref, sm_ref, ci_ref, sia_ref, sib_ref,
                     wqb_ref, wqi_ref, gql_ref, gqn_ref, gkn_ref, gki_ref,
                     q_out, k_out, vt_out, qi_out, ki_out, wt_out):
    cm, sm = cm_ref[...], sm_ref[...]
    ci, sia, sib = ci_ref[...], sia_ref[...], sib_ref[...]
    cb = _rms(ql_ref[...], gql_ref[...]).astype(BF16)
    qf = _dot(cb, wqb_ref[...])
    scale = HEAD_DIM ** -0.5
    for h in range(N_HEADS):
        sl = slice(h * HEAD_DIM, (h + 1) * HEAD_DIM)
        qh = _rope128(_rms(qf[:, sl], gqn_ref[...]), cm, sm)
        q_out[:, sl] = (qh * (scale * LOG2E)).astype(BF16)
        kh = _rope128(_rms(k_ref[:, sl], gkn_ref[...]), cm, sm)
        k_out[:, sl] = kh.astype(BF16)
    vt_out[0] = v_ref[...].T.astype(BF16)
    qi = _dot(cb, wqi_ref[...])
    for p in range(IDX_HEADS * IDX_DIM // LANES):
        sl = slice(p * LANES, (p + 1) * LANES)
        qi_out[:, sl] = _rope64(qi[:, sl], ci, sia, sib).astype(BF16)
    misc = misc_ref[...]
    lane = lax.broadcasted_iota(I32, misc.shape, 1)
    kraw = jnp.where(lane < IDX_DIM, misc, 0.0)
    kn = kraw * lax.rsqrt(jnp.sum(kraw * kraw, axis=-1, keepdims=True) * (1.0 / IDX_DIM) + EPS)
    kn = _rope64(kn * gki_ref[...], ci, sia, sib)
    ki_out[...] = (kn + pltpu.roll(kn, IDX_DIM, 1)).astype(BF16)
    wt = (misc * (IDX_HEADS ** -0.5 * IDX_DIM ** -0.5)).T
    wt_out[...] = wt[IDX_DIM:IDX_DIM + IDX_HEADS, :]


def _dsa_prep(u, tabs, wqb, wqi, gql, gqn, gkn, gki, tm=256):
    n = u.shape[0]
    nt = n // tm
    ublk = lambda width, col: pl.BlockSpec((tm, width), lambda i: (i, col // width))
    tab = pl.BlockSpec((tm, LANES), lambda i: (i, 0))
    full = lambda a: pl.BlockSpec(a.shape, lambda i: (0,) * a.ndim)
    return pl.pallas_call(
        _dsa_prep_kernel,
        out_shape=(jax.ShapeDtypeStruct((n, WIDTH), BF16),
                   jax.ShapeDtypeStruct((n, WIDTH), BF16),
                   jax.ShapeDtypeStruct((nt, WIDTH, tm), BF16),
                   jax.ShapeDtypeStruct((n, IDX_HEADS * IDX_DIM), BF16),
                   jax.ShapeDtypeStruct((n, LANES), BF16),
                   jax.ShapeDtypeStruct((IDX_HEADS, n), F32)),
        grid=(nt,),
        in_specs=[ublk(WIDTH, U_K), ublk(WIDTH, U_V), ublk(Q_LORA, U_QLAT), ublk(LANES, U_MISC),
                  tab, tab, tab, tab, tab,
                  full(wqb), full(wqi), full(gql), full(gqn), full(gkn), full(gki)],
        out_specs=(pl.BlockSpec((tm, WIDTH), lambda i: (i, 0)),
                   pl.BlockSpec((tm, WIDTH), lambda i: (i, 0)),
                   pl.BlockSpec((1, WIDTH, tm), lambda i: (i, 0, 0)),
                   pl.BlockSpec((tm, IDX_HEADS * IDX_DIM), lambda i: (i, 0)),
                   pl.BlockSpec((tm, LANES), lambda i: (i, 0)),
                   pl.BlockSpec((IDX_HEADS, tm), lambda i: (0, i))),
        compiler_params=_params("parallel"),
        name="dsa_prep",
    )(u, u, u, u, *tabs, wqb, wqi, gql, gqn, gkn, gki)


def _sortable(x):
    bits = lax.bitcast_convert_type(x, I32)
    return bits ^ ((bits >> 31) & 0x7FFFFFFF)


def _dsa_attn_kernel(qi_ref, wt_ref, q_ref, ki_ref, k_ref, vt_ref, o_ref,
                     qm_ref, hi_ref, lo_ref, kb_ref, bias_ref, acc0, acc1, acc2, acc3, *, tq, kt, top_k):
    acc_refs = (acc0, acc1, acc2, acc3)
    q0 = pl.program_id(1) * tq
    n_tiles = (q0 + tq + kt - 1) // kt

    lane_q = lax.broadcasted_iota(I32, (tq, LANES), 1)
    for h in range(IDX_HEADS):
        src = qi_ref[:, (h // 2) * LANES:(h // 2 + 1) * LANES].astype(F32)
        keep = (lane_q < IDX_DIM) if h % 2 == 0 else (lane_q >= IDX_DIM)
        qm_ref[h * tq:(h + 1) * tq, :] = jnp.where(keep, src, 0.0).astype(BF16)

    qpos = q0 + lax.broadcasted_iota(I32, (1, tq), 1)
    limit = ((qpos >> CHUNK_SHIFT) + 1) << CHUNK_SHIFT
    sub = 128
    key_iota = lax.broadcasted_iota(I32, (sub, tq), 0)

    def tile_rows(jt):
        return pl.ds(pl.multiple_of(jt * kt, kt), kt)

    def score_tile(jt, carry):
        for part in range(kt // sub):
            base = jt * kt + part * sub
            rows = pl.ds(pl.multiple_of(base, sub), sub)
            ki = ki_ref[rows, :]
            acc = jnp.zeros((sub, tq), F32)
            for h in range(IDX_HEADS):
                x = _dot_nt(ki, qm_ref[h * tq:(h + 1) * tq, :])
                acc = acc + wt_ref[h:h + 1, :] * jnp.maximum(x, 0.0)
            key = jnp.where(base + key_iota < limit, _sortable(acc), INT_MIN)
            hi_ref[rows, :] = (key >> 16).astype(I16)
            lo_ref[rows, :] = ((key & 0xFFFF) - 32768).astype(I16)
        return carry

    lax.fori_loop(0, n_tiles, score_tile, 0)

    one, zero = jnp.ones((), BF16), jnp.zeros((), BF16)
    neg_inf = jnp.full((), -jnp.inf, BF16)
    n_rows = (n_tiles * kt).astype(F32)

    def count16(ref, trial, strict):
        t16 = jnp.broadcast_to(trial.astype(I16), (PACK, tq))

        def body(jt, c):
            x = ref[tile_rows(jt), :]
            parts = [jnp.zeros((PACK, tq), BF16) for _ in range(4)]
            for r in range(kt // PACK):
                xr = x[r * PACK:(r + 1) * PACK, :]
                parts[r % 4] = parts[r % 4] + jnp.where((xr > t16) if strict else (xr >= t16), one, zero)
            return c + ((parts[0] + parts[1]) + (parts[2] + parts[3])).astype(F32)
        c = lax.fori_loop(0, n_tiles, body, jnp.zeros((PACK, tq), F32))
        return jnp.sum(c, axis=0, keepdims=True)

    def bisect16(ref, want):
        def step(it, carry):
            lo, n_lo = carry
            trial = lo + lax.shift_left(jnp.int32(1), 15 - it)
            n = count16(ref, trial, False)
            ok = n >= want
            return jnp.where(ok, trial, lo), jnp.where(ok, n, n_lo)
        return lax.fori_loop(0, 16, step, (jnp.full((1, tq), INT16_MIN, I32), jnp.full((1, tq), n_rows, F32)))

    tau_hi, n_hi_ge = bisect16(hi_ref, float(top_k))
    n_hi_gt = count16(hi_ref, tau_hi, True)
    want_lo = top_k - n_hi_gt
    tau_hi16 = jnp.broadcast_to(tau_hi.astype(I16), (kt, tq))

    def bucket_tile(jt, carry):
        rows = tile_rows(jt)
        kb_ref[rows, :] = jnp.where(hi_ref[rows, :] == tau_hi16, lo_ref[rows, :],
                                    jnp.full((), INT16_MIN, I16))
        return carry

    lax.fori_loop(0, n_tiles, bucket_tile, 0)
    tau_lo, n_kb_ge = bisect16(kb_ref, want_lo)
    live = (tau_hi > INT16_MIN) | (tau_lo > INT16_MIN)
    n_tie_ge = jnp.where(tau_lo == INT16_MIN, n_hi_ge - n_hi_gt, n_kb_ge)
    excess = jnp.max(jnp.where(live, n_tie_ge - want_lo, 0.0))

    @pl.when(excess <= 0)
    def _():
        tau_lo16 = jnp.broadcast_to(jnp.where(live, tau_lo, -INT16_MIN - 1).astype(I16), (kt, tq))

        def body(jt, carry):
            rows = tile_rows(jt)
            h16 = hi_ref[rows, :]
            in_bucket = jnp.where(lo_ref[rows, :] >= tau_lo16, zero, neg_inf)
            bias_ref[rows, :] = jnp.where(h16 > tau_hi16, zero, jnp.where(h16 == tau_hi16, in_bucket, neg_inf))
            return carry
        lax.fori_loop(0, n_tiles, body, 0)

    @pl.when(excess > 0)
    def _():
        need = want_lo - count16(kb_ref, tau_lo, True)
        live_f = jnp.where(live, 1.0, 0.0)
        tri = (lax.broadcasted_iota(I32, (kt, kt), 0)
               >= lax.broadcasted_iota(I32, (kt, kt), 1)).astype(BF16)

        def body(jt, seen):
            rows = tile_rows(jt)
            h32 = hi_ref[rows, :].astype(I32)
            l32 = lo_ref[rows, :].astype(I32)
            same_hi = h32 == tau_hi
            gt = jnp.where(h32 > tau_hi, 1.0, jnp.where(same_hi, jnp.where(l32 > tau_lo, 1.0, 0.0), 0.0))
            eq = jnp.where(same_hi, jnp.where(l32 == tau_lo, 1.0, 0.0), 0.0)
            rank = _dot(tri, eq.astype(BF16)) + seen
            sel = gt + eq * jnp.where(rank <= need, live_f, 0.0)
            bias_ref[rows, :] = jnp.where(sel > 0.0, 0.0, -jnp.inf).astype(BF16)
            return rank[kt - 1:kt, :]
        lax.fori_loop(0, n_tiles, body, jnp.zeros((1, tq), F32))

    for a in acc_refs:
        a[...] = jnp.zeros_like(a)
    heads = [slice(h * HEAD_DIM, (h + 1) * HEAD_DIM) for h in range(N_HEADS)]

    def attn_tile(jt, carry):
        ms, ls = carry
        rows = tile_rows(jt)
        bias = bias_ref[rows, :].astype(F32)
        lts = [_dot_nt(k_ref[rows, sl], q_ref[:, sl]) + bias for sl in heads]
        new_ms, new_ls, alphas, ps = [], [], [], []
        for h in range(N_HEADS):
            m_new = jnp.maximum(ms[h], jnp.max(lts[h], axis=0, keepdims=True))
            m_safe = jnp.where(m_new == -jnp.inf, 0.0, m_new)
            p = jnp.exp2(lts[h] - m_safe)
            alpha = jnp.exp2(ms[h] - m_safe)
            new_ls.append(alpha * ls[h] + jnp.sum(p, axis=0, keepdims=True))
            new_ms.append(m_new)
            alphas.append(alpha)
            ps.append(p.astype(BF16))
        for h in range(N_HEADS):
            acc_refs[h][...] = acc_refs[h][...] * alphas[h] + _dot(vt_ref[jt, heads[h], :], ps[h])
        return tuple(new_ms), tuple(new_ls)

    init = (tuple(jnp.full((1, tq), -jnp.inf, F32) for _ in range(N_HEADS)),
            tuple(jnp.zeros((1, tq), F32) for _ in range(N_HEADS)))
    _, ls = lax.fori_loop(0, n_tiles, attn_tile, init)
    for h in range(N_HEADS):
        out_t = acc_refs[h][...] * (1.0 / ls[h])
        o_ref[:, heads[h]] = out_t.T


def _dsa_attn(q, k, vt, qi, ki, wt, bsz, seq, tq=256, kt=256):
    n = bsz * seq
    nq = seq // tq
    top_k = min(TOPK_MAX, seq // 4)
    once = dict(pipeline_mode=pl.Buffered(1))
    strip16 = pltpu.VMEM((seq, tq), I16)
    return pl.pallas_call(
        functools.partial(_dsa_attn_kernel, tq=tq, kt=kt, top_k=top_k),
        out_shape=jax.ShapeDtypeStruct((n, WIDTH), F32),
        grid=(bsz, nq),
        in_specs=[pl.BlockSpec((tq, IDX_HEADS * IDX_DIM), lambda b, i: (b * nq + i, 0)),
                  pl.BlockSpec((IDX_HEADS, tq), lambda b, i: (0, b * nq + i)),
                  pl.BlockSpec((tq, WIDTH), lambda b, i: (b * nq + i, 0)),
                  pl.BlockSpec((seq, LANES), lambda b, i: (b, 0), **once),
                  pl.BlockSpec((seq, WIDTH), lambda b, i: (b, 0), **once),
                  pl.BlockSpec((seq // kt, WIDTH, kt), lambda b, i: (b, 0, 0), **once)],
        out_specs=pl.BlockSpec((tq, WIDTH), lambda b, i: (b * nq + i, 0)),
        scratch_shapes=[pltpu.VMEM((IDX_HEADS * tq, LANES), BF16), strip16, strip16, strip16,
                        pltpu.VMEM((seq, tq), BF16)]
                       + [pltpu.VMEM((HEAD_DIM, tq), F32) for _ in range(N_HEADS)],
        compiler_params=_params("parallel", "arbitrary"),
        name="dsa_attn",
    )(qi, wt, q, ki, k, vt)


def _mem_kv_kernel(mem_ref, g_ref, wk_ref, wv_ref, gk_ref, k_out, v_out):
    mb = _rms(mem_ref[...], g_ref[...]).astype(BF16)
    kf = _dot(mb, wk_ref[...])
    for h in range(N_HEADS):
        sl = slice(h * HEAD_DIM, (h + 1) * HEAD_DIM)
        k_out[:, sl] = _rms(kf[:, sl], gk_ref[...]).astype(BF16)
    v_out[...] = _dot(mb, wv_ref[...]).astype(BF16)


def _mem_kv(mem2, gain, wk, wv, gk, n_mem):
    n, d = mem2.shape
    full = lambda a: pl.BlockSpec(a.shape, lambda b: (0,) * a.ndim)
    return pl.pallas_call(
        _mem_kv_kernel,
        out_shape=(jax.ShapeDtypeStruct((n, WIDTH), BF16), jax.ShapeDtypeStruct((n, WIDTH), BF16)),
        grid=(n // n_mem,),
        in_specs=[pl.BlockSpec((n_mem, d), lambda b: (b, 0)), full(gain), full(wk), full(wv), full(gk)],
        out_specs=(pl.BlockSpec((n_mem, WIDTH), lambda b: (b, 0)),
                   pl.BlockSpec((n_mem, WIDTH), lambda b: (b, 0))),
        compiler_params=_params("parallel"),
        name="mem_kv",
    )(mem2, gain, wk, wv, gk)


def _xattn_kernel(x_ref, oa_ref, ob_ref, wout_ref, g_ref, wq_ref, gq_ref, km_ref, vm_ref, wo_ref, o_ref):
    x1 = (x_ref[...] + _dot(oa_ref[...].astype(BF16), wout_ref[:WIDTH, :])
          + _dot(ob_ref[...].astype(BF16), wout_ref[WIDTH:, :]))
    hb = _rms(x1, g_ref[...]).astype(BF16)
    qf = _dot(hb, wq_ref[...])
    scale = HEAD_DIM ** -0.5
    outs = []
    for h in range(N_HEADS):
        sl = slice(h * HEAD_DIM, (h + 1) * HEAD_DIM)
        qh = (_rms(qf[:, sl], gq_ref[...]) * scale).astype(BF16)
        logits = _dot_nt(qh, km_ref[:, sl])
        p = jnp.exp(logits - jnp.max(logits, axis=-1, keepdims=True))
        oh = _dot(p.astype(BF16), vm_ref[:, sl])
        outs.append((oh * (1.0 / jnp.sum(p, axis=-1, keepdims=True))).astype(BF16))
    o_ref[...] = x1 + _dot(jnp.concatenate(outs, axis=1), wo_ref[...])


def _xattn(x2, oa, ob, wout, gain, wq, gq, km, vm, wo, seq, n_mem, tm=512):
    n, d = x2.shape
    per_b = seq // tm
    row = lambda w: pl.BlockSpec((tm, w), lambda i: (i, 0))
    full = lambda a: pl.BlockSpec(a.shape, lambda i: (0,) * a.ndim)
    memb = pl.BlockSpec((n_mem, WIDTH), lambda i: (i // per_b, 0))
    return pl.pallas_call(
        _xattn_kernel,
        out_shape=jax.ShapeDtypeStruct((n, d), F32),
        grid=(n // tm,),
        in_specs=[row(d), row(WIDTH), row(WIDTH), full(wout), full(gain), full(wq), full(gq),
                  memb, memb, full(wo)],
        out_specs=row(d),
        compiler_params=_params("parallel"),
        name="xattn",
    )(x2, oa, ob, wout, gain, wq, gq, km, vm, wo)


def _moe_kernel(x_ref, g_ref, wr_hi_ref, wr_lo_ref, br_ref, wgu_ref, wdn_ref, o_ref, hb_ref, gate_ref):
    e = pl.program_id(1)
    lane = lax.broadcasted_iota(I32, gate_ref.shape, 1)
    lane_f = lane.astype(F32)

    @pl.when(e == 0)
    def _():
        x = x_ref[...]
        h = _rms(x, g_ref[...])
        h_hi = h.astype(BF16)
        h_lo = (h - h_hi.astype(F32)).astype(BF16)
        hb_ref[...] = h_hi
        lg = (_dot(h_hi, wr_hi_ref[...]) + _dot(h_lo, wr_hi_ref[...]) + _dot(h_hi, wr_lo_ref[...])
              + br_ref[...])
        first = lambda cond: jnp.min(jnp.where(cond, lane_f, 1e9), axis=-1, keepdims=True)
        gl = jnp.where(lane < N_GROUPS, lg, -jnp.inf)
        gmax = jnp.max(gl, axis=-1, keepdims=True)
        gsel = first(gl == gmax)
        g_w = 1.0 / jnp.sum(jnp.exp(gl - gmax), axis=-1, keepdims=True)
        grp_of_lane = ((lane - N_GROUPS) >> 2).astype(F32)
        in_grp = (lane >= N_GROUPS) & (lane < N_GROUPS + N_EXPERTS) & (grp_of_lane == gsel)
        el = jnp.where(in_grp, lg, -jnp.inf)
        v1 = jnp.max(el, axis=-1, keepdims=True)
        i1 = first(el == v1)
        el2 = jnp.where(lane_f == i1, -jnp.inf, el)
        v2 = jnp.max(el2, axis=-1, keepdims=True)
        i2 = first(el2 == v2)
        r = jnp.exp(v2 - v1)
        w1 = 1.0 / (1.0 + r)
        gate_ref[...] = jnp.where(lane_f == i1, w1 * g_w, jnp.where(lane_f == i2, r * w1 * g_w, 0.0))
        o_ref[...] = x

    ge = jnp.sum(jnp.where(lane == e + N_GROUPS, gate_ref[...], 0.0), axis=-1, keepdims=True)
    gu = _dot(hb_ref[...], wgu_ref[0])
    up = gu[:, :D_EXPERT]
    act = up * jax.nn.sigmoid(up) * gu[:, D_EXPERT:] * ge
    o_ref[...] += _dot(act.astype(BF16), wdn_ref[0])


def _moe(x2, gain, wr_hi, wr_lo, br, wgu, wdn, tm=1024):
    n, d = x2.shape
    full = lambda a: pl.BlockSpec(a.shape, lambda i, e: (0,) * a.ndim)
    return pl.pallas_call(
        _moe_kernel,
        out_shape=jax.ShapeDtypeStruct((n, d), F32),
        grid=(n // tm, N_EXPERTS),
        in_specs=[pl.BlockSpec((tm, d), lambda i, e: (i, 0)), full(gain), full(wr_hi), full(wr_lo), full(br),
                  pl.BlockSpec((1, d, 2 * D_EXPERT), lambda i, e: (e, 0, 0)),
                  pl.BlockSpec((1, D_EXPERT, d), lambda i, e: (e, 0, 0))],
        out_specs=pl.BlockSpec((tm, d), lambda i, e: (i, 0)),
        scratch_shapes=[pltpu.VMEM((tm, d), BF16), pltpu.VMEM((tm, LANES), F32)],
        compiler_params=_params("parallel", "arbitrary"),
        name="moe",
    )(x2, gain, wr_hi, wr_lo, br, wgu, wdn)


def _rope_tables(positions):
    pos = positions.reshape(-1).astype(F32)[:, None]

    def cs(dim):
        inv = ROPE_THETA ** (-jnp.arange(0, dim, 2, dtype=F32) / dim)
        ang = pos * inv
        return jnp.cos(ang), jnp.sin(ang)

    c, s = cs(HEAD_DIM)
    ci, si = cs(IDX_DIM)
    z = jnp.zeros_like(si)
    return (jnp.concatenate([c, c], 1), jnp.concatenate([-s, s], 1),
            jnp.concatenate([ci, ci, ci, ci], 1), jnp.concatenate([-si, z, -si, z], 1),
            jnp.concatenate([z, si, z, si], 1))


def _reorder_in_weight(w):
    a = w[:, :4 * WIDTH]
    o = 4 * WIDTH
    qlat = w[:, o:o + Q_LORA]; o += Q_LORA
    k = w[:, o:o + WIDTH]; o += WIDTH
    v = w[:, o:o + WIDTH]; o += WIDTH
    misc = w[:, o:]
    pad = jnp.zeros((w.shape[0], LANES - misc.shape[1]), w.dtype)
    return jnp.concatenate([a, k, v, qlat, misc, pad], axis=1).astype(BF16)


def _row(v):
    return v.reshape(1, -1).astype(F32)


def kernel(x, mem, positions, norm_mix, w_in, lb_raw, a_gnorm, b_qlat_gain, b_wqb, b_wqidx, b_qnorm,
           b_knorm, b_kidx_norm, w_out, norm_x, norm_mem, x_wq, x_wk, x_wv, x_wo, x_qnorm, x_knorm,
           norm_ffn, w_rg, b_rg, w_re, b_re, w_gu, w_dn):
    bsz, seq, d = x.shape
    n_mem = mem.shape[1]
    depth = w_in.shape[0]
    x2 = x.reshape(bsz * seq, d)
    mem2 = mem.reshape(bsz * n_mem, d)
    tabs = _rope_tables(positions)
    pad_lanes = lambda v: jnp.pad(v, ((0, 0), (0, LANES - v.shape[1])))
    for l in range(depth):
        u = _in_proj(x2, _row(norm_mix[l]), _reorder_in_weight(w_in[l]))
        o_a = _hgrn(u, lb_raw.astype(F32), _row(a_gnorm[l]), l, bsz, seq)
        q, k, vt, qi, ki, wt = _dsa_prep(
            u, tabs, b_wqb[l].astype(BF16), b_wqidx[l].astype(BF16), _row(b_qlat_gain[l]),
            _row(b_qnorm[l]), _row(b_knorm[l]), pad_lanes(_row(b_kidx_norm[l])))
        o_b = _dsa_attn(q, k, vt, qi, ki, wt, bsz, seq)
        km, vm = _mem_kv(mem2, _row(norm_mem[l]), x_wk[l].astype(BF16), x_wv[l].astype(BF16),
                         _row(x_knorm[l]), n_mem)
        x2 = _xattn(x2, o_a, o_b, w_out[l].astype(BF16), _row(norm_x[l]), x_wq[l].astype(BF16),
                    _row(x_qnorm[l]), km, vm, x_wo[l].astype(BF16), seq, n_mem)
        wr = pad_lanes(jnp.concatenate([w_rg[l], w_re[l].reshape(d, N_EXPERTS)], axis=1).astype(F32))
        wr_hi = wr.astype(BF16)
        wr_lo = (wr - wr_hi.astype(F32)).astype(BF16)
        br = pad_lanes(jnp.concatenate([b_rg[l], b_re[l].reshape(-1)]).reshape(1, -1).astype(F32))
        x2 = _moe(x2, _row(norm_ffn[l]), wr_hi, wr_lo, br,
                  w_gu[l].reshape(N_EXPERTS, d, 2 * D_EXPERT).astype(BF16),
                  w_dn[l].reshape(N_EXPERTS, D_EXPERT, d).astype(BF16))
    return x2.reshape(bsz, seq, d)
```

```python
import functools

import numpy as np
import jax
import jax.numpy as jnp
from jax import lax
from jax.experimental import pallas as pl
from jax.experimental.pallas import tpu as pltpu

F32 = jnp.float32
BF16 = jnp.bfloat16
I32 = jnp.int32
I16 = jnp.int16

EPS = 1e-6
ROPE_THETA = 10000.0
CHUNK = 64
CHUNK_SHIFT = 6
SUB = 16
SUB_SHIFT = 4
N_HEADS = 4
HEAD_DIM = 128
WIDTH = N_HEADS * HEAD_DIM
Q_LORA = 256
IDX_HEADS = 16
IDX_DIM = 64
TOPK_MAX = 256
N_GROUPS = 4
EXP_PER_GROUP = 4
N_EXPERTS = N_GROUPS * EXP_PER_GROUP
D_EXPERT = 256
LANES = 128
INT_MIN = -2 ** 31
INT16_MIN = -2 ** 15
PACK = 16
LOG2E = 1.4426950408889634

U_AQ, U_AF, U_AI, U_AG = 0, 512, 1024, 1536
U_K, U_V, U_QLAT, U_MISC = 2048, 2560, 3072, 3328
U_COLS = 3456

VMEM_LIMIT = 56 * 1024 * 1024


def _rms(x, gain):
    return x * lax.rsqrt(jnp.mean(x * x, axis=-1, keepdims=True) + EPS) * gain


def _dot(a, b):
    return jnp.dot(a, b, preferred_element_type=F32)


def _dot_nt(a, b):
    return lax.dot_general(a, b, (((1,), (1,)), ((), ())), preferred_element_type=F32)


def _params(*sem):
    return pltpu.CompilerParams(dimension_semantics=sem, vmem_limit_bytes=VMEM_LIMIT)


def _in_proj_kernel(x_ref, g_ref, w_ref, u_ref, *, col_chunk):
    hb = _rms(x_ref[...], g_ref[...]).astype(BF16)
    for c0 in range(0, U_COLS, col_chunk):
        u_ref[:, c0:c0 + col_chunk] = _dot(hb, w_ref[:, c0:c0 + col_chunk])


def _in_proj(x2, gain, w_bf, tm=256):
    n, d = x2.shape
    return pl.pallas_call(
        functools.partial(_in_proj_kernel, col_chunk=1152),
        out_shape=jax.ShapeDtypeStruct((n, U_COLS), F32),
        grid=(n // tm,),
        in_specs=[pl.BlockSpec((tm, d), lambda i: (i, 0)),
                  pl.BlockSpec((1, d), lambda i: (0, 0)),
                  pl.BlockSpec((d, U_COLS), lambda i: (0, 0))],
        out_specs=pl.BlockSpec((tm, U_COLS), lambda i: (i, 0)),
        compiler_params=_params("parallel"),
        name="in_proj",
    )(x2, gain, w_bf)


def _split3(x):
    hi = x.astype(BF16)
    r1 = x - hi.astype(F32)
    mid = r1.astype(BF16)
    lo = (r1 - mid.astype(F32)).astype(BF16)
    return hi, mid, lo


def _hgrn_kernel(q_ref, f_ref, i_ref, g_ref, lbraw_ref, gn_ref, e_ref, o_ref, st_ref,
                 *, layer, n_chunks, unroll):
    @pl.when(pl.program_id(2) == 0)
    def _():
        st_ref[...] = jnp.zeros_like(st_ref)

    lr = lbraw_ref[...]
    ex = jnp.exp(lr - jnp.max(lr, axis=0, keepdims=True))
    sm = ex / jnp.sum(ex, axis=0, keepdims=True)
    lb = jnp.zeros((1, HEAD_DIM), F32)
    for r in range(1, layer + 1):
        lb = lb + sm[r:r + 1, :]
    log_lb = jnp.log(lb)
    log_1mlb = jnp.log(1.0 - lb)

    row = lax.broadcasted_iota(I32, (CHUNK, HEAD_DIM), 0)
    col = lax.broadcasted_iota(I32, (CHUNK, HEAD_DIM), 1)
    row_l = row & (SUB - 1)
    row_b = row >> SUB_SHIFT
    col_b = col >> SUB_SHIFT
    tri = (lax.broadcasted_iota(I32, (CHUNK, CHUNK), 0)
           >= lax.broadcasted_iota(I32, (CHUNK, CHUNK), 1)).astype(BF16)
    n_sub = CHUNK // SUB
    zeros_c = jnp.zeros((CHUNK, HEAD_DIM), F32)

    def sub_bcast(t, s):
        t4 = t.reshape(n_sub, SUB, HEAD_DIM)
        return jnp.broadcast_to(t4[:, s:s + 1, :], (n_sub, SUB, HEAD_DIM)).reshape(CHUNK, HEAD_DIM)

    def gates(z):
        ls = jnp.minimum(z, 0.0) - jnp.log(1.0 + jnp.exp(-jnp.abs(z)))
        if layer == 0:
            return ls, (ls - z) * LOG2E
        cc = log_1mlb + ls
        log_f = jnp.maximum(log_lb, cc) + jnp.log(1.0 + jnp.exp(-jnp.abs(log_lb - cc)))
        return log_f, (cc - z) * LOG2E

    def diag_terms(q, b2, c2):
        pieces = []
        for s in range(SUB):
            d = jnp.where(row_l >= s, b2 - sub_bcast(c2, s), -jnp.inf)
            pieces.append((q * jnp.exp2(d)).astype(BF16))
        return jnp.concatenate(pieces, axis=1)

    def below_keys(b2, c2):
        kts = []
        for i in range(1, n_sub):
            r_i = b2[i * SUB:i * SUB + 1, :]
            kts.append(jnp.exp2(jnp.where(row < i * SUB, r_i - c2, -jnp.inf)))
            kts.append(zeros_c)
        return jnp.concatenate(kts, axis=0).astype(BF16)

    def below_scores(r):
        a = jnp.zeros((CHUNK, HEAD_DIM), F32)
        for i in range(1, n_sub):
            a = a + jnp.where(row_b == i, r[:, (i - 1) * LANES:i * LANES], 0.0)
        return a

    def chunks(it, carry):
        cs = range(unroll)
        rows = [pl.ds(pl.multiple_of((it * unroll + c) * CHUNK, CHUNK), CHUNK) for c in cs]
        qr = [q_ref[r, :] for r in rows]
        v = [i_ref[r, :] for r in rows]
        q = [x * jax.nn.sigmoid(x) for x in qr]
        gt = [gates(f_ref[r, :]) for r in rows]
        b2 = []
        for c in cs:
            hi, mid, lo = _split3(gt[c][0])
            b2.append((_dot(tri, hi) + _dot(tri, mid) + _dot(tri, lo)) * LOG2E)
        c2 = [b2[c] - gt[c][1] for c in cs]
        b_last = [x[CHUNK - 1:CHUNK, :] for x in b2]
        a2 = [_dot(diag_terms(q[c], b2[c], c2[c]), e_ref[...]) for c in cs]
        rr = [_dot_nt((q[c] * jnp.exp2(b2[c] - sub_bcast(b2[c], 0))).astype(BF16), below_keys(b2[c], c2[c]))
              for c in cs]
        upd = [_dot(v[c].T.astype(BF16), jnp.exp2(b_last[c] - c2[c]).astype(BF16)) for c in cs]
        qe = [(q[c] * jnp.exp2(b2[c])).astype(BF16) for c in cs]
        st = st_ref[...]
        o_inter = []
        for c in cs:
            o_inter.append(_dot_nt(qe[c], st.astype(BF16)))
            st = st * jnp.exp2(b_last[c]) + upd[c]
        st_ref[...] = st
        a = [jnp.where(col_b == row_b, a2[c], 0.0) + below_scores(rr[c]) for c in cs]
        o_intra = [_dot(a[c][:, :CHUNK].astype(BF16), v[c].astype(BF16)) for c in cs]
        for c in cs:
            g = g_ref[rows[c], :]
            o_ref[rows[c], :] = _rms(o_inter[c] + o_intra[c], gn_ref[...]) * (g * jax.nn.sigmoid(g))
        return carry

    lax.fori_loop(0, n_chunks // unroll, chunks, 0)


def _hgrn_selector():
    e = np.zeros((SUB * HEAD_DIM, LANES), np.float32)
    s_of_row = np.arange(SUB * HEAD_DIM) // HEAD_DIM
    cols = np.arange(LANES)
    e[:, :] = ((cols[None, :] % SUB) == s_of_row[:, None]) & (cols[None, :] < CHUNK)
    return jnp.asarray(e, BF16)


def _hgrn(u, lb_raw, gn, layer, bsz, seq, tb=512):
    n = bsz * seq
    nb = seq // tb
    blk = lambda k: pl.BlockSpec((tb, HEAD_DIM), lambda b, h, j, k=k: (b * nb + j, h + N_HEADS * k))
    n_layers = lb_raw.shape[0]
    return pl.pallas_call(
        functools.partial(_hgrn_kernel, layer=layer, n_chunks=tb // CHUNK, unroll=4),
        out_shape=jax.ShapeDtypeStruct((n, WIDTH), F32),
        grid=(bsz, N_HEADS, nb),
        in_specs=[blk(0), blk(1), blk(2), blk(3),
                  pl.BlockSpec((n_layers, HEAD_DIM), lambda b, h, j: (0, h)),
                  pl.BlockSpec((1, HEAD_DIM), lambda b, h, j: (0, 0)),
                  pl.BlockSpec((SUB * HEAD_DIM, LANES), lambda b, h, j: (0, 0))],
        out_specs=pl.BlockSpec((tb, HEAD_DIM), lambda b, h, j: (b * nb + j, h)),
        scratch_shapes=[pltpu.VMEM((HEAD_DIM, HEAD_DIM), F32)],
        compiler_params=_params("parallel", "parallel", "arbitrary"),
        name="hgrn",
    )(u, u, u, u, lb_raw, gn, _hgrn_selector())


def _rope128(x, cos, sin_signed):
    return x * cos + pltpu.roll(x, 64, 1) * sin_signed


def _rope64(x, cos, sin_a, sin_b):
    return x * cos + pltpu.roll(x, 96, 1) * sin_a + pltpu.roll(x, 32, 1) * sin_b


def _dsa_prep_kernel(k_ref, v_ref, ql_ref, misc_ref, cm_ref, sm_ref, ci_ref, sia_ref, sib_ref,
                     wqb_ref, wqi_ref, gql_ref, gqn_ref, gkn_ref, gki_ref,
                     q_out, k_out, vt_out, qi_out, ki_out, wt_out):
    cm, sm = cm_ref[...], sm_ref[...]
    ci, sia, sib = ci_ref[...], sia_ref[...], sib_ref[...]
    cb = _rms(ql_ref[...], gql_ref[...]).astype(BF16)
    qf = _dot(cb, wqb_ref[...])
    scale = HEAD_DIM ** -0.5
    for h in range(N_HEADS):
        sl = slice(h * HEAD_DIM, (h + 1) * HEAD_DIM)
        qh = _rope128(_rms(qf[:, sl], gqn_ref[...]), cm, sm)
        q_out[:, sl] = (qh * (scale * LOG2E)).astype(BF16)
        kh = _rope128(_rms(k_ref[:, sl], gkn_ref[...]), cm, sm)
        k_out[:, sl] = kh.astype(BF16)
    vt_out[0] = v_ref[...].T.astype(BF16)
    qi = _dot(cb, wqi_ref[...])
    for p in range(IDX_HEADS * IDX_DIM // LANES):
        sl = slice(p * LANES, (p + 1) * LANES)
        qi_out[:, sl] = _rope64(qi[:, sl], ci, sia, sib).astype(BF16)
    misc = misc_ref[...]
    lane = lax.broadcasted_iota(I32, misc.shape, 1)
    kraw = jnp.where(lane < IDX_DIM, misc, 0.0)
    kn = kraw * lax.rsqrt(jnp.sum(kraw * kraw, axis=-1, keepdims=True) * (1.0 / IDX_DIM) + EPS)
    kn = _rope64(kn * gki_ref[...], ci, sia, sib)
    ki_out[...] = (kn + pltpu.roll(kn, IDX_DIM, 1)).astype(BF16)
    wt = (misc * (IDX_HEADS ** -0.5 * IDX_DIM ** -0.5)).T
    wt_out[...] = wt[IDX_DIM:IDX_DIM + IDX_HEADS, :]


def _dsa_prep(u, tabs, wqb, wqi, gql, gqn, gkn, gki, tm=256):
    n = u.shape[0]
    nt = n // tm
    ublk = lambda width, col: pl.BlockSpec((tm, width), lambda i: (i, col // width))
    tab = pl.BlockSpec((tm, LANES), lambda i: (i, 0))
    full = lambda a: pl.BlockSpec(a.shape, lambda i: (0,) * a.ndim)
    return pl.pallas_call(
        _dsa_prep_kernel,
        out_shape=(jax.ShapeDtypeStruct((n, WIDTH), BF16),
                   jax.ShapeDtypeStruct((n, WIDTH), BF16),
                   jax.ShapeDtypeStruct((nt, WIDTH, tm), BF16),
                   jax.ShapeDtypeStruct((n, IDX_HEADS * IDX_DIM), BF16),
                   jax.ShapeDtypeStruct((n, LANES), BF16),
                   jax.ShapeDtypeStruct((IDX_HEADS, n), F32)),
        grid=(nt,),
        in_specs=[ublk(WIDTH, U_K), ublk(WIDTH, U_V), ublk(Q_LORA, U_QLAT), ublk(LANES, U_MISC),
                  tab, tab, tab, tab, tab,
                  full(wqb), full(wqi), full(gql), full(gqn), full(gkn), full(gki)],
        out_specs=(pl.BlockSpec((tm, WIDTH), lambda i: (i, 0)),
                   pl.BlockSpec((tm, WIDTH), lambda i: (i, 0)),
                   pl.BlockSpec((1, WIDTH, tm), lambda i: (i, 0, 0)),
                   pl.BlockSpec((tm, IDX_HEADS * IDX_DIM), lambda i: (i, 0)),
                   pl.BlockSpec((tm, LANES), lambda i: (i, 0)),
                   pl.BlockSpec((IDX_HEADS, tm), lambda i: (0, i))),
        compiler_params=_params("parallel"),
        name="dsa_prep",
    )(u, u, u, u, *tabs, wqb, wqi, gql, gqn, gkn, gki)


def _sortable(x):
    bits = lax.bitcast_convert_type(x, I32)
    return bits ^ ((bits >> 31) & 0x7FFFFFFF)


def _dsa_attn_kernel(qi_ref, wt_ref, q_ref, ki_ref, k_ref, vt_ref, o_ref,
                     qm_ref, hi_ref, lo_ref, kb_ref, bias_ref, acc0, acc1, acc2, acc3, *, tq, kt, top_k):
    acc_refs = (acc0, acc1, acc2, acc3)
    q0 = pl.program_id(1) * tq
    n_tiles = (q0 + tq + kt - 1) // kt

    lane_q = lax.broadcasted_iota(I32, (tq, LANES), 1)
    for h in range(IDX_HEADS):
        src = qi_ref[:, (h // 2) * LANES:(h // 2 + 1) * LANES].astype(F32)
        keep = (lane_q < IDX_DIM) if h % 2 == 0 else (lane_q >= IDX_DIM)
        qm_ref[h * tq:(h + 1) * tq, :] = jnp.where(keep, src, 0.0).astype(BF16)

    qpos = q0 + lax.broadcasted_iota(I32, (1, tq), 1)
    limit = ((qpos >> CHUNK_SHIFT) + 1) << CHUNK_SHIFT
    sub = 128
    key_iota = lax.broadcasted_iota(I32, (sub, tq), 0)

    def tile_rows(jt):
        return pl.ds(pl.multiple_of(jt * kt, kt), kt)

    def score_tile(jt, carry):
        for part in range(kt // sub):
            base = jt * kt + part * sub
            rows = pl.ds(pl.multiple_of(base, sub), sub)
            ki = ki_ref[rows, :]
            acc = jnp.zeros((sub, tq), F32)
            for h in range(IDX_HEADS):
                x = _dot_nt(ki, qm_ref[h * tq:(h + 1) * tq, :])
                acc = acc + wt_ref[h:h + 1, :] * jnp.maximum(x, 0.0)
            key = jnp.where(base + key_iota < limit, _sortable(acc), INT_MIN)
            hi_ref[rows, :] = (key >> 16).astype(I16)
            lo_ref[rows, :] = ((key & 0xFFFF) - 32768).astype(I16)
        return carry

    lax.fori_loop(0, n_tiles, score_tile, 0)

    one, zero = jnp.ones((), BF16), jnp.zeros((), BF16)
    neg_inf = jnp.full((), -jnp.inf, BF16)
    n_rows = (n_tiles * kt).astype(F32)

    def count16(ref, trial, strict):
        t16 = jnp.broadcast_to(trial.astype(I16), (PACK, tq))

        def body(jt, c):
            x = ref[tile_rows(jt), :]
            parts = [jnp.zeros((PACK, tq), BF16) for _ in range(4)]
            for r in range(kt // PACK):
                xr = x[r * PACK:(r + 1) * PACK, :]
                parts[r % 4] = parts[r % 4] + jnp.where((xr > t16) if strict else (xr >= t16), one, zero)
            return c + ((parts[0] + parts[1]) + (parts[2] + parts[3])).astype(F32)
        c = lax.fori_loop(0, n_tiles, body, jnp.zeros((PACK, tq), F32))
        return jnp.sum(c, axis=0, keepdims=True)

    def bisect16(ref, want):
        def step(it, carry):
            lo, n_lo = carry
            trial = lo + lax.shift_left(jnp.int32(1), 15 - it)
            n = count16(ref, trial, False)
            ok = n >= want
            return jnp.where(ok, trial, lo), jnp.where(ok, n, n_lo)
        return lax.fori_loop(0, 16, step, (jnp.full((1, tq), INT16_MIN, I32), jnp.full((1, tq), n_rows, F32)))

    tau_hi, n_hi_ge = bisect16(hi_ref, float(top_k))
    n_hi_gt = count16(hi_ref, tau_hi, True)
    want_lo = top_k - n_hi_gt
    tau_hi16 = jnp.broadcast_to(tau_hi.astype(I16), (kt, tq))

    def bucket_tile(jt, carry):
        rows = tile_rows(jt)
        kb_ref[rows, :] = jnp.where(hi_ref[rows, :] == tau_hi16, lo_ref[rows, :],
                                    jnp.full((), INT16_MIN, I16))
        return carry

    lax.fori_loop(0, n_tiles, bucket_tile, 0)
    tau_lo, n_kb_ge = bisect16(kb_ref, want_lo)
    live = (tau_hi > INT16_MIN) | (tau_lo > INT16_MIN)
    n_tie_ge = jnp.where(tau_lo == INT16_MIN, n_hi_ge - n_hi_gt, n_kb_ge)
    excess = jnp.max(jnp.where(live, n_tie_ge - want_lo, 0.0))

    @pl.when(excess <= 0)
    def _():
        tau_lo16 = jnp.broadcast_to(jnp.where(live, tau_lo, -INT16_MIN - 1).astype(I16), (kt, tq))

        def body(jt, carry):
            rows = tile_rows(jt)
            h16 = hi_ref[rows, :]
            in_bucket = jnp.where(lo_ref[rows, :] >= tau_lo16, zero, neg_inf)
            bias_ref[rows, :] = jnp.where(h16 > tau_hi16, zero, jnp.where(h16 == tau_hi16, in_bucket, neg_inf))
            return carry
        lax.fori_loop(0, n_tiles, body, 0)

    @pl.when(excess > 0)
    def _():
        need = want_lo - count16(kb_ref, tau_lo, True)
        live_f = jnp.where(live, 1.0, 0.0)
        tri = (lax.broadcasted_iota(I32, (kt, kt), 0)
               >= lax.broadcasted_iota(I32, (kt, kt), 1)).astype(BF16)

        def body(jt, seen):
            rows = tile_rows(jt)
            h32 = hi_ref[rows, :].astype(I32)
            l32 = lo_ref[rows, :].astype(I32)
            same_hi = h32 == tau_hi
            gt = jnp.where(h32 > tau_hi, 1.0, jnp.where(same_hi, jnp.where(l32 > tau_lo, 1.0, 0.0), 0.0))
            eq = jnp.where(same_hi, jnp.where(l32 == tau_lo, 1.0, 0.0), 0.0)
            rank = _dot(tri, eq.astype(BF16)) + seen
            sel = gt + eq * jnp.where(rank <= need, live_f, 0.0)
            bias_ref[rows, :] = jnp.where(sel > 0.0, 0.0, -jnp.inf).astype(BF16)
            return rank[kt - 1:kt, :]
        lax.fori_loop(0, n_tiles, body, jnp.zeros((1, tq), F32))

    for a in acc_refs:
        a[...] = jnp.zeros_like(a)
    heads = [slice(h * HEAD_DIM, (h + 1) * HEAD_DIM) for h in range(N_HEADS)]

    def attn_tile(jt, carry):
        ms, ls = carry
        rows = tile_rows(jt)
        bias = bias_ref[rows, :].astype(F32)
        lts = [_dot_nt(k_ref[rows, sl], q_ref[:, sl]) + bias for sl in heads]
        new_ms, new_ls, alphas, ps = [], [], [], []
        for h in range(N_HEADS):
            m_new = jnp.maximum(ms[h], jnp.max(lts[h], axis=0, keepdims=True))
            m_safe = jnp.where(m_new == -jnp.inf, 0.0, m_new)
            p = jnp.exp2(lts[h] - m_safe)
            alpha = jnp.exp2(ms[h] - m_safe)
            new_ls.append(alpha * ls[h] + jnp.sum(p, axis=0, keepdims=True))
            new_ms.append(m_new)
            alphas.append(alpha)
            ps.append(p.astype(BF16))
        for h in range(N_HEADS):
            acc_refs[h][...] = acc_refs[h][...] * alphas[h] + _dot(vt_ref[jt, heads[h], :], ps[h])
        return tuple(new_ms), tuple(new_ls)

    init = (tuple(jnp.full((1, tq), -jnp.inf, F32) for _ in range(N_HEADS)),
            tuple(jnp.zeros((1, tq), F32) for _ in range(N_HEADS)))
    _, ls = lax.fori_loop(0, n_tiles, attn_tile, init)
    for h in range(N_HEADS):
        out_t = acc_refs[h][...] * (1.0 / ls[h])
        o_ref[:, heads[h]] = out_t.T


def _dsa_attn(q, k, vt, qi, ki, wt, bsz, seq, tq=256, kt=256):
    n = bsz * seq
    nq = seq // tq
    top_k = min(TOPK_MAX, seq // 4)
    once = dict(pipeline_mode=pl.Buffered(1))
    strip16 = pltpu.VMEM((seq, tq), I16)
    return pl.pallas_call(
        functools.partial(_dsa_attn_kernel, tq=tq, kt=kt, top_k=top_k),
        out_shape=jax.ShapeDtypeStruct((n, WIDTH), F32),
        grid=(bsz, nq),
        in_specs=[pl.BlockSpec((tq, IDX_HEADS * IDX_DIM), lambda b, i: (b * nq + i, 0)),
                  pl.BlockSpec((IDX_HEADS, tq), lambda b, i: (0, b * nq + i)),
                  pl.BlockSpec((tq, WIDTH), lambda b, i: (b * nq + i, 0)),
                  pl.BlockSpec((seq, LANES), lambda b, i: (b, 0), **once),
                  pl.BlockSpec((seq, WIDTH), lambda b, i: (b, 0), **once),
                  pl.BlockSpec((seq // kt, WIDTH, kt), lambda b, i: (b, 0, 0), **once)],
        out_specs=pl.BlockSpec((tq, WIDTH), lambda b, i: (b * nq + i, 0)),
        scratch_shapes=[pltpu.VMEM((IDX_HEADS * tq, LANES), BF16), strip16, strip16, strip16,
                        pltpu.VMEM((seq, tq), BF16)]
                       + [pltpu.VMEM((HEAD_DIM, tq), F32) for _ in range(N_HEADS)],
        compiler_params=_params("parallel", "arbitrary"),
        name="dsa_attn",
    )(qi, wt, q, ki, k, vt)


def _mem_kv_kernel(mem_ref, g_ref, wk_ref, wv_ref, gk_ref, k_out, v_out):
    mb = _rms(mem_ref[...], g_ref[...]).astype(BF16)
    kf = _dot(mb, wk_ref[...])
    for h in range(N_HEADS):
        sl = slice(h * HEAD_DIM, (h + 1) * HEAD_DIM)
        k_out[:, sl] = _rms(kf[:, sl], gk_ref[...]).astype(BF16)
    v_out[...] = _dot(mb, wv_ref[...]).astype(BF16)


def _mem_kv(mem2, gain, wk, wv, gk, n_mem):
    n, d = mem2.shape
    full = lambda a: pl.BlockSpec(a.shape, lambda b: (0,) * a.ndim)
    return pl.pallas_call(
        _mem_kv_kernel,
        out_shape=(jax.ShapeDtypeStruct((n, WIDTH), BF16), jax.ShapeDtypeStruct((n, WIDTH), BF16)),
        grid=(n // n_mem,),
        in_specs=[pl.BlockSpec((n_mem, d), lambda b: (b, 0)), full(gain), full(wk), full(wv), full(gk)],
        out_specs=(pl.BlockSpec((n_mem, WIDTH), lambda b: (b, 0)),
                   pl.BlockSpec((n_mem, WIDTH), lambda b: (b, 0))),
        compiler_params=_params("parallel"),
        name="mem_kv",
    )(mem2, gain, wk, wv, gk)


def _xattn_kernel(x_ref, oa_ref, ob_ref, wout_ref, g_ref, wq_ref, gq_ref, km_ref, vm_ref, wo_ref, o_ref):
    x1 = (x_ref[...] + _dot(oa_ref[...].astype(BF16), wout_ref[:WIDTH, :])
          + _dot(ob_ref[...].astype(BF16), wout_ref[WIDTH:, :]))
    hb = _rms(x1, g_ref[...]).astype(BF16)
    qf = _dot(hb, wq_ref[...])
    scale = HEAD_DIM ** -0.5
    outs = []
    for h in range(N_HEADS):
        sl = slice(h * HEAD_DIM, (h + 1) * HEAD_DIM)
        qh = (_rms(qf[:, sl], gq_ref[...]) * scale).astype(BF16)
        logits = _dot_nt(qh, km_ref[:, sl])
        p = jnp.exp(logits - jnp.max(logits, axis=-1, keepdims=True))
        oh = _dot(p.astype(BF16), vm_ref[:, sl])
        outs.append((oh * (1.0 / jnp.sum(p, axis=-1, keepdims=True))).astype(BF16))
    o_ref[...] = x1 + _dot(jnp.concatenate(outs, axis=1), wo_ref[...])


def _xattn(x2, oa, ob, wout, gain, wq, gq, km, vm, wo, seq, n_mem, tm=512):
    n, d = x2.shape
    per_b = seq // tm
    row = lambda w: pl.BlockSpec((tm, w), lambda i: (i, 0))
    full = lambda a: pl.BlockSpec(a.shape, lambda i: (0,) * a.ndim)
    memb = pl.BlockSpec((n_mem, WIDTH), lambda i: (i // per_b, 0))
    return pl.pallas_call(
        _xattn_kernel,
        out_shape=jax.ShapeDtypeStruct((n, d), F32),
        grid=(n // tm,),
        in_specs=[row(d), row(WIDTH), row(WIDTH), full(wout), full(gain), full(wq), full(gq),
                  memb, memb, full(wo)],
        out_specs=row(d),
        compiler_params=_params("parallel"),
        name="xattn",
    )(x2, oa, ob, wout, gain, wq, gq, km, vm, wo)


def _moe_kernel(x_ref, g_ref, wr_hi_ref, wr_lo_ref, br_ref, wgu_ref, wdn_ref, o_ref, hb_ref, gate_ref):
    e = pl.program_id(1)
    lane = lax.broadcasted_iota(I32, gate_ref.shape, 1)
    lane_f = lane.astype(F32)

    @pl.when(e == 0)
    def _():
        x = x_ref[...]
        h = _rms(x, g_ref[...])
        h_hi = h.astype(BF16)
        h_lo = (h - h_hi.astype(F32)).astype(BF16)
        hb_ref[...] = h_hi
        lg = (_dot(h_hi, wr_hi_ref[...]) + _dot(h_lo, wr_hi_ref[...]) + _dot(h_hi, wr_lo_ref[...])
              + br_ref[...])
        first = lambda cond: jnp.min(jnp.where(cond, lane_f, 1e9), axis=-1, keepdims=True)
        gl = jnp.where(lane < N_GROUPS, lg, -jnp.inf)
        gmax = jnp.max(gl, axis=-1, keepdims=True)
        gsel = first(gl == gmax)
        g_w = 1.0 / jnp.sum(jnp.exp(gl - gmax), axis=-1, keepdims=True)
        grp_of_lane = ((lane - N_GROUPS) >> 2).astype(F32)
        in_grp = (lane >= N_GROUPS) & (lane < N_GROUPS + N_EXPERTS) & (grp_of_lane == gsel)
        el = jnp.where(in_grp, lg, -jnp.inf)
        v1 = jnp.max(el, axis=-1, keepdims=True)
        i1 = first(el == v1)
        el2 = jnp.where(lane_f == i1, -jnp.inf, el)
        v2 = jnp.max(el2, axis=-1, keepdims=True)
        i2 = first(el2 == v2)
        r = jnp.exp(v2 - v1)
        w1 = 1.0 / (1.0 + r)
        gate_ref[...] = jnp.where(lane_f == i1, w1 * g_w, jnp.where(lane_f == i2, r * w1 * g_w, 0.0))
        o_ref[...] = x

    ge = jnp.sum(jnp.where(lane == e + N_GROUPS, gate_ref[...], 0.0), axis=-1, keepdims=True)
    gu = _dot(hb_ref[...], wgu_ref[0])
    up = gu[:, :D_EXPERT]
    act = up * jax.nn.sigmoid(up) * gu[:, D_EXPERT:] * ge
    o_ref[...] += _dot(act.astype(BF16), wdn_ref[0])


def _moe(x2, gain, wr_hi, wr_lo, br, wgu, wdn, tm=1024):
    n, d = x2.shape
    full = lambda a: pl.BlockSpec(a.shape, lambda i, e: (0,) * a.ndim)
    return pl.pallas_call(
        _moe_kernel,
        out_shape=jax.ShapeDtypeStruct((n, d), F32),
        grid=(n // tm, N_EXPERTS),
        in_specs=[pl.BlockSpec((tm, d), lambda i, e: (i, 0)), full(gain), full(wr_hi), full(wr_lo), full(br),
                  pl.BlockSpec((1, d, 2 * D_EXPERT), lambda i, e: (e, 0, 0)),
                  pl.BlockSpec((1, D_EXPERT, d), lambda i, e: (e, 0, 0))],
        out_specs=pl.BlockSpec((tm, d), lambda i, e: (i, 0)),
        scratch_shapes=[pltpu.VMEM((tm, d), BF16), pltpu.VMEM((tm, LANES), F32)],
        compiler_params=_params("parallel", "arbitrary"),
        name="moe",
    )(x2, gain, wr_hi, wr_lo, br, wgu, wdn)


def _rope_tables(positions):
    pos = positions.reshape(-1).astype(F32)[:, None]

    def cs(dim):
        inv = ROPE_THETA ** (-jnp.arange(0, dim, 2, dtype=F32) / dim)
        ang = pos * inv
        return jnp.cos(ang), jnp.sin(ang)

    c, s = cs(HEAD_DIM)
    ci, si = cs(IDX_DIM)
    z = jnp.zeros_like(si)
    return (jnp.concatenate([c, c], 1), jnp.concatenate([-s, s], 1),
            jnp.concatenate([ci, ci, ci, ci], 1), jnp.concatenate([-si, z, -si, z], 1),
            jnp.concatenate([z, si, z, si], 1))


def _reorder_in_weight(w):
    a = w[:, :4 * WIDTH]
    o = 4 * WIDTH
    qlat = w[:, o:o + Q_LORA]; o += Q_LORA
    k = w[:, o:o + WIDTH]; o += WIDTH
    v = w[:, o:o + WIDTH]; o += WIDTH
    misc = w[:, o:]
    pad = jnp.zeros((w.shape[0], LANES - misc.shape[1]), w.dtype)
    return jnp.concatenate([a, k, v, qlat, misc, pad], axis=1).astype(BF16)


def _row(v):
    return v.reshape(1, -1).astype(F32)


def kernel(x, mem, positions, norm_mix, w_in, lb_raw, a_gnorm, b_qlat_gain, b_wqb, b_wqidx, b_qnorm,
           b_knorm, b_kidx_norm, w_out, norm_x, norm_mem, x_wq, x_wk, x_wv, x_wo, x_qnorm, x_knorm,
           norm_ffn, w_rg, b_rg, w_re, b_re, w_gu, w_dn):
    bsz, seq, d = x.shape
    n_mem = mem.shape[1]
    depth = w_in.shape[0]
    x2 = x.reshape(bsz * seq, d)
    mem2 = mem.reshape(bsz * n_mem, d)
    tabs = _rope_tables(positions)
    pad_lanes = lambda v: jnp.pad(v, ((0, 0), (0, LANES - v.shape[1])))
    for l in range(depth):
        u = _in_proj(x2, _row(norm_mix[l]), _reorder_in_weight(w_in[l]))
        o_a = _hgrn(u, lb_raw.astype(F32), _row(a_gnorm[l]), l, bsz, seq)
        q, k, vt, qi, ki, wt = _dsa_prep(
            u, tabs, b_wqb[l].astype(BF16), b_wqidx[l].astype(BF16), _row(b_qlat_gain[l]),
            _row(b_qnorm[l]), _row(b_knorm[l]), pad_lanes(_row(b_kidx_norm[l])))
        o_b = _dsa_attn(q, k, vt, qi, ki, wt, bsz, seq)
        km, vm = _mem_kv(mem2, _row(norm_mem[l]), x_wk[l].astype(BF16), x_wv[l].astype(BF16),
                         _row(x_knorm[l]), n_mem)
        x2 = _xattn(x2, o_a, o_b, w_out[l].astype(BF16), _row(norm_x[l]), x_wq[l].astype(BF16),
                    _row(x_qnorm[l]), km, vm, x_wo[l].astype(BF16), seq, n_mem)
        wr = pad_lanes(jnp.concatenate([w_rg[l], w_re[l].reshape(d, N_EXPERTS)], axis=1).astype(F32))
        wr_hi = wr.astype(BF16)
        wr_lo = (wr - wr_hi.astype(F32)).astype(BF16)
        br = pad_lanes(jnp.concatenate([b_rg[l], b_re[l].reshape(-1)]).reshape(1, -1).astype(F32))
        x2 = _moe(x2, _row(norm_ffn[l]), wr_hi, wr_lo, br,
                  w_gu[l].reshape(N_EXPERTS, d, 2 * D_EXPERT).astype(BF16),
                  w_dn[l].reshape(N_EXPERTS, D_EXPERT, d).astype(BF16))
    return x2.reshape(bsz, seq, d)
```

```python
import functools

import numpy as np
import jax
import jax.numpy as jnp
from jax import lax
from jax.experimental import pallas as pl
from jax.experimental.pallas import tpu as pltpu

F32 = jnp.float32
BF16 = jnp.bfloat16
I32 = jnp.int32
I16 = jnp.int16

EPS = 1e-6
ROPE_THETA = 10000.0
CHUNK = 64
CHUNK_SHIFT = 6
SUB = 16
SUB_SHIFT = 4
N_HEADS = 4
HEAD_DIM = 128
WIDTH = N_HEADS * HEAD_DIM
Q_LORA = 256
IDX_HEADS = 16
IDX_DIM = 64
TOPK_MAX = 256
N_GROUPS = 4
EXP_PER_GROUP = 4
N_EXPERTS = N_GROUPS * EXP_PER_GROUP
D_EXPERT = 256
LANES = 128
INT_MIN = -2 ** 31
INT16_MIN = -2 ** 15
PACK = 16
LOG2E = 1.4426950408889634
DSA_KEY_TILE = 512

U_AQ, U_AF, U_AI, U_AG = 0, 512, 1024, 1536
U_K, U_V, U_QLAT, U_MISC = 2048, 2560, 3072, 3328
U_COLS = 3456

VMEM_LIMIT = 56 * 1024 * 1024


def _rms(x, gain):
    return x * lax.rsqrt(jnp.mean(x * x, axis=-1, keepdims=True) + EPS) * gain


def _dot(a, b):
    return jnp.dot(a, b, preferred_element_type=F32)


def _dot_nt(a, b):
    return lax.dot_general(a, b, (((1,), (1,)), ((), ())), preferred_element_type=F32)


def _params(*sem):
    return pltpu.CompilerParams(dimension_semantics=sem, vmem_limit_bytes=VMEM_LIMIT)


def _in_proj_kernel(x_ref, g_ref, w_ref, u_ref, *, col_chunk):
    hb = _rms(x_ref[...], g_ref[...]).astype(BF16)
    for c0 in range(0, U_COLS, col_chunk):
        u_ref[:, c0:c0 + col_chunk] = _dot(hb, w_ref[:, c0:c0 + col_chunk])


def _in_proj(x2, gain, w_bf, tm=256):
    n, d = x2.shape
    return pl.pallas_call(
        functools.partial(_in_proj_kernel, col_chunk=1152),
        out_shape=jax.ShapeDtypeStruct((n, U_COLS), F32),
        grid=(n // tm,),
        in_specs=[pl.BlockSpec((tm, d), lambda i: (i, 0)),
                  pl.BlockSpec((1, d), lambda i: (0, 0)),
                  pl.BlockSpec((d, U_COLS), lambda i: (0, 0))],
        out_specs=pl.BlockSpec((tm, U_COLS), lambda i: (i, 0)),
        compiler_params=_params("parallel"),
        name="in_proj",
    )(x2, gain, w_bf)


def _split3(x):
    hi = x.astype(BF16)
    r1 = x - hi.astype(F32)
    mid = r1.astype(BF16)
    lo = (r1 - mid.astype(F32)).astype(BF16)
    return hi, mid, lo


def _hgrn_kernel(q_ref, f_ref, i_ref, g_ref, lbraw_ref, gn_ref, e_ref, o_ref, st_ref,
                 *, layer, n_chunks, unroll):
    @pl.when(pl.program_id(2) == 0)
    def _():
        st_ref[...] = jnp.zeros_like(st_ref)

    lr = lbraw_ref[...]
    ex = jnp.exp(lr - jnp.max(lr, axis=0, keepdims=True))
    sm = ex / jnp.sum(ex, axis=0, keepdims=True)
    lb = jnp.zeros((1, HEAD_DIM), F32)
    for r in range(1, layer + 1):
        lb = lb + sm[r:r + 1, :]
    log_lb = jnp.log(lb)
    log_1mlb = jnp.log(1.0 - lb)

    row = lax.broadcasted_iota(I32, (CHUNK, HEAD_DIM), 0)
    col = lax.broadcasted_iota(I32, (CHUNK, HEAD_DIM), 1)
    row_l = row & (SUB - 1)
    row_b = row >> SUB_SHIFT
    col_b = col >> SUB_SHIFT
    tri = (lax.broadcasted_iota(I32, (CHUNK, CHUNK), 0)
           >= lax.broadcasted_iota(I32, (CHUNK, CHUNK), 1)).astype(BF16)
    n_sub = CHUNK // SUB
    zeros_c = jnp.zeros((CHUNK, HEAD_DIM), F32)

    def sub_bcast(t, s):
        t4 = t.reshape(n_sub, SUB, HEAD_DIM)
        return jnp.broadcast_to(t4[:, s:s + 1, :], (n_sub, SUB, HEAD_DIM)).reshape(CHUNK, HEAD_DIM)

    def gates(z):
        ls = jnp.minimum(z, 0.0) - jnp.log(1.0 + jnp.exp(-jnp.abs(z)))
        if layer == 0:
            return ls, (ls - z) * LOG2E
        cc = log_1mlb + ls
        log_f = jnp.maximum(log_lb, cc) + jnp.log(1.0 + jnp.exp(-jnp.abs(log_lb - cc)))
        return log_f, (cc - z) * LOG2E

    def diag_terms(q, b2, c2):
        pieces = []
        for s in range(SUB):
            d = jnp.where(row_l >= s, b2 - sub_bcast(c2, s), -jnp.inf)
            pieces.append((q * jnp.exp2(d)).astype(BF16))
        return jnp.concatenate(pieces, axis=1)

    def below_keys(b2, c2):
        kts = []
        for i in range(1, n_sub):
            r_i = b2[i * SUB:i * SUB + 1, :]
            kts.append(jnp.exp2(jnp.where(row < i * SUB, r_i - c2, -jnp.inf)))
            kts.append(zeros_c)
        return jnp.concatenate(kts, axis=0).astype(BF16)

    def below_scores(r):
        a = jnp.zeros((CHUNK, HEAD_DIM), F32)
        for i in range(1, n_sub):
            a = a + jnp.where(row_b == i, r[:, (i - 1) * LANES:i * LANES], 0.0)
        return a

    def chunks(it, carry):
        cs = range(unroll)
        rows = [pl.ds(pl.multiple_of((it * unroll + c) * CHUNK, CHUNK), CHUNK) for c in cs]
        qr = [q_ref[r, :] for r in rows]
        v = [i_ref[r, :] for r in rows]
        q = [x * jax.nn.sigmoid(x) for x in qr]
        gt = [gates(f_ref[r, :]) for r in rows]
        b2 = []
        for c in cs:
            hi, mid, lo = _split3(gt[c][0])
            b2.append((_dot(tri, hi) + _dot(tri, mid) + _dot(tri, lo)) * LOG2E)
        c2 = [b2[c] - gt[c][1] for c in cs]
        b_last = [x[CHUNK - 1:CHUNK, :] for x in b2]
        a2 = [_dot(diag_terms(q[c], b2[c], c2[c]), e_ref[...]) for c in cs]
        rr = [_dot_nt((q[c] * jnp.exp2(b2[c] - sub_bcast(b2[c], 0))).astype(BF16), below_keys(b2[c], c2[c]))
              for c in cs]
        upd = [_dot(v[c].T.astype(BF16), jnp.exp2(b_last[c] - c2[c]).astype(BF16)) for c in cs]
        qe = [(q[c] * jnp.exp2(b2[c])).astype(BF16) for c in cs]
        st = st_ref[...]
        o_inter = []
        for c in cs:
            o_inter.append(_dot_nt(qe[c], st.astype(BF16)))
            st = st * jnp.exp2(b_last[c]) + upd[c]
        st_ref[...] = st
        a = [jnp.where(col_b == row_b, a2[c], 0.0) + below_scores(rr[c]) for c in cs]
        o_intra = [_dot(a[c][:, :CHUNK].astype(BF16), v[c].astype(BF16)) for c in cs]
        for c in cs:
            g = g_ref[rows[c], :]
            o_ref[rows[c], :] = _rms(o_inter[c] + o_intra[c], gn_ref[...]) * (g * jax.nn.sigmoid(g))
        return carry

    lax.fori_loop(0, n_chunks // unroll, chunks, 0)


def _hgrn_selector():
    e = np.zeros((SUB * HEAD_DIM, LANES), np.float32)
    s_of_row = np.arange(SUB * HEAD_DIM) // HEAD_DIM
    cols = np.arange(LANES)
    e[:, :] = ((cols[None, :] % SUB) == s_of_row[:, None]) & (cols[None, :] < CHUNK)
    return jnp.asarray(e, BF16)


def _hgrn(u, lb_raw, gn, layer, bsz, seq, tb=512):
    n = bsz * seq
    nb = seq // tb
    blk = lambda k: pl.BlockSpec((tb, HEAD_DIM), lambda b, h, j, k=k: (b * nb + j, h + N_HEADS * k))
    n_layers = lb_raw.shape[0]
    return pl.pallas_call(
        functools.partial(_hgrn_kernel, layer=layer, n_chunks=tb // CHUNK, unroll=4),
        out_shape=jax.ShapeDtypeStruct((n, WIDTH), F32),
        grid=(bsz, N_HEADS, nb),
        in_specs=[blk(0), blk(1), blk(2), blk(3),
                  pl.BlockSpec((n_layers, HEAD_DIM), lambda b, h, j: (0, h)),
                  pl.BlockSpec((1, HEAD_DIM), lambda b, h, j: (0, 0)),
                  pl.BlockSpec((SUB * HEAD_DIM, LANES), lambda b, h, j: (0, 0))],
        out_specs=pl.BlockSpec((tb, HEAD_DIM), lambda b, h, j: (b * nb + j, h)),
        scratch_shapes=[pltpu.VMEM((HEAD_DIM, HEAD_DIM), F32)],
        compiler_params=_params("parallel", "parallel", "arbitrary"),
        name="hgrn",
    )(u, u, u, u, lb_raw, gn, _hgrn_selector())


def _rope128(x, cos, sin_signed):
    return x * cos + pltpu.roll(x, 64, 1) * sin_signed


def _rope64(x, cos, sin_a, sin_b):
    return x * cos + pltpu.roll(x, 96, 1) * sin_a + pltpu.roll(x, 32, 1) * sin_b


def _dsa_prep_kernel(k_ref, v_ref, ql_ref, misc_ref, cm_ref, sm_ref, ci_ref, sia_ref, sib_ref,
                     wqb_ref, wqi_ref, gql_ref, gqn_ref, gkn_ref, gki_ref,
                     q_out, k_out, vt_out, qi_out, ki_out, wt_out):
    cm, sm = cm_ref[...], sm_ref[...]
    ci, sia, sib = ci_ref[...], sia_ref[...], sib_ref[...]
    cb = _rms(ql_ref[...], gql_ref[...]).astype(BF16)
    qf = _dot(cb, wqb_ref[...])
    scale = HEAD_DIM ** -0.5
    for h in range(N_HEADS):
        sl = slice(h * HEAD_DIM, (h + 1) * HEAD_DIM)
        qh = _rope128(_rms(qf[:, sl], gqn_ref[...]), cm, sm)
        q_out[:, sl] = (qh * (scale * LOG2E)).astype(BF16)
        kh = _rope128(_rms(k_ref[:, sl], gkn_ref[...]), cm, sm)
        k_out[:, sl] = kh.astype(BF16)
    vt_out[0] = v_ref[...].T.astype(BF16)
    qi = _dot(cb, wqi_ref[...])
    for p in range(IDX_HEADS * IDX_DIM // LANES):
        sl = slice(p * LANES, (p + 1) * LANES)
        qi_out[:, sl] = _rope64(qi[:, sl], ci, sia, sib).astype(BF16)
    misc = misc_ref[...]
    lane = lax.broadcasted_iota(I32, misc.shape, 1)
    kraw = jnp.where(lane < IDX_DIM, misc, 0.0)
    kn = kraw * lax.rsqrt(jnp.sum(kraw * kraw, axis=-1, keepdims=True) * (1.0 / IDX_DIM) + EPS)
    kn = _rope64(kn * gki_ref[...], ci, sia, sib)
    ki_out[...] = (kn + pltpu.roll(kn, IDX_DIM, 1)).astype(BF16)
    wt = (misc * (IDX_HEADS ** -0.5 * IDX_DIM ** -0.5)).T
    wt_out[...] = wt[IDX_DIM:IDX_DIM + IDX_HEADS, :]


def _dsa_prep(u, tabs, wqb, wqi, gql, gqn, gkn, gki, tm=DSA_KEY_TILE):
    n = u.shape[0]
    nt = n // tm
    ublk = lambda width, col: pl.BlockSpec((tm, width), lambda i: (i, col // width))
    tab = pl.BlockSpec((tm, LANES), lambda i: (i, 0))
    full = lambda a: pl.BlockSpec(a.shape, lambda i: (0,) * a.ndim)
    return pl.pallas_call(
        _dsa_prep_kernel,
        out_shape=(jax.ShapeDtypeStruct((n, WIDTH), BF16),
                   jax.ShapeDtypeStruct((n, WIDTH), BF16),
                   jax.ShapeDtypeStruct((nt, WIDTH, tm), BF16),
                   jax.ShapeDtypeStruct((n, IDX_HEADS * IDX_DIM), BF16),
                   jax.ShapeDtypeStruct((n, LANES), BF16),
                   jax.ShapeDtypeStruct((IDX_HEADS, n), F32)),
        grid=(nt,),
        in_specs=[ublk(WIDTH, U_K), ublk(WIDTH, U_V), ublk(Q_LORA, U_QLAT), ublk(LANES, U_MISC),
                  tab, tab, tab, tab, tab,
                  full(wqb), full(wqi), full(gql), full(gqn), full(gkn), full(gki)],
        out_specs=(pl.BlockSpec((tm, WIDTH), lambda i: (i, 0)),
                   pl.BlockSpec((tm, WIDTH), lambda i: (i, 0)),
                   pl.BlockSpec((1, WIDTH, tm), lambda i: (i, 0, 0)),
                   pl.BlockSpec((tm, IDX_HEADS * IDX_DIM), lambda i: (i, 0)),
                   pl.BlockSpec((tm, LANES), lambda i: (i, 0)),
                   pl.BlockSpec((IDX_HEADS, tm), lambda i: (0, i))),
        compiler_params=_params("parallel"),
        name="dsa_prep",
    )(u, u, u, u, *tabs, wqb, wqi, gql, gqn, gkn, gki)


def _sortable(x):
    bits = lax.bitcast_convert_type(x, I32)
    return bits ^ ((bits >> 31) & 0x7FFFFFFF)


def _dsa_attn_kernel(qi_ref, wt_ref, q_ref, ki_ref, k_ref, vt_ref, o_ref,
                     qm_ref, hi_ref, lo_ref, kb_ref, bias_ref, acc0, acc1, acc2, acc3, *, tq, kt, top_k):
    acc_refs = (acc0, acc1, acc2, acc3)
    q0 = pl.program_id(1) * tq
    n_tiles = (q0 + tq + kt - 1) // kt

    lane_q = lax.broadcasted_iota(I32, (tq, LANES), 1)
    for h in range(IDX_HEADS):
        src = qi_ref[:, (h // 2) * LANES:(h // 2 + 1) * LANES].astype(F32)
        keep = (lane_q < IDX_DIM) if h % 2 == 0 else (lane_q >= IDX_DIM)
        qm_ref[h * tq:(h + 1) * tq, :] = jnp.where(keep, src, 0.0).astype(BF16)

    qpos = q0 + lax.broadcasted_iota(I32, (1, tq), 1)
    limit = ((qpos >> CHUNK_SHIFT) + 1) << CHUNK_SHIFT
    sub = 128
    key_iota = lax.broadcasted_iota(I32, (sub, tq), 0)

    def tile_rows(jt):
        return pl.ds(pl.multiple_of(jt * kt, kt), kt)

    def score_tile(jt, carry):
        for part in range(kt // sub):
            base = jt * kt + part * sub
            rows = pl.ds(pl.multiple_of(base, sub), sub)
            ki = ki_ref[rows, :]
            acc = jnp.zeros((sub, tq), F32)
            for h in range(IDX_HEADS):
                x = _dot_nt(ki, qm_ref[h * tq:(h + 1) * tq, :])
                acc = acc + wt_ref[h:h + 1, :] * jnp.maximum(x, 0.0)
            key = jnp.where(base + key_iota < limit, _sortable(acc), INT_MIN)
            hi_ref[rows, :] = (key >> 16).astype(I16)
            lo_ref[rows, :] = ((key & 0xFFFF) - 32768).astype(I16)
        return carry

    lax.fori_loop(0, n_tiles, score_tile, 0)

    one, zero = jnp.ones((), BF16), jnp.zeros((), BF16)
    neg_inf = jnp.full((), -jnp.inf, BF16)
    n_rows = (n_tiles * kt).astype(F32)

    def count16(ref, trial, strict):
        t16 = jnp.broadcast_to(trial.astype(I16), (PACK, tq))

        def body(jt, c):
            x = ref[tile_rows(jt), :]
            parts = [jnp.zeros((PACK, tq), BF16) for _ in range(4)]
            for r in range(kt // PACK):
                xr = x[r * PACK:(r + 1) * PACK, :]
                parts[r % 4] = parts[r % 4] + jnp.where((xr > t16) if strict else (xr >= t16), one, zero)
            return c + ((parts[0] + parts[1]) + (parts[2] + parts[3])).astype(F32)
        c = lax.fori_loop(0, n_tiles, body, jnp.zeros((PACK, tq), F32))
        return jnp.sum(c, axis=0, keepdims=True)

    def bisect16(ref, want):
        def step(it, carry):
            lo, n_lo = carry
            trial = lo + lax.shift_left(jnp.int32(1), 15 - it)
            n = count16(ref, trial, False)
            ok = n >= want
            return jnp.where(ok, trial, lo), jnp.where(ok, n, n_lo)
        return lax.fori_loop(0, 16, step, (jnp.full((1, tq), INT16_MIN, I32), jnp.full((1, tq), n_rows, F32)))

    tau_hi, n_hi_ge = bisect16(hi_ref, float(top_k))
    n_hi_gt = count16(hi_ref, tau_hi, True)
    want_lo = top_k - n_hi_gt
    tau_hi16 = jnp.broadcast_to(tau_hi.astype(I16), (kt, tq))

    def bucket_tile(jt, carry):
        rows = tile_rows(jt)
        kb_ref[rows, :] = jnp.where(hi_ref[rows, :] == tau_hi16, lo_ref[rows, :],
                                    jnp.full((), INT16_MIN, I16))
        return carry

    lax.fori_loop(0, n_tiles, bucket_tile, 0)
    tau_lo, n_kb_ge = bisect16(kb_ref, want_lo)
    live = (tau_hi > INT16_MIN) | (tau_lo > INT16_MIN)
    n_tie_ge = jnp.where(tau_lo == INT16_MIN, n_hi_ge - n_hi_gt, n_kb_ge)
    excess = jnp.max(jnp.where(live, n_tie_ge - want_lo, 0.0))

    @pl.when(excess <= 0)
    def _():
        tau_lo16 = jnp.broadcast_to(jnp.where(live, tau_lo, -INT16_MIN - 1).astype(I16), (kt, tq))

        def body(jt, carry):
            rows = tile_rows(jt)
            h16 = hi_ref[rows, :]
            in_bucket = jnp.where(lo_ref[rows, :] >= tau_lo16, zero, neg_inf)
            bias_ref[rows, :] = jnp.where(h16 > tau_hi16, zero, jnp.where(h16 == tau_hi16, in_bucket, neg_inf))
            return carry
        lax.fori_loop(0, n_tiles, body, 0)

    @pl.when(excess > 0)
    def _():
        need = want_lo - count16(kb_ref, tau_lo, True)
        live_f = jnp.where(live, 1.0, 0.0)
        tri = (lax.broadcasted_iota(I32, (kt, kt), 0)
               >= lax.broadcasted_iota(I32, (kt, kt), 1)).astype(BF16)

        def body(jt, seen):
            rows = tile_rows(jt)
            h32 = hi_ref[rows, :].astype(I32)
            l32 = lo_ref[rows, :].astype(I32)
            same_hi = h32 == tau_hi
            gt = jnp.where(h32 > tau_hi, 1.0, jnp.where(same_hi, jnp.where(l32 > tau_lo, 1.0, 0.0), 0.0))
            eq = jnp.where(same_hi, jnp.where(l32 == tau_lo, 1.0, 0.0), 0.0)
            rank = _dot(tri, eq.astype(BF16)) + seen
            sel = gt + eq * jnp.where(rank <= need, live_f, 0.0)
            bias_ref[rows, :] = jnp.where(sel > 0.0, 0.0, -jnp.inf).astype(BF16)
            return rank[kt - 1:kt, :]
        lax.fori_loop(0, n_tiles, body, jnp.zeros((1, tq), F32))

    for a in acc_refs:
        a[...] = jnp.zeros_like(a)
    heads = [slice(h * HEAD_DIM, (h + 1) * HEAD_DIM) for h in range(N_HEADS)]

    def attn_tile(jt, carry):
        ms, ls = carry
        rows = tile_rows(jt)
        bias = bias_ref[rows, :].astype(F32)
        lts = [_dot_nt(k_ref[rows, sl], q_ref[:, sl]) + bias for sl in heads]
        new_ms, new_ls, alphas, ps = [], [], [], []
        for h in range(N_HEADS):
            m_new = jnp.maximum(ms[h], jnp.max(lts[h], axis=0, keepdims=True))
            m_safe = jnp.where(m_new == -jnp.inf, 0.0, m_new)
            p = jnp.exp2(lts[h] - m_safe)
            alpha = jnp.exp2(ms[h] - m_safe)
            new_ls.append(alpha * ls[h] + jnp.sum(p, axis=0, keepdims=True))
            new_ms.append(m_new)
            alphas.append(alpha)
            ps.append(p.astype(BF16))
        for h in range(N_HEADS):
            acc_refs[h][...] = acc_refs[h][...] * alphas[h] + _dot(vt_ref[jt, heads[h], :], ps[h])
        return tuple(new_ms), tuple(new_ls)

    init = (tuple(jnp.full((1, tq), -jnp.inf, F32) for _ in range(N_HEADS)),
            tuple(jnp.zeros((1, tq), F32) for _ in range(N_HEADS)))
    _, ls = lax.fori_loop(0, n_tiles, attn_tile, init)
    for h in range(N_HEADS):
        out_t = acc_refs[h][...] * (1.0 / ls[h])
        o_ref[:, heads[h]] = out_t.T


def _dsa_attn(q, k, vt, qi, ki, wt, bsz, seq, tq=256, kt=DSA_KEY_TILE):
    n = bsz * seq
    nq = seq // tq
    top_k = min(TOPK_MAX, seq // 4)
    once = dict(pipeline_mode=pl.Buffered(1))
    strip16 = pltpu.VMEM((seq, tq), I16)
    return pl.pallas_call(
        functools.partial(_dsa_attn_kernel, tq=tq, kt=kt, top_k=top_k),
        out_shape=jax.ShapeDtypeStruct((n, WIDTH), F32),
        grid=(bsz, nq),
        in_specs=[pl.BlockSpec((tq, IDX_HEADS * IDX_DIM), lambda b, i: (b * nq + i, 0)),
                  pl.BlockSpec((IDX_HEADS, tq), lambda b, i: (0, b * nq + i)),
                  pl.BlockSpec((tq, WIDTH), lambda b, i: (b * nq + i, 0)),
                  pl.BlockSpec((seq, LANES), lambda b, i: (b, 0), **once),
                  pl.BlockSpec((seq, WIDTH), lambda b, i: (b, 0), **once),
                  pl.BlockSpec((seq // kt, WIDTH, kt), lambda b, i: (b, 0, 0), **once)],
        out_specs=pl.BlockSpec((tq, WIDTH), lambda b, i: (b * nq + i, 0)),
        scratch_shapes=[pltpu.VMEM((IDX_HEADS * tq, LANES), BF16), strip16, strip16, strip16,
                        pltpu.VMEM((seq, tq), BF16)]
                       + [pltpu.VMEM((HEAD_DIM, tq), F32) for _ in range(N_HEADS)],
        compiler_params=_params("parallel", "arbitrary"),
        name="dsa_attn",
    )(qi, wt, q, ki, k, vt)


def _mem_kv_kernel(mem_ref, g_ref, wk_ref, wv_ref, gk_ref, k_out, v_out):
    mb = _rms(mem_ref[...], g_ref[...]).astype(BF16)
    kf = _dot(mb, wk_ref[...])
    for h in range(N_HEADS):
        sl = slice(h * HEAD_DIM, (h + 1) * HEAD_DIM)
        k_out[:, sl] = _rms(kf[:, sl], gk_ref[...]).astype(BF16)
    v_out[...] = _dot(mb, wv_ref[...]).astype(BF16)


def _mem_kv(mem2, gain, wk, wv, gk, n_mem):
    n, d = mem2.shape
    full = lambda a: pl.BlockSpec(a.shape, lambda b: (0,) * a.ndim)
    return pl.pallas_call(
        _mem_kv_kernel,
        out_shape=(jax.ShapeDtypeStruct((n, WIDTH), BF16), jax.ShapeDtypeStruct((n, WIDTH), BF16)),
        grid=(n // n_mem,),
        in_specs=[pl.BlockSpec((n_mem, d), lambda b: (b, 0)), full(gain), full(wk), full(wv), full(gk)],
        out_specs=(pl.BlockSpec((n_mem, WIDTH), lambda b: (b, 0)),
                   pl.BlockSpec((n_mem, WIDTH), lambda b: (b, 0))),
        compiler_params=_params("parallel"),
        name="mem_kv",
    )(mem2, gain, wk, wv, gk)


def _xattn_kernel(x_ref, oa_ref, ob_ref, wout_ref, g_ref, wq_ref, gq_ref, km_ref, vm_ref, wo_ref, o_ref):
    x1 = (x_ref[...] + _dot(oa_ref[...].astype(BF16), wout_ref[:WIDTH, :])
          + _dot(ob_ref[...].astype(BF16), wout_ref[WIDTH:, :]))
    hb = _rms(x1, g_ref[...]).astype(BF16)
    qf = _dot(hb, wq_ref[...])
    scale = HEAD_DIM ** -0.5
    outs = []
    for h in range(N_HEADS):
        sl = slice(h * HEAD_DIM, (h + 1) * HEAD_DIM)
        qh = (_rms(qf[:, sl], gq_ref[...]) * scale).astype(BF16)
        logits = _dot_nt(qh, km_ref[:, sl])
        p = jnp.exp(logits - jnp.max(logits, axis=-1, keepdims=True))
        oh = _dot(p.astype(BF16), vm_ref[:, sl])
        outs.append((oh * (1.0 / jnp.sum(p, axis=-1, keepdims=True))).astype(BF16))
    o_ref[...] = x1 + _dot(jnp.concatenate(outs, axis=1), wo_ref[...])


def _xattn(x2, oa, ob, wout, gain, wq, gq, km, vm, wo, seq, n_mem, tm=512):
    n, d = x2.shape
    per_b = seq // tm
    row = lambda w: pl.BlockSpec((tm, w), lambda i: (i, 0))
    full = lambda a: pl.BlockSpec(a.shape, lambda i: (0,) * a.ndim)
    memb = pl.BlockSpec((n_mem, WIDTH), lambda i: (i // per_b, 0))
    return pl.pallas_call(
        _xattn_kernel,
        out_shape=jax.ShapeDtypeStruct((n, d), F32),
        grid=(n // tm,),
        in_specs=[row(d), row(WIDTH), row(WIDTH), full(wout), full(gain), full(wq), full(gq),
                  memb, memb, full(wo)],
        out_specs=row(d),
        compiler_params=_params("parallel"),
        name="xattn",
    )(x2, oa, ob, wout, gain, wq, gq, km, vm, wo)


def _moe_kernel(x_ref, g_ref, wr_hi_ref, wr_lo_ref, br_ref, wgu_ref, wdn_ref, o_ref, hb_ref, gate_ref):
    e = pl.program_id(1)
    lane = lax.broadcasted_iota(I32, gate_ref.shape, 1)
    lane_f = lane.astype(F32)

    @pl.when(e == 0)
    def _():
        x = x_ref[...]
        h = _rms(x, g_ref[...])
        h_hi = h.astype(BF16)
        h_lo = (h - h_hi.astype(F32)).astype(BF16)
        hb_ref[...] = h_hi
        lg = (_dot(h_hi, wr_hi_ref[...]) + _dot(h_lo, wr_hi_ref[...]) + _dot(h_hi, wr_lo_ref[...])
              + br_ref[...])
        first = lambda cond: jnp.min(jnp.where(cond, lane_f, 1e9), axis=-1, keepdims=True)
        gl = jnp.where(lane < N_GROUPS, lg, -jnp.inf)
        gmax = jnp.max(gl, axis=-1, keepdims=True)
        gsel = first(gl == gmax)
        g_w = 1.0 / jnp.sum(jnp.exp(gl - gmax), axis=-1, keepdims=True)
        grp_of_lane = ((lane - N_GROUPS) >> 2).astype(F32)
        in_grp = (lane >= N_GROUPS) & (lane < N_GROUPS + N_EXPERTS) & (grp_of_lane == gsel)
        el = jnp.where(in_grp, lg, -jnp.inf)
        v1 = jnp.max(el, axis=-1, keepdims=True)
        i1 = first(el == v1)
        el2 = jnp.where(lane_f == i1, -jnp.inf, el)
        v2 = jnp.max(el2, axis=-1, keepdims=True)
        i2 = first(el2 == v2)
        r = jnp.exp(v2 - v1)
        w1 = 1.0 / (1.0 + r)
        gate_ref[...] = jnp.where(lane_f == i1, w1 * g_w, jnp.where(lane_f == i2, r * w1 * g_w, 0.0))
        o_ref[...] = x

    ge = jnp.sum(jnp.where(lane == e + N_GROUPS, gate_ref[...], 0.0), axis=-1, keepdims=True)
    gu = _dot(hb_ref[...], wgu_ref[0])
    up = gu[:, :D_EXPERT]
    act = up * jax.nn.sigmoid(up) * gu[:, D_EXPERT:] * ge
    o_ref[...] += _dot(act.astype(BF16), wdn_ref[0])


def _moe(x2, gain, wr_hi, wr_lo, br, wgu, wdn, tm=1024):
    n, d = x2.shape
    full = lambda a: pl.BlockSpec(a.shape, lambda i, e: (0,) * a.ndim)
    return pl.pallas_call(
        _moe_kernel,
        out_shape=jax.ShapeDtypeStruct((n, d), F32),
        grid=(n // tm, N_EXPERTS),
        in_specs=[pl.BlockSpec((tm, d), lambda i, e: (i, 0)), full(gain), full(wr_hi), full(wr_lo), full(br),
                  pl.BlockSpec((1, d, 2 * D_EXPERT), lambda i, e: (e, 0, 0)),
                  pl.BlockSpec((1, D_EXPERT, d), lambda i, e: (e, 0, 0))],
        out_specs=pl.BlockSpec((tm, d), lambda i, e: (i, 0)),
        scratch_shapes=[pltpu.VMEM((tm, d), BF16), pltpu.VMEM((tm, LANES), F32)],
        compiler_params=_params("parallel", "arbitrary"),
        name="moe",
    )(x2, gain, wr_hi, wr_lo, br, wgu, wdn)


def _rope_tables(positions):
    pos = positions.reshape(-1).astype(F32)[:, None]

    def cs(dim):
        inv = ROPE_THETA ** (-jnp.arange(0, dim, 2, dtype=F32) / dim)
        ang = pos * inv
        return jnp.cos(ang), jnp.sin(ang)

    c, s = cs(HEAD_DIM)
    ci, si = cs(IDX_DIM)
    z = jnp.zeros_like(si)
    return (jnp.concatenate([c, c], 1), jnp.concatenate([-s, s], 1),
            jnp.concatenate([ci, ci, ci, ci], 1), jnp.concatenate([-si, z, -si, z], 1),
            jnp.concatenate([z, si, z, si], 1))


def _reorder_in_weight(w):
    a = w[:, :4 * WIDTH]
    o = 4 * WIDTH
    qlat = w[:, o:o + Q_LORA]; o += Q_LORA
    k = w[:, o:o + WIDTH]; o += WIDTH
    v = w[:, o:o + WIDTH]; o += WIDTH
    misc = w[:, o:]
    pad = jnp.zeros((w.shape[0], LANES - misc.shape[1]), w.dtype)
    return jnp.concatenate([a, k, v, qlat, misc, pad], axis=1).astype(BF16)


def _row(v):
    return v.reshape(1, -1).astype(F32)


def kernel(x, mem, positions, norm_mix, w_in, lb_raw, a_gnorm, b_qlat_gain, b_wqb, b_wqidx, b_qnorm,
           b_knorm, b_kidx_norm, w_out, norm_x, norm_mem, x_wq, x_wk, x_wv, x_wo, x_qnorm, x_knorm,
           norm_ffn, w_rg, b_rg, w_re, b_re, w_gu, w_dn):
    bsz, seq, d = x.shape
    n_mem = mem.shape[1]
    depth = w_in.shape[0]
    x2 = x.reshape(bsz * seq, d)
    mem2 = mem.reshape(bsz * n_mem, d)
    tabs = _rope_tables(positions)
    pad_lanes = lambda v: jnp.pad(v, ((0, 0), (0, LANES - v.shape[1])))
    for l in range(depth):
        u = _in_proj(x2, _row(norm_mix[l]), _reorder_in_weight(w_in[l]))
        o_a = _hgrn(u, lb_raw.astype(F32), _row(a_gnorm[l]), l, bsz, seq)
        q, k, vt, qi, ki, wt = _dsa_prep(
            u, tabs, b_wqb[l].astype(BF16), b_wqidx[l].astype(BF16), _row(b_qlat_gain[l]),
            _row(b_qnorm[l]), _row(b_knorm[l]), pad_lanes(_row(b_kidx_norm[l])))
        o_b = _dsa_attn(q, k, vt, qi, ki, wt, bsz, seq)
        km, vm = _mem_kv(mem2, _row(norm_mem[l]), x_wk[l].astype(BF16), x_wv[l].astype(BF16),
                         _row(x_knorm[l]), n_mem)
        x2 = _xattn(x2, o_a, o_b, w_out[l].astype(BF16), _row(norm_x[l]), x_wq[l].astype(BF16),
                    _row(x_qnorm[l]), km, vm, x_wo[l].astype(BF16), seq, n_mem)
        wr = pad_lanes(jnp.concatenate([w_rg[l], w_re[l].reshape(d, N_EXPERTS)], axis=1).astype(F32))
        wr_hi = wr.astype(BF16)
        wr_lo = (wr - wr_hi.astype(F32)).astype(BF16)
        br = pad_lanes(jnp.concatenate([b_rg[l], b_re[l].reshape(-1)]).reshape(1, -1).astype(F32))
        x2 = _moe(x2, _row(norm_ffn[l]), wr_hi, wr_lo, br,
                  w_gu[l].reshape(N_EXPERTS, d, 2 * D_EXPERT).astype(BF16),
                  w_dn[l].reshape(N_EXPERTS, D_EXPERT, d).astype(BF16))
    return x2.reshape(bsz, seq, d)
```

```python
import functools

import numpy as np
import jax
import jax.numpy as jnp
from jax import lax
from jax.experimental import pallas as pl
from jax.experimental.pallas import tpu as pltpu

F32 = jnp.float32
BF16 = jnp.bfloat16
I32 = jnp.int32
I16 = jnp.int16

EPS = 1e-6
ROPE_THETA = 10000.0
CHUNK = 64
CHUNK_SHIFT = 6
SUB = 16
SUB_SHIFT = 4
N_HEADS = 4
HEAD_DIM = 128
WIDTH = N_HEADS * HEAD_DIM
Q_LORA = 256
IDX_HEADS = 16
IDX_DIM = 64
TOPK_MAX = 256
N_GROUPS = 4
EXP_PER_GROUP = 4
N_EXPERTS = N_GROUPS * EXP_PER_GROUP
D_EXPERT = 256
LANES = 128
INT_MIN = -2 ** 31
INT16_MIN = -2 ** 15
PACK = 16
LOG2E = 1.4426950408889634
DSA_KEY_TILE = 512

U_AQ, U_AF, U_AI, U_AG = 0, 512, 1024, 1536
U_K, U_V, U_QLAT, U_MISC = 2048, 2560, 3072, 3328
U_COLS = 3456

VMEM_LIMIT = 56 * 1024 * 1024


def _rms(x, gain):
    return x * lax.rsqrt(jnp.mean(x * x, axis=-1, keepdims=True) + EPS) * gain


def _dot(a, b):
    return jnp.dot(a, b, preferred_element_type=F32)


def _dot_nt(a, b):
    return lax.dot_general(a, b, (((1,), (1,)), ((), ())), preferred_element_type=F32)


def _params(*sem):
    return pltpu.CompilerParams(dimension_semantics=sem, vmem_limit_bytes=VMEM_LIMIT)


def _in_proj_kernel(x_ref, g_ref, w_ref, u_ref, *, col_chunk):
    hb = _rms(x_ref[...], g_ref[...]).astype(BF16)
    for c0 in range(0, U_COLS, col_chunk):
        u_ref[:, c0:c0 + col_chunk] = _dot(hb, w_ref[:, c0:c0 + col_chunk])


def _in_proj(x2, gain, w_bf, tm=256):
    n, d = x2.shape
    return pl.pallas_call(
        functools.partial(_in_proj_kernel, col_chunk=1152),
        out_shape=jax.ShapeDtypeStruct((n, U_COLS), F32),
        grid=(n // tm,),
        in_specs=[pl.BlockSpec((tm, d), lambda i: (i, 0)),
                  pl.BlockSpec((1, d), lambda i: (0, 0)),
                  pl.BlockSpec((d, U_COLS), lambda i: (0, 0))],
        out_specs=pl.BlockSpec((tm, U_COLS), lambda i: (i, 0)),
        compiler_params=_params("parallel"),
        name="in_proj",
    )(x2, gain, w_bf)


def _split3(x):
    hi = x.astype(BF16)
    r1 = x - hi.astype(F32)
    mid = r1.astype(BF16)
    lo = (r1 - mid.astype(F32)).astype(BF16)
    return hi, mid, lo


def _hgrn_kernel(q_ref, f_ref, i_ref, g_ref, lbraw_ref, gn_ref, e_ref, o_ref, st_ref,
                 *, layer, n_chunks, unroll):
    @pl.when(pl.program_id(2) == 0)
    def _():
        st_ref[...] = jnp.zeros_like(st_ref)

    lr = lbraw_ref[...]
    ex = jnp.exp(lr - jnp.max(lr, axis=0, keepdims=True))
    sm = ex / jnp.sum(ex, axis=0, keepdims=True)
    lb = jnp.zeros((1, HEAD_DIM), F32)
    for r in range(1, layer + 1):
        lb = lb + sm[r:r + 1, :]
    log_lb = jnp.log(lb)
    log_1mlb = jnp.log(1.0 - lb)

    row = lax.broadcasted_iota(I32, (CHUNK, HEAD_DIM), 0)
    col = lax.broadcasted_iota(I32, (CHUNK, HEAD_DIM), 1)
    row_l = row & (SUB - 1)
    row_b = row >> SUB_SHIFT
    col_b = col >> SUB_SHIFT
    tri = (lax.broadcasted_iota(I32, (CHUNK, CHUNK), 0)
           >= lax.broadcasted_iota(I32, (CHUNK, CHUNK), 1)).astype(BF16)
    n_sub = CHUNK // SUB
    zeros_c = jnp.zeros((CHUNK, HEAD_DIM), F32)

    def sub_bcast(t, s):
        t4 = t.reshape(n_sub, SUB, HEAD_DIM)
        return jnp.broadcast_to(t4[:, s:s + 1, :], (n_sub, SUB, HEAD_DIM)).reshape(CHUNK, HEAD_DIM)

    def gates(z):
        ls = jnp.minimum(z, 0.0) - jnp.log(1.0 + jnp.exp(-jnp.abs(z)))
        if layer == 0:
            return ls, (ls - z) * LOG2E
        cc = log_1mlb + ls
        log_f = jnp.maximum(log_lb, cc) + jnp.log(1.0 + jnp.exp(-jnp.abs(log_lb - cc)))
        return log_f, (cc - z) * LOG2E

    def diag_terms(q, b2, c2):
        pieces = []
        for s in range(SUB):
            d = jnp.where(row_l >= s, b2 - sub_bcast(c2, s), -jnp.inf)
            pieces.append((q * jnp.exp2(d)).astype(BF16))
        return jnp.concatenate(pieces, axis=1)

    def below_keys(b2, c2):
        kts = []
        for i in range(1, n_sub):
            r_i = b2[i * SUB:i * SUB + 1, :]
            kts.append(jnp.exp2(jnp.where(row < i * SUB, r_i - c2, -jnp.inf)))
            kts.append(zeros_c)
        return jnp.concatenate(kts, axis=0).astype(BF16)

    def below_scores(r):
        a = jnp.zeros((CHUNK, HEAD_DIM), F32)
        for i in range(1, n_sub):
            a = a + jnp.where(row_b == i, r[:, (i - 1) * LANES:i * LANES], 0.0)
        return a

    def chunks(it, carry):
        cs = range(unroll)
        rows = [pl.ds(pl.multiple_of((it * unroll + c) * CHUNK, CHUNK), CHUNK) for c in cs]
        qr = [q_ref[r, :] for r in rows]
        v = [i_ref[r, :] for r in rows]
        q = [x * jax.nn.sigmoid(x) for x in qr]
        gt = [gates(f_ref[r, :]) for r in rows]
        b2 = []
        for c in cs:
            hi, mid, lo = _split3(gt[c][0])
            b2.append((_dot(tri, hi) + _dot(tri, mid) + _dot(tri, lo)) * LOG2E)
        c2 = [b2[c] - gt[c][1] for c in cs]
        b_last = [x[CHUNK - 1:CHUNK, :] for x in b2]
        a2 = [_dot(diag_terms(q[c], b2[c], c2[c]), e_ref[...]) for c in cs]
        rr = [_dot_nt((q[c] * jnp.exp2(b2[c] - sub_bcast(b2[c], 0))).astype(BF16), below_keys(b2[c], c2[c]))
              for c in cs]
        upd = [_dot(v[c].T.astype(BF16), jnp.exp2(b_last[c] - c2[c]).astype(BF16)) for c in cs]
        qe = [(q[c] * jnp.exp2(b2[c])).astype(BF16) for c in cs]
        st = st_ref[...]
        o_inter = []
        for c in cs:
            o_inter.append(_dot_nt(qe[c], st.astype(BF16)))
            st = st * jnp.exp2(b_last[c]) + upd[c]
        st_ref[...] = st
        a = [jnp.where(col_b == row_b, a2[c], 0.0) + below_scores(rr[c]) for c in cs]
        o_intra = [_dot(a[c][:, :CHUNK].astype(BF16), v[c].astype(BF16)) for c in cs]
        for c in cs:
            g = g_ref[rows[c], :]
            o_ref[rows[c], :] = _rms(o_inter[c] + o_intra[c], gn_ref[...]) * (g * jax.nn.sigmoid(g))
        return carry

    lax.fori_loop(0, n_chunks // unroll, chunks, 0)


def _hgrn_selector():
    e = np.zeros((SUB * HEAD_DIM, LANES), np.float32)
    s_of_row = np.arange(SUB * HEAD_DIM) // HEAD_DIM
    cols = np.arange(LANES)
    e[:, :] = ((cols[None, :] % SUB) == s_of_row[:, None]) & (cols[None, :] < CHUNK)
    return jnp.asarray(e, BF16)


def _hgrn(u, lb_raw, gn, layer, bsz, seq, tb=512):
    n = bsz * seq
    nb = seq // tb
    blk = lambda k: pl.BlockSpec((tb, HEAD_DIM), lambda b, h, j, k=k: (b * nb + j, h + N_HEADS * k))
    n_layers = lb_raw.shape[0]
    return pl.pallas_call(
        functools.partial(_hgrn_kernel, layer=layer, n_chunks=tb // CHUNK, unroll=4),
        out_shape=jax.ShapeDtypeStruct((n, WIDTH), F32),
        grid=(bsz, N_HEADS, nb),
        in_specs=[blk(0), blk(1), blk(2), blk(3),
                  pl.BlockSpec((n_layers, HEAD_DIM), lambda b, h, j: (0, h)),
                  pl.BlockSpec((1, HEAD_DIM), lambda b, h, j: (0, 0)),
                  pl.BlockSpec((SUB * HEAD_DIM, LANES), lambda b, h, j: (0, 0))],
        out_specs=pl.BlockSpec((tb, HEAD_DIM), lambda b, h, j: (b * nb + j, h)),
        scratch_shapes=[pltpu.VMEM((HEAD_DIM, HEAD_DIM), F32)],
        compiler_params=_params("parallel", "parallel", "arbitrary"),
        name="hgrn",
    )(u, u, u, u, lb_raw, gn, _hgrn_selector())


def _rope128(x, cos, sin_signed):
    return x * cos + pltpu.roll(x, 64, 1) * sin_signed


def _rope64(x, cos, sin_a, sin_b):
    return x * cos + pltpu.roll(x, 96, 1) * sin_a + pltpu.roll(x, 32, 1) * sin_b


def _dsa_prep_kernel(k_ref, v_ref, ql_ref, misc_ref, cm_ref, sm_ref, ci_ref, sia_ref, sib_ref,
                     wqb_ref, wqi_ref, gql_ref, gqn_ref, gkn_ref, gki_ref,
                     q_out, k_out, vt_out, qi_out, ki_out, wt_out):
    cm, sm = cm_ref[...], sm_ref[...]
    ci, sia, sib = ci_ref[...], sia_ref[...], sib_ref[...]
    cb = _rms(ql_ref[...], gql_ref[...]).astype(BF16)
    qf = _dot(cb, wqb_ref[...])
    scale = HEAD_DIM ** -0.5
    for h in range(N_HEADS):
        sl = slice(h * HEAD_DIM, (h + 1) * HEAD_DIM)
        qh = _rope128(_rms(qf[:, sl], gqn_ref[...]), cm, sm)
        q_out[:, sl] = (qh * (scale * LOG2E)).astype(BF16)
        kh = _rope128(_rms(k_ref[:, sl], gkn_ref[...]), cm, sm)
        k_out[:, sl] = kh.astype(BF16)
    vt_out[0] = v_ref[...].T.astype(BF16)
    qi = _dot(cb, wqi_ref[...])
    for p in range(IDX_HEADS * IDX_DIM // LANES):
        sl = slice(p * LANES, (p + 1) * LANES)
        qi_out[:, sl] = _rope64(qi[:, sl], ci, sia, sib).astype(BF16)
    misc = misc_ref[...]
    lane = lax.broadcasted_iota(I32, misc.shape, 1)
    kraw = jnp.where(lane < IDX_DIM, misc, 0.0)
    kn = kraw * lax.rsqrt(jnp.sum(kraw * kraw, axis=-1, keepdims=True) * (1.0 / IDX_DIM) + EPS)
    kn = _rope64(kn * gki_ref[...], ci, sia, sib)
    ki_out[...] = (kn + pltpu.roll(kn, IDX_DIM, 1)).astype(BF16)
    wt = (misc * (IDX_HEADS ** -0.5 * IDX_DIM ** -0.5)).T
    wt_out[...] = wt[IDX_DIM:IDX_DIM + IDX_HEADS, :]


def _dsa_prep(u, tabs, wqb, wqi, gql, gqn, gkn, gki, tm=DSA_KEY_TILE):
    n = u.shape[0]
    nt = n // tm
    ublk = lambda width, col: pl.BlockSpec((tm, width), lambda i: (i, col // width))
    tab = pl.BlockSpec((tm, LANES), lambda i: (i, 0))
    full = lambda a: pl.BlockSpec(a.shape, lambda i: (0,) * a.ndim)
    return pl.pallas_call(
        _dsa_prep_kernel,
        out_shape=(jax.ShapeDtypeStruct((n, WIDTH), BF16),
                   jax.ShapeDtypeStruct((n, WIDTH), BF16),
                   jax.ShapeDtypeStruct((nt, WIDTH, tm), BF16),
                   jax.ShapeDtypeStruct((n, IDX_HEADS * IDX_DIM), BF16),
                   jax.ShapeDtypeStruct((n, LANES), BF16),
                   jax.ShapeDtypeStruct((IDX_HEADS, n), F32)),
        grid=(nt,),
        in_specs=[ublk(WIDTH, U_K), ublk(WIDTH, U_V), ublk(Q_LORA, U_QLAT), ublk(LANES, U_MISC),
                  tab, tab, tab, tab, tab,
                  full(wqb), full(wqi), full(gql), full(gqn), full(gkn), full(gki)],
        out_specs=(pl.BlockSpec((tm, WIDTH), lambda i: (i, 0)),
                   pl.BlockSpec((tm, WIDTH), lambda i: (i, 0)),
                   pl.BlockSpec((1, WIDTH, tm), lambda i: (i, 0, 0)),
                   pl.BlockSpec((tm, IDX_HEADS * IDX_DIM), lambda i: (i, 0)),
                   pl.BlockSpec((tm, LANES), lambda i: (i, 0)),
                   pl.BlockSpec((IDX_HEADS, tm), lambda i: (0, i))),
        compiler_params=_params("parallel"),
        name="dsa_prep",
    )(u, u, u, u, *tabs, wqb, wqi, gql, gqn, gkn, gki)


def _sortable(x):
    bits = lax.bitcast_convert_type(x, I32)
    return bits ^ ((bits >> 31) & 0x7FFFFFFF)


def _dsa_attn_kernel(qi_ref, wt_ref, q_ref, ki_ref, k_ref, vt_ref, o_ref,
                     qm_ref, hi_ref, lo_ref, kb_ref, bias_ref, acc0, acc1, acc2, acc3, *, tq, kt, top_k):
    acc_refs = (acc0, acc1, acc2, acc3)
    q0 = pl.program_id(1) * tq
    n_tiles = (q0 + tq + kt - 1) // kt

    lane_q = lax.broadcasted_iota(I32, (tq, LANES), 1)
    for h in range(IDX_HEADS):
        src = qi_ref[:, (h // 2) * LANES:(h // 2 + 1) * LANES].astype(F32)
        keep = (lane_q < IDX_DIM) if h % 2 == 0 else (lane_q >= IDX_DIM)
        qm_ref[h * tq:(h + 1) * tq, :] = jnp.where(keep, src, 0.0).astype(BF16)

    qpos = q0 + lax.broadcasted_iota(I32, (1, tq), 1)
    limit = ((qpos >> CHUNK_SHIFT) + 1) << CHUNK_SHIFT
    sub = 128
    key_iota = lax.broadcasted_iota(I32, (sub, tq), 0)

    def tile_rows(jt):
        return pl.ds(pl.multiple_of(jt * kt, kt), kt)

    def score_tile(jt, carry):
        for part in range(kt // sub):
            base = jt * kt + part * sub
            rows = pl.ds(pl.multiple_of(base, sub), sub)
            ki = ki_ref[rows, :]
            acc = jnp.zeros((sub, tq), F32)
            for h in range(IDX_HEADS):
                x = _dot_nt(ki, qm_ref[h * tq:(h + 1) * tq, :])
                acc = acc + wt_ref[h:h + 1, :] * jnp.maximum(x, 0.0)
            key = jnp.where(base + key_iota < limit, _sortable(acc), INT_MIN)
            hi_ref[rows, :] = (key >> 16).astype(I16)
            lo_ref[rows, :] = ((key & 0xFFFF) - 32768).astype(I16)
        return carry

    lax.fori_loop(0, n_tiles, score_tile, 0)

    one, zero = jnp.ones((), BF16), jnp.zeros((), BF16)
    neg_inf = jnp.full((), -jnp.inf, BF16)
    n_rows = (n_tiles * kt).astype(F32)

    def count16(ref, trial, strict):
        t16 = jnp.broadcast_to(trial.astype(I16), (PACK, tq))

        def body(jt, c):
            x = ref[tile_rows(jt), :]
            parts = [jnp.zeros((PACK, tq), BF16) for _ in range(4)]
            for r in range(kt // PACK):
                xr = x[r * PACK:(r + 1) * PACK, :]
                parts[r % 4] = parts[r % 4] + jnp.where((xr > t16) if strict else (xr >= t16), one, zero)
            return c + ((parts[0] + parts[1]) + (parts[2] + parts[3])).astype(F32)
        c = lax.fori_loop(0, n_tiles, body, jnp.zeros((PACK, tq), F32))
        return jnp.sum(c, axis=0, keepdims=True)

    def bisect16(ref, want):
        def step(it, carry):
            lo, n_lo = carry
            trial = lo + lax.shift_left(jnp.int32(1), 15 - it)
            n = count16(ref, trial, False)
            ok = n >= want
            return jnp.where(ok, trial, lo), jnp.where(ok, n, n_lo)
        return lax.fori_loop(0, 16, step, (jnp.full((1, tq), INT16_MIN, I32), jnp.full((1, tq), n_rows, F32)))

    tau_hi, n_hi_ge = bisect16(hi_ref, float(top_k))
    n_hi_gt = count16(hi_ref, tau_hi, True)
    want_lo = top_k - n_hi_gt
    tau_hi16 = jnp.broadcast_to(tau_hi.astype(I16), (kt, tq))

    def bucket_tile(jt, carry):
        rows = tile_rows(jt)
        kb_ref[rows, :] = jnp.where(hi_ref[rows, :] == tau_hi16, lo_ref[rows, :],
                                    jnp.full((), INT16_MIN, I16))
        return carry

    lax.fori_loop(0, n_tiles, bucket_tile, 0)
    tau_lo, n_kb_ge = bisect16(kb_ref, want_lo)
    live = (tau_hi > INT16_MIN) | (tau_lo > INT16_MIN)
    n_tie_ge = jnp.where(tau_lo == INT16_MIN, n_hi_ge - n_hi_gt, n_kb_ge)
    excess = jnp.max(jnp.where(live, n_tie_ge - want_lo, 0.0))

    @pl.when(excess <= 0)
    def _():
        tau_lo16 = jnp.broadcast_to(jnp.where(live, tau_lo, -INT16_MIN - 1).astype(I16), (kt, tq))

        def body(jt, carry):
            rows = tile_rows(jt)
            h16 = hi_ref[rows, :]
            in_bucket = jnp.where(lo_ref[rows, :] >= tau_lo16, zero, neg_inf)
            bias_ref[rows, :] = jnp.where(h16 > tau_hi16, zero, jnp.where(h16 == tau_hi16, in_bucket, neg_inf))
            return carry
        lax.fori_loop(0, n_tiles, body, 0)

    @pl.when(excess > 0)
    def _():
        need = want_lo - count16(kb_ref, tau_lo, True)
        live_f = jnp.where(live, 1.0, 0.0)
        tri = (lax.broadcasted_iota(I32, (kt, kt), 0)
               >= lax.broadcasted_iota(I32, (kt, kt), 1)).astype(BF16)

        def body(jt, seen):
            rows = tile_rows(jt)
            h32 = hi_ref[rows, :].astype(I32)
            l32 = lo_ref[rows, :].astype(I32)
            same_hi = h32 == tau_hi
            gt = jnp.where(h32 > tau_hi, 1.0, jnp.where(same_hi, jnp.where(l32 > tau_lo, 1.0, 0.0), 0.0))
            eq = jnp.where(same_hi, jnp.where(l32 == tau_lo, 1.0, 0.0), 0.0)
            rank = _dot(tri, eq.astype(BF16)) + seen
            sel = gt + eq * jnp.where(rank <= need, live_f, 0.0)
            bias_ref[rows, :] = jnp.where(sel > 0.0, 0.0, -jnp.inf).astype(BF16)
            return rank[kt - 1:kt, :]
        lax.fori_loop(0, n_tiles, body, jnp.zeros((1, tq), F32))

    for a in acc_refs:
        a[...] = jnp.zeros_like(a)
    heads = [slice(h * HEAD_DIM, (h + 1) * HEAD_DIM) for h in range(N_HEADS)]

    def attn_tile(jt, carry):
        ms, ls = carry
        rows = tile_rows(jt)
        bias = bias_ref[rows, :].astype(F32)
        lts = [_dot_nt(k_ref[rows, sl], q_ref[:, sl]) + bias for sl in heads]
        new_ms, new_ls, alphas, ps = [], [], [], []
        for h in range(N_HEADS):
            m_new = jnp.maximum(ms[h], jnp.max(lts[h], axis=0, keepdims=True))
            m_safe = jnp.where(m_new == -jnp.inf, 0.0, m_new)
            p = jnp.exp2(lts[h] - m_safe)
            alpha = jnp.exp2(ms[h] - m_safe)
            new_ls.append(alpha * ls[h] + jnp.sum(p, axis=0, keepdims=True))
            new_ms.append(m_new)
            alphas.append(alpha)
            ps.append(p.astype(BF16))
        for h in range(N_HEADS):
            acc_refs[h][...] = acc_refs[h][...] * alphas[h] + _dot(vt_ref[jt, heads[h], :], ps[h])
        return tuple(new_ms), tuple(new_ls)

    init = (tuple(jnp.full((1, tq), -jnp.inf, F32) for _ in range(N_HEADS)),
            tuple(jnp.zeros((1, tq), F32) for _ in range(N_HEADS)))
    _, ls = lax.fori_loop(0, n_tiles, attn_tile, init)
    for h in range(N_HEADS):
        out_t = acc_refs[h][...] * (1.0 / ls[h])
        o_ref[:, heads[h]] = out_t.T


def _dsa_attn(q, k, vt, qi, ki, wt, bsz, seq, tq=256, kt=DSA_KEY_TILE):
    n = bsz * seq
    nq = seq // tq
    top_k = min(TOPK_MAX, seq // 4)
    once = dict(pipeline_mode=pl.Buffered(1))
    strip16 = pltpu.VMEM((seq, tq), I16)
    return pl.pallas_call(
        functools.partial(_dsa_attn_kernel, tq=tq, kt=kt, top_k=top_k),
        out_shape=jax.ShapeDtypeStruct((n, WIDTH), F32),
        grid=(bsz, nq),
        in_specs=[pl.BlockSpec((tq, IDX_HEADS * IDX_DIM), lambda b, i: (b * nq + i, 0)),
                  pl.BlockSpec((IDX_HEADS, tq), lambda b, i: (0, b * nq + i)),
                  pl.BlockSpec((tq, WIDTH), lambda b, i: (b * nq + i, 0)),
                  pl.BlockSpec((seq, LANES), lambda b, i: (b, 0), **once),
                  pl.BlockSpec((seq, WIDTH), lambda b, i: (b, 0), **once),
                  pl.BlockSpec((seq // kt, WIDTH, kt), lambda b, i: (b, 0, 0), **once)],
        out_specs=pl.BlockSpec((tq, WIDTH), lambda b, i: (b * nq + i, 0)),
        scratch_shapes=[pltpu.VMEM((IDX_HEADS * tq, LANES), BF16), strip16, strip16, strip16,
                        pltpu.VMEM((seq, tq), BF16)]
                       + [pltpu.VMEM((HEAD_DIM, tq), F32) for _ in range(N_HEADS)],
        compiler_params=_params("parallel", "arbitrary"),
        name="dsa_attn",
    )(qi, wt, q, ki, k, vt)


def _mem_kv_kernel(mem_ref, g_ref, wk_ref, wv_ref, gk_ref, k_out, v_out):
    mb = _rms(mem_ref[...], g_ref[...]).astype(BF16)
    kf = _dot(mb, wk_ref[...])
    for h in range(N_HEADS):
        sl = slice(h * HEAD_DIM, (h + 1) * HEAD_DIM)
        k_out[:, sl] = _rms(kf[:, sl], gk_ref[...]).astype(BF16)
    v_out[...] = _dot(mb, wv_ref[...]).astype(BF16)


def _mem_kv(mem2, gain, wk, wv, gk, n_mem):
    n, d = mem2.shape
    full = lambda a: pl.BlockSpec(a.shape, lambda b: (0,) * a.ndim)
    return pl.pallas_call(
        _mem_kv_kernel,
        out_shape=(jax.ShapeDtypeStruct((n, WIDTH), BF16), jax.ShapeDtypeStruct((n, WIDTH), BF16)),
        grid=(n // n_mem,),
        in_specs=[pl.BlockSpec((n_mem, d), lambda b: (b, 0)), full(gain), full(wk), full(wv), full(gk)],
        out_specs=(pl.BlockSpec((n_mem, WIDTH), lambda b: (b, 0)),
                   pl.BlockSpec((n_mem, WIDTH), lambda b: (b, 0))),
        compiler_params=_params("parallel"),
        name="mem_kv",
    )(mem2, gain, wk, wv, gk)


def _xattn_kernel(x_ref, oa_ref, ob_ref, wout_ref, g_ref, wq_ref, gq_ref, km_ref, vm_ref, wo_ref, o_ref):
    x1 = (x_ref[...] + _dot(oa_ref[...].astype(BF16), wout_ref[:WIDTH, :])
          + _dot(ob_ref[...].astype(BF16), wout_ref[WIDTH:, :]))
    hb = _rms(x1, g_ref[...]).astype(BF16)
    qf = _dot(hb, wq_ref[...])
    scale = HEAD_DIM ** -0.5
    outs = []
    for h in range(N_HEADS):
        sl = slice(h * HEAD_DIM, (h + 1) * HEAD_DIM)
        qh = (_rms(qf[:, sl], gq_ref[...]) * scale).astype(BF16)
        logits = _dot_nt(qh, km_ref[:, sl])
        p = jnp.exp(logits - jnp.max(logits, axis=-1, keepdims=True))
        oh = _dot(p.astype(BF16), vm_ref[:, sl])
        outs.append((oh * (1.0 / jnp.sum(p, axis=-1, keepdims=True))).astype(BF16))
    o_ref[...] = x1 + _dot(jnp.concatenate(outs, axis=1), wo_ref[...])


def _xattn(x2, oa, ob, wout, gain, wq, gq, km, vm, wo, seq, n_mem, tm=512):
    n, d = x2.shape
    per_b = seq // tm
    row = lambda w: pl.BlockSpec((tm, w), lambda i: (i, 0))
    full = lambda a: pl.BlockSpec(a.shape, lambda i: (0,) * a.ndim)
    memb = pl.BlockSpec((n_mem, WIDTH), lambda i: (i // per_b, 0))
    return pl.pallas_call(
        _xattn_kernel,
        out_shape=jax.ShapeDtypeStruct((n, d), F32),
        grid=(n // tm,),
        in_specs=[row(d), row(WIDTH), row(WIDTH), full(wout), full(gain), full(wq), full(gq),
                  memb, memb, full(wo)],
        out_specs=row(d),
        compiler_params=_params("parallel"),
        name="xattn",
    )(x2, oa, ob, wout, gain, wq, gq, km, vm, wo)


def _moe_kernel(x_ref, g_ref, wr_hi_ref, wr_lo_ref, br_ref, wgu_ref, wdn_ref, o_ref, hs_ref, gs_ref, ys_ref,
                *, tm, sub):
    lane = lax.broadcasted_iota(I32, (tm, LANES), 1)
    lane_f = lane.astype(F32)
    x = x_ref[...]
    h = _rms(x, g_ref[...])
    h_hi = h.astype(BF16)
    h_lo = (h - h_hi.astype(F32)).astype(BF16)
    lg = (_dot(h_hi, wr_hi_ref[...]) + _dot(h_lo, wr_hi_ref[...]) + _dot(h_hi, wr_lo_ref[...])
          + br_ref[...])
    first = lambda cond: jnp.min(jnp.where(cond, lane_f, 1e9), axis=-1, keepdims=True)
    gl = jnp.where(lane < N_GROUPS, lg, -jnp.inf)
    gmax = jnp.max(gl, axis=-1, keepdims=True)
    gsel = first(gl == gmax)
    g_w = 1.0 / jnp.sum(jnp.exp(gl - gmax), axis=-1, keepdims=True)
    grp_of_lane = ((lane - N_GROUPS) >> 2).astype(F32)
    in_grp = (lane >= N_GROUPS) & (lane < N_GROUPS + N_EXPERTS) & (grp_of_lane == gsel)
    el = jnp.where(in_grp, lg, -jnp.inf)
    v1 = jnp.max(el, axis=-1, keepdims=True)
    i1 = first(el == v1)
    el2 = jnp.where(lane_f == i1, -jnp.inf, el)
    v2 = jnp.max(el2, axis=-1, keepdims=True)
    i2 = first(el2 == v2)
    r = jnp.exp(v2 - v1)
    w1 = 1.0 / (1.0 + r)
    gate = jnp.where(lane_f == i1, w1 * g_w, jnp.where(lane_f == i2, r * w1 * g_w, 0.0))

    onehot = jnp.where(lane_f == gsel, 1.0, 0.0)
    t_row = lax.broadcasted_iota(I32, (tm, tm), 0)
    t_col = lax.broadcasted_iota(I32, (tm, tm), 1)
    before = _dot(jnp.where(t_row > t_col, 1.0, 0.0).astype(BF16), onehot.astype(BF16))
    counts = jnp.sum(onehot, axis=0, keepdims=True)
    lane1 = lax.broadcasted_iota(I32, (1, LANES), 1)
    ends = []
    run = jnp.zeros((1, 1), F32)
    for g in range(N_GROUPS - 1):
        run = run + jnp.sum(jnp.where(lane1 == g, counts, 0.0), axis=-1, keepdims=True)
        ends.append(run)
    start_of = sum(jnp.where(lane1 == g + 1, ends[g], 0.0) for g in range(N_GROUPS - 1))
    pos = jnp.sum(onehot * (before + start_of), axis=-1, keepdims=True)
    to_sorted_t = jnp.where(pos == t_col.astype(F32), 1.0, 0.0)
    to_sorted = to_sorted_t.T.astype(BF16)
    hs_ref[...] = _dot(to_sorted, h_hi).astype(BF16)
    g_hi, g_mid, g_lo = _split3(gate)
    gs_ref[...] = _dot(to_sorted, g_hi) + _dot(to_sorted, g_mid) + _dot(to_sorted, g_lo)

    bounds = [e_[0, 0].astype(I32) for e_ in ends]
    lane_s = lax.broadcasted_iota(I32, (sub, LANES), 1)

    def slab(j, carry):
        r0 = j * sub
        rows = pl.ds(pl.multiple_of(r0, sub), sub)
        g_first = sum((b <= r0).astype(I32) for b in bounds)
        g_last = sum((b <= r0 + sub - 1).astype(I32) for b in bounds)
        hs = hs_ref[rows, :]
        gs = gs_ref[rows, :]
        ys_ref[rows, :] = jnp.zeros((sub, ys_ref.shape[1]), F32)

        def group(g, c):
            gu = _dot(hs, wgu_ref[g])
            acts = []
            for e in range(EXP_PER_GROUP):
                ge = jnp.sum(jnp.where(lane_s == g * EXP_PER_GROUP + e + N_GROUPS, gs, 0.0), axis=-1, keepdims=True)
                up = gu[:, 2 * e * D_EXPERT:(2 * e + 1) * D_EXPERT]
                acts.append((up * jax.nn.sigmoid(up) * gu[:, (2 * e + 1) * D_EXPERT:(2 * e + 2) * D_EXPERT] * ge).astype(BF16))
            ys_ref[rows, :] += _dot(jnp.concatenate(acts, axis=1), wdn_ref[g])
            return c

        lax.fori_loop(g_first, g_last + 1, group, 0)
        return carry

    lax.fori_loop(0, tm // sub, slab, 0)
    o_ref[...] = x + _dot(to_sorted_t.astype(BF16), ys_ref[...].astype(BF16))


def _moe(x2, gain, wr_hi, wr_lo, br, wgu, wdn, tm=512, sub=128):
    n, d = x2.shape
    full = lambda a: pl.BlockSpec(a.shape, lambda i: (0,) * a.ndim)
    once = lambda a: pl.BlockSpec(a.shape, lambda i: (0,) * a.ndim, pipeline_mode=pl.Buffered(1))
    return pl.pallas_call(
        functools.partial(_moe_kernel, tm=tm, sub=sub),
        out_shape=jax.ShapeDtypeStruct((n, d), F32),
        grid=(n // tm,),
        in_specs=[pl.BlockSpec((tm, d), lambda i: (i, 0)), full(gain), full(wr_hi), full(wr_lo), full(br),
                  once(wgu), once(wdn)],
        out_specs=pl.BlockSpec((tm, d), lambda i: (i, 0)),
        scratch_shapes=[pltpu.VMEM((tm, d), BF16), pltpu.VMEM((tm, LANES), F32), pltpu.VMEM((tm, d), F32)],
        compiler_params=_params("parallel"),
        name="moe",
    )(x2, gain, wr_hi, wr_lo, br, wgu, wdn)


def _rope_tables(positions):
    pos = positions.reshape(-1).astype(F32)[:, None]

    def cs(dim):
        inv = ROPE_THETA ** (-jnp.arange(0, dim, 2, dtype=F32) / dim)
        ang = pos * inv
        return jnp.cos(ang), jnp.sin(ang)

    c, s = cs(HEAD_DIM)
    ci, si = cs(IDX_DIM)
    z = jnp.zeros_like(si)
    return (jnp.concatenate([c, c], 1), jnp.concatenate([-s, s], 1),
            jnp.concatenate([ci, ci, ci, ci], 1), jnp.concatenate([-si, z, -si, z], 1),
            jnp.concatenate([z, si, z, si], 1))


def _reorder_in_weight(w):
    a = w[:, :4 * WIDTH]
    o = 4 * WIDTH
    qlat = w[:, o:o + Q_LORA]; o += Q_LORA
    k = w[:, o:o + WIDTH]; o += WIDTH
    v = w[:, o:o + WIDTH]; o += WIDTH
    misc = w[:, o:]
    pad = jnp.zeros((w.shape[0], LANES - misc.shape[1]), w.dtype)
    return jnp.concatenate([a, k, v, qlat, misc, pad], axis=1).astype(BF16)


def _row(v):
    return v.reshape(1, -1).astype(F32)


def kernel(x, mem, positions, norm_mix, w_in, lb_raw, a_gnorm, b_qlat_gain, b_wqb, b_wqidx, b_qnorm,
           b_knorm, b_kidx_norm, w_out, norm_x, norm_mem, x_wq, x_wk, x_wv, x_wo, x_qnorm, x_knorm,
           norm_ffn, w_rg, b_rg, w_re, b_re, w_gu, w_dn):
    bsz, seq, d = x.shape
    n_mem = mem.shape[1]
    depth = w_in.shape[0]
    x2 = x.reshape(bsz * seq, d)
    mem2 = mem.reshape(bsz * n_mem, d)
    tabs = _rope_tables(positions)
    pad_lanes = lambda v: jnp.pad(v, ((0, 0), (0, LANES - v.shape[1])))
    for l in range(depth):
        u = _in_proj(x2, _row(norm_mix[l]), _reorder_in_weight(w_in[l]))
        o_a = _hgrn(u, lb_raw.astype(F32), _row(a_gnorm[l]), l, bsz, seq)
        q, k, vt, qi, ki, wt = _dsa_prep(
            u, tabs, b_wqb[l].astype(BF16), b_wqidx[l].astype(BF16), _row(b_qlat_gain[l]),
            _row(b_qnorm[l]), _row(b_knorm[l]), pad_lanes(_row(b_kidx_norm[l])))
        o_b = _dsa_attn(q, k, vt, qi, ki, wt, bsz, seq)
        km, vm = _mem_kv(mem2, _row(norm_mem[l]), x_wk[l].astype(BF16), x_wv[l].astype(BF16),
                         _row(x_knorm[l]), n_mem)
        x2 = _xattn(x2, o_a, o_b, w_out[l].astype(BF16), _row(norm_x[l]), x_wq[l].astype(BF16),
                    _row(x_qnorm[l]), km, vm, x_wo[l].astype(BF16), seq, n_mem)
        wr = pad_lanes(jnp.concatenate([w_rg[l], w_re[l].reshape(d, N_EXPERTS)], axis=1).astype(F32))
        wr_hi = wr.astype(BF16)
        wr_lo = (wr - wr_hi.astype(F32)).astype(BF16)
        br = pad_lanes(jnp.concatenate([b_rg[l], b_re[l].reshape(-1)]).reshape(1, -1).astype(F32))
        x2 = _moe(x2, _row(norm_ffn[l]), wr_hi, wr_lo, br,
                  w_gu[l].transpose(0, 2, 1, 3).reshape(N_GROUPS, d, EXP_PER_GROUP * 2 * D_EXPERT).astype(BF16),
                  w_dn[l].reshape(N_GROUPS, EXP_PER_GROUP * D_EXPERT, d).astype(BF16))
    return x2.reshape(bsz, seq, d)
```

```python
import functools

import numpy as np
import jax
import jax.numpy as jnp
from jax import lax
from jax.experimental import pallas as pl
from jax.experimental.pallas import tpu as pltpu

F32 = jnp.float32
BF16 = jnp.bfloat16
I32 = jnp.int32
I16 = jnp.int16

EPS = 1e-6
ROPE_THETA = 10000.0
CHUNK = 64
CHUNK_SHIFT = 6
SUB = 16
SUB_SHIFT = 4
N_HEADS = 4
HEAD_DIM = 128
WIDTH = N_HEADS * HEAD_DIM
Q_LORA = 256
IDX_HEADS = 16
IDX_DIM = 64
TOPK_MAX = 256
N_GROUPS = 4
EXP_PER_GROUP = 4
N_EXPERTS = N_GROUPS * EXP_PER_GROUP
D_EXPERT = 256
LANES = 128
INT_MIN = -2 ** 31
INT16_MIN = -2 ** 15
PACK = 16
LOG2E = 1.4426950408889634
VT_ROWS = HEAD_DIM + PACK
DSA_KEY_TILE = 512

U_AQ, U_AF, U_AI, U_AG = 0, 512, 1024, 1536
U_K, U_V, U_QLAT, U_MISC = 2048, 2560, 3072, 3328
U_COLS = 3456

VMEM_LIMIT = 56 * 1024 * 1024


def _rms(x, gain):
    return x * lax.rsqrt(jnp.mean(x * x, axis=-1, keepdims=True) + EPS) * gain


def _dot(a, b):
    return jnp.dot(a, b, preferred_element_type=F32)


def _dot_nt(a, b):
    return lax.dot_general(a, b, (((1,), (1,)), ((), ())), preferred_element_type=F32)


def _params(*sem):
    return pltpu.CompilerParams(dimension_semantics=sem, vmem_limit_bytes=VMEM_LIMIT)


def _in_proj_kernel(x_ref, g_ref, w_ref, u_ref, *, col_chunk):
    hb = _rms(x_ref[...], g_ref[...]).astype(BF16)
    for c0 in range(0, U_COLS, col_chunk):
        u_ref[:, c0:c0 + col_chunk] = _dot(hb, w_ref[:, c0:c0 + col_chunk])


def _in_proj(x2, gain, w_bf, tm=256):
    n, d = x2.shape
    return pl.pallas_call(
        functools.partial(_in_proj_kernel, col_chunk=1152),
        out_shape=jax.ShapeDtypeStruct((n, U_COLS), F32),
        grid=(n // tm,),
        in_specs=[pl.BlockSpec((tm, d), lambda i: (i, 0)),
                  pl.BlockSpec((1, d), lambda i: (0, 0)),
                  pl.BlockSpec((d, U_COLS), lambda i: (0, 0))],
        out_specs=pl.BlockSpec((tm, U_COLS), lambda i: (i, 0)),
        compiler_params=_params("parallel"),
        name="in_proj",
    )(x2, gain, w_bf)


def _split3(x):
    hi = x.astype(BF16)
    r1 = x - hi.astype(F32)
    mid = r1.astype(BF16)
    lo = (r1 - mid.astype(F32)).astype(BF16)
    return hi, mid, lo


def _hgrn_kernel(q_ref, f_ref, i_ref, g_ref, lbraw_ref, gn_ref, e_ref, o_ref, st_ref,
                 *, layer, n_chunks, unroll):
    @pl.when(pl.program_id(2) == 0)
    def _():
        st_ref[...] = jnp.zeros_like(st_ref)

    lr = lbraw_ref[...]
    ex = jnp.exp(lr - jnp.max(lr, axis=0, keepdims=True))
    sm = ex / jnp.sum(ex, axis=0, keepdims=True)
    lb = jnp.zeros((1, HEAD_DIM), F32)
    for r in range(1, layer + 1):
        lb = lb + sm[r:r + 1, :]
    log_lb = jnp.log(lb)
    log_1mlb = jnp.log(1.0 - lb)

    row = lax.broadcasted_iota(I32, (CHUNK, HEAD_DIM), 0)
    col = lax.broadcasted_iota(I32, (CHUNK, HEAD_DIM), 1)
    row_l = row & (SUB - 1)
    row_b = row >> SUB_SHIFT
    col_b = col >> SUB_SHIFT
    tri = (lax.broadcasted_iota(I32, (CHUNK, CHUNK), 0)
           >= lax.broadcasted_iota(I32, (CHUNK, CHUNK), 1)).astype(BF16)
    n_sub = CHUNK // SUB
    zeros_c = jnp.zeros((CHUNK, HEAD_DIM), F32)

    def sub_bcast(t, s):
        t4 = t.reshape(n_sub, SUB, HEAD_DIM)
        return jnp.broadcast_to(t4[:, s:s + 1, :], (n_sub, SUB, HEAD_DIM)).reshape(CHUNK, HEAD_DIM)

    def gates(z):
        ls = jnp.minimum(z, 0.0) - jnp.log(1.0 + jnp.exp(-jnp.abs(z)))
        if layer == 0:
            return ls, (ls - z) * LOG2E
        cc = log_1mlb + ls
        log_f = jnp.maximum(log_lb, cc) + jnp.log(1.0 + jnp.exp(-jnp.abs(log_lb - cc)))
        return log_f, (cc - z) * LOG2E

    def diag_terms(q, b2, c2):
        pieces = []
        for s in range(SUB):
            d = jnp.where(row_l >= s, b2 - sub_bcast(c2, s), -jnp.inf)
            pieces.append((q * jnp.exp2(d)).astype(BF16))
        return jnp.concatenate(pieces, axis=1)

    def below_keys(b2, c2):
        kts = []
        for i in range(1, n_sub):
            r_i = b2[i * SUB:i * SUB + 1, :]
            kts.append(jnp.exp2(jnp.where(row < i * SUB, r_i - c2, -jnp.inf)))
            kts.append(zeros_c)
        return jnp.concatenate(kts, axis=0).astype(BF16)

    def below_scores(r):
        a = jnp.zeros((CHUNK, HEAD_DIM), F32)
        for i in range(1, n_sub):
            a = a + jnp.where(row_b == i, r[:, (i - 1) * LANES:i * LANES], 0.0)
        return a

    def chunks(it, carry):
        cs = range(unroll)
        rows = [pl.ds(pl.multiple_of((it * unroll + c) * CHUNK, CHUNK), CHUNK) for c in cs]
        qr = [q_ref[r, :] for r in rows]
        v = [i_ref[r, :] for r in rows]
        q = [x * jax.nn.sigmoid(x) for x in qr]
        gt = [gates(f_ref[r, :]) for r in rows]
        b2 = []
        for c in cs:
            hi, mid, lo = _split3(gt[c][0])
            b2.append((_dot(tri, hi) + _dot(tri, mid) + _dot(tri, lo)) * LOG2E)
        c2 = [b2[c] - gt[c][1] for c in cs]
        b_last = [x[CHUNK - 1:CHUNK, :] for x in b2]
        a2 = [_dot(diag_terms(q[c], b2[c], c2[c]), e_ref[...]) for c in cs]
        rr = [_dot_nt((q[c] * jnp.exp2(b2[c] - sub_bcast(b2[c], 0))).astype(BF16), below_keys(b2[c], c2[c]))
              for c in cs]
        upd = [_dot(v[c].T.astype(BF16), jnp.exp2(b_last[c] - c2[c]).astype(BF16)) for c in cs]
        qe = [(q[c] * jnp.exp2(b2[c])).astype(BF16) for c in cs]
        st = st_ref[...]
        o_inter = []
        for c in cs:
            o_inter.append(_dot_nt(qe[c], st.astype(BF16)))
            st = st * jnp.exp2(b_last[c]) + upd[c]
        st_ref[...] = st
        a = [jnp.where(col_b == row_b, a2[c], 0.0) + below_scores(rr[c]) for c in cs]
        o_intra = [_dot(a[c][:, :CHUNK].astype(BF16), v[c].astype(BF16)) for c in cs]
        for c in cs:
            g = g_ref[rows[c], :]
            o_ref[rows[c], :] = _rms(o_inter[c] + o_intra[c], gn_ref[...]) * (g * jax.nn.sigmoid(g))
        return carry

    lax.fori_loop(0, n_chunks // unroll, chunks, 0)


def _hgrn_selector():
    e = np.zeros((SUB * HEAD_DIM, LANES), np.float32)
    s_of_row = np.arange(SUB * HEAD_DIM) // HEAD_DIM
    cols = np.arange(LANES)
    e[:, :] = ((cols[None, :] % SUB) == s_of_row[:, None]) & (cols[None, :] < CHUNK)
    return jnp.asarray(e, BF16)


def _hgrn(u, lb_raw, gn, layer, bsz, seq, tb=512):
    n = bsz * seq
    nb = seq // tb
    blk = lambda k: pl.BlockSpec((tb, HEAD_DIM), lambda b, h, j, k=k: (b * nb + j, h + N_HEADS * k))
    n_layers = lb_raw.shape[0]
    return pl.pallas_call(
        functools.partial(_hgrn_kernel, layer=layer, n_chunks=tb // CHUNK, unroll=4),
        out_shape=jax.ShapeDtypeStruct((n, WIDTH), F32),
        grid=(bsz, N_HEADS, nb),
        in_specs=[blk(0), blk(1), blk(2), blk(3),
                  pl.BlockSpec((n_layers, HEAD_DIM), lambda b, h, j: (0, h)),
                  pl.BlockSpec((1, HEAD_DIM), lambda b, h, j: (0, 0)),
                  pl.BlockSpec((SUB * HEAD_DIM, LANES), lambda b, h, j: (0, 0))],
        out_specs=pl.BlockSpec((tb, HEAD_DIM), lambda b, h, j: (b * nb + j, h)),
        scratch_shapes=[pltpu.VMEM((HEAD_DIM, HEAD_DIM), F32)],
        compiler_params=_params("parallel", "parallel", "arbitrary"),
        name="hgrn",
    )(u, u, u, u, lb_raw, gn, _hgrn_selector())


def _rope128(x, cos, sin_signed):
    return x * cos + pltpu.roll(x, 64, 1) * sin_signed


def _rope64(x, cos, sin_a, sin_b):
    return x * cos + pltpu.roll(x, 96, 1) * sin_a + pltpu.roll(x, 32, 1) * sin_b


def _dsa_prep_kernel(k_ref, v_ref, ql_ref, misc_ref, cm_ref, sm_ref, ci_ref, sia_ref, sib_ref,
                     wqb_ref, wqi_ref, gql_ref, gqn_ref, gkn_ref, gki_ref,
                     q_out, k_out, vt_out, qi_out, ki_out, wt_out):
    cm, sm = cm_ref[...], sm_ref[...]
    ci, sia, sib = ci_ref[...], sia_ref[...], sib_ref[...]
    cb = _rms(ql_ref[...], gql_ref[...]).astype(BF16)
    qf = _dot(cb, wqb_ref[...])
    scale = HEAD_DIM ** -0.5
    for h in range(N_HEADS):
        sl = slice(h * HEAD_DIM, (h + 1) * HEAD_DIM)
        qh = _rope128(_rms(qf[:, sl], gqn_ref[...]), cm, sm)
        q_out[:, sl] = (qh * (scale * LOG2E)).astype(BF16)
        kh = _rope128(_rms(k_ref[:, sl], gkn_ref[...]), cm, sm)
        k_out[:, sl] = kh.astype(BF16)
    vt = v_ref[...].T.astype(BF16)
    ones = jnp.ones((PACK, vt.shape[1]), BF16)
    for h in range(N_HEADS):
        vt_out[0, h * VT_ROWS:h * VT_ROWS + HEAD_DIM, :] = vt[h * HEAD_DIM:(h + 1) * HEAD_DIM, :]
        vt_out[0, h * VT_ROWS + HEAD_DIM:(h + 1) * VT_ROWS, :] = ones
    qi = _dot(cb, wqi_ref[...])
    for p in range(IDX_HEADS * IDX_DIM // LANES):
        sl = slice(p * LANES, (p + 1) * LANES)
        qi_out[:, sl] = _rope64(qi[:, sl], ci, sia, sib).astype(BF16)
    misc = misc_ref[...]
    lane = lax.broadcasted_iota(I32, misc.shape, 1)
    kraw = jnp.where(lane < IDX_DIM, misc, 0.0)
    kn = kraw * lax.rsqrt(jnp.sum(kraw * kraw, axis=-1, keepdims=True) * (1.0 / IDX_DIM) + EPS)
    kn = _rope64(kn * gki_ref[...], ci, sia, sib)
    ki_out[...] = (kn + pltpu.roll(kn, IDX_DIM, 1)).astype(BF16)
    wt = (misc * (IDX_HEADS ** -0.5 * IDX_DIM ** -0.5)).T
    wt_out[...] = wt[IDX_DIM:IDX_DIM + IDX_HEADS, :]


def _dsa_prep(u, tabs, wqb, wqi, gql, gqn, gkn, gki, tm=DSA_KEY_TILE):
    n = u.shape[0]
    nt = n // tm
    ublk = lambda width, col: pl.BlockSpec((tm, width), lambda i: (i, col // width))
    tab = pl.BlockSpec((tm, LANES), lambda i: (i, 0))
    full = lambda a: pl.BlockSpec(a.shape, lambda i: (0,) * a.ndim)
    return pl.pallas_call(
        _dsa_prep_kernel,
        out_shape=(jax.ShapeDtypeStruct((n, WIDTH), BF16),
                   jax.ShapeDtypeStruct((n, WIDTH), BF16),
                   jax.ShapeDtypeStruct((nt, N_HEADS * VT_ROWS, tm), BF16),
                   jax.ShapeDtypeStruct((n, IDX_HEADS * IDX_DIM), BF16),
                   jax.ShapeDtypeStruct((n, LANES), BF16),
                   jax.ShapeDtypeStruct((IDX_HEADS, n), F32)),
        grid=(nt,),
        in_specs=[ublk(WIDTH, U_K), ublk(WIDTH, U_V), ublk(Q_LORA, U_QLAT), ublk(LANES, U_MISC),
                  tab, tab, tab, tab, tab,
                  full(wqb), full(wqi), full(gql), full(gqn), full(gkn), full(gki)],
        out_specs=(pl.BlockSpec((tm, WIDTH), lambda i: (i, 0)),
                   pl.BlockSpec((tm, WIDTH), lambda i: (i, 0)),
                   pl.BlockSpec((1, N_HEADS * VT_ROWS, tm), lambda i: (i, 0, 0)),
                   pl.BlockSpec((tm, IDX_HEADS * IDX_DIM), lambda i: (i, 0)),
                   pl.BlockSpec((tm, LANES), lambda i: (i, 0)),
                   pl.BlockSpec((IDX_HEADS, tm), lambda i: (0, i))),
        compiler_params=_params("parallel"),
        name="dsa_prep",
    )(u, u, u, u, *tabs, wqb, wqi, gql, gqn, gkn, gki)


def _sortable(x):
    bits = lax.bitcast_convert_type(x, I32)
    return bits ^ ((bits >> 31) & 0x7FFFFFFF)


def _dsa_attn_kernel(qi_ref, wt_ref, q_ref, ki_ref, k_ref, vt_ref, o_ref,
                     qm_ref, hi_ref, lo_ref, kb_ref, bias_ref, lt_ref, acc0, acc1, acc2, acc3, *, tq, kt, top_k):
    acc_refs = (acc0, acc1, acc2, acc3)
    q0 = pl.program_id(1) * tq
    n_tiles = (q0 + tq + kt - 1) // kt

    lane_q = lax.broadcasted_iota(I32, (tq, LANES), 1)
    for h in range(IDX_HEADS):
        src = qi_ref[:, (h // 2) * LANES:(h // 2 + 1) * LANES].astype(F32)
        keep = (lane_q < IDX_DIM) if h % 2 == 0 else (lane_q >= IDX_DIM)
        qm_ref[h * tq:(h + 1) * tq, :] = jnp.where(keep, src, 0.0).astype(BF16)

    qpos = q0 + lax.broadcasted_iota(I32, (1, tq), 1)
    limit = ((qpos >> CHUNK_SHIFT) + 1) << CHUNK_SHIFT
    sub = 128
    key_iota = lax.broadcasted_iota(I32, (sub, tq), 0)

    def tile_rows(jt):
        return pl.ds(pl.multiple_of(jt * kt, kt), kt)

    def score_tile(jt, carry):
        for part in range(kt // sub):
            base = jt * kt + part * sub
            rows = pl.ds(pl.multiple_of(base, sub), sub)
            ki = ki_ref[rows, :]
            acc = jnp.zeros((sub, tq), F32)
            for h in range(IDX_HEADS):
                x = _dot_nt(ki, qm_ref[h * tq:(h + 1) * tq, :])
                acc = acc + wt_ref[h:h + 1, :] * jnp.maximum(x, 0.0)
            key = jnp.where(base + key_iota < limit, _sortable(acc), INT_MIN)
            hi_ref[rows, :] = (key >> 16).astype(I16)
            lo_ref[rows, :] = ((key & 0xFFFF) - 32768).astype(I16)
        return carry

    lax.fori_loop(0, n_tiles, score_tile, 0)

    one, zero = jnp.ones((), BF16), jnp.zeros((), BF16)
    neg_inf = jnp.full((), -jnp.inf, BF16)
    n_rows = (n_tiles * kt).astype(F32)

    def count16(ref, trial, strict):
        t16 = jnp.broadcast_to(trial.astype(I16), (PACK, tq))

        def body(jt, c):
            x = ref[tile_rows(jt), :]
            parts = [jnp.zeros((PACK, tq), BF16) for _ in range(4)]
            for r in range(kt // PACK):
                xr = x[r * PACK:(r + 1) * PACK, :]
                parts[r % 4] = parts[r % 4] + jnp.where((xr > t16) if strict else (xr >= t16), one, zero)
            return c + ((parts[0] + parts[1]) + (parts[2] + parts[3])).astype(F32)
        c = lax.fori_loop(0, n_tiles, body, jnp.zeros((PACK, tq), F32))
        return jnp.sum(c, axis=0, keepdims=True)

    def bisect16(ref, want, n_all, early_exit):
        def step(it, carry):
            lo, n_lo = carry
            trial = lo + lax.shift_left(jnp.int32(1), 15 - it)
            n = count16(ref, trial, False)
            ok = n >= want
            return jnp.where(ok, trial, lo), jnp.where(ok, n, n_lo)

        init = (jnp.full((1, tq), INT16_MIN, I32), n_all)
        if not early_exit:
            return lax.fori_loop(0, 16, step, init)
        head = 8

        def unresolved(carry):
            it, _, n_lo = carry
            return (it < 16) & (jnp.max(jnp.where(n_lo != want, 1.0, 0.0)) > 0.0)

        def two_steps(carry):
            it, lo, n_lo = carry
            return (it + 2,) + step(it + 1, step(it, (lo, n_lo)))

        _, lo, n_lo = lax.while_loop(unresolved, two_steps, (jnp.int32(head),) + lax.fori_loop(0, head, step, init))
        return lo, n_lo

    tau_hi, n_hi_ge = bisect16(hi_ref, float(top_k), jnp.full((1, tq), n_rows, F32), False)
    n_hi_gt = count16(hi_ref, tau_hi, True)
    want_lo = top_k - n_hi_gt
    tau_hi16 = jnp.broadcast_to(tau_hi.astype(I16), (kt, tq))

    def bucket_tile(jt, carry):
        rows = tile_rows(jt)
        kb_ref[rows, :] = jnp.where(hi_ref[rows, :] == tau_hi16, lo_ref[rows, :],
                                    jnp.full((), INT16_MIN, I16))
        return carry

    lax.fori_loop(0, n_tiles, bucket_tile, 0)
    tau_lo, n_kb_ge = bisect16(kb_ref, want_lo, n_hi_ge - n_hi_gt, True)
    live = (tau_hi > INT16_MIN) | (tau_lo > INT16_MIN)
    excess = jnp.max(jnp.where(live, n_kb_ge - want_lo, 0.0))

    @pl.when(excess <= 0)
    def _():
        tau_lo16 = jnp.broadcast_to(jnp.where(live, tau_lo, -INT16_MIN - 1).astype(I16), (kt, tq))

        def body(jt, carry):
            rows = tile_rows(jt)
            h16 = hi_ref[rows, :]
            in_bucket = jnp.where(lo_ref[rows, :] >= tau_lo16, zero, neg_inf)
            bias_ref[rows, :] = jnp.where(h16 > tau_hi16, zero, jnp.where(h16 == tau_hi16, in_bucket, neg_inf))
            return carry
        lax.fori_loop(0, n_tiles, body, 0)

    @pl.when(excess > 0)
    def _():
        need = want_lo - count16(kb_ref, tau_lo, True)
        live_f = jnp.where(live, 1.0, 0.0)
        tri = (lax.broadcasted_iota(I32, (kt, kt), 0)
               >= lax.broadcasted_iota(I32, (kt, kt), 1)).astype(BF16)

        def body(jt, seen):
            rows = tile_rows(jt)
            h32 = hi_ref[rows, :].astype(I32)
            l32 = lo_ref[rows, :].astype(I32)
            same_hi = h32 == tau_hi
            gt = jnp.where(h32 > tau_hi, 1.0, jnp.where(same_hi, jnp.where(l32 > tau_lo, 1.0, 0.0), 0.0))
            eq = jnp.where(same_hi, jnp.where(l32 == tau_lo, 1.0, 0.0), 0.0)
            rank = _dot(tri, eq.astype(BF16)) + seen
            sel = gt + eq * jnp.where(rank <= need, live_f, 0.0)
            bias_ref[rows, :] = jnp.where(sel > 0.0, 0.0, -jnp.inf).astype(BF16)
            return rank[kt - 1:kt, :]
        lax.fori_loop(0, n_tiles, body, jnp.zeros((1, tq), F32))

    for a in acc_refs:
        a[...] = jnp.zeros_like(a)
    heads = [slice(h * HEAD_DIM, (h + 1) * HEAD_DIM) for h in range(N_HEADS)]

    @pl.when((n_tiles & 1) == 1)
    def _():
        bias_ref[tile_rows(n_tiles), :] = jnp.full((kt, tq), -jnp.inf, BF16)

    def attn_pair(jp, ms):
        tiles = (2 * jp, 2 * jp + 1)
        for i, jt in enumerate(tiles):
            rows = tile_rows(jt)
            bias = bias_ref[rows, :].astype(F32)
            for h in range(N_HEADS):
                lt_ref[i, h] = _dot_nt(k_ref[rows, heads[h]], q_ref[:, heads[h]]) + bias
        ms = list(ms)
        for i, jt in enumerate(tiles):
            alphas, ps = [], []
            for h in range(N_HEADS):
                m_new = jnp.maximum(ms[h], jnp.max(lt_ref[i, h], axis=0, keepdims=True))
                m_safe = jnp.where(m_new == -jnp.inf, 0.0, m_new)
                ps.append(jnp.exp2(lt_ref[i, h] - m_safe).astype(BF16))
                alphas.append(jnp.exp2(ms[h] - m_safe))
                ms[h] = m_new
            for h in range(N_HEADS):
                acc_refs[h][...] = (acc_refs[h][...] * alphas[h]
                                    + _dot(vt_ref[jt, h * VT_ROWS:(h + 1) * VT_ROWS, :], ps[h]))
        return tuple(ms)

    lax.fori_loop(0, (n_tiles + 1) // 2, attn_pair,
                  tuple(jnp.full((1, tq), -jnp.inf, F32) for _ in range(N_HEADS)))
    for h in range(N_HEADS):
        acc = acc_refs[h][...]
        out_t = acc[:HEAD_DIM, :] * (1.0 / acc[HEAD_DIM:HEAD_DIM + 1, :])
        o_ref[:, heads[h]] = out_t.T


def _dsa_attn(q, k, vt, qi, ki, wt, bsz, seq, tq=256, kt=DSA_KEY_TILE):
    n = bsz * seq
    nq = seq // tq
    top_k = min(TOPK_MAX, seq // 4)
    once = dict(pipeline_mode=pl.Buffered(1))
    strip16 = pltpu.VMEM((seq, tq), I16)
    return pl.pallas_call(
        functools.partial(_dsa_attn_kernel, tq=tq, kt=kt, top_k=top_k),
        out_shape=jax.ShapeDtypeStruct((n, WIDTH), F32),
        grid=(bsz, nq),
        in_specs=[pl.BlockSpec((tq, IDX_HEADS * IDX_DIM), lambda b, i: (b * nq + i, 0)),
                  pl.BlockSpec((IDX_HEADS, tq), lambda b, i: (0, b * nq + i)),
                  pl.BlockSpec((tq, WIDTH), lambda b, i: (b * nq + i, 0)),
                  pl.BlockSpec((seq, LANES), lambda b, i: (b, 0), **once),
                  pl.BlockSpec((seq, WIDTH), lambda b, i: (b, 0), **once),
                  pl.BlockSpec((seq // kt, N_HEADS * VT_ROWS, kt), lambda b, i: (b, 0, 0), **once)],
        out_specs=pl.BlockSpec((tq, WIDTH), lambda b, i: (b * nq + i, 0)),
        scratch_shapes=[pltpu.VMEM((IDX_HEADS * tq, LANES), BF16), strip16, strip16, strip16,
                        pltpu.VMEM((seq, tq), BF16), pltpu.VMEM((2, N_HEADS, kt, tq), F32)]
                       + [pltpu.VMEM((VT_ROWS, tq), F32) for _ in range(N_HEADS)],
        compiler_params=_params("parallel", "arbitrary"),
        name="dsa_attn",
    )(qi, wt, q, ki, k, vt)


def _mem_kv_kernel(mem_ref, g_ref, wk_ref, wv_ref, gk_ref, k_out, v_out):
    mb = _rms(mem_ref[...], g_ref[...]).astype(BF16)
    kf = _dot(mb, wk_ref[...])
    for h in range(N_HEADS):
        sl = slice(h * HEAD_DIM, (h + 1) * HEAD_DIM)
        k_out[:, sl] = _rms(kf[:, sl], gk_ref[...]).astype(BF16)
    v_out[...] = _dot(mb, wv_ref[...]).astype(BF16)


def _mem_kv(mem2, gain, wk, wv, gk, n_mem):
    n, d = mem2.shape
    full = lambda a: pl.BlockSpec(a.shape, lambda b: (0,) * a.ndim)
    return pl.pallas_call(
        _mem_kv_kernel,
        out_shape=(jax.ShapeDtypeStruct((n, WIDTH), BF16), jax.ShapeDtypeStruct((n, WIDTH), BF16)),
        grid=(n // n_mem,),
        in_specs=[pl.BlockSpec((n_mem, d), lambda b: (b, 0)), full(gain), full(wk), full(wv), full(gk)],
        out_specs=(pl.BlockSpec((n_mem, WIDTH), lambda b: (b, 0)),
                   pl.BlockSpec((n_mem, WIDTH), lambda b: (b, 0))),
        compiler_params=_params("parallel"),
        name="mem_kv",
    )(mem2, gain, wk, wv, gk)


def _xattn_kernel(x_ref, oa_ref, ob_ref, wout_ref, g_ref, wq_ref, gq_ref, km_ref, vm_ref, wo_ref, o_ref):
    x1 = (x_ref[...] + _dot(oa_ref[...].astype(BF16), wout_ref[:WIDTH, :])
          + _dot(ob_ref[...].astype(BF16), wout_ref[WIDTH:, :]))
    hb = _rms(x1, g_ref[...]).astype(BF16)
    qf = _dot(hb, wq_ref[...])
    scale = HEAD_DIM ** -0.5
    outs = []
    for h in range(N_HEADS):
        sl = slice(h * HEAD_DIM, (h + 1) * HEAD_DIM)
        qh = (_rms(qf[:, sl], gq_ref[...]) * scale).astype(BF16)
        logits = _dot_nt(qh, km_ref[:, sl])
        p = jnp.exp(logits - jnp.max(logits, axis=-1, keepdims=True))
        oh = _dot(p.astype(BF16), vm_ref[:, sl])
        outs.append((oh * (1.0 / jnp.sum(p, axis=-1, keepdims=True))).astype(BF16))
    o_ref[...] = x1 + _dot(jnp.concatenate(outs, axis=1), wo_ref[...])


def _xattn(x2, oa, ob, wout, gain, wq, gq, km, vm, wo, seq, n_mem, tm=512):
    n, d = x2.shape
    per_b = seq // tm
    row = lambda w: pl.BlockSpec((tm, w), lambda i: (i, 0))
    full = lambda a: pl.BlockSpec(a.shape, lambda i: (0,) * a.ndim)
    memb = pl.BlockSpec((n_mem, WIDTH), lambda i: (i // per_b, 0))
    return pl.pallas_call(
        _xattn_kernel,
        out_shape=jax.ShapeDtypeStruct((n, d), F32),
        grid=(n // tm,),
        in_specs=[row(d), row(WIDTH), row(WIDTH), full(wout), full(gain), full(wq), full(gq),
                  memb, memb, full(wo)],
        out_specs=row(d),
        compiler_params=_params("parallel"),
        name="xattn",
    )(x2, oa, ob, wout, gain, wq, gq, km, vm, wo)


def _moe_kernel(x_ref, g_ref, wr_hi_ref, wr_lo_ref, br_ref, wgu_ref, wdn_ref, o_ref, hs_ref, gs_ref, ys_ref,
                *, tm, sub):
    lane = lax.broadcasted_iota(I32, (tm, LANES), 1)
    lane_f = lane.astype(F32)
    x = x_ref[...]
    h = _rms(x, g_ref[...])
    h_hi = h.astype(BF16)
    h_lo = (h - h_hi.astype(F32)).astype(BF16)
    lg = (_dot(h_hi, wr_hi_ref[...]) + _dot(h_lo, wr_hi_ref[...]) + _dot(h_hi, wr_lo_ref[...])
          + br_ref[...])
    first = lambda cond: jnp.min(jnp.where(cond, lane_f, 1e9), axis=-1, keepdims=True)
    gl = jnp.where(lane < N_GROUPS, lg, -jnp.inf)
    gmax = jnp.max(gl, axis=-1, keepdims=True)
    gsel = first(gl == gmax)
    g_w = 1.0 / jnp.sum(jnp.exp(gl - gmax), axis=-1, keepdims=True)
    grp_of_lane = ((lane - N_GROUPS) >> 2).astype(F32)
    in_grp = (lane >= N_GROUPS) & (lane < N_GROUPS + N_EXPERTS) & (grp_of_lane == gsel)
    el = jnp.where(in_grp, lg, -jnp.inf)
    v1 = jnp.max(el, axis=-1, keepdims=True)
    i1 = first(el == v1)
    el2 = jnp.where(lane_f == i1, -jnp.inf, el)
    v2 = jnp.max(el2, axis=-1, keepdims=True)
    i2 = first(el2 == v2)
    r = jnp.exp(v2 - v1)
    w1 = 1.0 / (1.0 + r)
    gate = jnp.where(lane_f == i1, w1 * g_w, jnp.where(lane_f == i2, r * w1 * g_w, 0.0))

    onehot = jnp.where(lane_f == gsel, 1.0, 0.0)
    t_row = lax.broadcasted_iota(I32, (tm, tm), 0)
    t_col = lax.broadcasted_iota(I32, (tm, tm), 1)
    before = _dot(jnp.where(t_row > t_col, 1.0, 0.0).astype(BF16), onehot.astype(BF16))
    counts = jnp.sum(onehot, axis=0, keepdims=True)
    lane1 = lax.broadcasted_iota(I32, (1, LANES), 1)
    ends = []
    run = jnp.zeros((1, 1), F32)
    for g in range(N_GROUPS - 1):
        run = run + jnp.sum(jnp.where(lane1 == g, counts, 0.0), axis=-1, keepdims=True)
        ends.append(run)
    start_of = sum(jnp.where(lane1 == g + 1, ends[g], 0.0) for g in range(N_GROUPS - 1))
    pos = jnp.sum(onehot * (before + start_of), axis=-1, keepdims=True)
    to_sorted_t = jnp.where(pos == t_col.astype(F32), 1.0, 0.0)
    to_sorted = to_sorted_t.T.astype(BF16)
    hs_ref[...] = _dot(to_sorted, h_hi).astype(BF16)
    g_hi, g_mid, g_lo = _split3(gate)
    gs_ref[...] = _dot(to_sorted, g_hi) + _dot(to_sorted, g_mid) + _dot(to_sorted, g_lo)

    bounds = [e_[0, 0].astype(I32) for e_ in ends]
    lane_s = lax.broadcasted_iota(I32, (sub, LANES), 1)

    def slab(j, carry):
        r0 = j * sub
        rows = pl.ds(pl.multiple_of(r0, sub), sub)
        g_first = sum((b <= r0).astype(I32) for b in bounds)
        g_last = sum((b <= r0 + sub - 1).astype(I32) for b in bounds)
        hs = hs_ref[rows, :]
        gs = gs_ref[rows, :]
        ys_ref[rows, :] = jnp.zeros((sub, ys_ref.shape[1]), F32)

        def group(g, c):
            gu = _dot(hs, wgu_ref[g])
            acts = []
            for e in range(EXP_PER_GROUP):
                ge = jnp.sum(jnp.where(lane_s == g * EXP_PER_GROUP + e + N_GROUPS, gs, 0.0), axis=-1, keepdims=True)
                up = gu[:, 2 * e * D_EXPERT:(2 * e + 1) * D_EXPERT]
                acts.append((up * jax.nn.sigmoid(up) * gu[:, (2 * e + 1) * D_EXPERT:(2 * e + 2) * D_EXPERT] * ge).astype(BF16))
            ys_ref[rows, :] += _dot(jnp.concatenate(acts, axis=1), wdn_ref[g])
            return c

        lax.fori_loop(g_first, g_last + 1, group, 0)
        return carry

    lax.fori_loop(0, tm // sub, slab, 0)
    o_ref[...] = x + _dot(to_sorted_t.astype(BF16), ys_ref[...].astype(BF16))


def _moe(x2, gain, wr_hi, wr_lo, br, wgu, wdn, tm=512, sub=128):
    n, d = x2.shape
    full = lambda a: pl.BlockSpec(a.shape, lambda i: (0,) * a.ndim)
    once = lambda a: pl.BlockSpec(a.shape, lambda i: (0,) * a.ndim, pipeline_mode=pl.Buffered(1))
    return pl.pallas_call(
        functools.partial(_moe_kernel, tm=tm, sub=sub),
        out_shape=jax.ShapeDtypeStruct((n, d), F32),
        grid=(n // tm,),
        in_specs=[pl.BlockSpec((tm, d), lambda i: (i, 0)), full(gain), full(wr_hi), full(wr_lo), full(br),
                  once(wgu), once(wdn)],
        out_specs=pl.BlockSpec((tm, d), lambda i: (i, 0)),
        scratch_shapes=[pltpu.VMEM((tm, d), BF16), pltpu.VMEM((tm, LANES), F32), pltpu.VMEM((tm, d), F32)],
        compiler_params=_params("parallel"),
        name="moe",
    )(x2, gain, wr_hi, wr_lo, br, wgu, wdn)


def _rope_tables(positions):
    pos = positions.reshape(-1).astype(F32)[:, None]

    def cs(dim):
        inv = ROPE_THETA ** (-jnp.arange(0, dim, 2, dtype=F32) / dim)
        ang = pos * inv
        return jnp.cos(ang), jnp.sin(ang)

    c, s = cs(HEAD_DIM)
    ci, si = cs(IDX_DIM)
    z = jnp.zeros_like(si)
    return (jnp.concatenate([c, c], 1), jnp.concatenate([-s, s], 1),
            jnp.concatenate([ci, ci, ci, ci], 1), jnp.concatenate([-si, z, -si, z], 1),
            jnp.concatenate([z, si, z, si], 1))


def _reorder_in_weight(w):
    a = w[:, :4 * WIDTH]
    o = 4 * WIDTH
    qlat = w[:, o:o + Q_LORA]; o += Q_LORA
    k = w[:, o:o + WIDTH]; o += WIDTH
    v = w[:, o:o + WIDTH]; o += WIDTH
    misc = w[:, o:]
    pad = jnp.zeros((w.shape[0], LANES - misc.shape[1]), w.dtype)
    return jnp.concatenate([a, k, v, qlat, misc, pad], axis=1).astype(BF16)


def _row(v):
    return v.reshape(1, -1).astype(F32)


def kernel(x, mem, positions, norm_mix, w_in, lb_raw, a_gnorm, b_qlat_gain, b_wqb, b_wqidx, b_qnorm,
           b_knorm, b_kidx_norm, w_out, norm_x, norm_mem, x_wq, x_wk, x_wv, x_wo, x_qnorm, x_knorm,
           norm_ffn, w_rg, b_rg, w_re, b_re, w_gu, w_dn):
    bsz, seq, d = x.shape
    n_mem = mem.shape[1]
    depth = w_in.shape[0]
    x2 = x.reshape(bsz * seq, d)
    mem2 = mem.reshape(bsz * n_mem, d)
    tabs = _rope_tables(positions)
    pad_lanes = lambda v: jnp.pad(v, ((0, 0), (0, LANES - v.shape[1])))
    for l in range(depth):
        u = _in_proj(x2, _row(norm_mix[l]), _reorder_in_weight(w_in[l]))
        o_a = _hgrn(u, lb_raw.astype(F32), _row(a_gnorm[l]), l, bsz, seq)
        q, k, vt, qi, ki, wt = _dsa_prep(
            u, tabs, b_wqb[l].astype(BF16), b_wqidx[l].astype(BF16), _row(b_qlat_gain[l]),
            _row(b_qnorm[l]), _row(b_knorm[l]), pad_lanes(_row(b_kidx_norm[l])))
        o_b = _dsa_attn(q, k, vt, qi, ki, wt, bsz, seq)
        km, vm = _mem_kv(mem2, _row(norm_mem[l]), x_wk[l].astype(BF16), x_wv[l].astype(BF16),
                         _row(x_knorm[l]), n_mem)
        x2 = _xattn(x2, o_a, o_b, w_out[l].astype(BF16), _row(norm_x[l]), x_wq[l].astype(BF16),
                    _row(x_qnorm[l]), km, vm, x_wo[l].astype(BF16), seq, n_mem)
        wr = pad_lanes(jnp.concatenate([w_rg[l], w_re[l].reshape(d, N_EXPERTS)], axis=1).astype(F32))
        wr_hi = wr.astype(BF16)
        wr_lo = (wr - wr_hi.astype(F32)).astype(BF16)
        br = pad_lanes(jnp.concatenate([b_rg[l], b_re[l].reshape(-1)]).reshape(1, -1).astype(F32))
        x2 = _moe(x2, _row(norm_ffn[l]), wr_hi, wr_lo, br,
                  w_gu[l].transpose(0, 2, 1, 3).reshape(N_GROUPS, d, EXP_PER_GROUP * 2 * D_EXPERT).astype(BF16),
                  w_dn[l].reshape(N_GROUPS, EXP_PER_GROUP * D_EXPERT, d).astype(BF16))
    return x2.reshape(bsz, seq, d)
```

```python
import functools

import numpy as np
import jax
import jax.numpy as jnp
from jax import lax
from jax.experimental import pallas as pl
from jax.experimental.pallas import tpu as pltpu

F32 = jnp.float32
BF16 = jnp.bfloat16
I32 = jnp.int32
I16 = jnp.int16

EPS = 1e-6
ROPE_THETA = 10000.0
CHUNK = 64
CHUNK_SHIFT = 6
SUB = 16
SUB_SHIFT = 4
N_HEADS = 4
HEAD_DIM = 128
WIDTH = N_HEADS * HEAD_DIM
Q_LORA = 256
IDX_HEADS = 16
IDX_DIM = 64
TOPK_MAX = 256
N_GROUPS = 4
EXP_PER_GROUP = 4
N_EXPERTS = N_GROUPS * EXP_PER_GROUP
D_EXPERT = 256
LANES = 128
INT_MIN = -2 ** 31
INT16_MIN = -2 ** 15
PACK = 16
LOG2E = 1.4426950408889634
VT_ROWS = HEAD_DIM + PACK
DSA_KEY_TILE = 512

U_AQ, U_AF, U_AI, U_AG = 0, 512, 1024, 1536
U_K, U_V, U_QLAT, U_MISC = 2048, 2560, 3072, 3328
U_COLS = 3456

VMEM_LIMIT = 56 * 1024 * 1024


def _rms(x, gain):
    return x * lax.rsqrt(jnp.mean(x * x, axis=-1, keepdims=True) + EPS) * gain


def _dot(a, b):
    return jnp.dot(a, b, preferred_element_type=F32)


def _dot_nt(a, b):
    return lax.dot_general(a, b, (((1,), (1,)), ((), ())), preferred_element_type=F32)


def _params(*sem):
    return pltpu.CompilerParams(dimension_semantics=sem, vmem_limit_bytes=VMEM_LIMIT)


def _in_proj_kernel(x_ref, g_ref, w_ref, u_ref, *, col_chunk):
    hb = _rms(x_ref[...], g_ref[...]).astype(BF16)
    for c0 in range(0, U_COLS, col_chunk):
        u_ref[:, c0:c0 + col_chunk] = _dot(hb, w_ref[:, c0:c0 + col_chunk])


def _in_proj(x2, gain, w_bf, tm=256):
    n, d = x2.shape
    return pl.pallas_call(
        functools.partial(_in_proj_kernel, col_chunk=1152),
        out_shape=jax.ShapeDtypeStruct((n, U_COLS), F32),
        grid=(n // tm,),
        in_specs=[pl.BlockSpec((tm, d), lambda i: (i, 0)),
                  pl.BlockSpec((1, d), lambda i: (0, 0)),
                  pl.BlockSpec((d, U_COLS), lambda i: (0, 0))],
        out_specs=pl.BlockSpec((tm, U_COLS), lambda i: (i, 0)),
        compiler_params=_params("parallel"),
        name="in_proj",
    )(x2, gain, w_bf)


def _split3(x):
    hi = x.astype(BF16)
    r1 = x - hi.astype(F32)
    mid = r1.astype(BF16)
    lo = (r1 - mid.astype(F32)).astype(BF16)
    return hi, mid, lo


def _hgrn_kernel(q_ref, f_ref, i_ref, g_ref, lbraw_ref, gn_ref, e_ref, o_ref, st_ref,
                 *, layer, n_chunks, unroll):
    @pl.when(pl.program_id(2) == 0)
    def _():
        st_ref[...] = jnp.zeros_like(st_ref)

    lr = lbraw_ref[...]
    ex = jnp.exp(lr - jnp.max(lr, axis=0, keepdims=True))
    sm = ex / jnp.sum(ex, axis=0, keepdims=True)
    lb = jnp.zeros((1, HEAD_DIM), F32)
    for r in range(1, layer + 1):
        lb = lb + sm[r:r + 1, :]
    log_lb = jnp.log(lb)
    log_1mlb = jnp.log(1.0 - lb)

    row = lax.broadcasted_iota(I32, (CHUNK, HEAD_DIM), 0)
    col = lax.broadcasted_iota(I32, (CHUNK, HEAD_DIM), 1)
    row_l = row & (SUB - 1)
    row_b = row >> SUB_SHIFT
    col_b = col >> SUB_SHIFT
    tri = (lax.broadcasted_iota(I32, (CHUNK, CHUNK), 0)
           >= lax.broadcasted_iota(I32, (CHUNK, CHUNK), 1)).astype(BF16)
    n_sub = CHUNK // SUB
    zeros_c = jnp.zeros((CHUNK, HEAD_DIM), F32)

    def sub_bcast(t, s):
        t4 = t.reshape(n_sub, SUB, HEAD_DIM)
        return jnp.broadcast_to(t4[:, s:s + 1, :], (n_sub, SUB, HEAD_DIM)).reshape(CHUNK, HEAD_DIM)

    def gates(z):
        ls = jnp.minimum(z, 0.0) - jnp.log(1.0 + jnp.exp(-jnp.abs(z)))
        if layer == 0:
            return ls, (ls - z) * LOG2E
        cc = log_1mlb + ls
        log_f = jnp.maximum(log_lb, cc) + jnp.log(1.0 + jnp.exp(-jnp.abs(log_lb - cc)))
        return log_f, (cc - z) * LOG2E

    def diag_terms(q, b2, c2):
        pieces = []
        for s in range(SUB):
            d = jnp.where(row_l >= s, b2 - sub_bcast(c2, s), -jnp.inf)
            pieces.append((q * jnp.exp2(d)).astype(BF16))
        return jnp.concatenate(pieces, axis=1)

    def below_keys(b2, c2):
        kts = []
        for i in range(1, n_sub):
            r_i = b2[i * SUB:i * SUB + 1, :]
            kts.append(jnp.exp2(jnp.where(row < i * SUB, r_i - c2, -jnp.inf)))
            kts.append(zeros_c)
        return jnp.concatenate(kts, axis=0).astype(BF16)

    def below_scores(r):
        a = jnp.zeros((CHUNK, HEAD_DIM), F32)
        for i in range(1, n_sub):
            a = a + jnp.where(row_b == i, r[:, (i - 1) * LANES:i * LANES], 0.0)
        return a

    def chunks(it, carry):
        cs = range(unroll)
        rows = [pl.ds(pl.multiple_of((it * unroll + c) * CHUNK, CHUNK), CHUNK) for c in cs]
        qr = [q_ref[r, :] for r in rows]
        v = [i_ref[r, :] for r in rows]
        q = [x * jax.nn.sigmoid(x) for x in qr]
        gt = [gates(f_ref[r, :]) for r in rows]
        b2 = []
        for c in cs:
            hi, mid, lo = _split3(gt[c][0])
            b2.append((_dot(tri, hi) + _dot(tri, mid) + _dot(tri, lo)) * LOG2E)
        c2 = [b2[c] - gt[c][1] for c in cs]
        b_last = [x[CHUNK - 1:CHUNK, :] for x in b2]
        a2 = [_dot(diag_terms(q[c], b2[c], c2[c]), e_ref[...]) for c in cs]
        rr = [_dot_nt((q[c] * jnp.exp2(b2[c] - sub_bcast(b2[c], 0))).astype(BF16), below_keys(b2[c], c2[c]))
              for c in cs]
        upd = [_dot(v[c].T.astype(BF16), jnp.exp2(b_last[c] - c2[c]).astype(BF16)) for c in cs]
        qe = [(q[c] * jnp.exp2(b2[c])).astype(BF16) for c in cs]
        st = st_ref[...]
        o_inter = []
        for c in cs:
            o_inter.append(_dot_nt(qe[c], st.astype(BF16)))
            st = st * jnp.exp2(b_last[c]) + upd[c]
        st_ref[...] = st
        a = [jnp.where(col_b == row_b, a2[c], 0.0) + below_scores(rr[c]) for c in cs]
        o_intra = [_dot(a[c][:, :CHUNK].astype(BF16), v[c].astype(BF16)) for c in cs]
        for c in cs:
            g = g_ref[rows[c], :]
            o_ref[rows[c], :] = _rms(o_inter[c] + o_intra[c], gn_ref[...]) * (g * jax.nn.sigmoid(g))
        return carry

    lax.fori_loop(0, n_chunks // unroll, chunks, 0)


def _hgrn_selector():
    e = np.zeros((SUB * HEAD_DIM, LANES), np.float32)
    s_of_row = np.arange(SUB * HEAD_DIM) // HEAD_DIM
    cols = np.arange(LANES)
    e[:, :] = ((cols[None, :] % SUB) == s_of_row[:, None]) & (cols[None, :] < CHUNK)
    return jnp.asarray(e, BF16)


def _hgrn(u, lb_raw, gn, layer, bsz, seq, tb=512):
    n = bsz * seq
    nb = seq // tb
    blk = lambda k: pl.BlockSpec((tb, HEAD_DIM), lambda b, h, j, k=k: (b * nb + j, h + N_HEADS * k))
    n_layers = lb_raw.shape[0]
    return pl.pallas_call(
        functools.partial(_hgrn_kernel, layer=layer, n_chunks=tb // CHUNK, unroll=4),
        out_shape=jax.ShapeDtypeStruct((n, WIDTH), F32),
        grid=(bsz, N_HEADS, nb),
        in_specs=[blk(0), blk(1), blk(2), blk(3),
                  pl.BlockSpec((n_layers, HEAD_DIM), lambda b, h, j: (0, h)),
                  pl.BlockSpec((1, HEAD_DIM), lambda b, h, j: (0, 0)),
                  pl.BlockSpec((SUB * HEAD_DIM, LANES), lambda b, h, j: (0, 0))],
        out_specs=pl.BlockSpec((tb, HEAD_DIM), lambda b, h, j: (b * nb + j, h)),
        scratch_shapes=[pltpu.VMEM((HEAD_DIM, HEAD_DIM), F32)],
        compiler_params=_params("parallel", "parallel", "arbitrary"),
        name="hgrn",
    )(u, u, u, u, lb_raw, gn, _hgrn_selector())


def _rope128(x, cos, sin_signed):
    return x * cos + pltpu.roll(x, 64, 1) * sin_signed


def _rope64(x, cos, sin_a, sin_b):
    return x * cos + pltpu.roll(x, 96, 1) * sin_a + pltpu.roll(x, 32, 1) * sin_b


def _dsa_prep_kernel(k_ref, v_ref, ql_ref, misc_ref, cm_ref, sm_ref, ci_ref, sia_ref, sib_ref,
                     wqb_ref, wqi_ref, gql_ref, gqn_ref, gkn_ref, gki_ref,
                     q_out, k_out, vt_out, qi_out, ki_out, wt_out):
    cm, sm = cm_ref[...], sm_ref[...]
    ci, sia, sib = ci_ref[...], sia_ref[...], sib_ref[...]
    cb = _rms(ql_ref[...], gql_ref[...]).astype(BF16)
    qf = _dot(cb, wqb_ref[...])
    scale = HEAD_DIM ** -0.5
    for h in range(N_HEADS):
        sl = slice(h * HEAD_DIM, (h + 1) * HEAD_DIM)
        qh = _rope128(_rms(qf[:, sl], gqn_ref[...]), cm, sm)
        q_out[:, sl] = (qh * (scale * LOG2E)).astype(BF16)
        kh = _rope128(_rms(k_ref[:, sl], gkn_ref[...]), cm, sm)
        k_out[:, sl] = kh.astype(BF16)
    vt = v_ref[...].T.astype(BF16)
    ones = jnp.ones((PACK, vt.shape[1]), BF16)
    for h in range(N_HEADS):
        vt_out[0, h * VT_ROWS:h * VT_ROWS + HEAD_DIM, :] = vt[h * HEAD_DIM:(h + 1) * HEAD_DIM, :]
        vt_out[0, h * VT_ROWS + HEAD_DIM:(h + 1) * VT_ROWS, :] = ones
    qi = _dot(cb, wqi_ref[...])
    for p in range(IDX_HEADS * IDX_DIM // LANES):
        sl = slice(p * LANES, (p + 1) * LANES)
        qi_out[:, sl] = _rope64(qi[:, sl], ci, sia, sib).astype(BF16)
    misc = misc_ref[...]
    lane = lax.broadcasted_iota(I32, misc.shape, 1)
    kraw = jnp.where(lane < IDX_DIM, misc, 0.0)
    kn = kraw * lax.rsqrt(jnp.sum(kraw * kraw, axis=-1, keepdims=True) * (1.0 / IDX_DIM) + EPS)
    kn = _rope64(kn * gki_ref[...], ci, sia, sib)
    ki_out[...] = (kn + pltpu.roll(kn, IDX_DIM, 1)).astype(BF16)
    wt = (misc * (IDX_HEADS ** -0.5 * IDX_DIM ** -0.5)).T
    wt_out[...] = wt[IDX_DIM:IDX_DIM + IDX_HEADS, :]


def _dsa_prep(u, tabs, wqb, wqi, gql, gqn, gkn, gki, tm=DSA_KEY_TILE):
    n = u.shape[0]
    nt = n // tm
    ublk = lambda width, col: pl.BlockSpec((tm, width), lambda i: (i, col // width))
    tab = pl.BlockSpec((tm, LANES), lambda i: (i, 0))
    full = lambda a: pl.BlockSpec(a.shape, lambda i: (0,) * a.ndim)
    return pl.pallas_call(
        _dsa_prep_kernel,
        out_shape=(jax.ShapeDtypeStruct((n, WIDTH), BF16),
                   jax.ShapeDtypeStruct((n, WIDTH), BF16),
                   jax.ShapeDtypeStruct((nt, N_HEADS * VT_ROWS, tm), BF16),
                   jax.ShapeDtypeStruct((n, IDX_HEADS * IDX_DIM), BF16),
                   jax.ShapeDtypeStruct((n, LANES), BF16),
                   jax.ShapeDtypeStruct((IDX_HEADS, n), F32)),
        grid=(nt,),
        in_specs=[ublk(WIDTH, U_K), ublk(WIDTH, U_V), ublk(Q_LORA, U_QLAT), ublk(LANES, U_MISC),
                  tab, tab, tab, tab, tab,
                  full(wqb), full(wqi), full(gql), full(gqn), full(gkn), full(gki)],
        out_specs=(pl.BlockSpec((tm, WIDTH), lambda i: (i, 0)),
                   pl.BlockSpec((tm, WIDTH), lambda i: (i, 0)),
                   pl.BlockSpec((1, N_HEADS * VT_ROWS, tm), lambda i: (i, 0, 0)),
                   pl.BlockSpec((tm, IDX_HEADS * IDX_DIM), lambda i: (i, 0)),
                   pl.BlockSpec((tm, LANES), lambda i: (i, 0)),
                   pl.BlockSpec((IDX_HEADS, tm), lambda i: (0, i))),
        compiler_params=_params("parallel"),
        name="dsa_prep",
    )(u, u, u, u, *tabs, wqb, wqi, gql, gqn, gkn, gki)


def _sortable(x):
    bits = lax.bitcast_convert_type(x, I32)
    return bits ^ ((bits >> 31) & 0x7FFFFFFF)


def _dsa_attn_kernel(qi_ref, wt_ref, qin_ref, wtn_ref, q_ref, ki_ref, k_ref, vt_ref, o_ref,
                     qm_ref, hi_ref, lo_ref, kb_ref, bias_ref, lt_ref, acc0, acc1, acc2, acc3,
                     *, tq, kt, top_k, n_blocks):
    acc_refs = (acc0, acc1, acc2, acc3)
    blk = pl.program_id(1)
    cur = blk & 1
    nxt = 1 - cur
    q0 = blk * tq
    n_tiles = (q0 + tq + kt - 1) // kt
    n_pairs = (n_tiles + 1) // 2
    sub = 128
    key_iota = lax.broadcasted_iota(I32, (sub, tq), 0)
    lane_q = lax.broadcasted_iota(I32, (tq, LANES), 1)

    def tile_rows(jt):
        return pl.ds(pl.multiple_of(jt * kt, kt), kt)

    def limit_of(q_start):
        qpos = q_start + lax.broadcasted_iota(I32, (1, tq), 1)
        return ((qpos >> CHUNK_SHIFT) + 1) << CHUNK_SHIFT

    def build_qm(src_ref):
        for h in range(IDX_HEADS):
            src = src_ref[:, (h // 2) * LANES:(h // 2 + 1) * LANES].astype(F32)
            keep = (lane_q < IDX_DIM) if h % 2 == 0 else (lane_q >= IDX_DIM)
            qm_ref[h * tq:(h + 1) * tq, :] = jnp.where(keep, src, 0.0).astype(BF16)

    def score_tile(jt, slot, limit, w_ref):
        for part in range(kt // sub):
            base = jt * kt + part * sub
            rows = pl.ds(pl.multiple_of(base, sub), sub)
            ki = ki_ref[rows, :]
            acc = jnp.zeros((sub, tq), F32)
            for h in range(IDX_HEADS):
                x = _dot_nt(ki, qm_ref[h * tq:(h + 1) * tq, :])
                acc = acc + w_ref[h:h + 1, :] * jnp.maximum(x, 0.0)
            key = jnp.where(base + key_iota < limit, _sortable(acc), INT_MIN)
            hi_ref[slot, rows, :] = (key >> 16).astype(I16)
            lo_ref[slot, rows, :] = ((key & 0xFFFF) - 32768).astype(I16)

    @pl.when(blk == 0)
    def _():
        build_qm(qi_ref)
        limit0 = limit_of(0)

        def body(jt, carry):
            score_tile(jt, 0, limit0, wt_ref)
            return carry
        lax.fori_loop(0, n_tiles, body, 0)

    one, zero = jnp.ones((), BF16), jnp.zeros((), BF16)
    neg_inf = jnp.full((), -jnp.inf, BF16)
    n_rows = (n_tiles * kt).astype(F32)
    hi_rows = lambda rows: hi_ref[cur, rows, :]
    lo_rows = lambda rows: lo_ref[cur, rows, :]
    kb_rows = lambda rows: kb_ref[rows, :]

    def count16(get, trial, strict):
        t16 = jnp.broadcast_to(trial.astype(I16), (PACK, tq))

        def body(jt, c):
            x = get(tile_rows(jt))
            parts = [jnp.zeros((PACK, tq), BF16) for _ in range(4)]
            for r in range(kt // PACK):
                xr = x[r * PACK:(r + 1) * PACK, :]
                parts[r % 4] = parts[r % 4] + jnp.where((xr > t16) if strict else (xr >= t16), one, zero)
            return c + ((parts[0] + parts[1]) + (parts[2] + parts[3])).astype(F32)
        c = lax.fori_loop(0, n_tiles, body, jnp.zeros((PACK, tq), F32))
        return jnp.sum(c, axis=0, keepdims=True)

    def bisect16(get, want, n_all, early_exit):
        def step(it, carry):
            lo, n_lo = carry
            trial = lo + lax.shift_left(jnp.int32(1), 15 - it)
            n = count16(get, trial, False)
            ok = n >= want
            return jnp.where(ok, trial, lo), jnp.where(ok, n, n_lo)

        init = (jnp.full((1, tq), INT16_MIN, I32), n_all)
        if not early_exit:
            return lax.fori_loop(0, 16, step, init)
        head = 8

        def unresolved(carry):
            it, _, n_lo = carry
            return (it < 16) & (jnp.max(jnp.where(n_lo != want, 1.0, 0.0)) > 0.0)

        def two_steps(carry):
            it, lo, n_lo = carry
            return (it + 2,) + step(it + 1, step(it, (lo, n_lo)))

        _, lo, n_lo = lax.while_loop(unresolved, two_steps, (jnp.int32(head),) + lax.fori_loop(0, head, step, init))
        return lo, n_lo

    tau_hi, n_hi_ge = bisect16(hi_rows, float(top_k), jnp.full((1, tq), n_rows, F32), False)
    n_hi_gt = count16(hi_rows, tau_hi, True)
    want_lo = top_k - n_hi_gt
    tau_hi16 = jnp.broadcast_to(tau_hi.astype(I16), (kt, tq))

    def bucket_tile(jt, carry):
        rows = tile_rows(jt)
        kb_ref[rows, :] = jnp.where(hi_rows(rows) == tau_hi16, lo_rows(rows), jnp.full((), INT16_MIN, I16))
        return carry

    lax.fori_loop(0, n_tiles, bucket_tile, 0)
    tau_lo, n_kb_ge = bisect16(kb_rows, want_lo, n_hi_ge - n_hi_gt, True)
    live = (tau_hi > INT16_MIN) | (tau_lo > INT16_MIN)
    excess = jnp.max(jnp.where(live, n_kb_ge - want_lo, 0.0))

    @pl.when(excess <= 0)
    def _():
        tau_lo16 = jnp.broadcast_to(jnp.where(live, tau_lo, -INT16_MIN - 1).astype(I16), (kt, tq))

        def body(jt, carry):
            rows = tile_rows(jt)
            h16 = hi_rows(rows)
            in_bucket = jnp.where(lo_rows(rows) >= tau_lo16, zero, neg_inf)
            bias_ref[rows, :] = jnp.where(h16 > tau_hi16, zero, jnp.where(h16 == tau_hi16, in_bucket, neg_inf))
            return carry
        lax.fori_loop(0, n_tiles, body, 0)

    @pl.when(excess > 0)
    def _():
        need = want_lo - count16(kb_rows, tau_lo, True)
        live_f = jnp.where(live, 1.0, 0.0)
        tri = (lax.broadcasted_iota(I32, (kt, kt), 0)
               >= lax.broadcasted_iota(I32, (kt, kt), 1)).astype(BF16)

        def body(jt, seen):
            rows = tile_rows(jt)
            h32 = hi_rows(rows).astype(I32)
            l32 = lo_rows(rows).astype(I32)
            same_hi = h32 == tau_hi
            gt = jnp.where(h32 > tau_hi, 1.0, jnp.where(same_hi, jnp.where(l32 > tau_lo, 1.0, 0.0), 0.0))
            eq = jnp.where(same_hi, jnp.where(l32 == tau_lo, 1.0, 0.0), 0.0)
            rank = _dot(tri, eq.astype(BF16)) + seen
            sel = gt + eq * jnp.where(rank <= need, live_f, 0.0)
            bias_ref[rows, :] = jnp.where(sel > 0.0, 0.0, -jnp.inf).astype(BF16)
            return rank[kt - 1:kt, :]
        lax.fori_loop(0, n_tiles, body, jnp.zeros((1, tq), F32))

    for a in acc_refs:
        a[...] = jnp.zeros_like(a)
    heads = [slice(h * HEAD_DIM, (h + 1) * HEAD_DIM) for h in range(N_HEADS)]

    @pl.when((n_tiles & 1) == 1)
    def _():
        bias_ref[tile_rows(n_tiles), :] = jnp.full((kt, tq), -jnp.inf, BF16)

    limit_next = limit_of(q0 + tq)

    def attn_pair(jp, ms, score_next):
        tiles = (2 * jp, 2 * jp + 1)
        for i, jt in enumerate(tiles):
            rows = tile_rows(jt)
            bias = bias_ref[rows, :].astype(F32)
            for h in range(N_HEADS):
                lt_ref[i, h] = _dot_nt(k_ref[rows, heads[h]], q_ref[:, heads[h]]) + bias
        ms = list(ms)
        for i, jt in enumerate(tiles):
            if score_next:
                score_tile(jt, nxt, limit_next, wtn_ref)
            alphas, ps = [], []
            for h in range(N_HEADS):
                m_new = jnp.maximum(ms[h], jnp.max(lt_ref[i, h], axis=0, keepdims=True))
                m_safe = jnp.where(m_new == -jnp.inf, 0.0, m_new)
                ps.append(jnp.exp2(lt_ref[i, h] - m_safe).astype(BF16))
                alphas.append(jnp.exp2(ms[h] - m_safe))
                ms[h] = m_new
            for h in range(N_HEADS):
                acc_refs[h][...] = (acc_refs[h][...] * alphas[h]
                                    + _dot(vt_ref[jt, h * VT_ROWS:(h + 1) * VT_ROWS, :], ps[h]))
        return tuple(ms)

    m_init = tuple(jnp.full((1, tq), -jnp.inf, F32) for _ in range(N_HEADS))

    @pl.when(blk < n_blocks - 1)
    def _():
        build_qm(qin_ref)
        lax.fori_loop(0, n_pairs, functools.partial(attn_pair, score_next=True), m_init)

        @pl.when((q0 + 2 * tq + kt - 1) // kt > 2 * n_pairs)
        def _():
            score_tile(2 * n_pairs, nxt, limit_next, wtn_ref)

    @pl.when(blk == n_blocks - 1)
    def _():
        lax.fori_loop(0, n_pairs, functools.partial(attn_pair, score_next=False), m_init)

    for h in range(N_HEADS):
        acc = acc_refs[h][...]
        out_t = acc[:HEAD_DIM, :] * (1.0 / acc[HEAD_DIM:HEAD_DIM + 1, :])
        o_ref[:, heads[h]] = out_t.T


def _dsa_attn(q, k, vt, qi, ki, wt, bsz, seq, tq=256, kt=DSA_KEY_TILE):
    n = bsz * seq
    nq = seq // tq
    top_k = min(TOPK_MAX, seq // 4)
    once = dict(pipeline_mode=pl.Buffered(1))
    nxt_blk = lambda b, i: b * nq + jnp.minimum(i + 1, nq - 1)
    return pl.pallas_call(
        functools.partial(_dsa_attn_kernel, tq=tq, kt=kt, top_k=top_k, n_blocks=nq),
        out_shape=jax.ShapeDtypeStruct((n, WIDTH), F32),
        grid=(bsz, nq),
        in_specs=[pl.BlockSpec((tq, IDX_HEADS * IDX_DIM), lambda b, i: (b * nq + i, 0)),
                  pl.BlockSpec((IDX_HEADS, tq), lambda b, i: (0, b * nq + i)),
                  pl.BlockSpec((tq, IDX_HEADS * IDX_DIM), lambda b, i: (nxt_blk(b, i), 0)),
                  pl.BlockSpec((IDX_HEADS, tq), lambda b, i: (0, nxt_blk(b, i))),
                  pl.BlockSpec((tq, WIDTH), lambda b, i: (b * nq + i, 0)),
                  pl.BlockSpec((seq, LANES), lambda b, i: (b, 0), **once),
                  pl.BlockSpec((seq, WIDTH), lambda b, i: (b, 0), **once),
                  pl.BlockSpec((seq // kt, N_HEADS * VT_ROWS, kt), lambda b, i: (b, 0, 0), **once)],
        out_specs=pl.BlockSpec((tq, WIDTH), lambda b, i: (b * nq + i, 0)),
        scratch_shapes=[pltpu.VMEM((IDX_HEADS * tq, LANES), BF16),
                        pltpu.VMEM((2, seq, tq), I16), pltpu.VMEM((2, seq, tq), I16), pltpu.VMEM((seq, tq), I16),
                        pltpu.VMEM((seq, tq), BF16), pltpu.VMEM((2, N_HEADS, kt, tq), F32)]
                       + [pltpu.VMEM((VT_ROWS, tq), F32) for _ in range(N_HEADS)],
        compiler_params=_params("arbitrary", "arbitrary"),
        name="dsa_attn",
    )(qi, wt, qi, wt, q, ki, k, vt)


def _mem_kv_kernel(mem_ref, g_ref, wk_ref, wv_ref, gk_ref, k_out, v_out):
    mb = _rms(mem_ref[...], g_ref[...]).astype(BF16)
    kf = _dot(mb, wk_ref[...])
    for h in range(N_HEADS):
        sl = slice(h * HEAD_DIM, (h + 1) * HEAD_DIM)
        k_out[:, sl] = _rms(kf[:, sl], gk_ref[...]).astype(BF16)
    v_out[...] = _dot(mb, wv_ref[...]).astype(BF16)


def _mem_kv(mem2, gain, wk, wv, gk, n_mem):
    n, d = mem2.shape
    full = lambda a: pl.BlockSpec(a.shape, lambda b: (0,) * a.ndim)
    return pl.pallas_call(
        _mem_kv_kernel,
        out_shape=(jax.ShapeDtypeStruct((n, WIDTH), BF16), jax.ShapeDtypeStruct((n, WIDTH), BF16)),
        grid=(n // n_mem,),
        in_specs=[pl.BlockSpec((n_mem, d), lambda b: (b, 0)), full(gain), full(wk), full(wv), full(gk)],
        out_specs=(pl.BlockSpec((n_mem, WIDTH), lambda b: (b, 0)),
                   pl.BlockSpec((n_mem, WIDTH), lambda b: (b, 0))),
        compiler_params=_params("parallel"),
        name="mem_kv",
    )(mem2, gain, wk, wv, gk)


def _xattn_kernel(x_ref, oa_ref, ob_ref, wout_ref, g_ref, wq_ref, gq_ref, km_ref, vm_ref, wo_ref, o_ref):
    x1 = (x_ref[...] + _dot(oa_ref[...].astype(BF16), wout_ref[:WIDTH, :])
          + _dot(ob_ref[...].astype(BF16), wout_ref[WIDTH:, :]))
    hb = _rms(x1, g_ref[...]).astype(BF16)
    qf = _dot(hb, wq_ref[...])
    scale = HEAD_DIM ** -0.5
    outs = []
    for h in range(N_HEADS):
        sl = slice(h * HEAD_DIM, (h + 1) * HEAD_DIM)
        qh = (_rms(qf[:, sl], gq_ref[...]) * scale).astype(BF16)
        logits = _dot_nt(qh, km_ref[:, sl])
        p = jnp.exp(logits - jnp.max(logits, axis=-1, keepdims=True))
        oh = _dot(p.astype(BF16), vm_ref[:, sl])
        outs.append((oh * (1.0 / jnp.sum(p, axis=-1, keepdims=True))).astype(BF16))
    o_ref[...] = x1 + _dot(jnp.concatenate(outs, axis=1), wo_ref[...])


def _xattn(x2, oa, ob, wout, gain, wq, gq, km, vm, wo, seq, n_mem, tm=512):
    n, d = x2.shape
    per_b = seq // tm
    row = lambda w: pl.BlockSpec((tm, w), lambda i: (i, 0))
    full = lambda a: pl.BlockSpec(a.shape, lambda i: (0,) * a.ndim)
    memb = pl.BlockSpec((n_mem, WIDTH), lambda i: (i // per_b, 0))
    return pl.pallas_call(
        _xattn_kernel,
        out_shape=jax.ShapeDtypeStruct((n, d), F32),
        grid=(n // tm,),
        in_specs=[row(d), row(WIDTH), row(WIDTH), full(wout), full(gain), full(wq), full(gq),
                  memb, memb, full(wo)],
        out_specs=row(d),
        compiler_params=_params("parallel"),
        name="xattn",
    )(x2, oa, ob, wout, gain, wq, gq, km, vm, wo)


def _moe_kernel(x_ref, g_ref, wr_hi_ref, wr_lo_ref, br_ref, wgu_ref, wdn_ref, o_ref, hs_ref, gs_ref, ys_ref,
                *, tm, sub):
    lane = lax.broadcasted_iota(I32, (tm, LANES), 1)
    lane_f = lane.astype(F32)
    x = x_ref[...]
    h = _rms(x, g_ref[...])
    h_hi = h.astype(BF16)
    h_lo = (h - h_hi.astype(F32)).astype(BF16)
    lg = (_dot(h_hi, wr_hi_ref[...]) + _dot(h_lo, wr_hi_ref[...]) + _dot(h_hi, wr_lo_ref[...])
          + br_ref[...])
    first = lambda cond: jnp.min(jnp.where(cond, lane_f, 1e9), axis=-1, keepdims=True)
    gl = jnp.where(lane < N_GROUPS, lg, -jnp.inf)
    gmax = jnp.max(gl, axis=-1, keepdims=True)
    gsel = first(gl == gmax)
    g_w = 1.0 / jnp.sum(jnp.exp(gl - gmax), axis=-1, keepdims=True)
    grp_of_lane = ((lane - N_GROUPS) >> 2).astype(F32)
    in_grp = (lane >= N_GROUPS) & (lane < N_GROUPS + N_EXPERTS) & (grp_of_lane == gsel)
    el = jnp.where(in_grp, lg, -jnp.inf)
    v1 = jnp.max(el, axis=-1, keepdims=True)
    i1 = first(el == v1)
    el2 = jnp.where(lane_f == i1, -jnp.inf, el)
    v2 = jnp.max(el2, axis=-1, keepdims=True)
    i2 = first(el2 == v2)
    r = jnp.exp(v2 - v1)
    w1 = 1.0 / (1.0 + r)
    gate = jnp.where(lane_f == i1, w1 * g_w, jnp.where(lane_f == i2, r * w1 * g_w, 0.0))

    onehot = jnp.where(lane_f == gsel, 1.0, 0.0)
    t_row = lax.broadcasted_iota(I32, (tm, tm), 0)
    t_col = lax.broadcasted_iota(I32, (tm, tm), 1)
    before = _dot(jnp.where(t_row > t_col, 1.0, 0.0).astype(BF16), onehot.astype(BF16))
    counts = jnp.sum(onehot, axis=0, keepdims=True)
    lane1 = lax.broadcasted_iota(I32, (1, LANES), 1)
    ends = []
    run = jnp.zeros((1, 1), F32)
    for g in range(N_GROUPS - 1):
        run = run + jnp.sum(jnp.where(lane1 == g, counts, 0.0), axis=-1, keepdims=True)
        ends.append(run)
    start_of = sum(jnp.where(lane1 == g + 1, ends[g], 0.0) for g in range(N_GROUPS - 1))
    pos = jnp.sum(onehot * (before + start_of), axis=-1, keepdims=True)
    to_sorted_t = jnp.where(pos == t_col.astype(F32), 1.0, 0.0)
    to_sorted = to_sorted_t.T.astype(BF16)
    hs_ref[...] = _dot(to_sorted, h_hi).astype(BF16)
    g_hi, g_mid, g_lo = _split3(gate)
    gs_ref[...] = _dot(to_sorted, g_hi) + _dot(to_sorted, g_mid) + _dot(to_sorted, g_lo)

    bounds = [e_[0, 0].astype(I32) for e_ in ends]
    lane_s = lax.broadcasted_iota(I32, (sub, LANES), 1)

    def slab(j, carry):
        r0 = j * sub
        rows = pl.ds(pl.multiple_of(r0, sub), sub)
        g_first = sum((b <= r0).astype(I32) for b in bounds)
        g_last = sum((b <= r0 + sub - 1).astype(I32) for b in bounds)
        hs = hs_ref[rows, :]
        gs = gs_ref[rows, :]
        ys_ref[rows, :] = jnp.zeros((sub, ys_ref.shape[1]), F32)

        def group(g, c):
            gu = _dot(hs, wgu_ref[g])
            acts = []
            for e in range(EXP_PER_GROUP):
                ge = jnp.sum(jnp.where(lane_s == g * EXP_PER_GROUP + e + N_GROUPS, gs, 0.0), axis=-1, keepdims=True)
                up = gu[:, 2 * e * D_EXPERT:(2 * e + 1) * D_EXPERT]
                acts.append((up * jax.nn.sigmoid(up) * gu[:, (2 * e + 1) * D_EXPERT:(2 * e + 2) * D_EXPERT] * ge).astype(BF16))
            ys_ref[rows, :] += _dot(jnp.concatenate(acts, axis=1), wdn_ref[g])
            return c

        lax.fori_loop(g_first, g_last + 1, group, 0)
        return carry

    lax.fori_loop(0, tm // sub, slab, 0)
    o_ref[...] = x + _dot(to_sorted_t.astype(BF16), ys_ref[...].astype(BF16))


def _moe(x2, gain, wr_hi, wr_lo, br, wgu, wdn, tm=512, sub=128):
    n, d = x2.shape
    full = lambda a: pl.BlockSpec(a.shape, lambda i: (0,) * a.ndim)
    once = lambda a: pl.BlockSpec(a.shape, lambda i: (0,) * a.ndim, pipeline_mode=pl.Buffered(1))
    return pl.pallas_call(
        functools.partial(_moe_kernel, tm=tm, sub=sub),
        out_shape=jax.ShapeDtypeStruct((n, d), F32),
        grid=(n // tm,),
        in_specs=[pl.BlockSpec((tm, d), lambda i: (i, 0)), full(gain), full(wr_hi), full(wr_lo), full(br),
                  once(wgu), once(wdn)],
        out_specs=pl.BlockSpec((tm, d), lambda i: (i, 0)),
        scratch_shapes=[pltpu.VMEM((tm, d), BF16), pltpu.VMEM((tm, LANES), F32), pltpu.VMEM((tm, d), F32)],
        compiler_params=_params("parallel"),
        name="moe",
    )(x2, gain, wr_hi, wr_lo, br, wgu, wdn)


def _rope_tables(positions):
    pos = positions.reshape(-1).astype(F32)[:, None]

    def cs(dim):
        inv = ROPE_THETA ** (-jnp.arange(0, dim, 2, dtype=F32) / dim)
        ang = pos * inv
        return jnp.cos(ang), jnp.sin(ang)

    c, s = cs(HEAD_DIM)
    ci, si = cs(IDX_DIM)
    z = jnp.zeros_like(si)
    return (jnp.concatenate([c, c], 1), jnp.concatenate([-s, s], 1),
            jnp.concatenate([ci, ci, ci, ci], 1), jnp.concatenate([-si, z, -si, z], 1),
            jnp.concatenate([z, si, z, si], 1))


def _reorder_in_weight(w):
    a = w[:, :4 * WIDTH]
    o = 4 * WIDTH
    qlat = w[:, o:o + Q_LORA]; o += Q_LORA
    k = w[:, o:o + WIDTH]; o += WIDTH
    v = w[:, o:o + WIDTH]; o += WIDTH
    misc = w[:, o:]
    pad = jnp.zeros((w.shape[0], LANES - misc.shape[1]), w.dtype)
    return jnp.concatenate([a, k, v, qlat, misc, pad], axis=1).astype(BF16)


def _row(v):
    return v.reshape(1, -1).astype(F32)


def kernel(x, mem, positions, norm_mix, w_in, lb_raw, a_gnorm, b_qlat_gain, b_wqb, b_wqidx, b_qnorm,
           b_knorm, b_kidx_norm, w_out, norm_x, norm_mem, x_wq, x_wk, x_wv, x_wo, x_qnorm, x_knorm,
           norm_ffn, w_rg, b_rg, w_re, b_re, w_gu, w_dn):
    bsz, seq, d = x.shape
    n_mem = mem.shape[1]
    depth = w_in.shape[0]
    x2 = x.reshape(bsz * seq, d)
    mem2 = mem.reshape(bsz * n_mem, d)
    tabs = _rope_tables(positions)
    pad_lanes = lambda v: jnp.pad(v, ((0, 0), (0, LANES - v.shape[1])))
    for l in range(depth):
        u = _in_proj(x2, _row(norm_mix[l]), _reorder_in_weight(w_in[l]))
        o_a = _hgrn(u, lb_raw.astype(F32), _row(a_gnorm[l]), l, bsz, seq)
        q, k, vt, qi, ki, wt = _dsa_prep(
            u, tabs, b_wqb[l].astype(BF16), b_wqidx[l].astype(BF16), _row(b_qlat_gain[l]),
            _row(b_qnorm[l]), _row(b_knorm[l]), pad_lanes(_row(b_kidx_norm[l])))
        o_b = _dsa_attn(q, k, vt, qi, ki, wt, bsz, seq)
        km, vm = _mem_kv(mem2, _row(norm_mem[l]), x_wk[l].astype(BF16), x_wv[l].astype(BF16),
                         _row(x_knorm[l]), n_mem)
        x2 = _xattn(x2, o_a, o_b, w_out[l].astype(BF16), _row(norm_x[l]), x_wq[l].astype(BF16),
                    _row(x_qnorm[l]), km, vm, x_wo[l].astype(BF16), seq, n_mem)
        wr = pad_lanes(jnp.concatenate([w_rg[l], w_re[l].reshape(d, N_EXPERTS)], axis=1).astype(F32))
        wr_hi = wr.astype(BF16)
        wr_lo = (wr - wr_hi.astype(F32)).astype(BF16)
        br = pad_lanes(jnp.concatenate([b_rg[l], b_re[l].reshape(-1)]).reshape(1, -1).astype(F32))
        x2 = _moe(x2, _row(norm_ffn[l]), wr_hi, wr_lo, br,
                  w_gu[l].transpose(0, 2, 1, 3).reshape(N_GROUPS, d, EXP_PER_GROUP * 2 * D_EXPERT).astype(BF16),
                  w_dn[l].reshape(N_GROUPS, EXP_PER_GROUP * D_EXPERT, d).astype(BF16))
    return x2.reshape(bsz, seq, d)
```

```python
import functools

import numpy as np
import jax
import jax.numpy as jnp
from jax import lax
from jax.experimental import pallas as pl
from jax.experimental.pallas import tpu as pltpu

F32 = jnp.float32
BF16 = jnp.bfloat16
I32 = jnp.int32
I16 = jnp.int16

EPS = 1e-6
ROPE_THETA = 10000.0
CHUNK = 64
CHUNK_SHIFT = 6
SUB = 16
SUB_SHIFT = 4
N_HEADS = 4
HEAD_DIM = 128
WIDTH = N_HEADS * HEAD_DIM
Q_LORA = 256
IDX_HEADS = 16
IDX_DIM = 64
TOPK_MAX = 256
N_GROUPS = 4
EXP_PER_GROUP = 4
N_EXPERTS = N_GROUPS * EXP_PER_GROUP
D_EXPERT = 256
LANES = 128
INT_MIN = -2 ** 31
INT16_MIN = -2 ** 15
PACK = 16
LOG2E = 1.4426950408889634
VT_ROWS = HEAD_DIM + PACK
DSA_KEY_TILE = 512

U_AQ, U_AF, U_AI, U_AG = 0, 512, 1024, 1536
U_K, U_V, U_QLAT, U_MISC = 2048, 2560, 3072, 3328
U_COLS = 3456

VMEM_LIMIT = 56 * 1024 * 1024


def _rms(x, gain):
    return x * lax.rsqrt(jnp.mean(x * x, axis=-1, keepdims=True) + EPS) * gain


def _dot(a, b):
    return jnp.dot(a, b, preferred_element_type=F32)


def _dot_nt(a, b):
    return lax.dot_general(a, b, (((1,), (1,)), ((), ())), preferred_element_type=F32)


def _params(*sem):
    return pltpu.CompilerParams(dimension_semantics=sem, vmem_limit_bytes=VMEM_LIMIT)


def _in_proj_kernel(x_ref, g_ref, w_ref, u_ref, *, col_chunk):
    hb = _rms(x_ref[...], g_ref[...]).astype(BF16)
    for c0 in range(0, U_COLS, col_chunk):
        u_ref[:, c0:c0 + col_chunk] = _dot(hb, w_ref[:, c0:c0 + col_chunk])


def _in_proj(x2, gain, w_bf, tm=512):
    n, d = x2.shape
    return pl.pallas_call(
        functools.partial(_in_proj_kernel, col_chunk=1152),
        out_shape=jax.ShapeDtypeStruct((n, U_COLS), F32),
        grid=(n // tm,),
        in_specs=[pl.BlockSpec((tm, d), lambda i: (i, 0)),
                  pl.BlockSpec((1, d), lambda i: (0, 0)),
                  pl.BlockSpec((d, U_COLS), lambda i: (0, 0))],
        out_specs=pl.BlockSpec((tm, U_COLS), lambda i: (i, 0)),
        compiler_params=_params("parallel"),
        name="in_proj",
    )(x2, gain, w_bf)


def _split3(x):
    hi = x.astype(BF16)
    r1 = x - hi.astype(F32)
    mid = r1.astype(BF16)
    lo = (r1 - mid.astype(F32)).astype(BF16)
    return hi, mid, lo


def _hgrn_kernel(q_ref, f_ref, i_ref, g_ref, lbraw_ref, gn_ref, e_ref, o_ref, st_ref,
                 *, layer, n_chunks, unroll):
    @pl.when(pl.program_id(2) == 0)
    def _():
        st_ref[...] = jnp.zeros_like(st_ref)

    lr = lbraw_ref[...]
    ex = jnp.exp(lr - jnp.max(lr, axis=0, keepdims=True))
    sm = ex / jnp.sum(ex, axis=0, keepdims=True)
    lb = jnp.zeros((1, HEAD_DIM), F32)
    for r in range(1, layer + 1):
        lb = lb + sm[r:r + 1, :]
    log_lb = jnp.log(lb)
    log_1mlb = jnp.log(1.0 - lb)

    row = lax.broadcasted_iota(I32, (CHUNK, HEAD_DIM), 0)
    col = lax.broadcasted_iota(I32, (CHUNK, HEAD_DIM), 1)
    row_l = row & (SUB - 1)
    row_b = row >> SUB_SHIFT
    col_b = col >> SUB_SHIFT
    tri = (lax.broadcasted_iota(I32, (CHUNK, CHUNK), 0)
           >= lax.broadcasted_iota(I32, (CHUNK, CHUNK), 1)).astype(BF16)
    n_sub = CHUNK // SUB
    zeros_c = jnp.zeros((CHUNK, HEAD_DIM), F32)

    def sub_bcast(t, s):
        t4 = t.reshape(n_sub, SUB, HEAD_DIM)
        return jnp.broadcast_to(t4[:, s:s + 1, :], (n_sub, SUB, HEAD_DIM)).reshape(CHUNK, HEAD_DIM)

    def gates(z):
        ls = jnp.minimum(z, 0.0) - jnp.log(1.0 + jnp.exp(-jnp.abs(z)))
        if layer == 0:
            return ls, (ls - z) * LOG2E
        cc = log_1mlb + ls
        log_f = jnp.maximum(log_lb, cc) + jnp.log(1.0 + jnp.exp(-jnp.abs(log_lb - cc)))
        return log_f, (cc - z) * LOG2E

    def diag_terms(q, b2, c2):
        pieces = []
        for s in range(SUB):
            d = jnp.where(row_l >= s, b2 - sub_bcast(c2, s), -jnp.inf)
            pieces.append((q * jnp.exp2(d)).astype(BF16))
        return jnp.concatenate(pieces, axis=1)

    def below_keys(b2, c2):
        kts = []
        for i in range(1, n_sub):
            r_i = b2[i * SUB:i * SUB + 1, :]
            kts.append(jnp.exp2(jnp.where(row < i * SUB, r_i - c2, -jnp.inf)))
            kts.append(zeros_c)
        return jnp.concatenate(kts, axis=0).astype(BF16)

    def below_scores(r):
        a = jnp.zeros((CHUNK, HEAD_DIM), F32)
        for i in range(1, n_sub):
            a = a + jnp.where(row_b == i, r[:, (i - 1) * LANES:i * LANES], 0.0)
        return a

    def chunks(it, carry):
        cs = range(unroll)
        rows = [pl.ds(pl.multiple_of((it * unroll + c) * CHUNK, CHUNK), CHUNK) for c in cs]
        qr = [q_ref[r, :] for r in rows]
        v = [i_ref[r, :] for r in rows]
        q = [x * jax.nn.sigmoid(x) for x in qr]
        gt = [gates(f_ref[r, :]) for r in rows]
        b2 = []
        for c in cs:
            hi, mid, lo = _split3(gt[c][0])
            b2.append((_dot(tri, hi) + _dot(tri, mid) + _dot(tri, lo)) * LOG2E)
        c2 = [b2[c] - gt[c][1] for c in cs]
        b_last = [x[CHUNK - 1:CHUNK, :] for x in b2]
        a2 = [_dot(diag_terms(q[c], b2[c], c2[c]), e_ref[...]) for c in cs]
        rr = [_dot_nt((q[c] * jnp.exp2(b2[c] - sub_bcast(b2[c], 0))).astype(BF16), below_keys(b2[c], c2[c]))
              for c in cs]
        upd = [_dot(v[c].T.astype(BF16), jnp.exp2(b_last[c] - c2[c]).astype(BF16)) for c in cs]
        qe = [(q[c] * jnp.exp2(b2[c])).astype(BF16) for c in cs]
        st = st_ref[...]
        o_inter = []
        for c in cs:
            o_inter.append(_dot_nt(qe[c], st.astype(BF16)))
            st = st * jnp.exp2(b_last[c]) + upd[c]
        st_ref[...] = st
        a = [jnp.where(col_b == row_b, a2[c], 0.0) + below_scores(rr[c]) for c in cs]
        o_intra = [_dot(a[c][:, :CHUNK].astype(BF16), v[c].astype(BF16)) for c in cs]
        for c in cs:
            g = g_ref[rows[c], :]
            o_ref[rows[c], :] = _rms(o_inter[c] + o_intra[c], gn_ref[...]) * (g * jax.nn.sigmoid(g))
        return carry

    lax.fori_loop(0, n_chunks // unroll, chunks, 0)


def _hgrn_selector():
    e = np.zeros((SUB * HEAD_DIM, LANES), np.float32)
    s_of_row = np.arange(SUB * HEAD_DIM) // HEAD_DIM
    cols = np.arange(LANES)
    e[:, :] = ((cols[None, :] % SUB) == s_of_row[:, None]) & (cols[None, :] < CHUNK)
    return jnp.asarray(e, BF16)


def _hgrn(u, lb_raw, gn, layer, bsz, seq, tb=1024):
    n = bsz * seq
    nb = seq // tb
    blk = lambda k: pl.BlockSpec((tb, HEAD_DIM), lambda b, h, j, k=k: (b * nb + j, h + N_HEADS * k))
    n_layers = lb_raw.shape[0]
    return pl.pallas_call(
        functools.partial(_hgrn_kernel, layer=layer, n_chunks=tb // CHUNK, unroll=16),
        out_shape=jax.ShapeDtypeStruct((n, WIDTH), F32),
        grid=(bsz, N_HEADS, nb),
        in_specs=[blk(0), blk(1), blk(2), blk(3),
                  pl.BlockSpec((n_layers, HEAD_DIM), lambda b, h, j: (0, h)),
                  pl.BlockSpec((1, HEAD_DIM), lambda b, h, j: (0, 0)),
                  pl.BlockSpec((SUB * HEAD_DIM, LANES), lambda b, h, j: (0, 0))],
        out_specs=pl.BlockSpec((tb, HEAD_DIM), lambda b, h, j: (b * nb + j, h)),
        scratch_shapes=[pltpu.VMEM((HEAD_DIM, HEAD_DIM), F32)],
        compiler_params=_params("parallel", "parallel", "arbitrary"),
        name="hgrn",
    )(u, u, u, u, lb_raw, gn, _hgrn_selector())


def _rope128(x, cos, sin_signed):
    return x * cos + pltpu.roll(x, 64, 1) * sin_signed


def _rope64(x, cos, sin_a, sin_b):
    return x * cos + pltpu.roll(x, 96, 1) * sin_a + pltpu.roll(x, 32, 1) * sin_b


def _dsa_prep_kernel(k_ref, v_ref, ql_ref, misc_ref, cm_ref, sm_ref, ci_ref, sia_ref, sib_ref,
                     wqb_ref, wqi_ref, gql_ref, gqn_ref, gkn_ref, gki_ref,
                     q_out, k_out, vt_out, qi_out, ki_out, wt_out):
    cm, sm = cm_ref[...], sm_ref[...]
    ci, sia, sib = ci_ref[...], sia_ref[...], sib_ref[...]
    cb = _rms(ql_ref[...], gql_ref[...]).astype(BF16)
    qf = _dot(cb, wqb_ref[...])
    scale = HEAD_DIM ** -0.5
    for h in range(N_HEADS):
        sl = slice(h * HEAD_DIM, (h + 1) * HEAD_DIM)
        qh = _rope128(_rms(qf[:, sl], gqn_ref[...]), cm, sm)
        q_out[:, sl] = (qh * (scale * LOG2E)).astype(BF16)
        kh = _rope128(_rms(k_ref[:, sl], gkn_ref[...]), cm, sm)
        k_out[:, sl] = kh.astype(BF16)
    vt = v_ref[...].T.astype(BF16)
    ones = jnp.ones((PACK, vt.shape[1]), BF16)
    for h in range(N_HEADS):
        vt_out[0, h * VT_ROWS:h * VT_ROWS + HEAD_DIM, :] = vt[h * HEAD_DIM:(h + 1) * HEAD_DIM, :]
        vt_out[0, h * VT_ROWS + HEAD_DIM:(h + 1) * VT_ROWS, :] = ones
    qi = _dot(cb, wqi_ref[...])
    for p in range(IDX_HEADS * IDX_DIM // LANES):
        sl = slice(p * LANES, (p + 1) * LANES)
        qi_out[:, sl] = _rope64(qi[:, sl], ci, sia, sib).astype(BF16)
    misc = misc_ref[...]
    lane = lax.broadcasted_iota(I32, misc.shape, 1)
    kraw = jnp.where(lane < IDX_DIM, misc, 0.0)
    kn = kraw * lax.rsqrt(jnp.sum(kraw * kraw, axis=-1, keepdims=True) * (1.0 / IDX_DIM) + EPS)
    kn = _rope64(kn * gki_ref[...], ci, sia, sib)
    ki_out[...] = (kn + pltpu.roll(kn, IDX_DIM, 1)).astype(BF16)
    wt = (misc * (IDX_HEADS ** -0.5 * IDX_DIM ** -0.5)).T
    wt_out[...] = wt[IDX_DIM:IDX_DIM + IDX_HEADS, :]


def _dsa_prep(u, tabs, wqb, wqi, gql, gqn, gkn, gki, tm=DSA_KEY_TILE):
    n = u.shape[0]
    nt = n // tm
    ublk = lambda width, col: pl.BlockSpec((tm, width), lambda i: (i, col // width))
    tab = pl.BlockSpec((tm, LANES), lambda i: (i, 0))
    full = lambda a: pl.BlockSpec(a.shape, lambda i: (0,) * a.ndim)
    return pl.pallas_call(
        _dsa_prep_kernel,
        out_shape=(jax.ShapeDtypeStruct((n, WIDTH), BF16),
                   jax.ShapeDtypeStruct((n, WIDTH), BF16),
                   jax.ShapeDtypeStruct((nt, N_HEADS * VT_ROWS, tm), BF16),
                   jax.ShapeDtypeStruct((n, IDX_HEADS * IDX_DIM), BF16),
                   jax.ShapeDtypeStruct((n, LANES), BF16),
                   jax.ShapeDtypeStruct((IDX_HEADS, n), F32)),
        grid=(nt,),
        in_specs=[ublk(WIDTH, U_K), ublk(WIDTH, U_V), ublk(Q_LORA, U_QLAT), ublk(LANES, U_MISC),
                  tab, tab, tab, tab, tab,
                  full(wqb), full(wqi), full(gql), full(gqn), full(gkn), full(gki)],
        out_specs=(pl.BlockSpec((tm, WIDTH), lambda i: (i, 0)),
                   pl.BlockSpec((tm, WIDTH), lambda i: (i, 0)),
                   pl.BlockSpec((1, N_HEADS * VT_ROWS, tm), lambda i: (i, 0, 0)),
                   pl.BlockSpec((tm, IDX_HEADS * IDX_DIM), lambda i: (i, 0)),
                   pl.BlockSpec((tm, LANES), lambda i: (i, 0)),
                   pl.BlockSpec((IDX_HEADS, tm), lambda i: (0, i))),
        compiler_params=_params("parallel"),
        name="dsa_prep",
    )(u, u, u, u, *tabs, wqb, wqi, gql, gqn, gkn, gki)


def _sortable(x):
    bits = lax.bitcast_convert_type(x, I32)
    return bits ^ ((bits >> 31) & 0x7FFFFFFF)


def _dsa_attn_kernel(qi_ref, wt_ref, qin_ref, wtn_ref, q_ref, ki_ref, k_ref, vt_ref, o_ref,
                     qm_ref, hi_ref, lo_ref, kb_ref, bias_ref, lt_ref, acc0, acc1, acc2, acc3,
                     *, tq, kt, top_k, n_blocks):
    acc_refs = (acc0, acc1, acc2, acc3)
    blk = pl.program_id(1)
    cur = blk & 1
    nxt = 1 - cur
    q0 = blk * tq
    n_tiles = (q0 + tq + kt - 1) // kt
    n_pairs = (n_tiles + 1) // 2
    sub = 128
    key_iota = lax.broadcasted_iota(I32, (sub, tq), 0)
    lane_q = lax.broadcasted_iota(I32, (tq, LANES), 1)

    def tile_rows(jt):
        return pl.ds(pl.multiple_of(jt * kt, kt), kt)

    def limit_of(q_start):
        qpos = q_start + lax.broadcasted_iota(I32, (1, tq), 1)
        return ((qpos >> CHUNK_SHIFT) + 1) << CHUNK_SHIFT

    def build_qm(src_ref):
        for h in range(IDX_HEADS):
            src = src_ref[:, (h // 2) * LANES:(h // 2 + 1) * LANES].astype(F32)
            keep = (lane_q < IDX_DIM) if h % 2 == 0 else (lane_q >= IDX_DIM)
            qm_ref[h * tq:(h + 1) * tq, :] = jnp.where(keep, src, 0.0).astype(BF16)

    def score_tile(jt, slot, limit, w_ref):
        for part in range(kt // sub):
            base = jt * kt + part * sub
            rows = pl.ds(pl.multiple_of(base, sub), sub)
            ki = ki_ref[rows, :]
            acc = jnp.zeros((sub, tq), F32)
            for h in range(IDX_HEADS):
                x = _dot_nt(ki, qm_ref[h * tq:(h + 1) * tq, :])
                acc = acc + w_ref[h:h + 1, :] * jnp.maximum(x, 0.0)
            key = jnp.where(base + key_iota < limit, _sortable(acc), INT_MIN)
            hi_ref[slot, rows, :] = (key >> 16).astype(I16)
            lo_ref[slot, rows, :] = ((key & 0xFFFF) - 32768).astype(I16)

    @pl.when(blk == 0)
    def _():
        build_qm(qi_ref)
        limit0 = limit_of(0)

        def body(jt, carry):
            score_tile(jt, 0, limit0, wt_ref)
            return carry
        lax.fori_loop(0, n_tiles, body, 0)

    one, zero = jnp.ones((), BF16), jnp.zeros((), BF16)
    neg_inf = jnp.full((), -jnp.inf, BF16)
    n_rows = (n_tiles * kt).astype(F32)
    hi_rows = lambda rows: hi_ref[cur, rows, :]
    lo_rows = lambda rows: lo_ref[cur, rows, :]
    kb_rows = lambda rows: kb_ref[rows, :]

    def count16(get, trial, strict):
        t16 = jnp.broadcast_to(trial.astype(I16), (PACK, tq))

        def body(jt, c):
            x = get(tile_rows(jt))
            parts = [jnp.zeros((PACK, tq), BF16) for _ in range(4)]
            for r in range(kt // PACK):
                xr = x[r * PACK:(r + 1) * PACK, :]
                parts[r % 4] = parts[r % 4] + jnp.where((xr > t16) if strict else (xr >= t16), one, zero)
            return c + ((parts[0] + parts[1]) + (parts[2] + parts[3])).astype(F32)
        c = lax.fori_loop(0, n_tiles, body, jnp.zeros((PACK, tq), F32))
        return jnp.sum(c, axis=0, keepdims=True)

    def bisect16(get, want, n_all, early_exit):
        def step(it, carry):
            lo, n_lo = carry
            trial = lo + lax.shift_left(jnp.int32(1), 15 - it)
            n = count16(get, trial, False)
            ok = n >= want
            return jnp.where(ok, trial, lo), jnp.where(ok, n, n_lo)

        init = (jnp.full((1, tq), INT16_MIN, I32), n_all)
        if not early_exit:
            return lax.fori_loop(0, 16, step, init)
        head = 8

        def unresolved(carry):
            it, _, n_lo = carry
            return (it < 16) & (jnp.max(jnp.where(n_lo != want, 1.0, 0.0)) > 0.0)

        def two_steps(carry):
            it, lo, n_lo = carry
            return (it + 2,) + step(it + 1, step(it, (lo, n_lo)))

        _, lo, n_lo = lax.while_loop(unresolved, two_steps, (jnp.int32(head),) + lax.fori_loop(0, head, step, init))
        return lo, n_lo

    tau_hi, n_hi_ge = bisect16(hi_rows, float(top_k), jnp.full((1, tq), n_rows, F32), False)
    n_hi_gt = count16(hi_rows, tau_hi, True)
    want_lo = top_k - n_hi_gt
    tau_hi16 = jnp.broadcast_to(tau_hi.astype(I16), (kt, tq))

    def bucket_tile(jt, carry):
        rows = tile_rows(jt)
        kb_ref[rows, :] = jnp.where(hi_rows(rows) == tau_hi16, lo_rows(rows), jnp.full((), INT16_MIN, I16))
        return carry

    lax.fori_loop(0, n_tiles, bucket_tile, 0)
    tau_lo, n_kb_ge = bisect16(kb_rows, want_lo, n_hi_ge - n_hi_gt, True)
    live = (tau_hi > INT16_MIN) | (tau_lo > INT16_MIN)
    excess = jnp.max(jnp.where(live, n_kb_ge - want_lo, 0.0))

    @pl.when(excess <= 0)
    def _():
        tau_lo16 = jnp.broadcast_to(jnp.where(live, tau_lo, -INT16_MIN - 1).astype(I16), (kt, tq))

        def body(jt, carry):
            rows = tile_rows(jt)
            h16 = hi_rows(rows)
            in_bucket = jnp.where(lo_rows(rows) >= tau_lo16, zero, neg_inf)
            bias_ref[rows, :] = jnp.where(h16 > tau_hi16, zero, jnp.where(h16 == tau_hi16, in_bucket, neg_inf))
            return carry
        lax.fori_loop(0, n_tiles, body, 0)

    @pl.when(excess > 0)
    def _():
        need = want_lo - count16(kb_rows, tau_lo, True)
        live_f = jnp.where(live, 1.0, 0.0)
        tri = (lax.broadcasted_iota(I32, (kt, kt), 0)
               >= lax.broadcasted_iota(I32, (kt, kt), 1)).astype(BF16)

        def body(jt, seen):
            rows = tile_rows(jt)
            h32 = hi_rows(rows).astype(I32)
            l32 = lo_rows(rows).astype(I32)
            same_hi = h32 == tau_hi
            gt = jnp.where(h32 > tau_hi, 1.0, jnp.where(same_hi, jnp.where(l32 > tau_lo, 1.0, 0.0), 0.0))
            eq = jnp.where(same_hi, jnp.where(l32 == tau_lo, 1.0, 0.0), 0.0)
            rank = _dot(tri, eq.astype(BF16)) + seen
            sel = gt + eq * jnp.where(rank <= need, live_f, 0.0)
            bias_ref[rows, :] = jnp.where(sel > 0.0, 0.0, -jnp.inf).astype(BF16)
            return rank[kt - 1:kt, :]
        lax.fori_loop(0, n_tiles, body, jnp.zeros((1, tq), F32))

    for a in acc_refs:
        a[...] = jnp.zeros_like(a)
    heads = [slice(h * HEAD_DIM, (h + 1) * HEAD_DIM) for h in range(N_HEADS)]

    @pl.when((n_tiles & 1) == 1)
    def _():
        bias_ref[tile_rows(n_tiles), :] = jnp.full((kt, tq), -jnp.inf, BF16)

    limit_next = limit_of(q0 + tq)

    def attn_pair(jp, ms, score_next):
        tiles = (2 * jp, 2 * jp + 1)
        for i, jt in enumerate(tiles):
            rows = tile_rows(jt)
            bias = bias_ref[rows, :].astype(F32)
            for h in range(N_HEADS):
                lt_ref[i, h] = _dot_nt(k_ref[rows, heads[h]], q_ref[:, heads[h]]) + bias
        ms = list(ms)
        for i, jt in enumerate(tiles):
            if score_next:
                score_tile(jt, nxt, limit_next, wtn_ref)
            alphas, ps = [], []
            for h in range(N_HEADS):
                m_new = jnp.maximum(ms[h], jnp.max(lt_ref[i, h], axis=0, keepdims=True))
                m_safe = jnp.where(m_new == -jnp.inf, 0.0, m_new)
                ps.append(jnp.exp2(lt_ref[i, h] - m_safe).astype(BF16))
                alphas.append(jnp.exp2(ms[h] - m_safe))
                ms[h] = m_new
            for h in range(N_HEADS):
                acc_refs[h][...] = (acc_refs[h][...] * alphas[h]
                                    + _dot(vt_ref[jt, h * VT_ROWS:(h + 1) * VT_ROWS, :], ps[h]))
        return tuple(ms)

    m_init = tuple(jnp.full((1, tq), -jnp.inf, F32) for _ in range(N_HEADS))

    @pl.when(blk < n_blocks - 1)
    def _():
        build_qm(qin_ref)
        lax.fori_loop(0, n_pairs, functools.partial(attn_pair, score_next=True), m_init)

        @pl.when((q0 + 2 * tq + kt - 1) // kt > 2 * n_pairs)
        def _():
            score_tile(2 * n_pairs, nxt, limit_next, wtn_ref)

    @pl.when(blk == n_blocks - 1)
    def _():
        lax.fori_loop(0, n_pairs, functools.partial(attn_pair, score_next=False), m_init)

    for h in range(N_HEADS):
        acc = acc_refs[h][...]
        out_t = acc[:HEAD_DIM, :] * (1.0 / acc[HEAD_DIM:HEAD_DIM + 1, :])
        o_ref[:, heads[h]] = out_t.T


def _dsa_attn(q, k, vt, qi, ki, wt, bsz, seq, tq=256, kt=DSA_KEY_TILE):
    n = bsz * seq
    nq = seq // tq
    top_k = min(TOPK_MAX, seq // 4)
    once = dict(pipeline_mode=pl.Buffered(1))
    nxt_blk = lambda b, i: b * nq + jnp.minimum(i + 1, nq - 1)
    return pl.pallas_call(
        functools.partial(_dsa_attn_kernel, tq=tq, kt=kt, top_k=top_k, n_blocks=nq),
        out_shape=jax.ShapeDtypeStruct((n, WIDTH), F32),
        grid=(bsz, nq),
        in_specs=[pl.BlockSpec((tq, IDX_HEADS * IDX_DIM), lambda b, i: (b * nq + i, 0)),
                  pl.BlockSpec((IDX_HEADS, tq), lambda b, i: (0, b * nq + i)),
                  pl.BlockSpec((tq, IDX_HEADS * IDX_DIM), lambda b, i: (nxt_blk(b, i), 0)),
                  pl.BlockSpec((IDX_HEADS, tq), lambda b, i: (0, nxt_blk(b, i))),
                  pl.BlockSpec((tq, WIDTH), lambda b, i: (b * nq + i, 0)),
                  pl.BlockSpec((seq, LANES), lambda b, i: (b, 0), **once),
                  pl.BlockSpec((seq, WIDTH), lambda b, i: (b, 0), **once),
                  pl.BlockSpec((seq // kt, N_HEADS * VT_ROWS, kt), lambda b, i: (b, 0, 0), **once)],
        out_specs=pl.BlockSpec((tq, WIDTH), lambda b, i: (b * nq + i, 0)),
        scratch_shapes=[pltpu.VMEM((IDX_HEADS * tq, LANES), BF16),
                        pltpu.VMEM((2, seq, tq), I16), pltpu.VMEM((2, seq, tq), I16), pltpu.VMEM((seq, tq), I16),
                        pltpu.VMEM((seq, tq), BF16), pltpu.VMEM((2, N_HEADS, kt, tq), F32)]
                       + [pltpu.VMEM((VT_ROWS, tq), F32) for _ in range(N_HEADS)],
        compiler_params=_params("arbitrary", "arbitrary"),
        name="dsa_attn",
    )(qi, wt, qi, wt, q, ki, k, vt)


def _mem_kv_kernel(mem_ref, g_ref, wk_ref, wv_ref, gk_ref, k_out, v_out):
    mb = _rms(mem_ref[...], g_ref[...]).astype(BF16)
    kf = _dot(mb, wk_ref[...])
    for h in range(N_HEADS):
        sl = slice(h * HEAD_DIM, (h + 1) * HEAD_DIM)
        k_out[:, sl] = _rms(kf[:, sl], gk_ref[...]).astype(BF16)
    v_out[...] = _dot(mb, wv_ref[...]).astype(BF16)


def _mem_kv(mem2, gain, wk, wv, gk, n_mem):
    n, d = mem2.shape
    full = lambda a: pl.BlockSpec(a.shape, lambda b: (0,) * a.ndim)
    return pl.pallas_call(
        _mem_kv_kernel,
        out_shape=(jax.ShapeDtypeStruct((n, WIDTH), BF16), jax.ShapeDtypeStruct((n, WIDTH), BF16)),
        grid=(n // n_mem,),
        in_specs=[pl.BlockSpec((n_mem, d), lambda b: (b, 0)), full(gain), full(wk), full(wv), full(gk)],
        out_specs=(pl.BlockSpec((n_mem, WIDTH), lambda b: (b, 0)),
                   pl.BlockSpec((n_mem, WIDTH), lambda b: (b, 0))),
        compiler_params=_params("parallel"),
        name="mem_kv",
    )(mem2, gain, wk, wv, gk)


def _xattn_kernel(x_ref, oa_ref, ob_ref, wout_ref, g_ref, wq_ref, gq_ref, km_ref, vm_ref, wo_ref, o_ref):
    x1 = (x_ref[...] + _dot(oa_ref[...].astype(BF16), wout_ref[:WIDTH, :])
          + _dot(ob_ref[...].astype(BF16), wout_ref[WIDTH:, :]))
    hb = _rms(x1, g_ref[...]).astype(BF16)
    qf = _dot(hb, wq_ref[...])
    scale = HEAD_DIM ** -0.5
    outs = []
    for h in range(N_HEADS):
        sl = slice(h * HEAD_DIM, (h + 1) * HEAD_DIM)
        qh = (_rms(qf[:, sl], gq_ref[...]) * scale).astype(BF16)
        logits = _dot_nt(qh, km_ref[:, sl])
        p = jnp.exp(logits - jnp.max(logits, axis=-1, keepdims=True))
        oh = _dot(p.astype(BF16), vm_ref[:, sl])
        outs.append((oh * (1.0 / jnp.sum(p, axis=-1, keepdims=True))).astype(BF16))
    o_ref[...] = x1 + _dot(jnp.concatenate(outs, axis=1), wo_ref[...])


def _xattn(x2, oa, ob, wout, gain, wq, gq, km, vm, wo, seq, n_mem, tm=512):
    n, d = x2.shape
    per_b = seq // tm
    row = lambda w: pl.BlockSpec((tm, w), lambda i: (i, 0))
    full = lambda a: pl.BlockSpec(a.shape, lambda i: (0,) * a.ndim)
    memb = pl.BlockSpec((n_mem, WIDTH), lambda i: (i // per_b, 0))
    return pl.pallas_call(
        _xattn_kernel,
        out_shape=jax.ShapeDtypeStruct((n, d), F32),
        grid=(n // tm,),
        in_specs=[row(d), row(WIDTH), row(WIDTH), full(wout), full(gain), full(wq), full(gq),
                  memb, memb, full(wo)],
        out_specs=row(d),
        compiler_params=_params("parallel"),
        name="xattn",
    )(x2, oa, ob, wout, gain, wq, gq, km, vm, wo)


def _moe_kernel(x_ref, g_ref, wr_hi_ref, wr_lo_ref, br_ref, wgu_ref, wdn_ref, o_ref, hs_ref, gs_ref, ys_ref,
                *, tm, sub):
    lane = lax.broadcasted_iota(I32, (tm, LANES), 1)
    lane_f = lane.astype(F32)
    x = x_ref[...]
    h = _rms(x, g_ref[...])
    h_hi = h.astype(BF16)
    h_lo = (h - h_hi.astype(F32)).astype(BF16)
    lg = (_dot(h_hi, wr_hi_ref[...]) + _dot(h_lo, wr_hi_ref[...]) + _dot(h_hi, wr_lo_ref[...])
          + br_ref[...])
    first = lambda cond: jnp.min(jnp.where(cond, lane_f, 1e9), axis=-1, keepdims=True)
    gl = jnp.where(lane < N_GROUPS, lg, -jnp.inf)
    gmax = jnp.max(gl, axis=-1, keepdims=True)
    gsel = first(gl == gmax)
    g_w = 1.0 / jnp.sum(jnp.exp(gl - gmax), axis=-1, keepdims=True)
    grp_of_lane = ((lane - N_GROUPS) >> 2).astype(F32)
    in_grp = (lane >= N_GROUPS) & (lane < N_GROUPS + N_EXPERTS) & (grp_of_lane == gsel)
    el = jnp.where(in_grp, lg, -jnp.inf)
    v1 = jnp.max(el, axis=-1, keepdims=True)
    i1 = first(el == v1)
    el2 = jnp.where(lane_f == i1, -jnp.inf, el)
    v2 = jnp.max(el2, axis=-1, keepdims=True)
    i2 = first(el2 == v2)
    r = jnp.exp(v2 - v1)
    w1 = 1.0 / (1.0 + r)
    gate = jnp.where(lane_f == i1, w1 * g_w, jnp.where(lane_f == i2, r * w1 * g_w, 0.0))

    onehot = jnp.where(lane_f == gsel, 1.0, 0.0)
    t_row = lax.broadcasted_iota(I32, (tm, tm), 0)
    t_col = lax.broadcasted_iota(I32, (tm, tm), 1)
    before = _dot(jnp.where(t_row > t_col, 1.0, 0.0).astype(BF16), onehot.astype(BF16))
    counts = jnp.sum(onehot, axis=0, keepdims=True)
    lane1 = lax.broadcasted_iota(I32, (1, LANES), 1)
    ends = []
    run = jnp.zeros((1, 1), F32)
    for g in range(N_GROUPS - 1):
        run = run + jnp.sum(jnp.where(lane1 == g, counts, 0.0), axis=-1, keepdims=True)
        ends.append(run)
    start_of = sum(jnp.where(lane1 == g + 1, ends[g], 0.0) for g in range(N_GROUPS - 1))
    pos = jnp.sum(onehot * (before + start_of), axis=-1, keepdims=True)
    to_sorted_t = jnp.where(pos == t_col.astype(F32), 1.0, 0.0)
    to_sorted = to_sorted_t.T.astype(BF16)
    hs_ref[...] = _dot(to_sorted, h_hi).astype(BF16)
    g_hi, g_mid, g_lo = _split3(gate)
    gs_ref[...] = _dot(to_sorted, g_hi) + _dot(to_sorted, g_mid) + _dot(to_sorted, g_lo)

    bounds = [e_[0, 0].astype(I32) for e_ in ends]
    lane_s = lax.broadcasted_iota(I32, (sub, LANES), 1)

    def slab(j, carry):
        r0 = j * sub
        rows = pl.ds(pl.multiple_of(r0, sub), sub)
        g_first = sum((b <= r0).astype(I32) for b in bounds)
        g_last = sum((b <= r0 + sub - 1).astype(I32) for b in bounds)
        hs = hs_ref[rows, :]
        gs = gs_ref[rows, :]
        ys_ref[rows, :] = jnp.zeros((sub, ys_ref.shape[1]), F32)

        def group(g, c):
            gu = _dot(hs, wgu_ref[g])
            acts = []
            for e in range(EXP_PER_GROUP):
                ge = jnp.sum(jnp.where(lane_s == g * EXP_PER_GROUP + e + N_GROUPS, gs, 0.0), axis=-1, keepdims=True)
                up = gu[:, 2 * e * D_EXPERT:(2 * e + 1) * D_EXPERT]
                acts.append((up * jax.nn.sigmoid(up) * gu[:, (2 * e + 1) * D_EXPERT:(2 * e + 2) * D_EXPERT] * ge).astype(BF16))
            ys_ref[rows, :] += _dot(jnp.concatenate(acts, axis=1), wdn_ref[g])
            return c

        lax.fori_loop(g_first, g_last + 1, group, 0)
        return carry

    lax.fori_loop(0, tm // sub, slab, 0)
    o_ref[...] = x + _dot(to_sorted_t.astype(BF16), ys_ref[...].astype(BF16))


def _moe(x2, gain, wr_hi, wr_lo, br, wgu, wdn, tm=512, sub=128):
    n, d = x2.shape
    full = lambda a: pl.BlockSpec(a.shape, lambda i: (0,) * a.ndim)
    once = lambda a: pl.BlockSpec(a.shape, lambda i: (0,) * a.ndim, pipeline_mode=pl.Buffered(1))
    return pl.pallas_call(
        functools.partial(_moe_kernel, tm=tm, sub=sub),
        out_shape=jax.ShapeDtypeStruct((n, d), F32),
        grid=(n // tm,),
        in_specs=[pl.BlockSpec((tm, d), lambda i: (i, 0)), full(gain), full(wr_hi), full(wr_lo), full(br),
                  once(wgu), once(wdn)],
        out_specs=pl.BlockSpec((tm, d), lambda i: (i, 0)),
        scratch_shapes=[pltpu.VMEM((tm, d), BF16), pltpu.VMEM((tm, LANES), F32), pltpu.VMEM((tm, d), F32)],
        compiler_params=_params("parallel"),
        name="moe",
    )(x2, gain, wr_hi, wr_lo, br, wgu, wdn)


def _rope_tables(positions):
    pos = positions.reshape(-1).astype(F32)[:, None]

    def cs(dim):
        inv = ROPE_THETA ** (-jnp.arange(0, dim, 2, dtype=F32) / dim)
        ang = pos * inv
        return jnp.cos(ang), jnp.sin(ang)

    c, s = cs(HEAD_DIM)
    ci, si = cs(IDX_DIM)
    z = jnp.zeros_like(si)
    return (jnp.concatenate([c, c], 1), jnp.concatenate([-s, s], 1),
            jnp.concatenate([ci, ci, ci, ci], 1), jnp.concatenate([-si, z, -si, z], 1),
            jnp.concatenate([z, si, z, si], 1))


def _reorder_in_weight(w):
    a = w[:, :4 * WIDTH]
    o = 4 * WIDTH
    qlat = w[:, o:o + Q_LORA]; o += Q_LORA
    k = w[:, o:o + WIDTH]; o += WIDTH
    v = w[:, o:o + WIDTH]; o += WIDTH
    misc = w[:, o:]
    pad = jnp.zeros((w.shape[0], LANES - misc.shape[1]), w.dtype)
    return jnp.concatenate([a, k, v, qlat, misc, pad], axis=1).astype(BF16)


def _row(v):
    return v.reshape(1, -1).astype(F32)


def kernel(x, mem, positions, norm_mix, w_in, lb_raw, a_gnorm, b_qlat_gain, b_wqb, b_wqidx, b_qnorm,
           b_knorm, b_kidx_norm, w_out, norm_x, norm_mem, x_wq, x_wk, x_wv, x_wo, x_qnorm, x_knorm,
           norm_ffn, w_rg, b_rg, w_re, b_re, w_gu, w_dn):
    bsz, seq, d = x.shape
    n_mem = mem.shape[1]
    depth = w_in.shape[0]
    x2 = x.reshape(bsz * seq, d)
    mem2 = mem.reshape(bsz * n_mem, d)
    tabs = _rope_tables(positions)
    pad_lanes = lambda v: jnp.pad(v, ((0, 0), (0, LANES - v.shape[1])))
    for l in range(depth):
        u = _in_proj(x2, _row(norm_mix[l]), _reorder_in_weight(w_in[l]))
        o_a = _hgrn(u, lb_raw.astype(F32), _row(a_gnorm[l]), l, bsz, seq)
        q, k, vt, qi, ki, wt = _dsa_prep(
            u, tabs, b_wqb[l].astype(BF16), b_wqidx[l].astype(BF16), _row(b_qlat_gain[l]),
            _row(b_qnorm[l]), _row(b_knorm[l]), pad_lanes(_row(b_kidx_norm[l])))
        o_b = _dsa_attn(q, k, vt, qi, ki, wt, bsz, seq)
        km, vm = _mem_kv(mem2, _row(norm_mem[l]), x_wk[l].astype(BF16), x_wv[l].astype(BF16),
                         _row(x_knorm[l]), n_mem)
        x2 = _xattn(x2, o_a, o_b, w_out[l].astype(BF16), _row(norm_x[l]), x_wq[l].astype(BF16),
                    _row(x_qnorm[l]), km, vm, x_wo[l].astype(BF16), seq, n_mem)
        wr = pad_lanes(jnp.concatenate([w_rg[l], w_re[l].reshape(d, N_EXPERTS)], axis=1).astype(F32))
        wr_hi = wr.astype(BF16)
        wr_lo = (wr - wr_hi.astype(F32)).astype(BF16)
        br = pad_lanes(jnp.concatenate([b_rg[l], b_re[l].reshape(-1)]).reshape(1, -1).astype(F32))
        x2 = _moe(x2, _row(norm_ffn[l]), wr_hi, wr_lo, br,
                  w_gu[l].transpose(0, 2, 1, 3).reshape(N_GROUPS, d, EXP_PER_GROUP * 2 * D_EXPERT).astype(BF16),
                  w_dn[l].reshape(N_GROUPS, EXP_PER_GROUP * D_EXPERT, d).astype(BF16))
    return x2.reshape(bsz, seq, d)
```

```python
import functools

import numpy as np
import jax
import jax.numpy as jnp
from jax import lax
from jax.experimental import pallas as pl
from jax.experimental.pallas import tpu as pltpu

F32 = jnp.float32
BF16 = jnp.bfloat16
I32 = jnp.int32
I16 = jnp.int16

EPS = 1e-6
ROPE_THETA = 10000.0
CHUNK = 64
CHUNK_SHIFT = 6
SUB = 16
SUB_SHIFT = 4
N_HEADS = 4
HEAD_DIM = 128
WIDTH = N_HEADS * HEAD_DIM
Q_LORA = 256
IDX_HEADS = 16
IDX_DIM = 64
TOPK_MAX = 256
N_GROUPS = 4
EXP_PER_GROUP = 4
N_EXPERTS = N_GROUPS * EXP_PER_GROUP
D_EXPERT = 256
LANES = 128
INT_MIN = -2 ** 31
INT16_MIN = -2 ** 15
PACK = 16
LOG2E = 1.4426950408889634
VT_ROWS = HEAD_DIM + PACK
DSA_KEY_TILE = 512

U_AQ, U_AF, U_AI, U_AG = 0, 512, 1024, 1536
U_K, U_V, U_QLAT, U_MISC = 2048, 2560, 3072, 3328
U_COLS = 3456

VMEM_LIMIT = 56 * 1024 * 1024


def _rms(x, gain):
    return x * lax.rsqrt(jnp.mean(x * x, axis=-1, keepdims=True) + EPS) * gain


def _dot(a, b):
    return jnp.dot(a, b, preferred_element_type=F32)


def _rms_head(x, gain, width=HEAD_DIM):
    ones = (lax.broadcasted_iota(I32, (LANES, LANES), 0) < width).astype(BF16)
    ss = _dot((x * x).astype(BF16), ones)
    return x * lax.rsqrt(ss * (1.0 / width) + EPS) * gain


def _dot_nt(a, b):
    return lax.dot_general(a, b, (((1,), (1,)), ((), ())), preferred_element_type=F32)


def _params(*sem):
    return pltpu.CompilerParams(dimension_semantics=sem, vmem_limit_bytes=VMEM_LIMIT)


def _in_proj_kernel(x_ref, g_ref, w_ref, u_ref, *, col_chunk):
    hb = _rms(x_ref[...], g_ref[...]).astype(BF16)
    for c0 in range(0, U_COLS, col_chunk):
        u_ref[:, c0:c0 + col_chunk] = _dot(hb, w_ref[:, c0:c0 + col_chunk])


def _in_proj(x2, gain, w_bf, tm=512):
    n, d = x2.shape
    return pl.pallas_call(
        functools.partial(_in_proj_kernel, col_chunk=1152),
        out_shape=jax.ShapeDtypeStruct((n, U_COLS), F32),
        grid=(n // tm,),
        in_specs=[pl.BlockSpec((tm, d), lambda i: (i, 0)),
                  pl.BlockSpec((1, d), lambda i: (0, 0)),
                  pl.BlockSpec((d, U_COLS), lambda i: (0, 0))],
        out_specs=pl.BlockSpec((tm, U_COLS), lambda i: (i, 0)),
        compiler_params=_params("parallel"),
        name="in_proj",
    )(x2, gain, w_bf)


def _split3(x):
    hi = x.astype(BF16)
    r1 = x - hi.astype(F32)
    mid = r1.astype(BF16)
    lo = (r1 - mid.astype(F32)).astype(BF16)
    return hi, mid, lo


def _hgrn_kernel(q_ref, f_ref, i_ref, g_ref, lbraw_ref, gn_ref, e_ref, o_ref, st_ref,
                 *, layer, n_chunks, unroll):
    @pl.when(pl.program_id(2) == 0)
    def _():
        st_ref[...] = jnp.zeros_like(st_ref)

    lr = lbraw_ref[...]
    ex = jnp.exp(lr - jnp.max(lr, axis=0, keepdims=True))
    sm = ex / jnp.sum(ex, axis=0, keepdims=True)
    lb = jnp.zeros((1, HEAD_DIM), F32)
    for r in range(1, layer + 1):
        lb = lb + sm[r:r + 1, :]
    log_lb = jnp.log(lb)
    log_1mlb = jnp.log(1.0 - lb)

    row = lax.broadcasted_iota(I32, (CHUNK, HEAD_DIM), 0)
    col = lax.broadcasted_iota(I32, (CHUNK, HEAD_DIM), 1)
    row_l = row & (SUB - 1)
    row_b = row >> SUB_SHIFT
    col_b = col >> SUB_SHIFT
    tri = (lax.broadcasted_iota(I32, (CHUNK, CHUNK), 0)
           >= lax.broadcasted_iota(I32, (CHUNK, CHUNK), 1)).astype(BF16)
    n_sub = CHUNK // SUB
    zeros_c = jnp.zeros((CHUNK, HEAD_DIM), F32)

    def sub_bcast(t, s):
        t4 = t.reshape(n_sub, SUB, HEAD_DIM)
        return jnp.broadcast_to(t4[:, s:s + 1, :], (n_sub, SUB, HEAD_DIM)).reshape(CHUNK, HEAD_DIM)

    def gates(z):
        ls = jnp.minimum(z, 0.0) - jnp.log(1.0 + jnp.exp(-jnp.abs(z)))
        if layer == 0:
            return ls, (ls - z) * LOG2E
        cc = log_1mlb + ls
        log_f = jnp.maximum(log_lb, cc) + jnp.log(1.0 + jnp.exp(-jnp.abs(log_lb - cc)))
        return log_f, (cc - z) * LOG2E

    def diag_terms(q, b2, c2):
        pieces = []
        for s in range(SUB):
            d = jnp.where(row_l >= s, b2 - sub_bcast(c2, s), -jnp.inf)
            pieces.append((q * jnp.exp2(d)).astype(BF16))
        return jnp.concatenate(pieces, axis=1)

    def below_keys(b2, c2):
        kts = []
        for i in range(1, n_sub):
            r_i = b2[i * SUB:i * SUB + 1, :]
            kts.append(jnp.exp2(jnp.where(row < i * SUB, r_i - c2, -jnp.inf)))
            kts.append(zeros_c)
        return jnp.concatenate(kts, axis=0).astype(BF16)

    def below_scores(r):
        a = jnp.zeros((CHUNK, HEAD_DIM), F32)
        for i in range(1, n_sub):
            a = a + jnp.where(row_b == i, r[:, (i - 1) * LANES:i * LANES], 0.0)
        return a

    def chunks(it, carry):
        cs = range(unroll)
        rows = [pl.ds(pl.multiple_of((it * unroll + c) * CHUNK, CHUNK), CHUNK) for c in cs]
        qr = [q_ref[r, :] for r in rows]
        v = [i_ref[r, :] for r in rows]
        q = [x * jax.nn.sigmoid(x) for x in qr]
        gt = [gates(f_ref[r, :]) for r in rows]
        b2 = []
        for c in cs:
            hi, mid, lo = _split3(gt[c][0])
            b2.append((_dot(tri, hi) + _dot(tri, mid) + _dot(tri, lo)) * LOG2E)
        c2 = [b2[c] - gt[c][1] for c in cs]
        b_last = [x[CHUNK - 1:CHUNK, :] for x in b2]
        a2 = [_dot(diag_terms(q[c], b2[c], c2[c]), e_ref[...]) for c in cs]
        rr = [_dot_nt((q[c] * jnp.exp2(b2[c] - sub_bcast(b2[c], 0))).astype(BF16), below_keys(b2[c], c2[c]))
              for c in cs]
        upd = [_dot(v[c].T.astype(BF16), jnp.exp2(b_last[c] - c2[c]).astype(BF16)) for c in cs]
        qe = [(q[c] * jnp.exp2(b2[c])).astype(BF16) for c in cs]
        st = st_ref[...]
        o_inter = []
        for c in cs:
            o_inter.append(_dot_nt(qe[c], st.astype(BF16)))
            st = st * jnp.exp2(b_last[c]) + upd[c]
        st_ref[...] = st
        a = [jnp.where(col_b == row_b, a2[c], 0.0) + below_scores(rr[c]) for c in cs]
        o_intra = [_dot(a[c][:, :CHUNK].astype(BF16), v[c].astype(BF16)) for c in cs]
        for c in cs:
            g = g_ref[rows[c], :]
            o_ref[rows[c], :] = _rms(o_inter[c] + o_intra[c], gn_ref[...]) * (g * jax.nn.sigmoid(g))
        return carry

    lax.fori_loop(0, n_chunks // unroll, chunks, 0)


def _hgrn_selector():
    e = np.zeros((SUB * HEAD_DIM, LANES), np.float32)
    s_of_row = np.arange(SUB * HEAD_DIM) // HEAD_DIM
    cols = np.arange(LANES)
    e[:, :] = ((cols[None, :] % SUB) == s_of_row[:, None]) & (cols[None, :] < CHUNK)
    return jnp.asarray(e, BF16)


def _hgrn(u, lb_raw, gn, layer, bsz, seq, tb=1024):
    n = bsz * seq
    nb = seq // tb
    blk = lambda k: pl.BlockSpec((tb, HEAD_DIM), lambda b, h, j, k=k: (b * nb + j, h + N_HEADS * k))
    n_layers = lb_raw.shape[0]
    return pl.pallas_call(
        functools.partial(_hgrn_kernel, layer=layer, n_chunks=tb // CHUNK, unroll=16),
        out_shape=jax.ShapeDtypeStruct((n, WIDTH), F32),
        grid=(bsz, N_HEADS, nb),
        in_specs=[blk(0), blk(1), blk(2), blk(3),
                  pl.BlockSpec((n_layers, HEAD_DIM), lambda b, h, j: (0, h)),
                  pl.BlockSpec((1, HEAD_DIM), lambda b, h, j: (0, 0)),
                  pl.BlockSpec((SUB * HEAD_DIM, LANES), lambda b, h, j: (0, 0))],
        out_specs=pl.BlockSpec((tb, HEAD_DIM), lambda b, h, j: (b * nb + j, h)),
        scratch_shapes=[pltpu.VMEM((HEAD_DIM, HEAD_DIM), F32)],
        compiler_params=_params("parallel", "parallel", "arbitrary"),
        name="hgrn",
    )(u, u, u, u, lb_raw, gn, _hgrn_selector())


def _rope128(x, cos, sin_signed):
    return x * cos + pltpu.roll(x, 64, 1) * sin_signed


def _rope64(x, cos, sin_a, sin_b):
    return x * cos + pltpu.roll(x, 96, 1) * sin_a + pltpu.roll(x, 32, 1) * sin_b


def _dsa_prep_kernel(k_ref, v_ref, ql_ref, misc_ref, cm_ref, sm_ref, ci_ref, sia_ref, sib_ref,
                     wqb_ref, wqi_ref, gql_ref, gqn_ref, gkn_ref, gki_ref,
                     q_out, k_out, vt_out, qi_out, ki_out, wt_out):
    cm, sm = cm_ref[...], sm_ref[...]
    ci, sia, sib = ci_ref[...], sia_ref[...], sib_ref[...]
    cb = _rms(ql_ref[...], gql_ref[...]).astype(BF16)
    qf = _dot(cb, wqb_ref[...])
    scale = HEAD_DIM ** -0.5
    for h in range(N_HEADS):
        sl = slice(h * HEAD_DIM, (h + 1) * HEAD_DIM)
        qh = _rope128(_rms_head(qf[:, sl], gqn_ref[...]), cm, sm)
        q_out[:, sl] = (qh * (scale * LOG2E)).astype(BF16)
        kh = _rope128(_rms_head(k_ref[:, sl], gkn_ref[...]), cm, sm)
        k_out[:, sl] = kh.astype(BF16)
    vt = v_ref[...].T.astype(BF16)
    ones = jnp.ones((PACK, vt.shape[1]), BF16)
    for h in range(N_HEADS):
        vt_out[0, h * VT_ROWS:h * VT_ROWS + HEAD_DIM, :] = vt[h * HEAD_DIM:(h + 1) * HEAD_DIM, :]
        vt_out[0, h * VT_ROWS + HEAD_DIM:(h + 1) * VT_ROWS, :] = ones
    qi = _dot(cb, wqi_ref[...])
    for p in range(IDX_HEADS * IDX_DIM // LANES):
        sl = slice(p * LANES, (p + 1) * LANES)
        qi_out[:, sl] = _rope64(qi[:, sl], ci, sia, sib).astype(BF16)
    misc = misc_ref[...]
    kn = _rope64(_rms_head(misc, gki_ref[...], IDX_DIM), ci, sia, sib)
    ki_out[...] = (kn + pltpu.roll(kn, IDX_DIM, 1)).astype(BF16)
    wt = (misc * (IDX_HEADS ** -0.5 * IDX_DIM ** -0.5)).T
    wt_out[...] = wt[IDX_DIM:IDX_DIM + IDX_HEADS, :]


def _dsa_prep(u, tabs, wqb, wqi, gql, gqn, gkn, gki, tm=DSA_KEY_TILE):
    n = u.shape[0]
    nt = n // tm
    ublk = lambda width, col: pl.BlockSpec((tm, width), lambda i: (i, col // width))
    tab = pl.BlockSpec((tm, LANES), lambda i: (i, 0))
    full = lambda a: pl.BlockSpec(a.shape, lambda i: (0,) * a.ndim)
    return pl.pallas_call(
        _dsa_prep_kernel,
        out_shape=(jax.ShapeDtypeStruct((n, WIDTH), BF16),
                   jax.ShapeDtypeStruct((n, WIDTH), BF16),
                   jax.ShapeDtypeStruct((nt, N_HEADS * VT_ROWS, tm), BF16),
                   jax.ShapeDtypeStruct((n, IDX_HEADS * IDX_DIM), BF16),
                   jax.ShapeDtypeStruct((n, LANES), BF16),
                   jax.ShapeDtypeStruct((IDX_HEADS, n), F32)),
        grid=(nt,),
        in_specs=[ublk(WIDTH, U_K), ublk(WIDTH, U_V), ublk(Q_LORA, U_QLAT), ublk(LANES, U_MISC),
                  tab, tab, tab, tab, tab,
                  full(wqb), full(wqi), full(gql), full(gqn), full(gkn), full(gki)],
        out_specs=(pl.BlockSpec((tm, WIDTH), lambda i: (i, 0)),
                   pl.BlockSpec((tm, WIDTH), lambda i: (i, 0)),
                   pl.BlockSpec((1, N_HEADS * VT_ROWS, tm), lambda i: (i, 0, 0)),
                   pl.BlockSpec((tm, IDX_HEADS * IDX_DIM), lambda i: (i, 0)),
                   pl.BlockSpec((tm, LANES), lambda i: (i, 0)),
                   pl.BlockSpec((IDX_HEADS, tm), lambda i: (0, i))),
        compiler_params=_params("parallel"),
        name="dsa_prep",
    )(u, u, u, u, *tabs, wqb, wqi, gql, gqn, gkn, gki)


def _sortable(x):
    bits = lax.bitcast_convert_type(x, I32)
    return bits ^ ((bits >> 31) & 0x7FFFFFFF)


def _dsa_attn_kernel(qi_ref, wt_ref, qin_ref, wtn_ref, q_ref, ki_ref, k_ref, vt_ref, o_ref,
                     qm_ref, hi_ref, lo_ref, kb_ref, bias_ref, lt_ref, acc0, acc1, acc2, acc3,
                     *, tq, kt, top_k, n_blocks):
    acc_refs = (acc0, acc1, acc2, acc3)
    blk = pl.program_id(1)
    cur = blk & 1
    nxt = 1 - cur
    q0 = blk * tq
    n_tiles = (q0 + tq + kt - 1) // kt
    n_pairs = (n_tiles + 1) // 2
    sub = 128
    key_iota = lax.broadcasted_iota(I32, (sub, tq), 0)
    lane_q = lax.broadcasted_iota(I32, (tq, LANES), 1)

    def tile_rows(jt):
        return pl.ds(pl.multiple_of(jt * kt, kt), kt)

    def limit_of(q_start):
        qpos = q_start + lax.broadcasted_iota(I32, (1, tq), 1)
        return ((qpos >> CHUNK_SHIFT) + 1) << CHUNK_SHIFT

    def build_qm(src_ref):
        for h in range(IDX_HEADS):
            src = src_ref[:, (h // 2) * LANES:(h // 2 + 1) * LANES].astype(F32)
            keep = (lane_q < IDX_DIM) if h % 2 == 0 else (lane_q >= IDX_DIM)
            qm_ref[h * tq:(h + 1) * tq, :] = jnp.where(keep, src, 0.0).astype(BF16)

    def score_tile(jt, slot, limit, w_ref):
        for part in range(kt // sub):
            base = jt * kt + part * sub
            rows = pl.ds(pl.multiple_of(base, sub), sub)
            ki = ki_ref[rows, :]
            acc = jnp.zeros((sub, tq), F32)
            for h in range(IDX_HEADS):
                x = _dot_nt(ki, qm_ref[h * tq:(h + 1) * tq, :])
                acc = acc + w_ref[h:h + 1, :] * jnp.maximum(x, 0.0)
            key = jnp.where(base + key_iota < limit, _sortable(acc), INT_MIN)
            hi_ref[slot, rows, :] = (key >> 16).astype(I16)
            lo_ref[slot, rows, :] = ((key & 0xFFFF) - 32768).astype(I16)

    @pl.when(blk == 0)
    def _():
        build_qm(qi_ref)
        limit0 = limit_of(0)

        def body(jt, carry):
            score_tile(jt, 0, limit0, wt_ref)
            return carry
        lax.fori_loop(0, n_tiles, body, 0)

    one, zero = jnp.ones((), BF16), jnp.zeros((), BF16)
    neg_inf = jnp.full((), -jnp.inf, BF16)
    n_rows = (n_tiles * kt).astype(F32)
    hi_rows = lambda rows: hi_ref[cur, rows, :]
    lo_rows = lambda rows: lo_ref[cur, rows, :]
    kb_rows = lambda rows: kb_ref[rows, :]

    def count16(get, trial, strict):
        t16 = jnp.broadcast_to(trial.astype(I16), (PACK, tq))

        def body(jt, c):
            x = get(tile_rows(jt))
            parts = [jnp.zeros((PACK, tq), BF16) for _ in range(4)]
            for r in range(kt // PACK):
                xr = x[r * PACK:(r + 1) * PACK, :]
                parts[r % 4] = parts[r % 4] + jnp.where((xr > t16) if strict else (xr >= t16), one, zero)
            return c + ((parts[0] + parts[1]) + (parts[2] + parts[3])).astype(F32)
        c = lax.fori_loop(0, n_tiles, body, jnp.zeros((PACK, tq), F32))
        return jnp.sum(c, axis=0, keepdims=True)

    def bisect16(get, want, n_all, early_exit):
        def step(it, carry):
            lo, n_lo = carry
            trial = lo + lax.shift_left(jnp.int32(1), 15 - it)
            n = count16(get, trial, False)
            ok = n >= want
            return jnp.where(ok, trial, lo), jnp.where(ok, n, n_lo)

        init = (jnp.full((1, tq), INT16_MIN, I32), n_all)
        if not early_exit:
            return lax.fori_loop(0, 16, step, init)
        head = 8

        def unresolved(carry):
            it, _, n_lo = carry
            return (it < 16) & (jnp.max(jnp.where(n_lo != want, 1.0, 0.0)) > 0.0)

        def two_steps(carry):
            it, lo, n_lo = carry
            return (it + 2,) + step(it + 1, step(it, (lo, n_lo)))

        _, lo, n_lo = lax.while_loop(unresolved, two_steps, (jnp.int32(head),) + lax.fori_loop(0, head, step, init))
        return lo, n_lo

    tau_hi, n_hi_ge = bisect16(hi_rows, float(top_k), jnp.full((1, tq), n_rows, F32), False)
    n_hi_gt = count16(hi_rows, tau_hi, True)
    want_lo = top_k - n_hi_gt
    tau_hi16 = jnp.broadcast_to(tau_hi.astype(I16), (kt, tq))

    def bucket_tile(jt, carry):
        rows = tile_rows(jt)
        kb_ref[rows, :] = jnp.where(hi_rows(rows) == tau_hi16, lo_rows(rows), jnp.full((), INT16_MIN, I16))
        return carry

    lax.fori_loop(0, n_tiles, bucket_tile, 0)
    tau_lo, n_kb_ge = bisect16(kb_rows, want_lo, n_hi_ge - n_hi_gt, True)
    live = (tau_hi > INT16_MIN) | (tau_lo > INT16_MIN)
    excess = jnp.max(jnp.where(live, n_kb_ge - want_lo, 0.0))

    @pl.when(excess <= 0)
    def _():
        tau_lo16 = jnp.broadcast_to(jnp.where(live, tau_lo, -INT16_MIN - 1).astype(I16), (kt, tq))

        def body(jt, carry):
            rows = tile_rows(jt)
            h16 = hi_rows(rows)
            in_bucket = jnp.where(lo_rows(rows) >= tau_lo16, zero, neg_inf)
            bias_ref[rows, :] = jnp.where(h16 > tau_hi16, zero, jnp.where(h16 == tau_hi16, in_bucket, neg_inf))
            return carry
        lax.fori_loop(0, n_tiles, body, 0)

    @pl.when(excess > 0)
    def _():
        need = want_lo - count16(kb_rows, tau_lo, True)
        live_f = jnp.where(live, 1.0, 0.0)
        tri = (lax.broadcasted_iota(I32, (kt, kt), 0)
               >= lax.broadcasted_iota(I32, (kt, kt), 1)).astype(BF16)

        def body(jt, seen):
            rows = tile_rows(jt)
            h32 = hi_rows(rows).astype(I32)
            l32 = lo_rows(rows).astype(I32)
            same_hi = h32 == tau_hi
            gt = jnp.where(h32 > tau_hi, 1.0, jnp.where(same_hi, jnp.where(l32 > tau_lo, 1.0, 0.0), 0.0))
            eq = jnp.where(same_hi, jnp.where(l32 == tau_lo, 1.0, 0.0), 0.0)
            rank = _dot(tri, eq.astype(BF16)) + seen
            sel = gt + eq * jnp.where(rank <= need, live_f, 0.0)
            bias_ref[rows, :] = jnp.where(sel > 0.0, 0.0, -jnp.inf).astype(BF16)
            return rank[kt - 1:kt, :]
        lax.fori_loop(0, n_tiles, body, jnp.zeros((1, tq), F32))

    for a in acc_refs:
        a[...] = jnp.zeros_like(a)
    heads = [slice(h * HEAD_DIM, (h + 1) * HEAD_DIM) for h in range(N_HEADS)]

    @pl.when((n_tiles & 1) == 1)
    def _():
        bias_ref[tile_rows(n_tiles), :] = jnp.full((kt, tq), -jnp.inf, BF16)

    limit_next = limit_of(q0 + tq)

    def attn_pair(jp, ms, score_next):
        tiles = (2 * jp, 2 * jp + 1)
        for i, jt in enumerate(tiles):
            rows = tile_rows(jt)
            bias = bias_ref[rows, :].astype(F32)
            for h in range(N_HEADS):
                lt_ref[i, h] = _dot_nt(k_ref[rows, heads[h]], q_ref[:, heads[h]]) + bias
        ms = list(ms)
        for i, jt in enumerate(tiles):
            if score_next:
                score_tile(jt, nxt, limit_next, wtn_ref)
            alphas, ps = [], []
            for h in range(N_HEADS):
                m_new = jnp.maximum(ms[h], jnp.max(lt_ref[i, h], axis=0, keepdims=True))
                m_safe = jnp.where(m_new == -jnp.inf, 0.0, m_new)
                ps.append(jnp.exp2(lt_ref[i, h] - m_safe).astype(BF16))
                alphas.append(jnp.exp2(ms[h] - m_safe))
                ms[h] = m_new
            for h in range(N_HEADS):
                acc_refs[h][...] = (acc_refs[h][...] * alphas[h]
                                    + _dot(vt_ref[jt, h * VT_ROWS:(h + 1) * VT_ROWS, :], ps[h]))
        return tuple(ms)

    m_init = tuple(jnp.full((1, tq), -jnp.inf, F32) for _ in range(N_HEADS))

    @pl.when(blk < n_blocks - 1)
    def _():
        build_qm(qin_ref)
        lax.fori_loop(0, n_pairs, functools.partial(attn_pair, score_next=True), m_init)

        @pl.when((q0 + 2 * tq + kt - 1) // kt > 2 * n_pairs)
        def _():
            score_tile(2 * n_pairs, nxt, limit_next, wtn_ref)

    @pl.when(blk == n_blocks - 1)
    def _():
        lax.fori_loop(0, n_pairs, functools.partial(attn_pair, score_next=False), m_init)

    for h in range(N_HEADS):
        acc = acc_refs[h][...]
        out_t = acc[:HEAD_DIM, :] * (1.0 / acc[HEAD_DIM:HEAD_DIM + 1, :])
        o_ref[:, heads[h]] = out_t.T


def _dsa_attn(q, k, vt, qi, ki, wt, bsz, seq, tq=256, kt=DSA_KEY_TILE):
    n = bsz * seq
    nq = seq // tq
    top_k = min(TOPK_MAX, seq // 4)
    once = dict(pipeline_mode=pl.Buffered(1))
    nxt_blk = lambda b, i: b * nq + jnp.minimum(i + 1, nq - 1)
    return pl.pallas_call(
        functools.partial(_dsa_attn_kernel, tq=tq, kt=kt, top_k=top_k, n_blocks=nq),
        out_shape=jax.ShapeDtypeStruct((n, WIDTH), F32),
        grid=(bsz, nq),
        in_specs=[pl.BlockSpec((tq, IDX_HEADS * IDX_DIM), lambda b, i: (b * nq + i, 0)),
                  pl.BlockSpec((IDX_HEADS, tq), lambda b, i: (0, b * nq + i)),
                  pl.BlockSpec((tq, IDX_HEADS * IDX_DIM), lambda b, i: (nxt_blk(b, i), 0)),
                  pl.BlockSpec((IDX_HEADS, tq), lambda b, i: (0, nxt_blk(b, i))),
                  pl.BlockSpec((tq, WIDTH), lambda b, i: (b * nq + i, 0)),
                  pl.BlockSpec((seq, LANES), lambda b, i: (b, 0), **once),
                  pl.BlockSpec((seq, WIDTH), lambda b, i: (b, 0), **once),
                  pl.BlockSpec((seq // kt, N_HEADS * VT_ROWS, kt), lambda b, i: (b, 0, 0), **once)],
        out_specs=pl.BlockSpec((tq, WIDTH), lambda b, i: (b * nq + i, 0)),
        scratch_shapes=[pltpu.VMEM((IDX_HEADS * tq, LANES), BF16),
                        pltpu.VMEM((2, seq, tq), I16), pltpu.VMEM((2, seq, tq), I16), pltpu.VMEM((seq, tq), I16),
                        pltpu.VMEM((seq, tq), BF16), pltpu.VMEM((2, N_HEADS, kt, tq), F32)]
                       + [pltpu.VMEM((VT_ROWS, tq), F32) for _ in range(N_HEADS)],
        compiler_params=_params("arbitrary", "arbitrary"),
        name="dsa_attn",
    )(qi, wt, qi, wt, q, ki, k, vt)


def _mem_kv_kernel(mem_ref, g_ref, wk_ref, wv_ref, gk_ref, k_out, v_out):
    mb = _rms(mem_ref[...], g_ref[...]).astype(BF16)
    kf = _dot(mb, wk_ref[...])
    for h in range(N_HEADS):
        sl = slice(h * HEAD_DIM, (h + 1) * HEAD_DIM)
        k_out[:, sl] = _rms_head(kf[:, sl], gk_ref[...]).astype(BF16)
    v_out[...] = _dot(mb, wv_ref[...]).astype(BF16)


def _mem_kv(mem2, gain, wk, wv, gk, n_mem):
    n, d = mem2.shape
    full = lambda a: pl.BlockSpec(a.shape, lambda b: (0,) * a.ndim)
    return pl.pallas_call(
        _mem_kv_kernel,
        out_shape=(jax.ShapeDtypeStruct((n, WIDTH), BF16), jax.ShapeDtypeStruct((n, WIDTH), BF16)),
        grid=(n // n_mem,),
        in_specs=[pl.BlockSpec((n_mem, d), lambda b: (b, 0)), full(gain), full(wk), full(wv), full(gk)],
        out_specs=(pl.BlockSpec((n_mem, WIDTH), lambda b: (b, 0)),
                   pl.BlockSpec((n_mem, WIDTH), lambda b: (b, 0))),
        compiler_params=_params("parallel"),
        name="mem_kv",
    )(mem2, gain, wk, wv, gk)


def _xattn_kernel(x_ref, oa_ref, ob_ref, wout_ref, g_ref, wq_ref, gq_ref, km_ref, vm_ref, wo_ref, o_ref):
    x1 = (x_ref[...] + _dot(oa_ref[...].astype(BF16), wout_ref[:WIDTH, :])
          + _dot(ob_ref[...].astype(BF16), wout_ref[WIDTH:, :]))
    hb = _rms(x1, g_ref[...]).astype(BF16)
    qf = _dot(hb, wq_ref[...])
    scale = HEAD_DIM ** -0.5
    outs = []
    for h in range(N_HEADS):
        sl = slice(h * HEAD_DIM, (h + 1) * HEAD_DIM)
        qh = (_rms_head(qf[:, sl], gq_ref[...]) * scale).astype(BF16)
        logits = _dot_nt(qh, km_ref[:, sl])
        p = jnp.exp(logits - jnp.max(logits, axis=-1, keepdims=True))
        oh = _dot(p.astype(BF16), vm_ref[:, sl])
        outs.append((oh * (1.0 / jnp.sum(p, axis=-1, keepdims=True))).astype(BF16))
    o_ref[...] = x1 + _dot(jnp.concatenate(outs, axis=1), wo_ref[...])


def _xattn(x2, oa, ob, wout, gain, wq, gq, km, vm, wo, seq, n_mem, tm=512):
    n, d = x2.shape
    per_b = seq // tm
    row = lambda w: pl.BlockSpec((tm, w), lambda i: (i, 0))
    full = lambda a: pl.BlockSpec(a.shape, lambda i: (0,) * a.ndim)
    memb = pl.BlockSpec((n_mem, WIDTH), lambda i: (i // per_b, 0))
    return pl.pallas_call(
        _xattn_kernel,
        out_shape=jax.ShapeDtypeStruct((n, d), F32),
        grid=(n // tm,),
        in_specs=[row(d), row(WIDTH), row(WIDTH), full(wout), full(gain), full(wq), full(gq),
                  memb, memb, full(wo)],
        out_specs=row(d),
        compiler_params=_params("parallel"),
        name="xattn",
    )(x2, oa, ob, wout, gain, wq, gq, km, vm, wo)


def _moe_kernel(x_ref, g_ref, wr_hi_ref, wr_lo_ref, br_ref, wgu_ref, wdn_ref, o_ref, hs_ref, gs_ref, ys_ref,
                *, tm, sub):
    lane = lax.broadcasted_iota(I32, (tm, LANES), 1)
    lane_f = lane.astype(F32)
    x = x_ref[...]
    h = _rms(x, g_ref[...])
    h_hi = h.astype(BF16)
    h_lo = (h - h_hi.astype(F32)).astype(BF16)
    lg = (_dot(h_hi, wr_hi_ref[...]) + _dot(h_lo, wr_hi_ref[...]) + _dot(h_hi, wr_lo_ref[...])
          + br_ref[...])
    first = lambda cond: jnp.min(jnp.where(cond, lane_f, 1e9), axis=-1, keepdims=True)
    gl = jnp.where(lane < N_GROUPS, lg, -jnp.inf)
    gmax = jnp.max(gl, axis=-1, keepdims=True)
    gsel = first(gl == gmax)
    g_w = 1.0 / jnp.sum(jnp.exp(gl - gmax), axis=-1, keepdims=True)
    grp_of_lane = ((lane - N_GROUPS) >> 2).astype(F32)
    in_grp = (lane >= N_GROUPS) & (lane < N_GROUPS + N_EXPERTS) & (grp_of_lane == gsel)
    el = jnp.where(in_grp, lg, -jnp.inf)
    v1 = jnp.max(el, axis=-1, keepdims=True)
    i1 = first(el == v1)
    el2 = jnp.where(lane_f == i1, -jnp.inf, el)
    v2 = jnp.max(el2, axis=-1, keepdims=True)
    i2 = first(el2 == v2)
    r = jnp.exp(v2 - v1)
    w1 = 1.0 / (1.0 + r)
    gate = jnp.where(lane_f == i1, w1 * g_w, jnp.where(lane_f == i2, r * w1 * g_w, 0.0))

    onehot = jnp.where(lane_f == gsel, 1.0, 0.0)
    t_row = lax.broadcasted_iota(I32, (tm, tm), 0)
    t_col = lax.broadcasted_iota(I32, (tm, tm), 1)
    before = _dot(jnp.where(t_row > t_col, 1.0, 0.0).astype(BF16), onehot.astype(BF16))
    counts = jnp.sum(onehot, axis=0, keepdims=True)
    lane1 = lax.broadcasted_iota(I32, (1, LANES), 1)
    ends = []
    run = jnp.zeros((1, 1), F32)
    for g in range(N_GROUPS - 1):
        run = run + jnp.sum(jnp.where(lane1 == g, counts, 0.0), axis=-1, keepdims=True)
        ends.append(run)
    start_of = sum(jnp.where(lane1 == g + 1, ends[g], 0.0) for g in range(N_GROUPS - 1))
    pos = jnp.sum(onehot * (before + start_of), axis=-1, keepdims=True)
    to_sorted_t = jnp.where(pos == t_col.astype(F32), 1.0, 0.0)
    to_sorted = to_sorted_t.T.astype(BF16)
    hs_ref[...] = _dot(to_sorted, h_hi).astype(BF16)
    g_hi, g_mid, g_lo = _split3(gate)
    gs_ref[...] = _dot(to_sorted, g_hi) + _dot(to_sorted, g_mid) + _dot(to_sorted, g_lo)

    bounds = [e_[0, 0].astype(I32) for e_ in ends]
    lane_s = lax.broadcasted_iota(I32, (sub, LANES), 1)

    def slab(j, carry):
        r0 = j * sub
        rows = pl.ds(pl.multiple_of(r0, sub), sub)
        g_first = sum((b <= r0).astype(I32) for b in bounds)
        g_last = sum((b <= r0 + sub - 1).astype(I32) for b in bounds)
        hs = hs_ref[rows, :]
        gs = gs_ref[rows, :]
        ys_ref[rows, :] = jnp.zeros((sub, ys_ref.shape[1]), F32)

        def group(g, c):
            gu = _dot(hs, wgu_ref[g])
            acts = []
            for e in range(EXP_PER_GROUP):
                ge = jnp.sum(jnp.where(lane_s == g * EXP_PER_GROUP + e + N_GROUPS, gs, 0.0), axis=-1, keepdims=True)
                up = gu[:, 2 * e * D_EXPERT:(2 * e + 1) * D_EXPERT]
                acts.append((up * jax.nn.sigmoid(up) * gu[:, (2 * e + 1) * D_EXPERT:(2 * e + 2) * D_EXPERT] * ge).astype(BF16))
            ys_ref[rows, :] += _dot(jnp.concatenate(acts, axis=1), wdn_ref[g])
            return c

        lax.fori_loop(g_first, g_last + 1, group, 0)
        return carry

    lax.fori_loop(0, tm // sub, slab, 0)
    o_ref[...] = x + _dot(to_sorted_t.astype(BF16), ys_ref[...].astype(BF16))


def _moe(x2, gain, wr_hi, wr_lo, br, wgu, wdn, tm=512, sub=128):
    n, d = x2.shape
    full = lambda a: pl.BlockSpec(a.shape, lambda i: (0,) * a.ndim)
    once = lambda a: pl.BlockSpec(a.shape, lambda i: (0,) * a.ndim, pipeline_mode=pl.Buffered(1))
    return pl.pallas_call(
        functools.partial(_moe_kernel, tm=tm, sub=sub),
        out_shape=jax.ShapeDtypeStruct((n, d), F32),
        grid=(n // tm,),
        in_specs=[pl.BlockSpec((tm, d), lambda i: (i, 0)), full(gain), full(wr_hi), full(wr_lo), full(br),
                  once(wgu), once(wdn)],
        out_specs=pl.BlockSpec((tm, d), lambda i: (i, 0)),
        scratch_shapes=[pltpu.VMEM((tm, d), BF16), pltpu.VMEM((tm, LANES), F32), pltpu.VMEM((tm, d), F32)],
        compiler_params=_params("parallel"),
        name="moe",
    )(x2, gain, wr_hi, wr_lo, br, wgu, wdn)


def _rope_tables(positions):
    pos = positions.reshape(-1).astype(F32)[:, None]

    def cs(dim):
        inv = ROPE_THETA ** (-jnp.arange(0, dim, 2, dtype=F32) / dim)
        ang = pos * inv
        return jnp.cos(ang), jnp.sin(ang)

    c, s = cs(HEAD_DIM)
    ci, si = cs(IDX_DIM)
    z = jnp.zeros_like(si)
    return (jnp.concatenate([c, c], 1), jnp.concatenate([-s, s], 1),
            jnp.concatenate([ci, ci, ci, ci], 1), jnp.concatenate([-si, z, -si, z], 1),
            jnp.concatenate([z, si, z, si], 1))


def _reorder_in_weight(w):
    a = w[:, :4 * WIDTH]
    o = 4 * WIDTH
    qlat = w[:, o:o + Q_LORA]; o += Q_LORA
    k = w[:, o:o + WIDTH]; o += WIDTH
    v = w[:, o:o + WIDTH]; o += WIDTH
    misc = w[:, o:]
    pad = jnp.zeros((w.shape[0], LANES - misc.shape[1]), w.dtype)
    return jnp.concatenate([a, k, v, qlat, misc, pad], axis=1).astype(BF16)


def _row(v):
    return v.reshape(1, -1).astype(F32)


def kernel(x, mem, positions, norm_mix, w_in, lb_raw, a_gnorm, b_qlat_gain, b_wqb, b_wqidx, b_qnorm,
           b_knorm, b_kidx_norm, w_out, norm_x, norm_mem, x_wq, x_wk, x_wv, x_wo, x_qnorm, x_knorm,
           norm_ffn, w_rg, b_rg, w_re, b_re, w_gu, w_dn):
    bsz, seq, d = x.shape
    n_mem = mem.shape[1]
    depth = w_in.shape[0]
    x2 = x.reshape(bsz * seq, d)
    mem2 = mem.reshape(bsz * n_mem, d)
    tabs = _rope_tables(positions)
    pad_lanes = lambda v: jnp.pad(v, ((0, 0), (0, LANES - v.shape[1])))
    for l in range(depth):
        u = _in_proj(x2, _row(norm_mix[l]), _reorder_in_weight(w_in[l]))
        o_a = _hgrn(u, lb_raw.astype(F32), _row(a_gnorm[l]), l, bsz, seq)
        q, k, vt, qi, ki, wt = _dsa_prep(
            u, tabs, b_wqb[l].astype(BF16), b_wqidx[l].astype(BF16), _row(b_qlat_gain[l]),
            _row(b_qnorm[l]), _row(b_knorm[l]), pad_lanes(_row(b_kidx_norm[l])))
        o_b = _dsa_attn(q, k, vt, qi, ki, wt, bsz, seq)
        km, vm = _mem_kv(mem2, _row(norm_mem[l]), x_wk[l].astype(BF16), x_wv[l].astype(BF16),
                         _row(x_knorm[l]), n_mem)
        x2 = _xattn(x2, o_a, o_b, w_out[l].astype(BF16), _row(norm_x[l]), x_wq[l].astype(BF16),
                    _row(x_qnorm[l]), km, vm, x_wo[l].astype(BF16), seq, n_mem)
        wr = pad_lanes(jnp.concatenate([w_rg[l], w_re[l].reshape(d, N_EXPERTS)], axis=1).astype(F32))
        wr_hi = wr.astype(BF16)
        wr_lo = (wr - wr_hi.astype(F32)).astype(BF16)
        br = pad_lanes(jnp.concatenate([b_rg[l], b_re[l].reshape(-1)]).reshape(1, -1).astype(F32))
        x2 = _moe(x2, _row(norm_ffn[l]), wr_hi, wr_lo, br,
                  w_gu[l].transpose(0, 2, 1, 3).reshape(N_GROUPS, d, EXP_PER_GROUP * 2 * D_EXPERT).astype(BF16),
                  w_dn[l].reshape(N_GROUPS, EXP_PER_GROUP * D_EXPERT, d).astype(BF16))
    return x2.reshape(bsz, seq, d)
```

```python
import functools

import numpy as np
import jax
import jax.numpy as jnp
from jax import lax
from jax.experimental import pallas as pl
from jax.experimental.pallas import tpu as pltpu

F32 = jnp.float32
BF16 = jnp.bfloat16
I32 = jnp.int32
I16 = jnp.int16

EPS = 1e-6
ROPE_THETA = 10000.0
CHUNK = 64
CHUNK_SHIFT = 6
SUB = 16
SUB_SHIFT = 4
N_HEADS = 4
HEAD_DIM = 128
WIDTH = N_HEADS * HEAD_DIM
Q_LORA = 256
IDX_HEADS = 16
IDX_DIM = 64
TOPK_MAX = 256
N_GROUPS = 4
EXP_PER_GROUP = 4
N_EXPERTS = N_GROUPS * EXP_PER_GROUP
D_EXPERT = 256
LANES = 128
INT_MIN = -2 ** 31
INT16_MIN = -2 ** 15
PACK = 16
LOG2E = 1.4426950408889634
VT_ROWS = HEAD_DIM + PACK
DSA_KEY_TILE = 512

U_AQ, U_AF, U_AI, U_AG = 0, 512, 1024, 1536
U_K, U_V, U_QLAT, U_MISC = 2048, 2560, 3072, 3328
U_COLS = 3456

VMEM_LIMIT = 56 * 1024 * 1024


def _rms(x, gain):
    return x * lax.rsqrt(jnp.mean(x * x, axis=-1, keepdims=True) + EPS) * gain


def _dot(a, b):
    return jnp.dot(a, b, preferred_element_type=F32)


def _rms_head(x, gain, width=HEAD_DIM):
    ones = (lax.broadcasted_iota(I32, (LANES, LANES), 0) < width).astype(BF16)
    ss = _dot((x * x).astype(BF16), ones)
    return x * lax.rsqrt(ss * (1.0 / width) + EPS) * gain


def _dot_nt(a, b):
    return lax.dot_general(a, b, (((1,), (1,)), ((), ())), preferred_element_type=F32)


def _params(*sem):
    return pltpu.CompilerParams(dimension_semantics=sem, vmem_limit_bytes=VMEM_LIMIT)


def _in_proj_kernel(x_ref, g_ref, w_ref, u_ref, *, col_chunk):
    hb = _rms(x_ref[...], g_ref[...]).astype(BF16)
    for c0 in range(0, U_COLS, col_chunk):
        u_ref[:, c0:c0 + col_chunk] = _dot(hb, w_ref[:, c0:c0 + col_chunk])


def _in_proj(x2, gain, w_bf, tm=512):
    n, d = x2.shape
    return pl.pallas_call(
        functools.partial(_in_proj_kernel, col_chunk=1152),
        out_shape=jax.ShapeDtypeStruct((n, U_COLS), F32),
        grid=(n // tm,),
        in_specs=[pl.BlockSpec((tm, d), lambda i: (i, 0)),
                  pl.BlockSpec((1, d), lambda i: (0, 0)),
                  pl.BlockSpec((d, U_COLS), lambda i: (0, 0))],
        out_specs=pl.BlockSpec((tm, U_COLS), lambda i: (i, 0)),
        compiler_params=_params("parallel"),
        name="in_proj",
    )(x2, gain, w_bf)


def _split3(x):
    hi = x.astype(BF16)
    r1 = x - hi.astype(F32)
    mid = r1.astype(BF16)
    lo = (r1 - mid.astype(F32)).astype(BF16)
    return hi, mid, lo


def _hgrn_kernel(q_ref, f_ref, i_ref, g_ref, lbraw_ref, gn_ref, e_ref, o_ref, st_ref,
                 *, layer, n_chunks, unroll):
    @pl.when(pl.program_id(2) == 0)
    def _():
        st_ref[...] = jnp.zeros_like(st_ref)

    lr = lbraw_ref[...]
    ex = jnp.exp(lr - jnp.max(lr, axis=0, keepdims=True))
    sm = ex / jnp.sum(ex, axis=0, keepdims=True)
    lb = jnp.zeros((1, HEAD_DIM), F32)
    for r in range(1, layer + 1):
        lb = lb + sm[r:r + 1, :]
    log_lb = jnp.log(lb)
    log_1mlb = jnp.log(1.0 - lb)

    row = lax.broadcasted_iota(I32, (CHUNK, HEAD_DIM), 0)
    col = lax.broadcasted_iota(I32, (CHUNK, HEAD_DIM), 1)
    row_l = row & (SUB - 1)
    row_b = row >> SUB_SHIFT
    col_b = col >> SUB_SHIFT
    tri = (lax.broadcasted_iota(I32, (CHUNK, CHUNK), 0)
           >= lax.broadcasted_iota(I32, (CHUNK, CHUNK), 1)).astype(BF16)
    n_sub = CHUNK // SUB
    zeros_c = jnp.zeros((CHUNK, HEAD_DIM), F32)

    def sub_bcast(t, s):
        t4 = t.reshape(n_sub, SUB, HEAD_DIM)
        return jnp.broadcast_to(t4[:, s:s + 1, :], (n_sub, SUB, HEAD_DIM)).reshape(CHUNK, HEAD_DIM)

    def gates(z):
        ls = jnp.minimum(z, 0.0) - jnp.log(1.0 + jnp.exp(-jnp.abs(z)))
        if layer == 0:
            return ls, (ls - z) * LOG2E
        cc = log_1mlb + ls
        log_f = jnp.maximum(log_lb, cc) + jnp.log(1.0 + jnp.exp(-jnp.abs(log_lb - cc)))
        return log_f, (cc - z) * LOG2E

    def diag_terms(q, b2, c2):
        pieces = []
        for s in range(SUB):
            d = jnp.where(row_l >= s, b2 - sub_bcast(c2, s), -jnp.inf)
            pieces.append((q * jnp.exp2(d)).astype(BF16))
        return jnp.concatenate(pieces, axis=1)

    def below_keys(b2, c2):
        kts = []
        for i in range(1, n_sub):
            r_i = b2[i * SUB:i * SUB + 1, :]
            kts.append(jnp.exp2(jnp.where(row < i * SUB, r_i - c2, -jnp.inf)))
            kts.append(zeros_c)
        return jnp.concatenate(kts, axis=0).astype(BF16)

    def below_scores(r):
        a = jnp.zeros((CHUNK, HEAD_DIM), F32)
        for i in range(1, n_sub):
            a = a + jnp.where(row_b == i, r[:, (i - 1) * LANES:i * LANES], 0.0)
        return a

    def chunks(it, carry):
        cs = range(unroll)
        rows = [pl.ds(pl.multiple_of((it * unroll + c) * CHUNK, CHUNK), CHUNK) for c in cs]
        qr = [q_ref[r, :] for r in rows]
        v = [i_ref[r, :] for r in rows]
        q = [x * jax.nn.sigmoid(x) for x in qr]
        gt = [gates(f_ref[r, :]) for r in rows]
        b2 = []
        for c in cs:
            hi, mid, lo = _split3(gt[c][0])
            b2.append((_dot(tri, hi) + _dot(tri, mid) + _dot(tri, lo)) * LOG2E)
        c2 = [b2[c] - gt[c][1] for c in cs]
        b_last = [x[CHUNK - 1:CHUNK, :] for x in b2]
        a2 = [_dot(diag_terms(q[c], b2[c], c2[c]), e_ref[...]) for c in cs]
        rr = [_dot_nt((q[c] * jnp.exp2(b2[c] - sub_bcast(b2[c], 0))).astype(BF16), below_keys(b2[c], c2[c]))
              for c in cs]
        upd = [_dot(v[c].T.astype(BF16), jnp.exp2(b_last[c] - c2[c]).astype(BF16)) for c in cs]
        qe = [(q[c] * jnp.exp2(b2[c])).astype(BF16) for c in cs]
        st = st_ref[...]
        o_inter = []
        for c in cs:
            o_inter.append(_dot_nt(qe[c], st.astype(BF16)))
            st = st * jnp.exp2(b_last[c]) + upd[c]
        st_ref[...] = st
        a = [jnp.where(col_b == row_b, a2[c], 0.0) + below_scores(rr[c]) for c in cs]
        o_intra = [_dot(a[c][:, :CHUNK].astype(BF16), v[c].astype(BF16)) for c in cs]
        for c in cs:
            g = g_ref[rows[c], :]
            o_ref[rows[c], :] = _rms(o_inter[c] + o_intra[c], gn_ref[...]) * (g * jax.nn.sigmoid(g))
        return carry

    lax.fori_loop(0, n_chunks // unroll, chunks, 0)


def _hgrn_selector():
    e = np.zeros((SUB * HEAD_DIM, LANES), np.float32)
    s_of_row = np.arange(SUB * HEAD_DIM) // HEAD_DIM
    cols = np.arange(LANES)
    e[:, :] = ((cols[None, :] % SUB) == s_of_row[:, None]) & (cols[None, :] < CHUNK)
    return jnp.asarray(e, BF16)


def _hgrn(u, lb_raw, gn, layer, bsz, seq, tb=1024):
    n = bsz * seq
    nb = seq // tb
    blk = lambda k: pl.BlockSpec((tb, HEAD_DIM), lambda b, h, j, k=k: (b * nb + j, h + N_HEADS * k))
    n_layers = lb_raw.shape[0]
    return pl.pallas_call(
        functools.partial(_hgrn_kernel, layer=layer, n_chunks=tb // CHUNK, unroll=16),
        out_shape=jax.ShapeDtypeStruct((n, WIDTH), F32),
        grid=(bsz, N_HEADS, nb),
        in_specs=[blk(0), blk(1), blk(2), blk(3),
                  pl.BlockSpec((n_layers, HEAD_DIM), lambda b, h, j: (0, h)),
                  pl.BlockSpec((1, HEAD_DIM), lambda b, h, j: (0, 0)),
                  pl.BlockSpec((SUB * HEAD_DIM, LANES), lambda b, h, j: (0, 0))],
        out_specs=pl.BlockSpec((tb, HEAD_DIM), lambda b, h, j: (b * nb + j, h)),
        scratch_shapes=[pltpu.VMEM((HEAD_DIM, HEAD_DIM), F32)],
        compiler_params=_params("parallel", "parallel", "arbitrary"),
        name="hgrn",
    )(u, u, u, u, lb_raw, gn, _hgrn_selector())


def _rope128(x, cos, sin_signed):
    return x * cos + pltpu.roll(x, 64, 1) * sin_signed


def _rope64(x, cos, sin_a, sin_b):
    return x * cos + pltpu.roll(x, 96, 1) * sin_a + pltpu.roll(x, 32, 1) * sin_b


def _dsa_prep_kernel(k_ref, v_ref, ql_ref, misc_ref, cm_ref, sm_ref, ci_ref, sia_ref, sib_ref,
                     wqb_ref, wqi_ref, gql_ref, gqn_ref, gkn_ref, gki_ref,
                     q_out, k_out, vt_out, qi_out, ki_out, wt_out):
    cm, sm = cm_ref[...], sm_ref[...]
    ci, sia, sib = ci_ref[...], sia_ref[...], sib_ref[...]
    cb = _rms(ql_ref[...], gql_ref[...]).astype(BF16)
    qf = _dot(cb, wqb_ref[...])
    scale = HEAD_DIM ** -0.5
    for h in range(N_HEADS):
        sl = slice(h * HEAD_DIM, (h + 1) * HEAD_DIM)
        qh = _rope128(_rms_head(qf[:, sl], gqn_ref[...]), cm, sm)
        q_out[:, sl] = (qh * (scale * LOG2E)).astype(BF16)
        kh = _rope128(_rms_head(k_ref[:, sl], gkn_ref[...]), cm, sm)
        k_out[:, sl] = kh.astype(BF16)
    vt = v_ref[...].T.astype(BF16)
    ones = jnp.ones((PACK, vt.shape[1]), BF16)
    for h in range(N_HEADS):
        vt_out[0, h * VT_ROWS:h * VT_ROWS + HEAD_DIM, :] = vt[h * HEAD_DIM:(h + 1) * HEAD_DIM, :]
        vt_out[0, h * VT_ROWS + HEAD_DIM:(h + 1) * VT_ROWS, :] = ones
    qi = _dot(cb, wqi_ref[...])
    for p in range(IDX_HEADS * IDX_DIM // LANES):
        sl = slice(p * LANES, (p + 1) * LANES)
        qi_out[:, sl] = _rope64(qi[:, sl], ci, sia, sib).astype(BF16)
    misc = misc_ref[...]
    kn = _rope64(_rms_head(misc, gki_ref[...], IDX_DIM), ci, sia, sib)
    ki_out[...] = (kn + pltpu.roll(kn, IDX_DIM, 1)).astype(BF16)
    wt = (misc * (IDX_HEADS ** -0.5 * IDX_DIM ** -0.5)).T
    wt_out[...] = wt[IDX_DIM:IDX_DIM + IDX_HEADS, :]


def _dsa_prep(u, tabs, wqb, wqi, gql, gqn, gkn, gki, tm=DSA_KEY_TILE):
    n = u.shape[0]
    nt = n // tm
    ublk = lambda width, col: pl.BlockSpec((tm, width), lambda i: (i, col // width))
    tab = pl.BlockSpec((tm, LANES), lambda i: (i, 0))
    full = lambda a: pl.BlockSpec(a.shape, lambda i: (0,) * a.ndim)
    return pl.pallas_call(
        _dsa_prep_kernel,
        out_shape=(jax.ShapeDtypeStruct((n, WIDTH), BF16),
                   jax.ShapeDtypeStruct((n, WIDTH), BF16),
                   jax.ShapeDtypeStruct((nt, N_HEADS * VT_ROWS, tm), BF16),
                   jax.ShapeDtypeStruct((n, IDX_HEADS * IDX_DIM), BF16),
                   jax.ShapeDtypeStruct((n, LANES), BF16),
                   jax.ShapeDtypeStruct((IDX_HEADS, n), F32)),
        grid=(nt,),
        in_specs=[ublk(WIDTH, U_K), ublk(WIDTH, U_V), ublk(Q_LORA, U_QLAT), ublk(LANES, U_MISC),
                  tab, tab, tab, tab, tab,
                  full(wqb), full(wqi), full(gql), full(gqn), full(gkn), full(gki)],
        out_specs=(pl.BlockSpec((tm, WIDTH), lambda i: (i, 0)),
                   pl.BlockSpec((tm, WIDTH), lambda i: (i, 0)),
                   pl.BlockSpec((1, N_HEADS * VT_ROWS, tm), lambda i: (i, 0, 0)),
                   pl.BlockSpec((tm, IDX_HEADS * IDX_DIM), lambda i: (i, 0)),
                   pl.BlockSpec((tm, LANES), lambda i: (i, 0)),
                   pl.BlockSpec((IDX_HEADS, tm), lambda i: (0, i))),
        compiler_params=_params("parallel"),
        name="dsa_prep",
    )(u, u, u, u, *tabs, wqb, wqi, gql, gqn, gkn, gki)


def _sortable(x):
    bits = lax.bitcast_convert_type(x, I32)
    return bits ^ ((bits >> 31) & 0x7FFFFFFF)


def _dsa_attn_kernel(qi_ref, wt_ref, qin_ref, wtn_ref, q_ref, ki_ref, k_ref, vt_ref, o_ref,
                     qm_ref, hi_ref, lo_ref, kb_ref, bias_ref, lt_ref, acc0, acc1, acc2, acc3,
                     *, tq, kt, top_k, n_blocks):
    acc_refs = (acc0, acc1, acc2, acc3)
    blk = pl.program_id(1)
    cur = blk & 1
    nxt = 1 - cur
    q0 = blk * tq
    n_tiles = (q0 + tq + kt - 1) // kt
    n_pairs = (n_tiles + 1) // 2
    sub = 128
    key_iota = lax.broadcasted_iota(I32, (sub, tq), 0)
    lane_q = lax.broadcasted_iota(I32, (tq, LANES), 1)

    def tile_rows(jt):
        return pl.ds(pl.multiple_of(jt * kt, kt), kt)

    def limit_of(q_start):
        qpos = q_start + lax.broadcasted_iota(I32, (1, tq), 1)
        return ((qpos >> CHUNK_SHIFT) + 1) << CHUNK_SHIFT

    def build_qm(src_ref):
        for h in range(IDX_HEADS):
            src = src_ref[:, (h // 2) * LANES:(h // 2 + 1) * LANES].astype(F32)
            keep = (lane_q < IDX_DIM) if h % 2 == 0 else (lane_q >= IDX_DIM)
            qm_ref[h * tq:(h + 1) * tq, :] = jnp.where(keep, src, 0.0).astype(BF16)

    def score_tile(jt, slot, limit, w_ref):
        for part in range(kt // sub):
            base = jt * kt + part * sub
            rows = pl.ds(pl.multiple_of(base, sub), sub)
            ki = ki_ref[rows, :]
            acc = jnp.zeros((sub, tq), F32)
            for h in range(IDX_HEADS):
                x = _dot_nt(ki, qm_ref[h * tq:(h + 1) * tq, :])
                acc = acc + w_ref[h:h + 1, :] * jnp.maximum(x, 0.0)
            key = jnp.where(base + key_iota < limit, _sortable(acc), INT_MIN)
            hi_ref[slot, rows, :] = (key >> 16).astype(I16)
            lo_ref[slot, rows, :] = ((key & 0xFFFF) - 32768).astype(I16)

    @pl.when(blk == 0)
    def _():
        build_qm(qi_ref)
        limit0 = limit_of(0)

        def body(jt, carry):
            score_tile(jt, 0, limit0, wt_ref)
            return carry
        lax.fori_loop(0, n_tiles, body, 0)

    one, zero = jnp.ones((), BF16), jnp.zeros((), BF16)
    neg_inf = jnp.full((), -jnp.inf, BF16)
    n_rows = (n_tiles * kt).astype(F32)
    hi_rows = lambda rows: hi_ref[cur, rows, :]
    lo_rows = lambda rows: lo_ref[cur, rows, :]
    kb_rows = lambda rows: kb_ref[rows, :]

    def count16(get, trial, strict):
        t16 = jnp.broadcast_to(trial.astype(I16), (PACK, tq))

        def body(jt, c):
            x = get(tile_rows(jt))
            parts = [jnp.zeros((PACK, tq), BF16) for _ in range(4)]
            for r in range(kt // PACK):
                xr = x[r * PACK:(r + 1) * PACK, :]
                parts[r % 4] = parts[r % 4] + jnp.where((xr > t16) if strict else (xr >= t16), one, zero)
            return c + ((parts[0] + parts[1]) + (parts[2] + parts[3])).astype(F32)
        c = lax.fori_loop(0, n_tiles, body, jnp.zeros((PACK, tq), F32))
        return jnp.sum(c, axis=0, keepdims=True)

    def bisect16(get, want, n_all, early_exit):
        def step(it, carry):
            lo, n_lo = carry
            trial = lo + lax.shift_left(jnp.int32(1), 15 - it)
            n = count16(get, trial, False)
            ok = n >= want
            return jnp.where(ok, trial, lo), jnp.where(ok, n, n_lo)

        init = (jnp.full((1, tq), INT16_MIN, I32), n_all)
        if not early_exit:
            return lax.fori_loop(0, 16, step, init)
        head = 8

        def unresolved(carry):
            it, _, n_lo = carry
            return (it < 16) & (jnp.max(jnp.where(n_lo != want, 1.0, 0.0)) > 0.0)

        def two_steps(carry):
            it, lo, n_lo = carry
            return (it + 2,) + step(it + 1, step(it, (lo, n_lo)))

        _, lo, n_lo = lax.while_loop(unresolved, two_steps, (jnp.int32(head),) + lax.fori_loop(0, head, step, init))
        return lo, n_lo

    tau_hi, n_hi_ge = bisect16(hi_rows, float(top_k), jnp.full((1, tq), n_rows, F32), False)
    n_hi_gt = count16(hi_rows, tau_hi, True)
    want_lo = top_k - n_hi_gt
    tau_hi16 = jnp.broadcast_to(tau_hi.astype(I16), (kt, tq))

    def bucket_tile(jt, carry):
        rows = tile_rows(jt)
        kb_ref[rows, :] = jnp.where(hi_rows(rows) == tau_hi16, lo_rows(rows), jnp.full((), INT16_MIN, I16))
        return carry

    lax.fori_loop(0, n_tiles, bucket_tile, 0)
    tau_lo, n_kb_ge = bisect16(kb_rows, want_lo, n_hi_ge - n_hi_gt, True)
    live = (tau_hi > INT16_MIN) | (tau_lo > INT16_MIN)
    excess = jnp.max(jnp.where(live, n_kb_ge - want_lo, 0.0))

    @pl.when(excess <= 0)
    def _():
        tau_lo16 = jnp.broadcast_to(jnp.where(live, tau_lo, -INT16_MIN - 1).astype(I16), (kt, tq))

        def body(jt, carry):
            rows = tile_rows(jt)
            h16 = hi_rows(rows)
            in_bucket = jnp.where(lo_rows(rows) >= tau_lo16, zero, neg_inf)
            bias_ref[rows, :] = jnp.where(h16 > tau_hi16, zero, jnp.where(h16 == tau_hi16, in_bucket, neg_inf))
            return carry
        lax.fori_loop(0, n_tiles, body, 0)

    @pl.when(excess > 0)
    def _():
        need = want_lo - count16(kb_rows, tau_lo, True)
        live_f = jnp.where(live, 1.0, 0.0)
        tri = (lax.broadcasted_iota(I32, (kt, kt), 0)
               >= lax.broadcasted_iota(I32, (kt, kt), 1)).astype(BF16)

        def body(jt, seen):
            rows = tile_rows(jt)
            h32 = hi_rows(rows).astype(I32)
            l32 = lo_rows(rows).astype(I32)
            same_hi = h32 == tau_hi
            gt = jnp.where(h32 > tau_hi, 1.0, jnp.where(same_hi, jnp.where(l32 > tau_lo, 1.0, 0.0), 0.0))
            eq = jnp.where(same_hi, jnp.where(l32 == tau_lo, 1.0, 0.0), 0.0)
            rank = _dot(tri, eq.astype(BF16)) + seen
            sel = gt + eq * jnp.where(rank <= need, live_f, 0.0)
            bias_ref[rows, :] = jnp.where(sel > 0.0, 0.0, -jnp.inf).astype(BF16)
            return rank[kt - 1:kt, :]
        lax.fori_loop(0, n_tiles, body, jnp.zeros((1, tq), F32))

    for a in acc_refs:
        a[...] = jnp.zeros_like(a)
    heads = [slice(h * HEAD_DIM, (h + 1) * HEAD_DIM) for h in range(N_HEADS)]

    @pl.when((n_tiles & 1) == 1)
    def _():
        bias_ref[tile_rows(n_tiles), :] = jnp.full((kt, tq), -jnp.inf, BF16)

    limit_next = limit_of(q0 + tq)

    def attn_pair(jp, ms, score_next):
        tiles = (2 * jp, 2 * jp + 1)
        for i, jt in enumerate(tiles):
            rows = tile_rows(jt)
            bias = bias_ref[rows, :].astype(F32)
            for h in range(N_HEADS):
                lt_ref[i, h] = _dot_nt(k_ref[rows, heads[h]], q_ref[:, heads[h]]) + bias
        ms = list(ms)
        for i, jt in enumerate(tiles):
            if score_next:
                score_tile(jt, nxt, limit_next, wtn_ref)
            alphas, ps = [], []
            for h in range(N_HEADS):
                m_new = jnp.maximum(ms[h], jnp.max(lt_ref[i, h], axis=0, keepdims=True))
                m_safe = jnp.where(m_new == -jnp.inf, 0.0, m_new)
                ps.append(jnp.exp2(lt_ref[i, h] - m_safe).astype(BF16))
                alphas.append(jnp.exp2(ms[h] - m_safe))
                ms[h] = m_new
            for h in range(N_HEADS):
                acc_refs[h][...] = (acc_refs[h][...] * alphas[h]
                                    + _dot(vt_ref[jt, h * VT_ROWS:(h + 1) * VT_ROWS, :], ps[h]))
        return tuple(ms)

    m_init = tuple(jnp.full((1, tq), -jnp.inf, F32) for _ in range(N_HEADS))

    @pl.when(blk < n_blocks - 1)
    def _():
        build_qm(qin_ref)
        lax.fori_loop(0, n_pairs, functools.partial(attn_pair, score_next=True), m_init)

        @pl.when((q0 + 2 * tq + kt - 1) // kt > 2 * n_pairs)
        def _():
            score_tile(2 * n_pairs, nxt, limit_next, wtn_ref)

    @pl.when(blk == n_blocks - 1)
    def _():
        lax.fori_loop(0, n_pairs, functools.partial(attn_pair, score_next=False), m_init)

    for h in range(N_HEADS):
        acc = acc_refs[h][...]
        out_t = acc[:HEAD_DIM, :] * (1.0 / acc[HEAD_DIM:HEAD_DIM + 1, :])
        o_ref[:, heads[h]] = out_t.T


def _dsa_attn(q, k, vt, qi, ki, wt, bsz, seq, tq=256, kt=DSA_KEY_TILE):
    n = bsz * seq
    nq = seq // tq
    top_k = min(TOPK_MAX, seq // 4)
    once = dict(pipeline_mode=pl.Buffered(1))
    nxt_blk = lambda b, i: b * nq + jnp.minimum(i + 1, nq - 1)
    return pl.pallas_call(
        functools.partial(_dsa_attn_kernel, tq=tq, kt=kt, top_k=top_k, n_blocks=nq),
        out_shape=jax.ShapeDtypeStruct((n, WIDTH), F32),
        grid=(bsz, nq),
        in_specs=[pl.BlockSpec((tq, IDX_HEADS * IDX_DIM), lambda b, i: (b * nq + i, 0)),
                  pl.BlockSpec((IDX_HEADS, tq), lambda b, i: (0, b * nq + i)),
                  pl.BlockSpec((tq, IDX_HEADS * IDX_DIM), lambda b, i: (nxt_blk(b, i), 0)),
                  pl.BlockSpec((IDX_HEADS, tq), lambda b, i: (0, nxt_blk(b, i))),
                  pl.BlockSpec((tq, WIDTH), lambda b, i: (b * nq + i, 0)),
                  pl.BlockSpec((seq, LANES), lambda b, i: (b, 0), **once),
                  pl.BlockSpec((seq, WIDTH), lambda b, i: (b, 0), **once),
                  pl.BlockSpec((seq // kt, N_HEADS * VT_ROWS, kt), lambda b, i: (b, 0, 0), **once)],
        out_specs=pl.BlockSpec((tq, WIDTH), lambda b, i: (b * nq + i, 0)),
        scratch_shapes=[pltpu.VMEM((IDX_HEADS * tq, LANES), BF16),
                        pltpu.VMEM((2, seq, tq), I16), pltpu.VMEM((2, seq, tq), I16), pltpu.VMEM((seq, tq), I16),
                        pltpu.VMEM((seq, tq), BF16), pltpu.VMEM((2, N_HEADS, kt, tq), F32)]
                       + [pltpu.VMEM((VT_ROWS, tq), F32) for _ in range(N_HEADS)],
        compiler_params=_params("arbitrary", "arbitrary"),
        name="dsa_attn",
    )(qi, wt, qi, wt, q, ki, k, vt)


def _mem_kv_kernel(mem_ref, g_ref, wk_ref, wv_ref, gk_ref, k_out, v_out):
    mb = _rms(mem_ref[...], g_ref[...]).astype(BF16)
    kf = _dot(mb, wk_ref[...])
    for h in range(N_HEADS):
        sl = slice(h * HEAD_DIM, (h + 1) * HEAD_DIM)
        k_out[:, sl] = _rms_head(kf[:, sl], gk_ref[...]).astype(BF16)
    v_out[...] = _dot(mb, wv_ref[...]).astype(BF16)


def _mem_kv(mem2, gain, wk, wv, gk, n_mem):
    n, d = mem2.shape
    full = lambda a: pl.BlockSpec(a.shape, lambda b: (0,) * a.ndim)
    return pl.pallas_call(
        _mem_kv_kernel,
        out_shape=(jax.ShapeDtypeStruct((n, WIDTH), BF16), jax.ShapeDtypeStruct((n, WIDTH), BF16)),
        grid=(n // n_mem,),
        in_specs=[pl.BlockSpec((n_mem, d), lambda b: (b, 0)), full(gain), full(wk), full(wv), full(gk)],
        out_specs=(pl.BlockSpec((n_mem, WIDTH), lambda b: (b, 0)),
                   pl.BlockSpec((n_mem, WIDTH), lambda b: (b, 0))),
        compiler_params=_params("parallel"),
        name="mem_kv",
    )(mem2, gain, wk, wv, gk)


def _xattn_kernel(x_ref, oa_ref, ob_ref, wout_ref, g_ref, wq_ref, gq_ref, km_ref, vm_ref, wo_ref, o_ref):
    x1 = (x_ref[...] + _dot(oa_ref[...].astype(BF16), wout_ref[:WIDTH, :])
          + _dot(ob_ref[...].astype(BF16), wout_ref[WIDTH:, :]))
    hb = _rms(x1, g_ref[...]).astype(BF16)
    qf = _dot(hb, wq_ref[...])
    scale = HEAD_DIM ** -0.5
    outs = []
    for h in range(N_HEADS):
        sl = slice(h * HEAD_DIM, (h + 1) * HEAD_DIM)
        qh = (_rms(qf[:, sl], gq_ref[...]) * scale).astype(BF16)
        logits = _dot_nt(qh, km_ref[:, sl])
        p = jnp.exp(logits - jnp.max(logits, axis=-1, keepdims=True))
        oh = _dot(p.astype(BF16), vm_ref[:, sl])
        outs.append((oh * (1.0 / jnp.sum(p, axis=-1, keepdims=True))).astype(BF16))
    o_ref[...] = x1 + _dot(jnp.concatenate(outs, axis=1), wo_ref[...])


def _xattn(x2, oa, ob, wout, gain, wq, gq, km, vm, wo, seq, n_mem, tm=512):
    n, d = x2.shape
    per_b = seq // tm
    row = lambda w: pl.BlockSpec((tm, w), lambda i: (i, 0))
    full = lambda a: pl.BlockSpec(a.shape, lambda i: (0,) * a.ndim)
    memb = pl.BlockSpec((n_mem, WIDTH), lambda i: (i // per_b, 0))
    return pl.pallas_call(
        _xattn_kernel,
        out_shape=jax.ShapeDtypeStruct((n, d), F32),
        grid=(n // tm,),
        in_specs=[row(d), row(WIDTH), row(WIDTH), full(wout), full(gain), full(wq), full(gq),
                  memb, memb, full(wo)],
        out_specs=row(d),
        compiler_params=_params("parallel"),
        name="xattn",
    )(x2, oa, ob, wout, gain, wq, gq, km, vm, wo)


def _moe_kernel(x_ref, g_ref, wr_hi_ref, wr_lo_ref, br_ref, wgu_ref, wdn_ref, o_ref, hs_ref, gs_ref, ys_ref,
                *, tm, sub):
    lane = lax.broadcasted_iota(I32, (tm, LANES), 1)
    lane_f = lane.astype(F32)
    x = x_ref[...]
    h = _rms(x, g_ref[...])
    h_hi = h.astype(BF16)
    h_lo = (h - h_hi.astype(F32)).astype(BF16)
    lg = (_dot(h_hi, wr_hi_ref[...]) + _dot(h_lo, wr_hi_ref[...]) + _dot(h_hi, wr_lo_ref[...])
          + br_ref[...])
    first = lambda cond: jnp.min(jnp.where(cond, lane_f, 1e9), axis=-1, keepdims=True)
    gl = jnp.where(lane < N_GROUPS, lg, -jnp.inf)
    gmax = jnp.max(gl, axis=-1, keepdims=True)
    gsel = first(gl == gmax)
    g_w = 1.0 / jnp.sum(jnp.exp(gl - gmax), axis=-1, keepdims=True)
    grp_of_lane = ((lane - N_GROUPS) >> 2).astype(F32)
    in_grp = (lane >= N_GROUPS) & (lane < N_GROUPS + N_EXPERTS) & (grp_of_lane == gsel)
    el = jnp.where(in_grp, lg, -jnp.inf)
    v1 = jnp.max(el, axis=-1, keepdims=True)
    i1 = first(el == v1)
    el2 = jnp.where(lane_f == i1, -jnp.inf, el)
    v2 = jnp.max(el2, axis=-1, keepdims=True)
    i2 = first(el2 == v2)
    r = jnp.exp(v2 - v1)
    w1 = 1.0 / (1.0 + r)
    gate = jnp.where(lane_f == i1, w1 * g_w, jnp.where(lane_f == i2, r * w1 * g_w, 0.0))

    onehot = jnp.where(lane_f == gsel, 1.0, 0.0)
    t_row = lax.broadcasted_iota(I32, (tm, tm), 0)
    t_col = lax.broadcasted_iota(I32, (tm, tm), 1)
    before = _dot(jnp.where(t_row > t_col, 1.0, 0.0).astype(BF16), onehot.astype(BF16))
    counts = jnp.sum(onehot, axis=0, keepdims=True)
    lane1 = lax.broadcasted_iota(I32, (1, LANES), 1)
    ends = []
    run = jnp.zeros((1, 1), F32)
    for g in range(N_GROUPS - 1):
        run = run + jnp.sum(jnp.where(lane1 == g, counts, 0.0), axis=-1, keepdims=True)
        ends.append(run)
    start_of = sum(jnp.where(lane1 == g + 1, ends[g], 0.0) for g in range(N_GROUPS - 1))
    pos = jnp.sum(onehot * (before + start_of), axis=-1, keepdims=True)
    to_sorted_t = jnp.where(pos == t_col.astype(F32), 1.0, 0.0)
    to_sorted = to_sorted_t.T.astype(BF16)
    hs_ref[...] = _dot(to_sorted, h_hi).astype(BF16)
    g_hi, g_mid, g_lo = _split3(gate)
    gs_ref[...] = _dot(to_sorted, g_hi) + _dot(to_sorted, g_mid) + _dot(to_sorted, g_lo)

    bounds = [e_[0, 0].astype(I32) for e_ in ends]
    lane_s = lax.broadcasted_iota(I32, (sub, LANES), 1)

    def slab(j, carry):
        r0 = j * sub
        rows = pl.ds(pl.multiple_of(r0, sub), sub)
        g_first = sum((b <= r0).astype(I32) for b in bounds)
        g_last = sum((b <= r0 + sub - 1).astype(I32) for b in bounds)
        hs = hs_ref[rows, :]
        gs = gs_ref[rows, :]
        ys_ref[rows, :] = jnp.zeros((sub, ys_ref.shape[1]), F32)

        def group(g, c):
            gu = _dot(hs, wgu_ref[g])
            acts = []
            for e in range(EXP_PER_GROUP):
                ge = jnp.sum(jnp.where(lane_s == g * EXP_PER_GROUP + e + N_GROUPS, gs, 0.0), axis=-1, keepdims=True)
                up = gu[:, 2 * e * D_EXPERT:(2 * e + 1) * D_EXPERT]
                acts.append((up * jax.nn.sigmoid(up) * gu[:, (2 * e + 1) * D_EXPERT:(2 * e + 2) * D_EXPERT] * ge).astype(BF16))
            ys_ref[rows, :] += _dot(jnp.concatenate(acts, axis=1), wdn_ref[g])
            return c

        lax.fori_loop(g_first, g_last + 1, group, 0)
        return carry

    lax.fori_loop(0, tm // sub, slab, 0)
    o_ref[...] = x + _dot(to_sorted_t.astype(BF16), ys_ref[...].astype(BF16))


def _moe(x2, gain, wr_hi, wr_lo, br, wgu, wdn, tm=512, sub=128):
    n, d = x2.shape
    full = lambda a: pl.BlockSpec(a.shape, lambda i: (0,) * a.ndim)
    once = lambda a: pl.BlockSpec(a.shape, lambda i: (0,) * a.ndim, pipeline_mode=pl.Buffered(1))
    return pl.pallas_call(
        functools.partial(_moe_kernel, tm=tm, sub=sub),
        out_shape=jax.ShapeDtypeStruct((n, d), F32),
        grid=(n // tm,),
        in_specs=[pl.BlockSpec((tm, d), lambda i: (i, 0)), full(gain), full(wr_hi), full(wr_lo), full(br),
                  once(wgu), once(wdn)],
        out_specs=pl.BlockSpec((tm, d), lambda i: (i, 0)),
        scratch_shapes=[pltpu.VMEM((tm, d), BF16), pltpu.VMEM((tm, LANES), F32), pltpu.VMEM((tm, d), F32)],
        compiler_params=_params("parallel"),
        name="moe",
    )(x2, gain, wr_hi, wr_lo, br, wgu, wdn)


def _rope_tables(positions):
    pos = positions.reshape(-1).astype(F32)[:, None]

    def cs(dim):
        inv = ROPE_THETA ** (-jnp.arange(0, dim, 2, dtype=F32) / dim)
        ang = pos * inv
        return jnp.cos(ang), jnp.sin(ang)

    c, s = cs(HEAD_DIM)
    ci, si = cs(IDX_DIM)
    z = jnp.zeros_like(si)
    return (jnp.concatenate([c, c], 1), jnp.concatenate([-s, s], 1),
            jnp.concatenate([ci, ci, ci, ci], 1), jnp.concatenate([-si, z, -si, z], 1),
            jnp.concatenate([z, si, z, si], 1))


def _reorder_in_weight(w):
    a = w[:, :4 * WIDTH]
    o = 4 * WIDTH
    qlat = w[:, o:o + Q_LORA]; o += Q_LORA
    k = w[:, o:o + WIDTH]; o += WIDTH
    v = w[:, o:o + WIDTH]; o += WIDTH
    misc = w[:, o:]
    pad = jnp.zeros((w.shape[0], LANES - misc.shape[1]), w.dtype)
    return jnp.concatenate([a, k, v, qlat, misc, pad], axis=1).astype(BF16)


def _row(v):
    return v.reshape(1, -1).astype(F32)


def kernel(x, mem, positions, norm_mix, w_in, lb_raw, a_gnorm, b_qlat_gain, b_wqb, b_wqidx, b_qnorm,
           b_knorm, b_kidx_norm, w_out, norm_x, norm_mem, x_wq, x_wk, x_wv, x_wo, x_qnorm, x_knorm,
           norm_ffn, w_rg, b_rg, w_re, b_re, w_gu, w_dn):
    bsz, seq, d = x.shape
    n_mem = mem.shape[1]
    depth = w_in.shape[0]
    x2 = x.reshape(bsz * seq, d)
    mem2 = mem.reshape(bsz * n_mem, d)
    tabs = _rope_tables(positions)
    pad_lanes = lambda v: jnp.pad(v, ((0, 0), (0, LANES - v.shape[1])))
    for l in range(depth):
        u = _in_proj(x2, _row(norm_mix[l]), _reorder_in_weight(w_in[l]))
        o_a = _hgrn(u, lb_raw.astype(F32), _row(a_gnorm[l]), l, bsz, seq)
        q, k, vt, qi, ki, wt = _dsa_prep(
            u, tabs, b_wqb[l].astype(BF16), b_wqidx[l].astype(BF16), _row(b_qlat_gain[l]),
            _row(b_qnorm[l]), _row(b_knorm[l]), pad_lanes(_row(b_kidx_norm[l])))
        o_b = _dsa_attn(q, k, vt, qi, ki, wt, bsz, seq)
        km, vm = _mem_kv(mem2, _row(norm_mem[l]), x_wk[l].astype(BF16), x_wv[l].astype(BF16),
                         _row(x_knorm[l]), n_mem)
        x2 = _xattn(x2, o_a, o_b, w_out[l].astype(BF16), _row(norm_x[l]), x_wq[l].astype(BF16),
                    _row(x_qnorm[l]), km, vm, x_wo[l].astype(BF16), seq, n_mem)
        wr = pad_lanes(jnp.concatenate([w_rg[l], w_re[l].reshape(d, N_EXPERTS)], axis=1).astype(F32))
        wr_hi = wr.astype(BF16)
        wr_lo = (wr - wr_hi.astype(F32)).astype(BF16)
        br = pad_lanes(jnp.concatenate([b_rg[l], b_re[l].reshape(-1)]).reshape(1, -1).astype(F32))
        x2 = _moe(x2, _row(norm_ffn[l]), wr_hi, wr_lo, br,
                  w_gu[l].transpose(0, 2, 1, 3).reshape(N_GROUPS, d, EXP_PER_GROUP * 2 * D_EXPERT).astype(BF16),
                  w_dn[l].reshape(N_GROUPS, EXP_PER_GROUP * D_EXPERT, d).astype(BF16))
    return x2.reshape(bsz, seq, d)
```

```python
import functools

import numpy as np
import jax
import jax.numpy as jnp
from jax import lax
from jax.experimental import pallas as pl
from jax.experimental.pallas import tpu as pltpu

F32 = jnp.float32
BF16 = jnp.bfloat16
I32 = jnp.int32
I16 = jnp.int16

EPS = 1e-6
ROPE_THETA = 10000.0
CHUNK = 64
CHUNK_SHIFT = 6
SUB = 16
SUB_SHIFT = 4
N_HEADS = 4
HEAD_DIM = 128
WIDTH = N_HEADS * HEAD_DIM
Q_LORA = 256
IDX_HEADS = 16
IDX_DIM = 64
TOPK_MAX = 256
N_GROUPS = 4
EXP_PER_GROUP = 4
N_EXPERTS = N_GROUPS * EXP_PER_GROUP
D_EXPERT = 256
LANES = 128
INT_MIN = -2 ** 31
INT16_MIN = -2 ** 15
PACK = 16
LOG2E = 1.4426950408889634
VT_ROWS = HEAD_DIM + PACK
DSA_KEY_TILE = 512

U_AQ, U_AF, U_AI, U_AG = 0, 512, 1024, 1536
U_K, U_V, U_QLAT, U_MISC = 2048, 2560, 3072, 3328
U_COLS = 3456

VMEM_LIMIT = 56 * 1024 * 1024


def _rms(x, gain):
    return x * lax.rsqrt(jnp.mean(x * x, axis=-1, keepdims=True) + EPS) * gain


def _dot(a, b):
    return jnp.dot(a, b, preferred_element_type=F32)


def _rms_head(x, gain, width=HEAD_DIM):
    ones = (lax.broadcasted_iota(I32, (LANES, LANES), 0) < width).astype(BF16)
    ss = _dot((x * x).astype(BF16), ones)
    return x * lax.rsqrt(ss * (1.0 / width) + EPS) * gain


def _dot_nt(a, b):
    return lax.dot_general(a, b, (((1,), (1,)), ((), ())), preferred_element_type=F32)


def _params(*sem):
    return pltpu.CompilerParams(dimension_semantics=sem, vmem_limit_bytes=VMEM_LIMIT)


def _in_proj_kernel(x_ref, g_ref, w_ref, u_ref, *, col_chunk):
    hb = _rms(x_ref[...], g_ref[...]).astype(BF16)
    for c0 in range(0, U_COLS, col_chunk):
        u_ref[:, c0:c0 + col_chunk] = _dot(hb, w_ref[:, c0:c0 + col_chunk])


def _in_proj(x2, gain, w_bf, tm=512):
    n, d = x2.shape
    return pl.pallas_call(
        functools.partial(_in_proj_kernel, col_chunk=1152),
        out_shape=jax.ShapeDtypeStruct((n, U_COLS), F32),
        grid=(n // tm,),
        in_specs=[pl.BlockSpec((tm, d), lambda i: (i, 0)),
                  pl.BlockSpec((1, d), lambda i: (0, 0)),
                  pl.BlockSpec((d, U_COLS), lambda i: (0, 0))],
        out_specs=pl.BlockSpec((tm, U_COLS), lambda i: (i, 0)),
        compiler_params=_params("parallel"),
        name="in_proj",
    )(x2, gain, w_bf)


def _split3(x):
    hi = x.astype(BF16)
    r1 = x - hi.astype(F32)
    mid = r1.astype(BF16)
    lo = (r1 - mid.astype(F32)).astype(BF16)
    return hi, mid, lo


def _hgrn_kernel(q_ref, f_ref, i_ref, g_ref, lbraw_ref, gn_ref, e_ref, o_ref, st_ref,
                 *, layer, n_chunks, unroll):
    @pl.when(pl.program_id(2) == 0)
    def _():
        st_ref[...] = jnp.zeros_like(st_ref)

    lr = lbraw_ref[...]
    ex = jnp.exp(lr - jnp.max(lr, axis=0, keepdims=True))
    sm = ex / jnp.sum(ex, axis=0, keepdims=True)
    lb = jnp.zeros((1, HEAD_DIM), F32)
    for r in range(1, layer + 1):
        lb = lb + sm[r:r + 1, :]
    log_lb = jnp.log(lb)
    log_1mlb = jnp.log(1.0 - lb)

    row = lax.broadcasted_iota(I32, (CHUNK, HEAD_DIM), 0)
    col = lax.broadcasted_iota(I32, (CHUNK, HEAD_DIM), 1)
    row_l = row & (SUB - 1)
    row_b = row >> SUB_SHIFT
    col_b = col >> SUB_SHIFT
    tri = (lax.broadcasted_iota(I32, (CHUNK, CHUNK), 0)
           >= lax.broadcasted_iota(I32, (CHUNK, CHUNK), 1)).astype(BF16)
    n_sub = CHUNK // SUB
    zeros_c = jnp.zeros((CHUNK, HEAD_DIM), F32)

    def sub_bcast(t, s):
        t4 = t.reshape(n_sub, SUB, HEAD_DIM)
        return jnp.broadcast_to(t4[:, s:s + 1, :], (n_sub, SUB, HEAD_DIM)).reshape(CHUNK, HEAD_DIM)

    def gates(z):
        ls = jnp.minimum(z, 0.0) - jnp.log(1.0 + jnp.exp(-jnp.abs(z)))
        if layer == 0:
            return ls, (ls - z) * LOG2E
        cc = log_1mlb + ls
        log_f = jnp.maximum(log_lb, cc) + jnp.log(1.0 + jnp.exp(-jnp.abs(log_lb - cc)))
        return log_f, (cc - z) * LOG2E

    def diag_terms(q, b2, c2):
        pieces = []
        for s in range(SUB):
            d = jnp.where(row_l >= s, b2 - sub_bcast(c2, s), -jnp.inf)
            pieces.append((q * jnp.exp2(d)).astype(BF16))
        return jnp.concatenate(pieces, axis=1)

    def below_keys(b2, c2):
        kts = []
        for i in range(1, n_sub):
            r_i = b2[i * SUB:i * SUB + 1, :]
            kts.append(jnp.exp2(jnp.where(row < i * SUB, r_i - c2, -jnp.inf)))
            kts.append(zeros_c)
        return jnp.concatenate(kts, axis=0).astype(BF16)

    def below_scores(r):
        a = jnp.zeros((CHUNK, HEAD_DIM), F32)
        for i in range(1, n_sub):
            a = a + jnp.where(row_b == i, r[:, (i - 1) * LANES:i * LANES], 0.0)
        return a

    def chunks(it, carry):
        cs = range(unroll)
        rows = [pl.ds(pl.multiple_of((it * unroll + c) * CHUNK, CHUNK), CHUNK) for c in cs]
        qr = [q_ref[r, :] for r in rows]
        v = [i_ref[r, :] for r in rows]
        q = [x * jax.nn.sigmoid(x) for x in qr]
        gt = [gates(f_ref[r, :]) for r in rows]
        b2 = []
        for c in cs:
            hi, mid, lo = _split3(gt[c][0])
            b2.append((_dot(tri, hi) + _dot(tri, mid) + _dot(tri, lo)) * LOG2E)
        c2 = [b2[c] - gt[c][1] for c in cs]
        b_last = [x[CHUNK - 1:CHUNK, :] for x in b2]
        a2 = [_dot(diag_terms(q[c], b2[c], c2[c]), e_ref[...]) for c in cs]
        rr = [_dot_nt((q[c] * jnp.exp2(b2[c] - sub_bcast(b2[c], 0))).astype(BF16), below_keys(b2[c], c2[c]))
              for c in cs]
        upd = [_dot(v[c].T.astype(BF16), jnp.exp2(b_last[c] - c2[c]).astype(BF16)) for c in cs]
        qe = [(q[c] * jnp.exp2(b2[c])).astype(BF16) for c in cs]
        st = st_ref[...]
        o_inter = []
        for c in cs:
            o_inter.append(_dot_nt(qe[c], st.astype(BF16)))
            st = st * jnp.exp2(b_last[c]) + upd[c]
        st_ref[...] = st
        a = [jnp.where(col_b == row_b, a2[c], 0.0) + below_scores(rr[c]) for c in cs]
        o_intra = [_dot(a[c][:, :CHUNK].astype(BF16), v[c].astype(BF16)) for c in cs]
        for c in cs:
            g = g_ref[rows[c], :]
            o_ref[rows[c], :] = _rms(o_inter[c] + o_intra[c], gn_ref[...]) * (g * jax.nn.sigmoid(g))
        return carry

    lax.fori_loop(0, n_chunks // unroll, chunks, 0)


def _hgrn_selector():
    e = np.zeros((SUB * HEAD_DIM, LANES), np.float32)
    s_of_row = np.arange(SUB * HEAD_DIM) // HEAD_DIM
    cols = np.arange(LANES)
    e[:, :] = ((cols[None, :] % SUB) == s_of_row[:, None]) & (cols[None, :] < CHUNK)
    return jnp.asarray(e, BF16)


def _hgrn(u, lb_raw, gn, layer, bsz, seq, tb=1024):
    n = bsz * seq
    nb = seq // tb
    blk = lambda k: pl.BlockSpec((tb, HEAD_DIM), lambda b, h, j, k=k: (b * nb + j, h + N_HEADS * k))
    n_layers = lb_raw.shape[0]
    return pl.pallas_call(
        functools.partial(_hgrn_kernel, layer=layer, n_chunks=tb // CHUNK, unroll=16),
        out_shape=jax.ShapeDtypeStruct((n, WIDTH), F32),
        grid=(bsz, N_HEADS, nb),
        in_specs=[blk(0), blk(1), blk(2), blk(3),
                  pl.BlockSpec((n_layers, HEAD_DIM), lambda b, h, j: (0, h)),
                  pl.BlockSpec((1, HEAD_DIM), lambda b, h, j: (0, 0)),
                  pl.BlockSpec((SUB * HEAD_DIM, LANES), lambda b, h, j: (0, 0))],
        out_specs=pl.BlockSpec((tb, HEAD_DIM), lambda b, h, j: (b * nb + j, h)),
        scratch_shapes=[pltpu.VMEM((HEAD_DIM, HEAD_DIM), F32)],
        compiler_params=_params("parallel", "parallel", "arbitrary"),
        name="hgrn",
    )(u, u, u, u, lb_raw, gn, _hgrn_selector())


def _rope128(x, cos, sin_signed):
    return x * cos + pltpu.roll(x, 64, 1) * sin_signed


def _rope64(x, cos, sin_a, sin_b):
    return x * cos + pltpu.roll(x, 96, 1) * sin_a + pltpu.roll(x, 32, 1) * sin_b


def _dsa_prep_kernel(k_ref, v_ref, ql_ref, misc_ref, ta_ref, tb_ref,
                     wqb_ref, wqi_ref, gql_ref, gqn_ref, gkn_ref, gki_ref,
                     q_out, k_out, vt_out, qi_out, ki_out, wt_out):
    ta, tb = ta_ref[...], tb_ref[...]
    lane = lax.broadcasted_iota(I32, ta.shape, 1)
    ta_swapped = pltpu.roll(ta, 64, 1)
    cm = jnp.where(lane < 64, ta, ta_swapped)
    sm = jnp.where(lane < 64, -ta_swapped, ta)
    quarter = lane >> 5
    r32, r64, r96 = (pltpu.roll(tb, sh, 1) for sh in (32, 64, 96))
    ci = jnp.where(quarter == 0, tb, jnp.where(quarter == 1, r32, jnp.where(quarter == 2, r64, r96)))
    si = jnp.where(quarter == 0, r96, jnp.where(quarter == 1, tb, jnp.where(quarter == 2, r32, r64)))
    sia = jnp.where((lane & 32) == 0, -si, 0.0)
    sib = jnp.where((lane & 32) != 0, si, 0.0)
    cb = _rms(ql_ref[...], gql_ref[...]).astype(BF16)
    qf = _dot(cb, wqb_ref[...])
    scale = HEAD_DIM ** -0.5
    for h in range(N_HEADS):
        sl = slice(h * HEAD_DIM, (h + 1) * HEAD_DIM)
        qh = _rope128(_rms_head(qf[:, sl], gqn_ref[...]), cm, sm)
        q_out[:, sl] = (qh * (scale * LOG2E)).astype(BF16)
        kh = _rope128(_rms_head(k_ref[:, sl], gkn_ref[...]), cm, sm)
        k_out[:, sl] = kh.astype(BF16)
    vt = v_ref[...].T.astype(BF16)
    ones = jnp.ones((PACK, vt.shape[1]), BF16)
    for h in range(N_HEADS):
        vt_out[0, h * VT_ROWS:h * VT_ROWS + HEAD_DIM, :] = vt[h * HEAD_DIM:(h + 1) * HEAD_DIM, :]
        vt_out[0, h * VT_ROWS + HEAD_DIM:(h + 1) * VT_ROWS, :] = ones
    qi = _dot(cb, wqi_ref[...])
    for p in range(IDX_HEADS * IDX_DIM // LANES):
        sl = slice(p * LANES, (p + 1) * LANES)
        qi_out[:, sl] = _rope64(qi[:, sl], ci, sia, sib).astype(BF16)
    misc = misc_ref[...]
    kn = _rope64(_rms_head(misc, gki_ref[...], IDX_DIM), ci, sia, sib)
    ki_out[...] = (kn + pltpu.roll(kn, IDX_DIM, 1)).astype(BF16)
    wt = (misc * (IDX_HEADS ** -0.5 * IDX_DIM ** -0.5)).T
    wt_out[...] = wt[IDX_DIM:IDX_DIM + IDX_HEADS, :]


def _dsa_prep(u, tabs, wqb, wqi, gql, gqn, gkn, gki, tm=DSA_KEY_TILE):
    n = u.shape[0]
    nt = n // tm
    ublk = lambda width, col: pl.BlockSpec((tm, width), lambda i: (i, col // width))
    tab = pl.BlockSpec((tm, LANES), lambda i: (i, 0))
    full = lambda a: pl.BlockSpec(a.shape, lambda i: (0,) * a.ndim)
    return pl.pallas_call(
        _dsa_prep_kernel,
        out_shape=(jax.ShapeDtypeStruct((n, WIDTH), BF16),
                   jax.ShapeDtypeStruct((n, WIDTH), BF16),
                   jax.ShapeDtypeStruct((nt, N_HEADS * VT_ROWS, tm), BF16),
                   jax.ShapeDtypeStruct((n, IDX_HEADS * IDX_DIM), BF16),
                   jax.ShapeDtypeStruct((n, LANES), BF16),
                   jax.ShapeDtypeStruct((IDX_HEADS, n), F32)),
        grid=(nt,),
        in_specs=[ublk(WIDTH, U_K), ublk(WIDTH, U_V), ublk(Q_LORA, U_QLAT), ublk(LANES, U_MISC),
                  tab, tab,
                  full(wqb), full(wqi), full(gql), full(gqn), full(gkn), full(gki)],
        out_specs=(pl.BlockSpec((tm, WIDTH), lambda i: (i, 0)),
                   pl.BlockSpec((tm, WIDTH), lambda i: (i, 0)),
                   pl.BlockSpec((1, N_HEADS * VT_ROWS, tm), lambda i: (i, 0, 0)),
                   pl.BlockSpec((tm, IDX_HEADS * IDX_DIM), lambda i: (i, 0)),
                   pl.BlockSpec((tm, LANES), lambda i: (i, 0)),
                   pl.BlockSpec((IDX_HEADS, tm), lambda i: (0, i))),
        compiler_params=_params("parallel"),
        name="dsa_prep",
    )(u, u, u, u, *tabs, wqb, wqi, gql, gqn, gkn, gki)


def _sortable(x):
    bits = lax.bitcast_convert_type(x, I32)
    return bits ^ ((bits >> 31) & 0x7FFFFFFF)


def _dsa_attn_kernel(qi_ref, wt_ref, qin_ref, wtn_ref, q_ref, ki_ref, k_ref, vt_ref, o_ref,
                     qm_ref, hi_ref, lo_ref, kb_ref, bias_ref, lt_ref, acc0, acc1, acc2, acc3,
                     *, tq, kt, top_k, n_blocks):
    acc_refs = (acc0, acc1, acc2, acc3)
    blk = pl.program_id(1)
    cur = blk & 1
    nxt = 1 - cur
    q0 = blk * tq
    n_tiles = (q0 + tq + kt - 1) // kt
    n_pairs = (n_tiles + 1) // 2
    sub = 128
    key_iota = lax.broadcasted_iota(I32, (sub, tq), 0)
    lane_q = lax.broadcasted_iota(I32, (tq, LANES), 1)

    def tile_rows(jt):
        return pl.ds(pl.multiple_of(jt * kt, kt), kt)

    def limit_of(q_start):
        qpos = q_start + lax.broadcasted_iota(I32, (1, tq), 1)
        return ((qpos >> CHUNK_SHIFT) + 1) << CHUNK_SHIFT

    def build_qm(src_ref):
        for h in range(IDX_HEADS):
            src = src_ref[:, (h // 2) * LANES:(h // 2 + 1) * LANES].astype(F32)
            keep = (lane_q < IDX_DIM) if h % 2 == 0 else (lane_q >= IDX_DIM)
            qm_ref[h * tq:(h + 1) * tq, :] = jnp.where(keep, src, 0.0).astype(BF16)

    def score_tile(jt, slot, limit, w_ref):
        for part in range(kt // sub):
            base = jt * kt + part * sub
            rows = pl.ds(pl.multiple_of(base, sub), sub)
            ki = ki_ref[rows, :]
            acc = jnp.zeros((sub, tq), F32)
            for h in range(IDX_HEADS):
                x = _dot_nt(ki, qm_ref[h * tq:(h + 1) * tq, :])
                acc = acc + w_ref[h:h + 1, :] * jnp.maximum(x, 0.0)
            key = jnp.where(base + key_iota < limit, _sortable(acc), INT_MIN)
            hi_ref[slot, rows, :] = (key >> 16).astype(I16)
            lo_ref[slot, rows, :] = ((key & 0xFFFF) - 32768).astype(I16)

    @pl.when(blk == 0)
    def _():
        build_qm(qi_ref)
        limit0 = limit_of(0)

        def body(jt, carry):
            score_tile(jt, 0, limit0, wt_ref)
            return carry
        lax.fori_loop(0, n_tiles, body, 0)

    one, zero = jnp.ones((), BF16), jnp.zeros((), BF16)
    neg_inf = jnp.full((), -jnp.inf, BF16)
    n_rows = (n_tiles * kt).astype(F32)
    hi_rows = lambda rows: hi_ref[cur, rows, :]
    lo_rows = lambda rows: lo_ref[cur, rows, :]
    kb_rows = lambda rows: kb_ref[rows, :]

    def count16(get, trial, strict):
        t16 = jnp.broadcast_to(trial.astype(I16), (PACK, tq))

        def body(jt, c):
            x = get(tile_rows(jt))
            parts = [jnp.zeros((PACK, tq), BF16) for _ in range(4)]
            for r in range(kt // PACK):
                xr = x[r * PACK:(r + 1) * PACK, :]
                parts[r % 4] = parts[r % 4] + jnp.where((xr > t16) if strict else (xr >= t16), one, zero)
            return c + ((parts[0] + parts[1]) + (parts[2] + parts[3])).astype(F32)
        c = lax.fori_loop(0, n_tiles, body, jnp.zeros((PACK, tq), F32))
        return jnp.sum(c, axis=0, keepdims=True)

    def bisect16(get, want, n_all, early_exit):
        def step(it, carry):
            lo, n_lo, n_up = carry
            trial = lo + lax.shift_left(jnp.int32(1), 15 - it)
            n = count16(get, trial, False)
            ok = n >= want
            return jnp.where(ok, trial, lo), jnp.where(ok, n, n_lo), jnp.where(ok, n_up, n)

        init = (jnp.full((1, tq), INT16_MIN, I32), n_all, jnp.zeros((1, tq), F32))
        if not early_exit:
            return lax.fori_loop(0, 16, step, init)
        head = 8

        def unresolved(carry):
            it, _, n_lo, _ = carry
            return (it < 16) & (jnp.max(jnp.where(n_lo != want, 1.0, 0.0)) > 0.0)

        def two_steps(carry):
            return (carry[0] + 2,) + step(carry[0] + 1, step(carry[0], carry[1:]))

        return lax.while_loop(unresolved, two_steps, (jnp.int32(head),) + lax.fori_loop(0, head, step, init))[1:]

    tau_hi, n_hi_ge, n_hi_gt = bisect16(hi_rows, float(top_k), jnp.full((1, tq), n_rows, F32), False)
    want_lo = top_k - n_hi_gt
    tau_hi16 = jnp.broadcast_to(tau_hi.astype(I16), (kt, tq))

    def bucket_tile(jt, carry):
        rows = tile_rows(jt)
        kb_ref[rows, :] = jnp.where(hi_rows(rows) == tau_hi16, lo_rows(rows), jnp.full((), INT16_MIN, I16))
        return carry

    lax.fori_loop(0, n_tiles, bucket_tile, 0)
    tau_lo, n_kb_ge, _ = bisect16(kb_rows, want_lo, n_hi_ge - n_hi_gt, True)
    live = (tau_hi > INT16_MIN) | (tau_lo > INT16_MIN)
    excess = jnp.max(jnp.where(live, n_kb_ge - want_lo, 0.0))

    @pl.when(excess <= 0)
    def _():
        tau_lo16 = jnp.broadcast_to(jnp.where(live, tau_lo, -INT16_MIN - 1).astype(I16), (kt, tq))

        def body(jt, carry):
            rows = tile_rows(jt)
            h16 = hi_rows(rows)
            in_bucket = jnp.where(lo_rows(rows) >= tau_lo16, zero, neg_inf)
            bias_ref[rows, :] = jnp.where(h16 > tau_hi16, zero, jnp.where(h16 == tau_hi16, in_bucket, neg_inf))
            return carry
        lax.fori_loop(0, n_tiles, body, 0)

    @pl.when(excess > 0)
    def _():
        need = want_lo - count16(kb_rows, tau_lo, True)
        live_f = jnp.where(live, 1.0, 0.0)
        tri = (lax.broadcasted_iota(I32, (kt, kt), 0)
               >= lax.broadcasted_iota(I32, (kt, kt), 1)).astype(BF16)

        def body(jt, seen):
            rows = tile_rows(jt)
            h32 = hi_rows(rows).astype(I32)
            l32 = lo_rows(rows).astype(I32)
            same_hi = h32 == tau_hi
            gt = jnp.where(h32 > tau_hi, 1.0, jnp.where(same_hi, jnp.where(l32 > tau_lo, 1.0, 0.0), 0.0))
            eq = jnp.where(same_hi, jnp.where(l32 == tau_lo, 1.0, 0.0), 0.0)
            rank = _dot(tri, eq.astype(BF16)) + seen
            sel = gt + eq * jnp.where(rank <= need, live_f, 0.0)
            bias_ref[rows, :] = jnp.where(sel > 0.0, 0.0, -jnp.inf).astype(BF16)
            return rank[kt - 1:kt, :]
        lax.fori_loop(0, n_tiles, body, jnp.zeros((1, tq), F32))

    for a in acc_refs:
        a[...] = jnp.zeros_like(a)
    heads = [slice(h * HEAD_DIM, (h + 1) * HEAD_DIM) for h in range(N_HEADS)]

    @pl.when((n_tiles & 1) == 1)
    def _():
        bias_ref[tile_rows(n_tiles), :] = jnp.full((kt, tq), -jnp.inf, BF16)

    limit_next = limit_of(q0 + tq)

    def attn_pair(jp, ms, score_next):
        tiles = (2 * jp, 2 * jp + 1)
        for i, jt in enumerate(tiles):
            rows = tile_rows(jt)
            bias = bias_ref[rows, :].astype(F32)
            for h in range(N_HEADS):
                lt_ref[i, h] = _dot_nt(k_ref[rows, heads[h]], q_ref[:, heads[h]]) + bias
        ms = list(ms)
        for i, jt in enumerate(tiles):
            if score_next:
                score_tile(jt, nxt, limit_next, wtn_ref)
            alphas, ps = [], []
            for h in range(N_HEADS):
                m_new = jnp.maximum(ms[h], jnp.max(lt_ref[i, h], axis=0, keepdims=True))
                m_safe = jnp.where(m_new == -jnp.inf, 0.0, m_new)
                ps.append(jnp.exp2(lt_ref[i, h] - m_safe).astype(BF16))
                alphas.append(jnp.exp2(ms[h] - m_safe))
                ms[h] = m_new
            for h in range(N_HEADS):
                acc_refs[h][...] = (acc_refs[h][...] * alphas[h]
                                    + _dot(vt_ref[jt, h * VT_ROWS:(h + 1) * VT_ROWS, :], ps[h]))
        return tuple(ms)

    m_init = tuple(jnp.full((1, tq), -jnp.inf, F32) for _ in range(N_HEADS))

    @pl.when(blk < n_blocks - 1)
    def _():
        build_qm(qin_ref)
        lax.fori_loop(0, n_pairs, functools.partial(attn_pair, score_next=True), m_init)

        @pl.when((q0 + 2 * tq + kt - 1) // kt > 2 * n_pairs)
        def _():
            score_tile(2 * n_pairs, nxt, limit_next, wtn_ref)

    @pl.when(blk == n_blocks - 1)
    def _():
        lax.fori_loop(0, n_pairs, functools.partial(attn_pair, score_next=False), m_init)

    for h in range(N_HEADS):
        acc = acc_refs[h][...]
        out_t = acc[:HEAD_DIM, :] * (1.0 / acc[HEAD_DIM:HEAD_DIM + 1, :])
        o_ref[:, heads[h]] = out_t.T


def _dsa_attn(q, k, vt, qi, ki, wt, bsz, seq, tq=256, kt=DSA_KEY_TILE):
    n = bsz * seq
    nq = seq // tq
    top_k = min(TOPK_MAX, seq // 4)
    once = dict(pipeline_mode=pl.Buffered(1))
    nxt_blk = lambda b, i: b * nq + jnp.minimum(i + 1, nq - 1)
    return pl.pallas_call(
        functools.partial(_dsa_attn_kernel, tq=tq, kt=kt, top_k=top_k, n_blocks=nq),
        out_shape=jax.ShapeDtypeStruct((n, WIDTH), F32),
        grid=(bsz, nq),
        in_specs=[pl.BlockSpec((tq, IDX_HEADS * IDX_DIM), lambda b, i: (b * nq + i, 0)),
                  pl.BlockSpec((IDX_HEADS, tq), lambda b, i: (0, b * nq + i)),
                  pl.BlockSpec((tq, IDX_HEADS * IDX_DIM), lambda b, i: (nxt_blk(b, i), 0)),
                  pl.BlockSpec((IDX_HEADS, tq), lambda b, i: (0, nxt_blk(b, i))),
                  pl.BlockSpec((tq, WIDTH), lambda b, i: (b * nq + i, 0)),
                  pl.BlockSpec((seq, LANES), lambda b, i: (b, 0), **once),
                  pl.BlockSpec((seq, WIDTH), lambda b, i: (b, 0), **once),
                  pl.BlockSpec((seq // kt, N_HEADS * VT_ROWS, kt), lambda b, i: (b, 0, 0), **once)],
        out_specs=pl.BlockSpec((tq, WIDTH), lambda b, i: (b * nq + i, 0)),
        scratch_shapes=[pltpu.VMEM((IDX_HEADS * tq, LANES), BF16),
                        pltpu.VMEM((2, seq, tq), I16), pltpu.VMEM((2, seq, tq), I16), pltpu.VMEM((seq, tq), I16),
                        pltpu.VMEM((seq, tq), BF16), pltpu.VMEM((2, N_HEADS, kt, tq), F32)]
                       + [pltpu.VMEM((VT_ROWS, tq), F32) for _ in range(N_HEADS)],
        compiler_params=_params("arbitrary", "arbitrary"),
        name="dsa_attn",
    )(qi, wt, qi, wt, q, ki, k, vt)


def _mem_kv_kernel(mem_ref, g_ref, wk_ref, wv_ref, gk_ref, k_out, v_out):
    mb = _rms(mem_ref[...], g_ref[...]).astype(BF16)
    kf = _dot(mb, wk_ref[...])
    for h in range(N_HEADS):
        sl = slice(h * HEAD_DIM, (h + 1) * HEAD_DIM)
        k_out[:, sl] = _rms_head(kf[:, sl], gk_ref[...]).astype(BF16)
    v_out[...] = _dot(mb, wv_ref[...]).astype(BF16)


def _mem_kv(mem2, gain, wk, wv, gk, n_mem):
    n, d = mem2.shape
    full = lambda a: pl.BlockSpec(a.shape, lambda b: (0,) * a.ndim)
    return pl.pallas_call(
        _mem_kv_kernel,
        out_shape=(jax.ShapeDtypeStruct((n, WIDTH), BF16), jax.ShapeDtypeStruct((n, WIDTH), BF16)),
        grid=(n // n_mem,),
        in_specs=[pl.BlockSpec((n_mem, d), lambda b: (b, 0)), full(gain), full(wk), full(wv), full(gk)],
        out_specs=(pl.BlockSpec((n_mem, WIDTH), lambda b: (b, 0)),
                   pl.BlockSpec((n_mem, WIDTH), lambda b: (b, 0))),
        compiler_params=_params("parallel"),
        name="mem_kv",
    )(mem2, gain, wk, wv, gk)


def _xattn_kernel(x_ref, oa_ref, ob_ref, wout_ref, g_ref, wq_ref, gq_ref, km_ref, vm_ref, wo_ref, o_ref):
    x1 = (x_ref[...] + _dot(oa_ref[...].astype(BF16), wout_ref[:WIDTH, :])
          + _dot(ob_ref[...].astype(BF16), wout_ref[WIDTH:, :]))
    hb = _rms(x1, g_ref[...]).astype(BF16)
    qf = _dot(hb, wq_ref[...])
    scale = HEAD_DIM ** -0.5
    outs = []
    for h in range(N_HEADS):
        sl = slice(h * HEAD_DIM, (h + 1) * HEAD_DIM)
        qh = (_rms(qf[:, sl], gq_ref[...]) * scale).astype(BF16)
        logits = _dot_nt(qh, km_ref[:, sl])
        p = jnp.exp(logits - jnp.max(logits, axis=-1, keepdims=True))
        oh = _dot(p.astype(BF16), vm_ref[:, sl])
        outs.append((oh * (1.0 / jnp.sum(p, axis=-1, keepdims=True))).astype(BF16))
    o_ref[...] = x1 + _dot(jnp.concatenate(outs, axis=1), wo_ref[...])


def _xattn(x2, oa, ob, wout, gain, wq, gq, km, vm, wo, seq, n_mem, tm=512):
    n, d = x2.shape
    per_b = seq // tm
    row = lambda w: pl.BlockSpec((tm, w), lambda i: (i, 0))
    full = lambda a: pl.BlockSpec(a.shape, lambda i: (0,) * a.ndim)
    memb = pl.BlockSpec((n_mem, WIDTH), lambda i: (i // per_b, 0))
    return pl.pallas_call(
        _xattn_kernel,
        out_shape=jax.ShapeDtypeStruct((n, d), F32),
        grid=(n // tm,),
        in_specs=[row(d), row(WIDTH), row(WIDTH), full(wout), full(gain), full(wq), full(gq),
                  memb, memb, full(wo)],
        out_specs=row(d),
        compiler_params=_params("parallel"),
        name="xattn",
    )(x2, oa, ob, wout, gain, wq, gq, km, vm, wo)


def _moe_kernel(x_ref, g_ref, wr_hi_ref, wr_lo_ref, br_ref, wgu_ref, wdn_ref, o_ref, hs_ref, gs_ref, ys_ref,
                *, tm, sub):
    lane = lax.broadcasted_iota(I32, (tm, LANES), 1)
    lane_f = lane.astype(F32)
    x = x_ref[...]
    h = _rms(x, g_ref[...])
    h_hi = h.astype(BF16)
    h_lo = (h - h_hi.astype(F32)).astype(BF16)
    lg = (_dot(h_hi, wr_hi_ref[...]) + _dot(h_lo, wr_hi_ref[...]) + _dot(h_hi, wr_lo_ref[...])
          + br_ref[...])
    first = lambda cond: jnp.min(jnp.where(cond, lane_f, 1e9), axis=-1, keepdims=True)
    gl = jnp.where(lane < N_GROUPS, lg, -jnp.inf)
    gmax = jnp.max(gl, axis=-1, keepdims=True)
    gsel = first(gl == gmax)
    g_w = 1.0 / jnp.sum(jnp.exp(gl - gmax), axis=-1, keepdims=True)
    grp_of_lane = ((lane - N_GROUPS) >> 2).astype(F32)
    in_grp = (lane >= N_GROUPS) & (lane < N_GROUPS + N_EXPERTS) & (grp_of_lane == gsel)
    el = jnp.where(in_grp, lg, -jnp.inf)
    v1 = jnp.max(el, axis=-1, keepdims=True)
    i1 = first(el == v1)
    el2 = jnp.where(lane_f == i1, -jnp.inf, el)
    v2 = jnp.max(el2, axis=-1, keepdims=True)
    i2 = first(el2 == v2)
    r = jnp.exp(v2 - v1)
    w1 = 1.0 / (1.0 + r)
    gate = jnp.where(lane_f == i1, w1 * g_w, jnp.where(lane_f == i2, r * w1 * g_w, 0.0))

    onehot = jnp.where(lane_f == gsel, 1.0, 0.0)
    t_row = lax.broadcasted_iota(I32, (tm, tm), 0)
    t_col = lax.broadcasted_iota(I32, (tm, tm), 1)
    before = _dot(jnp.where(t_row > t_col, 1.0, 0.0).astype(BF16), onehot.astype(BF16))
    counts = jnp.sum(onehot, axis=0, keepdims=True)
    lane1 = lax.broadcasted_iota(I32, (1, LANES), 1)
    ends = []
    run = jnp.zeros((1, 1), F32)
    for g in range(N_GROUPS - 1):
        run = run + jnp.sum(jnp.where(lane1 == g, counts, 0.0), axis=-1, keepdims=True)
        ends.append(run)
    start_of = sum(jnp.where(lane1 == g + 1, ends[g], 0.0) for g in range(N_GROUPS - 1))
    pos = jnp.sum(onehot * (before + start_of), axis=-1, keepdims=True)
    to_sorted_t = jnp.where(pos == t_col.astype(F32), 1.0, 0.0)
    to_sorted = to_sorted_t.T.astype(BF16)
    hs_ref[...] = _dot(to_sorted, h_hi).astype(BF16)
    g_hi, g_mid, g_lo = _split3(gate)
    gs_ref[...] = _dot(to_sorted, g_hi) + _dot(to_sorted, g_mid) + _dot(to_sorted, g_lo)

    bounds = [e_[0, 0].astype(I32) for e_ in ends]
    lane_s = lax.broadcasted_iota(I32, (sub, LANES), 1)

    def slab(j, carry):
        r0 = j * sub
        rows = pl.ds(pl.multiple_of(r0, sub), sub)
        g_first = sum((b <= r0).astype(I32) for b in bounds)
        g_last = sum((b <= r0 + sub - 1).astype(I32) for b in bounds)
        hs = hs_ref[rows, :]
        gs = gs_ref[rows, :]
        ys_ref[rows, :] = jnp.zeros((sub, ys_ref.shape[1]), F32)

        def group(g, c):
            gu = _dot(hs, wgu_ref[g])
            acts = []
            for e in range(EXP_PER_GROUP):
                ge = jnp.sum(jnp.where(lane_s == g * EXP_PER_GROUP + e + N_GROUPS, gs, 0.0), axis=-1, keepdims=True)
                up = gu[:, 2 * e * D_EXPERT:(2 * e + 1) * D_EXPERT]
                acts.append((up * jax.nn.sigmoid(up) * gu[:, (2 * e + 1) * D_EXPERT:(2 * e + 2) * D_EXPERT] * ge).astype(BF16))
            ys_ref[rows, :] += _dot(jnp.concatenate(acts, axis=1), wdn_ref[g])
            return c

        lax.fori_loop(g_first, g_last + 1, group, 0)
        return carry

    lax.fori_loop(0, tm // sub, slab, 0)
    o_ref[...] = x + _dot(to_sorted_t.astype(BF16), ys_ref[...].astype(BF16))


def _moe(x2, gain, wr_hi, wr_lo, br, wgu, wdn, tm=512, sub=128):
    n, d = x2.shape
    full = lambda a: pl.BlockSpec(a.shape, lambda i: (0,) * a.ndim)
    once = lambda a: pl.BlockSpec(a.shape, lambda i: (0,) * a.ndim, pipeline_mode=pl.Buffered(1))
    return pl.pallas_call(
        functools.partial(_moe_kernel, tm=tm, sub=sub),
        out_shape=jax.ShapeDtypeStruct((n, d), F32),
        grid=(n // tm,),
        in_specs=[pl.BlockSpec((tm, d), lambda i: (i, 0)), full(gain), full(wr_hi), full(wr_lo), full(br),
                  once(wgu), once(wdn)],
        out_specs=pl.BlockSpec((tm, d), lambda i: (i, 0)),
        scratch_shapes=[pltpu.VMEM((tm, d), BF16), pltpu.VMEM((tm, LANES), F32), pltpu.VMEM((tm, d), F32)],
        compiler_params=_params("parallel"),
        name="moe",
    )(x2, gain, wr_hi, wr_lo, br, wgu, wdn)


def _rope_tables(positions):
    pos = positions.reshape(-1).astype(F32)[:, None]

    def cs(dim):
        inv = ROPE_THETA ** (-jnp.arange(0, dim, 2, dtype=F32) / dim)
        ang = pos * inv
        return jnp.cos(ang), jnp.sin(ang)

    c, s = cs(HEAD_DIM)
    ci, si = cs(IDX_DIM)
    return jnp.concatenate([c, s], 1), jnp.concatenate([ci, si, jnp.zeros_like(ci), jnp.zeros_like(ci)], 1)


def _reorder_in_weight(w):
    a = w[:, :4 * WIDTH]
    o = 4 * WIDTH
    qlat = w[:, o:o + Q_LORA]; o += Q_LORA
    k = w[:, o:o + WIDTH]; o += WIDTH
    v = w[:, o:o + WIDTH]; o += WIDTH
    misc = w[:, o:]
    pad = jnp.zeros((w.shape[0], LANES - misc.shape[1]), w.dtype)
    return jnp.concatenate([a, k, v, qlat, misc, pad], axis=1).astype(BF16)


def _row(v):
    return v.reshape(1, -1).astype(F32)


def kernel(x, mem, positions, norm_mix, w_in, lb_raw, a_gnorm, b_qlat_gain, b_wqb, b_wqidx, b_qnorm,
           b_knorm, b_kidx_norm, w_out, norm_x, norm_mem, x_wq, x_wk, x_wv, x_wo, x_qnorm, x_knorm,
           norm_ffn, w_rg, b_rg, w_re, b_re, w_gu, w_dn):
    bsz, seq, d = x.shape
    n_mem = mem.shape[1]
    depth = w_in.shape[0]
    x2 = x.reshape(bsz * seq, d)
    mem2 = mem.reshape(bsz * n_mem, d)
    tabs = _rope_tables(positions)
    pad_lanes = lambda v: jnp.pad(v, ((0, 0), (0, LANES - v.shape[1])))
    for l in range(depth):
        u = _in_proj(x2, _row(norm_mix[l]), _reorder_in_weight(w_in[l]))
        o_a = _hgrn(u, lb_raw.astype(F32), _row(a_gnorm[l]), l, bsz, seq)
        q, k, vt, qi, ki, wt = _dsa_prep(
            u, tabs, b_wqb[l].astype(BF16), b_wqidx[l].astype(BF16), _row(b_qlat_gain[l]),
            _row(b_qnorm[l]), _row(b_knorm[l]), pad_lanes(_row(b_kidx_norm[l])))
        o_b = _dsa_attn(q, k, vt, qi, ki, wt, bsz, seq)
        km, vm = _mem_kv(mem2, _row(norm_mem[l]), x_wk[l].astype(BF16), x_wv[l].astype(BF16),
                         _row(x_knorm[l]), n_mem)
        x2 = _xattn(x2, o_a, o_b, w_out[l].astype(BF16), _row(norm_x[l]), x_wq[l].astype(BF16),
                    _row(x_qnorm[l]), km, vm, x_wo[l].astype(BF16), seq, n_mem)
        wr = pad_lanes(jnp.concatenate([w_rg[l], w_re[l].reshape(d, N_EXPERTS)], axis=1).astype(F32))
        wr_hi = wr.astype(BF16)
        wr_lo = (wr - wr_hi.astype(F32)).astype(BF16)
        br = pad_lanes(jnp.concatenate([b_rg[l], b_re[l].reshape(-1)]).reshape(1, -1).astype(F32))
        x2 = _moe(x2, _row(norm_ffn[l]), wr_hi, wr_lo, br,
                  w_gu[l].transpose(0, 2, 1, 3).reshape(N_GROUPS, d, EXP_PER_GROUP * 2 * D_EXPERT).astype(BF16),
                  w_dn[l].reshape(N_GROUPS, EXP_PER_GROUP * D_EXPERT, d).astype(BF16))
    return x2.reshape(bsz, seq, d)
```

```python
import functools

import numpy as np
import jax
import jax.numpy as jnp
from jax import lax
from jax.experimental import pallas as pl
from jax.experimental.pallas import tpu as pltpu

F32 = jnp.float32
BF16 = jnp.bfloat16
I32 = jnp.int32
I16 = jnp.int16

EPS = 1e-6
ROPE_THETA = 10000.0
CHUNK = 64
CHUNK_SHIFT = CHUNK.bit_length() - 1
SUB = 16
SUB_SHIFT = SUB.bit_length() - 1
N_HEADS = 4
HEAD_DIM = 128
WIDTH = N_HEADS * HEAD_DIM
Q_LORA = 256
D_MODEL = 1024
IDX_HEADS = 16
IDX_DIM = 64
TOPK_MAX = 256
N_GROUPS = 4
EXP_PER_GROUP = 4
N_EXPERTS = N_GROUPS * EXP_PER_GROUP
D_EXPERT = 256
LANES = 128
INT_MIN = -2 ** 31
INT16_MIN, INT16_MAX = -2 ** 15, 2 ** 15 - 1
HALF_BITS = 16
PACK = 16
LOG2E = 1.4426950408889634
VT_ROWS = HEAD_DIM + PACK
DSA_KEY_TILE = 512
SCORE_ROWS = 128
SURE_PASSES = 8

U_K = 4 * WIDTH
U_V = U_K + WIDTH
U_QLAT = U_V + WIDTH
U_MISC = U_QLAT + Q_LORA
U_COLS = U_MISC + LANES

VMEM_LIMIT = 56 * 1024 * 1024


def _rms(x, gain):
    return x * lax.rsqrt(jnp.mean(x * x, axis=-1, keepdims=True) + EPS) * gain


def _dot(a, b):
    return jnp.dot(a, b, preferred_element_type=F32)


def _rms_head(x, gain, width=HEAD_DIM):
    ones = (lax.broadcasted_iota(I32, (LANES, LANES), 0) < width).astype(BF16)
    ss = _dot((x * x).astype(BF16), ones)
    return x * lax.rsqrt(ss * (1.0 / width) + EPS) * gain


def _dot_nt(a, b):
    return lax.dot_general(a, b, (((1,), (1,)), ((), ())), preferred_element_type=F32)


def _params(*sem):
    return pltpu.CompilerParams(dimension_semantics=sem, vmem_limit_bytes=VMEM_LIMIT)


def _in_proj_kernel(x_ref, g_ref, w_ref, u_ref, *, col_chunk):
    hb = _rms(x_ref[...], g_ref[...]).astype(BF16)
    for c0 in range(0, U_COLS, col_chunk):
        u_ref[:, c0:c0 + col_chunk] = _dot(hb, w_ref[:, c0:c0 + col_chunk])


def _in_proj(x2, gain, w_bf, tm=512):
    n, d = x2.shape
    return pl.pallas_call(
        functools.partial(_in_proj_kernel, col_chunk=1152),
        out_shape=jax.ShapeDtypeStruct((n, U_COLS), F32),
        grid=(n // tm,),
        in_specs=[pl.BlockSpec((tm, d), lambda i: (i, 0)),
                  pl.BlockSpec((1, d), lambda i: (0, 0)),
                  pl.BlockSpec((d, U_COLS), lambda i: (0, 0))],
        out_specs=pl.BlockSpec((tm, U_COLS), lambda i: (i, 0)),
        compiler_params=_params("parallel"),
        name="in_proj",
    )(x2, gain, w_bf)


def _split3(x):
    hi = x.astype(BF16)
    r1 = x - hi.astype(F32)
    mid = r1.astype(BF16)
    lo = (r1 - mid.astype(F32)).astype(BF16)
    return hi, mid, lo


def _hgrn_kernel(q_ref, f_ref, i_ref, g_ref, lbraw_ref, gn_ref, e_ref, o_ref, st_ref,
                 *, layer, n_chunks, unroll):
    @pl.when(pl.program_id(2) == 0)
    def _():
        st_ref[...] = jnp.zeros_like(st_ref)

    lr = lbraw_ref[...]
    ex = jnp.exp(lr - jnp.max(lr, axis=0, keepdims=True))
    sm = ex / jnp.sum(ex, axis=0, keepdims=True)
    lb = jnp.zeros((1, HEAD_DIM), F32)
    for r in range(1, layer + 1):
        lb = lb + sm[r:r + 1, :]
    log_lb = jnp.log(lb)
    log_1mlb = jnp.log(1.0 - lb)

    row = lax.broadcasted_iota(I32, (CHUNK, HEAD_DIM), 0)
    col = lax.broadcasted_iota(I32, (CHUNK, HEAD_DIM), 1)
    row_l = row & (SUB - 1)
    row_b = row >> SUB_SHIFT
    col_b = col >> SUB_SHIFT
    tri = (lax.broadcasted_iota(I32, (CHUNK, CHUNK), 0)
           >= lax.broadcasted_iota(I32, (CHUNK, CHUNK), 1)).astype(BF16)
    n_sub = CHUNK // SUB
    zeros_c = jnp.zeros((CHUNK, HEAD_DIM), F32)

    def sub_bcast(t, s):
        t4 = t.reshape(n_sub, SUB, HEAD_DIM)
        return jnp.broadcast_to(t4[:, s:s + 1, :], (n_sub, SUB, HEAD_DIM)).reshape(CHUNK, HEAD_DIM)

    def gates(z):
        ls = jnp.minimum(z, 0.0) - jnp.log(1.0 + jnp.exp(-jnp.abs(z)))
        if layer == 0:
            return ls, (ls - z) * LOG2E
        cc = log_1mlb + ls
        log_f = jnp.maximum(log_lb, cc) + jnp.log(1.0 + jnp.exp(-jnp.abs(log_lb - cc)))
        return log_f, (cc - z) * LOG2E

    def diag_terms(q, b2, c2):
        pieces = []
        for s in range(SUB):
            d = jnp.where(row_l >= s, b2 - sub_bcast(c2, s), -jnp.inf)
            pieces.append((q * jnp.exp2(d)).astype(BF16))
        return jnp.concatenate(pieces, axis=1)

    def below_keys(b2, c2):
        kts = []
        for i in range(1, n_sub):
            r_i = b2[i * SUB:i * SUB + 1, :]
            kts.append(jnp.exp2(jnp.where(row < i * SUB, r_i - c2, -jnp.inf)))
            kts.append(zeros_c)
        return jnp.concatenate(kts, axis=0).astype(BF16)

    def below_scores(r):
        a = jnp.zeros((CHUNK, HEAD_DIM), F32)
        for i in range(1, n_sub):
            a = a + jnp.where(row_b == i, r[:, (i - 1) * LANES:i * LANES], 0.0)
        return a

    def chunks(it, carry):
        cs = range(unroll)
        rows = [pl.ds(pl.multiple_of((it * unroll + c) * CHUNK, CHUNK), CHUNK) for c in cs]
        qr = [q_ref[r, :] for r in rows]
        v = [i_ref[r, :] for r in rows]
        q = [x * jax.nn.sigmoid(x) for x in qr]
        gt = [gates(f_ref[r, :]) for r in rows]
        b2 = []
        for c in cs:
            hi, mid, lo = _split3(gt[c][0])
            b2.append((_dot(tri, hi) + _dot(tri, mid) + _dot(tri, lo)) * LOG2E)
        c2 = [b2[c] - gt[c][1] for c in cs]
        b_last = [x[CHUNK - 1:CHUNK, :] for x in b2]
        a2 = [_dot(diag_terms(q[c], b2[c], c2[c]), e_ref[...]) for c in cs]
        rr = [_dot_nt((q[c] * jnp.exp2(b2[c] - sub_bcast(b2[c], 0))).astype(BF16), below_keys(b2[c], c2[c]))
              for c in cs]
        upd = [_dot(v[c].T.astype(BF16), jnp.exp2(b_last[c] - c2[c]).astype(BF16)) for c in cs]
        qe = [(q[c] * jnp.exp2(b2[c])).astype(BF16) for c in cs]
        st = st_ref[...]
        o_inter = []
        for c in cs:
            o_inter.append(_dot_nt(qe[c], st.astype(BF16)))
            st = st * jnp.exp2(b_last[c]) + upd[c]
        st_ref[...] = st
        a = [jnp.where(col_b == row_b, a2[c], 0.0) + below_scores(rr[c]) for c in cs]
        o_intra = [_dot(a[c][:, :CHUNK].astype(BF16), v[c].astype(BF16)) for c in cs]
        for c in cs:
            g = g_ref[rows[c], :]
            o_ref[rows[c], :] = _rms(o_inter[c] + o_intra[c], gn_ref[...]) * (g * jax.nn.sigmoid(g))
        return carry

    lax.fori_loop(0, n_chunks // unroll, chunks, 0)


def _hgrn_selector():
    e = np.zeros((SUB * HEAD_DIM, LANES), np.float32)
    s_of_row = np.arange(SUB * HEAD_DIM) // HEAD_DIM
    cols = np.arange(LANES)
    e[:, :] = ((cols[None, :] % SUB) == s_of_row[:, None]) & (cols[None, :] < CHUNK)
    return jnp.asarray(e, BF16)


def _hgrn(u, lb_raw, gn, layer, bsz, seq, tb=1024):
    n = bsz * seq
    nb = seq // tb
    blk = lambda k: pl.BlockSpec((tb, HEAD_DIM), lambda b, h, j, k=k: (b * nb + j, h + N_HEADS * k))
    n_layers = lb_raw.shape[0]
    return pl.pallas_call(
        functools.partial(_hgrn_kernel, layer=layer, n_chunks=tb // CHUNK, unroll=16),
        out_shape=jax.ShapeDtypeStruct((n, WIDTH), F32),
        grid=(bsz, N_HEADS, nb),
        in_specs=[blk(0), blk(1), blk(2), blk(3),
                  pl.BlockSpec((n_layers, HEAD_DIM), lambda b, h, j: (0, h)),
                  pl.BlockSpec((1, HEAD_DIM), lambda b, h, j: (0, 0)),
                  pl.BlockSpec((SUB * HEAD_DIM, LANES), lambda b, h, j: (0, 0))],
        out_specs=pl.BlockSpec((tb, HEAD_DIM), lambda b, h, j: (b * nb + j, h)),
        scratch_shapes=[pltpu.VMEM((HEAD_DIM, HEAD_DIM), F32)],
        compiler_params=_params("parallel", "parallel", "arbitrary"),
        name="hgrn",
    )(u, u, u, u, lb_raw, gn, _hgrn_selector())


def _rope128(x, cos, sin_signed):
    return x * cos + pltpu.roll(x, 64, 1) * sin_signed


def _rope64(x, cos, sin_a, sin_b):
    return x * cos + pltpu.roll(x, 96, 1) * sin_a + pltpu.roll(x, 32, 1) * sin_b


def _dsa_prep_kernel(k_ref, v_ref, ql_ref, misc_ref, ta_ref, tb_ref,
                     wqb_ref, wqi_ref, gql_ref, gqn_ref, gkn_ref, gki_ref,
                     q_out, k_out, vt_out, qi_out, ki_out, wt_out):
    ta, tb = ta_ref[...], tb_ref[...]
    lane = lax.broadcasted_iota(I32, ta.shape, 1)
    ta_swapped = pltpu.roll(ta, 64, 1)
    cm = jnp.where(lane < 64, ta, ta_swapped)
    sm = jnp.where(lane < 64, -ta_swapped, ta)
    quarter = lane >> 5
    r32, r64, r96 = (pltpu.roll(tb, sh, 1) for sh in (32, 64, 96))
    ci = jnp.where(quarter == 0, tb, jnp.where(quarter == 1, r32, jnp.where(quarter == 2, r64, r96)))
    si = jnp.where(quarter == 0, r96, jnp.where(quarter == 1, tb, jnp.where(quarter == 2, r32, r64)))
    sia = jnp.where((lane & 32) == 0, -si, 0.0)
    sib = jnp.where((lane & 32) != 0, si, 0.0)
    cb = _rms(ql_ref[...], gql_ref[...]).astype(BF16)
    qf = _dot(cb, wqb_ref[...])
    scale = HEAD_DIM ** -0.5
    for h in range(N_HEADS):
        sl = slice(h * HEAD_DIM, (h + 1) * HEAD_DIM)
        qh = _rope128(_rms_head(qf[:, sl], gqn_ref[...]), cm, sm)
        q_out[:, sl] = (qh * (scale * LOG2E)).astype(BF16)
        kh = _rope128(_rms_head(k_ref[:, sl], gkn_ref[...]), cm, sm)
        k_out[:, sl] = kh.astype(BF16)
    vt = v_ref[...].T.astype(BF16)
    ones = jnp.ones((PACK, vt.shape[1]), BF16)
    for h in range(N_HEADS):
        vt_out[0, h * VT_ROWS:h * VT_ROWS + HEAD_DIM, :] = vt[h * HEAD_DIM:(h + 1) * HEAD_DIM, :]
        vt_out[0, h * VT_ROWS + HEAD_DIM:(h + 1) * VT_ROWS, :] = ones
    qi = _dot(cb, wqi_ref[...])
    for p in range(IDX_HEADS * IDX_DIM // LANES):
        sl = slice(p * LANES, (p + 1) * LANES)
        qi_out[:, sl] = _rope64(qi[:, sl], ci, sia, sib).astype(BF16)
    misc = misc_ref[...]
    kn = _rope64(_rms_head(misc, gki_ref[...], IDX_DIM), ci, sia, sib)
    ki_out[...] = (kn + pltpu.roll(kn, IDX_DIM, 1)).astype(BF16)
    wt = (misc * (IDX_HEADS ** -0.5 * IDX_DIM ** -0.5)).T
    wt_out[...] = wt[IDX_DIM:IDX_DIM + IDX_HEADS, :]


def _dsa_prep(u, tabs, wqb, wqi, gql, gqn, gkn, gki, tm=DSA_KEY_TILE):
    n = u.shape[0]
    nt = n // tm
    ublk = lambda width, col: pl.BlockSpec((tm, width), lambda i: (i, col // width))
    tab = pl.BlockSpec((tm, LANES), lambda i: (i, 0))
    full = lambda a: pl.BlockSpec(a.shape, lambda i: (0,) * a.ndim)
    return pl.pallas_call(
        _dsa_prep_kernel,
        out_shape=(jax.ShapeDtypeStruct((n, WIDTH), BF16),
                   jax.ShapeDtypeStruct((n, WIDTH), BF16),
                   jax.ShapeDtypeStruct((nt, N_HEADS * VT_ROWS, tm), BF16),
                   jax.ShapeDtypeStruct((n, IDX_HEADS * IDX_DIM), BF16),
                   jax.ShapeDtypeStruct((n, LANES), BF16),
                   jax.ShapeDtypeStruct((IDX_HEADS, n), F32)),
        grid=(nt,),
        in_specs=[ublk(WIDTH, U_K), ublk(WIDTH, U_V), ublk(Q_LORA, U_QLAT), ublk(LANES, U_MISC),
                  tab, tab,
                  full(wqb), full(wqi), full(gql), full(gqn), full(gkn), full(gki)],
        out_specs=(pl.BlockSpec((tm, WIDTH), lambda i: (i, 0)),
                   pl.BlockSpec((tm, WIDTH), lambda i: (i, 0)),
                   pl.BlockSpec((1, N_HEADS * VT_ROWS, tm), lambda i: (i, 0, 0)),
                   pl.BlockSpec((tm, IDX_HEADS * IDX_DIM), lambda i: (i, 0)),
                   pl.BlockSpec((tm, LANES), lambda i: (i, 0)),
                   pl.BlockSpec((IDX_HEADS, tm), lambda i: (0, i))),
        compiler_params=_params("parallel"),
        name="dsa_prep",
    )(u, u, u, u, *tabs, wqb, wqi, gql, gqn, gkn, gki)


def _sortable(x):
    bits = lax.bitcast_convert_type(x, I32)
    return bits ^ ((bits >> 31) & 0x7FFFFFFF)


def _dsa_attn_kernel(qi_ref, wt_ref, qin_ref, wtn_ref, q_ref, ki_ref, k_ref, vt_ref, o_ref,
                     qm_ref, hi_ref, lo_ref, kb_ref, lt_ref, acc0, acc1, acc2, acc3,
                     *, tq, kt, top_k, n_blocks):
    acc_refs = (acc0, acc1, acc2, acc3)
    blk = pl.program_id(1)
    cur = blk & 1
    nxt = 1 - cur
    q0 = blk * tq
    n_tiles = (q0 + tq + kt - 1) // kt
    n_pairs = (n_tiles + 1) // 2
    sub = SCORE_ROWS
    key_iota = lax.broadcasted_iota(I32, (sub, tq), 0)
    lane_q = lax.broadcasted_iota(I32, (tq, LANES), 1)

    def tile_rows(jt):
        return pl.ds(pl.multiple_of(jt * kt, kt), kt)

    def limit_of(q_start):
        qpos = q_start + lax.broadcasted_iota(I32, (1, tq), 1)
        return ((qpos >> CHUNK_SHIFT) + 1) << CHUNK_SHIFT

    def build_qm(src_ref):
        for h in range(IDX_HEADS):
            src = src_ref[:, (h // 2) * LANES:(h // 2 + 1) * LANES].astype(F32)
            keep = (lane_q < IDX_DIM) if h % 2 == 0 else (lane_q >= IDX_DIM)
            qm_ref[h * tq:(h + 1) * tq, :] = jnp.where(keep, src, 0.0).astype(BF16)

    def score_tile(jt, slot, limit, w_ref):
        for part in range(kt // sub):
            base = jt * kt + part * sub
            rows = pl.ds(pl.multiple_of(base, sub), sub)
            ki = ki_ref[rows, :]
            acc = jnp.zeros((sub, tq), F32)
            for h in range(IDX_HEADS):
                x = _dot_nt(ki, qm_ref[h * tq:(h + 1) * tq, :])
                acc = acc + w_ref[h:h + 1, :] * jnp.maximum(x, 0.0)
            key = jnp.where(base + key_iota < limit, _sortable(acc), INT_MIN)
            hi_ref[slot, rows, :] = (key >> HALF_BITS).astype(I16)
            lo_ref[slot, rows, :] = ((key & (2 ** HALF_BITS - 1)) + INT16_MIN).astype(I16)

    @pl.when(blk == 0)
    def _():
        build_qm(qi_ref)
        limit0 = limit_of(0)

        def body(jt, carry):
            score_tile(jt, 0, limit0, wt_ref)
            return carry
        lax.fori_loop(0, n_tiles, body, 0)

    one, zero = jnp.ones((), BF16), jnp.zeros((), BF16)
    neg_inf = jnp.full((), -jnp.inf, BF16)
    n_rows = (n_tiles * kt).astype(F32)
    hi_rows = lambda rows: hi_ref[cur, rows, :]
    lo_rows = lambda rows: lo_ref[cur, rows, :]
    kb_rows = lambda rows: kb_ref[rows, :]

    def count16(get, trial, strict):
        t16 = jnp.broadcast_to(trial.astype(I16), (PACK, tq))

        def body(jt, c):
            x = get(tile_rows(jt))
            parts = [jnp.zeros((PACK, tq), BF16) for _ in range(4)]
            for r in range(kt // PACK):
                xr = x[r * PACK:(r + 1) * PACK, :]
                parts[r % 4] = parts[r % 4] + jnp.where((xr > t16) if strict else (xr >= t16), one, zero)
            return c + ((parts[0] + parts[1]) + (parts[2] + parts[3])).astype(F32)
        c = lax.fori_loop(0, n_tiles, body, jnp.zeros((PACK, tq), F32))
        return jnp.sum(c, axis=0, keepdims=True)

    def bisect16(get, want, n_all, early_exit):
        def step(it, carry):
            lo, n_lo, n_up = carry
            trial = lo + lax.shift_left(jnp.int32(1), HALF_BITS - 1 - it)
            n = count16(get, trial, False)
            ok = n >= want
            return jnp.where(ok, trial, lo), jnp.where(ok, n, n_lo), jnp.where(ok, n_up, n)

        init = (jnp.full((1, tq), INT16_MIN, I32), n_all, jnp.zeros((1, tq), F32))
        if not early_exit:
            return lax.fori_loop(0, HALF_BITS, step, init)

        def unresolved(carry):
            it, _, n_lo, _ = carry
            return (it < HALF_BITS) & (jnp.max(jnp.where(n_lo != want, 1.0, 0.0)) > 0.0)

        def two_steps(carry):
            return (carry[0] + 2,) + step(carry[0] + 1, step(carry[0], carry[1:]))

        return lax.while_loop(unresolved, two_steps, (jnp.int32(SURE_PASSES),) + lax.fori_loop(0, SURE_PASSES, step, init))[1:]

    tau_hi, n_hi_ge, n_hi_gt = bisect16(hi_rows, float(top_k), jnp.full((1, tq), n_rows, F32), False)
    want_lo = top_k - n_hi_gt
    tau_hi16 = jnp.broadcast_to(tau_hi.astype(I16), (kt, tq))

    def bucket_tile(jt, carry):
        rows = tile_rows(jt)
        kb_ref[rows, :] = jnp.where(hi_rows(rows) == tau_hi16, lo_rows(rows), jnp.full((), INT16_MIN, I16))
        return carry

    lax.fori_loop(0, n_tiles, bucket_tile, 0)
    tau_lo, n_kb_ge, _ = bisect16(kb_rows, want_lo, n_hi_ge - n_hi_gt, True)
    live = (tau_hi > INT16_MIN) | (tau_lo > INT16_MIN)
    excess = jnp.max(jnp.where(live, n_kb_ge - want_lo, 0.0))

    @pl.when(excess > 0)
    def _():
        need = want_lo - count16(kb_rows, tau_lo, True)
        live_f = jnp.where(live, 1.0, 0.0)
        tri = (lax.broadcasted_iota(I32, (kt, kt), 0)
               >= lax.broadcasted_iota(I32, (kt, kt), 1)).astype(BF16)

        def body(jt, seen):
            rows = tile_rows(jt)
            h32 = hi_rows(rows).astype(I32)
            eq = jnp.where(h32 == tau_hi, jnp.where(lo_rows(rows).astype(I32) == tau_lo, 1.0, 0.0), 0.0)
            rank = _dot(tri, eq.astype(BF16)) + seen
            dropped = eq * (1.0 - jnp.where(rank <= need, live_f, 0.0))
            hi_ref[cur, rows, :] = jnp.where(dropped > 0.0, INT16_MIN, h32).astype(I16)
            return rank[kt - 1:kt, :]
        lax.fori_loop(0, n_tiles, body, jnp.zeros((1, tq), F32))

    tau_lo16 = jnp.broadcast_to(jnp.where(live, tau_lo, INT16_MAX).astype(I16), (kt, tq))

    def bias_rows(rows):
        h16 = hi_rows(rows)
        in_bucket = jnp.where(lo_rows(rows) >= tau_lo16, zero, neg_inf)
        return jnp.where(h16 > tau_hi16, zero, jnp.where(h16 == tau_hi16, in_bucket, neg_inf)).astype(F32)

    for a in acc_refs:
        a[...] = jnp.zeros_like(a)
    heads = [slice(h * HEAD_DIM, (h + 1) * HEAD_DIM) for h in range(N_HEADS)]

    @pl.when((n_tiles & 1) == 1)
    def _():
        hi_ref[cur, tile_rows(n_tiles), :] = jnp.full((kt, tq), INT16_MIN, I16)
        lo_ref[cur, tile_rows(n_tiles), :] = jnp.full((kt, tq), INT16_MIN, I16)

    limit_next = limit_of(q0 + tq)

    def attn_pair(jp, ms, score_next):
        tiles = (2 * jp, 2 * jp + 1)
        for i, jt in enumerate(tiles):
            rows = tile_rows(jt)
            bias = bias_rows(rows)
            for h in range(N_HEADS):
                lt_ref[i, h] = _dot_nt(k_ref[rows, heads[h]], q_ref[:, heads[h]]) + bias
        ms = list(ms)
        for i, jt in enumerate(tiles):
            if score_next:
                score_tile(jt, nxt, limit_next, wtn_ref)
            alphas, ps = [], []
            for h in range(N_HEADS):
                m_new = jnp.maximum(ms[h], jnp.max(lt_ref[i, h], axis=0, keepdims=True))
                m_safe = jnp.where(m_new == -jnp.inf, 0.0, m_new)
                ps.append(jnp.exp2(lt_ref[i, h] - m_safe).astype(BF16))
                alphas.append(jnp.exp2(ms[h] - m_safe))
                ms[h] = m_new
            for h in range(N_HEADS):
                acc_refs[h][...] = (acc_refs[h][...] * alphas[h]
                                    + _dot(vt_ref[jt, h * VT_ROWS:(h + 1) * VT_ROWS, :], ps[h]))
        return tuple(ms)

    m_init = tuple(jnp.full((1, tq), -jnp.inf, F32) for _ in range(N_HEADS))

    @pl.when(blk < n_blocks - 1)
    def _():
        build_qm(qin_ref)
        lax.fori_loop(0, n_pairs, functools.partial(attn_pair, score_next=True), m_init)

        @pl.when((q0 + 2 * tq + kt - 1) // kt > 2 * n_pairs)
        def _():
            score_tile(2 * n_pairs, nxt, limit_next, wtn_ref)

    @pl.when(blk == n_blocks - 1)
    def _():
        lax.fori_loop(0, n_pairs, functools.partial(attn_pair, score_next=False), m_init)

    for h in range(N_HEADS):
        acc = acc_refs[h][...]
        out_t = acc[:HEAD_DIM, :] * (1.0 / acc[HEAD_DIM:HEAD_DIM + 1, :])
        o_ref[:, heads[h]] = out_t.T


def _dsa_attn(q, k, vt, qi, ki, wt, bsz, seq, tq=256, kt=DSA_KEY_TILE):
    n = bsz * seq
    nq = seq // tq
    top_k = min(TOPK_MAX, seq // 4)
    once = dict(pipeline_mode=pl.Buffered(1))
    nxt_blk = lambda b, i: b * nq + jnp.minimum(i + 1, nq - 1)
    return pl.pallas_call(
        functools.partial(_dsa_attn_kernel, tq=tq, kt=kt, top_k=top_k, n_blocks=nq),
        out_shape=jax.ShapeDtypeStruct((n, WIDTH), F32),
        grid=(bsz, nq),
        in_specs=[pl.BlockSpec((tq, IDX_HEADS * IDX_DIM), lambda b, i: (b * nq + i, 0)),
                  pl.BlockSpec((IDX_HEADS, tq), lambda b, i: (0, b * nq + i)),
                  pl.BlockSpec((tq, IDX_HEADS * IDX_DIM), lambda b, i: (nxt_blk(b, i), 0)),
                  pl.BlockSpec((IDX_HEADS, tq), lambda b, i: (0, nxt_blk(b, i))),
                  pl.BlockSpec((tq, WIDTH), lambda b, i: (b * nq + i, 0)),
                  pl.BlockSpec((seq, LANES), lambda b, i: (b, 0), **once),
                  pl.BlockSpec((seq, WIDTH), lambda b, i: (b, 0), **once),
                  pl.BlockSpec((seq // kt, N_HEADS * VT_ROWS, kt), lambda b, i: (b, 0, 0), **once)],
        out_specs=pl.BlockSpec((tq, WIDTH), lambda b, i: (b * nq + i, 0)),
        scratch_shapes=[pltpu.VMEM((IDX_HEADS * tq, LANES), BF16),
                        pltpu.VMEM((2, seq, tq), I16), pltpu.VMEM((2, seq, tq), I16), pltpu.VMEM((seq, tq), I16),
                        pltpu.VMEM((2, N_HEADS, kt, tq), F32)]
                       + [pltpu.VMEM((VT_ROWS, tq), F32) for _ in range(N_HEADS)],
        compiler_params=_params("arbitrary", "arbitrary"),
        name="dsa_attn",
    )(qi, wt, qi, wt, q, ki, k, vt)


def _mem_kv_kernel(mem_ref, g_ref, wk_ref, wv_ref, gk_ref, k_out, v_out):
    mb = _rms(mem_ref[...], g_ref[...]).astype(BF16)
    kf = _dot(mb, wk_ref[...])
    for h in range(N_HEADS):
        sl = slice(h * HEAD_DIM, (h + 1) * HEAD_DIM)
        k_out[:, sl] = _rms_head(kf[:, sl], gk_ref[...]).astype(BF16)
    v_out[...] = _dot(mb, wv_ref[...]).astype(BF16)


def _mem_kv(mem2, gain, wk, wv, gk, n_mem):
    n, d = mem2.shape
    full = lambda a: pl.BlockSpec(a.shape, lambda b: (0,) * a.ndim)
    return pl.pallas_call(
        _mem_kv_kernel,
        out_shape=(jax.ShapeDtypeStruct((n, WIDTH), BF16), jax.ShapeDtypeStruct((n, WIDTH), BF16)),
        grid=(n // n_mem,),
        in_specs=[pl.BlockSpec((n_mem, d), lambda b: (b, 0)), full(gain), full(wk), full(wv), full(gk)],
        out_specs=(pl.BlockSpec((n_mem, WIDTH), lambda b: (b, 0)),
                   pl.BlockSpec((n_mem, WIDTH), lambda b: (b, 0))),
        compiler_params=_params("parallel"),
        name="mem_kv",
    )(mem2, gain, wk, wv, gk)


def _xattn_kernel(x_ref, oa_ref, ob_ref, wout_ref, g_ref, wq_ref, gq_ref, km_ref, vm_ref, wo_ref, o_ref):
    x1 = (x_ref[...] + _dot(oa_ref[...].astype(BF16), wout_ref[:WIDTH, :])
          + _dot(ob_ref[...].astype(BF16), wout_ref[WIDTH:, :]))
    hb = _rms(x1, g_ref[...]).astype(BF16)
    qf = _dot(hb, wq_ref[...])
    scale = HEAD_DIM ** -0.5
    outs = []
    for h in range(N_HEADS):
        sl = slice(h * HEAD_DIM, (h + 1) * HEAD_DIM)
        qh = (_rms(qf[:, sl], gq_ref[...]) * scale).astype(BF16)
        logits = _dot_nt(qh, km_ref[:, sl])
        p = jnp.exp(logits - jnp.max(logits, axis=-1, keepdims=True))
        oh = _dot(p.astype(BF16), vm_ref[:, sl])
        outs.append((oh * (1.0 / jnp.sum(p, axis=-1, keepdims=True))).astype(BF16))
    o_ref[...] = x1 + _dot(jnp.concatenate(outs, axis=1), wo_ref[...])


def _xattn(x2, oa, ob, wout, gain, wq, gq, km, vm, wo, seq, n_mem, tm=512):
    n, d = x2.shape
    per_b = seq // tm
    row = lambda w: pl.BlockSpec((tm, w), lambda i: (i, 0))
    full = lambda a: pl.BlockSpec(a.shape, lambda i: (0,) * a.ndim)
    memb = pl.BlockSpec((n_mem, WIDTH), lambda i: (i // per_b, 0))
    return pl.pallas_call(
        _xattn_kernel,
        out_shape=jax.ShapeDtypeStruct((n, d), F32),
        grid=(n // tm,),
        in_specs=[row(d), row(WIDTH), row(WIDTH), full(wout), full(gain), full(wq), full(gq),
                  memb, memb, full(wo)],
        out_specs=row(d),
        compiler_params=_params("parallel"),
        name="xattn",
    )(x2, oa, ob, wout, gain, wq, gq, km, vm, wo)


def _moe_kernel(x_ref, g_ref, wr_hi_ref, wr_lo_ref, br_ref, wgu_ref, wdn_ref, o_ref, hs_ref, gs_ref, ys_ref,
                *, tm, sub):
    lane = lax.broadcasted_iota(I32, (tm, LANES), 1)
    lane_f = lane.astype(F32)
    x = x_ref[...]
    h = _rms(x, g_ref[...])
    h_hi = h.astype(BF16)
    h_lo = (h - h_hi.astype(F32)).astype(BF16)
    lg = (_dot(h_hi, wr_hi_ref[...]) + _dot(h_lo, wr_hi_ref[...]) + _dot(h_hi, wr_lo_ref[...])
          + br_ref[...])
    first = lambda cond: jnp.min(jnp.where(cond, lane_f, 1e9), axis=-1, keepdims=True)
    gl = jnp.where(lane < N_GROUPS, lg, -jnp.inf)
    gmax = jnp.max(gl, axis=-1, keepdims=True)
    gsel = first(gl == gmax)
    g_w = 1.0 / jnp.sum(jnp.exp(gl - gmax), axis=-1, keepdims=True)
    grp_of_lane = ((lane - N_GROUPS) >> 2).astype(F32)
    in_grp = (lane >= N_GROUPS) & (lane < N_GROUPS + N_EXPERTS) & (grp_of_lane == gsel)
    el = jnp.where(in_grp, lg, -jnp.inf)
    v1 = jnp.max(el, axis=-1, keepdims=True)
    i1 = first(el == v1)
    el2 = jnp.where(lane_f == i1, -jnp.inf, el)
    v2 = jnp.max(el2, axis=-1, keepdims=True)
    i2 = first(el2 == v2)
    r = jnp.exp(v2 - v1)
    w1 = 1.0 / (1.0 + r)
    gate = jnp.where(lane_f == i1, w1 * g_w, jnp.where(lane_f == i2, r * w1 * g_w, 0.0))

    onehot = jnp.where(lane_f == gsel, 1.0, 0.0)
    t_row = lax.broadcasted_iota(I32, (tm, tm), 0)
    t_col = lax.broadcasted_iota(I32, (tm, tm), 1)
    before = _dot(jnp.where(t_row > t_col, 1.0, 0.0).astype(BF16), onehot.astype(BF16))
    counts = jnp.sum(onehot, axis=0, keepdims=True)
    lane1 = lax.broadcasted_iota(I32, (1, LANES), 1)
    ends = []
    run = jnp.zeros((1, 1), F32)
    for g in range(N_GROUPS - 1):
        run = run + jnp.sum(jnp.where(lane1 == g, counts, 0.0), axis=-1, keepdims=True)
        ends.append(run)
    start_of = sum(jnp.where(lane1 == g + 1, ends[g], 0.0) for g in range(N_GROUPS - 1))
    pos = jnp.sum(onehot * (before + start_of), axis=-1, keepdims=True)
    to_sorted_t = jnp.where(pos == t_col.astype(F32), 1.0, 0.0)
    to_sorted = to_sorted_t.T.astype(BF16)
    hs_ref[...] = _dot(to_sorted, h_hi).astype(BF16)
    g_hi, g_mid, g_lo = _split3(gate)
    gs_ref[...] = _dot(to_sorted, g_hi) + _dot(to_sorted, g_mid) + _dot(to_sorted, g_lo)

    bounds = [e_[0, 0].astype(I32) for e_ in ends]
    lane_s = lax.broadcasted_iota(I32, (sub, LANES), 1)

    def slab(j, carry):
        r0 = j * sub
        rows = pl.ds(pl.multiple_of(r0, sub), sub)
        g_first = sum((b <= r0).astype(I32) for b in bounds)
        g_last = sum((b <= r0 + sub - 1).astype(I32) for b in bounds)
        hs = hs_ref[rows, :]
        gs = gs_ref[rows, :]
        ys_ref[rows, :] = jnp.zeros((sub, ys_ref.shape[1]), F32)

        def group(g, c):
            gu = _dot(hs, wgu_ref[g])
            acts = []
            for e in range(EXP_PER_GROUP):
                ge = jnp.sum(jnp.where(lane_s == g * EXP_PER_GROUP + e + N_GROUPS, gs, 0.0), axis=-1, keepdims=True)
                up = gu[:, 2 * e * D_EXPERT:(2 * e + 1) * D_EXPERT]
                acts.append((up * jax.nn.sigmoid(up) * gu[:, (2 * e + 1) * D_EXPERT:(2 * e + 2) * D_EXPERT] * ge).astype(BF16))
            ys_ref[rows, :] += _dot(jnp.concatenate(acts, axis=1), wdn_ref[g])
            return c

        lax.fori_loop(g_first, g_last + 1, group, 0)
        return carry

    lax.fori_loop(0, tm // sub, slab, 0)
    o_ref[...] = x + _dot(to_sorted_t.astype(BF16), ys_ref[...].astype(BF16))


def _moe(x2, gain, wr_hi, wr_lo, br, wgu, wdn, tm=512, sub=128):
    n, d = x2.shape
    full = lambda a: pl.BlockSpec(a.shape, lambda i: (0,) * a.ndim)
    once = lambda a: pl.BlockSpec(a.shape, lambda i: (0,) * a.ndim, pipeline_mode=pl.Buffered(1))
    return pl.pallas_call(
        functools.partial(_moe_kernel, tm=tm, sub=sub),
        out_shape=jax.ShapeDtypeStruct((n, d), F32),
        grid=(n // tm,),
        in_specs=[pl.BlockSpec((tm, d), lambda i: (i, 0)), full(gain), full(wr_hi), full(wr_lo), full(br),
                  once(wgu), once(wdn)],
        out_specs=pl.BlockSpec((tm, d), lambda i: (i, 0)),
        scratch_shapes=[pltpu.VMEM((tm, d), BF16), pltpu.VMEM((tm, LANES), F32), pltpu.VMEM((tm, d), F32)],
        compiler_params=_params("parallel"),
        name="moe",
    )(x2, gain, wr_hi, wr_lo, br, wgu, wdn)


def _rope_tables(positions):
    pos = positions.reshape(-1).astype(F32)[:, None]

    def cs(dim):
        inv = ROPE_THETA ** (-jnp.arange(0, dim, 2, dtype=F32) / dim)
        ang = pos * inv
        return jnp.cos(ang), jnp.sin(ang)

    c, s = cs(HEAD_DIM)
    ci, si = cs(IDX_DIM)
    return jnp.concatenate([c, s], 1), jnp.concatenate([ci, si, jnp.zeros_like(ci), jnp.zeros_like(ci)], 1)


def _reorder_in_weight(w):
    a = w[:, :4 * WIDTH]
    o = 4 * WIDTH
    qlat = w[:, o:o + Q_LORA]; o += Q_LORA
    k = w[:, o:o + WIDTH]; o += WIDTH
    v = w[:, o:o + WIDTH]; o += WIDTH
    misc = w[:, o:]
    pad = jnp.zeros((w.shape[0], LANES - misc.shape[1]), w.dtype)
    return jnp.concatenate([a, k, v, qlat, misc, pad], axis=1).astype(BF16)


def _row(v):
    return v.reshape(1, -1).astype(F32)


def kernel(x, mem, positions, norm_mix, w_in, lb_raw, a_gnorm, b_qlat_gain, b_wqb, b_wqidx, b_qnorm,
           b_knorm, b_kidx_norm, w_out, norm_x, norm_mem, x_wq, x_wk, x_wv, x_wo, x_qnorm, x_knorm,
           norm_ffn, w_rg, b_rg, w_re, b_re, w_gu, w_dn):
    bsz, seq, d = x.shape
    n_mem = mem.shape[1]
    depth = w_in.shape[0]
    assert d == D_MODEL and seq % (2 * DSA_KEY_TILE) == 0 and n_mem % LANES == 0, (x.shape, mem.shape)
    x2 = x.reshape(bsz * seq, d)
    mem2 = mem.reshape(bsz * n_mem, d)
    tabs = _rope_tables(positions)
    pad_lanes = lambda v: jnp.pad(v, ((0, 0), (0, LANES - v.shape[1])))
    for l in range(depth):
        u = _in_proj(x2, _row(norm_mix[l]), _reorder_in_weight(w_in[l]))
        o_a = _hgrn(u, lb_raw.astype(F32), _row(a_gnorm[l]), l, bsz, seq)
        q, k, vt, qi, ki, wt = _dsa_prep(
            u, tabs, b_wqb[l].astype(BF16), b_wqidx[l].astype(BF16), _row(b_qlat_gain[l]),
            _row(b_qnorm[l]), _row(b_knorm[l]), pad_lanes(_row(b_kidx_norm[l])))
        o_b = _dsa_attn(q, k, vt, qi, ki, wt, bsz, seq)
        km, vm = _mem_kv(mem2, _row(norm_mem[l]), x_wk[l].astype(BF16), x_wv[l].astype(BF16),
                         _row(x_knorm[l]), n_mem)
        x2 = _xattn(x2, o_a, o_b, w_out[l].astype(BF16), _row(norm_x[l]), x_wq[l].astype(BF16),
                    _row(x_qnorm[l]), km, vm, x_wo[l].astype(BF16), seq, n_mem)
        wr = pad_lanes(jnp.concatenate([w_rg[l], w_re[l].reshape(d, N_EXPERTS)], axis=1).astype(F32))
        wr_hi = wr.astype(BF16)
        wr_lo = (wr - wr_hi.astype(F32)).astype(BF16)
        br = pad_lanes(jnp.concatenate([b_rg[l], b_re[l].reshape(-1)]).reshape(1, -1).astype(F32))
        x2 = _moe(x2, _row(norm_ffn[l]), wr_hi, wr_lo, br,
                  w_gu[l].transpose(0, 2, 1, 3).reshape(N_GROUPS, d, EXP_PER_GROUP * 2 * D_EXPERT).astype(BF16),
                  w_dn[l].reshape(N_GROUPS, EXP_PER_GROUP * D_EXPERT, d).astype(BF16))
    return x2.reshape(bsz, seq, d)
```

```python
import functools

import numpy as np
import jax
import jax.numpy as jnp
from jax import lax
from jax.experimental import pallas as pl
from jax.experimental.pallas import tpu as pltpu

F32 = jnp.float32
BF16 = jnp.bfloat16
I32 = jnp.int32
I16 = jnp.int16

EPS = 1e-6
ROPE_THETA = 10000.0
CHUNK = 64
CHUNK_SHIFT = CHUNK.bit_length() - 1
SUB = 16
SUB_SHIFT = SUB.bit_length() - 1
N_HEADS = 4
HEAD_DIM = 128
WIDTH = N_HEADS * HEAD_DIM
Q_LORA = 256
D_MODEL = 1024
IDX_HEADS = 16
IDX_DIM = 64
ROT_I = IDX_DIM // 2
TOPK_MAX = 256
N_GROUPS = 4
EXP_PER_GROUP = 4
N_EXPERTS = N_GROUPS * EXP_PER_GROUP
D_EXPERT = 256
LANES = 128
INT_MIN = -2 ** 31
INT16_MIN, INT16_MAX = -2 ** 15, 2 ** 15 - 1
HALF_BITS = 16
PACK = 16
LOG2E = 1.4426950408889634
VT_ROWS = HEAD_DIM + PACK
DSA_KEY_TILE = 512
SCORE_ROWS = 128
SURE_PASSES = 8

U_K = 4 * WIDTH
U_V = U_K + WIDTH
U_QLAT = U_V + WIDTH
U_MISC = U_QLAT + Q_LORA
U_COLS = U_MISC + LANES

VMEM_LIMIT = 56 * 1024 * 1024


def _rms(x, gain):
    return x * lax.rsqrt(jnp.mean(x * x, axis=-1, keepdims=True) + EPS) * gain


def _dot(a, b):
    return jnp.dot(a, b, preferred_element_type=F32)


def _rms_head(x, gain, width=HEAD_DIM):
    ones = (lax.broadcasted_iota(I32, (LANES, LANES), 0) < width).astype(BF16)
    ss = _dot((x * x).astype(BF16), ones)
    return x * lax.rsqrt(ss * (1.0 / width) + EPS) * gain


def _dot_nt(a, b):
    return lax.dot_general(a, b, (((1,), (1,)), ((), ())), preferred_element_type=F32)


def _params(*sem):
    return pltpu.CompilerParams(dimension_semantics=sem, vmem_limit_bytes=VMEM_LIMIT)


def _in_proj_kernel(x_ref, g_ref, w_ref, u_ref, *, col_chunk):
    hb = _rms(x_ref[...], g_ref[...]).astype(BF16)
    for c0 in range(0, U_COLS, col_chunk):
        u_ref[:, c0:c0 + col_chunk] = _dot(hb, w_ref[:, c0:c0 + col_chunk])


def _in_proj(x2, gain, w_bf, tm=1024):
    n, d = x2.shape
    return pl.pallas_call(
        functools.partial(_in_proj_kernel, col_chunk=1152),
        out_shape=jax.ShapeDtypeStruct((n, U_COLS), F32),
        grid=(n // tm,),
        in_specs=[pl.BlockSpec((tm, d), lambda i: (i, 0)),
                  pl.BlockSpec((1, d), lambda i: (0, 0)),
                  pl.BlockSpec((d, U_COLS), lambda i: (0, 0), pipeline_mode=pl.Buffered(1))],
        out_specs=pl.BlockSpec((tm, U_COLS), lambda i: (i, 0)),
        compiler_params=_params("parallel"),
        name="in_proj",
    )(x2, gain, w_bf)


def _split3(x):
    hi = x.astype(BF16)
    r1 = x - hi.astype(F32)
    mid = r1.astype(BF16)
    lo = (r1 - mid.astype(F32)).astype(BF16)
    return hi, mid, lo


def _hgrn_kernel(q_ref, f_ref, i_ref, g_ref, lbraw_ref, gn_ref, e_ref, o_ref, st_ref,
                 *, layer, n_chunks, unroll):
    @pl.when(pl.program_id(2) == 0)
    def _():
        st_ref[...] = jnp.zeros_like(st_ref)

    lr = lbraw_ref[...]
    ex = jnp.exp(lr - jnp.max(lr, axis=0, keepdims=True))
    sm = ex / jnp.sum(ex, axis=0, keepdims=True)
    lb = jnp.zeros((1, HEAD_DIM), F32)
    for r in range(1, layer + 1):
        lb = lb + sm[r:r + 1, :]
    log_lb = jnp.log(lb)
    log_1mlb = jnp.log(1.0 - lb)

    row = lax.broadcasted_iota(I32, (CHUNK, HEAD_DIM), 0)
    col = lax.broadcasted_iota(I32, (CHUNK, HEAD_DIM), 1)
    row_l = row & (SUB - 1)
    row_b = row >> SUB_SHIFT
    col_b = col >> SUB_SHIFT
    tri = (lax.broadcasted_iota(I32, (CHUNK, CHUNK), 0)
           >= lax.broadcasted_iota(I32, (CHUNK, CHUNK), 1)).astype(BF16)
    n_sub = CHUNK // SUB
    zeros_c = jnp.zeros((CHUNK, HEAD_DIM), F32)

    def sub_bcast(t, s):
        t4 = t.reshape(n_sub, SUB, HEAD_DIM)
        return jnp.broadcast_to(t4[:, s:s + 1, :], (n_sub, SUB, HEAD_DIM)).reshape(CHUNK, HEAD_DIM)

    def gates(z):
        ls = jnp.minimum(z, 0.0) - jnp.log(1.0 + jnp.exp(-jnp.abs(z)))
        if layer == 0:
            return ls, (ls - z) * LOG2E
        cc = log_1mlb + ls
        log_f = jnp.maximum(log_lb, cc) + jnp.log(1.0 + jnp.exp(-jnp.abs(log_lb - cc)))
        return log_f, (cc - z) * LOG2E

    def diag_terms(q, b2, c2):
        pieces = []
        for s in range(SUB):
            d = jnp.where(row_l >= s, b2 - sub_bcast(c2, s), -jnp.inf)
            pieces.append((q * jnp.exp2(d)).astype(BF16))
        return jnp.concatenate(pieces, axis=1)

    def below_keys(b2, c2):
        kts = []
        for i in range(1, n_sub):
            r_i = b2[i * SUB:i * SUB + 1, :]
            kts.append(jnp.exp2(jnp.where(row < i * SUB, r_i - c2, -jnp.inf)))
            kts.append(zeros_c)
        return jnp.concatenate(kts, axis=0).astype(BF16)

    def below_scores(r):
        a = jnp.zeros((CHUNK, HEAD_DIM), F32)
        for i in range(1, n_sub):
            a = a + jnp.where(row_b == i, r[:, (i - 1) * LANES:i * LANES], 0.0)
        return a

    def chunks(it, carry):
        cs = range(unroll)
        rows = [pl.ds(pl.multiple_of((it * unroll + c) * CHUNK, CHUNK), CHUNK) for c in cs]
        qr = [q_ref[r, :] for r in rows]
        v = [i_ref[r, :] for r in rows]
        q = [x * jax.nn.sigmoid(x) for x in qr]
        gt = [gates(f_ref[r, :]) for r in rows]
        b2 = []
        for c in cs:
            hi, mid, lo = _split3(gt[c][0])
            b2.append((_dot(tri, hi) + _dot(tri, mid) + _dot(tri, lo)) * LOG2E)
        c2 = [b2[c] - gt[c][1] for c in cs]
        b_last = [x[CHUNK - 1:CHUNK, :] for x in b2]
        a2 = [_dot(diag_terms(q[c], b2[c], c2[c]), e_ref[...]) for c in cs]
        rr = [_dot_nt((q[c] * jnp.exp2(b2[c] - sub_bcast(b2[c], 0))).astype(BF16), below_keys(b2[c], c2[c]))
              for c in cs]
        upd = [_dot(v[c].T.astype(BF16), jnp.exp2(b_last[c] - c2[c]).astype(BF16)) for c in cs]
        qe = [(q[c] * jnp.exp2(b2[c])).astype(BF16) for c in cs]
        st = st_ref[...]
        o_inter = []
        for c in cs:
            o_inter.append(_dot_nt(qe[c], st.astype(BF16)))
            st = st * jnp.exp2(b_last[c]) + upd[c]
        st_ref[...] = st
        a = [jnp.where(col_b == row_b, a2[c], 0.0) + below_scores(rr[c]) for c in cs]
        o_intra = [_dot(a[c][:, :CHUNK].astype(BF16), v[c].astype(BF16)) for c in cs]
        for c in cs:
            g = g_ref[rows[c], :]
            o_ref[rows[c], :] = _rms(o_inter[c] + o_intra[c], gn_ref[...]) * (g * jax.nn.sigmoid(g))
        return carry

    lax.fori_loop(0, n_chunks // unroll, chunks, 0)


def _hgrn_selector():
    e = np.zeros((SUB * HEAD_DIM, LANES), np.float32)
    s_of_row = np.arange(SUB * HEAD_DIM) // HEAD_DIM
    cols = np.arange(LANES)
    e[:, :] = ((cols[None, :] % SUB) == s_of_row[:, None]) & (cols[None, :] < CHUNK)
    return jnp.asarray(e, BF16)


def _hgrn(u, lb_raw, gn, layer, bsz, seq, tb=1024):
    n = bsz * seq
    nb = seq // tb
    blk = lambda k: pl.BlockSpec((tb, HEAD_DIM), lambda b, h, j, k=k: (b * nb + j, h + N_HEADS * k))
    n_layers = lb_raw.shape[0]
    return pl.pallas_call(
        functools.partial(_hgrn_kernel, layer=layer, n_chunks=tb // CHUNK, unroll=16),
        out_shape=jax.ShapeDtypeStruct((n, WIDTH), F32),
        grid=(bsz, N_HEADS, nb),
        in_specs=[blk(0), blk(1), blk(2), blk(3),
                  pl.BlockSpec((n_layers, HEAD_DIM), lambda b, h, j: (0, h)),
                  pl.BlockSpec((1, HEAD_DIM), lambda b, h, j: (0, 0)),
                  pl.BlockSpec((SUB * HEAD_DIM, LANES), lambda b, h, j: (0, 0))],
        out_specs=pl.BlockSpec((tb, HEAD_DIM), lambda b, h, j: (b * nb + j, h)),
        scratch_shapes=[pltpu.VMEM((HEAD_DIM, HEAD_DIM), F32)],
        compiler_params=_params("parallel", "parallel", "arbitrary"),
        name="hgrn",
    )(u, u, u, u, lb_raw, gn, _hgrn_selector())


def _rope128(x, cos, sin_signed):
    return x * cos + pltpu.roll(x, HEAD_DIM // 2, 1) * sin_signed


def _rope64(x, cos, sin_a, sin_b):
    return x * cos + pltpu.roll(x, LANES - ROT_I, 1) * sin_a + pltpu.roll(x, ROT_I, 1) * sin_b


def _dsa_prep_kernel(k_ref, v_ref, ql_ref, misc_ref, ta_ref, tb_ref,
                     wqb_ref, wqi_ref, gql_ref, gqn_ref, gkn_ref, gki_ref,
                     q_out, k_out, vt_out, qi_out, ki_out, wt_out):
    ta, tb = ta_ref[...], tb_ref[...]
    lane = lax.broadcasted_iota(I32, ta.shape, 1)
    ta_swapped = pltpu.roll(ta, HEAD_DIM // 2, 1)
    cm = jnp.where(lane < HEAD_DIM // 2, ta, ta_swapped)
    sm = jnp.where(lane < HEAD_DIM // 2, -ta_swapped, ta)
    quarter = lane >> (ROT_I.bit_length() - 1)
    r32, r64, r96 = (pltpu.roll(tb, sh * ROT_I, 1) for sh in (1, 2, 3))
    ci = jnp.where(quarter == 0, tb, jnp.where(quarter == 1, r32, jnp.where(quarter == 2, r64, r96)))
    si = jnp.where(quarter == 0, r96, jnp.where(quarter == 1, tb, jnp.where(quarter == 2, r32, r64)))
    sia = jnp.where((lane & ROT_I) == 0, -si, 0.0)
    sib = jnp.where((lane & ROT_I) != 0, si, 0.0)
    cb = _rms(ql_ref[...], gql_ref[...]).astype(BF16)
    qf = _dot(cb, wqb_ref[...])
    scale = HEAD_DIM ** -0.5
    for h in range(N_HEADS):
        sl = slice(h * HEAD_DIM, (h + 1) * HEAD_DIM)
        qh = _rope128(_rms_head(qf[:, sl], gqn_ref[...]), cm, sm)
        q_out[:, sl] = (qh * (scale * LOG2E)).astype(BF16)
        kh = _rope128(_rms_head(k_ref[:, sl], gkn_ref[...]), cm, sm)
        k_out[:, sl] = kh.astype(BF16)
    vt = v_ref[...].T.astype(BF16)
    ones = jnp.ones((PACK, vt.shape[1]), BF16)
    for h in range(N_HEADS):
        vt_out[0, h * VT_ROWS:h * VT_ROWS + HEAD_DIM, :] = vt[h * HEAD_DIM:(h + 1) * HEAD_DIM, :]
        vt_out[0, h * VT_ROWS + HEAD_DIM:(h + 1) * VT_ROWS, :] = ones
    qi = _dot(cb, wqi_ref[...])
    for p in range(IDX_HEADS * IDX_DIM // LANES):
        sl = slice(p * LANES, (p + 1) * LANES)
        qi_out[:, sl] = _rope64(qi[:, sl], ci, sia, sib).astype(BF16)
    misc = misc_ref[...]
    kn = _rope64(_rms_head(misc, gki_ref[...], IDX_DIM), ci, sia, sib)
    ki_out[...] = (kn + pltpu.roll(kn, IDX_DIM, 1)).astype(BF16)
    wt = (misc * (IDX_HEADS ** -0.5 * IDX_DIM ** -0.5)).T
    wt_out[...] = wt[IDX_DIM:IDX_DIM + IDX_HEADS, :]


def _dsa_prep(u, tabs, wqb, wqi, gql, gqn, gkn, gki, tm=DSA_KEY_TILE):
    n = u.shape[0]
    nt = n // tm
    ublk = lambda width, col: pl.BlockSpec((tm, width), lambda i: (i, col // width))
    tab = pl.BlockSpec((tm, LANES), lambda i: (i, 0))
    full = lambda a: pl.BlockSpec(a.shape, lambda i: (0,) * a.ndim)
    return pl.pallas_call(
        _dsa_prep_kernel,
        out_shape=(jax.ShapeDtypeStruct((n, WIDTH), BF16),
                   jax.ShapeDtypeStruct((n, WIDTH), BF16),
                   jax.ShapeDtypeStruct((nt, N_HEADS * VT_ROWS, tm), BF16),
                   jax.ShapeDtypeStruct((n, IDX_HEADS * IDX_DIM), BF16),
                   jax.ShapeDtypeStruct((n, LANES), BF16),
                   jax.ShapeDtypeStruct((IDX_HEADS, n), F32)),
        grid=(nt,),
        in_specs=[ublk(WIDTH, U_K), ublk(WIDTH, U_V), ublk(Q_LORA, U_QLAT), ublk(LANES, U_MISC),
                  tab, tab,
                  full(wqb), full(wqi), full(gql), full(gqn), full(gkn), full(gki)],
        out_specs=(pl.BlockSpec((tm, WIDTH), lambda i: (i, 0)),
                   pl.BlockSpec((tm, WIDTH), lambda i: (i, 0)),
                   pl.BlockSpec((1, N_HEADS * VT_ROWS, tm), lambda i: (i, 0, 0)),
                   pl.BlockSpec((tm, IDX_HEADS * IDX_DIM), lambda i: (i, 0)),
                   pl.BlockSpec((tm, LANES), lambda i: (i, 0)),
                   pl.BlockSpec((IDX_HEADS, tm), lambda i: (0, i))),
        compiler_params=_params("parallel"),
        name="dsa_prep",
    )(u, u, u, u, *tabs, wqb, wqi, gql, gqn, gkn, gki)


def _sortable(x):
    bits = lax.bitcast_convert_type(x, I32)
    return bits ^ ((bits >> 31) & 0x7FFFFFFF)


def _dsa_attn_kernel(qi_ref, wt_ref, qin_ref, wtn_ref, q_ref, ki_ref, k_ref, vt_ref, o_ref,
                     qm_ref, hi_ref, lo_ref, kb_ref, lt_ref, acc0, acc1, acc2, acc3,
                     *, tq, kt, top_k, n_blocks):
    acc_refs = (acc0, acc1, acc2, acc3)
    blk = pl.program_id(1)
    cur = blk & 1
    nxt = 1 - cur
    q0 = blk * tq
    n_tiles = (q0 + tq + kt - 1) // kt
    n_pairs = (n_tiles + 1) // 2
    sub = SCORE_ROWS
    key_iota = lax.broadcasted_iota(I32, (sub, tq), 0)
    lane_q = lax.broadcasted_iota(I32, (tq, LANES), 1)

    def tile_rows(jt):
        return pl.ds(pl.multiple_of(jt * kt, kt), kt)

    def limit_of(q_start):
        qpos = q_start + lax.broadcasted_iota(I32, (1, tq), 1)
        return ((qpos >> CHUNK_SHIFT) + 1) << CHUNK_SHIFT

    def build_qm(src_ref):
        for h in range(IDX_HEADS):
            src = src_ref[:, (h // 2) * LANES:(h // 2 + 1) * LANES].astype(F32)
            keep = (lane_q < IDX_DIM) if h % 2 == 0 else (lane_q >= IDX_DIM)
            qm_ref[h * tq:(h + 1) * tq, :] = jnp.where(keep, src, 0.0).astype(BF16)

    def score_tile(jt, slot, limit, w_ref):
        for part in range(kt // sub):
            base = jt * kt + part * sub
            rows = pl.ds(pl.multiple_of(base, sub), sub)
            ki = ki_ref[rows, :]
            acc = jnp.zeros((sub, tq), F32)
            for h in range(IDX_HEADS):
                x = _dot_nt(ki, qm_ref[h * tq:(h + 1) * tq, :])
                acc = acc + w_ref[h:h + 1, :] * jnp.maximum(x, 0.0)
            key = jnp.where(base + key_iota < limit, _sortable(acc), INT_MIN)
            hi_ref[slot, rows, :] = (key >> HALF_BITS).astype(I16)
            lo_ref[slot, rows, :] = ((key & (2 ** HALF_BITS - 1)) + INT16_MIN).astype(I16)

    @pl.when(blk == 0)
    def _():
        build_qm(qi_ref)
        limit0 = limit_of(0)

        def body(jt, carry):
            score_tile(jt, 0, limit0, wt_ref)
            return carry
        lax.fori_loop(0, n_tiles, body, 0)

    one, zero = jnp.ones((), BF16), jnp.zeros((), BF16)
    neg_inf = jnp.full((), -jnp.inf, BF16)
    n_rows = (n_tiles * kt).astype(F32)
    hi_rows = lambda rows: hi_ref[cur, rows, :]
    lo_rows = lambda rows: lo_ref[cur, rows, :]
    kb_rows = lambda rows: kb_ref[rows, :]

    def count16(get, trial, strict):
        t16 = jnp.broadcast_to(trial.astype(I16), (PACK, tq))

        def body(jt, c):
            x = get(tile_rows(jt))
            parts = [jnp.zeros((PACK, tq), BF16) for _ in range(4)]
            for r in range(kt // PACK):
                xr = x[r * PACK:(r + 1) * PACK, :]
                parts[r % 4] = parts[r % 4] + jnp.where((xr > t16) if strict else (xr >= t16), one, zero)
            return c + ((parts[0] + parts[1]) + (parts[2] + parts[3])).astype(F32)
        c = lax.fori_loop(0, n_tiles, body, jnp.zeros((PACK, tq), F32))
        return jnp.sum(c, axis=0, keepdims=True)

    def bisect16(get, want, n_all, early_exit):
        def step(it, carry):
            lo, n_lo, n_up = carry
            trial = lo + lax.shift_left(jnp.int32(1), HALF_BITS - 1 - it)
            n = count16(get, trial, False)
            ok = n >= want
            return jnp.where(ok, trial, lo), jnp.where(ok, n, n_lo), jnp.where(ok, n_up, n)

        init = (jnp.full((1, tq), INT16_MIN, I32), n_all, jnp.zeros((1, tq), F32))
        if not early_exit:
            return lax.fori_loop(0, HALF_BITS, step, init)

        def unresolved(carry):
            it, _, n_lo, _ = carry
            return (it < HALF_BITS) & (jnp.max(jnp.where(n_lo != want, 1.0, 0.0)) > 0.0)

        def two_steps(carry):
            return (carry[0] + 2,) + step(carry[0] + 1, step(carry[0], carry[1:]))

        return lax.while_loop(unresolved, two_steps, (jnp.int32(SURE_PASSES),) + lax.fori_loop(0, SURE_PASSES, step, init))[1:]

    tau_hi, n_hi_ge, n_hi_gt = bisect16(hi_rows, float(top_k), jnp.full((1, tq), n_rows, F32), False)
    want_lo = top_k - n_hi_gt
    tau_hi16 = jnp.broadcast_to(tau_hi.astype(I16), (kt, tq))

    def bucket_tile(jt, carry):
        rows = tile_rows(jt)
        kb_ref[rows, :] = jnp.where(hi_rows(rows) == tau_hi16, lo_rows(rows), jnp.full((), INT16_MIN, I16))
        return carry

    lax.fori_loop(0, n_tiles, bucket_tile, 0)
    tau_lo, n_kb_ge, _ = bisect16(kb_rows, want_lo, n_hi_ge - n_hi_gt, True)
    live = (tau_hi > INT16_MIN) | (tau_lo > INT16_MIN)
    excess = jnp.max(jnp.where(live, n_kb_ge - want_lo, 0.0))

    @pl.when(excess > 0)
    def _():
        need = want_lo - count16(kb_rows, tau_lo, True)
        live_f = jnp.where(live, 1.0, 0.0)
        tri = (lax.broadcasted_iota(I32, (kt, kt), 0)
               >= lax.broadcasted_iota(I32, (kt, kt), 1)).astype(BF16)

        def body(jt, seen):
            rows = tile_rows(jt)
            h32 = hi_rows(rows).astype(I32)
            eq = jnp.where(h32 == tau_hi, jnp.where(lo_rows(rows).astype(I32) == tau_lo, 1.0, 0.0), 0.0)
            rank = _dot(tri, eq.astype(BF16)) + seen
            dropped = eq * (1.0 - jnp.where(rank <= need, live_f, 0.0))
            hi_ref[cur, rows, :] = jnp.where(dropped > 0.0, INT16_MIN, h32).astype(I16)
            return rank[kt - 1:kt, :]
        lax.fori_loop(0, n_tiles, body, jnp.zeros((1, tq), F32))

    tau_lo16 = jnp.broadcast_to(jnp.where(live, tau_lo, INT16_MAX).astype(I16), (kt, tq))

    def bias_rows(rows):
        h16 = hi_rows(rows)
        in_bucket = jnp.where(lo_rows(rows) >= tau_lo16, zero, neg_inf)
        return jnp.where(h16 > tau_hi16, zero, jnp.where(h16 == tau_hi16, in_bucket, neg_inf)).astype(F32)

    for a in acc_refs:
        a[...] = jnp.zeros_like(a)
    heads = [slice(h * HEAD_DIM, (h + 1) * HEAD_DIM) for h in range(N_HEADS)]

    @pl.when((n_tiles & 1) == 1)
    def _():
        hi_ref[cur, tile_rows(n_tiles), :] = jnp.full((kt, tq), INT16_MIN, I16)
        lo_ref[cur, tile_rows(n_tiles), :] = jnp.full((kt, tq), INT16_MIN, I16)

    limit_next = limit_of(q0 + tq)

    def attn_pair(jp, ms, score_next):
        tiles = (2 * jp, 2 * jp + 1)
        for i, jt in enumerate(tiles):
            rows = tile_rows(jt)
            bias = bias_rows(rows)
            for h in range(N_HEADS):
                lt_ref[i, h] = _dot_nt(k_ref[rows, heads[h]], q_ref[:, heads[h]]) + bias
        ms = list(ms)
        for i, jt in enumerate(tiles):
            if score_next:
                score_tile(jt, nxt, limit_next, wtn_ref)
            alphas, ps = [], []
            for h in range(N_HEADS):
                m_new = jnp.maximum(ms[h], jnp.max(lt_ref[i, h], axis=0, keepdims=True))
                m_safe = jnp.where(m_new == -jnp.inf, 0.0, m_new)
                ps.append(jnp.exp2(lt_ref[i, h] - m_safe).astype(BF16))
                alphas.append(jnp.exp2(ms[h] - m_safe))
                ms[h] = m_new
            for h in range(N_HEADS):
                acc_refs[h][...] = (acc_refs[h][...] * alphas[h]
                                    + _dot(vt_ref[jt, h * VT_ROWS:(h + 1) * VT_ROWS, :], ps[h]))
        return tuple(ms)

    m_init = tuple(jnp.full((1, tq), -jnp.inf, F32) for _ in range(N_HEADS))

    @pl.when(blk < n_blocks - 1)
    def _():
        build_qm(qin_ref)
        lax.fori_loop(0, n_pairs, functools.partial(attn_pair, score_next=True), m_init)

        @pl.when((q0 + 2 * tq + kt - 1) // kt > 2 * n_pairs)
        def _():
            score_tile(2 * n_pairs, nxt, limit_next, wtn_ref)

    @pl.when(blk == n_blocks - 1)
    def _():
        lax.fori_loop(0, n_pairs, functools.partial(attn_pair, score_next=False), m_init)

    for h in range(N_HEADS):
        acc = acc_refs[h][...]
        out_t = acc[:HEAD_DIM, :] * (1.0 / acc[HEAD_DIM:HEAD_DIM + 1, :])
        o_ref[:, heads[h]] = out_t.T


def _dsa_attn(q, k, vt, qi, ki, wt, bsz, seq, tq=256, kt=DSA_KEY_TILE):
    n = bsz * seq
    nq = seq // tq
    top_k = min(TOPK_MAX, seq // 4)
    once = dict(pipeline_mode=pl.Buffered(1))
    nxt_blk = lambda b, i: b * nq + jnp.minimum(i + 1, nq - 1)
    return pl.pallas_call(
        functools.partial(_dsa_attn_kernel, tq=tq, kt=kt, top_k=top_k, n_blocks=nq),
        out_shape=jax.ShapeDtypeStruct((n, WIDTH), F32),
        grid=(bsz, nq),
        in_specs=[pl.BlockSpec((tq, IDX_HEADS * IDX_DIM), lambda b, i: (b * nq + i, 0)),
                  pl.BlockSpec((IDX_HEADS, tq), lambda b, i: (0, b * nq + i)),
                  pl.BlockSpec((tq, IDX_HEADS * IDX_DIM), lambda b, i: (nxt_blk(b, i), 0)),
                  pl.BlockSpec((IDX_HEADS, tq), lambda b, i: (0, nxt_blk(b, i))),
                  pl.BlockSpec((tq, WIDTH), lambda b, i: (b * nq + i, 0)),
                  pl.BlockSpec((seq, LANES), lambda b, i: (b, 0), **once),
                  pl.BlockSpec((seq, WIDTH), lambda b, i: (b, 0), **once),
                  pl.BlockSpec((seq // kt, N_HEADS * VT_ROWS, kt), lambda b, i: (b, 0, 0), **once)],
        out_specs=pl.BlockSpec((tq, WIDTH), lambda b, i: (b * nq + i, 0)),
        scratch_shapes=[pltpu.VMEM((IDX_HEADS * tq, LANES), BF16),
                        pltpu.VMEM((2, seq, tq), I16), pltpu.VMEM((2, seq, tq), I16), pltpu.VMEM((seq, tq), I16),
                        pltpu.VMEM((2, N_HEADS, kt, tq), F32)]
                       + [pltpu.VMEM((VT_ROWS, tq), F32) for _ in range(N_HEADS)],
        compiler_params=_params("arbitrary", "arbitrary"),
        name="dsa_attn",
    )(qi, wt, qi, wt, q, ki, k, vt)


def _mem_kv_kernel(mem_ref, g_ref, wk_ref, wv_ref, gk_ref, k_out, v_out):
    mb = _rms(mem_ref[...], g_ref[...]).astype(BF16)
    kf = _dot(mb, wk_ref[...])
    for h in range(N_HEADS):
        sl = slice(h * HEAD_DIM, (h + 1) * HEAD_DIM)
        k_out[:, sl] = _rms_head(kf[:, sl], gk_ref[...]).astype(BF16)
    v_out[...] = _dot(mb, wv_ref[...]).astype(BF16)


def _mem_kv(mem2, gain, wk, wv, gk, n_mem):
    n, d = mem2.shape
    full = lambda a: pl.BlockSpec(a.shape, lambda b: (0,) * a.ndim)
    return pl.pallas_call(
        _mem_kv_kernel,
        out_shape=(jax.ShapeDtypeStruct((n, WIDTH), BF16), jax.ShapeDtypeStruct((n, WIDTH), BF16)),
        grid=(n // n_mem,),
        in_specs=[pl.BlockSpec((n_mem, d), lambda b: (b, 0)), full(gain), full(wk), full(wv), full(gk)],
        out_specs=(pl.BlockSpec((n_mem, WIDTH), lambda b: (b, 0)),
                   pl.BlockSpec((n_mem, WIDTH), lambda b: (b, 0))),
        compiler_params=_params("parallel"),
        name="mem_kv",
    )(mem2, gain, wk, wv, gk)


def _xattn_kernel(x_ref, oa_ref, ob_ref, wout_ref, g_ref, wq_ref, gq_ref, km_ref, vm_ref, wo_ref, o_ref):
    x1 = (x_ref[...] + _dot(oa_ref[...].astype(BF16), wout_ref[:WIDTH, :])
          + _dot(ob_ref[...].astype(BF16), wout_ref[WIDTH:, :]))
    hb = _rms(x1, g_ref[...]).astype(BF16)
    qf = _dot(hb, wq_ref[...])
    scale = HEAD_DIM ** -0.5
    outs = []
    for h in range(N_HEADS):
        sl = slice(h * HEAD_DIM, (h + 1) * HEAD_DIM)
        qh = (_rms(qf[:, sl], gq_ref[...]) * scale).astype(BF16)
        logits = _dot_nt(qh, km_ref[:, sl])
        p = jnp.exp(logits - jnp.max(logits, axis=-1, keepdims=True))
        oh = _dot(p.astype(BF16), vm_ref[:, sl])
        outs.append((oh * (1.0 / jnp.sum(p, axis=-1, keepdims=True))).astype(BF16))
    o_ref[...] = x1 + _dot(jnp.concatenate(outs, axis=1), wo_ref[...])


def _xattn(x2, oa, ob, wout, gain, wq, gq, km, vm, wo, seq, n_mem, tm=1024):
    n, d = x2.shape
    per_b = seq // tm
    row = lambda w: pl.BlockSpec((tm, w), lambda i: (i, 0))
    full = lambda a: pl.BlockSpec(a.shape, lambda i: (0,) * a.ndim)
    memb = pl.BlockSpec((n_mem, WIDTH), lambda i: (i // per_b, 0))
    return pl.pallas_call(
        _xattn_kernel,
        out_shape=jax.ShapeDtypeStruct((n, d), F32),
        grid=(n // tm,),
        in_specs=[row(d), row(WIDTH), row(WIDTH), full(wout), full(gain), full(wq), full(gq),
                  memb, memb, full(wo)],
        out_specs=row(d),
        compiler_params=_params("parallel"),
        name="xattn",
    )(x2, oa, ob, wout, gain, wq, gq, km, vm, wo)


def _moe_kernel(x_ref, g_ref, wr_hi_ref, wr_lo_ref, br_ref, wgu_ref, wdn_ref, o_ref, hs_ref, gs_ref, ys_ref,
                *, tm, sub):
    lane = lax.broadcasted_iota(I32, (tm, LANES), 1)
    lane_f = lane.astype(F32)
    x = x_ref[...]
    h = _rms(x, g_ref[...])
    h_hi = h.astype(BF16)
    h_lo = (h - h_hi.astype(F32)).astype(BF16)
    lg = (_dot(h_hi, wr_hi_ref[...]) + _dot(h_lo, wr_hi_ref[...]) + _dot(h_hi, wr_lo_ref[...])
          + br_ref[...])
    first = lambda cond: jnp.min(jnp.where(cond, lane_f, 1e9), axis=-1, keepdims=True)
    gl = jnp.where(lane < N_GROUPS, lg, -jnp.inf)
    gmax = jnp.max(gl, axis=-1, keepdims=True)
    gsel = first(gl == gmax)
    g_w = 1.0 / jnp.sum(jnp.exp(gl - gmax), axis=-1, keepdims=True)
    grp_of_lane = ((lane - N_GROUPS) >> (EXP_PER_GROUP.bit_length() - 1)).astype(F32)
    in_grp = (lane >= N_GROUPS) & (lane < N_GROUPS + N_EXPERTS) & (grp_of_lane == gsel)
    el = jnp.where(in_grp, lg, -jnp.inf)
    v1 = jnp.max(el, axis=-1, keepdims=True)
    i1 = first(el == v1)
    el2 = jnp.where(lane_f == i1, -jnp.inf, el)
    v2 = jnp.max(el2, axis=-1, keepdims=True)
    i2 = first(el2 == v2)
    r = jnp.exp(v2 - v1)
    w1 = 1.0 / (1.0 + r)
    gate = jnp.where(lane_f == i1, w1 * g_w, jnp.where(lane_f == i2, r * w1 * g_w, 0.0))

    onehot = jnp.where(lane_f == gsel, 1.0, 0.0)
    t_row = lax.broadcasted_iota(I32, (tm, tm), 0)
    t_col = lax.broadcasted_iota(I32, (tm, tm), 1)
    before = _dot(jnp.where(t_row > t_col, 1.0, 0.0).astype(BF16), onehot.astype(BF16))
    counts = jnp.sum(onehot, axis=0, keepdims=True)
    lane1 = lax.broadcasted_iota(I32, (1, LANES), 1)
    ends = []
    run = jnp.zeros((1, 1), F32)
    for g in range(N_GROUPS - 1):
        run = run + jnp.sum(jnp.where(lane1 == g, counts, 0.0), axis=-1, keepdims=True)
        ends.append(run)
    start_of = sum(jnp.where(lane1 == g + 1, ends[g], 0.0) for g in range(N_GROUPS - 1))
    pos = jnp.sum(onehot * (before + start_of), axis=-1, keepdims=True)
    to_sorted_t = jnp.where(pos == t_col.astype(F32), 1.0, 0.0)
    to_sorted = to_sorted_t.T.astype(BF16)
    hs_ref[...] = _dot(to_sorted, h_hi).astype(BF16)
    g_hi, g_mid, g_lo = _split3(gate)
    gs_ref[...] = _dot(to_sorted, g_hi) + _dot(to_sorted, g_mid) + _dot(to_sorted, g_lo)

    bounds = [e_[0, 0].astype(I32) for e_ in ends]
    lane_s = lax.broadcasted_iota(I32, (sub, LANES), 1)

    def slab(j, carry):
        r0 = j * sub
        rows = pl.ds(pl.multiple_of(r0, sub), sub)
        g_first = sum((b <= r0).astype(I32) for b in bounds)
        g_last = sum((b <= r0 + sub - 1).astype(I32) for b in bounds)
        hs = hs_ref[rows, :]
        gs = gs_ref[rows, :]
        ys_ref[rows, :] = jnp.zeros((sub, ys_ref.shape[1]), F32)

        def group(g, c):
            gu = _dot(hs, wgu_ref[g])
            acts = []
            for e in range(EXP_PER_GROUP):
                ge = jnp.sum(jnp.where(lane_s == g * EXP_PER_GROUP + e + N_GROUPS, gs, 0.0), axis=-1, keepdims=True)
                up = gu[:, 2 * e * D_EXPERT:(2 * e + 1) * D_EXPERT]
                acts.append((up * jax.nn.sigmoid(up) * gu[:, (2 * e + 1) * D_EXPERT:(2 * e + 2) * D_EXPERT] * ge).astype(BF16))
            ys_ref[rows, :] += _dot(jnp.concatenate(acts, axis=1), wdn_ref[g])
            return c

        lax.fori_loop(g_first, g_last + 1, group, 0)
        return carry

    lax.fori_loop(0, tm // sub, slab, 0)
    o_ref[...] = x + _dot(to_sorted_t.astype(BF16), ys_ref[...].astype(BF16))


def _moe(x2, gain, wr_hi, wr_lo, br, wgu, wdn, tm=512, sub=128):
    n, d = x2.shape
    full = lambda a: pl.BlockSpec(a.shape, lambda i: (0,) * a.ndim)
    once = lambda a: pl.BlockSpec(a.shape, lambda i: (0,) * a.ndim, pipeline_mode=pl.Buffered(1))
    return pl.pallas_call(
        functools.partial(_moe_kernel, tm=tm, sub=sub),
        out_shape=jax.ShapeDtypeStruct((n, d), F32),
        grid=(n // tm,),
        in_specs=[pl.BlockSpec((tm, d), lambda i: (i, 0)), full(gain), full(wr_hi), full(wr_lo), full(br),
                  once(wgu), once(wdn)],
        out_specs=pl.BlockSpec((tm, d), lambda i: (i, 0)),
        scratch_shapes=[pltpu.VMEM((tm, d), BF16), pltpu.VMEM((tm, LANES), F32), pltpu.VMEM((tm, d), F32)],
        compiler_params=_params("parallel"),
        name="moe",
    )(x2, gain, wr_hi, wr_lo, br, wgu, wdn)


def _rope_tables(positions):
    pos = positions.reshape(-1).astype(F32)[:, None]

    def cs(dim):
        inv = ROPE_THETA ** (-jnp.arange(0, dim, 2, dtype=F32) / dim)
        ang = pos * inv
        return jnp.cos(ang), jnp.sin(ang)

    c, s = cs(HEAD_DIM)
    ci, si = cs(IDX_DIM)
    return jnp.concatenate([c, s], 1), jnp.concatenate([ci, si, jnp.zeros_like(ci), jnp.zeros_like(ci)], 1)


def _reorder_in_weight(w):
    a = w[:, :4 * WIDTH]
    o = 4 * WIDTH
    qlat = w[:, o:o + Q_LORA]; o += Q_LORA
    k = w[:, o:o + WIDTH]; o += WIDTH
    v = w[:, o:o + WIDTH]; o += WIDTH
    misc = w[:, o:]
    pad = jnp.zeros((w.shape[0], LANES - misc.shape[1]), w.dtype)
    return jnp.concatenate([a, k, v, qlat, misc, pad], axis=1).astype(BF16)


def _row(v):
    return v.reshape(1, -1).astype(F32)


def kernel(x, mem, positions, norm_mix, w_in, lb_raw, a_gnorm, b_qlat_gain, b_wqb, b_wqidx, b_qnorm,
           b_knorm, b_kidx_norm, w_out, norm_x, norm_mem, x_wq, x_wk, x_wv, x_wo, x_qnorm, x_knorm,
           norm_ffn, w_rg, b_rg, w_re, b_re, w_gu, w_dn):
    bsz, seq, d = x.shape
    n_mem = mem.shape[1]
    depth = w_in.shape[0]
    assert d == D_MODEL and seq % (2 * DSA_KEY_TILE) == 0 and n_mem % LANES == 0, (x.shape, mem.shape)
    x2 = x.reshape(bsz * seq, d)
    mem2 = mem.reshape(bsz * n_mem, d)
    tabs = _rope_tables(positions)
    pad_lanes = lambda v: jnp.pad(v, ((0, 0), (0, LANES - v.shape[1])))
    for l in range(depth):
        u = _in_proj(x2, _row(norm_mix[l]), _reorder_in_weight(w_in[l]))
        o_a = _hgrn(u, lb_raw.astype(F32), _row(a_gnorm[l]), l, bsz, seq)
        q, k, vt, qi, ki, wt = _dsa_prep(
            u, tabs, b_wqb[l].astype(BF16), b_wqidx[l].astype(BF16), _row(b_qlat_gain[l]),
            _row(b_qnorm[l]), _row(b_knorm[l]), pad_lanes(_row(b_kidx_norm[l])))
        o_b = _dsa_attn(q, k, vt, qi, ki, wt, bsz, seq)
        km, vm = _mem_kv(mem2, _row(norm_mem[l]), x_wk[l].astype(BF16), x_wv[l].astype(BF16),
                         _row(x_knorm[l]), n_mem)
        x2 = _xattn(x2, o_a, o_b, w_out[l].astype(BF16), _row(norm_x[l]), x_wq[l].astype(BF16),
                    _row(x_qnorm[l]), km, vm, x_wo[l].astype(BF16), seq, n_mem)
        wr = pad_lanes(jnp.concatenate([w_rg[l], w_re[l].reshape(d, N_EXPERTS)], axis=1).astype(F32))
        wr_hi = wr.astype(BF16)
        wr_lo = (wr - wr_hi.astype(F32)).astype(BF16)
        br = pad_lanes(jnp.concatenate([b_rg[l], b_re[l].reshape(-1)]).reshape(1, -1).astype(F32))
        x2 = _moe(x2, _row(norm_ffn[l]), wr_hi, wr_lo, br,
                  w_gu[l].transpose(0, 2, 1, 3).reshape(N_GROUPS, d, EXP_PER_GROUP * 2 * D_EXPERT).astype(BF16),
                  w_dn[l].reshape(N_GROUPS, EXP_PER_GROUP * D_EXPERT, d).astype(BF16))
    return x2.reshape(bsz, seq, d)
```

```python
import functools

import numpy as np
import jax
import jax.numpy as jnp
from jax import lax
from jax.experimental import pallas as pl
from jax.experimental.pallas import tpu as pltpu

F32 = jnp.float32
BF16 = jnp.bfloat16
I32 = jnp.int32
I16 = jnp.int16

EPS = 1e-6
ROPE_THETA = 10000.0
CHUNK = 64
CHUNK_SHIFT = CHUNK.bit_length() - 1
SUB = 16
SUB_SHIFT = SUB.bit_length() - 1
N_HEADS = 4
HEAD_DIM = 128
WIDTH = N_HEADS * HEAD_DIM
Q_LORA = 256
D_MODEL = 1024
IDX_HEADS = 16
IDX_DIM = 64
ROT_I = IDX_DIM // 2
TOPK_MAX = 256
N_GROUPS = 4
EXP_PER_GROUP = 4
N_EXPERTS = N_GROUPS * EXP_PER_GROUP
D_EXPERT = 256
LANES = 128
INT_MIN = -2 ** 31
INT16_MIN, INT16_MAX = -2 ** 15, 2 ** 15 - 1
HALF_BITS = 16
PACK = 16
LOG2E = 1.4426950408889634
VT_ROWS = HEAD_DIM + PACK
DSA_KEY_TILE = 512
SCORE_ROWS = 128
POOL_ROWS = 256
SURE_PASSES = 8

U_K = 4 * WIDTH
U_V = U_K + WIDTH
U_QLAT = U_V + WIDTH
U_MISC = U_QLAT + Q_LORA
U_COLS = U_MISC + LANES

VMEM_LIMIT = 56 * 1024 * 1024


def _rms(x, gain):
    return x * lax.rsqrt(jnp.mean(x * x, axis=-1, keepdims=True) + EPS) * gain


def _dot(a, b):
    return jnp.dot(a, b, preferred_element_type=F32)


def _rms_head(x, gain, width=HEAD_DIM):
    ones = (lax.broadcasted_iota(I32, (LANES, LANES), 0) < width).astype(BF16)
    ss = _dot((x * x).astype(BF16), ones)
    return x * lax.rsqrt(ss * (1.0 / width) + EPS) * gain


def _dot_nt(a, b):
    return lax.dot_general(a, b, (((1,), (1,)), ((), ())), preferred_element_type=F32)


def _params(*sem):
    return pltpu.CompilerParams(dimension_semantics=sem, vmem_limit_bytes=VMEM_LIMIT)


def _in_proj_kernel(x_ref, g_ref, w_ref, u_ref, *, col_chunk):
    hb = _rms(x_ref[...], g_ref[...]).astype(BF16)
    for c0 in range(0, U_COLS, col_chunk):
        u_ref[:, c0:c0 + col_chunk] = _dot(hb, w_ref[:, c0:c0 + col_chunk])


def _in_proj(x2, gain, w_bf, tm=1024):
    n, d = x2.shape
    return pl.pallas_call(
        functools.partial(_in_proj_kernel, col_chunk=1152),
        out_shape=jax.ShapeDtypeStruct((n, U_COLS), F32),
        grid=(n // tm,),
        in_specs=[pl.BlockSpec((tm, d), lambda i: (i, 0)),
                  pl.BlockSpec((1, d), lambda i: (0, 0)),
                  pl.BlockSpec((d, U_COLS), lambda i: (0, 0), pipeline_mode=pl.Buffered(1))],
        out_specs=pl.BlockSpec((tm, U_COLS), lambda i: (i, 0)),
        compiler_params=_params("parallel"),
        name="in_proj",
    )(x2, gain, w_bf)


def _split3(x):
    hi = x.astype(BF16)
    r1 = x - hi.astype(F32)
    mid = r1.astype(BF16)
    lo = (r1 - mid.astype(F32)).astype(BF16)
    return hi, mid, lo


def _hgrn_kernel(q_ref, f_ref, i_ref, g_ref, lbraw_ref, gn_ref, e_ref, o_ref, st_ref,
                 *, layer, n_chunks, unroll):
    @pl.when(pl.program_id(2) == 0)
    def _():
        st_ref[...] = jnp.zeros_like(st_ref)

    lr = lbraw_ref[...]
    ex = jnp.exp(lr - jnp.max(lr, axis=0, keepdims=True))
    sm = ex / jnp.sum(ex, axis=0, keepdims=True)
    lb = jnp.zeros((1, HEAD_DIM), F32)
    for r in range(1, layer + 1):
        lb = lb + sm[r:r + 1, :]
    log_lb = jnp.log(lb)
    log_1mlb = jnp.log(1.0 - lb)

    row = lax.broadcasted_iota(I32, (CHUNK, HEAD_DIM), 0)
    col = lax.broadcasted_iota(I32, (CHUNK, HEAD_DIM), 1)
    row_l = row & (SUB - 1)
    row_b = row >> SUB_SHIFT
    col_b = col >> SUB_SHIFT
    tri = (lax.broadcasted_iota(I32, (CHUNK, CHUNK), 0)
           >= lax.broadcasted_iota(I32, (CHUNK, CHUNK), 1)).astype(BF16)
    n_sub = CHUNK // SUB
    zeros_c = jnp.zeros((CHUNK, HEAD_DIM), F32)

    def sub_bcast(t, s):
        t4 = t.reshape(n_sub, SUB, HEAD_DIM)
        return jnp.broadcast_to(t4[:, s:s + 1, :], (n_sub, SUB, HEAD_DIM)).reshape(CHUNK, HEAD_DIM)

    def gates(z):
        ls = jnp.minimum(z, 0.0) - jnp.log(1.0 + jnp.exp(-jnp.abs(z)))
        if layer == 0:
            return ls, (ls - z) * LOG2E
        cc = log_1mlb + ls
        log_f = jnp.maximum(log_lb, cc) + jnp.log(1.0 + jnp.exp(-jnp.abs(log_lb - cc)))
        return log_f, (cc - z) * LOG2E

    def diag_terms(q, b2, c2):
        pieces = []
        for s in range(SUB):
            d = jnp.where(row_l >= s, b2 - sub_bcast(c2, s), -jnp.inf)
            pieces.append((q * jnp.exp2(d)).astype(BF16))
        return jnp.concatenate(pieces, axis=1)

    def below_keys(b2, c2):
        kts = []
        for i in range(1, n_sub):
            r_i = b2[i * SUB:i * SUB + 1, :]
            kts.append(jnp.exp2(jnp.where(row < i * SUB, r_i - c2, -jnp.inf)))
            kts.append(zeros_c)
        return jnp.concatenate(kts, axis=0).astype(BF16)

    def below_scores(r):
        a = jnp.zeros((CHUNK, HEAD_DIM), F32)
        for i in range(1, n_sub):
            a = a + jnp.where(row_b == i, r[:, (i - 1) * LANES:i * LANES], 0.0)
        return a

    def chunks(it, carry):
        cs = range(unroll)
        rows = [pl.ds(pl.multiple_of((it * unroll + c) * CHUNK, CHUNK), CHUNK) for c in cs]
        qr = [q_ref[r, :] for r in rows]
        v = [i_ref[r, :] for r in rows]
        q = [x * jax.nn.sigmoid(x) for x in qr]
        gt = [gates(f_ref[r, :]) for r in rows]
        b2 = []
        for c in cs:
            hi, mid, lo = _split3(gt[c][0])
            b2.append((_dot(tri, hi) + _dot(tri, mid) + _dot(tri, lo)) * LOG2E)
        c2 = [b2[c] - gt[c][1] for c in cs]
        b_last = [x[CHUNK - 1:CHUNK, :] for x in b2]
        a2 = [_dot(diag_terms(q[c], b2[c], c2[c]), e_ref[...]) for c in cs]
        rr = [_dot_nt((q[c] * jnp.exp2(b2[c] - sub_bcast(b2[c], 0))).astype(BF16), below_keys(b2[c], c2[c]))
              for c in cs]
        upd = [_dot(v[c].T.astype(BF16), jnp.exp2(b_last[c] - c2[c]).astype(BF16)) for c in cs]
        qe = [(q[c] * jnp.exp2(b2[c])).astype(BF16) for c in cs]
        st = st_ref[...]
        o_inter = []
        for c in cs:
            o_inter.append(_dot_nt(qe[c], st.astype(BF16)))
            st = st * jnp.exp2(b_last[c]) + upd[c]
        st_ref[...] = st
        a = [jnp.where(col_b == row_b, a2[c], 0.0) + below_scores(rr[c]) for c in cs]
        o_intra = [_dot(a[c][:, :CHUNK].astype(BF16), v[c].astype(BF16)) for c in cs]
        for c in cs:
            g = g_ref[rows[c], :]
            o_ref[rows[c], :] = _rms(o_inter[c] + o_intra[c], gn_ref[...]) * (g * jax.nn.sigmoid(g))
        return carry

    lax.fori_loop(0, n_chunks // unroll, chunks, 0)


def _hgrn_selector():
    e = np.zeros((SUB * HEAD_DIM, LANES), np.float32)
    s_of_row = np.arange(SUB * HEAD_DIM) // HEAD_DIM
    cols = np.arange(LANES)
    e[:, :] = ((cols[None, :] % SUB) == s_of_row[:, None]) & (cols[None, :] < CHUNK)
    return jnp.asarray(e, BF16)


def _hgrn(u, lb_raw, gn, layer, bsz, seq, tb=1024):
    n = bsz * seq
    nb = seq // tb
    blk = lambda k: pl.BlockSpec((tb, HEAD_DIM), lambda b, h, j, k=k: (b * nb + j, h + N_HEADS * k))
    n_layers = lb_raw.shape[0]
    return pl.pallas_call(
        functools.partial(_hgrn_kernel, layer=layer, n_chunks=tb // CHUNK, unroll=16),
        out_shape=jax.ShapeDtypeStruct((n, WIDTH), F32),
        grid=(bsz, N_HEADS, nb),
        in_specs=[blk(0), blk(1), blk(2), blk(3),
                  pl.BlockSpec((n_layers, HEAD_DIM), lambda b, h, j: (0, h)),
                  pl.BlockSpec((1, HEAD_DIM), lambda b, h, j: (0, 0)),
                  pl.BlockSpec((SUB * HEAD_DIM, LANES), lambda b, h, j: (0, 0))],
        out_specs=pl.BlockSpec((tb, HEAD_DIM), lambda b, h, j: (b * nb + j, h)),
        scratch_shapes=[pltpu.VMEM((HEAD_DIM, HEAD_DIM), F32)],
        compiler_params=_params("parallel", "parallel", "arbitrary"),
        name="hgrn",
    )(u, u, u, u, lb_raw, gn, _hgrn_selector())


def _rope128(x, cos, sin_signed):
    return x * cos + pltpu.roll(x, HEAD_DIM // 2, 1) * sin_signed


def _rope64(x, cos, sin_a, sin_b):
    return x * cos + pltpu.roll(x, LANES - ROT_I, 1) * sin_a + pltpu.roll(x, ROT_I, 1) * sin_b


def _dsa_prep_kernel(k_ref, v_ref, ql_ref, misc_ref, ta_ref, tb_ref,
                     wqb_ref, wqi_ref, gql_ref, gqn_ref, gkn_ref, gki_ref,
                     q_out, k_out, vt_out, qi_out, ki_out, wt_out):
    ta, tb = ta_ref[...], tb_ref[...]
    lane = lax.broadcasted_iota(I32, ta.shape, 1)
    ta_swapped = pltpu.roll(ta, HEAD_DIM // 2, 1)
    cm = jnp.where(lane < HEAD_DIM // 2, ta, ta_swapped)
    sm = jnp.where(lane < HEAD_DIM // 2, -ta_swapped, ta)
    quarter = lane >> (ROT_I.bit_length() - 1)
    r32, r64, r96 = (pltpu.roll(tb, sh * ROT_I, 1) for sh in (1, 2, 3))
    ci = jnp.where(quarter == 0, tb, jnp.where(quarter == 1, r32, jnp.where(quarter == 2, r64, r96)))
    si = jnp.where(quarter == 0, r96, jnp.where(quarter == 1, tb, jnp.where(quarter == 2, r32, r64)))
    sia = jnp.where((lane & ROT_I) == 0, -si, 0.0)
    sib = jnp.where((lane & ROT_I) != 0, si, 0.0)
    cb = _rms(ql_ref[...], gql_ref[...]).astype(BF16)
    qf = _dot(cb, wqb_ref[...])
    scale = HEAD_DIM ** -0.5
    for h in range(N_HEADS):
        sl = slice(h * HEAD_DIM, (h + 1) * HEAD_DIM)
        qh = _rope128(_rms_head(qf[:, sl], gqn_ref[...]), cm, sm)
        q_out[:, sl] = (qh * (scale * LOG2E)).astype(BF16)
        kh = _rope128(_rms_head(k_ref[:, sl], gkn_ref[...]), cm, sm)
        k_out[:, sl] = kh.astype(BF16)
    vt = v_ref[...].T.astype(BF16)
    ones = jnp.ones((PACK, vt.shape[1]), BF16)
    for h in range(N_HEADS):
        vt_out[0, h * VT_ROWS:h * VT_ROWS + HEAD_DIM, :] = vt[h * HEAD_DIM:(h + 1) * HEAD_DIM, :]
        vt_out[0, h * VT_ROWS + HEAD_DIM:(h + 1) * VT_ROWS, :] = ones
    qi = _dot(cb, wqi_ref[...])
    for p in range(IDX_HEADS * IDX_DIM // LANES):
        sl = slice(p * LANES, (p + 1) * LANES)
        qi_out[:, sl] = _rope64(qi[:, sl], ci, sia, sib).astype(BF16)
    misc = misc_ref[...]
    kn = _rope64(_rms_head(misc, gki_ref[...], IDX_DIM), ci, sia, sib)
    ki_out[...] = (kn + pltpu.roll(kn, IDX_DIM, 1)).astype(BF16)
    wt = (misc * (IDX_HEADS ** -0.5 * IDX_DIM ** -0.5)).T
    wt_out[...] = wt[IDX_DIM:IDX_DIM + IDX_HEADS, :]


def _dsa_prep(u, tabs, wqb, wqi, gql, gqn, gkn, gki, tm=DSA_KEY_TILE):
    n = u.shape[0]
    nt = n // tm
    ublk = lambda width, col: pl.BlockSpec((tm, width), lambda i: (i, col // width))
    tab = pl.BlockSpec((tm, LANES), lambda i: (i, 0))
    full = lambda a: pl.BlockSpec(a.shape, lambda i: (0,) * a.ndim)
    return pl.pallas_call(
        _dsa_prep_kernel,
        out_shape=(jax.ShapeDtypeStruct((n, WIDTH), BF16),
                   jax.ShapeDtypeStruct((n, WIDTH), BF16),
                   jax.ShapeDtypeStruct((nt, N_HEADS * VT_ROWS, tm), BF16),
                   jax.ShapeDtypeStruct((n, IDX_HEADS * IDX_DIM), BF16),
                   jax.ShapeDtypeStruct((n, LANES), BF16),
                   jax.ShapeDtypeStruct((IDX_HEADS, n), F32)),
        grid=(nt,),
        in_specs=[ublk(WIDTH, U_K), ublk(WIDTH, U_V), ublk(Q_LORA, U_QLAT), ublk(LANES, U_MISC),
                  tab, tab,
                  full(wqb), full(wqi), full(gql), full(gqn), full(gkn), full(gki)],
        out_specs=(pl.BlockSpec((tm, WIDTH), lambda i: (i, 0)),
                   pl.BlockSpec((tm, WIDTH), lambda i: (i, 0)),
                   pl.BlockSpec((1, N_HEADS * VT_ROWS, tm), lambda i: (i, 0, 0)),
                   pl.BlockSpec((tm, IDX_HEADS * IDX_DIM), lambda i: (i, 0)),
                   pl.BlockSpec((tm, LANES), lambda i: (i, 0)),
                   pl.BlockSpec((IDX_HEADS, tm), lambda i: (0, i))),
        compiler_params=_params("parallel"),
        name="dsa_prep",
    )(u, u, u, u, *tabs, wqb, wqi, gql, gqn, gkn, gki)


def _sortable(x):
    bits = lax.bitcast_convert_type(x, I32)
    return bits ^ ((bits >> 31) & 0x7FFFFFFF)


def _dsa_attn_kernel(qi_ref, wt_ref, qin_ref, wtn_ref, q_ref, ki_ref, k_ref, vt_ref, o_ref,
                     qm_ref, hi_ref, lo_ref, kb_ref, pk_ref, lt_ref, acc0, acc1, acc2, acc3,
                     *, tq, kt, top_k, n_blocks):
    acc_refs = (acc0, acc1, acc2, acc3)
    blk = pl.program_id(1)
    cur = blk & 1
    nxt = 1 - cur
    q0 = blk * tq
    n_tiles = (q0 + tq + kt - 1) // kt
    n_pairs = (n_tiles + 1) // 2
    pooled_per_tile = kt // POOL_ROWS * 2 * PACK
    sub = SCORE_ROWS
    key_iota = lax.broadcasted_iota(I32, (sub, tq), 0)
    lane_q = lax.broadcasted_iota(I32, (tq, LANES), 1)

    def tile_rows(jt):
        return pl.ds(pl.multiple_of(jt * kt, kt), kt)

    def limit_of(q_start):
        qpos = q_start + lax.broadcasted_iota(I32, (1, tq), 1)
        return ((qpos >> CHUNK_SHIFT) + 1) << CHUNK_SHIFT

    def build_qm(src_ref):
        for h in range(IDX_HEADS):
            src = src_ref[:, (h // 2) * LANES:(h // 2 + 1) * LANES].astype(F32)
            keep = (lane_q < IDX_DIM) if h % 2 == 0 else (lane_q >= IDX_DIM)
            qm_ref[h * tq:(h + 1) * tq, :] = jnp.where(keep, src, 0.0).astype(BF16)

    def score_tile(jt, slot, limit, w_ref):
        for part in range(kt // sub):
            base = jt * kt + part * sub
            rows = pl.ds(pl.multiple_of(base, sub), sub)
            ki = ki_ref[rows, :]
            acc = jnp.zeros((sub, tq), F32)
            for h in range(IDX_HEADS):
                x = _dot_nt(ki, qm_ref[h * tq:(h + 1) * tq, :])
                acc = acc + w_ref[h:h + 1, :] * jnp.maximum(x, 0.0)
            key = jnp.where(base + key_iota < limit, _sortable(acc), INT_MIN)
            hi_ref[slot, rows, :] = (key >> HALF_BITS).astype(I16)
            lo_ref[slot, rows, :] = ((key & (2 ** HALF_BITS - 1)) + INT16_MIN).astype(I16)

    @pl.when(blk == 0)
    def _():
        build_qm(qi_ref)
        limit0 = limit_of(0)

        def body(jt, carry):
            score_tile(jt, 0, limit0, wt_ref)
            return carry
        lax.fori_loop(0, n_tiles, body, 0)

    one, zero = jnp.ones((), BF16), jnp.zeros((), BF16)
    neg_inf = jnp.full((), -jnp.inf, BF16)
    n_rows = (n_tiles * kt).astype(F32)
    hi_rows = lambda rows: hi_ref[cur, rows, :]
    lo_rows = lambda rows: lo_ref[cur, rows, :]
    kb_rows = lambda rows: kb_ref[rows, :]

    def count16(get, trial, strict, trips=None):
        t16 = jnp.broadcast_to(trial.astype(I16), (PACK, tq))

        def body(jt, c):
            x = get(tile_rows(jt))
            parts = [jnp.zeros((PACK, tq), BF16) for _ in range(4)]
            for r in range(kt // PACK):
                xr = x[r * PACK:(r + 1) * PACK, :]
                parts[r % 4] = parts[r % 4] + jnp.where((xr > t16) if strict else (xr >= t16), one, zero)
            return c + ((parts[0] + parts[1]) + (parts[2] + parts[3])).astype(F32)
        c = lax.fori_loop(0, n_tiles if trips is None else trips, body, jnp.zeros((PACK, tq), F32))
        return jnp.sum(c, axis=0, keepdims=True)

    def bisect16(get, want, n_all, early_exit, trips=None):
        def step(it, carry):
            lo, n_lo, n_up = carry
            trial = lo + lax.shift_left(jnp.int32(1), HALF_BITS - 1 - it)
            n = count16(get, trial, False, trips)
            ok = n >= want
            return jnp.where(ok, trial, lo), jnp.where(ok, n, n_lo), jnp.where(ok, n_up, n)

        init = (jnp.full((1, tq), INT16_MIN, I32), n_all, jnp.zeros((1, tq), F32))
        if not early_exit:
            return lax.fori_loop(0, HALF_BITS, step, init)

        def unresolved(carry):
            it, _, n_lo, _ = carry
            return (it < HALF_BITS) & (jnp.max(jnp.where(n_lo != want, 1.0, 0.0)) > 0.0)

        def two_steps(carry):
            return (carry[0] + 2,) + step(carry[0] + 1, step(carry[0], carry[1:]))

        return lax.while_loop(unresolved, two_steps, (jnp.int32(SURE_PASSES),) + lax.fori_loop(0, SURE_PASSES, step, init))[1:]

    tau_hi, n_hi_ge, n_hi_gt = bisect16(hi_rows, float(top_k), jnp.full((1, tq), n_rows, F32), False)
    want_lo = top_k - n_hi_gt
    tau_hi16 = jnp.broadcast_to(tau_hi.astype(I16), (kt, tq))

    min16 = jnp.full((PACK, tq), INT16_MIN, I16)

    def bucket_tile(jt, carry):
        rows = tile_rows(jt)
        kb = jnp.where(hi_rows(rows) == tau_hi16, lo_rows(rows), jnp.full((), INT16_MIN, I16))
        kb_ref[rows, :] = kb
        for g in range(kt // POOL_ROWS):
            top1, top2 = min16, min16
            for r in range(POOL_ROWS // PACK):
                x = kb[g * POOL_ROWS + r * PACK:g * POOL_ROWS + (r + 1) * PACK, :]
                below = x < top1
                runner = jnp.where(below, x, top1)
                top2 = jnp.where(runner > top2, runner, top2)
                top1 = jnp.where(below, top1, x)
            base = pl.multiple_of(jt * pooled_per_tile + g * 2 * PACK, 2 * PACK)
            pk_ref[pl.ds(base, PACK), :] = top1
            pk_ref[pl.ds(base + PACK, PACK), :] = top2
        return carry

    lax.fori_loop(0, n_tiles, bucket_tile, 0)
    pooled_trips = (n_tiles * pooled_per_tile + kt - 1) // kt

    def pad_pooled(jt, carry):
        pk_ref[pl.ds(pl.multiple_of(jt * pooled_per_tile, pooled_per_tile), pooled_per_tile), :] = jnp.full(
            (pooled_per_tile, tq), INT16_MIN, I16)
        return carry

    lax.fori_loop(n_tiles, pooled_trips * (kt // pooled_per_tile), pad_pooled, 0)
    n_bucket = n_hi_ge - n_hi_gt
    pk_rows = lambda rows: pk_ref[rows, :]
    tau_p, _, _ = bisect16(pk_rows, want_lo, n_bucket, True, pooled_trips)
    n_ge_p = jnp.where(tau_p == INT16_MIN, n_bucket, count16(kb_rows, tau_p, False))
    n_gt_p = count16(kb_rows, tau_p, True)
    pooled_ok = (n_ge_p == want_lo) | ((n_ge_p > want_lo) & (n_gt_p < want_lo))

    def full_search():
        t, n_ge, _ = bisect16(kb_rows, want_lo, n_bucket, True)
        return t, n_ge, count16(kb_rows, t, True)

    tau_lo, n_kb_ge, n_kb_gt = lax.cond(
        jnp.min(jnp.where(pooled_ok, 1.0, 0.0)) > 0.0,
        lambda: (tau_p, n_ge_p, n_gt_p), full_search)
    live = (tau_hi > INT16_MIN) | (tau_lo > INT16_MIN)
    excess = jnp.max(jnp.where(live, n_kb_ge - want_lo, 0.0))

    @pl.when(excess > 0)
    def _():
        need = want_lo - n_kb_gt
        live_f = jnp.where(live, 1.0, 0.0)
        tri = (lax.broadcasted_iota(I32, (kt, kt), 0)
               >= lax.broadcasted_iota(I32, (kt, kt), 1)).astype(BF16)

        def body(jt, seen):
            rows = tile_rows(jt)
            h32 = hi_rows(rows).astype(I32)
            eq = jnp.where(h32 == tau_hi, jnp.where(lo_rows(rows).astype(I32) == tau_lo, 1.0, 0.0), 0.0)
            rank = _dot(tri, eq.astype(BF16)) + seen
            dropped = eq * (1.0 - jnp.where(rank <= need, live_f, 0.0))
            hi_ref[cur, rows, :] = jnp.where(dropped > 0.0, INT16_MIN, h32).astype(I16)
            return rank[kt - 1:kt, :]
        lax.fori_loop(0, n_tiles, body, jnp.zeros((1, tq), F32))

    tau_lo16 = jnp.broadcast_to(jnp.where(live, tau_lo, INT16_MAX).astype(I16), (kt, tq))

    def bias_rows(rows):
        h16 = hi_rows(rows)
        in_bucket = jnp.where(lo_rows(rows) >= tau_lo16, zero, neg_inf)
        return jnp.where(h16 > tau_hi16, zero, jnp.where(h16 == tau_hi16, in_bucket, neg_inf)).astype(F32)

    for a in acc_refs:
        a[...] = jnp.zeros_like(a)
    heads = [slice(h * HEAD_DIM, (h + 1) * HEAD_DIM) for h in range(N_HEADS)]

    @pl.when((n_tiles & 1) == 1)
    def _():
        hi_ref[cur, tile_rows(n_tiles), :] = jnp.full((kt, tq), INT16_MIN, I16)
        lo_ref[cur, tile_rows(n_tiles), :] = jnp.full((kt, tq), INT16_MIN, I16)

    limit_next = limit_of(q0 + tq)

    def attn_pair(jp, ms, score_next):
        tiles = (2 * jp, 2 * jp + 1)
        for i, jt in enumerate(tiles):
            rows = tile_rows(jt)
            bias = bias_rows(rows)
            for h in range(N_HEADS):
                lt_ref[i, h] = _dot_nt(k_ref[rows, heads[h]], q_ref[:, heads[h]]) + bias
        ms = list(ms)
        for i, jt in enumerate(tiles):
            if score_next:
                score_tile(jt, nxt, limit_next, wtn_ref)
            alphas, ps = [], []
            for h in range(N_HEADS):
                m_new = jnp.maximum(ms[h], jnp.max(lt_ref[i, h], axis=0, keepdims=True))
                m_safe = jnp.where(m_new == -jnp.inf, 0.0, m_new)
                ps.append(jnp.exp2(lt_ref[i, h] - m_safe).astype(BF16))
                alphas.append(jnp.exp2(ms[h] - m_safe))
                ms[h] = m_new
            for h in range(N_HEADS):
                acc_refs[h][...] = (acc_refs[h][...] * alphas[h]
                                    + _dot(vt_ref[jt, h * VT_ROWS:(h + 1) * VT_ROWS, :], ps[h]))
        return tuple(ms)

    m_init = tuple(jnp.full((1, tq), -jnp.inf, F32) for _ in range(N_HEADS))

    @pl.when(blk < n_blocks - 1)
    def _():
        build_qm(qin_ref)
        lax.fori_loop(0, n_pairs, functools.partial(attn_pair, score_next=True), m_init)

        @pl.when((q0 + 2 * tq + kt - 1) // kt > 2 * n_pairs)
        def _():
            score_tile(2 * n_pairs, nxt, limit_next, wtn_ref)

    @pl.when(blk == n_blocks - 1)
    def _():
        lax.fori_loop(0, n_pairs, functools.partial(attn_pair, score_next=False), m_init)

    for h in range(N_HEADS):
        acc = acc_refs[h][...]
        out_t = acc[:HEAD_DIM, :] * (1.0 / acc[HEAD_DIM:HEAD_DIM + 1, :])
        o_ref[:, heads[h]] = out_t.T


def _dsa_attn(q, k, vt, qi, ki, wt, bsz, seq, tq=256, kt=DSA_KEY_TILE):
    n = bsz * seq
    nq = seq // tq
    top_k = min(TOPK_MAX, seq // 4)
    once = dict(pipeline_mode=pl.Buffered(1))
    nxt_blk = lambda b, i: b * nq + jnp.minimum(i + 1, nq - 1)
    return pl.pallas_call(
        functools.partial(_dsa_attn_kernel, tq=tq, kt=kt, top_k=top_k, n_blocks=nq),
        out_shape=jax.ShapeDtypeStruct((n, WIDTH), F32),
        grid=(bsz, nq),
        in_specs=[pl.BlockSpec((tq, IDX_HEADS * IDX_DIM), lambda b, i: (b * nq + i, 0)),
                  pl.BlockSpec((IDX_HEADS, tq), lambda b, i: (0, b * nq + i)),
                  pl.BlockSpec((tq, IDX_HEADS * IDX_DIM), lambda b, i: (nxt_blk(b, i), 0)),
                  pl.BlockSpec((IDX_HEADS, tq), lambda b, i: (0, nxt_blk(b, i))),
                  pl.BlockSpec((tq, WIDTH), lambda b, i: (b * nq + i, 0)),
                  pl.BlockSpec((seq, LANES), lambda b, i: (b, 0), **once),
                  pl.BlockSpec((seq, WIDTH), lambda b, i: (b, 0), **once),
                  pl.BlockSpec((seq // kt, N_HEADS * VT_ROWS, kt), lambda b, i: (b, 0, 0), **once)],
        out_specs=pl.BlockSpec((tq, WIDTH), lambda b, i: (b * nq + i, 0)),
        scratch_shapes=[pltpu.VMEM((IDX_HEADS * tq, LANES), BF16),
                        pltpu.VMEM((2, seq, tq), I16), pltpu.VMEM((2, seq, tq), I16), pltpu.VMEM((seq, tq), I16),
                        pltpu.VMEM((pl.cdiv(seq // POOL_ROWS * 2 * PACK, kt) * kt, tq), I16),
                        pltpu.VMEM((2, N_HEADS, kt, tq), F32)]
                       + [pltpu.VMEM((VT_ROWS, tq), F32) for _ in range(N_HEADS)],
        compiler_params=_params("arbitrary", "arbitrary"),
        name="dsa_attn",
    )(qi, wt, qi, wt, q, ki, k, vt)


def _mem_kv_kernel(mem_ref, g_ref, wk_ref, wv_ref, gk_ref, k_out, v_out):
    mb = _rms(mem_ref[...], g_ref[...]).astype(BF16)
    kf = _dot(mb, wk_ref[...])
    for h in range(N_HEADS):
        sl = slice(h * HEAD_DIM, (h + 1) * HEAD_DIM)
        k_out[:, sl] = _rms_head(kf[:, sl], gk_ref[...]).astype(BF16)
    v_out[...] = _dot(mb, wv_ref[...]).astype(BF16)


def _mem_kv(mem2, gain, wk, wv, gk, n_mem):
    n, d = mem2.shape
    full = lambda a: pl.BlockSpec(a.shape, lambda b: (0,) * a.ndim)
    return pl.pallas_call(
        _mem_kv_kernel,
        out_shape=(jax.ShapeDtypeStruct((n, WIDTH), BF16), jax.ShapeDtypeStruct((n, WIDTH), BF16)),
        grid=(n // n_mem,),
        in_specs=[pl.BlockSpec((n_mem, d), lambda b: (b, 0)), full(gain), full(wk), full(wv), full(gk)],
        out_specs=(pl.BlockSpec((n_mem, WIDTH), lambda b: (b, 0)),
                   pl.BlockSpec((n_mem, WIDTH), lambda b: (b, 0))),
        compiler_params=_params("parallel"),
        name="mem_kv",
    )(mem2, gain, wk, wv, gk)


def _xattn_kernel(x_ref, oa_ref, ob_ref, wout_ref, g_ref, wq_ref, gq_ref, km_ref, vm_ref, wo_ref, o_ref):
    x1 = (x_ref[...] + _dot(oa_ref[...].astype(BF16), wout_ref[:WIDTH, :])
          + _dot(ob_ref[...].astype(BF16), wout_ref[WIDTH:, :]))
    hb = _rms(x1, g_ref[...]).astype(BF16)
    qf = _dot(hb, wq_ref[...])
    scale = HEAD_DIM ** -0.5
    outs = []
    for h in range(N_HEADS):
        sl = slice(h * HEAD_DIM, (h + 1) * HEAD_DIM)
        qh = (_rms(qf[:, sl], gq_ref[...]) * scale).astype(BF16)
        logits = _dot_nt(qh, km_ref[:, sl])
        p = jnp.exp(logits - jnp.max(logits, axis=-1, keepdims=True))
        oh = _dot(p.astype(BF16), vm_ref[:, sl])
        outs.append((oh * (1.0 / jnp.sum(p, axis=-1, keepdims=True))).astype(BF16))
    o_ref[...] = x1 + _dot(jnp.concatenate(outs, axis=1), wo_ref[...])


def _xattn(x2, oa, ob, wout, gain, wq, gq, km, vm, wo, seq, n_mem, tm=1024):
    n, d = x2.shape
    per_b = seq // tm
    row = lambda w: pl.BlockSpec((tm, w), lambda i: (i, 0))
    full = lambda a: pl.BlockSpec(a.shape, lambda i: (0,) * a.ndim)
    memb = pl.BlockSpec((n_mem, WIDTH), lambda i: (i // per_b, 0))
    return pl.pallas_call(
        _xattn_kernel,
        out_shape=jax.ShapeDtypeStruct((n, d), F32),
        grid=(n // tm,),
        in_specs=[row(d), row(WIDTH), row(WIDTH), full(wout), full(gain), full(wq), full(gq),
                  memb, memb, full(wo)],
        out_specs=row(d),
        compiler_params=_params("parallel"),
        name="xattn",
    )(x2, oa, ob, wout, gain, wq, gq, km, vm, wo)


def _moe_kernel(x_ref, g_ref, wr_hi_ref, wr_lo_ref, br_ref, wgu_ref, wdn_ref, o_ref, hs_ref, gs_ref, ys_ref,
                *, tm, sub):
    lane = lax.broadcasted_iota(I32, (tm, LANES), 1)
    lane_f = lane.astype(F32)
    x = x_ref[...]
    h = _rms(x, g_ref[...])
    h_hi = h.astype(BF16)
    h_lo = (h - h_hi.astype(F32)).astype(BF16)
    lg = (_dot(h_hi, wr_hi_ref[...]) + _dot(h_lo, wr_hi_ref[...]) + _dot(h_hi, wr_lo_ref[...])
          + br_ref[...])
    first = lambda cond: jnp.min(jnp.where(cond, lane_f, 1e9), axis=-1, keepdims=True)
    gl = jnp.where(lane < N_GROUPS, lg, -jnp.inf)
    gmax = jnp.max(gl, axis=-1, keepdims=True)
    gsel = first(gl == gmax)
    g_w = 1.0 / jnp.sum(jnp.exp(gl - gmax), axis=-1, keepdims=True)
    grp_of_lane = ((lane - N_GROUPS) >> (EXP_PER_GROUP.bit_length() - 1)).astype(F32)
    in_grp = (lane >= N_GROUPS) & (lane < N_GROUPS + N_EXPERTS) & (grp_of_lane == gsel)
    el = jnp.where(in_grp, lg, -jnp.inf)
    v1 = jnp.max(el, axis=-1, keepdims=True)
    i1 = first(el == v1)
    el2 = jnp.where(lane_f == i1, -jnp.inf, el)
    v2 = jnp.max(el2, axis=-1, keepdims=True)
    i2 = first(el2 == v2)
    r = jnp.exp(v2 - v1)
    w1 = 1.0 / (1.0 + r)
    gate = jnp.where(lane_f == i1, w1 * g_w, jnp.where(lane_f == i2, r * w1 * g_w, 0.0))

    onehot = jnp.where(lane_f == gsel, 1.0, 0.0)
    t_row = lax.broadcasted_iota(I32, (tm, tm), 0)
    t_col = lax.broadcasted_iota(I32, (tm, tm), 1)
    before = _dot(jnp.where(t_row > t_col, 1.0, 0.0).astype(BF16), onehot.astype(BF16))
    counts = jnp.sum(onehot, axis=0, keepdims=True)
    lane1 = lax.broadcasted_iota(I32, (1, LANES), 1)
    ends = []
    run = jnp.zeros((1, 1), F32)
    for g in range(N_GROUPS - 1):
        run = run + jnp.sum(jnp.where(lane1 == g, counts, 0.0), axis=-1, keepdims=True)
        ends.append(run)
    start_of = sum(jnp.where(lane1 == g + 1, ends[g], 0.0) for g in range(N_GROUPS - 1))
    pos = jnp.sum(onehot * (before + start_of), axis=-1, keepdims=True)
    to_sorted_t = jnp.where(pos == t_col.astype(F32), 1.0, 0.0)
    to_sorted = to_sorted_t.T.astype(BF16)
    hs_ref[...] = _dot(to_sorted, h_hi).astype(BF16)
    g_hi, g_mid, g_lo = _split3(gate)
    gs_ref[...] = _dot(to_sorted, g_hi) + _dot(to_sorted, g_mid) + _dot(to_sorted, g_lo)

    bounds = [e_[0, 0].astype(I32) for e_ in ends]
    lane_s = lax.broadcasted_iota(I32, (sub, LANES), 1)

    def slab(j, carry):
        r0 = j * sub
        rows = pl.ds(pl.multiple_of(r0, sub), sub)
        g_first = sum((b <= r0).astype(I32) for b in bounds)
        g_last = sum((b <= r0 + sub - 1).astype(I32) for b in bounds)
        hs = hs_ref[rows, :]
        gs = gs_ref[rows, :]
        ys_ref[rows, :] = jnp.zeros((sub, ys_ref.shape[1]), F32)

        def group(g, c):
            gu = _dot(hs, wgu_ref[g])
            acts = []
            for e in range(EXP_PER_GROUP):
                ge = jnp.sum(jnp.where(lane_s == g * EXP_PER_GROUP + e + N_GROUPS, gs, 0.0), axis=-1, keepdims=True)
                up = gu[:, 2 * e * D_EXPERT:(2 * e + 1) * D_EXPERT]
                acts.append((up * jax.nn.sigmoid(up) * gu[:, (2 * e + 1) * D_EXPERT:(2 * e + 2) * D_EXPERT] * ge).astype(BF16))
            ys_ref[rows, :] += _dot(jnp.concatenate(acts, axis=1), wdn_ref[g])
            return c

        lax.fori_loop(g_first, g_last + 1, group, 0)
        return carry

    lax.fori_loop(0, tm // sub, slab, 0)
    o_ref[...] = x + _dot(to_sorted_t.astype(BF16), ys_ref[...].astype(BF16))


def _moe(x2, gain, wr_hi, wr_lo, br, wgu, wdn, tm=512, sub=128):
    n, d = x2.shape
    full = lambda a: pl.BlockSpec(a.shape, lambda i: (0,) * a.ndim)
    once = lambda a: pl.BlockSpec(a.shape, lambda i: (0,) * a.ndim, pipeline_mode=pl.Buffered(1))
    return pl.pallas_call(
        functools.partial(_moe_kernel, tm=tm, sub=sub),
        out_shape=jax.ShapeDtypeStruct((n, d), F32),
        grid=(n // tm,),
        in_specs=[pl.BlockSpec((tm, d), lambda i: (i, 0)), full(gain), full(wr_hi), full(wr_lo), full(br),
                  once(wgu), once(wdn)],
        out_specs=pl.BlockSpec((tm, d), lambda i: (i, 0)),
        scratch_shapes=[pltpu.VMEM((tm, d), BF16), pltpu.VMEM((tm, LANES), F32), pltpu.VMEM((tm, d), F32)],
        compiler_params=_params("parallel"),
        name="moe",
    )(x2, gain, wr_hi, wr_lo, br, wgu, wdn)


def _rope_tables(positions):
    pos = positions.reshape(-1).astype(F32)[:, None]

    def cs(dim):
        inv = ROPE_THETA ** (-jnp.arange(0, dim, 2, dtype=F32) / dim)
        ang = pos * inv
        return jnp.cos(ang), jnp.sin(ang)

    c, s = cs(HEAD_DIM)
    ci, si = cs(IDX_DIM)
    return jnp.concatenate([c, s], 1), jnp.concatenate([ci, si, jnp.zeros_like(ci), jnp.zeros_like(ci)], 1)


def _reorder_in_weight(w):
    a = w[:, :4 * WIDTH]
    o = 4 * WIDTH
    qlat = w[:, o:o + Q_LORA]; o += Q_LORA
    k = w[:, o:o + WIDTH]; o += WIDTH
    v = w[:, o:o + WIDTH]; o += WIDTH
    misc = w[:, o:]
    pad = jnp.zeros((w.shape[0], LANES - misc.shape[1]), w.dtype)
    return jnp.concatenate([a, k, v, qlat, misc, pad], axis=1).astype(BF16)


def _row(v):
    return v.reshape(1, -1).astype(F32)


def kernel(x, mem, positions, norm_mix, w_in, lb_raw, a_gnorm, b_qlat_gain, b_wqb, b_wqidx, b_qnorm,
           b_knorm, b_kidx_norm, w_out, norm_x, norm_mem, x_wq, x_wk, x_wv, x_wo, x_qnorm, x_knorm,
           norm_ffn, w_rg, b_rg, w_re, b_re, w_gu, w_dn):
    bsz, seq, d = x.shape
    n_mem = mem.shape[1]
    depth = w_in.shape[0]
    assert d == D_MODEL and seq % (2 * DSA_KEY_TILE) == 0 and n_mem % LANES == 0, (x.shape, mem.shape)
    x2 = x.reshape(bsz * seq, d)
    mem2 = mem.reshape(bsz * n_mem, d)
    tabs = _rope_tables(positions)
    pad_lanes = lambda v: jnp.pad(v, ((0, 0), (0, LANES - v.shape[1])))
    for l in range(depth):
        u = _in_proj(x2, _row(norm_mix[l]), _reorder_in_weight(w_in[l]))
        o_a = _hgrn(u, lb_raw.astype(F32), _row(a_gnorm[l]), l, bsz, seq)
        q, k, vt, qi, ki, wt = _dsa_prep(
            u, tabs, b_wqb[l].astype(BF16), b_wqidx[l].astype(BF16), _row(b_qlat_gain[l]),
            _row(b_qnorm[l]), _row(b_knorm[l]), pad_lanes(_row(b_kidx_norm[l])))
        o_b = _dsa_attn(q, k, vt, qi, ki, wt, bsz, seq)
        km, vm = _mem_kv(mem2, _row(norm_mem[l]), x_wk[l].astype(BF16), x_wv[l].astype(BF16),
                         _row(x_knorm[l]), n_mem)
        x2 = _xattn(x2, o_a, o_b, w_out[l].astype(BF16), _row(norm_x[l]), x_wq[l].astype(BF16),
                    _row(x_qnorm[l]), km, vm, x_wo[l].astype(BF16), seq, n_mem)
        wr = pad_lanes(jnp.concatenate([w_rg[l], w_re[l].reshape(d, N_EXPERTS)], axis=1).astype(F32))
        wr_hi = wr.astype(BF16)
        wr_lo = (wr - wr_hi.astype(F32)).astype(BF16)
        br = pad_lanes(jnp.concatenate([b_rg[l], b_re[l].reshape(-1)]).reshape(1, -1).astype(F32))
        x2 = _moe(x2, _row(norm_ffn[l]), wr_hi, wr_lo, br,
                  w_gu[l].transpose(0, 2, 1, 3).reshape(N_GROUPS, d, EXP_PER_GROUP * 2 * D_EXPERT).astype(BF16),
                  w_dn[l].reshape(N_GROUPS, EXP_PER_GROUP * D_EXPERT, d).astype(BF16))
    return x2.reshape(bsz, seq, d)
```

```python
import functools

import numpy as np
import jax
import jax.numpy as jnp
from jax import lax
from jax.experimental import pallas as pl
from jax.experimental.pallas import tpu as pltpu

F32 = jnp.float32
BF16 = jnp.bfloat16
I32 = jnp.int32
I16 = jnp.int16

EPS = 1e-6
ROPE_THETA = 10000.0
CHUNK = 64
CHUNK_SHIFT = CHUNK.bit_length() - 1
SUB = 16
SUB_SHIFT = SUB.bit_length() - 1
N_HEADS = 4
HEAD_DIM = 128
WIDTH = N_HEADS * HEAD_DIM
Q_LORA = 256
D_MODEL = 1024
IDX_HEADS = 16
IDX_DIM = 64
ROT_I = IDX_DIM // 2
TOPK_MAX = 256
N_GROUPS = 4
EXP_PER_GROUP = 4
N_EXPERTS = N_GROUPS * EXP_PER_GROUP
D_EXPERT = 256
LANES = 128
INT_MIN = -2 ** 31
INT16_MIN, INT16_MAX = -2 ** 15, 2 ** 15 - 1
HALF_BITS = 16
PACK = 16
LOG2E = 1.4426950408889634
VT_ROWS = HEAD_DIM + PACK
DSA_KEY_TILE = 512
SCORE_ROWS = 128
POOL_ROWS = 256
SURE_PASSES = 8

U_K = 4 * WIDTH
U_V = U_K + WIDTH
U_QLAT = U_V + WIDTH
U_MISC = U_QLAT + Q_LORA
U_COLS = U_MISC + LANES

VMEM_LIMIT = 56 * 1024 * 1024


def _rms(x, gain):
    return x * lax.rsqrt(jnp.mean(x * x, axis=-1, keepdims=True) + EPS) * gain


def _dot(a, b):
    return jnp.dot(a, b, preferred_element_type=F32)


def _rms_head(x, gain, width=HEAD_DIM):
    ones = (lax.broadcasted_iota(I32, (LANES, LANES), 0) < width).astype(BF16)
    ss = _dot((x * x).astype(BF16), ones)
    return x * lax.rsqrt(ss * (1.0 / width) + EPS) * gain


def _dot_nt(a, b):
    return lax.dot_general(a, b, (((1,), (1,)), ((), ())), preferred_element_type=F32)


def _params(*sem):
    return pltpu.CompilerParams(dimension_semantics=sem, vmem_limit_bytes=VMEM_LIMIT)


def _in_proj_kernel(x_ref, g_ref, w_ref, u_ref, *, col_chunk):
    hb = _rms(x_ref[...], g_ref[...]).astype(BF16)
    for c0 in range(0, U_COLS, col_chunk):
        u_ref[:, c0:c0 + col_chunk] = _dot(hb, w_ref[:, c0:c0 + col_chunk])


def _in_proj(x2, gain, w_bf, tm=1024):
    n, d = x2.shape
    return pl.pallas_call(
        functools.partial(_in_proj_kernel, col_chunk=1152),
        out_shape=jax.ShapeDtypeStruct((n, U_COLS), F32),
        grid=(n // tm,),
        in_specs=[pl.BlockSpec((tm, d), lambda i: (i, 0)),
                  pl.BlockSpec((1, d), lambda i: (0, 0)),
                  pl.BlockSpec((d, U_COLS), lambda i: (0, 0), pipeline_mode=pl.Buffered(1))],
        out_specs=pl.BlockSpec((tm, U_COLS), lambda i: (i, 0)),
        compiler_params=_params("parallel"),
        name="in_proj",
    )(x2, gain, w_bf)


def _split3(x):
    hi = x.astype(BF16)
    r1 = x - hi.astype(F32)
    mid = r1.astype(BF16)
    lo = (r1 - mid.astype(F32)).astype(BF16)
    return hi, mid, lo


def _hgrn_kernel(q_ref, f_ref, i_ref, g_ref, lbraw_ref, gn_ref, e_ref, o_ref, st_ref,
                 *, layer, n_chunks, unroll):
    @pl.when(pl.program_id(2) == 0)
    def _():
        st_ref[...] = jnp.zeros_like(st_ref)

    lr = lbraw_ref[...]
    ex = jnp.exp(lr - jnp.max(lr, axis=0, keepdims=True))
    sm = ex / jnp.sum(ex, axis=0, keepdims=True)
    lb = jnp.zeros((1, HEAD_DIM), F32)
    for r in range(1, layer + 1):
        lb = lb + sm[r:r + 1, :]
    log_lb = jnp.log(lb)
    log_1mlb = jnp.log(1.0 - lb)

    row = lax.broadcasted_iota(I32, (CHUNK, HEAD_DIM), 0)
    col = lax.broadcasted_iota(I32, (CHUNK, HEAD_DIM), 1)
    row_l = row & (SUB - 1)
    row_b = row >> SUB_SHIFT
    col_b = col >> SUB_SHIFT
    tri = (lax.broadcasted_iota(I32, (CHUNK, CHUNK), 0)
           >= lax.broadcasted_iota(I32, (CHUNK, CHUNK), 1)).astype(BF16)
    n_sub = CHUNK // SUB
    zeros_c = jnp.zeros((CHUNK, HEAD_DIM), F32)

    def sub_bcast(t, s):
        t4 = t.reshape(n_sub, SUB, HEAD_DIM)
        return jnp.broadcast_to(t4[:, s:s + 1, :], (n_sub, SUB, HEAD_DIM)).reshape(CHUNK, HEAD_DIM)

    def gates(z):
        ls = jnp.minimum(z, 0.0) - jnp.log(1.0 + jnp.exp(-jnp.abs(z)))
        if layer == 0:
            return ls, (ls - z) * LOG2E
        cc = log_1mlb + ls
        log_f = jnp.maximum(log_lb, cc) + jnp.log(1.0 + jnp.exp(-jnp.abs(log_lb - cc)))
        return log_f, (cc - z) * LOG2E

    def diag_terms(q, b2, c2):
        pieces = []
        for s in range(SUB):
            d = jnp.where(row_l >= s, b2 - sub_bcast(c2, s), -jnp.inf)
            pieces.append((q * jnp.exp2(d)).astype(BF16))
        return jnp.concatenate(pieces, axis=1)

    def below_keys(b2, c2):
        kts = []
        for i in range(1, n_sub):
            r_i = b2[i * SUB:i * SUB + 1, :]
            kts.append(jnp.exp2(jnp.where(row < i * SUB, r_i - c2, -jnp.inf)))
            kts.append(zeros_c)
        return jnp.concatenate(kts, axis=0).astype(BF16)

    def below_scores(r):
        a = jnp.zeros((CHUNK, HEAD_DIM), F32)
        for i in range(1, n_sub):
            a = a + jnp.where(row_b == i, r[:, (i - 1) * LANES:i * LANES], 0.0)
        return a

    def chunks(it, carry):
        cs = range(unroll)
        rows = [pl.ds(pl.multiple_of((it * unroll + c) * CHUNK, CHUNK), CHUNK) for c in cs]
        qr = [q_ref[r, :] for r in rows]
        v = [i_ref[r, :] for r in rows]
        q = [x * jax.nn.sigmoid(x) for x in qr]
        gt = [gates(f_ref[r, :]) for r in rows]
        b2 = []
        for c in cs:
            hi, mid, lo = _split3(gt[c][0])
            b2.append((_dot(tri, hi) + _dot(tri, mid) + _dot(tri, lo)) * LOG2E)
        c2 = [b2[c] - gt[c][1] for c in cs]
        b_last = [x[CHUNK - 1:CHUNK, :] for x in b2]
        a2 = [_dot(diag_terms(q[c], b2[c], c2[c]), e_ref[...]) for c in cs]
        rr = [_dot_nt((q[c] * jnp.exp2(b2[c] - sub_bcast(b2[c], 0))).astype(BF16), below_keys(b2[c], c2[c]))
              for c in cs]
        upd = [_dot(v[c].T.astype(BF16), jnp.exp2(b_last[c] - c2[c]).astype(BF16)) for c in cs]
        qe = [(q[c] * jnp.exp2(b2[c])).astype(BF16) for c in cs]
        st = st_ref[...]
        o_inter = []
        for c in cs:
            o_inter.append(_dot_nt(qe[c], st.astype(BF16)))
            st = st * jnp.exp2(b_last[c]) + upd[c]
        st_ref[...] = st
        a = [jnp.where(col_b == row_b, a2[c], 0.0) + below_scores(rr[c]) for c in cs]
        o_intra = [_dot(a[c][:, :CHUNK].astype(BF16), v[c].astype(BF16)) for c in cs]
        for c in cs:
            g = g_ref[rows[c], :]
            o_ref[rows[c], :] = _rms(o_inter[c] + o_intra[c], gn_ref[...]) * (g * jax.nn.sigmoid(g))
        return carry

    lax.fori_loop(0, n_chunks // unroll, chunks, 0)


def _hgrn_selector():
    e = np.zeros((SUB * HEAD_DIM, LANES), np.float32)
    s_of_row = np.arange(SUB * HEAD_DIM) // HEAD_DIM
    cols = np.arange(LANES)
    e[:, :] = ((cols[None, :] % SUB) == s_of_row[:, None]) & (cols[None, :] < CHUNK)
    return jnp.asarray(e, BF16)


def _hgrn(u, lb_raw, gn, layer, bsz, seq, tb=1024):
    n = bsz * seq
    nb = seq // tb
    blk = lambda k: pl.BlockSpec((tb, HEAD_DIM), lambda b, h, j, k=k: (b * nb + j, h + N_HEADS * k))
    n_layers = lb_raw.shape[0]
    return pl.pallas_call(
        functools.partial(_hgrn_kernel, layer=layer, n_chunks=tb // CHUNK, unroll=16),
        out_shape=jax.ShapeDtypeStruct((n, WIDTH), F32),
        grid=(bsz, N_HEADS, nb),
        in_specs=[blk(0), blk(1), blk(2), blk(3),
                  pl.BlockSpec((n_layers, HEAD_DIM), lambda b, h, j: (0, h)),
                  pl.BlockSpec((1, HEAD_DIM), lambda b, h, j: (0, 0)),
                  pl.BlockSpec((SUB * HEAD_DIM, LANES), lambda b, h, j: (0, 0))],
        out_specs=pl.BlockSpec((tb, HEAD_DIM), lambda b, h, j: (b * nb + j, h)),
        scratch_shapes=[pltpu.VMEM((HEAD_DIM, HEAD_DIM), F32)],
        compiler_params=_params("parallel", "parallel", "arbitrary"),
        name="hgrn",
    )(u, u, u, u, lb_raw, gn, _hgrn_selector())


def _rope128(x, cos, sin_signed):
    return x * cos + pltpu.roll(x, HEAD_DIM // 2, 1) * sin_signed


def _rope64(x, cos, sin_a, sin_b):
    return x * cos + pltpu.roll(x, LANES - ROT_I, 1) * sin_a + pltpu.roll(x, ROT_I, 1) * sin_b


def _dsa_prep_kernel(k_ref, v_ref, ql_ref, misc_ref, ta_ref, tb_ref,
                     wqb_ref, wqi_ref, gql_ref, gqn_ref, gkn_ref, gki_ref,
                     q_out, k_out, vt_out, qi_out, ki_out, wt_out):
    ta, tb = ta_ref[...], tb_ref[...]
    lane = lax.broadcasted_iota(I32, ta.shape, 1)
    ta_swapped = pltpu.roll(ta, HEAD_DIM // 2, 1)
    cm = jnp.where(lane < HEAD_DIM // 2, ta, ta_swapped)
    sm = jnp.where(lane < HEAD_DIM // 2, -ta_swapped, ta)
    quarter = lane >> (ROT_I.bit_length() - 1)
    r32, r64, r96 = (pltpu.roll(tb, sh * ROT_I, 1) for sh in (1, 2, 3))
    ci = jnp.where(quarter == 0, tb, jnp.where(quarter == 1, r32, jnp.where(quarter == 2, r64, r96)))
    si = jnp.where(quarter == 0, r96, jnp.where(quarter == 1, tb, jnp.where(quarter == 2, r32, r64)))
    sia = jnp.where((lane & ROT_I) == 0, -si, 0.0)
    sib = jnp.where((lane & ROT_I) != 0, si, 0.0)
    cb = _rms(ql_ref[...], gql_ref[...]).astype(BF16)
    qf = _dot(cb, wqb_ref[...])
    scale = HEAD_DIM ** -0.5
    for h in range(N_HEADS):
        sl = slice(h * HEAD_DIM, (h + 1) * HEAD_DIM)
        qh = _rope128(_rms_head(qf[:, sl], gqn_ref[...]), cm, sm)
        q_out[:, sl] = (qh * (scale * LOG2E)).astype(BF16)
        kh = _rope128(_rms_head(k_ref[:, sl], gkn_ref[...]), cm, sm)
        k_out[:, sl] = kh.astype(BF16)
    vt = v_ref[...].T.astype(BF16)
    ones = jnp.ones((PACK, vt.shape[1]), BF16)
    for h in range(N_HEADS):
        vt_out[0, h * VT_ROWS:h * VT_ROWS + HEAD_DIM, :] = vt[h * HEAD_DIM:(h + 1) * HEAD_DIM, :]
        vt_out[0, h * VT_ROWS + HEAD_DIM:(h + 1) * VT_ROWS, :] = ones
    qi = _dot(cb, wqi_ref[...])
    for p in range(IDX_HEADS * IDX_DIM // LANES):
        sl = slice(p * LANES, (p + 1) * LANES)
        qi_out[:, sl] = _rope64(qi[:, sl], ci, sia, sib).astype(BF16)
    misc = misc_ref[...]
    kn = _rope64(_rms_head(misc, gki_ref[...], IDX_DIM), ci, sia, sib)
    ki_out[...] = (kn + pltpu.roll(kn, IDX_DIM, 1)).astype(BF16)
    wt = (misc * (IDX_HEADS ** -0.5 * IDX_DIM ** -0.5)).T
    wt_out[...] = wt[IDX_DIM:IDX_DIM + IDX_HEADS, :]


def _dsa_prep(u, tabs, wqb, wqi, gql, gqn, gkn, gki, tm=DSA_KEY_TILE):
    n = u.shape[0]
    nt = n // tm
    ublk = lambda width, col: pl.BlockSpec((tm, width), lambda i: (i, col // width))
    tab = pl.BlockSpec((tm, LANES), lambda i: (i, 0))
    full = lambda a: pl.BlockSpec(a.shape, lambda i: (0,) * a.ndim)
    return pl.pallas_call(
        _dsa_prep_kernel,
        out_shape=(jax.ShapeDtypeStruct((n, WIDTH), BF16),
                   jax.ShapeDtypeStruct((n, WIDTH), BF16),
                   jax.ShapeDtypeStruct((nt, N_HEADS * VT_ROWS, tm), BF16),
                   jax.ShapeDtypeStruct((n, IDX_HEADS * IDX_DIM), BF16),
                   jax.ShapeDtypeStruct((n, LANES), BF16),
                   jax.ShapeDtypeStruct((IDX_HEADS, n), F32)),
        grid=(nt,),
        in_specs=[ublk(WIDTH, U_K), ublk(WIDTH, U_V), ublk(Q_LORA, U_QLAT), ublk(LANES, U_MISC),
                  tab, tab,
                  full(wqb), full(wqi), full(gql), full(gqn), full(gkn), full(gki)],
        out_specs=(pl.BlockSpec((tm, WIDTH), lambda i: (i, 0)),
                   pl.BlockSpec((tm, WIDTH), lambda i: (i, 0)),
                   pl.BlockSpec((1, N_HEADS * VT_ROWS, tm), lambda i: (i, 0, 0)),
                   pl.BlockSpec((tm, IDX_HEADS * IDX_DIM), lambda i: (i, 0)),
                   pl.BlockSpec((tm, LANES), lambda i: (i, 0)),
                   pl.BlockSpec((IDX_HEADS, tm), lambda i: (0, i))),
        compiler_params=_params("parallel"),
        name="dsa_prep",
    )(u, u, u, u, *tabs, wqb, wqi, gql, gqn, gkn, gki)


def _sortable(x):
    bits = lax.bitcast_convert_type(x, I32)
    return bits ^ ((bits >> 31) & 0x7FFFFFFF)


def _dsa_attn_kernel(qi_ref, wt_ref, qin_ref, wtn_ref, q_ref, ki_ref, k_ref, vt_ref, o_ref,
                     qm_ref, hi_ref, lo_ref, kb_ref, pk_ref, lt_ref, acc0, acc1, acc2, acc3,
                     *, tq, kt, top_k, n_blocks):
    acc_refs = (acc0, acc1, acc2, acc3)
    blk = pl.program_id(1)
    cur = blk & 1
    nxt = 1 - cur
    q0 = blk * tq
    n_tiles = (q0 + tq + kt - 1) // kt
    pooled_per_tile = kt // POOL_ROWS * 2 * PACK
    sub = SCORE_ROWS
    key_iota = lax.broadcasted_iota(I32, (sub, tq), 0)
    lane_q = lax.broadcasted_iota(I32, (tq, LANES), 1)

    def tile_rows(jt):
        return pl.ds(pl.multiple_of(jt * kt, kt), kt)

    def limit_of(q_start):
        qpos = q_start + lax.broadcasted_iota(I32, (1, tq), 1)
        return ((qpos >> CHUNK_SHIFT) + 1) << CHUNK_SHIFT

    def build_qm(src_ref):
        for h in range(IDX_HEADS):
            src = src_ref[:, (h // 2) * LANES:(h // 2 + 1) * LANES].astype(F32)
            keep = (lane_q < IDX_DIM) if h % 2 == 0 else (lane_q >= IDX_DIM)
            qm_ref[h * tq:(h + 1) * tq, :] = jnp.where(keep, src, 0.0).astype(BF16)

    def score_tile(jt, slot, limit, w_ref):
        for part in range(kt // sub):
            base = jt * kt + part * sub
            rows = pl.ds(pl.multiple_of(base, sub), sub)
            ki = ki_ref[rows, :]
            acc = jnp.zeros((sub, tq), F32)
            for h in range(IDX_HEADS):
                x = _dot_nt(ki, qm_ref[h * tq:(h + 1) * tq, :])
                acc = acc + w_ref[h:h + 1, :] * jnp.maximum(x, 0.0)
            key = jnp.where(base + key_iota < limit, _sortable(acc), INT_MIN)
            hi_ref[slot, rows, :] = (key >> HALF_BITS).astype(I16)
            lo_ref[slot, rows, :] = ((key & (2 ** HALF_BITS - 1)) + INT16_MIN).astype(I16)

    @pl.when(blk == 0)
    def _():
        build_qm(qi_ref)
        limit0 = limit_of(0)

        def body(jt, carry):
            score_tile(jt, 0, limit0, wt_ref)
            return carry
        lax.fori_loop(0, n_tiles, body, 0)

    one, zero = jnp.ones((), BF16), jnp.zeros((), BF16)
    neg_inf = jnp.full((), -jnp.inf, BF16)
    n_rows = (n_tiles * kt).astype(F32)
    hi_rows = lambda rows: hi_ref[cur, rows, :]
    lo_rows = lambda rows: lo_ref[cur, rows, :]
    kb_rows = lambda rows: kb_ref[rows, :]

    def count16(get, trial, strict, trips=None):
        t16 = jnp.broadcast_to(trial.astype(I16), (PACK, tq))

        def body(jt, c):
            x = get(tile_rows(jt))
            parts = [jnp.zeros((PACK, tq), BF16) for _ in range(4)]
            for r in range(kt // PACK):
                xr = x[r * PACK:(r + 1) * PACK, :]
                parts[r % 4] = parts[r % 4] + jnp.where((xr > t16) if strict else (xr >= t16), one, zero)
            return c + ((parts[0] + parts[1]) + (parts[2] + parts[3])).astype(F32)
        c = lax.fori_loop(0, n_tiles if trips is None else trips, body, jnp.zeros((PACK, tq), F32))
        return jnp.sum(c, axis=0, keepdims=True)

    def bisect16(get, want, n_all, early_exit, trips=None):
        def step(it, carry):
            lo, n_lo, n_up = carry
            trial = lo + lax.shift_left(jnp.int32(1), HALF_BITS - 1 - it)
            n = count16(get, trial, False, trips)
            ok = n >= want
            return jnp.where(ok, trial, lo), jnp.where(ok, n, n_lo), jnp.where(ok, n_up, n)

        init = (jnp.full((1, tq), INT16_MIN, I32), n_all, jnp.zeros((1, tq), F32))
        if not early_exit:
            return lax.fori_loop(0, HALF_BITS, step, init)

        def unresolved(carry):
            it, _, n_lo, _ = carry
            return (it < HALF_BITS) & (jnp.max(jnp.where(n_lo != want, 1.0, 0.0)) > 0.0)

        def two_steps(carry):
            return (carry[0] + 2,) + step(carry[0] + 1, step(carry[0], carry[1:]))

        return lax.while_loop(unresolved, two_steps, (jnp.int32(SURE_PASSES),) + lax.fori_loop(0, SURE_PASSES, step, init))[1:]

    tau_hi, n_hi_ge, n_hi_gt = bisect16(hi_rows, float(top_k), jnp.full((1, tq), n_rows, F32), False)
    want_lo = top_k - n_hi_gt
    tau_hi16 = jnp.broadcast_to(tau_hi.astype(I16), (kt, tq))

    min16 = jnp.full((PACK, tq), INT16_MIN, I16)

    def bucket_tile(jt, carry):
        rows = tile_rows(jt)
        kb = jnp.where(hi_rows(rows) == tau_hi16, lo_rows(rows), jnp.full((), INT16_MIN, I16))
        kb_ref[rows, :] = kb
        for g in range(kt // POOL_ROWS):
            top1, top2 = min16, min16
            for r in range(POOL_ROWS // PACK):
                x = kb[g * POOL_ROWS + r * PACK:g * POOL_ROWS + (r + 1) * PACK, :]
                below = x < top1
                runner = jnp.where(below, x, top1)
                top2 = jnp.where(runner > top2, runner, top2)
                top1 = jnp.where(below, top1, x)
            base = pl.multiple_of(jt * pooled_per_tile + g * 2 * PACK, 2 * PACK)
            pk_ref[pl.ds(base, PACK), :] = top1
            pk_ref[pl.ds(base + PACK, PACK), :] = top2
        return carry

    lax.fori_loop(0, n_tiles, bucket_tile, 0)
    pooled_trips = (n_tiles * pooled_per_tile + kt - 1) // kt

    def pad_pooled(jt, carry):
        pk_ref[pl.ds(pl.multiple_of(jt * pooled_per_tile, pooled_per_tile), pooled_per_tile), :] = jnp.full(
            (pooled_per_tile, tq), INT16_MIN, I16)
        return carry

    lax.fori_loop(n_tiles, pooled_trips * (kt // pooled_per_tile), pad_pooled, 0)
    n_bucket = n_hi_ge - n_hi_gt
    pk_rows = lambda rows: pk_ref[rows, :]
    tau_p, _, _ = bisect16(pk_rows, want_lo, n_bucket, True, pooled_trips)
    n_ge_p = jnp.where(tau_p == INT16_MIN, n_bucket, count16(kb_rows, tau_p, False))
    n_gt_p = count16(kb_rows, tau_p, True)
    pooled_ok = (n_ge_p == want_lo) | ((n_ge_p > want_lo) & (n_gt_p < want_lo))

    def full_search():
        t, n_ge, _ = bisect16(kb_rows, want_lo, n_bucket, True)
        return t, n_ge, count16(kb_rows, t, True)

    tau_lo, n_kb_ge, n_kb_gt = lax.cond(
        jnp.min(jnp.where(pooled_ok, 1.0, 0.0)) > 0.0,
        lambda: (tau_p, n_ge_p, n_gt_p), full_search)
    live = (tau_hi > INT16_MIN) | (tau_lo > INT16_MIN)
    excess = jnp.max(jnp.where(live, n_kb_ge - want_lo, 0.0))

    @pl.when(excess > 0)
    def _():
        need = want_lo - n_kb_gt
        live_f = jnp.where(live, 1.0, 0.0)
        tri = (lax.broadcasted_iota(I32, (kt, kt), 0)
               >= lax.broadcasted_iota(I32, (kt, kt), 1)).astype(BF16)

        def body(jt, seen):
            rows = tile_rows(jt)
            h32 = hi_rows(rows).astype(I32)
            eq = jnp.where(h32 == tau_hi, jnp.where(lo_rows(rows).astype(I32) == tau_lo, 1.0, 0.0), 0.0)
            rank = _dot(tri, eq.astype(BF16)) + seen
            dropped = eq * (1.0 - jnp.where(rank <= need, live_f, 0.0))
            hi_ref[cur, rows, :] = jnp.where(dropped > 0.0, INT16_MIN, h32).astype(I16)
            return rank[kt - 1:kt, :]
        lax.fori_loop(0, n_tiles, body, jnp.zeros((1, tq), F32))

    tau_lo16 = jnp.broadcast_to(jnp.where(live, tau_lo, INT16_MAX).astype(I16), (kt, tq))

    def bias_rows(rows):
        h16 = hi_rows(rows)
        in_bucket = jnp.where(lo_rows(rows) >= tau_lo16, zero, neg_inf)
        return jnp.where(h16 > tau_hi16, zero, jnp.where(h16 == tau_hi16, in_bucket, neg_inf)).astype(F32)

    for a in acc_refs:
        a[...] = jnp.zeros_like(a)
    heads = [slice(h * HEAD_DIM, (h + 1) * HEAD_DIM) for h in range(N_HEADS)]

    limit_next = limit_of(q0 + tq)

    def attn_tiles(tiles, ms, score_next):
        for i, jt in enumerate(tiles):
            rows = tile_rows(jt)
            bias = bias_rows(rows)
            for h in range(N_HEADS):
                lt_ref[i, h] = _dot_nt(k_ref[rows, heads[h]], q_ref[:, heads[h]]) + bias
        ms = list(ms)
        for i, jt in enumerate(tiles):
            if score_next:
                score_tile(jt, nxt, limit_next, wtn_ref)
            alphas, ps = [], []
            for h in range(N_HEADS):
                m_new = jnp.maximum(ms[h], jnp.max(lt_ref[i, h], axis=0, keepdims=True))
                m_safe = jnp.where(m_new == -jnp.inf, 0.0, m_new)
                ps.append(jnp.exp2(lt_ref[i, h] - m_safe).astype(BF16))
                alphas.append(jnp.exp2(ms[h] - m_safe))
                ms[h] = m_new
            for h in range(N_HEADS):
                acc_refs[h][...] = (acc_refs[h][...] * alphas[h]
                                    + _dot(vt_ref[jt, h * VT_ROWS:(h + 1) * VT_ROWS, :], ps[h]))
        return tuple(ms)

    def attention(score_next):
        ms = tuple(jnp.full((1, tq), -jnp.inf, F32) for _ in range(N_HEADS))
        ms = lax.fori_loop(0, n_tiles // 2, lambda jp, m: attn_tiles((2 * jp, 2 * jp + 1), m, score_next), ms)
        lax.cond((n_tiles & 1) == 1, lambda m: attn_tiles((n_tiles - 1,), m, score_next), lambda m: m, ms)

    @pl.when(blk < n_blocks - 1)
    def _():
        build_qm(qin_ref)
        attention(True)

        @pl.when((q0 + 2 * tq + kt - 1) // kt > n_tiles)
        def _():
            score_tile(n_tiles, nxt, limit_next, wtn_ref)

    @pl.when(blk == n_blocks - 1)
    def _():
        attention(False)

    for h in range(N_HEADS):
        acc = acc_refs[h][...]
        out_t = acc[:HEAD_DIM, :] * (1.0 / acc[HEAD_DIM:HEAD_DIM + 1, :])
        o_ref[:, heads[h]] = out_t.T


def _dsa_attn(q, k, vt, qi, ki, wt, bsz, seq, tq=256, kt=DSA_KEY_TILE):
    n = bsz * seq
    nq = seq // tq
    top_k = min(TOPK_MAX, seq // 4)
    once = dict(pipeline_mode=pl.Buffered(1))
    nxt_blk = lambda b, i: b * nq + jnp.minimum(i + 1, nq - 1)
    return pl.pallas_call(
        functools.partial(_dsa_attn_kernel, tq=tq, kt=kt, top_k=top_k, n_blocks=nq),
        out_shape=jax.ShapeDtypeStruct((n, WIDTH), F32),
        grid=(bsz, nq),
        in_specs=[pl.BlockSpec((tq, IDX_HEADS * IDX_DIM), lambda b, i: (b * nq + i, 0)),
                  pl.BlockSpec((IDX_HEADS, tq), lambda b, i: (0, b * nq + i)),
                  pl.BlockSpec((tq, IDX_HEADS * IDX_DIM), lambda b, i: (nxt_blk(b, i), 0)),
                  pl.BlockSpec((IDX_HEADS, tq), lambda b, i: (0, nxt_blk(b, i))),
                  pl.BlockSpec((tq, WIDTH), lambda b, i: (b * nq + i, 0)),
                  pl.BlockSpec((seq, LANES), lambda b, i: (b, 0), **once),
                  pl.BlockSpec((seq, WIDTH), lambda b, i: (b, 0), **once),
                  pl.BlockSpec((seq // kt, N_HEADS * VT_ROWS, kt), lambda b, i: (b, 0, 0), **once)],
        out_specs=pl.BlockSpec((tq, WIDTH), lambda b, i: (b * nq + i, 0)),
        scratch_shapes=[pltpu.VMEM((IDX_HEADS * tq, LANES), BF16),
                        pltpu.VMEM((2, seq, tq), I16), pltpu.VMEM((2, seq, tq), I16), pltpu.VMEM((seq, tq), I16),
                        pltpu.VMEM((pl.cdiv(seq // POOL_ROWS * 2 * PACK, kt) * kt, tq), I16),
                        pltpu.VMEM((2, N_HEADS, kt, tq), F32)]
                       + [pltpu.VMEM((VT_ROWS, tq), F32) for _ in range(N_HEADS)],
        compiler_params=_params("arbitrary", "arbitrary"),
        name="dsa_attn",
    )(qi, wt, qi, wt, q, ki, k, vt)


def _mem_kv_kernel(mem_ref, g_ref, wk_ref, wv_ref, gk_ref, k_out, v_out):
    mb = _rms(mem_ref[...], g_ref[...]).astype(BF16)
    kf = _dot(mb, wk_ref[...])
    for h in range(N_HEADS):
        sl = slice(h * HEAD_DIM, (h + 1) * HEAD_DIM)
        k_out[:, sl] = _rms_head(kf[:, sl], gk_ref[...]).astype(BF16)
    v_out[...] = _dot(mb, wv_ref[...]).astype(BF16)


def _mem_kv(mem2, gain, wk, wv, gk, n_mem):
    n, d = mem2.shape
    full = lambda a: pl.BlockSpec(a.shape, lambda b: (0,) * a.ndim)
    return pl.pallas_call(
        _mem_kv_kernel,
        out_shape=(jax.ShapeDtypeStruct((n, WIDTH), BF16), jax.ShapeDtypeStruct((n, WIDTH), BF16)),
        grid=(n // n_mem,),
        in_specs=[pl.BlockSpec((n_mem, d), lambda b: (b, 0)), full(gain), full(wk), full(wv), full(gk)],
        out_specs=(pl.BlockSpec((n_mem, WIDTH), lambda b: (b, 0)),
                   pl.BlockSpec((n_mem, WIDTH), lambda b: (b, 0))),
        compiler_params=_params("parallel"),
        name="mem_kv",
    )(mem2, gain, wk, wv, gk)


def _xattn_kernel(x_ref, oa_ref, ob_ref, wout_ref, g_ref, wq_ref, gq_ref, km_ref, vm_ref, wo_ref, o_ref):
    x1 = (x_ref[...] + _dot(oa_ref[...].astype(BF16), wout_ref[:WIDTH, :])
          + _dot(ob_ref[...].astype(BF16), wout_ref[WIDTH:, :]))
    hb = _rms(x1, g_ref[...]).astype(BF16)
    qf = _dot(hb, wq_ref[...])
    scale = HEAD_DIM ** -0.5
    outs = []
    for h in range(N_HEADS):
        sl = slice(h * HEAD_DIM, (h + 1) * HEAD_DIM)
        qh = (_rms(qf[:, sl], gq_ref[...]) * scale).astype(BF16)
        logits = _dot_nt(qh, km_ref[:, sl])
        p = jnp.exp(logits - jnp.max(logits, axis=-1, keepdims=True))
        oh = _dot(p.astype(BF16), vm_ref[:, sl])
        outs.append((oh * (1.0 / jnp.sum(p, axis=-1, keepdims=True))).astype(BF16))
    o_ref[...] = x1 + _dot(jnp.concatenate(outs, axis=1), wo_ref[...])


def _xattn(x2, oa, ob, wout, gain, wq, gq, km, vm, wo, seq, n_mem, tm=1024):
    n, d = x2.shape
    per_b = seq // tm
    row = lambda w: pl.BlockSpec((tm, w), lambda i: (i, 0))
    full = lambda a: pl.BlockSpec(a.shape, lambda i: (0,) * a.ndim)
    memb = pl.BlockSpec((n_mem, WIDTH), lambda i: (i // per_b, 0))
    return pl.pallas_call(
        _xattn_kernel,
        out_shape=jax.ShapeDtypeStruct((n, d), F32),
        grid=(n // tm,),
        in_specs=[row(d), row(WIDTH), row(WIDTH), full(wout), full(gain), full(wq), full(gq),
                  memb, memb, full(wo)],
        out_specs=row(d),
        compiler_params=_params("parallel"),
        name="xattn",
    )(x2, oa, ob, wout, gain, wq, gq, km, vm, wo)


def _moe_kernel(x_ref, g_ref, wr_hi_ref, wr_lo_ref, br_ref, wgu_ref, wdn_ref, o_ref, hs_ref, gs_ref, ys_ref,
                *, tm, sub):
    lane = lax.broadcasted_iota(I32, (tm, LANES), 1)
    lane_f = lane.astype(F32)
    x = x_ref[...]
    h = _rms(x, g_ref[...])
    h_hi = h.astype(BF16)
    h_lo = (h - h_hi.astype(F32)).astype(BF16)
    lg = (_dot(h_hi, wr_hi_ref[...]) + _dot(h_lo, wr_hi_ref[...]) + _dot(h_hi, wr_lo_ref[...])
          + br_ref[...])
    first = lambda cond: jnp.min(jnp.where(cond, lane_f, 1e9), axis=-1, keepdims=True)
    gl = jnp.where(lane < N_GROUPS, lg, -jnp.inf)
    gmax = jnp.max(gl, axis=-1, keepdims=True)
    gsel = first(gl == gmax)
    g_w = 1.0 / jnp.sum(jnp.exp(gl - gmax), axis=-1, keepdims=True)
    grp_of_lane = ((lane - N_GROUPS) >> (EXP_PER_GROUP.bit_length() - 1)).astype(F32)
    in_grp = (lane >= N_GROUPS) & (lane < N_GROUPS + N_EXPERTS) & (grp_of_lane == gsel)
    el = jnp.where(in_grp, lg, -jnp.inf)
    v1 = jnp.max(el, axis=-1, keepdims=True)
    i1 = first(el == v1)
    el2 = jnp.where(lane_f == i1, -jnp.inf, el)
    v2 = jnp.max(el2, axis=-1, keepdims=True)
    i2 = first(el2 == v2)
    r = jnp.exp(v2 - v1)
    w1 = 1.0 / (1.0 + r)
    gate = jnp.where(lane_f == i1, w1 * g_w, jnp.where(lane_f == i2, r * w1 * g_w, 0.0))

    onehot = jnp.where(lane_f == gsel, 1.0, 0.0)
    t_row = lax.broadcasted_iota(I32, (tm, tm), 0)
    t_col = lax.broadcasted_iota(I32, (tm, tm), 1)
    before = _dot(jnp.where(t_row > t_col, 1.0, 0.0).astype(BF16), onehot.astype(BF16))
    counts = jnp.sum(onehot, axis=0, keepdims=True)
    lane1 = lax.broadcasted_iota(I32, (1, LANES), 1)
    ends = []
    run = jnp.zeros((1, 1), F32)
    for g in range(N_GROUPS - 1):
        run = run + jnp.sum(jnp.where(lane1 == g, counts, 0.0), axis=-1, keepdims=True)
        ends.append(run)
    start_of = sum(jnp.where(lane1 == g + 1, ends[g], 0.0) for g in range(N_GROUPS - 1))
    pos = jnp.sum(onehot * (before + start_of), axis=-1, keepdims=True)
    to_sorted_t = jnp.where(pos == t_col.astype(F32), 1.0, 0.0)
    to_sorted = to_sorted_t.T.astype(BF16)
    hs_ref[...] = _dot(to_sorted, h_hi).astype(BF16)
    g_hi, g_mid, g_lo = _split3(gate)
    gs_ref[...] = _dot(to_sorted, g_hi) + _dot(to_sorted, g_mid) + _dot(to_sorted, g_lo)

    bounds = [e_[0, 0].astype(I32) for e_ in ends]
    lane_s = lax.broadcasted_iota(I32, (sub, LANES), 1)

    def slab(j, carry):
        r0 = j * sub
        rows = pl.ds(pl.multiple_of(r0, sub), sub)
        g_first = sum((b <= r0).astype(I32) for b in bounds)
        g_last = sum((b <= r0 + sub - 1).astype(I32) for b in bounds)
        hs = hs_ref[rows, :]
        gs = gs_ref[rows, :]
        ys_ref[rows, :] = jnp.zeros((sub, ys_ref.shape[1]), F32)

        def group(g, c):
            gu = _dot(hs, wgu_ref[g])
            acts = []
            for e in range(EXP_PER_GROUP):
                ge = jnp.sum(jnp.where(lane_s == g * EXP_PER_GROUP + e + N_GROUPS, gs, 0.0), axis=-1, keepdims=True)
                up = gu[:, 2 * e * D_EXPERT:(2 * e + 1) * D_EXPERT]
                acts.append((up * jax.nn.sigmoid(up) * gu[:, (2 * e + 1) * D_EXPERT:(2 * e + 2) * D_EXPERT] * ge).astype(BF16))
            ys_ref[rows, :] += _dot(jnp.concatenate(acts, axis=1), wdn_ref[g])
            return c

        lax.fori_loop(g_first, g_last + 1, group, 0)
        return carry

    lax.fori_loop(0, tm // sub, slab, 0)
    o_ref[...] = x + _dot(to_sorted_t.astype(BF16), ys_ref[...].astype(BF16))


def _moe(x2, gain, wr_hi, wr_lo, br, wgu, wdn, tm=512, sub=128):
    n, d = x2.shape
    full = lambda a: pl.BlockSpec(a.shape, lambda i: (0,) * a.ndim)
    once = lambda a: pl.BlockSpec(a.shape, lambda i: (0,) * a.ndim, pipeline_mode=pl.Buffered(1))
    return pl.pallas_call(
        functools.partial(_moe_kernel, tm=tm, sub=sub),
        out_shape=jax.ShapeDtypeStruct((n, d), F32),
        grid=(n // tm,),
        in_specs=[pl.BlockSpec((tm, d), lambda i: (i, 0)), full(gain), full(wr_hi), full(wr_lo), full(br),
                  once(wgu), once(wdn)],
        out_specs=pl.BlockSpec((tm, d), lambda i: (i, 0)),
        scratch_shapes=[pltpu.VMEM((tm, d), BF16), pltpu.VMEM((tm, LANES), F32), pltpu.VMEM((tm, d), F32)],
        compiler_params=_params("parallel"),
        name="moe",
    )(x2, gain, wr_hi, wr_lo, br, wgu, wdn)


def _rope_tables(positions):
    pos = positions.reshape(-1).astype(F32)[:, None]

    def cs(dim):
        inv = ROPE_THETA ** (-jnp.arange(0, dim, 2, dtype=F32) / dim)
        ang = pos * inv
        return jnp.cos(ang), jnp.sin(ang)

    c, s = cs(HEAD_DIM)
    ci, si = cs(IDX_DIM)
    return jnp.concatenate([c, s], 1), jnp.concatenate([ci, si, jnp.zeros_like(ci), jnp.zeros_like(ci)], 1)


def _reorder_in_weight(w):
    a = w[:, :4 * WIDTH]
    o = 4 * WIDTH
    qlat = w[:, o:o + Q_LORA]; o += Q_LORA
    k = w[:, o:o + WIDTH]; o += WIDTH
    v = w[:, o:o + WIDTH]; o += WIDTH
    misc = w[:, o:]
    pad = jnp.zeros((w.shape[0], LANES - misc.shape[1]), w.dtype)
    return jnp.concatenate([a, k, v, qlat, misc, pad], axis=1).astype(BF16)


def _row(v):
    return v.reshape(1, -1).astype(F32)


def kernel(x, mem, positions, norm_mix, w_in, lb_raw, a_gnorm, b_qlat_gain, b_wqb, b_wqidx, b_qnorm,
           b_knorm, b_kidx_norm, w_out, norm_x, norm_mem, x_wq, x_wk, x_wv, x_wo, x_qnorm, x_knorm,
           norm_ffn, w_rg, b_rg, w_re, b_re, w_gu, w_dn):
    bsz, seq, d = x.shape
    n_mem = mem.shape[1]
    depth = w_in.shape[0]
    assert d == D_MODEL and seq % (2 * DSA_KEY_TILE) == 0 and n_mem % LANES == 0, (x.shape, mem.shape)
    x2 = x.reshape(bsz * seq, d)
    mem2 = mem.reshape(bsz * n_mem, d)
    tabs = _rope_tables(positions)
    pad_lanes = lambda v: jnp.pad(v, ((0, 0), (0, LANES - v.shape[1])))
    for l in range(depth):
        u = _in_proj(x2, _row(norm_mix[l]), _reorder_in_weight(w_in[l]))
        o_a = _hgrn(u, lb_raw.astype(F32), _row(a_gnorm[l]), l, bsz, seq)
        q, k, vt, qi, ki, wt = _dsa_prep(
            u, tabs, b_wqb[l].astype(BF16), b_wqidx[l].astype(BF16), _row(b_qlat_gain[l]),
            _row(b_qnorm[l]), _row(b_knorm[l]), pad_lanes(_row(b_kidx_norm[l])))
        o_b = _dsa_attn(q, k, vt, qi, ki, wt, bsz, seq)
        km, vm = _mem_kv(mem2, _row(norm_mem[l]), x_wk[l].astype(BF16), x_wv[l].astype(BF16),
                         _row(x_knorm[l]), n_mem)
        x2 = _xattn(x2, o_a, o_b, w_out[l].astype(BF16), _row(norm_x[l]), x_wq[l].astype(BF16),
                    _row(x_qnorm[l]), km, vm, x_wo[l].astype(BF16), seq, n_mem)
        wr = pad_lanes(jnp.concatenate([w_rg[l], w_re[l].reshape(d, N_EXPERTS)], axis=1).astype(F32))
        wr_hi = wr.astype(BF16)
        wr_lo = (wr - wr_hi.astype(F32)).astype(BF16)
        br = pad_lanes(jnp.concatenate([b_rg[l], b_re[l].reshape(-1)]).reshape(1, -1).astype(F32))
        x2 = _moe(x2, _row(norm_ffn[l]), wr_hi, wr_lo, br,
                  w_gu[l].transpose(0, 2, 1, 3).reshape(N_GROUPS, d, EXP_PER_GROUP * 2 * D_EXPERT).astype(BF16),
                  w_dn[l].reshape(N_GROUPS, EXP_PER_GROUP * D_EXPERT, d).astype(BF16))
    return x2.reshape(bsz, seq, d)
```

```python
import functools

import numpy as np
import jax
import jax.numpy as jnp
from jax import lax
from jax.experimental import pallas as pl
from jax.experimental.pallas import tpu as pltpu

F32 = jnp.float32
BF16 = jnp.bfloat16
I32 = jnp.int32
I16 = jnp.int16

EPS = 1e-6
ROPE_THETA = 10000.0
CHUNK = 64
CHUNK_SHIFT = CHUNK.bit_length() - 1
SUB = 16
SUB_SHIFT = SUB.bit_length() - 1
N_HEADS = 4
HEAD_DIM = 128
WIDTH = N_HEADS * HEAD_DIM
Q_LORA = 256
D_MODEL = 1024
IDX_HEADS = 16
IDX_DIM = 64
ROT_I = IDX_DIM // 2
TOPK_MAX = 256
N_GROUPS = 4
EXP_PER_GROUP = 4
N_EXPERTS = N_GROUPS * EXP_PER_GROUP
D_EXPERT = 256
LANES = 128
INT_MIN = -2 ** 31
INT16_MIN, INT16_MAX = -2 ** 15, 2 ** 15 - 1
HALF_BITS = 16
PACK = 16
LOG2E = 1.4426950408889634
VT_ROWS = HEAD_DIM + PACK
DSA_KEY_TILE = 512
SCORE_ROWS = 128
POOL_ROWS = 256
SURE_PASSES = 8

U_K = 4 * WIDTH
U_V = U_K + WIDTH
U_QLAT = U_V + WIDTH
U_MISC = U_QLAT + Q_LORA
U_COLS = U_MISC + LANES

VMEM_LIMIT = 56 * 1024 * 1024


def _rms(x, gain):
    return x * lax.rsqrt(jnp.mean(x * x, axis=-1, keepdims=True) + EPS) * gain


def _dot(a, b):
    return jnp.dot(a, b, preferred_element_type=F32)


def _rms_head(x, gain, width=HEAD_DIM):
    ones = (lax.broadcasted_iota(I32, (LANES, LANES), 0) < width).astype(BF16)
    ss = _dot((x * x).astype(BF16), ones)
    return x * lax.rsqrt(ss * (1.0 / width) + EPS) * gain


def _dot_nt(a, b):
    return lax.dot_general(a, b, (((1,), (1,)), ((), ())), preferred_element_type=F32)


def _params(*sem):
    return pltpu.CompilerParams(dimension_semantics=sem, vmem_limit_bytes=VMEM_LIMIT)


def _in_proj_kernel(x_ref, g_ref, w_ref, u_ref, *, col_chunk):
    hb = _rms(x_ref[...], g_ref[...]).astype(BF16)
    for c0 in range(0, U_COLS, col_chunk):
        u_ref[:, c0:c0 + col_chunk] = _dot(hb, w_ref[:, c0:c0 + col_chunk])


def _in_proj(x2, gain, w_bf, tm=1024):
    n, d = x2.shape
    return pl.pallas_call(
        functools.partial(_in_proj_kernel, col_chunk=1152),
        out_shape=jax.ShapeDtypeStruct((n, U_COLS), F32),
        grid=(n // tm,),
        in_specs=[pl.BlockSpec((tm, d), lambda i: (i, 0)),
                  pl.BlockSpec((1, d), lambda i: (0, 0)),
                  pl.BlockSpec((d, U_COLS), lambda i: (0, 0), pipeline_mode=pl.Buffered(1))],
        out_specs=pl.BlockSpec((tm, U_COLS), lambda i: (i, 0)),
        compiler_params=_params("parallel"),
        name="in_proj",
    )(x2, gain, w_bf)


def _split3(x):
    hi = x.astype(BF16)
    r1 = x - hi.astype(F32)
    mid = r1.astype(BF16)
    lo = (r1 - mid.astype(F32)).astype(BF16)
    return hi, mid, lo


def _hgrn_kernel(q_ref, f_ref, i_ref, g_ref, lbraw_ref, gn_ref, e_ref, o_ref, st_ref,
                 *, layer, n_chunks, unroll):
    @pl.when(pl.program_id(2) == 0)
    def _():
        st_ref[...] = jnp.zeros_like(st_ref)

    lr = lbraw_ref[...]
    ex = jnp.exp(lr - jnp.max(lr, axis=0, keepdims=True))
    sm = ex / jnp.sum(ex, axis=0, keepdims=True)
    lb = jnp.zeros((1, HEAD_DIM), F32)
    for r in range(1, layer + 1):
        lb = lb + sm[r:r + 1, :]
    log_lb = jnp.log(lb)
    log_1mlb = jnp.log(1.0 - lb)

    row = lax.broadcasted_iota(I32, (CHUNK, HEAD_DIM), 0)
    col = lax.broadcasted_iota(I32, (CHUNK, HEAD_DIM), 1)
    row_l = row & (SUB - 1)
    row_b = row >> SUB_SHIFT
    col_b = col >> SUB_SHIFT
    tri = (lax.broadcasted_iota(I32, (CHUNK, CHUNK), 0)
           >= lax.broadcasted_iota(I32, (CHUNK, CHUNK), 1)).astype(BF16)
    n_sub = CHUNK // SUB
    zeros_c = jnp.zeros((CHUNK, HEAD_DIM), F32)

    def sub_bcast(t, s):
        t4 = t.reshape(n_sub, SUB, HEAD_DIM)
        return jnp.broadcast_to(t4[:, s:s + 1, :], (n_sub, SUB, HEAD_DIM)).reshape(CHUNK, HEAD_DIM)

    def gates(z):
        ls = jnp.minimum(z, 0.0) - jnp.log(1.0 + jnp.exp(-jnp.abs(z)))
        if layer == 0:
            return ls, (ls - z) * LOG2E
        cc = log_1mlb + ls
        log_f = jnp.maximum(log_lb, cc) + jnp.log(1.0 + jnp.exp(-jnp.abs(log_lb - cc)))
        return log_f, (cc - z) * LOG2E

    def diag_terms(q, b2, c2):
        pieces = []
        for s in range(SUB):
            d = jnp.where(row_l >= s, b2 - sub_bcast(c2, s), -jnp.inf)
            pieces.append((q * jnp.exp2(d)).astype(BF16))
        return jnp.concatenate(pieces, axis=1)

    def below_keys(b2, c2):
        kts = []
        for i in range(1, n_sub):
            r_i = b2[i * SUB:i * SUB + 1, :]
            kts.append(jnp.exp2(jnp.where(row < i * SUB, r_i - c2, -jnp.inf)))
            kts.append(zeros_c)
        return jnp.concatenate(kts, axis=0).astype(BF16)

    def below_scores(r):
        a = jnp.zeros((CHUNK, HEAD_DIM), F32)
        for i in range(1, n_sub):
            a = a + jnp.where(row_b == i, r[:, (i - 1) * LANES:i * LANES], 0.0)
        return a

    def chunks(it, carry):
        cs = range(unroll)
        rows = [pl.ds(pl.multiple_of((it * unroll + c) * CHUNK, CHUNK), CHUNK) for c in cs]
        qr = [q_ref[r, :] for r in rows]
        v = [i_ref[r, :] for r in rows]
        q = [x * jax.nn.sigmoid(x) for x in qr]
        gt = [gates(f_ref[r, :]) for r in rows]
        b2 = []
        for c in cs:
            hi, mid, lo = _split3(gt[c][0])
            b2.append((_dot(tri, hi) + _dot(tri, mid) + _dot(tri, lo)) * LOG2E)
        c2 = [b2[c] - gt[c][1] for c in cs]
        b_last = [x[CHUNK - 1:CHUNK, :] for x in b2]
        a2 = [_dot(diag_terms(q[c], b2[c], c2[c]), e_ref[...]) for c in cs]
        rr = [_dot_nt((q[c] * jnp.exp2(b2[c] - sub_bcast(b2[c], 0))).astype(BF16), below_keys(b2[c], c2[c]))
              for c in cs]
        upd = [_dot(v[c].T.astype(BF16), jnp.exp2(b_last[c] - c2[c]).astype(BF16)) for c in cs]
        qe = [(q[c] * jnp.exp2(b2[c])).astype(BF16) for c in cs]
        st = st_ref[...]
        o_inter = []
        for c in cs:
            o_inter.append(_dot_nt(qe[c], st.astype(BF16)))
            st = st * jnp.exp2(b_last[c]) + upd[c]
        st_ref[...] = st
        a = [jnp.where(col_b == row_b, a2[c], 0.0) + below_scores(rr[c]) for c in cs]
        o_intra = [_dot(a[c][:, :CHUNK].astype(BF16), v[c].astype(BF16)) for c in cs]
        for c in cs:
            g = g_ref[rows[c], :]
            o_ref[rows[c], :] = _rms(o_inter[c] + o_intra[c], gn_ref[...]) * (g * jax.nn.sigmoid(g))
        return carry

    lax.fori_loop(0, n_chunks // unroll, chunks, 0)


def _hgrn_selector():
    e = np.zeros((SUB * HEAD_DIM, LANES), np.float32)
    s_of_row = np.arange(SUB * HEAD_DIM) // HEAD_DIM
    cols = np.arange(LANES)
    e[:, :] = ((cols[None, :] % SUB) == s_of_row[:, None]) & (cols[None, :] < CHUNK)
    return jnp.asarray(e, BF16)


def _hgrn(u, lb_raw, gn, layer, bsz, seq, tb=1024):
    n = bsz * seq
    nb = seq // tb
    blk = lambda k: pl.BlockSpec((tb, HEAD_DIM), lambda b, h, j, k=k: (b * nb + j, h + N_HEADS * k))
    n_layers = lb_raw.shape[0]
    return pl.pallas_call(
        functools.partial(_hgrn_kernel, layer=layer, n_chunks=tb // CHUNK, unroll=16),
        out_shape=jax.ShapeDtypeStruct((n, WIDTH), F32),
        grid=(bsz, N_HEADS, nb),
        in_specs=[blk(0), blk(1), blk(2), blk(3),
                  pl.BlockSpec((n_layers, HEAD_DIM), lambda b, h, j: (0, h)),
                  pl.BlockSpec((1, HEAD_DIM), lambda b, h, j: (0, 0)),
                  pl.BlockSpec((SUB * HEAD_DIM, LANES), lambda b, h, j: (0, 0))],
        out_specs=pl.BlockSpec((tb, HEAD_DIM), lambda b, h, j: (b * nb + j, h)),
        scratch_shapes=[pltpu.VMEM((HEAD_DIM, HEAD_DIM), F32)],
        compiler_params=_params("parallel", "parallel", "arbitrary"),
        name="hgrn",
    )(u, u, u, u, lb_raw, gn, _hgrn_selector())


def _rope128(x, cos, sin_signed):
    return x * cos + pltpu.roll(x, HEAD_DIM // 2, 1) * sin_signed


def _rope64(x, cos, sin_a, sin_b):
    return x * cos + pltpu.roll(x, LANES - ROT_I, 1) * sin_a + pltpu.roll(x, ROT_I, 1) * sin_b


def _dsa_prep_kernel(k_ref, v_ref, ql_ref, misc_ref, ta_ref, tb_ref,
                     wqb_ref, wqi_ref, gql_ref, gqn_ref, gkn_ref, gki_ref,
                     q_out, k_out, vt_out, qi_out, ki_out, wt_out):
    ta, tb = ta_ref[...], tb_ref[...]
    lane = lax.broadcasted_iota(I32, ta.shape, 1)
    ta_swapped = pltpu.roll(ta, HEAD_DIM // 2, 1)
    cm = jnp.where(lane < HEAD_DIM // 2, ta, ta_swapped)
    sm = jnp.where(lane < HEAD_DIM // 2, -ta_swapped, ta)
    quarter = lane >> (ROT_I.bit_length() - 1)
    r32, r64, r96 = (pltpu.roll(tb, sh * ROT_I, 1) for sh in (1, 2, 3))
    ci = jnp.where(quarter == 0, tb, jnp.where(quarter == 1, r32, jnp.where(quarter == 2, r64, r96)))
    si = jnp.where(quarter == 0, r96, jnp.where(quarter == 1, tb, jnp.where(quarter == 2, r32, r64)))
    sia = jnp.where((lane & ROT_I) == 0, -si, 0.0)
    sib = jnp.where((lane & ROT_I) != 0, si, 0.0)
    cb = _rms(ql_ref[...], gql_ref[...]).astype(BF16)
    qf = _dot(cb, wqb_ref[...])
    scale = HEAD_DIM ** -0.5
    for h in range(N_HEADS):
        sl = slice(h * HEAD_DIM, (h + 1) * HEAD_DIM)
        qh = _rope128(_rms_head(qf[:, sl], gqn_ref[...]), cm, sm)
        q_out[:, sl] = (qh * (scale * LOG2E)).astype(BF16)
        kh = _rope128(_rms_head(k_ref[:, sl], gkn_ref[...]), cm, sm)
        k_out[:, sl] = kh.astype(BF16)
    vt = v_ref[...].T.astype(BF16)
    ones = jnp.ones((PACK, vt.shape[1]), BF16)
    for h in range(N_HEADS):
        vt_out[0, h * VT_ROWS:h * VT_ROWS + HEAD_DIM, :] = vt[h * HEAD_DIM:(h + 1) * HEAD_DIM, :]
        vt_out[0, h * VT_ROWS + HEAD_DIM:(h + 1) * VT_ROWS, :] = ones
    qi = _dot(cb, wqi_ref[...])
    for p in range(IDX_HEADS * IDX_DIM // LANES):
        sl = slice(p * LANES, (p + 1) * LANES)
        qi_out[:, sl] = _rope64(qi[:, sl], ci, sia, sib).astype(BF16)
    misc = misc_ref[...]
    kn = _rope64(_rms_head(misc, gki_ref[...], IDX_DIM), ci, sia, sib)
    ki_out[...] = (kn + pltpu.roll(kn, IDX_DIM, 1)).astype(BF16)
    wt = (misc * (IDX_HEADS ** -0.5 * IDX_DIM ** -0.5)).T
    wt_out[...] = wt[IDX_DIM:IDX_DIM + IDX_HEADS, :]


def _dsa_prep(u, tabs, wqb, wqi, gql, gqn, gkn, gki, tm=DSA_KEY_TILE):
    n = u.shape[0]
    nt = n // tm
    ublk = lambda width, col: pl.BlockSpec((tm, width), lambda i: (i, col // width))
    tab = pl.BlockSpec((tm, LANES), lambda i: (i, 0))
    full = lambda a: pl.BlockSpec(a.shape, lambda i: (0,) * a.ndim)
    return pl.pallas_call(
        _dsa_prep_kernel,
        out_shape=(jax.ShapeDtypeStruct((n, WIDTH), BF16),
                   jax.ShapeDtypeStruct((n, WIDTH), BF16),
                   jax.ShapeDtypeStruct((nt, N_HEADS * VT_ROWS, tm), BF16),
                   jax.ShapeDtypeStruct((n, IDX_HEADS * IDX_DIM), BF16),
                   jax.ShapeDtypeStruct((n, LANES), BF16),
                   jax.ShapeDtypeStruct((IDX_HEADS, n), F32)),
        grid=(nt,),
        in_specs=[ublk(WIDTH, U_K), ublk(WIDTH, U_V), ublk(Q_LORA, U_QLAT), ublk(LANES, U_MISC),
                  tab, tab,
                  full(wqb), full(wqi), full(gql), full(gqn), full(gkn), full(gki)],
        out_specs=(pl.BlockSpec((tm, WIDTH), lambda i: (i, 0)),
                   pl.BlockSpec((tm, WIDTH), lambda i: (i, 0)),
                   pl.BlockSpec((1, N_HEADS * VT_ROWS, tm), lambda i: (i, 0, 0)),
                   pl.BlockSpec((tm, IDX_HEADS * IDX_DIM), lambda i: (i, 0)),
                   pl.BlockSpec((tm, LANES), lambda i: (i, 0)),
                   pl.BlockSpec((IDX_HEADS, tm), lambda i: (0, i))),
        compiler_params=_params("parallel"),
        name="dsa_prep",
    )(u, u, u, u, *tabs, wqb, wqi, gql, gqn, gkn, gki)


def _sortable(x):
    bits = lax.bitcast_convert_type(x, I32)
    return bits ^ ((bits >> 31) & 0x7FFFFFFF)


def _dsa_attn_kernel(qi_ref, wt_ref, qin_ref, wtn_ref, q_ref, ki_ref, k_ref, vt_ref, o_ref,
                     qm_ref, hi_ref, lo_ref, kb_ref, pk_ref, lt_ref, acc0, acc1, acc2, acc3,
                     *, tq, kt, top_k, n_blocks):
    acc_refs = (acc0, acc1, acc2, acc3)
    blk = pl.program_id(1)
    cur = blk & 1
    nxt = 1 - cur
    q0 = blk * tq
    n_tiles = (q0 + tq + kt - 1) // kt
    pooled_per_tile = kt // POOL_ROWS * 2 * PACK
    sub = SCORE_ROWS
    key_iota = lax.broadcasted_iota(I32, (sub, tq), 0)
    lane_q = lax.broadcasted_iota(I32, (tq, LANES), 1)

    def tile_rows(jt):
        return pl.ds(pl.multiple_of(jt * kt, kt), kt)

    def limit_of(q_start):
        qpos = q_start + lax.broadcasted_iota(I32, (1, tq), 1)
        return ((qpos >> CHUNK_SHIFT) + 1) << CHUNK_SHIFT

    def build_qm(src_ref):
        for h in range(IDX_HEADS):
            src = src_ref[:, (h // 2) * LANES:(h // 2 + 1) * LANES].astype(F32)
            keep = (lane_q < IDX_DIM) if h % 2 == 0 else (lane_q >= IDX_DIM)
            qm_ref[h * tq:(h + 1) * tq, :] = jnp.where(keep, src, 0.0).astype(BF16)

    def score_tile(jt, slot, limit, w_ref):
        for part in range(kt // sub):
            base = jt * kt + part * sub
            rows = pl.ds(pl.multiple_of(base, sub), sub)
            ki = ki_ref[rows, :]
            acc = jnp.zeros((sub, tq), F32)
            for h in range(IDX_HEADS):
                x = _dot_nt(ki, qm_ref[h * tq:(h + 1) * tq, :])
                acc = acc + w_ref[h:h + 1, :] * jnp.maximum(x, 0.0)
            key = jnp.where(base + key_iota < limit, _sortable(acc), INT_MIN)
            hi_ref[slot, rows, :] = (key >> HALF_BITS).astype(I16)
            lo_ref[slot, rows, :] = ((key & (2 ** HALF_BITS - 1)) + INT16_MIN).astype(I16)

    @pl.when(blk == 0)
    def _():
        build_qm(qi_ref)
        limit0 = limit_of(0)

        def body(jt, carry):
            score_tile(jt, 0, limit0, wt_ref)
            return carry
        lax.fori_loop(0, n_tiles, body, 0)

    one, zero = jnp.ones((), BF16), jnp.zeros((), BF16)
    neg_inf = jnp.full((), -jnp.inf, BF16)
    n_rows = (n_tiles * kt).astype(F32)
    hi_rows = lambda rows: hi_ref[cur, rows, :]
    lo_rows = lambda rows: lo_ref[cur, rows, :]
    kb_rows = lambda rows: kb_ref[rows, :]

    def count16(get, trial, strict, trips=None):
        t16 = jnp.broadcast_to(trial.astype(I16), (PACK, tq))

        def hits(rows, n_rows):
            x = get(rows)
            parts = [jnp.zeros((PACK, tq), BF16) for _ in range(4)]
            for r in range(n_rows // PACK):
                xr = x[r * PACK:(r + 1) * PACK, :]
                parts[r % 4] = parts[r % 4] + jnp.where((xr > t16) if strict else (xr >= t16), one, zero)
            return ((parts[0] + parts[1]) + (parts[2] + parts[3])).astype(F32)

        n = n_tiles if trips is None else trips
        c = lax.fori_loop(0, n // 2, lambda jp, c: c + hits(pl.ds(pl.multiple_of(2 * jp * kt, 2 * kt), 2 * kt), 2 * kt),
                          jnp.zeros((PACK, tq), F32))
        c = lax.cond((n & 1) == 1, lambda c: c + hits(tile_rows(n - 1), kt), lambda c: c, c)
        return jnp.sum(c, axis=0, keepdims=True)

    def bisect16(get, want, n_all, early_exit, trips=None):
        def step(it, carry):
            lo, n_lo, n_up = carry
            trial = lo + lax.shift_left(jnp.int32(1), HALF_BITS - 1 - it)
            n = count16(get, trial, False, trips)
            ok = n >= want
            return jnp.where(ok, trial, lo), jnp.where(ok, n, n_lo), jnp.where(ok, n_up, n)

        init = (jnp.full((1, tq), INT16_MIN, I32), n_all, jnp.zeros((1, tq), F32))
        if not early_exit:
            return lax.fori_loop(0, HALF_BITS, step, init)

        def unresolved(carry):
            it, _, n_lo, _ = carry
            return (it < HALF_BITS) & (jnp.max(jnp.where(n_lo != want, 1.0, 0.0)) > 0.0)

        def two_steps(carry):
            return (carry[0] + 2,) + step(carry[0] + 1, step(carry[0], carry[1:]))

        return lax.while_loop(unresolved, two_steps, (jnp.int32(SURE_PASSES),) + lax.fori_loop(0, SURE_PASSES, step, init))[1:]

    tau_hi, n_hi_ge, n_hi_gt = bisect16(hi_rows, float(top_k), jnp.full((1, tq), n_rows, F32), False)
    want_lo = top_k - n_hi_gt
    tau_hi16 = jnp.broadcast_to(tau_hi.astype(I16), (kt, tq))

    min16 = jnp.full((PACK, tq), INT16_MIN, I16)

    def bucket_tile(jt, carry):
        rows = tile_rows(jt)
        kb = jnp.where(hi_rows(rows) == tau_hi16, lo_rows(rows), jnp.full((), INT16_MIN, I16))
        kb_ref[rows, :] = kb
        for g in range(kt // POOL_ROWS):
            top1, top2 = min16, min16
            for r in range(POOL_ROWS // PACK):
                x = kb[g * POOL_ROWS + r * PACK:g * POOL_ROWS + (r + 1) * PACK, :]
                below = x < top1
                runner = jnp.where(below, x, top1)
                top2 = jnp.where(runner > top2, runner, top2)
                top1 = jnp.where(below, top1, x)
            base = pl.multiple_of(jt * pooled_per_tile + g * 2 * PACK, 2 * PACK)
            pk_ref[pl.ds(base, PACK), :] = top1
            pk_ref[pl.ds(base + PACK, PACK), :] = top2
        return carry

    lax.fori_loop(0, n_tiles, bucket_tile, 0)
    pooled_trips = (n_tiles * pooled_per_tile + kt - 1) // kt

    def pad_pooled(jt, carry):
        pk_ref[pl.ds(pl.multiple_of(jt * pooled_per_tile, pooled_per_tile), pooled_per_tile), :] = jnp.full(
            (pooled_per_tile, tq), INT16_MIN, I16)
        return carry

    lax.fori_loop(n_tiles, pooled_trips * (kt // pooled_per_tile), pad_pooled, 0)
    n_bucket = n_hi_ge - n_hi_gt
    pk_rows = lambda rows: pk_ref[rows, :]
    tau_p, _, _ = bisect16(pk_rows, want_lo, n_bucket, True, pooled_trips)
    n_ge_p = jnp.where(tau_p == INT16_MIN, n_bucket, count16(kb_rows, tau_p, False))
    n_gt_p = count16(kb_rows, tau_p, True)
    pooled_ok = (n_ge_p == want_lo) | ((n_ge_p > want_lo) & (n_gt_p < want_lo))

    def full_search():
        t, n_ge, _ = bisect16(kb_rows, want_lo, n_bucket, True)
        return t, n_ge, count16(kb_rows, t, True)

    tau_lo, n_kb_ge, n_kb_gt = lax.cond(
        jnp.min(jnp.where(pooled_ok, 1.0, 0.0)) > 0.0,
        lambda: (tau_p, n_ge_p, n_gt_p), full_search)
    live = (tau_hi > INT16_MIN) | (tau_lo > INT16_MIN)
    excess = jnp.max(jnp.where(live, n_kb_ge - want_lo, 0.0))

    @pl.when(excess > 0)
    def _():
        need = want_lo - n_kb_gt
        live_f = jnp.where(live, 1.0, 0.0)
        tri = (lax.broadcasted_iota(I32, (kt, kt), 0)
               >= lax.broadcasted_iota(I32, (kt, kt), 1)).astype(BF16)

        def body(jt, seen):
            rows = tile_rows(jt)
            h32 = hi_rows(rows).astype(I32)
            eq = jnp.where(h32 == tau_hi, jnp.where(lo_rows(rows).astype(I32) == tau_lo, 1.0, 0.0), 0.0)
            rank = _dot(tri, eq.astype(BF16)) + seen
            dropped = eq * (1.0 - jnp.where(rank <= need, live_f, 0.0))
            hi_ref[cur, rows, :] = jnp.where(dropped > 0.0, INT16_MIN, h32).astype(I16)
            return rank[kt - 1:kt, :]
        lax.fori_loop(0, n_tiles, body, jnp.zeros((1, tq), F32))

    tau_lo16 = jnp.broadcast_to(jnp.where(live, tau_lo, INT16_MAX).astype(I16), (kt, tq))

    def bias_rows(rows):
        h16 = hi_rows(rows)
        in_bucket = jnp.where(lo_rows(rows) >= tau_lo16, zero, neg_inf)
        return jnp.where(h16 > tau_hi16, zero, jnp.where(h16 == tau_hi16, in_bucket, neg_inf)).astype(F32)

    for a in acc_refs:
        a[...] = jnp.zeros_like(a)
    heads = [slice(h * HEAD_DIM, (h + 1) * HEAD_DIM) for h in range(N_HEADS)]

    limit_next = limit_of(q0 + tq)

    def attn_tiles(tiles, ms, score_next):
        for i, jt in enumerate(tiles):
            rows = tile_rows(jt)
            bias = bias_rows(rows)
            for h in range(N_HEADS):
                lt_ref[i, h] = _dot_nt(k_ref[rows, heads[h]], q_ref[:, heads[h]]) + bias
        ms = list(ms)
        for i, jt in enumerate(tiles):
            if score_next:
                score_tile(jt, nxt, limit_next, wtn_ref)
            alphas, ps = [], []
            for h in range(N_HEADS):
                m_new = jnp.maximum(ms[h], jnp.max(lt_ref[i, h], axis=0, keepdims=True))
                m_safe = jnp.where(m_new == -jnp.inf, 0.0, m_new)
                ps.append(jnp.exp2(lt_ref[i, h] - m_safe).astype(BF16))
                alphas.append(jnp.exp2(ms[h] - m_safe))
                ms[h] = m_new
            for h in range(N_HEADS):
                acc_refs[h][...] = (acc_refs[h][...] * alphas[h]
                                    + _dot(vt_ref[jt, h * VT_ROWS:(h + 1) * VT_ROWS, :], ps[h]))
        return tuple(ms)

    def attention(score_next):
        ms = tuple(jnp.full((1, tq), -jnp.inf, F32) for _ in range(N_HEADS))
        ms = lax.fori_loop(0, n_tiles // 2, lambda jp, m: attn_tiles((2 * jp, 2 * jp + 1), m, score_next), ms)
        lax.cond((n_tiles & 1) == 1, lambda m: attn_tiles((n_tiles - 1,), m, score_next), lambda m: m, ms)

    @pl.when(blk < n_blocks - 1)
    def _():
        build_qm(qin_ref)
        attention(True)

        @pl.when((q0 + 2 * tq + kt - 1) // kt > n_tiles)
        def _():
            score_tile(n_tiles, nxt, limit_next, wtn_ref)

    @pl.when(blk == n_blocks - 1)
    def _():
        attention(False)

    for h in range(N_HEADS):
        acc = acc_refs[h][...]
        out_t = acc[:HEAD_DIM, :] * (1.0 / acc[HEAD_DIM:HEAD_DIM + 1, :])
        o_ref[:, heads[h]] = out_t.T


def _dsa_attn(q, k, vt, qi, ki, wt, bsz, seq, tq=256, kt=DSA_KEY_TILE):
    n = bsz * seq
    nq = seq // tq
    top_k = min(TOPK_MAX, seq // 4)
    once = dict(pipeline_mode=pl.Buffered(1))
    nxt_blk = lambda b, i: b * nq + jnp.minimum(i + 1, nq - 1)
    return pl.pallas_call(
        functools.partial(_dsa_attn_kernel, tq=tq, kt=kt, top_k=top_k, n_blocks=nq),
        out_shape=jax.ShapeDtypeStruct((n, WIDTH), F32),
        grid=(bsz, nq),
        in_specs=[pl.BlockSpec((tq, IDX_HEADS * IDX_DIM), lambda b, i: (b * nq + i, 0)),
                  pl.BlockSpec((IDX_HEADS, tq), lambda b, i: (0, b * nq + i)),
                  pl.BlockSpec((tq, IDX_HEADS * IDX_DIM), lambda b, i: (nxt_blk(b, i), 0)),
                  pl.BlockSpec((IDX_HEADS, tq), lambda b, i: (0, nxt_blk(b, i))),
                  pl.BlockSpec((tq, WIDTH), lambda b, i: (b * nq + i, 0)),
                  pl.BlockSpec((seq, LANES), lambda b, i: (b, 0), **once),
                  pl.BlockSpec((seq, WIDTH), lambda b, i: (b, 0), **once),
                  pl.BlockSpec((seq // kt, N_HEADS * VT_ROWS, kt), lambda b, i: (b, 0, 0), **once)],
        out_specs=pl.BlockSpec((tq, WIDTH), lambda b, i: (b * nq + i, 0)),
        scratch_shapes=[pltpu.VMEM((IDX_HEADS * tq, LANES), BF16),
                        pltpu.VMEM((2, seq, tq), I16), pltpu.VMEM((2, seq, tq), I16), pltpu.VMEM((seq, tq), I16),
                        pltpu.VMEM((pl.cdiv(seq // POOL_ROWS * 2 * PACK, kt) * kt, tq), I16),
                        pltpu.VMEM((2, N_HEADS, kt, tq), F32)]
                       + [pltpu.VMEM((VT_ROWS, tq), F32) for _ in range(N_HEADS)],
        compiler_params=_params("arbitrary", "arbitrary"),
        name="dsa_attn",
    )(qi, wt, qi, wt, q, ki, k, vt)


def _mem_kv_kernel(mem_ref, g_ref, wk_ref, wv_ref, gk_ref, k_out, v_out):
    mb = _rms(mem_ref[...], g_ref[...]).astype(BF16)
    kf = _dot(mb, wk_ref[...])
    for h in range(N_HEADS):
        sl = slice(h * HEAD_DIM, (h + 1) * HEAD_DIM)
        k_out[:, sl] = _rms_head(kf[:, sl], gk_ref[...]).astype(BF16)
    v_out[...] = _dot(mb, wv_ref[...]).astype(BF16)


def _mem_kv(mem2, gain, wk, wv, gk, n_mem):
    n, d = mem2.shape
    full = lambda a: pl.BlockSpec(a.shape, lambda b: (0,) * a.ndim)
    return pl.pallas_call(
        _mem_kv_kernel,
        out_shape=(jax.ShapeDtypeStruct((n, WIDTH), BF16), jax.ShapeDtypeStruct((n, WIDTH), BF16)),
        grid=(n // n_mem,),
        in_specs=[pl.BlockSpec((n_mem, d), lambda b: (b, 0)), full(gain), full(wk), full(wv), full(gk)],
        out_specs=(pl.BlockSpec((n_mem, WIDTH), lambda b: (b, 0)),
                   pl.BlockSpec((n_mem, WIDTH), lambda b: (b, 0))),
        compiler_params=_params("parallel"),
        name="mem_kv",
    )(mem2, gain, wk, wv, gk)


def _xattn_kernel(x_ref, oa_ref, ob_ref, wout_ref, g_ref, wq_ref, gq_ref, km_ref, vm_ref, wo_ref, o_ref):
    x1 = (x_ref[...] + _dot(oa_ref[...].astype(BF16), wout_ref[:WIDTH, :])
          + _dot(ob_ref[...].astype(BF16), wout_ref[WIDTH:, :]))
    hb = _rms(x1, g_ref[...]).astype(BF16)
    qf = _dot(hb, wq_ref[...])
    scale = HEAD_DIM ** -0.5
    outs = []
    for h in range(N_HEADS):
        sl = slice(h * HEAD_DIM, (h + 1) * HEAD_DIM)
        qh = (_rms(qf[:, sl], gq_ref[...]) * scale).astype(BF16)
        logits = _dot_nt(qh, km_ref[:, sl])
        p = jnp.exp(logits - jnp.max(logits, axis=-1, keepdims=True))
        oh = _dot(p.astype(BF16), vm_ref[:, sl])
        outs.append((oh * (1.0 / jnp.sum(p, axis=-1, keepdims=True))).astype(BF16))
    o_ref[...] = x1 + _dot(jnp.concatenate(outs, axis=1), wo_ref[...])


def _xattn(x2, oa, ob, wout, gain, wq, gq, km, vm, wo, seq, n_mem, tm=1024):
    n, d = x2.shape
    per_b = seq // tm
    row = lambda w: pl.BlockSpec((tm, w), lambda i: (i, 0))
    full = lambda a: pl.BlockSpec(a.shape, lambda i: (0,) * a.ndim)
    memb = pl.BlockSpec((n_mem, WIDTH), lambda i: (i // per_b, 0))
    return pl.pallas_call(
        _xattn_kernel,
        out_shape=jax.ShapeDtypeStruct((n, d), F32),
        grid=(n // tm,),
        in_specs=[row(d), row(WIDTH), row(WIDTH), full(wout), full(gain), full(wq), full(gq),
                  memb, memb, full(wo)],
        out_specs=row(d),
        compiler_params=_params("parallel"),
        name="xattn",
    )(x2, oa, ob, wout, gain, wq, gq, km, vm, wo)


def _moe_kernel(x_ref, g_ref, wr_hi_ref, wr_lo_ref, br_ref, wgu_ref, wdn_ref, o_ref, hs_ref, gs_ref, ys_ref,
                *, tm, sub):
    lane = lax.broadcasted_iota(I32, (tm, LANES), 1)
    lane_f = lane.astype(F32)
    x = x_ref[...]
    h = _rms(x, g_ref[...])
    h_hi = h.astype(BF16)
    h_lo = (h - h_hi.astype(F32)).astype(BF16)
    lg = (_dot(h_hi, wr_hi_ref[...]) + _dot(h_lo, wr_hi_ref[...]) + _dot(h_hi, wr_lo_ref[...])
          + br_ref[...])
    first = lambda cond: jnp.min(jnp.where(cond, lane_f, 1e9), axis=-1, keepdims=True)
    gl = jnp.where(lane < N_GROUPS, lg, -jnp.inf)
    gmax = jnp.max(gl, axis=-1, keepdims=True)
    gsel = first(gl == gmax)
    g_w = 1.0 / jnp.sum(jnp.exp(gl - gmax), axis=-1, keepdims=True)
    grp_of_lane = ((lane - N_GROUPS) >> (EXP_PER_GROUP.bit_length() - 1)).astype(F32)
    in_grp = (lane >= N_GROUPS) & (lane < N_GROUPS + N_EXPERTS) & (grp_of_lane == gsel)
    el = jnp.where(in_grp, lg, -jnp.inf)
    v1 = jnp.max(el, axis=-1, keepdims=True)
    i1 = first(el == v1)
    el2 = jnp.where(lane_f == i1, -jnp.inf, el)
    v2 = jnp.max(el2, axis=-1, keepdims=True)
    i2 = first(el2 == v2)
    r = jnp.exp(v2 - v1)
    w1 = 1.0 / (1.0 + r)
    gate = jnp.where(lane_f == i1, w1 * g_w, jnp.where(lane_f == i2, r * w1 * g_w, 0.0))

    onehot = jnp.where(lane_f == gsel, 1.0, 0.0)
    t_row = lax.broadcasted_iota(I32, (tm, tm), 0)
    t_col = lax.broadcasted_iota(I32, (tm, tm), 1)
    before = _dot(jnp.where(t_row > t_col, 1.0, 0.0).astype(BF16), onehot.astype(BF16))
    counts = jnp.sum(onehot, axis=0, keepdims=True)
    lane1 = lax.broadcasted_iota(I32, (1, LANES), 1)
    ends = []
    run = jnp.zeros((1, 1), F32)
    for g in range(N_GROUPS - 1):
        run = run + jnp.sum(jnp.where(lane1 == g, counts, 0.0), axis=-1, keepdims=True)
        ends.append(run)
    start_of = sum(jnp.where(lane1 == g + 1, ends[g], 0.0) for g in range(N_GROUPS - 1))
    pos = jnp.sum(onehot * (before + start_of), axis=-1, keepdims=True)
    to_sorted_t = jnp.where(pos == t_col.astype(F32), 1.0, 0.0)
    to_sorted = to_sorted_t.T.astype(BF16)
    hs_ref[...] = _dot(to_sorted, h_hi).astype(BF16)
    g_hi, g_mid, g_lo = _split3(gate)
    gs_ref[...] = _dot(to_sorted, g_hi) + _dot(to_sorted, g_mid) + _dot(to_sorted, g_lo)

    bounds = [e_[0, 0].astype(I32) for e_ in ends]
    lane_s = lax.broadcasted_iota(I32, (sub, LANES), 1)

    def slab(j, carry):
        r0 = j * sub
        rows = pl.ds(pl.multiple_of(r0, sub), sub)
        g_first = sum((b <= r0).astype(I32) for b in bounds)
        g_last = sum((b <= r0 + sub - 1).astype(I32) for b in bounds)
        hs = hs_ref[rows, :]
        gs = gs_ref[rows, :]
        ys_ref[rows, :] = jnp.zeros((sub, ys_ref.shape[1]), F32)

        def group(g, c):
            gu = _dot(hs, wgu_ref[g])
            acts = []
            for e in range(EXP_PER_GROUP):
                ge = jnp.sum(jnp.where(lane_s == g * EXP_PER_GROUP + e + N_GROUPS, gs, 0.0), axis=-1, keepdims=True)
                up = gu[:, 2 * e * D_EXPERT:(2 * e + 1) * D_EXPERT]
                acts.append((up * jax.nn.sigmoid(up) * gu[:, (2 * e + 1) * D_EXPERT:(2 * e + 2) * D_EXPERT] * ge).astype(BF16))
            ys_ref[rows, :] += _dot(jnp.concatenate(acts, axis=1), wdn_ref[g])
            return c

        lax.fori_loop(g_first, g_last + 1, group, 0)
        return carry

    lax.fori_loop(0, tm // sub, slab, 0)
    o_ref[...] = x + _dot(to_sorted_t.astype(BF16), ys_ref[...].astype(BF16))


def _moe(x2, gain, wr_hi, wr_lo, br, wgu, wdn, tm=512, sub=128):
    n, d = x2.shape
    full = lambda a: pl.BlockSpec(a.shape, lambda i: (0,) * a.ndim)
    once = lambda a: pl.BlockSpec(a.shape, lambda i: (0,) * a.ndim, pipeline_mode=pl.Buffered(1))
    return pl.pallas_call(
        functools.partial(_moe_kernel, tm=tm, sub=sub),
        out_shape=jax.ShapeDtypeStruct((n, d), F32),
        grid=(n // tm,),
        in_specs=[pl.BlockSpec((tm, d), lambda i: (i, 0)), full(gain), full(wr_hi), full(wr_lo), full(br),
                  once(wgu), once(wdn)],
        out_specs=pl.BlockSpec((tm, d), lambda i: (i, 0)),
        scratch_shapes=[pltpu.VMEM((tm, d), BF16), pltpu.VMEM((tm, LANES), F32), pltpu.VMEM((tm, d), F32)],
        compiler_params=_params("parallel"),
        name="moe",
    )(x2, gain, wr_hi, wr_lo, br, wgu, wdn)


def _rope_tables(positions):
    pos = positions.reshape(-1).astype(F32)[:, None]

    def cs(dim):
        inv = ROPE_THETA ** (-jnp.arange(0, dim, 2, dtype=F32) / dim)
        ang = pos * inv
        return jnp.cos(ang), jnp.sin(ang)

    c, s = cs(HEAD_DIM)
    ci, si = cs(IDX_DIM)
    return jnp.concatenate([c, s], 1), jnp.concatenate([ci, si, jnp.zeros_like(ci), jnp.zeros_like(ci)], 1)


def _reorder_in_weight(w):
    a = w[:, :4 * WIDTH]
    o = 4 * WIDTH
    qlat = w[:, o:o + Q_LORA]; o += Q_LORA
    k = w[:, o:o + WIDTH]; o += WIDTH
    v = w[:, o:o + WIDTH]; o += WIDTH
    misc = w[:, o:]
    pad = jnp.zeros((w.shape[0], LANES - misc.shape[1]), w.dtype)
    return jnp.concatenate([a, k, v, qlat, misc, pad], axis=1).astype(BF16)


def _row(v):
    return v.reshape(1, -1).astype(F32)


def kernel(x, mem, positions, norm_mix, w_in, lb_raw, a_gnorm, b_qlat_gain, b_wqb, b_wqidx, b_qnorm,
           b_knorm, b_kidx_norm, w_out, norm_x, norm_mem, x_wq, x_wk, x_wv, x_wo, x_qnorm, x_knorm,
           norm_ffn, w_rg, b_rg, w_re, b_re, w_gu, w_dn):
    bsz, seq, d = x.shape
    n_mem = mem.shape[1]
    depth = w_in.shape[0]
    assert d == D_MODEL and seq % (2 * DSA_KEY_TILE) == 0 and n_mem % LANES == 0, (x.shape, mem.shape)
    x2 = x.reshape(bsz * seq, d)
    mem2 = mem.reshape(bsz * n_mem, d)
    tabs = _rope_tables(positions)
    pad_lanes = lambda v: jnp.pad(v, ((0, 0), (0, LANES - v.shape[1])))
    for l in range(depth):
        u = _in_proj(x2, _row(norm_mix[l]), _reorder_in_weight(w_in[l]))
        o_a = _hgrn(u, lb_raw.astype(F32), _row(a_gnorm[l]), l, bsz, seq)
        q, k, vt, qi, ki, wt = _dsa_prep(
            u, tabs, b_wqb[l].astype(BF16), b_wqidx[l].astype(BF16), _row(b_qlat_gain[l]),
            _row(b_qnorm[l]), _row(b_knorm[l]), pad_lanes(_row(b_kidx_norm[l])))
        o_b = _dsa_attn(q, k, vt, qi, ki, wt, bsz, seq)
        km, vm = _mem_kv(mem2, _row(norm_mem[l]), x_wk[l].astype(BF16), x_wv[l].astype(BF16),
                         _row(x_knorm[l]), n_mem)
        x2 = _xattn(x2, o_a, o_b, w_out[l].astype(BF16), _row(norm_x[l]), x_wq[l].astype(BF16),
                    _row(x_qnorm[l]), km, vm, x_wo[l].astype(BF16), seq, n_mem)
        wr = pad_lanes(jnp.concatenate([w_rg[l], w_re[l].reshape(d, N_EXPERTS)], axis=1).astype(F32))
        wr_hi = wr.astype(BF16)
        wr_lo = (wr - wr_hi.astype(F32)).astype(BF16)
        br = pad_lanes(jnp.concatenate([b_rg[l], b_re[l].reshape(-1)]).reshape(1, -1).astype(F32))
        x2 = _moe(x2, _row(norm_ffn[l]), wr_hi, wr_lo, br,
                  w_gu[l].transpose(0, 2, 1, 3).reshape(N_GROUPS, d, EXP_PER_GROUP * 2 * D_EXPERT).astype(BF16),
                  w_dn[l].reshape(N_GROUPS, EXP_PER_GROUP * D_EXPERT, d).astype(BF16))
    return x2.reshape(bsz, seq, d)
```

```python
import functools

import numpy as np
import jax
import jax.numpy as jnp
from jax import lax
from jax.experimental import pallas as pl
from jax.experimental.pallas import tpu as pltpu

F32 = jnp.float32
BF16 = jnp.bfloat16
I32 = jnp.int32
I16 = jnp.int16

EPS = 1e-6
ROPE_THETA = 10000.0
CHUNK = 64
CHUNK_SHIFT = CHUNK.bit_length() - 1
SUB = 16
SUB_SHIFT = SUB.bit_length() - 1
N_HEADS = 4
HEAD_DIM = 128
WIDTH = N_HEADS * HEAD_DIM
Q_LORA = 256
D_MODEL = 1024
IDX_HEADS = 16
IDX_DIM = 64
ROT_I = IDX_DIM // 2
TOPK_MAX = 256
N_GROUPS = 4
EXP_PER_GROUP = 4
N_EXPERTS = N_GROUPS * EXP_PER_GROUP
D_EXPERT = 256
LANES = 128
INT_MIN = -2 ** 31
INT16_MIN, INT16_MAX = -2 ** 15, 2 ** 15 - 1
HALF_BITS = 16
PACK = 16
LOG2E = 1.4426950408889634
VT_ROWS = HEAD_DIM + PACK
DSA_KEY_TILE = 512
SCORE_ROWS = 128
POOL_ROWS = 256
SURE_PASSES = 8

U_K = 4 * WIDTH
U_V = U_K + WIDTH
U_QLAT = U_V + WIDTH
U_MISC = U_QLAT + Q_LORA
U_COLS = U_MISC + LANES

VMEM_LIMIT = 56 * 1024 * 1024


def _rms(x, gain):
    return x * lax.rsqrt(jnp.mean(x * x, axis=-1, keepdims=True) + EPS) * gain


def _dot(a, b):
    return jnp.dot(a, b, preferred_element_type=F32)


def _rms_head(x, gain, width=HEAD_DIM):
    ones = (lax.broadcasted_iota(I32, (LANES, LANES), 0) < width).astype(BF16)
    ss = _dot((x * x).astype(BF16), ones)
    return x * lax.rsqrt(ss * (1.0 / width) + EPS) * gain


def _dot_nt(a, b):
    return lax.dot_general(a, b, (((1,), (1,)), ((), ())), preferred_element_type=F32)


def _params(*sem):
    return pltpu.CompilerParams(dimension_semantics=sem, vmem_limit_bytes=VMEM_LIMIT)


def _in_proj_kernel(x_ref, g_ref, w_ref, u_ref, *, col_chunk):
    hb = _rms(x_ref[...], g_ref[...]).astype(BF16)
    for c0 in range(0, U_COLS, col_chunk):
        u_ref[:, c0:c0 + col_chunk] = _dot(hb, w_ref[:, c0:c0 + col_chunk])


def _in_proj(x2, gain, w_bf, tm=1024):
    n, d = x2.shape
    return pl.pallas_call(
        functools.partial(_in_proj_kernel, col_chunk=1152),
        out_shape=jax.ShapeDtypeStruct((n, U_COLS), F32),
        grid=(n // tm,),
        in_specs=[pl.BlockSpec((tm, d), lambda i: (i, 0)),
                  pl.BlockSpec((1, d), lambda i: (0, 0)),
                  pl.BlockSpec((d, U_COLS), lambda i: (0, 0), pipeline_mode=pl.Buffered(1))],
        out_specs=pl.BlockSpec((tm, U_COLS), lambda i: (i, 0)),
        compiler_params=_params("parallel"),
        name="in_proj",
    )(x2, gain, w_bf)


def _split3(x):
    hi = x.astype(BF16)
    r1 = x - hi.astype(F32)
    mid = r1.astype(BF16)
    lo = (r1 - mid.astype(F32)).astype(BF16)
    return hi, mid, lo


def _hgrn_kernel(q_ref, f_ref, i_ref, g_ref, lbraw_ref, gn_ref, e_ref, o_ref, st_ref,
                 *, layer, n_chunks, unroll):
    @pl.when(pl.program_id(2) == 0)
    def _():
        st_ref[...] = jnp.zeros_like(st_ref)

    lr = lbraw_ref[...]
    ex = jnp.exp(lr - jnp.max(lr, axis=0, keepdims=True))
    sm = ex / jnp.sum(ex, axis=0, keepdims=True)
    lb = jnp.zeros((1, HEAD_DIM), F32)
    for r in range(1, layer + 1):
        lb = lb + sm[r:r + 1, :]
    log_lb = jnp.log(lb)
    log_1mlb = jnp.log(1.0 - lb)

    row = lax.broadcasted_iota(I32, (CHUNK, HEAD_DIM), 0)
    col = lax.broadcasted_iota(I32, (CHUNK, HEAD_DIM), 1)
    row_l = row & (SUB - 1)
    row_b = row >> SUB_SHIFT
    col_b = col >> SUB_SHIFT
    tri = (lax.broadcasted_iota(I32, (CHUNK, CHUNK), 0)
           >= lax.broadcasted_iota(I32, (CHUNK, CHUNK), 1)).astype(BF16)
    n_sub = CHUNK // SUB
    zeros_c = jnp.zeros((CHUNK, HEAD_DIM), F32)

    def sub_bcast(t, s):
        t4 = t.reshape(n_sub, SUB, HEAD_DIM)
        return jnp.broadcast_to(t4[:, s:s + 1, :], (n_sub, SUB, HEAD_DIM)).reshape(CHUNK, HEAD_DIM)

    def gates(z):
        ls = jnp.minimum(z, 0.0) - jnp.log(1.0 + jnp.exp(-jnp.abs(z)))
        if layer == 0:
            return ls, (ls - z) * LOG2E
        cc = log_1mlb + ls
        log_f = jnp.maximum(log_lb, cc) + jnp.log(1.0 + jnp.exp(-jnp.abs(log_lb - cc)))
        return log_f, (cc - z) * LOG2E

    def diag_terms(q, b2, c2):
        pieces = []
        for s in range(SUB):
            d = jnp.where(row_l >= s, b2 - sub_bcast(c2, s), -jnp.inf)
            pieces.append((q * jnp.exp2(d)).astype(BF16))
        return jnp.concatenate(pieces, axis=1)

    def below_keys(b2, c2):
        kts = []
        for i in range(1, n_sub):
            r_i = b2[i * SUB:i * SUB + 1, :]
            kts.append(jnp.exp2(jnp.where(row < i * SUB, r_i - c2, -jnp.inf)))
            kts.append(zeros_c)
        return jnp.concatenate(kts, axis=0).astype(BF16)

    def below_scores(r):
        a = jnp.zeros((CHUNK, HEAD_DIM), F32)
        for i in range(1, n_sub):
            a = a + jnp.where(row_b == i, r[:, (i - 1) * LANES:i * LANES], 0.0)
        return a

    def chunks(it, carry):
        cs = range(unroll)
        rows = [pl.ds(pl.multiple_of((it * unroll + c) * CHUNK, CHUNK), CHUNK) for c in cs]
        qr = [q_ref[r, :] for r in rows]
        v = [i_ref[r, :] for r in rows]
        q = [x * jax.nn.sigmoid(x) for x in qr]
        gt = [gates(f_ref[r, :]) for r in rows]
        b2 = []
        for c in cs:
            hi, mid, lo = _split3(gt[c][0])
            b2.append((_dot(tri, hi) + _dot(tri, mid) + _dot(tri, lo)) * LOG2E)
        c2 = [b2[c] - gt[c][1] for c in cs]
        b_last = [x[CHUNK - 1:CHUNK, :] for x in b2]
        a2 = [_dot(diag_terms(q[c], b2[c], c2[c]), e_ref[...]) for c in cs]
        rr = [_dot_nt((q[c] * jnp.exp2(b2[c] - sub_bcast(b2[c], 0))).astype(BF16), below_keys(b2[c], c2[c]))
              for c in cs]
        upd = [_dot(v[c].T.astype(BF16), jnp.exp2(b_last[c] - c2[c]).astype(BF16)) for c in cs]
        qe = [(q[c] * jnp.exp2(b2[c])).astype(BF16) for c in cs]
        st = st_ref[...]
        o_inter = []
        for c in cs:
            o_inter.append(_dot_nt(qe[c], st.astype(BF16)))
            st = st * jnp.exp2(b_last[c]) + upd[c]
        st_ref[...] = st
        a = [jnp.where(col_b == row_b, a2[c], 0.0) + below_scores(rr[c]) for c in cs]
        o_intra = [_dot(a[c][:, :CHUNK].astype(BF16), v[c].astype(BF16)) for c in cs]
        for c in cs:
            g = g_ref[rows[c], :]
            o_ref[rows[c], :] = _rms(o_inter[c] + o_intra[c], gn_ref[...]) * (g * jax.nn.sigmoid(g))
        return carry

    lax.fori_loop(0, n_chunks // unroll, chunks, 0)


def _hgrn_selector():
    e = np.zeros((SUB * HEAD_DIM, LANES), np.float32)
    s_of_row = np.arange(SUB * HEAD_DIM) // HEAD_DIM
    cols = np.arange(LANES)
    e[:, :] = ((cols[None, :] % SUB) == s_of_row[:, None]) & (cols[None, :] < CHUNK)
    return jnp.asarray(e, BF16)


def _hgrn(u, lb_raw, gn, layer, bsz, seq, tb=1024):
    n = bsz * seq
    nb = seq // tb
    blk = lambda k: pl.BlockSpec((tb, HEAD_DIM), lambda b, h, j, k=k: (b * nb + j, h + N_HEADS * k))
    n_layers = lb_raw.shape[0]
    return pl.pallas_call(
        functools.partial(_hgrn_kernel, layer=layer, n_chunks=tb // CHUNK, unroll=16),
        out_shape=jax.ShapeDtypeStruct((n, WIDTH), F32),
        grid=(bsz, N_HEADS, nb),
        in_specs=[blk(0), blk(1), blk(2), blk(3),
                  pl.BlockSpec((n_layers, HEAD_DIM), lambda b, h, j: (0, h)),
                  pl.BlockSpec((1, HEAD_DIM), lambda b, h, j: (0, 0)),
                  pl.BlockSpec((SUB * HEAD_DIM, LANES), lambda b, h, j: (0, 0))],
        out_specs=pl.BlockSpec((tb, HEAD_DIM), lambda b, h, j: (b * nb + j, h)),
        scratch_shapes=[pltpu.VMEM((HEAD_DIM, HEAD_DIM), F32)],
        compiler_params=_params("parallel", "parallel", "arbitrary"),
        name="hgrn",
    )(u, u, u, u, lb_raw, gn, _hgrn_selector())


def _rope128(x, cos, sin_signed):
    return x * cos + pltpu.roll(x, HEAD_DIM // 2, 1) * sin_signed


def _rope64(x, cos, sin_a, sin_b):
    return x * cos + pltpu.roll(x, LANES - ROT_I, 1) * sin_a + pltpu.roll(x, ROT_I, 1) * sin_b


def _dsa_prep_kernel(k_ref, v_ref, ql_ref, misc_ref, ta_ref, tb_ref,
                     wqb_ref, wqi_ref, gql_ref, gqn_ref, gkn_ref, gki_ref,
                     q_out, k_out, vt_out, qi_out, ki_out, wt_out):
    ta, tb = ta_ref[...], tb_ref[...]
    lane = lax.broadcasted_iota(I32, ta.shape, 1)
    ta_swapped = pltpu.roll(ta, HEAD_DIM // 2, 1)
    cm = jnp.where(lane < HEAD_DIM // 2, ta, ta_swapped)
    sm = jnp.where(lane < HEAD_DIM // 2, -ta_swapped, ta)
    quarter = lane >> (ROT_I.bit_length() - 1)
    r32, r64, r96 = (pltpu.roll(tb, sh * ROT_I, 1) for sh in (1, 2, 3))
    ci = jnp.where(quarter == 0, tb, jnp.where(quarter == 1, r32, jnp.where(quarter == 2, r64, r96)))
    si = jnp.where(quarter == 0, r96, jnp.where(quarter == 1, tb, jnp.where(quarter == 2, r32, r64)))
    sia = jnp.where((lane & ROT_I) == 0, -si, 0.0)
    sib = jnp.where((lane & ROT_I) != 0, si, 0.0)
    cb = _rms(ql_ref[...], gql_ref[...]).astype(BF16)
    qf = _dot(cb, wqb_ref[...])
    scale = HEAD_DIM ** -0.5
    for h in range(N_HEADS):
        sl = slice(h * HEAD_DIM, (h + 1) * HEAD_DIM)
        qh = _rope128(_rms_head(qf[:, sl], gqn_ref[...]), cm, sm)
        q_out[:, sl] = (qh * (scale * LOG2E)).astype(BF16)
        kh = _rope128(_rms_head(k_ref[:, sl], gkn_ref[...]), cm, sm)
        k_out[:, sl] = kh.astype(BF16)
    vt = v_ref[...].T.astype(BF16)
    ones = jnp.ones((PACK, vt.shape[1]), BF16)
    for h in range(N_HEADS):
        vt_out[0, h * VT_ROWS:h * VT_ROWS + HEAD_DIM, :] = vt[h * HEAD_DIM:(h + 1) * HEAD_DIM, :]
        vt_out[0, h * VT_ROWS + HEAD_DIM:(h + 1) * VT_ROWS, :] = ones
    qi = _dot(cb, wqi_ref[...])
    for p in range(IDX_HEADS * IDX_DIM // LANES):
        sl = slice(p * LANES, (p + 1) * LANES)
        qi_out[:, sl] = _rope64(qi[:, sl], ci, sia, sib).astype(BF16)
    misc = misc_ref[...]
    kn = _rope64(_rms_head(misc, gki_ref[...], IDX_DIM), ci, sia, sib)
    ki_out[...] = (kn + pltpu.roll(kn, IDX_DIM, 1)).astype(BF16)
    wt = (misc * (IDX_HEADS ** -0.5 * IDX_DIM ** -0.5)).T
    wt_out[...] = wt[IDX_DIM:IDX_DIM + IDX_HEADS, :]


def _dsa_prep(u, tabs, wqb, wqi, gql, gqn, gkn, gki, tm=DSA_KEY_TILE):
    n = u.shape[0]
    nt = n // tm
    ublk = lambda width, col: pl.BlockSpec((tm, width), lambda i: (i, col // width))
    tab = pl.BlockSpec((tm, LANES), lambda i: (i, 0))
    full = lambda a: pl.BlockSpec(a.shape, lambda i: (0,) * a.ndim)
    return pl.pallas_call(
        _dsa_prep_kernel,
        out_shape=(jax.ShapeDtypeStruct((n, WIDTH), BF16),
                   jax.ShapeDtypeStruct((n, WIDTH), BF16),
                   jax.ShapeDtypeStruct((nt, N_HEADS * VT_ROWS, tm), BF16),
                   jax.ShapeDtypeStruct((n, IDX_HEADS * IDX_DIM), BF16),
                   jax.ShapeDtypeStruct((n, LANES), BF16),
                   jax.ShapeDtypeStruct((IDX_HEADS, n), F32)),
        grid=(nt,),
        in_specs=[ublk(WIDTH, U_K), ublk(WIDTH, U_V), ublk(Q_LORA, U_QLAT), ublk(LANES, U_MISC),
                  tab, tab,
                  full(wqb), full(wqi), full(gql), full(gqn), full(gkn), full(gki)],
        out_specs=(pl.BlockSpec((tm, WIDTH), lambda i: (i, 0)),
                   pl.BlockSpec((tm, WIDTH), lambda i: (i, 0)),
                   pl.BlockSpec((1, N_HEADS * VT_ROWS, tm), lambda i: (i, 0, 0)),
                   pl.BlockSpec((tm, IDX_HEADS * IDX_DIM), lambda i: (i, 0)),
                   pl.BlockSpec((tm, LANES), lambda i: (i, 0)),
                   pl.BlockSpec((IDX_HEADS, tm), lambda i: (0, i))),
        compiler_params=_params("parallel"),
        name="dsa_prep",
    )(u, u, u, u, *tabs, wqb, wqi, gql, gqn, gkn, gki)


def _sortable(x):
    bits = lax.bitcast_convert_type(x, I32)
    return bits ^ ((bits >> 31) & 0x7FFFFFFF)


def _dsa_attn_kernel(qi_ref, wt_ref, qin_ref, wtn_ref, q_ref, ki_ref, k_ref, vt_ref, o_ref,
                     qm_ref, hi_ref, lo_ref, kb_ref, pk_ref, lt_ref, acc0, acc1, acc2, acc3,
                     *, tq, kt, top_k, n_blocks):
    acc_refs = (acc0, acc1, acc2, acc3)
    blk = pl.program_id(1)
    cur = blk & 1
    nxt = 1 - cur
    q0 = blk * tq
    n_tiles = (q0 + tq + kt - 1) // kt
    pooled_per_tile = kt // POOL_ROWS * 2 * PACK
    sub = SCORE_ROWS
    key_iota = lax.broadcasted_iota(I32, (sub, tq), 0)
    lane_q = lax.broadcasted_iota(I32, (tq, LANES), 1)

    def tile_rows(jt):
        return pl.ds(pl.multiple_of(jt * kt, kt), kt)

    def limit_of(q_start):
        qpos = q_start + lax.broadcasted_iota(I32, (1, tq), 1)
        return ((qpos >> CHUNK_SHIFT) + 1) << CHUNK_SHIFT

    def build_qm(src_ref):
        for h in range(IDX_HEADS):
            src = src_ref[:, (h // 2) * LANES:(h // 2 + 1) * LANES].astype(F32)
            keep = (lane_q < IDX_DIM) if h % 2 == 0 else (lane_q >= IDX_DIM)
            qm_ref[h * tq:(h + 1) * tq, :] = jnp.where(keep, src, 0.0).astype(BF16)

    def score_tile(jt, slot, limit, w_ref):
        for part in range(kt // sub):
            base = jt * kt + part * sub
            rows = pl.ds(pl.multiple_of(base, sub), sub)
            ki = ki_ref[rows, :]
            acc = jnp.zeros((sub, tq), F32)
            for h in range(IDX_HEADS):
                x = _dot_nt(ki, qm_ref[h * tq:(h + 1) * tq, :])
                acc = acc + w_ref[h:h + 1, :] * jnp.maximum(x, 0.0)
            key = jnp.where(base + key_iota < limit, _sortable(acc), INT_MIN)
            hi_ref[slot, rows, :] = (key >> HALF_BITS).astype(I16)
            lo_ref[slot, rows, :] = ((key & (2 ** HALF_BITS - 1)) + INT16_MIN).astype(I16)

    @pl.when(blk == 0)
    def _():
        build_qm(qi_ref)
        limit0 = limit_of(0)

        def body(jt, carry):
            score_tile(jt, 0, limit0, wt_ref)
            return carry
        lax.fori_loop(0, n_tiles, body, 0)

    one, zero = jnp.ones((), BF16), jnp.zeros((), BF16)
    neg_inf = jnp.full((), -jnp.inf, BF16)
    n_rows = (n_tiles * kt).astype(F32)
    hi_rows = lambda rows: hi_ref[cur, rows, :]
    lo_rows = lambda rows: lo_ref[cur, rows, :]
    kb_rows = lambda rows: kb_ref[rows, :]

    def count16(get, trial, strict, trips=None):
        t16 = jnp.broadcast_to(trial.astype(I16), (PACK, tq))

        def hits(rows, n_rows):
            x = get(rows)
            parts = [jnp.zeros((PACK, tq), BF16) for _ in range(4)]
            for r in range(n_rows // PACK):
                xr = x[r * PACK:(r + 1) * PACK, :]
                parts[r % 4] = parts[r % 4] + jnp.where((xr > t16) if strict else (xr >= t16), one, zero)
            return ((parts[0] + parts[1]) + (parts[2] + parts[3])).astype(F32)

        n = n_tiles if trips is None else trips
        c = lax.fori_loop(0, n // 2, lambda jp, c: c + hits(pl.ds(pl.multiple_of(2 * jp * kt, 2 * kt), 2 * kt), 2 * kt),
                          jnp.zeros((PACK, tq), F32))
        c = lax.cond((n & 1) == 1, lambda c: c + hits(tile_rows(n - 1), kt), lambda c: c, c)
        return jnp.sum(c, axis=0, keepdims=True)

    def bisect16(get, want, n_all, early_exit, trips=None):
        def step(it, carry):
            lo, n_lo, n_up = carry
            trial = lo + lax.shift_left(jnp.int32(1), HALF_BITS - 1 - it)
            n = count16(get, trial, False, trips)
            ok = n >= want
            return jnp.where(ok, trial, lo), jnp.where(ok, n, n_lo), jnp.where(ok, n_up, n)

        init = (jnp.full((1, tq), INT16_MIN, I32), n_all, jnp.zeros((1, tq), F32))
        if not early_exit:
            return lax.fori_loop(0, HALF_BITS, step, init)

        def unresolved(carry):
            it, _, n_lo, _ = carry
            return (it < HALF_BITS) & (jnp.max(jnp.where(n_lo != want, 1.0, 0.0)) > 0.0)

        def two_steps(carry):
            return (carry[0] + 2,) + step(carry[0] + 1, step(carry[0], carry[1:]))

        return lax.while_loop(unresolved, two_steps, (jnp.int32(SURE_PASSES),) + lax.fori_loop(0, SURE_PASSES, step, init))[1:]

    tau_hi, n_hi_ge, n_hi_gt = bisect16(hi_rows, float(top_k), jnp.full((1, tq), n_rows, F32), False)
    want_lo = top_k - n_hi_gt
    tau_hi16 = jnp.broadcast_to(tau_hi.astype(I16), (kt, tq))

    min16 = jnp.full((PACK, tq), INT16_MIN, I16)

    def bucket_tile(jt, carry):
        rows = tile_rows(jt)
        kb = jnp.where(hi_rows(rows) == tau_hi16, lo_rows(rows), jnp.full((), INT16_MIN, I16))
        kb_ref[rows, :] = kb
        for g in range(kt // POOL_ROWS):
            top1, top2 = min16, min16
            for r in range(POOL_ROWS // PACK):
                x = kb[g * POOL_ROWS + r * PACK:g * POOL_ROWS + (r + 1) * PACK, :]
                below = x < top1
                runner = jnp.where(below, x, top1)
                top2 = jnp.where(runner > top2, runner, top2)
                top1 = jnp.where(below, top1, x)
            base = pl.multiple_of(jt * pooled_per_tile + g * 2 * PACK, 2 * PACK)
            pk_ref[pl.ds(base, PACK), :] = top1
            pk_ref[pl.ds(base + PACK, PACK), :] = top2
        return carry

    lax.fori_loop(0, n_tiles, bucket_tile, 0)
    pooled_trips = (n_tiles * pooled_per_tile + kt - 1) // kt

    def pad_pooled(jt, carry):
        pk_ref[pl.ds(pl.multiple_of(jt * pooled_per_tile, pooled_per_tile), pooled_per_tile), :] = jnp.full(
            (pooled_per_tile, tq), INT16_MIN, I16)
        return carry

    lax.fori_loop(n_tiles, pooled_trips * (kt // pooled_per_tile), pad_pooled, 0)
    n_bucket = n_hi_ge - n_hi_gt
    pk_rows = lambda rows: pk_ref[rows, :]
    tau_p, _, _ = bisect16(pk_rows, want_lo, n_bucket, True, pooled_trips)
    n_ge_p = jnp.where(tau_p == INT16_MIN, n_bucket, count16(kb_rows, tau_p, False))
    n_gt_p = count16(kb_rows, tau_p, True)
    pooled_ok = (n_ge_p == want_lo) | ((n_ge_p > want_lo) & (n_gt_p < want_lo))

    def full_search():
        t, n_ge, _ = bisect16(kb_rows, want_lo, n_bucket, True)
        return t, n_ge, count16(kb_rows, t, True)

    tau_lo, n_kb_ge, n_kb_gt = lax.cond(
        jnp.min(jnp.where(pooled_ok, 1.0, 0.0)) > 0.0,
        lambda: (tau_p, n_ge_p, n_gt_p), full_search)
    live = (tau_hi > INT16_MIN) | (tau_lo > INT16_MIN)
    excess = jnp.max(jnp.where(live, n_kb_ge - want_lo, 0.0))

    @pl.when(excess > 0)
    def _():
        need = want_lo - n_kb_gt
        live_f = jnp.where(live, 1.0, 0.0)
        tri = (lax.broadcasted_iota(I32, (kt, kt), 0)
               >= lax.broadcasted_iota(I32, (kt, kt), 1)).astype(BF16)

        def body(jt, seen):
            rows = tile_rows(jt)
            h32 = hi_rows(rows).astype(I32)
            eq = jnp.where(h32 == tau_hi, jnp.where(lo_rows(rows).astype(I32) == tau_lo, 1.0, 0.0), 0.0)
            rank = _dot(tri, eq.astype(BF16)) + seen
            dropped = eq * (1.0 - jnp.where(rank <= need, live_f, 0.0))
            hi_ref[cur, rows, :] = jnp.where(dropped > 0.0, INT16_MIN, h32).astype(I16)
            return rank[kt - 1:kt, :]
        lax.fori_loop(0, n_tiles, body, jnp.zeros((1, tq), F32))

    tau_lo16 = jnp.broadcast_to(jnp.where(live, tau_lo, INT16_MAX).astype(I16), (kt, tq))

    def bias_rows(rows):
        h16 = hi_rows(rows)
        in_bucket = jnp.where(lo_rows(rows) >= tau_lo16, zero, neg_inf)
        return jnp.where(h16 > tau_hi16, zero, jnp.where(h16 == tau_hi16, in_bucket, neg_inf)).astype(F32)

    for a in acc_refs:
        a[...] = jnp.zeros_like(a)
    heads = [slice(h * HEAD_DIM, (h + 1) * HEAD_DIM) for h in range(N_HEADS)]

    limit_next = limit_of(q0 + tq)

    def attn_tiles(tiles, ms, score_next):
        for i, jt in enumerate(tiles):
            rows = tile_rows(jt)
            bias = bias_rows(rows)
            for h in range(N_HEADS):
                lt_ref[i, h] = _dot_nt(k_ref[rows, heads[h]], q_ref[:, heads[h]]) + bias
        ms = list(ms)
        for i, jt in enumerate(tiles):
            if score_next:
                score_tile(jt, nxt, limit_next, wtn_ref)
            alphas, ps = [], []
            for h in range(N_HEADS):
                m_new = jnp.maximum(ms[h], jnp.max(lt_ref[i, h], axis=0, keepdims=True))
                m_safe = jnp.where(m_new == -jnp.inf, 0.0, m_new)
                ps.append(jnp.exp2(lt_ref[i, h] - m_safe).astype(BF16))
                alphas.append(jnp.exp2(ms[h] - m_safe))
                ms[h] = m_new
            for h in range(N_HEADS):
                acc_refs[h][...] = (acc_refs[h][...] * alphas[h]
                                    + _dot(vt_ref[jt, h * VT_ROWS:(h + 1) * VT_ROWS, :], ps[h]))
        return tuple(ms)

    def attention(score_next):
        ms = tuple(jnp.full((1, tq), -jnp.inf, F32) for _ in range(N_HEADS))
        ms = lax.fori_loop(0, n_tiles // 2, lambda jp, m: attn_tiles((2 * jp, 2 * jp + 1), m, score_next), ms)
        lax.cond((n_tiles & 1) == 1, lambda m: attn_tiles((n_tiles - 1,), m, score_next), lambda m: m, ms)

    @pl.when(blk < n_blocks - 1)
    def _():
        build_qm(qin_ref)
        attention(True)

        @pl.when((q0 + 2 * tq + kt - 1) // kt > n_tiles)
        def _():
            score_tile(n_tiles, nxt, limit_next, wtn_ref)

    @pl.when(blk == n_blocks - 1)
    def _():
        attention(False)

    for h in range(N_HEADS):
        acc = acc_refs[h][...]
        out_t = acc[:HEAD_DIM, :] * (1.0 / acc[HEAD_DIM:HEAD_DIM + 1, :])
        o_ref[:, heads[h]] = out_t.T


def _dsa_attn(q, k, vt, qi, ki, wt, bsz, seq, tq=256, kt=DSA_KEY_TILE):
    n = bsz * seq
    nq = seq // tq
    top_k = min(TOPK_MAX, seq // 4)
    once = dict(pipeline_mode=pl.Buffered(1))
    nxt_blk = lambda b, i: b * nq + jnp.minimum(i + 1, nq - 1)
    return pl.pallas_call(
        functools.partial(_dsa_attn_kernel, tq=tq, kt=kt, top_k=top_k, n_blocks=nq),
        out_shape=jax.ShapeDtypeStruct((n, WIDTH), F32),
        grid=(bsz, nq),
        in_specs=[pl.BlockSpec((tq, IDX_HEADS * IDX_DIM), lambda b, i: (b * nq + i, 0)),
                  pl.BlockSpec((IDX_HEADS, tq), lambda b, i: (0, b * nq + i)),
                  pl.BlockSpec((tq, IDX_HEADS * IDX_DIM), lambda b, i: (nxt_blk(b, i), 0)),
                  pl.BlockSpec((IDX_HEADS, tq), lambda b, i: (0, nxt_blk(b, i))),
                  pl.BlockSpec((tq, WIDTH), lambda b, i: (b * nq + i, 0)),
                  pl.BlockSpec((seq, LANES), lambda b, i: (b, 0), **once),
                  pl.BlockSpec((seq, WIDTH), lambda b, i: (b, 0), **once),
                  pl.BlockSpec((seq // kt, N_HEADS * VT_ROWS, kt), lambda b, i: (b, 0, 0), **once)],
        out_specs=pl.BlockSpec((tq, WIDTH), lambda b, i: (b * nq + i, 0)),
        scratch_shapes=[pltpu.VMEM((IDX_HEADS * tq, LANES), BF16),
                        pltpu.VMEM((2, seq, tq), I16), pltpu.VMEM((2, seq, tq), I16), pltpu.VMEM((seq, tq), I16),
                        pltpu.VMEM((pl.cdiv(seq // POOL_ROWS * 2 * PACK, kt) * kt, tq), I16),
                        pltpu.VMEM((2, N_HEADS, kt, tq), F32)]
                       + [pltpu.VMEM((VT_ROWS, tq), F32) for _ in range(N_HEADS)],
        compiler_params=_params("arbitrary", "arbitrary"),
        name="dsa_attn",
    )(qi, wt, qi, wt, q, ki, k, vt)


def _mem_kv_kernel(mem_ref, g_ref, wk_ref, wv_ref, gk_ref, k_out, v_out):
    mb = _rms(mem_ref[...], g_ref[...]).astype(BF16)
    kf = _dot(mb, wk_ref[...])
    for h in range(N_HEADS):
        sl = slice(h * HEAD_DIM, (h + 1) * HEAD_DIM)
        k_out[:, sl] = _rms_head(kf[:, sl], gk_ref[...]).astype(BF16)
    v_out[...] = _dot(mb, wv_ref[...]).astype(BF16)


def _mem_kv(mem2, gain, wk, wv, gk, n_mem):
    n, d = mem2.shape
    full = lambda a: pl.BlockSpec(a.shape, lambda b: (0,) * a.ndim)
    return pl.pallas_call(
        _mem_kv_kernel,
        out_shape=(jax.ShapeDtypeStruct((n, WIDTH), BF16), jax.ShapeDtypeStruct((n, WIDTH), BF16)),
        grid=(n // n_mem,),
        in_specs=[pl.BlockSpec((n_mem, d), lambda b: (b, 0)), full(gain), full(wk), full(wv), full(gk)],
        out_specs=(pl.BlockSpec((n_mem, WIDTH), lambda b: (b, 0)),
                   pl.BlockSpec((n_mem, WIDTH), lambda b: (b, 0))),
        compiler_params=_params("parallel"),
        name="mem_kv",
    )(mem2, gain, wk, wv, gk)


def _xattn_kernel(x_ref, oa_ref, ob_ref, wout_ref, g_ref, wq_ref, gq_ref, km_ref, vm_ref, wo_ref, o_ref):
    x1 = (x_ref[...] + _dot(oa_ref[...].astype(BF16), wout_ref[:WIDTH, :])
          + _dot(ob_ref[...].astype(BF16), wout_ref[WIDTH:, :]))
    hb = _rms(x1, g_ref[...]).astype(BF16)
    qf = _dot(hb, wq_ref[...])
    scale = HEAD_DIM ** -0.5
    outs = []
    for h in range(N_HEADS):
        sl = slice(h * HEAD_DIM, (h + 1) * HEAD_DIM)
        qh = (_rms(qf[:, sl], gq_ref[...]) * scale).astype(BF16)
        logits = _dot_nt(qh, km_ref[:, sl])
        p = jnp.exp(logits - jnp.max(logits, axis=-1, keepdims=True))
        oh = _dot(p.astype(BF16), vm_ref[:, sl])
        outs.append((oh * (1.0 / jnp.sum(p, axis=-1, keepdims=True))).astype(BF16))
    o_ref[...] = x1 + _dot(jnp.concatenate(outs, axis=1), wo_ref[...])


def _xattn(x2, oa, ob, wout, gain, wq, gq, km, vm, wo, seq, n_mem, tm=1024):
    n, d = x2.shape
    per_b = seq // tm
    row = lambda w: pl.BlockSpec((tm, w), lambda i: (i, 0))
    full = lambda a: pl.BlockSpec(a.shape, lambda i: (0,) * a.ndim)
    memb = pl.BlockSpec((n_mem, WIDTH), lambda i: (i // per_b, 0))
    return pl.pallas_call(
        _xattn_kernel,
        out_shape=jax.ShapeDtypeStruct((n, d), F32),
        grid=(n // tm,),
        in_specs=[row(d), row(WIDTH), row(WIDTH), full(wout), full(gain), full(wq), full(gq),
                  memb, memb, full(wo)],
        out_specs=row(d),
        compiler_params=_params("parallel"),
        name="xattn",
    )(x2, oa, ob, wout, gain, wq, gq, km, vm, wo)


def _moe_kernel(x_ref, g_ref, wr_hi_ref, wr_lo_ref, br_ref, wgu_ref, wdn_ref, o_ref, hs_ref, gs_ref, ys_ref,
                *, tm, sub):
    lane = lax.broadcasted_iota(I32, (tm, LANES), 1)
    lane_f = lane.astype(F32)
    x = x_ref[...]
    h = _rms(x, g_ref[...])
    h_hi = h.astype(BF16)
    h_lo = (h - h_hi.astype(F32)).astype(BF16)
    lg = (_dot(h_hi, wr_hi_ref[...]) + _dot(h_lo, wr_hi_ref[...]) + _dot(h_hi, wr_lo_ref[...])
          + br_ref[...])
    first = lambda cond: jnp.min(jnp.where(cond, lane_f, 1e9), axis=-1, keepdims=True)
    gl = jnp.where(lane < N_GROUPS, lg, -jnp.inf)
    gmax = jnp.max(gl, axis=-1, keepdims=True)
    gsel = first(gl == gmax)
    g_w = 1.0 / jnp.sum(jnp.exp(gl - gmax), axis=-1, keepdims=True)
    grp_of_lane = ((lane - N_GROUPS) >> (EXP_PER_GROUP.bit_length() - 1)).astype(F32)
    in_grp = (lane >= N_GROUPS) & (lane < N_GROUPS + N_EXPERTS) & (grp_of_lane == gsel)
    el = jnp.where(in_grp, lg, -jnp.inf)
    v1 = jnp.max(el, axis=-1, keepdims=True)
    i1 = first(el == v1)
    el2 = jnp.where(lane_f == i1, -jnp.inf, el)
    v2 = jnp.max(el2, axis=-1, keepdims=True)
    i2 = first(el2 == v2)
    r = jnp.exp(v2 - v1)
    w1 = 1.0 / (1.0 + r)
    gate = jnp.where(lane_f == i1, w1 * g_w, jnp.where(lane_f == i2, r * w1 * g_w, 0.0))

    onehot = jnp.where(lane_f == gsel, 1.0, 0.0)
    t_row = lax.broadcasted_iota(I32, (tm, tm), 0)
    t_col = lax.broadcasted_iota(I32, (tm, tm), 1)
    before = _dot(jnp.where(t_row > t_col, 1.0, 0.0).astype(BF16), onehot.astype(BF16))
    counts = jnp.sum(onehot, axis=0, keepdims=True)
    lane1 = lax.broadcasted_iota(I32, (1, LANES), 1)
    ends = []
    run = jnp.zeros((1, 1), F32)
    for g in range(N_GROUPS - 1):
        run = run + jnp.sum(jnp.where(lane1 == g, counts, 0.0), axis=-1, keepdims=True)
        ends.append(run)
    start_of = sum(jnp.where(lane1 == g + 1, ends[g], 0.0) for g in range(N_GROUPS - 1))
    pos = jnp.sum(onehot * (before + start_of), axis=-1, keepdims=True)
    to_sorted_t = jnp.where(pos == t_col.astype(F32), 1.0, 0.0)
    to_sorted = to_sorted_t.T.astype(BF16)
    hs_ref[...] = _dot(to_sorted, h_hi).astype(BF16)
    g_hi, g_mid, g_lo = _split3(gate)
    gs_ref[...] = _dot(to_sorted, g_hi) + _dot(to_sorted, g_mid) + _dot(to_sorted, g_lo)

    bounds = [e_[0, 0].astype(I32) for e_ in ends]
    lane_s = lax.broadcasted_iota(I32, (sub, LANES), 1)

    def slab(j, carry):
        r0 = j * sub
        rows = pl.ds(pl.multiple_of(r0, sub), sub)
        g_first = sum((b <= r0).astype(I32) for b in bounds)
        g_last = sum((b <= r0 + sub - 1).astype(I32) for b in bounds)
        hs = hs_ref[rows, :]
        gs = gs_ref[rows, :]
        ys_ref[rows, :] = jnp.zeros((sub, ys_ref.shape[1]), F32)

        def group(g, c):
            gu = _dot(hs, wgu_ref[g])
            acts = []
            for e in range(EXP_PER_GROUP):
                ge = jnp.sum(jnp.where(lane_s == g * EXP_PER_GROUP + e + N_GROUPS, gs, 0.0), axis=-1, keepdims=True)
                up = gu[:, 2 * e * D_EXPERT:(2 * e + 1) * D_EXPERT]
                acts.append((up * jax.nn.sigmoid(up) * gu[:, (2 * e + 1) * D_EXPERT:(2 * e + 2) * D_EXPERT] * ge).astype(BF16))
            ys_ref[rows, :] += _dot(jnp.concatenate(acts, axis=1), wdn_ref[g])
            return c

        lax.fori_loop(g_first, g_last + 1, group, 0)
        return carry

    lax.fori_loop(0, tm // sub, slab, 0)
    o_ref[...] = x + _dot(to_sorted_t.astype(BF16), ys_ref[...].astype(BF16))


def _moe(x2, gain, wr_hi, wr_lo, br, wgu, wdn, tm=512, sub=128):
    n, d = x2.shape
    full = lambda a: pl.BlockSpec(a.shape, lambda i: (0,) * a.ndim)
    once = lambda a: pl.BlockSpec(a.shape, lambda i: (0,) * a.ndim, pipeline_mode=pl.Buffered(1))
    return pl.pallas_call(
        functools.partial(_moe_kernel, tm=tm, sub=sub),
        out_shape=jax.ShapeDtypeStruct((n, d), F32),
        grid=(n // tm,),
        in_specs=[pl.BlockSpec((tm, d), lambda i: (i, 0)), full(gain), full(wr_hi), full(wr_lo), full(br),
                  once(wgu), once(wdn)],
        out_specs=pl.BlockSpec((tm, d), lambda i: (i, 0)),
        scratch_shapes=[pltpu.VMEM((tm, d), BF16), pltpu.VMEM((tm, LANES), F32), pltpu.VMEM((tm, d), F32)],
        compiler_params=_params("parallel"),
        name="moe",
    )(x2, gain, wr_hi, wr_lo, br, wgu, wdn)


def _rope_tables(positions):
    pos = positions.reshape(-1).astype(F32)[:, None]
    ang = pos * ROPE_THETA ** (-jnp.arange(0, HEAD_DIM, 2, dtype=F32) / HEAD_DIM)
    c, s = jnp.cos(ang), jnp.sin(ang)
    assert HEAD_DIM == 2 * IDX_DIM
    ci, si = c[:, ::2], s[:, ::2]
    return jnp.concatenate([c, s], 1), jnp.concatenate([ci, si, jnp.zeros_like(ci), jnp.zeros_like(ci)], 1)


def _reorder_in_weight(w):
    a = w[:, :4 * WIDTH]
    o = 4 * WIDTH
    qlat = w[:, o:o + Q_LORA]; o += Q_LORA
    k = w[:, o:o + WIDTH]; o += WIDTH
    v = w[:, o:o + WIDTH]; o += WIDTH
    misc = w[:, o:]
    pad = jnp.zeros((w.shape[0], LANES - misc.shape[1]), w.dtype)
    return jnp.concatenate([a, k, v, qlat, misc, pad], axis=1).astype(BF16)


def _row(v):
    return v.reshape(1, -1).astype(F32)


def kernel(x, mem, positions, norm_mix, w_in, lb_raw, a_gnorm, b_qlat_gain, b_wqb, b_wqidx, b_qnorm,
           b_knorm, b_kidx_norm, w_out, norm_x, norm_mem, x_wq, x_wk, x_wv, x_wo, x_qnorm, x_knorm,
           norm_ffn, w_rg, b_rg, w_re, b_re, w_gu, w_dn):
    bsz, seq, d = x.shape
    n_mem = mem.shape[1]
    depth = w_in.shape[0]
    assert d == D_MODEL and seq % (2 * DSA_KEY_TILE) == 0 and n_mem % LANES == 0, (x.shape, mem.shape)
    x2 = x.reshape(bsz * seq, d)
    mem2 = mem.reshape(bsz * n_mem, d)
    tabs = _rope_tables(positions)
    pad_lanes = lambda v: jnp.pad(v, ((0, 0), (0, LANES - v.shape[1])))
    for l in range(depth):
        u = _in_proj(x2, _row(norm_mix[l]), _reorder_in_weight(w_in[l]))
        o_a = _hgrn(u, lb_raw.astype(F32), _row(a_gnorm[l]), l, bsz, seq)
        q, k, vt, qi, ki, wt = _dsa_prep(
            u, tabs, b_wqb[l].astype(BF16), b_wqidx[l].astype(BF16), _row(b_qlat_gain[l]),
            _row(b_qnorm[l]), _row(b_knorm[l]), pad_lanes(_row(b_kidx_norm[l])))
        o_b = _dsa_attn(q, k, vt, qi, ki, wt, bsz, seq)
        km, vm = _mem_kv(mem2, _row(norm_mem[l]), x_wk[l].astype(BF16), x_wv[l].astype(BF16),
                         _row(x_knorm[l]), n_mem)
        x2 = _xattn(x2, o_a, o_b, w_out[l].astype(BF16), _row(norm_x[l]), x_wq[l].astype(BF16),
                    _row(x_qnorm[l]), km, vm, x_wo[l].astype(BF16), seq, n_mem)
        wr = pad_lanes(jnp.concatenate([w_rg[l], w_re[l].reshape(d, N_EXPERTS)], axis=1).astype(F32))
        wr_hi = wr.astype(BF16)
        wr_lo = (wr - wr_hi.astype(F32)).astype(BF16)
        br = pad_lanes(jnp.concatenate([b_rg[l], b_re[l].reshape(-1)]).reshape(1, -1).astype(F32))
        x2 = _moe(x2, _row(norm_ffn[l]), wr_hi, wr_lo, br,
                  w_gu[l].transpose(0, 2, 1, 3).reshape(N_GROUPS, d, EXP_PER_GROUP * 2 * D_EXPERT).astype(BF16),
                  w_dn[l].reshape(N_GROUPS, EXP_PER_GROUP * D_EXPERT, d).astype(BF16))
    return x2.reshape(bsz, seq, d)
```

```python
import functools

import numpy as np
import jax
import jax.numpy as jnp
from jax import lax
from jax.experimental import pallas as pl
from jax.experimental.pallas import tpu as pltpu

F32 = jnp.float32
BF16 = jnp.bfloat16
I32 = jnp.int32
I16 = jnp.int16

EPS = 1e-6
ROPE_THETA = 10000.0
CHUNK = 64
CHUNK_SHIFT = CHUNK.bit_length() - 1
SUB = 16
SUB_SHIFT = SUB.bit_length() - 1
N_HEADS = 4
HEAD_DIM = 128
WIDTH = N_HEADS * HEAD_DIM
Q_LORA = 256
D_MODEL = 1024
IDX_HEADS = 16
IDX_DIM = 64
ROT_I = IDX_DIM // 2
TOPK_MAX = 256
N_GROUPS = 4
EXP_PER_GROUP = 4
N_EXPERTS = N_GROUPS * EXP_PER_GROUP
D_EXPERT = 256
LANES = 128
INT_MIN = -2 ** 31
INT16_MIN, INT16_MAX = -2 ** 15, 2 ** 15 - 1
HALF_BITS = 16
PACK = 16
LOG2E = 1.4426950408889634
VT_ROWS = HEAD_DIM + PACK
DSA_KEY_TILE = 512
SCORE_ROWS = 128
POOL_ROWS = 256
SURE_PASSES = 8

U_K = 4 * WIDTH
U_V = U_K + WIDTH
U_QLAT = U_V + WIDTH
U_MISC = U_QLAT + Q_LORA
U_COLS = U_MISC + LANES

VMEM_LIMIT = 56 * 1024 * 1024


def _rms(x, gain):
    return x * lax.rsqrt(jnp.mean(x * x, axis=-1, keepdims=True) + EPS) * gain


def _dot(a, b):
    return jnp.dot(a, b, preferred_element_type=F32)


def _rms_head(x, gain, width=HEAD_DIM):
    ones = (lax.broadcasted_iota(I32, (LANES, LANES), 0) < width).astype(BF16)
    ss = _dot((x * x).astype(BF16), ones)
    return x * lax.rsqrt(ss * (1.0 / width) + EPS) * gain


def _dot_nt(a, b):
    return lax.dot_general(a, b, (((1,), (1,)), ((), ())), preferred_element_type=F32)


def _params(*sem):
    return pltpu.CompilerParams(dimension_semantics=sem, vmem_limit_bytes=VMEM_LIMIT)


def _in_proj_kernel(x_ref, g_ref, w_ref, u_ref, *, col_chunk):
    hb = _rms(x_ref[...], g_ref[...]).astype(BF16)
    for c0 in range(0, U_COLS, col_chunk):
        u_ref[:, c0:c0 + col_chunk] = _dot(hb, w_ref[:, c0:c0 + col_chunk])


def _in_proj(x2, gain, w_bf, tm=1024):
    n, d = x2.shape
    return pl.pallas_call(
        functools.partial(_in_proj_kernel, col_chunk=1152),
        out_shape=jax.ShapeDtypeStruct((n, U_COLS), F32),
        grid=(n // tm,),
        in_specs=[pl.BlockSpec((tm, d), lambda i: (i, 0)),
                  pl.BlockSpec((1, d), lambda i: (0, 0)),
                  pl.BlockSpec((d, U_COLS), lambda i: (0, 0), pipeline_mode=pl.Buffered(1))],
        out_specs=pl.BlockSpec((tm, U_COLS), lambda i: (i, 0)),
        compiler_params=_params("parallel"),
        name="in_proj",
    )(x2, gain, w_bf)


def _split3(x):
    hi = x.astype(BF16)
    r1 = x - hi.astype(F32)
    mid = r1.astype(BF16)
    lo = (r1 - mid.astype(F32)).astype(BF16)
    return hi, mid, lo


def _hgrn_kernel(q_ref, f_ref, i_ref, g_ref, lbraw_ref, gn_ref, e_ref, o_ref, st_ref,
                 *, layer, n_chunks, unroll):
    @pl.when(pl.program_id(2) == 0)
    def _():
        st_ref[...] = jnp.zeros_like(st_ref)

    lr = lbraw_ref[...]
    ex = jnp.exp(lr - jnp.max(lr, axis=0, keepdims=True))
    sm = ex / jnp.sum(ex, axis=0, keepdims=True)
    lb = jnp.zeros((1, HEAD_DIM), F32)
    for r in range(1, layer + 1):
        lb = lb + sm[r:r + 1, :]
    log_lb = jnp.log(lb)
    log_1mlb = jnp.log(1.0 - lb)

    row = lax.broadcasted_iota(I32, (CHUNK, HEAD_DIM), 0)
    col = lax.broadcasted_iota(I32, (CHUNK, HEAD_DIM), 1)
    row_l = row & (SUB - 1)
    row_b = row >> SUB_SHIFT
    col_b = col >> SUB_SHIFT
    tri = (lax.broadcasted_iota(I32, (CHUNK, CHUNK), 0)
           >= lax.broadcasted_iota(I32, (CHUNK, CHUNK), 1)).astype(BF16)
    n_sub = CHUNK // SUB
    zeros_c = jnp.zeros((CHUNK, HEAD_DIM), F32)

    def sub_bcast(t, s):
        t4 = t.reshape(n_sub, SUB, HEAD_DIM)
        return jnp.broadcast_to(t4[:, s:s + 1, :], (n_sub, SUB, HEAD_DIM)).reshape(CHUNK, HEAD_DIM)

    def gates(z):
        ls = jnp.minimum(z, 0.0) - jnp.log(1.0 + jnp.exp(-jnp.abs(z)))
        if layer == 0:
            return ls, (ls - z) * LOG2E
        cc = log_1mlb + ls
        log_f = jnp.maximum(log_lb, cc) + jnp.log(1.0 + jnp.exp(-jnp.abs(log_lb - cc)))
        return log_f, (cc - z) * LOG2E

    def diag_terms(q, b2, c2):
        pieces = []
        for s in range(SUB):
            d = jnp.where(row_l >= s, b2 - sub_bcast(c2, s), -jnp.inf)
            pieces.append((q * jnp.exp2(d)).astype(BF16))
        return jnp.concatenate(pieces, axis=1)

    def below_keys(b2, c2):
        kts = []
        for i in range(1, n_sub):
            r_i = b2[i * SUB:i * SUB + 1, :]
            kts.append(jnp.exp2(jnp.where(row < i * SUB, r_i - c2, -jnp.inf)))
            kts.append(zeros_c)
        return jnp.concatenate(kts, axis=0).astype(BF16)

    def below_scores(r):
        a = jnp.zeros((CHUNK, HEAD_DIM), F32)
        for i in range(1, n_sub):
            a = a + jnp.where(row_b == i, r[:, (i - 1) * LANES:i * LANES], 0.0)
        return a

    def chunks(it, carry):
        cs = range(unroll)
        rows = [pl.ds(pl.multiple_of((it * unroll + c) * CHUNK, CHUNK), CHUNK) for c in cs]
        qr = [q_ref[r, :] for r in rows]
        v = [i_ref[r, :] for r in rows]
        q = [x * jax.nn.sigmoid(x) for x in qr]
        gt = [gates(f_ref[r, :]) for r in rows]
        b2 = []
        for c in cs:
            hi, mid, lo = _split3(gt[c][0])
            b2.append((_dot(tri, hi) + _dot(tri, mid) + _dot(tri, lo)) * LOG2E)
        c2 = [b2[c] - gt[c][1] for c in cs]
        b_last = [x[CHUNK - 1:CHUNK, :] for x in b2]
        a2 = [_dot(diag_terms(q[c], b2[c], c2[c]), e_ref[...]) for c in cs]
        rr = [_dot_nt((q[c] * jnp.exp2(b2[c] - sub_bcast(b2[c], 0))).astype(BF16), below_keys(b2[c], c2[c]))
              for c in cs]
        upd = [_dot(v[c].T.astype(BF16), jnp.exp2(b_last[c] - c2[c]).astype(BF16)) for c in cs]
        qe = [(q[c] * jnp.exp2(b2[c])).astype(BF16) for c in cs]
        st = st_ref[...]
        o_inter = []
        for c in cs:
            o_inter.append(_dot_nt(qe[c], st.astype(BF16)))
            st = st * jnp.exp2(b_last[c]) + upd[c]
        st_ref[...] = st
        a = [jnp.where(col_b == row_b, a2[c], 0.0) + below_scores(rr[c]) for c in cs]
        o_intra = [_dot(a[c][:, :CHUNK].astype(BF16), v[c].astype(BF16)) for c in cs]
        for c in cs:
            g = g_ref[rows[c], :]
            o_ref[rows[c], :] = _rms(o_inter[c] + o_intra[c], gn_ref[...]) * (g * jax.nn.sigmoid(g))
        return carry

    lax.fori_loop(0, n_chunks // unroll, chunks, 0)


def _hgrn_selector():
    e = np.zeros((SUB * HEAD_DIM, LANES), np.float32)
    s_of_row = np.arange(SUB * HEAD_DIM) // HEAD_DIM
    cols = np.arange(LANES)
    e[:, :] = ((cols[None, :] % SUB) == s_of_row[:, None]) & (cols[None, :] < CHUNK)
    return jnp.asarray(e, BF16)


def _hgrn(u, lb_raw, gn, layer, bsz, seq, tb=1024):
    n = bsz * seq
    nb = seq // tb
    blk = lambda k: pl.BlockSpec((tb, HEAD_DIM), lambda b, h, j, k=k: (b * nb + j, h + N_HEADS * k))
    n_layers = lb_raw.shape[0]
    return pl.pallas_call(
        functools.partial(_hgrn_kernel, layer=layer, n_chunks=tb // CHUNK, unroll=16),
        out_shape=jax.ShapeDtypeStruct((n, WIDTH), F32),
        grid=(bsz, N_HEADS, nb),
        in_specs=[blk(0), blk(1), blk(2), blk(3),
                  pl.BlockSpec((n_layers, HEAD_DIM), lambda b, h, j: (0, h)),
                  pl.BlockSpec((1, HEAD_DIM), lambda b, h, j: (0, 0)),
                  pl.BlockSpec((SUB * HEAD_DIM, LANES), lambda b, h, j: (0, 0))],
        out_specs=pl.BlockSpec((tb, HEAD_DIM), lambda b, h, j: (b * nb + j, h)),
        scratch_shapes=[pltpu.VMEM((HEAD_DIM, HEAD_DIM), F32)],
        compiler_params=_params("parallel", "parallel", "arbitrary"),
        name="hgrn",
    )(u, u, u, u, lb_raw, gn, _hgrn_selector())


def _rope128(x, cos, sin_signed):
    return x * cos + pltpu.roll(x, HEAD_DIM // 2, 1) * sin_signed


def _rope64(x, cos, sin_a, sin_b):
    return x * cos + pltpu.roll(x, LANES - ROT_I, 1) * sin_a + pltpu.roll(x, ROT_I, 1) * sin_b


def _dsa_prep_kernel(k_ref, v_ref, ql_ref, misc_ref, ta_ref, tb_ref,
                     wqb_ref, wqi_ref, gql_ref, gqn_ref, gkn_ref, gki_ref,
                     q_out, k_out, vt_out, qi_out, ki_out, wt_out):
    ta, tb = ta_ref[...], tb_ref[...]
    lane = lax.broadcasted_iota(I32, ta.shape, 1)
    ta_swapped = pltpu.roll(ta, HEAD_DIM // 2, 1)
    cm = jnp.where(lane < HEAD_DIM // 2, ta, ta_swapped)
    sm = jnp.where(lane < HEAD_DIM // 2, -ta_swapped, ta)
    quarter = lane >> (ROT_I.bit_length() - 1)
    r32, r64, r96 = (pltpu.roll(tb, sh * ROT_I, 1) for sh in (1, 2, 3))
    ci = jnp.where(quarter == 0, tb, jnp.where(quarter == 1, r32, jnp.where(quarter == 2, r64, r96)))
    si = jnp.where(quarter == 0, r96, jnp.where(quarter == 1, tb, jnp.where(quarter == 2, r32, r64)))
    sia = jnp.where((lane & ROT_I) == 0, -si, 0.0)
    sib = jnp.where((lane & ROT_I) != 0, si, 0.0)
    cb = _rms(ql_ref[...], gql_ref[...]).astype(BF16)
    qf = _dot(cb, wqb_ref[...])
    scale = HEAD_DIM ** -0.5
    for h in range(N_HEADS):
        sl = slice(h * HEAD_DIM, (h + 1) * HEAD_DIM)
        qh = _rope128(_rms_head(qf[:, sl], gqn_ref[...]), cm, sm)
        q_out[:, sl] = (qh * (scale * LOG2E)).astype(BF16)
        kh = _rope128(_rms_head(k_ref[:, sl], gkn_ref[...]), cm, sm)
        k_out[:, sl] = kh.astype(BF16)
    vt = v_ref[...].T.astype(BF16)
    ones = jnp.ones((PACK, vt.shape[1]), BF16)
    for h in range(N_HEADS):
        vt_out[0, h * VT_ROWS:h * VT_ROWS + HEAD_DIM, :] = vt[h * HEAD_DIM:(h + 1) * HEAD_DIM, :]
        vt_out[0, h * VT_ROWS + HEAD_DIM:(h + 1) * VT_ROWS, :] = ones
    qi = _dot(cb, wqi_ref[...])
    for p in range(IDX_HEADS * IDX_DIM // LANES):
        sl = slice(p * LANES, (p + 1) * LANES)
        qi_out[:, sl] = _rope64(qi[:, sl], ci, sia, sib).astype(BF16)
    misc = misc_ref[...]
    kn = _rope64(_rms_head(misc, gki_ref[...], IDX_DIM), ci, sia, sib)
    ki_out[...] = (kn + pltpu.roll(kn, IDX_DIM, 1)).astype(BF16)
    wt = (misc * (IDX_HEADS ** -0.5 * IDX_DIM ** -0.5)).T
    wt_out[...] = wt[IDX_DIM:IDX_DIM + IDX_HEADS, :]


def _dsa_prep(u, tabs, wqb, wqi, gql, gqn, gkn, gki, tm=DSA_KEY_TILE):
    n = u.shape[0]
    nt = n // tm
    ublk = lambda width, col: pl.BlockSpec((tm, width), lambda i: (i, col // width))
    tab = pl.BlockSpec((tm, LANES), lambda i: (i, 0))
    full = lambda a: pl.BlockSpec(a.shape, lambda i: (0,) * a.ndim)
    return pl.pallas_call(
        _dsa_prep_kernel,
        out_shape=(jax.ShapeDtypeStruct((n, WIDTH), BF16),
                   jax.ShapeDtypeStruct((n, WIDTH), BF16),
                   jax.ShapeDtypeStruct((nt, N_HEADS * VT_ROWS, tm), BF16),
                   jax.ShapeDtypeStruct((n, IDX_HEADS * IDX_DIM), BF16),
                   jax.ShapeDtypeStruct((n, LANES), BF16),
                   jax.ShapeDtypeStruct((IDX_HEADS, n), F32)),
        grid=(nt,),
        in_specs=[ublk(WIDTH, U_K), ublk(WIDTH, U_V), ublk(Q_LORA, U_QLAT), ublk(LANES, U_MISC),
                  tab, tab,
                  full(wqb), full(wqi), full(gql), full(gqn), full(gkn), full(gki)],
        out_specs=(pl.BlockSpec((tm, WIDTH), lambda i: (i, 0)),
                   pl.BlockSpec((tm, WIDTH), lambda i: (i, 0)),
                   pl.BlockSpec((1, N_HEADS * VT_ROWS, tm), lambda i: (i, 0, 0)),
                   pl.BlockSpec((tm, IDX_HEADS * IDX_DIM), lambda i: (i, 0)),
                   pl.BlockSpec((tm, LANES), lambda i: (i, 0)),
                   pl.BlockSpec((IDX_HEADS, tm), lambda i: (0, i))),
        compiler_params=_params("parallel"),
        name="dsa_prep",
    )(u, u, u, u, *tabs, wqb, wqi, gql, gqn, gkn, gki)


def _sortable(x):
    bits = lax.bitcast_convert_type(x, I32)
    return bits ^ ((bits >> 31) & 0x7FFFFFFF)


def _dsa_attn_kernel(qi_ref, wt_ref, qin_ref, wtn_ref, q_ref, ki_ref, k_ref, vt_ref, o_ref,
                     qm_ref, hi_ref, lo_ref, kb_ref, pk_ref, lt_ref, acc0, acc1, acc2, acc3,
                     *, tq, kt, top_k, n_blocks):
    acc_refs = (acc0, acc1, acc2, acc3)
    blk = pl.program_id(1)
    cur = blk & 1
    nxt = 1 - cur
    q0 = blk * tq
    n_keys = q0 + tq
    sub = SCORE_ROWS
    key_iota = lax.broadcasted_iota(I32, (sub, tq), 0)
    lane_q = lax.broadcasted_iota(I32, (tq, LANES), 1)
    pooled_per_group = 2 * PACK

    def sweep(total, fn, carry):
        n_full = total // kt
        carry = lax.fori_loop(0, n_full // 2, lambda jp, c: fn(pl.multiple_of(jp * 2 * kt, 2 * kt), 2 * kt, c), carry)
        carry = lax.cond((n_full & 1) == 1, lambda c: fn(pl.multiple_of((n_full - 1) * kt, kt), kt, c),
                         lambda c: c, carry)
        return lax.cond(total % kt != 0, lambda c: fn(pl.multiple_of(n_full * kt, kt), kt // 2, c),
                        lambda c: c, carry)

    def limit_of(q_start):
        qpos = q_start + lax.broadcasted_iota(I32, (1, tq), 1)
        return ((qpos >> CHUNK_SHIFT) + 1) << CHUNK_SHIFT

    def build_qm(src_ref):
        for h in range(IDX_HEADS):
            src = src_ref[:, (h // 2) * LANES:(h // 2 + 1) * LANES].astype(F32)
            keep = (lane_q < IDX_DIM) if h % 2 == 0 else (lane_q >= IDX_DIM)
            qm_ref[h * tq:(h + 1) * tq, :] = jnp.where(keep, src, 0.0).astype(BF16)

    def score_rows(start, size, slot, limit, w_ref):
        for part in range(size // sub):
            base = start + part * sub
            rows = pl.ds(pl.multiple_of(base, sub), sub)
            ki = ki_ref[rows, :]
            acc = jnp.zeros((sub, tq), F32)
            for h in range(IDX_HEADS):
                x = _dot_nt(ki, qm_ref[h * tq:(h + 1) * tq, :])
                acc = acc + w_ref[h:h + 1, :] * jnp.maximum(x, 0.0)
            key = jnp.where(base + key_iota < limit, _sortable(acc), INT_MIN)
            hi_ref[slot, rows, :] = (key >> HALF_BITS).astype(I16)
            lo_ref[slot, rows, :] = ((key & (2 ** HALF_BITS - 1)) + INT16_MIN).astype(I16)

    @pl.when(blk == 0)
    def _():
        build_qm(qi_ref)
        limit0 = limit_of(0)

        def body(start, size, carry):
            score_rows(start, size, 0, limit0, wt_ref)
            return carry
        sweep(n_keys, body, 0)

    one, zero = jnp.ones((), BF16), jnp.zeros((), BF16)
    neg_inf = jnp.full((), -jnp.inf, BF16)
    hi_rows = lambda rows: hi_ref[cur, rows, :]
    lo_rows = lambda rows: lo_ref[cur, rows, :]
    kb_rows = lambda rows: kb_ref[rows, :]

    def count16(get, trial, strict, total=None):
        t16 = jnp.broadcast_to(trial.astype(I16), (PACK, tq))

        def hits(start, size, c):
            x = get(pl.ds(start, size))
            parts = [jnp.zeros((PACK, tq), BF16) for _ in range(4)]
            for r in range(size // PACK):
                xr = x[r * PACK:(r + 1) * PACK, :]
                parts[r % 4] = parts[r % 4] + jnp.where((xr > t16) if strict else (xr >= t16), one, zero)
            return c + ((parts[0] + parts[1]) + (parts[2] + parts[3])).astype(F32)

        c = sweep(n_keys if total is None else total, hits, jnp.zeros((PACK, tq), F32))
        return jnp.sum(c, axis=0, keepdims=True)

    def bisect16(get, want, n_all, early_exit, total=None):
        def step(it, carry):
            lo, n_lo, n_up = carry
            trial = lo + lax.shift_left(jnp.int32(1), HALF_BITS - 1 - it)
            n = count16(get, trial, False, total)
            ok = n >= want
            return jnp.where(ok, trial, lo), jnp.where(ok, n, n_lo), jnp.where(ok, n_up, n)

        init = (jnp.full((1, tq), INT16_MIN, I32), n_all, jnp.zeros((1, tq), F32))
        if not early_exit:
            return lax.fori_loop(0, HALF_BITS, step, init)

        def unresolved(carry):
            it, _, n_lo, _ = carry
            return (it < HALF_BITS) & (jnp.max(jnp.where(n_lo != want, 1.0, 0.0)) > 0.0)

        def two_steps(carry):
            return (carry[0] + 2,) + step(carry[0] + 1, step(carry[0], carry[1:]))

        return lax.while_loop(unresolved, two_steps, (jnp.int32(SURE_PASSES),) + lax.fori_loop(0, SURE_PASSES, step, init))[1:]

    tau_hi, n_hi_ge, n_hi_gt = bisect16(hi_rows, float(top_k), jnp.full((1, tq), n_keys.astype(F32), F32), False)
    want_lo = top_k - n_hi_gt
    tau_hi_row = tau_hi.astype(I16)

    min16 = jnp.full((PACK, tq), INT16_MIN, I16)

    def bucket_rows(start, size, carry):
        rows = pl.ds(start, size)
        kb = jnp.where(hi_rows(rows) == jnp.broadcast_to(tau_hi_row, (size, tq)), lo_rows(rows),
                       jnp.full((), INT16_MIN, I16))
        kb_ref[rows, :] = kb
        for g in range(size // POOL_ROWS):
            top1, top2 = min16, min16
            for r in range(POOL_ROWS // PACK):
                x = kb[g * POOL_ROWS + r * PACK:g * POOL_ROWS + (r + 1) * PACK, :]
                below = x < top1
                runner = jnp.where(below, x, top1)
                top2 = jnp.where(runner > top2, runner, top2)
                top1 = jnp.where(below, top1, x)
            base = pl.multiple_of((start // POOL_ROWS + g) * pooled_per_group, pooled_per_group)
            pk_ref[pl.ds(base, PACK), :] = top1
            pk_ref[pl.ds(base + PACK, PACK), :] = top2
        return carry

    sweep(n_keys, bucket_rows, 0)
    n_groups = n_keys // POOL_ROWS
    pooled_total = (n_groups * pooled_per_group + kt // 2 - 1) // (kt // 2) * (kt // 2)

    def pad_pooled(g, carry):
        pk_ref[pl.ds(pl.multiple_of(g * pooled_per_group, pooled_per_group), pooled_per_group), :] = jnp.full(
            (pooled_per_group, tq), INT16_MIN, I16)
        return carry

    lax.fori_loop(n_groups, pooled_total // pooled_per_group, pad_pooled, 0)
    n_bucket = n_hi_ge - n_hi_gt
    pk_rows = lambda rows: pk_ref[rows, :]
    tau_p, _, _ = bisect16(pk_rows, want_lo, n_bucket, True, pooled_total)
    n_ge_p = jnp.where(tau_p == INT16_MIN, n_bucket, count16(kb_rows, tau_p, False))
    n_gt_p = count16(kb_rows, tau_p, True)
    pooled_ok = (n_ge_p == want_lo) | ((n_ge_p > want_lo) & (n_gt_p < want_lo))

    def full_search():
        t, n_ge, _ = bisect16(kb_rows, want_lo, n_bucket, True)
        return t, n_ge, count16(kb_rows, t, True)

    tau_lo, n_kb_ge, n_kb_gt = lax.cond(
        jnp.min(jnp.where(pooled_ok, 1.0, 0.0)) > 0.0,
        lambda: (tau_p, n_ge_p, n_gt_p), full_search)
    live = (tau_hi > INT16_MIN) | (tau_lo > INT16_MIN)
    excess = jnp.max(jnp.where(live, n_kb_ge - want_lo, 0.0))

    @pl.when(excess > 0)
    def _():
        need = want_lo - n_kb_gt
        live_f = jnp.where(live, 1.0, 0.0)
        step_rows = kt // 2
        tri = (lax.broadcasted_iota(I32, (step_rows, step_rows), 0)
               >= lax.broadcasted_iota(I32, (step_rows, step_rows), 1)).astype(BF16)

        def body(start, size, seen):
            for part in range(size // step_rows):
                rows = pl.ds(pl.multiple_of(start + part * step_rows, step_rows), step_rows)
                h32 = hi_rows(rows).astype(I32)
                eq = jnp.where(h32 == tau_hi, jnp.where(lo_rows(rows).astype(I32) == tau_lo, 1.0, 0.0), 0.0)
                rank = _dot(tri, eq.astype(BF16)) + seen
                dropped = eq * (1.0 - jnp.where(rank <= need, live_f, 0.0))
                hi_ref[cur, rows, :] = jnp.where(dropped > 0.0, INT16_MIN, h32).astype(I16)
                seen = rank[step_rows - 1:step_rows, :]
            return seen
        sweep(n_keys, body, jnp.zeros((1, tq), F32))

    tau_lo_row = jnp.where(live, tau_lo, INT16_MAX).astype(I16)

    def bias_rows(rows, size):
        h16 = hi_rows(rows)
        th = jnp.broadcast_to(tau_hi_row, (size, tq))
        in_bucket = jnp.where(lo_rows(rows) >= jnp.broadcast_to(tau_lo_row, (size, tq)), zero, neg_inf)
        return jnp.where(h16 > th, zero, jnp.where(h16 == th, in_bucket, neg_inf)).astype(F32)

    for a in acc_refs:
        a[...] = jnp.zeros_like(a)
    heads = [slice(h * HEAD_DIM, (h + 1) * HEAD_DIM) for h in range(N_HEADS)]
    limit_next = limit_of(q0 + tq)

    def attend(start, size, ms, score_next):
        segs = [(start, kt), (start + kt, kt)] if size == 2 * kt else [(start, size)]
        for i, (s0, n) in enumerate(segs):
            rows = pl.ds(pl.multiple_of(s0, kt // 2), n)
            bias = bias_rows(rows, n)
            for h in range(N_HEADS):
                lt_ref[i, h, :n, :] = _dot_nt(k_ref[rows, heads[h]], q_ref[:, heads[h]]) + bias
        ms = list(ms)
        for i, (s0, n) in enumerate(segs):
            if score_next:
                score_rows(s0, n, nxt, limit_next, wtn_ref)
            alphas, ps = [], []
            for h in range(N_HEADS):
                lt = lt_ref[i, h, :n, :]
                m_new = jnp.maximum(ms[h], jnp.max(lt, axis=0, keepdims=True))
                m_safe = jnp.where(m_new == -jnp.inf, 0.0, m_new)
                ps.append(jnp.exp2(lt - m_safe).astype(BF16))
                alphas.append(jnp.exp2(ms[h] - m_safe))
                ms[h] = m_new
            for h in range(N_HEADS):
                acc_refs[h][...] = (acc_refs[h][...] * alphas[h]
                                    + _dot(vt_ref[s0 // kt, h * VT_ROWS:(h + 1) * VT_ROWS, :n], ps[h]))
        return tuple(ms)

    m_init = tuple(jnp.full((1, tq), -jnp.inf, F32) for _ in range(N_HEADS))

    @pl.when(blk < n_blocks - 1)
    def _():
        build_qm(qin_ref)
        sweep(n_keys, functools.partial(attend, score_next=True), m_init)
        score_rows(pl.multiple_of(n_keys, kt // 2), tq, nxt, limit_next, wtn_ref)

    @pl.when(blk == n_blocks - 1)
    def _():
        sweep(n_keys, functools.partial(attend, score_next=False), m_init)

    for h in range(N_HEADS):
        acc = acc_refs[h][...]
        out_t = acc[:HEAD_DIM, :] * (1.0 / acc[HEAD_DIM:HEAD_DIM + 1, :])
        o_ref[:, heads[h]] = out_t.T


def _dsa_attn(q, k, vt, qi, ki, wt, bsz, seq, tq=256, kt=DSA_KEY_TILE):
    n = bsz * seq
    nq = seq // tq
    top_k = min(TOPK_MAX, seq // 4)
    once = dict(pipeline_mode=pl.Buffered(1))
    nxt_blk = lambda b, i: b * nq + jnp.minimum(i + 1, nq - 1)
    return pl.pallas_call(
        functools.partial(_dsa_attn_kernel, tq=tq, kt=kt, top_k=top_k, n_blocks=nq),
        out_shape=jax.ShapeDtypeStruct((n, WIDTH), F32),
        grid=(bsz, nq),
        in_specs=[pl.BlockSpec((tq, IDX_HEADS * IDX_DIM), lambda b, i: (b * nq + i, 0)),
                  pl.BlockSpec((IDX_HEADS, tq), lambda b, i: (0, b * nq + i)),
                  pl.BlockSpec((tq, IDX_HEADS * IDX_DIM), lambda b, i: (nxt_blk(b, i), 0)),
                  pl.BlockSpec((IDX_HEADS, tq), lambda b, i: (0, nxt_blk(b, i))),
                  pl.BlockSpec((tq, WIDTH), lambda b, i: (b * nq + i, 0)),
                  pl.BlockSpec((seq, LANES), lambda b, i: (b, 0), **once),
                  pl.BlockSpec((seq, WIDTH), lambda b, i: (b, 0), **once),
                  pl.BlockSpec((seq // kt, N_HEADS * VT_ROWS, kt), lambda b, i: (b, 0, 0), **once)],
        out_specs=pl.BlockSpec((tq, WIDTH), lambda b, i: (b * nq + i, 0)),
        scratch_shapes=[pltpu.VMEM((IDX_HEADS * tq, LANES), BF16),
                        pltpu.VMEM((2, seq, tq), I16), pltpu.VMEM((2, seq, tq), I16), pltpu.VMEM((seq, tq), I16),
                        pltpu.VMEM((pl.cdiv(seq // POOL_ROWS * 2 * PACK, kt) * kt, tq), I16),
                        pltpu.VMEM((2, N_HEADS, kt, tq), F32)]
                       + [pltpu.VMEM((VT_ROWS, tq), F32) for _ in range(N_HEADS)],
        compiler_params=_params("arbitrary", "arbitrary"),
        name="dsa_attn",
    )(qi, wt, qi, wt, q, ki, k, vt)


def _mem_kv_kernel(mem_ref, g_ref, wk_ref, wv_ref, gk_ref, k_out, v_out):
    mb = _rms(mem_ref[...], g_ref[...]).astype(BF16)
    kf = _dot(mb, wk_ref[...])
    for h in range(N_HEADS):
        sl = slice(h * HEAD_DIM, (h + 1) * HEAD_DIM)
        k_out[:, sl] = _rms_head(kf[:, sl], gk_ref[...]).astype(BF16)
    v_out[...] = _dot(mb, wv_ref[...]).astype(BF16)


def _mem_kv(mem2, gain, wk, wv, gk, n_mem):
    n, d = mem2.shape
    full = lambda a: pl.BlockSpec(a.shape, lambda b: (0,) * a.ndim)
    return pl.pallas_call(
        _mem_kv_kernel,
        out_shape=(jax.ShapeDtypeStruct((n, WIDTH), BF16), jax.ShapeDtypeStruct((n, WIDTH), BF16)),
        grid=(n // n_mem,),
        in_specs=[pl.BlockSpec((n_mem, d), lambda b: (b, 0)), full(gain), full(wk), full(wv), full(gk)],
        out_specs=(pl.BlockSpec((n_mem, WIDTH), lambda b: (b, 0)),
                   pl.BlockSpec((n_mem, WIDTH), lambda b: (b, 0))),
        compiler_params=_params("parallel"),
        name="mem_kv",
    )(mem2, gain, wk, wv, gk)


def _xattn_kernel(x_ref, oa_ref, ob_ref, wout_ref, g_ref, wq_ref, gq_ref, km_ref, vm_ref, wo_ref, o_ref):
    x1 = (x_ref[...] + _dot(oa_ref[...].astype(BF16), wout_ref[:WIDTH, :])
          + _dot(ob_ref[...].astype(BF16), wout_ref[WIDTH:, :]))
    hb = _rms(x1, g_ref[...]).astype(BF16)
    qf = _dot(hb, wq_ref[...])
    scale = HEAD_DIM ** -0.5
    outs = []
    for h in range(N_HEADS):
        sl = slice(h * HEAD_DIM, (h + 1) * HEAD_DIM)
        qh = (_rms(qf[:, sl], gq_ref[...]) * scale).astype(BF16)
        logits = _dot_nt(qh, km_ref[:, sl])
        p = jnp.exp(logits - jnp.max(logits, axis=-1, keepdims=True))
        oh = _dot(p.astype(BF16), vm_ref[:, sl])
        outs.append((oh * (1.0 / jnp.sum(p, axis=-1, keepdims=True))).astype(BF16))
    o_ref[...] = x1 + _dot(jnp.concatenate(outs, axis=1), wo_ref[...])


def _xattn(x2, oa, ob, wout, gain, wq, gq, km, vm, wo, seq, n_mem, tm=1024):
    n, d = x2.shape
    per_b = seq // tm
    row = lambda w: pl.BlockSpec((tm, w), lambda i: (i, 0))
    full = lambda a: pl.BlockSpec(a.shape, lambda i: (0,) * a.ndim)
    memb = pl.BlockSpec((n_mem, WIDTH), lambda i: (i // per_b, 0))
    return pl.pallas_call(
        _xattn_kernel,
        out_shape=jax.ShapeDtypeStruct((n, d), F32),
        grid=(n // tm,),
        in_specs=[row(d), row(WIDTH), row(WIDTH), full(wout), full(gain), full(wq), full(gq),
                  memb, memb, full(wo)],
        out_specs=row(d),
        compiler_params=_params("parallel"),
        name="xattn",
    )(x2, oa, ob, wout, gain, wq, gq, km, vm, wo)


def _moe_kernel(x_ref, g_ref, wr_hi_ref, wr_lo_ref, br_ref, wgu_ref, wdn_ref, o_ref, hs_ref, gs_ref, ys_ref,
                *, tm, sub):
    lane = lax.broadcasted_iota(I32, (tm, LANES), 1)
    lane_f = lane.astype(F32)
    x = x_ref[...]
    h = _rms(x, g_ref[...])
    h_hi = h.astype(BF16)
    h_lo = (h - h_hi.astype(F32)).astype(BF16)
    lg = (_dot(h_hi, wr_hi_ref[...]) + _dot(h_lo, wr_hi_ref[...]) + _dot(h_hi, wr_lo_ref[...])
          + br_ref[...])
    first = lambda cond: jnp.min(jnp.where(cond, lane_f, 1e9), axis=-1, keepdims=True)
    gl = jnp.where(lane < N_GROUPS, lg, -jnp.inf)
    gmax = jnp.max(gl, axis=-1, keepdims=True)
    gsel = first(gl == gmax)
    g_w = 1.0 / jnp.sum(jnp.exp(gl - gmax), axis=-1, keepdims=True)
    grp_of_lane = ((lane - N_GROUPS) >> (EXP_PER_GROUP.bit_length() - 1)).astype(F32)
    in_grp = (lane >= N_GROUPS) & (lane < N_GROUPS + N_EXPERTS) & (grp_of_lane == gsel)
    el = jnp.where(in_grp, lg, -jnp.inf)
    v1 = jnp.max(el, axis=-1, keepdims=True)
    i1 = first(el == v1)
    el2 = jnp.where(lane_f == i1, -jnp.inf, el)
    v2 = jnp.max(el2, axis=-1, keepdims=True)
    i2 = first(el2 == v2)
    r = jnp.exp(v2 - v1)
    w1 = 1.0 / (1.0 + r)
    gate = jnp.where(lane_f == i1, w1 * g_w, jnp.where(lane_f == i2, r * w1 * g_w, 0.0))

    onehot = jnp.where(lane_f == gsel, 1.0, 0.0)
    t_row = lax.broadcasted_iota(I32, (tm, tm), 0)
    t_col = lax.broadcasted_iota(I32, (tm, tm), 1)
    before = _dot(jnp.where(t_row > t_col, 1.0, 0.0).astype(BF16), onehot.astype(BF16))
    counts = jnp.sum(onehot, axis=0, keepdims=True)
    lane1 = lax.broadcasted_iota(I32, (1, LANES), 1)
    ends = []
    run = jnp.zeros((1, 1), F32)
    for g in range(N_GROUPS - 1):
        run = run + jnp.sum(jnp.where(lane1 == g, counts, 0.0), axis=-1, keepdims=True)
        ends.append(run)
    start_of = sum(jnp.where(lane1 == g + 1, ends[g], 0.0) for g in range(N_GROUPS - 1))
    pos = jnp.sum(onehot * (before + start_of), axis=-1, keepdims=True)
    to_sorted_t = jnp.where(pos == t_col.astype(F32), 1.0, 0.0)
    to_sorted = to_sorted_t.T.astype(BF16)
    hs_ref[...] = _dot(to_sorted, h_hi).astype(BF16)
    g_hi, g_mid, g_lo = _split3(gate)
    gs_ref[...] = _dot(to_sorted, g_hi) + _dot(to_sorted, g_mid) + _dot(to_sorted, g_lo)

    bounds = [e_[0, 0].astype(I32) for e_ in ends]
    lane_s = lax.broadcasted_iota(I32, (sub, LANES), 1)

    def slab(j, carry):
        r0 = j * sub
        rows = pl.ds(pl.multiple_of(r0, sub), sub)
        g_first = sum((b <= r0).astype(I32) for b in bounds)
        g_last = sum((b <= r0 + sub - 1).astype(I32) for b in bounds)
        hs = hs_ref[rows, :]
        gs = gs_ref[rows, :]
        ys_ref[rows, :] = jnp.zeros((sub, ys_ref.shape[1]), F32)

        def group(g, c):
            gu = _dot(hs, wgu_ref[g])
            acts = []
            for e in range(EXP_PER_GROUP):
                ge = jnp.sum(jnp.where(lane_s == g * EXP_PER_GROUP + e + N_GROUPS, gs, 0.0), axis=-1, keepdims=True)
                up = gu[:, 2 * e * D_EXPERT:(2 * e + 1) * D_EXPERT]
                acts.append((up * jax.nn.sigmoid(up) * gu[:, (2 * e + 1) * D_EXPERT:(2 * e + 2) * D_EXPERT] * ge).astype(BF16))
            ys_ref[rows, :] += _dot(jnp.concatenate(acts, axis=1), wdn_ref[g])
            return c

        lax.fori_loop(g_first, g_last + 1, group, 0)
        return carry

    lax.fori_loop(0, tm // sub, slab, 0)
    o_ref[...] = x + _dot(to_sorted_t.astype(BF16), ys_ref[...].astype(BF16))


def _moe(x2, gain, wr_hi, wr_lo, br, wgu, wdn, tm=512, sub=128):
    n, d = x2.shape
    full = lambda a: pl.BlockSpec(a.shape, lambda i: (0,) * a.ndim)
    once = lambda a: pl.BlockSpec(a.shape, lambda i: (0,) * a.ndim, pipeline_mode=pl.Buffered(1))
    return pl.pallas_call(
        functools.partial(_moe_kernel, tm=tm, sub=sub),
        out_shape=jax.ShapeDtypeStruct((n, d), F32),
        grid=(n // tm,),
        in_specs=[pl.BlockSpec((tm, d), lambda i: (i, 0)), full(gain), full(wr_hi), full(wr_lo), full(br),
                  once(wgu), once(wdn)],
        out_specs=pl.BlockSpec((tm, d), lambda i: (i, 0)),
        scratch_shapes=[pltpu.VMEM((tm, d), BF16), pltpu.VMEM((tm, LANES), F32), pltpu.VMEM((tm, d), F32)],
        compiler_params=_params("parallel"),
        name="moe",
    )(x2, gain, wr_hi, wr_lo, br, wgu, wdn)


def _rope_tables(positions):
    pos = positions.reshape(-1).astype(F32)[:, None]

    def cs(dim):
        inv = ROPE_THETA ** (-jnp.arange(0, dim, 2, dtype=F32) / dim)
        ang = pos * inv
        return jnp.cos(ang), jnp.sin(ang)

    c, s = cs(HEAD_DIM)
    ci, si = cs(IDX_DIM)
    return jnp.concatenate([c, s], 1), jnp.concatenate([ci, si, jnp.zeros_like(ci), jnp.zeros_like(ci)], 1)


def _reorder_in_weight(w):
    a = w[:, :4 * WIDTH]
    o = 4 * WIDTH
    qlat = w[:, o:o + Q_LORA]; o += Q_LORA
    k = w[:, o:o + WIDTH]; o += WIDTH
    v = w[:, o:o + WIDTH]; o += WIDTH
    misc = w[:, o:]
    pad = jnp.zeros((w.shape[0], LANES - misc.shape[1]), w.dtype)
    return jnp.concatenate([a, k, v, qlat, misc, pad], axis=1).astype(BF16)


def _row(v):
    return v.reshape(1, -1).astype(F32)


def kernel(x, mem, positions, norm_mix, w_in, lb_raw, a_gnorm, b_qlat_gain, b_wqb, b_wqidx, b_qnorm,
           b_knorm, b_kidx_norm, w_out, norm_x, norm_mem, x_wq, x_wk, x_wv, x_wo, x_qnorm, x_knorm,
           norm_ffn, w_rg, b_rg, w_re, b_re, w_gu, w_dn):
    bsz, seq, d = x.shape
    n_mem = mem.shape[1]
    depth = w_in.shape[0]
    assert d == D_MODEL and seq % (2 * DSA_KEY_TILE) == 0 and n_mem % LANES == 0, (x.shape, mem.shape)
    x2 = x.reshape(bsz * seq, d)
    mem2 = mem.reshape(bsz * n_mem, d)
    tabs = _rope_tables(positions)
    pad_lanes = lambda v: jnp.pad(v, ((0, 0), (0, LANES - v.shape[1])))
    for l in range(depth):
        u = _in_proj(x2, _row(norm_mix[l]), _reorder_in_weight(w_in[l]))
        o_a = _hgrn(u, lb_raw.astype(F32), _row(a_gnorm[l]), l, bsz, seq)
        q, k, vt, qi, ki, wt = _dsa_prep(
            u, tabs, b_wqb[l].astype(BF16), b_wqidx[l].astype(BF16), _row(b_qlat_gain[l]),
            _row(b_qnorm[l]), _row(b_knorm[l]), pad_lanes(_row(b_kidx_norm[l])))
        o_b = _dsa_attn(q, k, vt, qi, ki, wt, bsz, seq)
        km, vm = _mem_kv(mem2, _row(norm_mem[l]), x_wk[l].astype(BF16), x_wv[l].astype(BF16),
                         _row(x_knorm[l]), n_mem)
        x2 = _xattn(x2, o_a, o_b, w_out[l].astype(BF16), _row(norm_x[l]), x_wq[l].astype(BF16),
                    _row(x_qnorm[l]), km, vm, x_wo[l].astype(BF16), seq, n_mem)
        wr = pad_lanes(jnp.concatenate([w_rg[l], w_re[l].reshape(d, N_EXPERTS)], axis=1).astype(F32))
        wr_hi = wr.astype(BF16)
        wr_lo = (wr - wr_hi.astype(F32)).astype(BF16)
        br = pad_lanes(jnp.concatenate([b_rg[l], b_re[l].reshape(-1)]).reshape(1, -1).astype(F32))
        x2 = _moe(x2, _row(norm_ffn[l]), wr_hi, wr_lo, br,
                  w_gu[l].transpose(0, 2, 1, 3).reshape(N_GROUPS, d, EXP_PER_GROUP * 2 * D_EXPERT).astype(BF16),
                  w_dn[l].reshape(N_GROUPS, EXP_PER_GROUP * D_EXPERT, d).astype(BF16))
    return x2.reshape(bsz, seq, d)
```

```python
import functools

import numpy as np
import jax
import jax.numpy as jnp
from jax import lax
from jax.experimental import pallas as pl
from jax.experimental.pallas import tpu as pltpu

F32 = jnp.float32
BF16 = jnp.bfloat16
I32 = jnp.int32
I16 = jnp.int16

EPS = 1e-6
ROPE_THETA = 10000.0
CHUNK = 64
CHUNK_SHIFT = CHUNK.bit_length() - 1
SUB = 16
SUB_SHIFT = SUB.bit_length() - 1
N_HEADS = 4
HEAD_DIM = 128
WIDTH = N_HEADS * HEAD_DIM
Q_LORA = 256
D_MODEL = 1024
IDX_HEADS = 16
IDX_DIM = 64
ROT_I = IDX_DIM // 2
TOPK_MAX = 256
N_GROUPS = 4
EXP_PER_GROUP = 4
N_EXPERTS = N_GROUPS * EXP_PER_GROUP
D_EXPERT = 256
LANES = 128
INT_MIN = -2 ** 31
INT16_MIN, INT16_MAX = -2 ** 15, 2 ** 15 - 1
HALF_BITS = 16
PACK = 16
LOG2E = 1.4426950408889634
VT_ROWS = HEAD_DIM + PACK
DSA_KEY_TILE = 512
SCORE_ROWS = 128
POOL_ROWS = 256
SURE_PASSES = 8

U_K = 4 * WIDTH
U_V = U_K + WIDTH
U_QLAT = U_V + WIDTH
U_MISC = U_QLAT + Q_LORA
U_COLS = U_MISC + LANES

VMEM_LIMIT = 56 * 1024 * 1024


def _rms(x, gain):
    return x * lax.rsqrt(jnp.mean(x * x, axis=-1, keepdims=True) + EPS) * gain


def _dot(a, b):
    return jnp.dot(a, b, preferred_element_type=F32)


def _rms_head(x, gain, width=HEAD_DIM):
    ones = (lax.broadcasted_iota(I32, (LANES, LANES), 0) < width).astype(BF16)
    ss = _dot((x * x).astype(BF16), ones)
    return x * lax.rsqrt(ss * (1.0 / width) + EPS) * gain


def _dot_nt(a, b):
    return lax.dot_general(a, b, (((1,), (1,)), ((), ())), preferred_element_type=F32)


def _params(*sem):
    return pltpu.CompilerParams(dimension_semantics=sem, vmem_limit_bytes=VMEM_LIMIT)


def _in_proj_kernel(x_ref, g_ref, w_ref, u_ref, *, col_chunk):
    hb = _rms(x_ref[...], g_ref[...]).astype(BF16)
    for c0 in range(0, U_COLS, col_chunk):
        u_ref[:, c0:c0 + col_chunk] = _dot(hb, w_ref[:, c0:c0 + col_chunk])


def _in_proj(x2, gain, w_bf, tm=1024):
    n, d = x2.shape
    return pl.pallas_call(
        functools.partial(_in_proj_kernel, col_chunk=1152),
        out_shape=jax.ShapeDtypeStruct((n, U_COLS), F32),
        grid=(n // tm,),
        in_specs=[pl.BlockSpec((tm, d), lambda i: (i, 0)),
                  pl.BlockSpec((1, d), lambda i: (0, 0)),
                  pl.BlockSpec((d, U_COLS), lambda i: (0, 0), pipeline_mode=pl.Buffered(1))],
        out_specs=pl.BlockSpec((tm, U_COLS), lambda i: (i, 0)),
        compiler_params=_params("parallel"),
        name="in_proj",
    )(x2, gain, w_bf)


def _split3(x):
    hi = x.astype(BF16)
    r1 = x - hi.astype(F32)
    mid = r1.astype(BF16)
    lo = (r1 - mid.astype(F32)).astype(BF16)
    return hi, mid, lo


def _hgrn_kernel(q_ref, f_ref, i_ref, g_ref, lbraw_ref, gn_ref, e_ref, o_ref, st_ref,
                 *, layer, n_chunks, unroll):
    @pl.when(pl.program_id(2) == 0)
    def _():
        st_ref[...] = jnp.zeros_like(st_ref)

    lr = lbraw_ref[...]
    ex = jnp.exp(lr - jnp.max(lr, axis=0, keepdims=True))
    sm = ex / jnp.sum(ex, axis=0, keepdims=True)
    lb = jnp.zeros((1, HEAD_DIM), F32)
    for r in range(1, layer + 1):
        lb = lb + sm[r:r + 1, :]
    log_lb = jnp.log(lb)
    log_1mlb = jnp.log(1.0 - lb)

    row = lax.broadcasted_iota(I32, (CHUNK, HEAD_DIM), 0)
    col = lax.broadcasted_iota(I32, (CHUNK, HEAD_DIM), 1)
    row_l = row & (SUB - 1)
    row_b = row >> SUB_SHIFT
    col_b = col >> SUB_SHIFT
    tri = (lax.broadcasted_iota(I32, (CHUNK, CHUNK), 0)
           >= lax.broadcasted_iota(I32, (CHUNK, CHUNK), 1)).astype(BF16)
    n_sub = CHUNK // SUB
    zeros_c = jnp.zeros((CHUNK, HEAD_DIM), F32)

    def sub_bcast(t, s):
        t4 = t.reshape(n_sub, SUB, HEAD_DIM)
        return jnp.broadcast_to(t4[:, s:s + 1, :], (n_sub, SUB, HEAD_DIM)).reshape(CHUNK, HEAD_DIM)

    def gates(z):
        ls = jnp.minimum(z, 0.0) - jnp.log(1.0 + jnp.exp(-jnp.abs(z)))
        if layer == 0:
            return ls, (ls - z) * LOG2E
        cc = log_1mlb + ls
        log_f = jnp.maximum(log_lb, cc) + jnp.log(1.0 + jnp.exp(-jnp.abs(log_lb - cc)))
        return log_f, (cc - z) * LOG2E

    def diag_terms(q, b2, c2):
        pieces = []
        for s in range(SUB):
            d = jnp.where(row_l >= s, b2 - sub_bcast(c2, s), -jnp.inf)
            pieces.append((q * jnp.exp2(d)).astype(BF16))
        return jnp.concatenate(pieces, axis=1)

    def below_keys(b2, c2):
        kts = []
        for i in range(1, n_sub):
            r_i = b2[i * SUB:i * SUB + 1, :]
            kts.append(jnp.exp2(jnp.where(row < i * SUB, r_i - c2, -jnp.inf)))
            kts.append(zeros_c)
        return jnp.concatenate(kts, axis=0).astype(BF16)

    def below_scores(r):
        a = jnp.zeros((CHUNK, HEAD_DIM), F32)
        for i in range(1, n_sub):
            a = a + jnp.where(row_b == i, r[:, (i - 1) * LANES:i * LANES], 0.0)
        return a

    def chunks(it, carry):
        cs = range(unroll)
        rows = [pl.ds(pl.multiple_of((it * unroll + c) * CHUNK, CHUNK), CHUNK) for c in cs]
        qr = [q_ref[r, :] for r in rows]
        v = [i_ref[r, :] for r in rows]
        q = [x * jax.nn.sigmoid(x) for x in qr]
        gt = [gates(f_ref[r, :]) for r in rows]
        b2 = []
        for c in cs:
            hi, mid, lo = _split3(gt[c][0])
            b2.append((_dot(tri, hi) + _dot(tri, mid) + _dot(tri, lo)) * LOG2E)
        c2 = [b2[c] - gt[c][1] for c in cs]
        b_last = [x[CHUNK - 1:CHUNK, :] for x in b2]
        a2 = [_dot(diag_terms(q[c], b2[c], c2[c]), e_ref[...]) for c in cs]
        rr = [_dot_nt((q[c] * jnp.exp2(b2[c] - sub_bcast(b2[c], 0))).astype(BF16), below_keys(b2[c], c2[c]))
              for c in cs]
        upd = [_dot(v[c].T.astype(BF16), jnp.exp2(b_last[c] - c2[c]).astype(BF16)) for c in cs]
        qe = [(q[c] * jnp.exp2(b2[c])).astype(BF16) for c in cs]
        st = st_ref[...]
        o_inter = []
        for c in cs:
            o_inter.append(_dot_nt(qe[c], st.astype(BF16)))
            st = st * jnp.exp2(b_last[c]) + upd[c]
        st_ref[...] = st
        a = [jnp.where(col_b == row_b, a2[c], 0.0) + below_scores(rr[c]) for c in cs]
        o_intra = [_dot(a[c][:, :CHUNK].astype(BF16), v[c].astype(BF16)) for c in cs]
        for c in cs:
            g = g_ref[rows[c], :]
            o_ref[rows[c], :] = _rms(o_inter[c] + o_intra[c], gn_ref[...]) * (g * jax.nn.sigmoid(g))
        return carry

    lax.fori_loop(0, n_chunks // unroll, chunks, 0)


def _hgrn_selector():
    e = np.zeros((SUB * HEAD_DIM, LANES), np.float32)
    s_of_row = np.arange(SUB * HEAD_DIM) // HEAD_DIM
    cols = np.arange(LANES)
    e[:, :] = ((cols[None, :] % SUB) == s_of_row[:, None]) & (cols[None, :] < CHUNK)
    return jnp.asarray(e, BF16)


def _hgrn(u, lb_raw, gn, layer, bsz, seq, tb=1024):
    n = bsz * seq
    nb = seq // tb
    blk = lambda k: pl.BlockSpec((tb, HEAD_DIM), lambda b, h, j, k=k: (b * nb + j, h + N_HEADS * k))
    n_layers = lb_raw.shape[0]
    return pl.pallas_call(
        functools.partial(_hgrn_kernel, layer=layer, n_chunks=tb // CHUNK, unroll=16),
        out_shape=jax.ShapeDtypeStruct((n, WIDTH), F32),
        grid=(bsz, N_HEADS, nb),
        in_specs=[blk(0), blk(1), blk(2), blk(3),
                  pl.BlockSpec((n_layers, HEAD_DIM), lambda b, h, j: (0, h)),
                  pl.BlockSpec((1, HEAD_DIM), lambda b, h, j: (0, 0)),
                  pl.BlockSpec((SUB * HEAD_DIM, LANES), lambda b, h, j: (0, 0))],
        out_specs=pl.BlockSpec((tb, HEAD_DIM), lambda b, h, j: (b * nb + j, h)),
        scratch_shapes=[pltpu.VMEM((HEAD_DIM, HEAD_DIM), F32)],
        compiler_params=_params("parallel", "parallel", "arbitrary"),
        name="hgrn",
    )(u, u, u, u, lb_raw, gn, _hgrn_selector())


def _rope128(x, cos, sin_signed):
    return x * cos + pltpu.roll(x, HEAD_DIM // 2, 1) * sin_signed


def _rope64(x, cos, sin_a, sin_b):
    return x * cos + pltpu.roll(x, LANES - ROT_I, 1) * sin_a + pltpu.roll(x, ROT_I, 1) * sin_b


def _dsa_prep_kernel(k_ref, v_ref, ql_ref, misc_ref, ta_ref, tb_ref,
                     wqb_ref, wqi_ref, gql_ref, gqn_ref, gkn_ref, gki_ref,
                     q_out, k_out, vt_out, qi_out, ki_out, wt_out):
    ta, tb = ta_ref[...], tb_ref[...]
    lane = lax.broadcasted_iota(I32, ta.shape, 1)
    ta_swapped = pltpu.roll(ta, HEAD_DIM // 2, 1)
    cm = jnp.where(lane < HEAD_DIM // 2, ta, ta_swapped)
    sm = jnp.where(lane < HEAD_DIM // 2, -ta_swapped, ta)
    quarter = lane >> (ROT_I.bit_length() - 1)
    r32, r64, r96 = (pltpu.roll(tb, sh * ROT_I, 1) for sh in (1, 2, 3))
    ci = jnp.where(quarter == 0, tb, jnp.where(quarter == 1, r32, jnp.where(quarter == 2, r64, r96)))
    si = jnp.where(quarter == 0, r96, jnp.where(quarter == 1, tb, jnp.where(quarter == 2, r32, r64)))
    sia = jnp.where((lane & ROT_I) == 0, -si, 0.0)
    sib = jnp.where((lane & ROT_I) != 0, si, 0.0)
    cb = _rms(ql_ref[...], gql_ref[...]).astype(BF16)
    qf = _dot(cb, wqb_ref[...])
    scale = HEAD_DIM ** -0.5
    for h in range(N_HEADS):
        sl = slice(h * HEAD_DIM, (h + 1) * HEAD_DIM)
        qh = _rope128(_rms_head(qf[:, sl], gqn_ref[...]), cm, sm)
        q_out[:, sl] = (qh * (scale * LOG2E)).astype(BF16)
        kh = _rope128(_rms_head(k_ref[:, sl], gkn_ref[...]), cm, sm)
        k_out[:, sl] = kh.astype(BF16)
    vt = v_ref[...].T.astype(BF16)
    ones = jnp.ones((PACK, vt.shape[1]), BF16)
    for h in range(N_HEADS):
        vt_out[0, h * VT_ROWS:h * VT_ROWS + HEAD_DIM, :] = vt[h * HEAD_DIM:(h + 1) * HEAD_DIM, :]
        vt_out[0, h * VT_ROWS + HEAD_DIM:(h + 1) * VT_ROWS, :] = ones
    qi = _dot(cb, wqi_ref[...])
    for p in range(IDX_HEADS * IDX_DIM // LANES):
        sl = slice(p * LANES, (p + 1) * LANES)
        qi_out[:, sl] = _rope64(qi[:, sl], ci, sia, sib).astype(BF16)
    misc = misc_ref[...]
    kn = _rope64(_rms_head(misc, gki_ref[...], IDX_DIM), ci, sia, sib)
    ki_out[...] = (kn + pltpu.roll(kn, IDX_DIM, 1)).astype(BF16)
    wt = (misc * (IDX_HEADS ** -0.5 * IDX_DIM ** -0.5)).T
    wt_out[...] = wt[IDX_DIM:IDX_DIM + IDX_HEADS, :]


def _dsa_prep(u, tabs, wqb, wqi, gql, gqn, gkn, gki, tm=DSA_KEY_TILE):
    n = u.shape[0]
    nt = n // tm
    ublk = lambda width, col: pl.BlockSpec((tm, width), lambda i: (i, col // width))
    tab = pl.BlockSpec((tm, LANES), lambda i: (i, 0))
    full = lambda a: pl.BlockSpec(a.shape, lambda i: (0,) * a.ndim)
    return pl.pallas_call(
        _dsa_prep_kernel,
        out_shape=(jax.ShapeDtypeStruct((n, WIDTH), BF16),
                   jax.ShapeDtypeStruct((n, WIDTH), BF16),
                   jax.ShapeDtypeStruct((nt, N_HEADS * VT_ROWS, tm), BF16),
                   jax.ShapeDtypeStruct((n, IDX_HEADS * IDX_DIM), BF16),
                   jax.ShapeDtypeStruct((n, LANES), BF16),
                   jax.ShapeDtypeStruct((IDX_HEADS, n), F32)),
        grid=(nt,),
        in_specs=[ublk(WIDTH, U_K), ublk(WIDTH, U_V), ublk(Q_LORA, U_QLAT), ublk(LANES, U_MISC),
                  tab, tab,
                  full(wqb), full(wqi), full(gql), full(gqn), full(gkn), full(gki)],
        out_specs=(pl.BlockSpec((tm, WIDTH), lambda i: (i, 0)),
                   pl.BlockSpec((tm, WIDTH), lambda i: (i, 0)),
                   pl.BlockSpec((1, N_HEADS * VT_ROWS, tm), lambda i: (i, 0, 0)),
                   pl.BlockSpec((tm, IDX_HEADS * IDX_DIM), lambda i: (i, 0)),
                   pl.BlockSpec((tm, LANES), lambda i: (i, 0)),
                   pl.BlockSpec((IDX_HEADS, tm), lambda i: (0, i))),
        compiler_params=_params("parallel"),
        name="dsa_prep",
    )(u, u, u, u, *tabs, wqb, wqi, gql, gqn, gkn, gki)


def _sortable(x):
    bits = lax.bitcast_convert_type(x, I32)
    return bits ^ ((bits >> 31) & 0x7FFFFFFF)


def _dsa_attn_kernel(qi_ref, wt_ref, qin_ref, wtn_ref, q_ref, ki_ref, k_ref, vt_ref, o_ref,
                     qm_ref, hi_ref, lo_ref, kb_ref, pk_ref, lt_ref, acc0, acc1, acc2, acc3,
                     *, tq, kt, top_k, n_blocks):
    acc_refs = (acc0, acc1, acc2, acc3)
    blk = pl.program_id(1)
    cur = blk & 1
    nxt = 1 - cur
    q0 = blk * tq
    n_keys = q0 + tq
    sub = SCORE_ROWS
    key_iota = lax.broadcasted_iota(I32, (sub, tq), 0)
    lane_q = lax.broadcasted_iota(I32, (tq, LANES), 1)
    pooled_per_group = 2 * PACK

    def sweep(total, fn, carry, unit=2):
        big = unit * kt
        carry = lax.fori_loop(0, total // big, lambda j, c: fn(pl.multiple_of(j * big, big), big, c), carry)
        size = big // 2
        while size >= kt // 2:
            start = pl.multiple_of(total // (2 * size) * (2 * size), size)
            carry = lax.cond((total // size) % 2 == 1, functools.partial(fn, start, size), lambda c: c, carry)
            size //= 2
        return carry

    def limit_of(q_start):
        qpos = q_start + lax.broadcasted_iota(I32, (1, tq), 1)
        return ((qpos >> CHUNK_SHIFT) + 1) << CHUNK_SHIFT

    def build_qm(src_ref):
        for h in range(IDX_HEADS):
            src = src_ref[:, (h // 2) * LANES:(h // 2 + 1) * LANES].astype(F32)
            keep = (lane_q < IDX_DIM) if h % 2 == 0 else (lane_q >= IDX_DIM)
            qm_ref[h * tq:(h + 1) * tq, :] = jnp.where(keep, src, 0.0).astype(BF16)

    def score_rows(start, size, slot, limit, w_ref):
        for part in range(size // sub):
            base = start + part * sub
            rows = pl.ds(pl.multiple_of(base, sub), sub)
            ki = ki_ref[rows, :]
            acc = jnp.zeros((sub, tq), F32)
            for h in range(IDX_HEADS):
                x = _dot_nt(ki, qm_ref[h * tq:(h + 1) * tq, :])
                acc = acc + w_ref[h:h + 1, :] * jnp.maximum(x, 0.0)
            key = jnp.where(base + key_iota < limit, _sortable(acc), INT_MIN)
            hi_ref[slot, rows, :] = (key >> HALF_BITS).astype(I16)
            lo_ref[slot, rows, :] = ((key & (2 ** HALF_BITS - 1)) + INT16_MIN).astype(I16)

    @pl.when(blk == 0)
    def _():
        build_qm(qi_ref)
        limit0 = limit_of(0)

        def body(start, size, carry):
            score_rows(start, size, 0, limit0, wt_ref)
            return carry
        sweep(n_keys, body, 0)

    one, zero = jnp.ones((), BF16), jnp.zeros((), BF16)
    neg_inf = jnp.full((), -jnp.inf, BF16)
    hi_rows = lambda rows: hi_ref[cur, rows, :]
    lo_rows = lambda rows: lo_ref[cur, rows, :]
    kb_rows = lambda rows: kb_ref[rows, :]

    def count16(get, trial, strict, total=None):
        t16 = jnp.broadcast_to(trial.astype(I16), (PACK, tq))

        def hits(start, size, c):
            x = get(pl.ds(start, size))
            parts = [jnp.zeros((PACK, tq), BF16) for _ in range(4)]
            for r in range(size // PACK):
                xr = x[r * PACK:(r + 1) * PACK, :]
                parts[r % 4] = parts[r % 4] + jnp.where((xr > t16) if strict else (xr >= t16), one, zero)
            return c + ((parts[0] + parts[1]) + (parts[2] + parts[3])).astype(F32)

        c = sweep(n_keys if total is None else total, hits, jnp.zeros((PACK, tq), F32), unit=4)
        return jnp.sum(c, axis=0, keepdims=True)

    def bisect16(get, want, n_all, early_exit, total=None):
        def step(it, carry):
            lo, n_lo, n_up = carry
            trial = lo + lax.shift_left(jnp.int32(1), HALF_BITS - 1 - it)
            n = count16(get, trial, False, total)
            ok = n >= want
            return jnp.where(ok, trial, lo), jnp.where(ok, n, n_lo), jnp.where(ok, n_up, n)

        init = (jnp.full((1, tq), INT16_MIN, I32), n_all, jnp.zeros((1, tq), F32))
        if not early_exit:
            return lax.fori_loop(0, HALF_BITS, step, init)

        def unresolved(carry):
            it, _, n_lo, _ = carry
            return (it < HALF_BITS) & (jnp.max(jnp.where(n_lo != want, 1.0, 0.0)) > 0.0)

        def two_steps(carry):
            return (carry[0] + 2,) + step(carry[0] + 1, step(carry[0], carry[1:]))

        return lax.while_loop(unresolved, two_steps, (jnp.int32(SURE_PASSES),) + lax.fori_loop(0, SURE_PASSES, step, init))[1:]

    tau_hi, n_hi_ge, n_hi_gt = bisect16(hi_rows, float(top_k), jnp.full((1, tq), n_keys.astype(F32), F32), False)
    want_lo = top_k - n_hi_gt
    tau_hi_row = tau_hi.astype(I16)

    min16 = jnp.full((PACK, tq), INT16_MIN, I16)

    def bucket_rows(start, size, carry):
        rows = pl.ds(start, size)
        kb = jnp.where(hi_rows(rows) == jnp.broadcast_to(tau_hi_row, (size, tq)), lo_rows(rows),
                       jnp.full((), INT16_MIN, I16))
        kb_ref[rows, :] = kb
        for g in range(size // POOL_ROWS):
            top1, top2 = min16, min16
            for r in range(POOL_ROWS // PACK):
                x = kb[g * POOL_ROWS + r * PACK:g * POOL_ROWS + (r + 1) * PACK, :]
                below = x < top1
                runner = jnp.where(below, x, top1)
                top2 = jnp.where(runner > top2, runner, top2)
                top1 = jnp.where(below, top1, x)
            base = pl.multiple_of((start // POOL_ROWS + g) * pooled_per_group, pooled_per_group)
            pk_ref[pl.ds(base, PACK), :] = top1
            pk_ref[pl.ds(base + PACK, PACK), :] = top2
        return carry

    sweep(n_keys, bucket_rows, 0)
    n_groups = n_keys // POOL_ROWS
    pooled_total = (n_groups * pooled_per_group + kt // 2 - 1) // (kt // 2) * (kt // 2)

    def pad_pooled(g, carry):
        pk_ref[pl.ds(pl.multiple_of(g * pooled_per_group, pooled_per_group), pooled_per_group), :] = jnp.full(
            (pooled_per_group, tq), INT16_MIN, I16)
        return carry

    lax.fori_loop(n_groups, pooled_total // pooled_per_group, pad_pooled, 0)
    n_bucket = n_hi_ge - n_hi_gt
    pk_rows = lambda rows: pk_ref[rows, :]
    tau_p, _, _ = bisect16(pk_rows, want_lo, n_bucket, True, pooled_total)
    n_ge_p = jnp.where(tau_p == INT16_MIN, n_bucket, count16(kb_rows, tau_p, False))
    n_gt_p = count16(kb_rows, tau_p, True)
    pooled_ok = (n_ge_p == want_lo) | ((n_ge_p > want_lo) & (n_gt_p < want_lo))

    def full_search():
        t, n_ge, _ = bisect16(kb_rows, want_lo, n_bucket, True)
        return t, n_ge, count16(kb_rows, t, True)

    tau_lo, n_kb_ge, n_kb_gt = lax.cond(
        jnp.min(jnp.where(pooled_ok, 1.0, 0.0)) > 0.0,
        lambda: (tau_p, n_ge_p, n_gt_p), full_search)
    live = (tau_hi > INT16_MIN) | (tau_lo > INT16_MIN)
    excess = jnp.max(jnp.where(live, n_kb_ge - want_lo, 0.0))

    @pl.when(excess > 0)
    def _():
        need = want_lo - n_kb_gt
        live_f = jnp.where(live, 1.0, 0.0)
        step_rows = kt // 2
        tri = (lax.broadcasted_iota(I32, (step_rows, step_rows), 0)
               >= lax.broadcasted_iota(I32, (step_rows, step_rows), 1)).astype(BF16)

        def body(start, size, seen):
            for part in range(size // step_rows):
                rows = pl.ds(pl.multiple_of(start + part * step_rows, step_rows), step_rows)
                h32 = hi_rows(rows).astype(I32)
                eq = jnp.where(h32 == tau_hi, jnp.where(lo_rows(rows).astype(I32) == tau_lo, 1.0, 0.0), 0.0)
                rank = _dot(tri, eq.astype(BF16)) + seen
                dropped = eq * (1.0 - jnp.where(rank <= need, live_f, 0.0))
                hi_ref[cur, rows, :] = jnp.where(dropped > 0.0, INT16_MIN, h32).astype(I16)
                seen = rank[step_rows - 1:step_rows, :]
            return seen
        sweep(n_keys, body, jnp.zeros((1, tq), F32))

    tau_lo_row = jnp.where(live, tau_lo, INT16_MAX).astype(I16)

    def bias_rows(rows, size):
        h16 = hi_rows(rows)
        th = jnp.broadcast_to(tau_hi_row, (size, tq))
        in_bucket = jnp.where(lo_rows(rows) >= jnp.broadcast_to(tau_lo_row, (size, tq)), zero, neg_inf)
        return jnp.where(h16 > th, zero, jnp.where(h16 == th, in_bucket, neg_inf)).astype(F32)

    for a in acc_refs:
        a[...] = jnp.zeros_like(a)
    heads = [slice(h * HEAD_DIM, (h + 1) * HEAD_DIM) for h in range(N_HEADS)]
    limit_next = limit_of(q0 + tq)

    def attend(start, size, ms, score_next):
        segs = [(start, kt), (start + kt, kt)] if size == 2 * kt else [(start, size)]
        for i, (s0, n) in enumerate(segs):
            rows = pl.ds(pl.multiple_of(s0, kt // 2), n)
            bias = bias_rows(rows, n)
            for h in range(N_HEADS):
                lt_ref[i, h, :n, :] = _dot_nt(k_ref[rows, heads[h]], q_ref[:, heads[h]]) + bias
        ms = list(ms)
        for i, (s0, n) in enumerate(segs):
            if score_next:
                score_rows(s0, n, nxt, limit_next, wtn_ref)
            alphas, ps = [], []
            for h in range(N_HEADS):
                lt = lt_ref[i, h, :n, :]
                m_new = jnp.maximum(ms[h], jnp.max(lt, axis=0, keepdims=True))
                m_safe = jnp.where(m_new == -jnp.inf, 0.0, m_new)
                ps.append(jnp.exp2(lt - m_safe).astype(BF16))
                alphas.append(jnp.exp2(ms[h] - m_safe))
                ms[h] = m_new
            for h in range(N_HEADS):
                acc_refs[h][...] = (acc_refs[h][...] * alphas[h]
                                    + _dot(vt_ref[s0 // kt, h * VT_ROWS:(h + 1) * VT_ROWS, :n], ps[h]))
        return tuple(ms)

    m_init = tuple(jnp.full((1, tq), -jnp.inf, F32) for _ in range(N_HEADS))

    @pl.when(blk < n_blocks - 1)
    def _():
        build_qm(qin_ref)
        sweep(n_keys, functools.partial(attend, score_next=True), m_init)
        score_rows(pl.multiple_of(n_keys, kt // 2), tq, nxt, limit_next, wtn_ref)

    @pl.when(blk == n_blocks - 1)
    def _():
        sweep(n_keys, functools.partial(attend, score_next=False), m_init)

    for h in range(N_HEADS):
        acc = acc_refs[h][...]
        out_t = acc[:HEAD_DIM, :] * (1.0 / acc[HEAD_DIM:HEAD_DIM + 1, :])
        o_ref[:, heads[h]] = out_t.T


def _dsa_attn(q, k, vt, qi, ki, wt, bsz, seq, tq=256, kt=DSA_KEY_TILE):
    n = bsz * seq
    nq = seq // tq
    top_k = min(TOPK_MAX, seq // 4)
    once = dict(pipeline_mode=pl.Buffered(1))
    nxt_blk = lambda b, i: b * nq + jnp.minimum(i + 1, nq - 1)
    return pl.pallas_call(
        functools.partial(_dsa_attn_kernel, tq=tq, kt=kt, top_k=top_k, n_blocks=nq),
        out_shape=jax.ShapeDtypeStruct((n, WIDTH), F32),
        grid=(bsz, nq),
        in_specs=[pl.BlockSpec((tq, IDX_HEADS * IDX_DIM), lambda b, i: (b * nq + i, 0)),
                  pl.BlockSpec((IDX_HEADS, tq), lambda b, i: (0, b * nq + i)),
                  pl.BlockSpec((tq, IDX_HEADS * IDX_DIM), lambda b, i: (nxt_blk(b, i), 0)),
                  pl.BlockSpec((IDX_HEADS, tq), lambda b, i: (0, nxt_blk(b, i))),
                  pl.BlockSpec((tq, WIDTH), lambda b, i: (b * nq + i, 0)),
                  pl.BlockSpec((seq, LANES), lambda b, i: (b, 0), **once),
                  pl.BlockSpec((seq, WIDTH), lambda b, i: (b, 0), **once),
                  pl.BlockSpec((seq // kt, N_HEADS * VT_ROWS, kt), lambda b, i: (b, 0, 0), **once)],
        out_specs=pl.BlockSpec((tq, WIDTH), lambda b, i: (b * nq + i, 0)),
        scratch_shapes=[pltpu.VMEM((IDX_HEADS * tq, LANES), BF16),
                        pltpu.VMEM((2, seq, tq), I16), pltpu.VMEM((2, seq, tq), I16), pltpu.VMEM((seq, tq), I16),
                        pltpu.VMEM((pl.cdiv(seq // POOL_ROWS * 2 * PACK, kt) * kt, tq), I16),
                        pltpu.VMEM((2, N_HEADS, kt, tq), F32)]
                       + [pltpu.VMEM((VT_ROWS, tq), F32) for _ in range(N_HEADS)],
        compiler_params=_params("arbitrary", "arbitrary"),
        name="dsa_attn",
    )(qi, wt, qi, wt, q, ki, k, vt)


def _mem_kv_kernel(mem_ref, g_ref, wk_ref, wv_ref, gk_ref, k_out, v_out):
    mb = _rms(mem_ref[...], g_ref[...]).astype(BF16)
    kf = _dot(mb, wk_ref[...])
    for h in range(N_HEADS):
        sl = slice(h * HEAD_DIM, (h + 1) * HEAD_DIM)
        k_out[:, sl] = _rms_head(kf[:, sl], gk_ref[...]).astype(BF16)
    v_out[...] = _dot(mb, wv_ref[...]).astype(BF16)


def _mem_kv(mem2, gain, wk, wv, gk, n_mem):
    n, d = mem2.shape
    full = lambda a: pl.BlockSpec(a.shape, lambda b: (0,) * a.ndim)
    return pl.pallas_call(
        _mem_kv_kernel,
        out_shape=(jax.ShapeDtypeStruct((n, WIDTH), BF16), jax.ShapeDtypeStruct((n, WIDTH), BF16)),
        grid=(n // n_mem,),
        in_specs=[pl.BlockSpec((n_mem, d), lambda b: (b, 0)), full(gain), full(wk), full(wv), full(gk)],
        out_specs=(pl.BlockSpec((n_mem, WIDTH), lambda b: (b, 0)),
                   pl.BlockSpec((n_mem, WIDTH), lambda b: (b, 0))),
        compiler_params=_params("parallel"),
        name="mem_kv",
    )(mem2, gain, wk, wv, gk)


def _xattn_kernel(x_ref, oa_ref, ob_ref, wout_ref, g_ref, wq_ref, gq_ref, km_ref, vm_ref, wo_ref, o_ref):
    x1 = (x_ref[...] + _dot(oa_ref[...].astype(BF16), wout_ref[:WIDTH, :])
          + _dot(ob_ref[...].astype(BF16), wout_ref[WIDTH:, :]))
    hb = _rms(x1, g_ref[...]).astype(BF16)
    qf = _dot(hb, wq_ref[...])
    scale = HEAD_DIM ** -0.5
    outs = []
    for h in range(N_HEADS):
        sl = slice(h * HEAD_DIM, (h + 1) * HEAD_DIM)
        qh = (_rms(qf[:, sl], gq_ref[...]) * scale).astype(BF16)
        logits = _dot_nt(qh, km_ref[:, sl])
        p = jnp.exp(logits - jnp.max(logits, axis=-1, keepdims=True))
        oh = _dot(p.astype(BF16), vm_ref[:, sl])
        outs.append((oh * (1.0 / jnp.sum(p, axis=-1, keepdims=True))).astype(BF16))
    o_ref[...] = x1 + _dot(jnp.concatenate(outs, axis=1), wo_ref[...])


def _xattn(x2, oa, ob, wout, gain, wq, gq, km, vm, wo, seq, n_mem, tm=1024):
    n, d = x2.shape
    per_b = seq // tm
    row = lambda w: pl.BlockSpec((tm, w), lambda i: (i, 0))
    full = lambda a: pl.BlockSpec(a.shape, lambda i: (0,) * a.ndim)
    memb = pl.BlockSpec((n_mem, WIDTH), lambda i: (i // per_b, 0))
    return pl.pallas_call(
        _xattn_kernel,
        out_shape=jax.ShapeDtypeStruct((n, d), F32),
        grid=(n // tm,),
        in_specs=[row(d), row(WIDTH), row(WIDTH), full(wout), full(gain), full(wq), full(gq),
                  memb, memb, full(wo)],
        out_specs=row(d),
        compiler_params=_params("parallel"),
        name="xattn",
    )(x2, oa, ob, wout, gain, wq, gq, km, vm, wo)


def _moe_kernel(x_ref, g_ref, wr_hi_ref, wr_lo_ref, br_ref, wgu_ref, wdn_ref, o_ref, hs_ref, gs_ref, ys_ref,
                *, tm, sub):
    lane = lax.broadcasted_iota(I32, (tm, LANES), 1)
    lane_f = lane.astype(F32)
    x = x_ref[...]
    h = _rms(x, g_ref[...])
    h_hi = h.astype(BF16)
    h_lo = (h - h_hi.astype(F32)).astype(BF16)
    lg = (_dot(h_hi, wr_hi_ref[...]) + _dot(h_lo, wr_hi_ref[...]) + _dot(h_hi, wr_lo_ref[...])
          + br_ref[...])
    first = lambda cond: jnp.min(jnp.where(cond, lane_f, 1e9), axis=-1, keepdims=True)
    gl = jnp.where(lane < N_GROUPS, lg, -jnp.inf)
    gmax = jnp.max(gl, axis=-1, keepdims=True)
    gsel = first(gl == gmax)
    g_w = 1.0 / jnp.sum(jnp.exp(gl - gmax), axis=-1, keepdims=True)
    grp_of_lane = ((lane - N_GROUPS) >> (EXP_PER_GROUP.bit_length() - 1)).astype(F32)
    in_grp = (lane >= N_GROUPS) & (lane < N_GROUPS + N_EXPERTS) & (grp_of_lane == gsel)
    el = jnp.where(in_grp, lg, -jnp.inf)
    v1 = jnp.max(el, axis=-1, keepdims=True)
    i1 = first(el == v1)
    el2 = jnp.where(lane_f == i1, -jnp.inf, el)
    v2 = jnp.max(el2, axis=-1, keepdims=True)
    i2 = first(el2 == v2)
    r = jnp.exp(v2 - v1)
    w1 = 1.0 / (1.0 + r)
    gate = jnp.where(lane_f == i1, w1 * g_w, jnp.where(lane_f == i2, r * w1 * g_w, 0.0))

    onehot = jnp.where(lane_f == gsel, 1.0, 0.0)
    t_row = lax.broadcasted_iota(I32, (tm, tm), 0)
    t_col = lax.broadcasted_iota(I32, (tm, tm), 1)
    before = _dot(jnp.where(t_row > t_col, 1.0, 0.0).astype(BF16), onehot.astype(BF16))
    counts = jnp.sum(onehot, axis=0, keepdims=True)
    lane1 = lax.broadcasted_iota(I32, (1, LANES), 1)
    ends = []
    run = jnp.zeros((1, 1), F32)
    for g in range(N_GROUPS - 1):
        run = run + jnp.sum(jnp.where(lane1 == g, counts, 0.0), axis=-1, keepdims=True)
        ends.append(run)
    start_of = sum(jnp.where(lane1 == g + 1, ends[g], 0.0) for g in range(N_GROUPS - 1))
    pos = jnp.sum(onehot * (before + start_of), axis=-1, keepdims=True)
    to_sorted_t = jnp.where(pos == t_col.astype(F32), 1.0, 0.0)
    to_sorted = to_sorted_t.T.astype(BF16)
    hs_ref[...] = _dot(to_sorted, h_hi).astype(BF16)
    g_hi, g_mid, g_lo = _split3(gate)
    gs_ref[...] = _dot(to_sorted, g_hi) + _dot(to_sorted, g_mid) + _dot(to_sorted, g_lo)

    bounds = [e_[0, 0].astype(I32) for e_ in ends]
    lane_s = lax.broadcasted_iota(I32, (sub, LANES), 1)

    def slab(j, carry):
        r0 = j * sub
        rows = pl.ds(pl.multiple_of(r0, sub), sub)
        g_first = sum((b <= r0).astype(I32) for b in bounds)
        g_last = sum((b <= r0 + sub - 1).astype(I32) for b in bounds)
        hs = hs_ref[rows, :]
        gs = gs_ref[rows, :]
        ys_ref[rows, :] = jnp.zeros((sub, ys_ref.shape[1]), F32)

        def group(g, c):
            gu = _dot(hs, wgu_ref[g])
            acts = []
            for e in range(EXP_PER_GROUP):
                ge = jnp.sum(jnp.where(lane_s == g * EXP_PER_GROUP + e + N_GROUPS, gs, 0.0), axis=-1, keepdims=True)
                up = gu[:, 2 * e * D_EXPERT:(2 * e + 1) * D_EXPERT]
                acts.append((up * jax.nn.sigmoid(up) * gu[:, (2 * e + 1) * D_EXPERT:(2 * e + 2) * D_EXPERT] * ge).astype(BF16))
            ys_ref[rows, :] += _dot(jnp.concatenate(acts, axis=1), wdn_ref[g])
            return c

        lax.fori_loop(g_first, g_last + 1, group, 0)
        return carry

    lax.fori_loop(0, tm // sub, slab, 0)
    o_ref[...] = x + _dot(to_sorted_t.astype(BF16), ys_ref[...].astype(BF16))


def _moe(x2, gain, wr_hi, wr_lo, br, wgu, wdn, tm=512, sub=128):
    n, d = x2.shape
    full = lambda a: pl.BlockSpec(a.shape, lambda i: (0,) * a.ndim)
    once = lambda a: pl.BlockSpec(a.shape, lambda i: (0,) * a.ndim, pipeline_mode=pl.Buffered(1))
    return pl.pallas_call(
        functools.partial(_moe_kernel, tm=tm, sub=sub),
        out_shape=jax.ShapeDtypeStruct((n, d), F32),
        grid=(n // tm,),
        in_specs=[pl.BlockSpec((tm, d), lambda i: (i, 0)), full(gain), full(wr_hi), full(wr_lo), full(br),
                  once(wgu), once(wdn)],
        out_specs=pl.BlockSpec((tm, d), lambda i: (i, 0)),
        scratch_shapes=[pltpu.VMEM((tm, d), BF16), pltpu.VMEM((tm, LANES), F32), pltpu.VMEM((tm, d), F32)],
        compiler_params=_params("parallel"),
        name="moe",
    )(x2, gain, wr_hi, wr_lo, br, wgu, wdn)


def _rope_tables(positions):
    pos = positions.reshape(-1).astype(F32)[:, None]

    def cs(dim):
        inv = ROPE_THETA ** (-jnp.arange(0, dim, 2, dtype=F32) / dim)
        ang = pos * inv
        return jnp.cos(ang), jnp.sin(ang)

    c, s = cs(HEAD_DIM)
    ci, si = cs(IDX_DIM)
    return jnp.concatenate([c, s], 1), jnp.concatenate([ci, si, jnp.zeros_like(ci), jnp.zeros_like(ci)], 1)


def _reorder_in_weight(w):
    a = w[:, :4 * WIDTH]
    o = 4 * WIDTH
    qlat = w[:, o:o + Q_LORA]; o += Q_LORA
    k = w[:, o:o + WIDTH]; o += WIDTH
    v = w[:, o:o + WIDTH]; o += WIDTH
    misc = w[:, o:]
    pad = jnp.zeros((w.shape[0], LANES - misc.shape[1]), w.dtype)
    return jnp.concatenate([a, k, v, qlat, misc, pad], axis=1).astype(BF16)


def _row(v):
    return v.reshape(1, -1).astype(F32)


def kernel(x, mem, positions, norm_mix, w_in, lb_raw, a_gnorm, b_qlat_gain, b_wqb, b_wqidx, b_qnorm,
           b_knorm, b_kidx_norm, w_out, norm_x, norm_mem, x_wq, x_wk, x_wv, x_wo, x_qnorm, x_knorm,
           norm_ffn, w_rg, b_rg, w_re, b_re, w_gu, w_dn):
    bsz, seq, d = x.shape
    n_mem = mem.shape[1]
    depth = w_in.shape[0]
    assert d == D_MODEL and seq % (2 * DSA_KEY_TILE) == 0 and n_mem % LANES == 0, (x.shape, mem.shape)
    x2 = x.reshape(bsz * seq, d)
    mem2 = mem.reshape(bsz * n_mem, d)
    tabs = _rope_tables(positions)
    pad_lanes = lambda v: jnp.pad(v, ((0, 0), (0, LANES - v.shape[1])))
    for l in range(depth):
        u = _in_proj(x2, _row(norm_mix[l]), _reorder_in_weight(w_in[l]))
        o_a = _hgrn(u, lb_raw.astype(F32), _row(a_gnorm[l]), l, bsz, seq)
        q, k, vt, qi, ki, wt = _dsa_prep(
            u, tabs, b_wqb[l].astype(BF16), b_wqidx[l].astype(BF16), _row(b_qlat_gain[l]),
            _row(b_qnorm[l]), _row(b_knorm[l]), pad_lanes(_row(b_kidx_norm[l])))
        o_b = _dsa_attn(q, k, vt, qi, ki, wt, bsz, seq)
        km, vm = _mem_kv(mem2, _row(norm_mem[l]), x_wk[l].astype(BF16), x_wv[l].astype(BF16),
                         _row(x_knorm[l]), n_mem)
        x2 = _xattn(x2, o_a, o_b, w_out[l].astype(BF16), _row(norm_x[l]), x_wq[l].astype(BF16),
                    _row(x_qnorm[l]), km, vm, x_wo[l].astype(BF16), seq, n_mem)
        wr = pad_lanes(jnp.concatenate([w_rg[l], w_re[l].reshape(d, N_EXPERTS)], axis=1).astype(F32))
        wr_hi = wr.astype(BF16)
        wr_lo = (wr - wr_hi.astype(F32)).astype(BF16)
        br = pad_lanes(jnp.concatenate([b_rg[l], b_re[l].reshape(-1)]).reshape(1, -1).astype(F32))
        x2 = _moe(x2, _row(norm_ffn[l]), wr_hi, wr_lo, br,
                  w_gu[l].transpose(0, 2, 1, 3).reshape(N_GROUPS, d, EXP_PER_GROUP * 2 * D_EXPERT).astype(BF16),
                  w_dn[l].reshape(N_GROUPS, EXP_PER_GROUP * D_EXPERT, d).astype(BF16))
    return x2.reshape(bsz, seq, d)
```

```python
import functools

import numpy as np
import jax
import jax.numpy as jnp
from jax import lax
from jax.experimental import pallas as pl
from jax.experimental.pallas import tpu as pltpu

F32 = jnp.float32
BF16 = jnp.bfloat16
I32 = jnp.int32
I16 = jnp.int16

EPS = 1e-6
ROPE_THETA = 10000.0
CHUNK = 64
CHUNK_SHIFT = CHUNK.bit_length() - 1
SUB = 16
SUB_SHIFT = SUB.bit_length() - 1
N_HEADS = 4
HEAD_DIM = 128
WIDTH = N_HEADS * HEAD_DIM
Q_LORA = 256
D_MODEL = 1024
IDX_HEADS = 16
IDX_DIM = 64
ROT_I = IDX_DIM // 2
TOPK_MAX = 256
N_GROUPS = 4
EXP_PER_GROUP = 4
N_EXPERTS = N_GROUPS * EXP_PER_GROUP
D_EXPERT = 256
LANES = 128
INT_MIN = -2 ** 31
INT16_MIN, INT16_MAX = -2 ** 15, 2 ** 15 - 1
HALF_BITS = 16
PACK = 16
LOG2E = 1.4426950408889634
VT_ROWS = HEAD_DIM + PACK
DSA_KEY_TILE = 512
SCORE_ROWS = 128
POOL_ROWS = 256
SURE_PASSES = 8

U_K = 4 * WIDTH
U_COLS = U_K + 2 * WIDTH + Q_LORA + LANES

VMEM_LIMIT = 56 * 1024 * 1024


def _rms(x, gain):
    return x * lax.rsqrt(jnp.mean(x * x, axis=-1, keepdims=True) + EPS) * gain


def _dot(a, b):
    return jnp.dot(a, b, preferred_element_type=F32)


def _rms_head(x, gain, width=HEAD_DIM):
    ones = (lax.broadcasted_iota(I32, (LANES, LANES), 0) < width).astype(BF16)
    ss = _dot((x * x).astype(BF16), ones)
    return x * lax.rsqrt(ss * (1.0 / width) + EPS) * gain


def _dot_nt(a, b):
    return lax.dot_general(a, b, (((1,), (1,)), ((), ())), preferred_element_type=F32)


def _params(*sem):
    return pltpu.CompilerParams(dimension_semantics=sem, vmem_limit_bytes=VMEM_LIMIT)


def _split3(x):
    hi = x.astype(BF16)
    r1 = x - hi.astype(F32)
    mid = r1.astype(BF16)
    lo = (r1 - mid.astype(F32)).astype(BF16)
    return hi, mid, lo


def _hgrn_kernel(q_ref, f_ref, i_ref, g_ref, lbraw_ref, gn_ref, e_ref, o_ref, st_ref,
                 *, layer, n_chunks, unroll):
    @pl.when(pl.program_id(2) == 0)
    def _():
        st_ref[...] = jnp.zeros_like(st_ref)

    lr = lbraw_ref[...]
    ex = jnp.exp(lr - jnp.max(lr, axis=0, keepdims=True))
    sm = ex / jnp.sum(ex, axis=0, keepdims=True)
    lb = jnp.zeros((1, HEAD_DIM), F32)
    for r in range(1, layer + 1):
        lb = lb + sm[r:r + 1, :]
    log_lb = jnp.log(lb)
    log_1mlb = jnp.log(1.0 - lb)

    row = lax.broadcasted_iota(I32, (CHUNK, HEAD_DIM), 0)
    col = lax.broadcasted_iota(I32, (CHUNK, HEAD_DIM), 1)
    row_l = row & (SUB - 1)
    row_b = row >> SUB_SHIFT
    col_b = col >> SUB_SHIFT
    tri = (lax.broadcasted_iota(I32, (CHUNK, CHUNK), 0)
           >= lax.broadcasted_iota(I32, (CHUNK, CHUNK), 1)).astype(BF16)
    n_sub = CHUNK // SUB
    zeros_c = jnp.zeros((CHUNK, HEAD_DIM), F32)

    def sub_bcast(t, s):
        t4 = t.reshape(n_sub, SUB, HEAD_DIM)
        return jnp.broadcast_to(t4[:, s:s + 1, :], (n_sub, SUB, HEAD_DIM)).reshape(CHUNK, HEAD_DIM)

    def gates(z):
        ls = jnp.minimum(z, 0.0) - jnp.log(1.0 + jnp.exp(-jnp.abs(z)))
        if layer == 0:
            return ls, (ls - z) * LOG2E
        cc = log_1mlb + ls
        log_f = jnp.maximum(log_lb, cc) + jnp.log(1.0 + jnp.exp(-jnp.abs(log_lb - cc)))
        return log_f, (cc - z) * LOG2E

    def diag_terms(q, b2, c2):
        pieces = []
        for s in range(SUB):
            d = jnp.where(row_l >= s, b2 - sub_bcast(c2, s), -jnp.inf)
            pieces.append((q * jnp.exp2(d)).astype(BF16))
        return jnp.concatenate(pieces, axis=1)

    def below_keys(b2, c2):
        kts = []
        for i in range(1, n_sub):
            r_i = b2[i * SUB:i * SUB + 1, :]
            kts.append(jnp.exp2(jnp.where(row < i * SUB, r_i - c2, -jnp.inf)))
            kts.append(zeros_c)
        return jnp.concatenate(kts, axis=0).astype(BF16)

    def below_scores(r):
        a = jnp.zeros((CHUNK, HEAD_DIM), F32)
        for i in range(1, n_sub):
            a = a + jnp.where(row_b == i, r[:, (i - 1) * LANES:i * LANES], 0.0)
        return a

    def chunks(it, carry):
        cs = range(unroll)
        rows = [pl.ds(pl.multiple_of((it * unroll + c) * CHUNK, CHUNK), CHUNK) for c in cs]
        qr = [q_ref[r, :] for r in rows]
        v = [i_ref[r, :] for r in rows]
        q = [x * jax.nn.sigmoid(x) for x in qr]
        gt = [gates(f_ref[r, :]) for r in rows]
        b2 = []
        for c in cs:
            hi, mid, lo = _split3(gt[c][0])
            b2.append((_dot(tri, hi) + _dot(tri, mid) + _dot(tri, lo)) * LOG2E)
        c2 = [b2[c] - gt[c][1] for c in cs]
        b_last = [x[CHUNK - 1:CHUNK, :] for x in b2]
        a2 = [_dot(diag_terms(q[c], b2[c], c2[c]), e_ref[...]) for c in cs]
        rr = [_dot_nt((q[c] * jnp.exp2(b2[c] - sub_bcast(b2[c], 0))).astype(BF16), below_keys(b2[c], c2[c]))
              for c in cs]
        upd = [_dot(v[c].T.astype(BF16), jnp.exp2(b_last[c] - c2[c]).astype(BF16)) for c in cs]
        qe = [(q[c] * jnp.exp2(b2[c])).astype(BF16) for c in cs]
        st = st_ref[...]
        o_inter = []
        for c in cs:
            o_inter.append(_dot_nt(qe[c], st.astype(BF16)))
            st = st * jnp.exp2(b_last[c]) + upd[c]
        st_ref[...] = st
        a = [jnp.where(col_b == row_b, a2[c], 0.0) + below_scores(rr[c]) for c in cs]
        o_intra = [_dot(a[c][:, :CHUNK].astype(BF16), v[c].astype(BF16)) for c in cs]
        for c in cs:
            g = g_ref[rows[c], :]
            o_ref[rows[c], :] = _rms(o_inter[c] + o_intra[c], gn_ref[...]) * (g * jax.nn.sigmoid(g))
        return carry

    lax.fori_loop(0, n_chunks // unroll, chunks, 0)


def _hgrn_selector():
    e = np.zeros((SUB * HEAD_DIM, LANES), np.float32)
    s_of_row = np.arange(SUB * HEAD_DIM) // HEAD_DIM
    cols = np.arange(LANES)
    e[:, :] = ((cols[None, :] % SUB) == s_of_row[:, None]) & (cols[None, :] < CHUNK)
    return jnp.asarray(e, BF16)


def _hgrn(u, lb_raw, gn, layer, bsz, seq, tb=1024):
    n = bsz * seq
    nb = seq // tb
    blk = lambda k: pl.BlockSpec((tb, HEAD_DIM), lambda b, h, j, k=k: (b * nb + j, h + N_HEADS * k))
    n_layers = lb_raw.shape[0]
    return pl.pallas_call(
        functools.partial(_hgrn_kernel, layer=layer, n_chunks=tb // CHUNK, unroll=16),
        out_shape=jax.ShapeDtypeStruct((n, WIDTH), F32),
        grid=(bsz, N_HEADS, nb),
        in_specs=[blk(0), blk(1), blk(2), blk(3),
                  pl.BlockSpec((n_layers, HEAD_DIM), lambda b, h, j: (0, h)),
                  pl.BlockSpec((1, HEAD_DIM), lambda b, h, j: (0, 0)),
                  pl.BlockSpec((SUB * HEAD_DIM, LANES), lambda b, h, j: (0, 0))],
        out_specs=pl.BlockSpec((tb, HEAD_DIM), lambda b, h, j: (b * nb + j, h)),
        scratch_shapes=[pltpu.VMEM((HEAD_DIM, HEAD_DIM), F32)],
        compiler_params=_params("parallel", "parallel", "arbitrary"),
        name="hgrn",
    )(u, u, u, u, lb_raw, gn, _hgrn_selector())


def _rope128(x, cos, sin_signed):
    return x * cos + pltpu.roll(x, HEAD_DIM // 2, 1) * sin_signed


def _rope64(x, cos, sin_a, sin_b):
    return x * cos + pltpu.roll(x, LANES - ROT_I, 1) * sin_a + pltpu.roll(x, ROT_I, 1) * sin_b


def _proj_prep_kernel(x_ref, g_ref, w_ref, ta_ref, tb_ref, wqb_ref, wqi_ref, gql_ref, gqn_ref, gkn_ref, gki_ref,
                      ua_out, q_out, k_out, vt_out, qi_out, ki_out, wt_out):
    hb = _rms(x_ref[...], g_ref[...]).astype(BF16)
    ub = _dot(hb, w_ref[:, U_K:])
    k, v = ub[:, :WIDTH], ub[:, WIDTH:2 * WIDTH]
    ql, misc = ub[:, 2 * WIDTH:2 * WIDTH + Q_LORA], ub[:, 2 * WIDTH + Q_LORA:]
    ua_out[...] = _dot(hb, w_ref[:, :U_K])

    ta, tb = ta_ref[...], tb_ref[...]
    lane = lax.broadcasted_iota(I32, ta.shape, 1)
    ta_swapped = pltpu.roll(ta, HEAD_DIM // 2, 1)
    cm = jnp.where(lane < HEAD_DIM // 2, ta, ta_swapped)
    sm = jnp.where(lane < HEAD_DIM // 2, -ta_swapped, ta)
    quarter = lane >> (ROT_I.bit_length() - 1)
    r32, r64, r96 = (pltpu.roll(tb, sh * ROT_I, 1) for sh in (1, 2, 3))
    ci = jnp.where(quarter == 0, tb, jnp.where(quarter == 1, r32, jnp.where(quarter == 2, r64, r96)))
    si = jnp.where(quarter == 0, r96, jnp.where(quarter == 1, tb, jnp.where(quarter == 2, r32, r64)))
    sia = jnp.where((lane & ROT_I) == 0, -si, 0.0)
    sib = jnp.where((lane & ROT_I) != 0, si, 0.0)
    cb = _rms(ql, gql_ref[...]).astype(BF16)
    qf = _dot(cb, wqb_ref[...])
    scale = HEAD_DIM ** -0.5
    for h in range(N_HEADS):
        sl = slice(h * HEAD_DIM, (h + 1) * HEAD_DIM)
        qh = _rope128(_rms_head(qf[:, sl], gqn_ref[...]), cm, sm)
        q_out[:, sl] = (qh * (scale * LOG2E)).astype(BF16)
        kh = _rope128(_rms_head(k[:, sl], gkn_ref[...]), cm, sm)
        k_out[:, sl] = kh.astype(BF16)
    vt = v.T.astype(BF16)
    ones = jnp.ones((PACK, vt.shape[1]), BF16)
    for h in range(N_HEADS):
        vt_out[0, h * VT_ROWS:h * VT_ROWS + HEAD_DIM, :] = vt[h * HEAD_DIM:(h + 1) * HEAD_DIM, :]
        vt_out[0, h * VT_ROWS + HEAD_DIM:(h + 1) * VT_ROWS, :] = ones
    qi = _dot(cb, wqi_ref[...])
    for p in range(IDX_HEADS * IDX_DIM // LANES):
        sl = slice(p * LANES, (p + 1) * LANES)
        qi_out[:, sl] = _rope64(qi[:, sl], ci, sia, sib).astype(BF16)
    kn = _rope64(_rms_head(misc, gki_ref[...], IDX_DIM), ci, sia, sib)
    ki_out[...] = (kn + pltpu.roll(kn, IDX_DIM, 1)).astype(BF16)
    wt = (misc * (IDX_HEADS ** -0.5 * IDX_DIM ** -0.5)).T
    wt_out[...] = wt[IDX_DIM:IDX_DIM + IDX_HEADS, :]


def _proj_prep(x2, gain, w_bf, tabs, wqb, wqi, gql, gqn, gkn, gki, tm=DSA_KEY_TILE):
    n, d = x2.shape
    nt = n // tm
    row = lambda w: pl.BlockSpec((tm, w), lambda i: (i, 0))
    full = lambda a: pl.BlockSpec(a.shape, lambda i: (0,) * a.ndim)
    return pl.pallas_call(
        _proj_prep_kernel,
        out_shape=(jax.ShapeDtypeStruct((n, U_K), F32),
                   jax.ShapeDtypeStruct((n, WIDTH), BF16),
                   jax.ShapeDtypeStruct((n, WIDTH), BF16),
                   jax.ShapeDtypeStruct((nt, N_HEADS * VT_ROWS, tm), BF16),
                   jax.ShapeDtypeStruct((n, IDX_HEADS * IDX_DIM), BF16),
                   jax.ShapeDtypeStruct((n, LANES), BF16),
                   jax.ShapeDtypeStruct((IDX_HEADS, n), F32)),
        grid=(nt,),
        in_specs=[row(d), full(gain), pl.BlockSpec(w_bf.shape, lambda i: (0, 0), pipeline_mode=pl.Buffered(1)),
                  row(LANES), row(LANES),
                  full(wqb), full(wqi), full(gql), full(gqn), full(gkn), full(gki)],
        out_specs=(row(U_K), row(WIDTH), row(WIDTH),
                   pl.BlockSpec((1, N_HEADS * VT_ROWS, tm), lambda i: (i, 0, 0)),
                   row(IDX_HEADS * IDX_DIM), row(LANES),
                   pl.BlockSpec((IDX_HEADS, tm), lambda i: (0, i))),
        compiler_params=_params("parallel"),
        name="proj_prep",
    )(x2, gain, w_bf, *tabs, wqb, wqi, gql, gqn, gkn, gki)


def _sortable(x):
    bits = lax.bitcast_convert_type(x, I32)
    return bits ^ ((bits >> 31) & 0x7FFFFFFF)


def _dsa_attn_kernel(qi_ref, wt_ref, qin_ref, wtn_ref, q_ref, ki_ref, k_ref, vt_ref, o_ref,
                     qm_ref, hi_ref, lo_ref, kb_ref, pk_ref, lt_ref, acc0, acc1, acc2, acc3,
                     *, tq, kt, top_k, n_blocks):
    acc_refs = (acc0, acc1, acc2, acc3)
    blk = pl.program_id(1)
    cur = blk & 1
    nxt = 1 - cur
    q0 = blk * tq
    n_keys = q0 + tq
    sub = SCORE_ROWS
    key_iota = lax.broadcasted_iota(I32, (sub, tq), 0)
    lane_q = lax.broadcasted_iota(I32, (tq, LANES), 1)
    pooled_per_group = 2 * PACK

    def sweep(total, fn, carry):
        n_full = total // kt
        carry = lax.fori_loop(0, n_full // 2, lambda jp, c: fn(pl.multiple_of(jp * 2 * kt, 2 * kt), 2 * kt, c), carry)
        carry = lax.cond((n_full & 1) == 1, lambda c: fn(pl.multiple_of((n_full - 1) * kt, kt), kt, c),
                         lambda c: c, carry)
        return lax.cond(total % kt != 0, lambda c: fn(pl.multiple_of(n_full * kt, kt), kt // 2, c),
                        lambda c: c, carry)

    def limit_of(q_start):
        qpos = q_start + lax.broadcasted_iota(I32, (1, tq), 1)
        return ((qpos >> CHUNK_SHIFT) + 1) << CHUNK_SHIFT

    def build_qm(src_ref):
        for h in range(IDX_HEADS):
            src = src_ref[:, (h // 2) * LANES:(h // 2 + 1) * LANES].astype(F32)
            keep = (lane_q < IDX_DIM) if h % 2 == 0 else (lane_q >= IDX_DIM)
            qm_ref[h * tq:(h + 1) * tq, :] = jnp.where(keep, src, 0.0).astype(BF16)

    def score_rows(start, size, slot, limit, w_ref):
        for part in range(size // sub):
            base = start + part * sub
            rows = pl.ds(pl.multiple_of(base, sub), sub)
            ki = ki_ref[rows, :]
            acc = jnp.zeros((sub, tq), F32)
            for h in range(IDX_HEADS):
                x = _dot_nt(ki, qm_ref[h * tq:(h + 1) * tq, :])
                acc = acc + w_ref[h:h + 1, :] * jnp.maximum(x, 0.0)
            key = jnp.where(base + key_iota < limit, _sortable(acc), INT_MIN)
            hi_ref[slot, rows, :] = (key >> HALF_BITS).astype(I16)
            lo_ref[slot, rows, :] = ((key & (2 ** HALF_BITS - 1)) + INT16_MIN).astype(I16)

    @pl.when(blk == 0)
    def _():
        build_qm(qi_ref)
        limit0 = limit_of(0)

        def body(start, size, carry):
            score_rows(start, size, 0, limit0, wt_ref)
            return carry
        sweep(n_keys, body, 0)

    one, zero = jnp.ones((), BF16), jnp.zeros((), BF16)
    neg_inf = jnp.full((), -jnp.inf, BF16)
    hi_rows = lambda rows: hi_ref[cur, rows, :]
    lo_rows = lambda rows: lo_ref[cur, rows, :]
    kb_rows = lambda rows: kb_ref[rows, :]

    def count16(get, trial, strict, total=None):
        t16 = jnp.broadcast_to(trial.astype(I16), (PACK, tq))

        def hits(start, size, c):
            x = get(pl.ds(start, size))
            parts = [jnp.zeros((PACK, tq), BF16) for _ in range(4)]
            for r in range(size // PACK):
                xr = x[r * PACK:(r + 1) * PACK, :]
                parts[r % 4] = parts[r % 4] + jnp.where((xr > t16) if strict else (xr >= t16), one, zero)
            return c + ((parts[0] + parts[1]) + (parts[2] + parts[3])).astype(F32)

        c = sweep(n_keys if total is None else total, hits, jnp.zeros((PACK, tq), F32))
        return jnp.sum(c, axis=0, keepdims=True)

    def bisect16(get, want, n_all, early_exit, total=None):
        def step(it, carry):
            lo, n_lo, n_up = carry
            trial = lo + lax.shift_left(jnp.int32(1), HALF_BITS - 1 - it)
            n = count16(get, trial, False, total)
            ok = n >= want
            return jnp.where(ok, trial, lo), jnp.where(ok, n, n_lo), jnp.where(ok, n_up, n)

        init = (jnp.full((1, tq), INT16_MIN, I32), n_all, jnp.zeros((1, tq), F32))
        if not early_exit:
            return lax.fori_loop(0, HALF_BITS, step, init)

        def unresolved(carry):
            it, _, n_lo, _ = carry
            return (it < HALF_BITS) & (jnp.max(jnp.where(n_lo != want, 1.0, 0.0)) > 0.0)

        def two_steps(carry):
            return (carry[0] + 2,) + step(carry[0] + 1, step(carry[0], carry[1:]))

        return lax.while_loop(unresolved, two_steps, (jnp.int32(SURE_PASSES),) + lax.fori_loop(0, SURE_PASSES, step, init))[1:]

    tau_hi, n_hi_ge, n_hi_gt = bisect16(hi_rows, float(top_k), jnp.full((1, tq), n_keys.astype(F32), F32), False)
    want_lo = top_k - n_hi_gt
    tau_hi_row = tau_hi.astype(I16)

    min16 = jnp.full((PACK, tq), INT16_MIN, I16)

    def bucket_rows(start, size, carry):
        rows = pl.ds(start, size)
        kb = jnp.where(hi_rows(rows) == jnp.broadcast_to(tau_hi_row, (size, tq)), lo_rows(rows),
                       jnp.full((), INT16_MIN, I16))
        kb_ref[rows, :] = kb
        for g in range(size // POOL_ROWS):
            top1, top2 = min16, min16
            for r in range(POOL_ROWS // PACK):
                x = kb[g * POOL_ROWS + r * PACK:g * POOL_ROWS + (r + 1) * PACK, :]
                below = x < top1
                runner = jnp.where(below, x, top1)
                top2 = jnp.where(runner > top2, runner, top2)
                top1 = jnp.where(below, top1, x)
            base = pl.multiple_of((start // POOL_ROWS + g) * pooled_per_group, pooled_per_group)
            pk_ref[pl.ds(base, PACK), :] = top1
            pk_ref[pl.ds(base + PACK, PACK), :] = top2
        return carry

    sweep(n_keys, bucket_rows, 0)
    n_groups = n_keys // POOL_ROWS
    pooled_total = (n_groups * pooled_per_group + kt // 2 - 1) // (kt // 2) * (kt // 2)

    def pad_pooled(g, carry):
        pk_ref[pl.ds(pl.multiple_of(g * pooled_per_group, pooled_per_group), pooled_per_group), :] = jnp.full(
            (pooled_per_group, tq), INT16_MIN, I16)
        return carry

    lax.fori_loop(n_groups, pooled_total // pooled_per_group, pad_pooled, 0)
    n_bucket = n_hi_ge - n_hi_gt
    pk_rows = lambda rows: pk_ref[rows, :]
    tau_p, _, _ = bisect16(pk_rows, want_lo, n_bucket, True, pooled_total)
    n_ge_p = jnp.where(tau_p == INT16_MIN, n_bucket, count16(kb_rows, tau_p, False))
    n_gt_p = count16(kb_rows, tau_p, True)
    pooled_ok = (n_ge_p == want_lo) | ((n_ge_p > want_lo) & (n_gt_p < want_lo))

    def full_search():
        t, n_ge, _ = bisect16(kb_rows, want_lo, n_bucket, True)
        return t, n_ge, count16(kb_rows, t, True)

    tau_lo, n_kb_ge, n_kb_gt = lax.cond(
        jnp.min(jnp.where(pooled_ok, 1.0, 0.0)) > 0.0,
        lambda: (tau_p, n_ge_p, n_gt_p), full_search)
    live = (tau_hi > INT16_MIN) | (tau_lo > INT16_MIN)
    excess = jnp.max(jnp.where(live, n_kb_ge - want_lo, 0.0))

    @pl.when(excess > 0)
    def _():
        need = want_lo - n_kb_gt
        live_f = jnp.where(live, 1.0, 0.0)
        step_rows = kt // 2
        tri = (lax.broadcasted_iota(I32, (step_rows, step_rows), 0)
               >= lax.broadcasted_iota(I32, (step_rows, step_rows), 1)).astype(BF16)

        def body(start, size, seen):
            for part in range(size // step_rows):
                rows = pl.ds(pl.multiple_of(start + part * step_rows, step_rows), step_rows)
                h32 = hi_rows(rows).astype(I32)
                eq = jnp.where(h32 == tau_hi, jnp.where(lo_rows(rows).astype(I32) == tau_lo, 1.0, 0.0), 0.0)
                rank = _dot(tri, eq.astype(BF16)) + seen
                dropped = eq * (1.0 - jnp.where(rank <= need, live_f, 0.0))
                hi_ref[cur, rows, :] = jnp.where(dropped > 0.0, INT16_MIN, h32).astype(I16)
                seen = rank[step_rows - 1:step_rows, :]
            return seen
        sweep(n_keys, body, jnp.zeros((1, tq), F32))

    tau_lo_row = jnp.where(live, tau_lo, INT16_MAX).astype(I16)

    def bias_rows(rows, size):
        h16 = hi_rows(rows)
        th = jnp.broadcast_to(tau_hi_row, (size, tq))
        in_bucket = jnp.where(lo_rows(rows) >= jnp.broadcast_to(tau_lo_row, (size, tq)), zero, neg_inf)
        return jnp.where(h16 > th, zero, jnp.where(h16 == th, in_bucket, neg_inf)).astype(F32)

    for a in acc_refs:
        a[...] = jnp.zeros_like(a)
    heads = [slice(h * HEAD_DIM, (h + 1) * HEAD_DIM) for h in range(N_HEADS)]
    limit_next = limit_of(q0 + tq)

    def attend(start, size, ms, score_next):
        segs = [(start, kt), (start + kt, kt)] if size == 2 * kt else [(start, size)]
        for i, (s0, n) in enumerate(segs):
            rows = pl.ds(pl.multiple_of(s0, kt // 2), n)
            bias = bias_rows(rows, n)
            for h in range(N_HEADS):
                lt_ref[i, h, :n, :] = _dot_nt(k_ref[rows, heads[h]], q_ref[:, heads[h]]) + bias
        ms = list(ms)
        for i, (s0, n) in enumerate(segs):
            if score_next:
                score_rows(s0, n, nxt, limit_next, wtn_ref)
            alphas, ps = [], []
            for h in range(N_HEADS):
                lt = lt_ref[i, h, :n, :]
                m_new = jnp.maximum(ms[h], jnp.max(lt, axis=0, keepdims=True))
                m_safe = jnp.where(m_new == -jnp.inf, 0.0, m_new)
                ps.append(jnp.exp2(lt - m_safe).astype(BF16))
                alphas.append(jnp.exp2(ms[h] - m_safe))
                ms[h] = m_new
            for h in range(N_HEADS):
                acc_refs[h][...] = (acc_refs[h][...] * alphas[h]
                                    + _dot(vt_ref[s0 // kt, h * VT_ROWS:(h + 1) * VT_ROWS, :n], ps[h]))
        return tuple(ms)

    m_init = tuple(jnp.full((1, tq), -jnp.inf, F32) for _ in range(N_HEADS))

    @pl.when(blk < n_blocks - 1)
    def _():
        build_qm(qin_ref)
        sweep(n_keys, functools.partial(attend, score_next=True), m_init)
        score_rows(pl.multiple_of(n_keys, kt // 2), tq, nxt, limit_next, wtn_ref)

    @pl.when(blk == n_blocks - 1)
    def _():
        sweep(n_keys, functools.partial(attend, score_next=False), m_init)

    for h in range(N_HEADS):
        acc = acc_refs[h][...]
        out_t = acc[:HEAD_DIM, :] * (1.0 / acc[HEAD_DIM:HEAD_DIM + 1, :])
        o_ref[:, heads[h]] = out_t.T


def _dsa_attn(q, k, vt, qi, ki, wt, bsz, seq, tq=256, kt=DSA_KEY_TILE):
    n = bsz * seq
    nq = seq // tq
    top_k = min(TOPK_MAX, seq // 4)
    once = dict(pipeline_mode=pl.Buffered(1))
    nxt_blk = lambda b, i: b * nq + jnp.minimum(i + 1, nq - 1)
    return pl.pallas_call(
        functools.partial(_dsa_attn_kernel, tq=tq, kt=kt, top_k=top_k, n_blocks=nq),
        out_shape=jax.ShapeDtypeStruct((n, WIDTH), F32),
        grid=(bsz, nq),
        in_specs=[pl.BlockSpec((tq, IDX_HEADS * IDX_DIM), lambda b, i: (b * nq + i, 0)),
                  pl.BlockSpec((IDX_HEADS, tq), lambda b, i: (0, b * nq + i)),
                  pl.BlockSpec((tq, IDX_HEADS * IDX_DIM), lambda b, i: (nxt_blk(b, i), 0)),
                  pl.BlockSpec((IDX_HEADS, tq), lambda b, i: (0, nxt_blk(b, i))),
                  pl.BlockSpec((tq, WIDTH), lambda b, i: (b * nq + i, 0)),
                  pl.BlockSpec((seq, LANES), lambda b, i: (b, 0), **once),
                  pl.BlockSpec((seq, WIDTH), lambda b, i: (b, 0), **once),
                  pl.BlockSpec((seq // kt, N_HEADS * VT_ROWS, kt), lambda b, i: (b, 0, 0), **once)],
        out_specs=pl.BlockSpec((tq, WIDTH), lambda b, i: (b * nq + i, 0)),
        scratch_shapes=[pltpu.VMEM((IDX_HEADS * tq, LANES), BF16),
                        pltpu.VMEM((2, seq, tq), I16), pltpu.VMEM((2, seq, tq), I16), pltpu.VMEM((seq, tq), I16),
                        pltpu.VMEM((pl.cdiv(seq // POOL_ROWS * 2 * PACK, kt) * kt, tq), I16),
                        pltpu.VMEM((2, N_HEADS, kt, tq), F32)]
                       + [pltpu.VMEM((VT_ROWS, tq), F32) for _ in range(N_HEADS)],
        compiler_params=_params("arbitrary", "arbitrary"),
        name="dsa_attn",
    )(qi, wt, qi, wt, q, ki, k, vt)


def _mem_kv_kernel(mem_ref, g_ref, wk_ref, wv_ref, gk_ref, k_out, v_out):
    mb = _rms(mem_ref[...], g_ref[...]).astype(BF16)
    kf = _dot(mb, wk_ref[...])
    for h in range(N_HEADS):
        sl = slice(h * HEAD_DIM, (h + 1) * HEAD_DIM)
        k_out[:, sl] = _rms_head(kf[:, sl], gk_ref[...]).astype(BF16)
    v_out[...] = _dot(mb, wv_ref[...]).astype(BF16)


def _mem_kv(mem2, gain, wk, wv, gk, n_mem):
    n, d = mem2.shape
    full = lambda a: pl.BlockSpec(a.shape, lambda b: (0,) * a.ndim)
    return pl.pallas_call(
        _mem_kv_kernel,
        out_shape=(jax.ShapeDtypeStruct((n, WIDTH), BF16), jax.ShapeDtypeStruct((n, WIDTH), BF16)),
        grid=(n // n_mem,),
        in_specs=[pl.BlockSpec((n_mem, d), lambda b: (b, 0)), full(gain), full(wk), full(wv), full(gk)],
        out_specs=(pl.BlockSpec((n_mem, WIDTH), lambda b: (b, 0)),
                   pl.BlockSpec((n_mem, WIDTH), lambda b: (b, 0))),
        compiler_params=_params("parallel"),
        name="mem_kv",
    )(mem2, gain, wk, wv, gk)


def _xattn_kernel(x_ref, oa_ref, ob_ref, wout_ref, g_ref, wq_ref, gq_ref, km_ref, vm_ref, wo_ref, o_ref):
    x1 = (x_ref[...] + _dot(oa_ref[...].astype(BF16), wout_ref[:WIDTH, :])
          + _dot(ob_ref[...].astype(BF16), wout_ref[WIDTH:, :]))
    hb = _rms(x1, g_ref[...]).astype(BF16)
    qf = _dot(hb, wq_ref[...])
    scale = HEAD_DIM ** -0.5
    outs = []
    for h in range(N_HEADS):
        sl = slice(h * HEAD_DIM, (h + 1) * HEAD_DIM)
        qh = (_rms(qf[:, sl], gq_ref[...]) * scale).astype(BF16)
        logits = _dot_nt(qh, km_ref[:, sl])
        p = jnp.exp(logits - jnp.max(logits, axis=-1, keepdims=True))
        oh = _dot(p.astype(BF16), vm_ref[:, sl])
        outs.append((oh * (1.0 / jnp.sum(p, axis=-1, keepdims=True))).astype(BF16))
    o_ref[...] = x1 + _dot(jnp.concatenate(outs, axis=1), wo_ref[...])


def _xattn(x2, oa, ob, wout, gain, wq, gq, km, vm, wo, seq, n_mem, tm=1024):
    n, d = x2.shape
    per_b = seq // tm
    row = lambda w: pl.BlockSpec((tm, w), lambda i: (i, 0))
    full = lambda a: pl.BlockSpec(a.shape, lambda i: (0,) * a.ndim)
    memb = pl.BlockSpec((n_mem, WIDTH), lambda i: (i // per_b, 0))
    return pl.pallas_call(
        _xattn_kernel,
        out_shape=jax.ShapeDtypeStruct((n, d), F32),
        grid=(n // tm,),
        in_specs=[row(d), row(WIDTH), row(WIDTH), full(wout), full(gain), full(wq), full(gq),
                  memb, memb, full(wo)],
        out_specs=row(d),
        compiler_params=_params("parallel"),
        name="xattn",
    )(x2, oa, ob, wout, gain, wq, gq, km, vm, wo)


def _moe_kernel(x_ref, g_ref, wr_hi_ref, wr_lo_ref, br_ref, wgu_ref, wdn_ref, o_ref, hs_ref, gs_ref, ys_ref,
                *, tm, sub):
    lane = lax.broadcasted_iota(I32, (tm, LANES), 1)
    lane_f = lane.astype(F32)
    x = x_ref[...]
    h = _rms(x, g_ref[...])
    h_hi = h.astype(BF16)
    h_lo = (h - h_hi.astype(F32)).astype(BF16)
    lg = (_dot(h_hi, wr_hi_ref[...]) + _dot(h_lo, wr_hi_ref[...]) + _dot(h_hi, wr_lo_ref[...])
          + br_ref[...])
    first = lambda cond: jnp.min(jnp.where(cond, lane_f, 1e9), axis=-1, keepdims=True)
    gl = jnp.where(lane < N_GROUPS, lg, -jnp.inf)
    gmax = jnp.max(gl, axis=-1, keepdims=True)
    gsel = first(gl == gmax)
    g_w = 1.0 / jnp.sum(jnp.exp(gl - gmax), axis=-1, keepdims=True)
    grp_of_lane = ((lane - N_GROUPS) >> (EXP_PER_GROUP.bit_length() - 1)).astype(F32)
    in_grp = (lane >= N_GROUPS) & (lane < N_GROUPS + N_EXPERTS) & (grp_of_lane == gsel)
    el = jnp.where(in_grp, lg, -jnp.inf)
    v1 = jnp.max(el, axis=-1, keepdims=True)
    i1 = first(el == v1)
    el2 = jnp.where(lane_f == i1, -jnp.inf, el)
    v2 = jnp.max(el2, axis=-1, keepdims=True)
    i2 = first(el2 == v2)
    r = jnp.exp(v2 - v1)
    w1 = 1.0 / (1.0 + r)
    gate = jnp.where(lane_f == i1, w1 * g_w, jnp.where(lane_f == i2, r * w1 * g_w, 0.0))

    onehot = jnp.where(lane_f == gsel, 1.0, 0.0)
    t_row = lax.broadcasted_iota(I32, (tm, tm), 0)
    t_col = lax.broadcasted_iota(I32, (tm, tm), 1)
    before = _dot(jnp.where(t_row > t_col, 1.0, 0.0).astype(BF16), onehot.astype(BF16))
    counts = jnp.sum(onehot, axis=0, keepdims=True)
    lane1 = lax.broadcasted_iota(I32, (1, LANES), 1)
    ends = []
    run = jnp.zeros((1, 1), F32)
    for g in range(N_GROUPS - 1):
        run = run + jnp.sum(jnp.where(lane1 == g, counts, 0.0), axis=-1, keepdims=True)
        ends.append(run)
    start_of = sum(jnp.where(lane1 == g + 1, ends[g], 0.0) for g in range(N_GROUPS - 1))
    pos = jnp.sum(onehot * (before + start_of), axis=-1, keepdims=True)
    to_sorted_t = jnp.where(pos == t_col.astype(F32), 1.0, 0.0)
    to_sorted = to_sorted_t.T.astype(BF16)
    hs_ref[...] = _dot(to_sorted, h_hi).astype(BF16)
    g_hi, g_mid, g_lo = _split3(gate)
    gs_ref[...] = _dot(to_sorted, g_hi) + _dot(to_sorted, g_mid) + _dot(to_sorted, g_lo)

    bounds = [e_[0, 0].astype(I32) for e_ in ends]
    lane_s = lax.broadcasted_iota(I32, (sub, LANES), 1)

    def slab(j, carry):
        r0 = j * sub
        rows = pl.ds(pl.multiple_of(r0, sub), sub)
        g_first = sum((b <= r0).astype(I32) for b in bounds)
        g_last = sum((b <= r0 + sub - 1).astype(I32) for b in bounds)
        hs = hs_ref[rows, :]
        gs = gs_ref[rows, :]
        ys_ref[rows, :] = jnp.zeros((sub, ys_ref.shape[1]), F32)

        def group(g, c):
            gu = _dot(hs, wgu_ref[g])
            acts = []
            for e in range(EXP_PER_GROUP):
                ge = jnp.sum(jnp.where(lane_s == g * EXP_PER_GROUP + e + N_GROUPS, gs, 0.0), axis=-1, keepdims=True)
                up = gu[:, 2 * e * D_EXPERT:(2 * e + 1) * D_EXPERT]
                acts.append((up * jax.nn.sigmoid(up) * gu[:, (2 * e + 1) * D_EXPERT:(2 * e + 2) * D_EXPERT] * ge).astype(BF16))
            ys_ref[rows, :] += _dot(jnp.concatenate(acts, axis=1), wdn_ref[g])
            return c

        lax.fori_loop(g_first, g_last + 1, group, 0)
        return carry

    lax.fori_loop(0, tm // sub, slab, 0)
    o_ref[...] = x + _dot(to_sorted_t.astype(BF16), ys_ref[...].astype(BF16))


def _moe(x2, gain, wr_hi, wr_lo, br, wgu, wdn, tm=512, sub=128):
    n, d = x2.shape
    full = lambda a: pl.BlockSpec(a.shape, lambda i: (0,) * a.ndim)
    once = lambda a: pl.BlockSpec(a.shape, lambda i: (0,) * a.ndim, pipeline_mode=pl.Buffered(1))
    return pl.pallas_call(
        functools.partial(_moe_kernel, tm=tm, sub=sub),
        out_shape=jax.ShapeDtypeStruct((n, d), F32),
        grid=(n // tm,),
        in_specs=[pl.BlockSpec((tm, d), lambda i: (i, 0)), full(gain), full(wr_hi), full(wr_lo), full(br),
                  once(wgu), once(wdn)],
        out_specs=pl.BlockSpec((tm, d), lambda i: (i, 0)),
        scratch_shapes=[pltpu.VMEM((tm, d), BF16), pltpu.VMEM((tm, LANES), F32), pltpu.VMEM((tm, d), F32)],
        compiler_params=_params("parallel"),
        name="moe",
    )(x2, gain, wr_hi, wr_lo, br, wgu, wdn)


def _rope_tables(positions):
    pos = positions.reshape(-1).astype(F32)[:, None]

    def cs(dim):
        inv = ROPE_THETA ** (-jnp.arange(0, dim, 2, dtype=F32) / dim)
        ang = pos * inv
        return jnp.cos(ang), jnp.sin(ang)

    c, s = cs(HEAD_DIM)
    ci, si = cs(IDX_DIM)
    return jnp.concatenate([c, s], 1), jnp.concatenate([ci, si, jnp.zeros_like(ci), jnp.zeros_like(ci)], 1)


def _reorder_in_weight(w):
    a = w[:, :4 * WIDTH]
    o = 4 * WIDTH
    qlat = w[:, o:o + Q_LORA]; o += Q_LORA
    k = w[:, o:o + WIDTH]; o += WIDTH
    v = w[:, o:o + WIDTH]; o += WIDTH
    misc = w[:, o:]
    pad = jnp.zeros((w.shape[0], LANES - misc.shape[1]), w.dtype)
    return jnp.concatenate([a, k, v, qlat, misc, pad], axis=1).astype(BF16)


def _row(v):
    return v.reshape(1, -1).astype(F32)


def kernel(x, mem, positions, norm_mix, w_in, lb_raw, a_gnorm, b_qlat_gain, b_wqb, b_wqidx, b_qnorm,
           b_knorm, b_kidx_norm, w_out, norm_x, norm_mem, x_wq, x_wk, x_wv, x_wo, x_qnorm, x_knorm,
           norm_ffn, w_rg, b_rg, w_re, b_re, w_gu, w_dn):
    bsz, seq, d = x.shape
    n_mem = mem.shape[1]
    depth = w_in.shape[0]
    assert d == D_MODEL and seq % (2 * DSA_KEY_TILE) == 0 and n_mem % LANES == 0, (x.shape, mem.shape)
    x2 = x.reshape(bsz * seq, d)
    mem2 = mem.reshape(bsz * n_mem, d)
    tabs = _rope_tables(positions)
    pad_lanes = lambda v: jnp.pad(v, ((0, 0), (0, LANES - v.shape[1])))
    for l in range(depth):
        u_a, q, k, vt, qi, ki, wt = _proj_prep(
            x2, _row(norm_mix[l]), _reorder_in_weight(w_in[l]), tabs, b_wqb[l].astype(BF16),
            b_wqidx[l].astype(BF16), _row(b_qlat_gain[l]), _row(b_qnorm[l]), _row(b_knorm[l]),
            pad_lanes(_row(b_kidx_norm[l])))
        o_a = _hgrn(u_a, lb_raw.astype(F32), _row(a_gnorm[l]), l, bsz, seq)
        o_b = _dsa_attn(q, k, vt, qi, ki, wt, bsz, seq)
        km, vm = _mem_kv(mem2, _row(norm_mem[l]), x_wk[l].astype(BF16), x_wv[l].astype(BF16),
                         _row(x_knorm[l]), n_mem)
        x2 = _xattn(x2, o_a, o_b, w_out[l].astype(BF16), _row(norm_x[l]), x_wq[l].astype(BF16),
                    _row(x_qnorm[l]), km, vm, x_wo[l].astype(BF16), seq, n_mem)
        wr = pad_lanes(jnp.concatenate([w_rg[l], w_re[l].reshape(d, N_EXPERTS)], axis=1).astype(F32))
        wr_hi = wr.astype(BF16)
        wr_lo = (wr - wr_hi.astype(F32)).astype(BF16)
        br = pad_lanes(jnp.concatenate([b_rg[l], b_re[l].reshape(-1)]).reshape(1, -1).astype(F32))
        x2 = _moe(x2, _row(norm_ffn[l]), wr_hi, wr_lo, br,
                  w_gu[l].transpose(0, 2, 1, 3).reshape(N_GROUPS, d, EXP_PER_GROUP * 2 * D_EXPERT).astype(BF16),
                  w_dn[l].reshape(N_GROUPS, EXP_PER_GROUP * D_EXPERT, d).astype(BF16))
    return x2.reshape(bsz, seq, d)
```

```python
import functools

import numpy as np
import jax
import jax.numpy as jnp
from jax import lax
from jax.experimental import pallas as pl
from jax.experimental.pallas import tpu as pltpu

F32 = jnp.float32
BF16 = jnp.bfloat16
I32 = jnp.int32
I16 = jnp.int16

EPS = 1e-6
ROPE_THETA = 10000.0
CHUNK = 64
CHUNK_SHIFT = CHUNK.bit_length() - 1
SUB = 16
SUB_SHIFT = SUB.bit_length() - 1
N_HEADS = 4
HEAD_DIM = 128
WIDTH = N_HEADS * HEAD_DIM
Q_LORA = 256
D_MODEL = 1024
IDX_HEADS = 16
IDX_DIM = 64
ROT_I = IDX_DIM // 2
TOPK_MAX = 256
N_GROUPS = 4
EXP_PER_GROUP = 4
N_EXPERTS = N_GROUPS * EXP_PER_GROUP
D_EXPERT = 256
LANES = 128
INT_MIN = -2 ** 31
INT16_MIN, INT16_MAX = -2 ** 15, 2 ** 15 - 1
HALF_BITS = 16
PACK = 16
LOG2E = 1.4426950408889634
VT_ROWS = HEAD_DIM + PACK
DSA_KEY_TILE = 512
SCORE_ROWS = 128
POOL_ROWS = 256
SURE_PASSES = 8

U_K = 4 * WIDTH
U_COLS = U_K + 2 * WIDTH + Q_LORA + LANES

VMEM_LIMIT = 56 * 1024 * 1024


def _rms(x, gain):
    return x * lax.rsqrt(jnp.mean(x * x, axis=-1, keepdims=True) + EPS) * gain


def _dot(a, b):
    return jnp.dot(a, b, preferred_element_type=F32)


def _rms_head(x, gain, width=HEAD_DIM):
    ones = (lax.broadcasted_iota(I32, (LANES, LANES), 0) < width).astype(BF16)
    ss = _dot((x * x).astype(BF16), ones)
    return x * lax.rsqrt(ss * (1.0 / width) + EPS) * gain


def _dot_nt(a, b):
    return lax.dot_general(a, b, (((1,), (1,)), ((), ())), preferred_element_type=F32)


def _params(*sem):
    return pltpu.CompilerParams(dimension_semantics=sem, vmem_limit_bytes=VMEM_LIMIT)


def _split3(x):
    hi = x.astype(BF16)
    r1 = x - hi.astype(F32)
    mid = r1.astype(BF16)
    lo = (r1 - mid.astype(F32)).astype(BF16)
    return hi, mid, lo


def _hgrn_kernel(q_ref, f_ref, i_ref, g_ref, lbraw_ref, gn_ref, e_ref, o_ref, st_ref,
                 *, layer, n_chunks, unroll):
    @pl.when(pl.program_id(2) == 0)
    def _():
        st_ref[...] = jnp.zeros_like(st_ref)

    lr = lbraw_ref[...]
    ex = jnp.exp(lr - jnp.max(lr, axis=0, keepdims=True))
    sm = ex / jnp.sum(ex, axis=0, keepdims=True)
    lb = jnp.zeros((1, HEAD_DIM), F32)
    for r in range(1, layer + 1):
        lb = lb + sm[r:r + 1, :]
    log_lb = jnp.log(lb)
    log_1mlb = jnp.log(1.0 - lb)

    row = lax.broadcasted_iota(I32, (CHUNK, HEAD_DIM), 0)
    col = lax.broadcasted_iota(I32, (CHUNK, HEAD_DIM), 1)
    row_l = row & (SUB - 1)
    row_b = row >> SUB_SHIFT
    col_b = col >> SUB_SHIFT
    tri = (lax.broadcasted_iota(I32, (CHUNK, CHUNK), 0)
           >= lax.broadcasted_iota(I32, (CHUNK, CHUNK), 1)).astype(BF16)
    n_sub = CHUNK // SUB
    zeros_c = jnp.zeros((CHUNK, HEAD_DIM), F32)

    def sub_bcast(t, s):
        t4 = t.reshape(n_sub, SUB, HEAD_DIM)
        return jnp.broadcast_to(t4[:, s:s + 1, :], (n_sub, SUB, HEAD_DIM)).reshape(CHUNK, HEAD_DIM)

    def gates(z):
        ls = jnp.minimum(z, 0.0) - jnp.log(1.0 + jnp.exp(-jnp.abs(z)))
        if layer == 0:
            return ls, (ls - z) * LOG2E
        cc = log_1mlb + ls
        log_f = jnp.maximum(log_lb, cc) + jnp.log(1.0 + jnp.exp(-jnp.abs(log_lb - cc)))
        return log_f, (cc - z) * LOG2E

    def diag_terms(q, b2, c2):
        pieces = []
        for s in range(SUB):
            d = jnp.where(row_l >= s, b2 - sub_bcast(c2, s), -jnp.inf)
            pieces.append((q * jnp.exp2(d)).astype(BF16))
        return jnp.concatenate(pieces, axis=1)

    def below_keys(b2, c2):
        kts = []
        for i in range(1, n_sub):
            r_i = b2[i * SUB:i * SUB + 1, :]
            kts.append(jnp.exp2(jnp.where(row < i * SUB, r_i - c2, -jnp.inf)))
            kts.append(zeros_c)
        return jnp.concatenate(kts, axis=0).astype(BF16)

    def below_scores(r):
        a = jnp.zeros((CHUNK, HEAD_DIM), F32)
        for i in range(1, n_sub):
            a = a + jnp.where(row_b == i, r[:, (i - 1) * LANES:i * LANES], 0.0)
        return a

    def chunks(it, carry):
        cs = range(unroll)
        rows = [pl.ds(pl.multiple_of((it * unroll + c) * CHUNK, CHUNK), CHUNK) for c in cs]
        qr = [q_ref[r, :] for r in rows]
        v = [i_ref[r, :] for r in rows]
        q = [x * jax.nn.sigmoid(x) for x in qr]
        gt = [gates(f_ref[r, :]) for r in rows]
        b2 = []
        for c in cs:
            hi, mid, lo = _split3(gt[c][0])
            b2.append((_dot(tri, hi) + _dot(tri, mid) + _dot(tri, lo)) * LOG2E)
        c2 = [b2[c] - gt[c][1] for c in cs]
        b_last = [x[CHUNK - 1:CHUNK, :] for x in b2]
        a2 = [_dot(diag_terms(q[c], b2[c], c2[c]), e_ref[...]) for c in cs]
        rr = [_dot_nt((q[c] * jnp.exp2(b2[c] - sub_bcast(b2[c], 0))).astype(BF16), below_keys(b2[c], c2[c]))
              for c in cs]
        upd = [_dot(v[c].T.astype(BF16), jnp.exp2(b_last[c] - c2[c]).astype(BF16)) for c in cs]
        qe = [(q[c] * jnp.exp2(b2[c])).astype(BF16) for c in cs]
        st = st_ref[...]
        o_inter = []
        for c in cs:
            o_inter.append(_dot_nt(qe[c], st.astype(BF16)))
            st = st * jnp.exp2(b_last[c]) + upd[c]
        st_ref[...] = st
        a = [jnp.where(col_b == row_b, a2[c], 0.0) + below_scores(rr[c]) for c in cs]
        o_intra = [_dot(a[c][:, :CHUNK].astype(BF16), v[c].astype(BF16)) for c in cs]
        for c in cs:
            g = g_ref[rows[c], :]
            o_ref[rows[c], :] = _rms(o_inter[c] + o_intra[c], gn_ref[...]) * (g * jax.nn.sigmoid(g))
        return carry

    lax.fori_loop(0, n_chunks // unroll, chunks, 0)


def _hgrn_selector():
    e = np.zeros((SUB * HEAD_DIM, LANES), np.float32)
    s_of_row = np.arange(SUB * HEAD_DIM) // HEAD_DIM
    cols = np.arange(LANES)
    e[:, :] = ((cols[None, :] % SUB) == s_of_row[:, None]) & (cols[None, :] < CHUNK)
    return jnp.asarray(e, BF16)


def _hgrn(u, lb_raw, gn, layer, bsz, seq, tb=1024):
    n = bsz * seq
    nb = seq // tb
    blk = lambda k: pl.BlockSpec((tb, HEAD_DIM), lambda b, h, j, k=k: (b * nb + j, h + N_HEADS * k))
    n_layers = lb_raw.shape[0]
    return pl.pallas_call(
        functools.partial(_hgrn_kernel, layer=layer, n_chunks=tb // CHUNK, unroll=16),
        out_shape=jax.ShapeDtypeStruct((n, WIDTH), F32),
        grid=(bsz, N_HEADS, nb),
        in_specs=[blk(0), blk(1), blk(2), blk(3),
                  pl.BlockSpec((n_layers, HEAD_DIM), lambda b, h, j: (0, h)),
                  pl.BlockSpec((1, HEAD_DIM), lambda b, h, j: (0, 0)),
                  pl.BlockSpec((SUB * HEAD_DIM, LANES), lambda b, h, j: (0, 0))],
        out_specs=pl.BlockSpec((tb, HEAD_DIM), lambda b, h, j: (b * nb + j, h)),
        scratch_shapes=[pltpu.VMEM((HEAD_DIM, HEAD_DIM), F32)],
        compiler_params=_params("parallel", "parallel", "arbitrary"),
        name="hgrn",
    )(u, u, u, u, lb_raw, gn, _hgrn_selector())


def _rope128(x, cos, sin_signed):
    return x * cos + pltpu.roll(x, HEAD_DIM // 2, 1) * sin_signed


def _rope64(x, cos, sin_a, sin_b):
    return x * cos + pltpu.roll(x, LANES - ROT_I, 1) * sin_a + pltpu.roll(x, ROT_I, 1) * sin_b


def _proj_prep_kernel(x_ref, g_ref, w_ref, ta_ref, tb_ref, wqb_ref, wqi_ref, gql_ref, gqn_ref, gkn_ref, gki_ref,
                      ua_out, q_out, k_out, vt_out, qi_out, ki_out, wt_out):
    hb = _rms(x_ref[...], g_ref[...]).astype(BF16)
    ub = _dot(hb, w_ref[:, U_K:])
    k, v = ub[:, :WIDTH], ub[:, WIDTH:2 * WIDTH]
    ql, misc = ub[:, 2 * WIDTH:2 * WIDTH + Q_LORA], ub[:, 2 * WIDTH + Q_LORA:]

    def project(c):
        ua_out[:, c * WIDTH:(c + 1) * WIDTH] = _dot(hb, w_ref[:, c * WIDTH:(c + 1) * WIDTH])

    ta, tb = ta_ref[...], tb_ref[...]
    lane = lax.broadcasted_iota(I32, ta.shape, 1)
    ta_swapped = pltpu.roll(ta, HEAD_DIM // 2, 1)
    cm = jnp.where(lane < HEAD_DIM // 2, ta, ta_swapped)
    sm = jnp.where(lane < HEAD_DIM // 2, -ta_swapped, ta)
    quarter = lane >> (ROT_I.bit_length() - 1)
    r32, r64, r96 = (pltpu.roll(tb, sh * ROT_I, 1) for sh in (1, 2, 3))
    ci = jnp.where(quarter == 0, tb, jnp.where(quarter == 1, r32, jnp.where(quarter == 2, r64, r96)))
    si = jnp.where(quarter == 0, r96, jnp.where(quarter == 1, tb, jnp.where(quarter == 2, r32, r64)))
    sia = jnp.where((lane & ROT_I) == 0, -si, 0.0)
    sib = jnp.where((lane & ROT_I) != 0, si, 0.0)
    cb = _rms(ql, gql_ref[...]).astype(BF16)
    scale = HEAD_DIM ** -0.5

    def prep_q():
        qf = _dot(cb, wqb_ref[...])
        for h in range(N_HEADS):
            sl = slice(h * HEAD_DIM, (h + 1) * HEAD_DIM)
            qh = _rope128(_rms_head(qf[:, sl], gqn_ref[...]), cm, sm)
            q_out[:, sl] = (qh * (scale * LOG2E)).astype(BF16)

    def prep_k():
        for h in range(N_HEADS):
            sl = slice(h * HEAD_DIM, (h + 1) * HEAD_DIM)
            kh = _rope128(_rms_head(k[:, sl], gkn_ref[...]), cm, sm)
            k_out[:, sl] = kh.astype(BF16)

    def prep_v():
        vt = v.T.astype(BF16)
        ones = jnp.ones((PACK, vt.shape[1]), BF16)
        for h in range(N_HEADS):
            vt_out[0, h * VT_ROWS:h * VT_ROWS + HEAD_DIM, :] = vt[h * HEAD_DIM:(h + 1) * HEAD_DIM, :]
            vt_out[0, h * VT_ROWS + HEAD_DIM:(h + 1) * VT_ROWS, :] = ones

    def prep_index(groups):
        qi = _dot(cb, wqi_ref[:, groups[0] * LANES:(groups[-1] + 1) * LANES])
        for n_, p in enumerate(groups):
            qi_out[:, p * LANES:(p + 1) * LANES] = _rope64(qi[:, n_ * LANES:(n_ + 1) * LANES], ci, sia, sib).astype(BF16)

    def prep_keys_index():
        kn = _rope64(_rms_head(misc, gki_ref[...], IDX_DIM), ci, sia, sib)
        ki_out[...] = (kn + pltpu.roll(kn, IDX_DIM, 1)).astype(BF16)
        wt = (misc * (IDX_HEADS ** -0.5 * IDX_DIM ** -0.5)).T
        wt_out[...] = wt[IDX_DIM:IDX_DIM + IDX_HEADS, :]

    prep_q()
    project(0)
    prep_v()
    prep_index(range(0, 4))
    project(1)
    prep_k()
    prep_index(range(4, 8))
    project(2)
    prep_keys_index()
    project(3)


def _proj_prep(x2, gain, w_bf, tabs, wqb, wqi, gql, gqn, gkn, gki, tm=DSA_KEY_TILE):
    n, d = x2.shape
    nt = n // tm
    row = lambda w: pl.BlockSpec((tm, w), lambda i: (i, 0))
    full = lambda a: pl.BlockSpec(a.shape, lambda i: (0,) * a.ndim)
    return pl.pallas_call(
        _proj_prep_kernel,
        out_shape=(jax.ShapeDtypeStruct((n, U_K), F32),
                   jax.ShapeDtypeStruct((n, WIDTH), BF16),
                   jax.ShapeDtypeStruct((n, WIDTH), BF16),
                   jax.ShapeDtypeStruct((nt, N_HEADS * VT_ROWS, tm), BF16),
                   jax.ShapeDtypeStruct((n, IDX_HEADS * IDX_DIM), BF16),
                   jax.ShapeDtypeStruct((n, LANES), BF16),
                   jax.ShapeDtypeStruct((IDX_HEADS, n), F32)),
        grid=(nt,),
        in_specs=[row(d), full(gain), pl.BlockSpec(w_bf.shape, lambda i: (0, 0), pipeline_mode=pl.Buffered(1)),
                  row(LANES), row(LANES),
                  full(wqb), full(wqi), full(gql), full(gqn), full(gkn), full(gki)],
        out_specs=(row(U_K), row(WIDTH), row(WIDTH),
                   pl.BlockSpec((1, N_HEADS * VT_ROWS, tm), lambda i: (i, 0, 0)),
                   row(IDX_HEADS * IDX_DIM), row(LANES),
                   pl.BlockSpec((IDX_HEADS, tm), lambda i: (0, i))),
        compiler_params=_params("parallel"),
        name="proj_prep",
    )(x2, gain, w_bf, *tabs, wqb, wqi, gql, gqn, gkn, gki)


def _sortable(x):
    bits = lax.bitcast_convert_type(x, I32)
    return bits ^ ((bits >> 31) & 0x7FFFFFFF)


def _dsa_attn_kernel(qi_ref, wt_ref, qin_ref, wtn_ref, q_ref, ki_ref, k_ref, vt_ref, o_ref,
                     qm_ref, hi_ref, lo_ref, kb_ref, pk_ref, lt_ref, acc0, acc1, acc2, acc3,
                     *, tq, kt, top_k, n_blocks):
    acc_refs = (acc0, acc1, acc2, acc3)
    blk = pl.program_id(1)
    cur = blk & 1
    nxt = 1 - cur
    q0 = blk * tq
    n_keys = q0 + tq
    sub = SCORE_ROWS
    key_iota = lax.broadcasted_iota(I32, (sub, tq), 0)
    lane_q = lax.broadcasted_iota(I32, (tq, LANES), 1)
    pooled_per_group = 2 * PACK

    def sweep(total, fn, carry):
        n_full = total // kt
        carry = lax.fori_loop(0, n_full // 2, lambda jp, c: fn(pl.multiple_of(jp * 2 * kt, 2 * kt), 2 * kt, c), carry)
        carry = lax.cond((n_full & 1) == 1, lambda c: fn(pl.multiple_of((n_full - 1) * kt, kt), kt, c),
                         lambda c: c, carry)
        return lax.cond(total % kt != 0, lambda c: fn(pl.multiple_of(n_full * kt, kt), kt // 2, c),
                        lambda c: c, carry)

    def limit_of(q_start):
        qpos = q_start + lax.broadcasted_iota(I32, (1, tq), 1)
        return ((qpos >> CHUNK_SHIFT) + 1) << CHUNK_SHIFT

    def build_qm(src_ref):
        for h in range(IDX_HEADS):
            src = src_ref[:, (h // 2) * LANES:(h // 2 + 1) * LANES].astype(F32)
            keep = (lane_q < IDX_DIM) if h % 2 == 0 else (lane_q >= IDX_DIM)
            qm_ref[h * tq:(h + 1) * tq, :] = jnp.where(keep, src, 0.0).astype(BF16)

    def score_rows(start, size, slot, limit, w_ref):
        for part in range(size // sub):
            base = start + part * sub
            rows = pl.ds(pl.multiple_of(base, sub), sub)
            ki = ki_ref[rows, :]
            acc = jnp.zeros((sub, tq), F32)
            for h in range(IDX_HEADS):
                x = _dot_nt(ki, qm_ref[h * tq:(h + 1) * tq, :])
                acc = acc + w_ref[h:h + 1, :] * jnp.maximum(x, 0.0)
            key = jnp.where(base + key_iota < limit, _sortable(acc), INT_MIN)
            hi_ref[slot, rows, :] = (key >> HALF_BITS).astype(I16)
            lo_ref[slot, rows, :] = ((key & (2 ** HALF_BITS - 1)) + INT16_MIN).astype(I16)

    @pl.when(blk == 0)
    def _():
        build_qm(qi_ref)
        limit0 = limit_of(0)

        def body(start, size, carry):
            score_rows(start, size, 0, limit0, wt_ref)
            return carry
        sweep(n_keys, body, 0)

    one, zero = jnp.ones((), BF16), jnp.zeros((), BF16)
    neg_inf = jnp.full((), -jnp.inf, BF16)
    hi_rows = lambda rows: hi_ref[cur, rows, :]
    lo_rows = lambda rows: lo_ref[cur, rows, :]
    kb_rows = lambda rows: kb_ref[rows, :]

    def count16(get, trial, strict, total=None):
        t16 = jnp.broadcast_to(trial.astype(I16), (PACK, tq))

        def hits(start, size, c):
            x = get(pl.ds(start, size))
            parts = [jnp.zeros((PACK, tq), BF16) for _ in range(4)]
            for r in range(size // PACK):
                xr = x[r * PACK:(r + 1) * PACK, :]
                parts[r % 4] = parts[r % 4] + jnp.where((xr > t16) if strict else (xr >= t16), one, zero)
            return c + ((parts[0] + parts[1]) + (parts[2] + parts[3])).astype(F32)

        c = sweep(n_keys if total is None else total, hits, jnp.zeros((PACK, tq), F32))
        return jnp.sum(c, axis=0, keepdims=True)

    def bisect16(get, want, n_all, early_exit, total=None):
        def step(it, carry):
            lo, n_lo, n_up = carry
            trial = lo + lax.shift_left(jnp.int32(1), HALF_BITS - 1 - it)
            n = count16(get, trial, False, total)
            ok = n >= want
            return jnp.where(ok, trial, lo), jnp.where(ok, n, n_lo), jnp.where(ok, n_up, n)

        init = (jnp.full((1, tq), INT16_MIN, I32), n_all, jnp.zeros((1, tq), F32))
        if not early_exit:
            return lax.fori_loop(0, HALF_BITS, step, init)

        def unresolved(carry):
            it, _, n_lo, _ = carry
            return (it < HALF_BITS) & (jnp.max(jnp.where(n_lo != want, 1.0, 0.0)) > 0.0)

        def two_steps(carry):
            return (carry[0] + 2,) + step(carry[0] + 1, step(carry[0], carry[1:]))

        return lax.while_loop(unresolved, two_steps, (jnp.int32(SURE_PASSES),) + lax.fori_loop(0, SURE_PASSES, step, init))[1:]

    tau_hi, n_hi_ge, n_hi_gt = bisect16(hi_rows, float(top_k), jnp.full((1, tq), n_keys.astype(F32), F32), False)
    want_lo = top_k - n_hi_gt
    tau_hi_row = tau_hi.astype(I16)

    min16 = jnp.full((PACK, tq), INT16_MIN, I16)

    def bucket_rows(start, size, carry):
        rows = pl.ds(start, size)
        kb = jnp.where(hi_rows(rows) == jnp.broadcast_to(tau_hi_row, (size, tq)), lo_rows(rows),
                       jnp.full((), INT16_MIN, I16))
        kb_ref[rows, :] = kb
        for g in range(size // POOL_ROWS):
            top1, top2 = min16, min16
            for r in range(POOL_ROWS // PACK):
                x = kb[g * POOL_ROWS + r * PACK:g * POOL_ROWS + (r + 1) * PACK, :]
                below = x < top1
                runner = jnp.where(below, x, top1)
                top2 = jnp.where(runner > top2, runner, top2)
                top1 = jnp.where(below, top1, x)
            base = pl.multiple_of((start // POOL_ROWS + g) * pooled_per_group, pooled_per_group)
            pk_ref[pl.ds(base, PACK), :] = top1
            pk_ref[pl.ds(base + PACK, PACK), :] = top2
        return carry

    sweep(n_keys, bucket_rows, 0)
    n_groups = n_keys // POOL_ROWS
    pooled_total = (n_groups * pooled_per_group + kt // 2 - 1) // (kt // 2) * (kt // 2)

    def pad_pooled(g, carry):
        pk_ref[pl.ds(pl.multiple_of(g * pooled_per_group, pooled_per_group), pooled_per_group), :] = jnp.full(
            (pooled_per_group, tq), INT16_MIN, I16)
        return carry

    lax.fori_loop(n_groups, pooled_total // pooled_per_group, pad_pooled, 0)
    n_bucket = n_hi_ge - n_hi_gt
    pk_rows = lambda rows: pk_ref[rows, :]
    tau_p, _, _ = bisect16(pk_rows, want_lo, n_bucket, True, pooled_total)
    n_ge_p = jnp.where(tau_p == INT16_MIN, n_bucket, count16(kb_rows, tau_p, False))
    n_gt_p = count16(kb_rows, tau_p, True)
    pooled_ok = (n_ge_p == want_lo) | ((n_ge_p > want_lo) & (n_gt_p < want_lo))

    def full_search():
        t, n_ge, _ = bisect16(kb_rows, want_lo, n_bucket, True)
        return t, n_ge, count16(kb_rows, t, True)

    tau_lo, n_kb_ge, n_kb_gt = lax.cond(
        jnp.min(jnp.where(pooled_ok, 1.0, 0.0)) > 0.0,
        lambda: (tau_p, n_ge_p, n_gt_p), full_search)
    live = (tau_hi > INT16_MIN) | (tau_lo > INT16_MIN)
    excess = jnp.max(jnp.where(live, n_kb_ge - want_lo, 0.0))

    @pl.when(excess > 0)
    def _():
        need = want_lo - n_kb_gt
        live_f = jnp.where(live, 1.0, 0.0)
        step_rows = kt // 2
        tri = (lax.broadcasted_iota(I32, (step_rows, step_rows), 0)
               >= lax.broadcasted_iota(I32, (step_rows, step_rows), 1)).astype(BF16)

        def body(start, size, seen):
            for part in range(size // step_rows):
                rows = pl.ds(pl.multiple_of(start + part * step_rows, step_rows), step_rows)
                h32 = hi_rows(rows).astype(I32)
                eq = jnp.where(h32 == tau_hi, jnp.where(lo_rows(rows).astype(I32) == tau_lo, 1.0, 0.0), 0.0)
                rank = _dot(tri, eq.astype(BF16)) + seen
                dropped = eq * (1.0 - jnp.where(rank <= need, live_f, 0.0))
                hi_ref[cur, rows, :] = jnp.where(dropped > 0.0, INT16_MIN, h32).astype(I16)
                seen = rank[step_rows - 1:step_rows, :]
            return seen
        sweep(n_keys, body, jnp.zeros((1, tq), F32))

    tau_lo_row = jnp.where(live, tau_lo, INT16_MAX).astype(I16)

    def bias_rows(rows, size):
        h16 = hi_rows(rows)
        th = jnp.broadcast_to(tau_hi_row, (size, tq))
        in_bucket = jnp.where(lo_rows(rows) >= jnp.broadcast_to(tau_lo_row, (size, tq)), zero, neg_inf)
        return jnp.where(h16 > th, zero, jnp.where(h16 == th, in_bucket, neg_inf)).astype(F32)

    for a in acc_refs:
        a[...] = jnp.zeros_like(a)
    heads = [slice(h * HEAD_DIM, (h + 1) * HEAD_DIM) for h in range(N_HEADS)]
    limit_next = limit_of(q0 + tq)

    def attend(start, size, ms, score_next):
        segs = [(start, kt), (start + kt, kt)] if size == 2 * kt else [(start, size)]
        for i, (s0, n) in enumerate(segs):
            rows = pl.ds(pl.multiple_of(s0, kt // 2), n)
            bias = bias_rows(rows, n)
            for h in range(N_HEADS):
                lt_ref[i, h, :n, :] = _dot_nt(k_ref[rows, heads[h]], q_ref[:, heads[h]]) + bias
        ms = list(ms)
        for i, (s0, n) in enumerate(segs):
            if score_next:
                score_rows(s0, n, nxt, limit_next, wtn_ref)
            alphas, ps = [], []
            for h in range(N_HEADS):
                lt = lt_ref[i, h, :n, :]
                m_new = jnp.maximum(ms[h], jnp.max(lt, axis=0, keepdims=True))
                m_safe = jnp.where(m_new == -jnp.inf, 0.0, m_new)
                ps.append(jnp.exp2(lt - m_safe).astype(BF16))
                alphas.append(jnp.exp2(ms[h] - m_safe))
                ms[h] = m_new
            for h in range(N_HEADS):
                acc_refs[h][...] = (acc_refs[h][...] * alphas[h]
                                    + _dot(vt_ref[s0 // kt, h * VT_ROWS:(h + 1) * VT_ROWS, :n], ps[h]))
        return tuple(ms)

    m_init = tuple(jnp.full((1, tq), -jnp.inf, F32) for _ in range(N_HEADS))

    @pl.when(blk < n_blocks - 1)
    def _():
        build_qm(qin_ref)
        sweep(n_keys, functools.partial(attend, score_next=True), m_init)
        score_rows(pl.multiple_of(n_keys, kt // 2), tq, nxt, limit_next, wtn_ref)

    @pl.when(blk == n_blocks - 1)
    def _():
        sweep(n_keys, functools.partial(attend, score_next=False), m_init)

    for h in range(N_HEADS):
        acc = acc_refs[h][...]
        out_t = acc[:HEAD_DIM, :] * (1.0 / acc[HEAD_DIM:HEAD_DIM + 1, :])
        o_ref[:, heads[h]] = out_t.T


def _dsa_attn(q, k, vt, qi, ki, wt, bsz, seq, tq=256, kt=DSA_KEY_TILE):
    n = bsz * seq
    nq = seq // tq
    top_k = min(TOPK_MAX, seq // 4)
    once = dict(pipeline_mode=pl.Buffered(1))
    nxt_blk = lambda b, i: b * nq + jnp.minimum(i + 1, nq - 1)
    return pl.pallas_call(
        functools.partial(_dsa_attn_kernel, tq=tq, kt=kt, top_k=top_k, n_blocks=nq),
        out_shape=jax.ShapeDtypeStruct((n, WIDTH), F32),
        grid=(bsz, nq),
        in_specs=[pl.BlockSpec((tq, IDX_HEADS * IDX_DIM), lambda b, i: (b * nq + i, 0)),
                  pl.BlockSpec((IDX_HEADS, tq), lambda b, i: (0, b * nq + i)),
                  pl.BlockSpec((tq, IDX_HEADS * IDX_DIM), lambda b, i: (nxt_blk(b, i), 0)),
                  pl.BlockSpec((IDX_HEADS, tq), lambda b, i: (0, nxt_blk(b, i))),
                  pl.BlockSpec((tq, WIDTH), lambda b, i: (b * nq + i, 0)),
                  pl.BlockSpec((seq, LANES), lambda b, i: (b, 0), **once),
                  pl.BlockSpec((seq, WIDTH), lambda b, i: (b, 0), **once),
                  pl.BlockSpec((seq // kt, N_HEADS * VT_ROWS, kt), lambda b, i: (b, 0, 0), **once)],
        out_specs=pl.BlockSpec((tq, WIDTH), lambda b, i: (b * nq + i, 0)),
        scratch_shapes=[pltpu.VMEM((IDX_HEADS * tq, LANES), BF16),
                        pltpu.VMEM((2, seq, tq), I16), pltpu.VMEM((2, seq, tq), I16), pltpu.VMEM((seq, tq), I16),
                        pltpu.VMEM((pl.cdiv(seq // POOL_ROWS * 2 * PACK, kt) * kt, tq), I16),
                        pltpu.VMEM((2, N_HEADS, kt, tq), F32)]
                       + [pltpu.VMEM((VT_ROWS, tq), F32) for _ in range(N_HEADS)],
        compiler_params=_params("arbitrary", "arbitrary"),
        name="dsa_attn",
    )(qi, wt, qi, wt, q, ki, k, vt)


def _mem_kv_kernel(mem_ref, g_ref, wk_ref, wv_ref, gk_ref, k_out, v_out):
    mb = _rms(mem_ref[...], g_ref[...]).astype(BF16)
    kf = _dot(mb, wk_ref[...])
    for h in range(N_HEADS):
        sl = slice(h * HEAD_DIM, (h + 1) * HEAD_DIM)
        k_out[:, sl] = _rms_head(kf[:, sl], gk_ref[...]).astype(BF16)
    v_out[...] = _dot(mb, wv_ref[...]).astype(BF16)


def _mem_kv(mem2, gain, wk, wv, gk, n_mem):
    n, d = mem2.shape
    full = lambda a: pl.BlockSpec(a.shape, lambda b: (0,) * a.ndim)
    return pl.pallas_call(
        _mem_kv_kernel,
        out_shape=(jax.ShapeDtypeStruct((n, WIDTH), BF16), jax.ShapeDtypeStruct((n, WIDTH), BF16)),
        grid=(n // n_mem,),
        in_specs=[pl.BlockSpec((n_mem, d), lambda b: (b, 0)), full(gain), full(wk), full(wv), full(gk)],
        out_specs=(pl.BlockSpec((n_mem, WIDTH), lambda b: (b, 0)),
                   pl.BlockSpec((n_mem, WIDTH), lambda b: (b, 0))),
        compiler_params=_params("parallel"),
        name="mem_kv",
    )(mem2, gain, wk, wv, gk)


def _xattn_kernel(x_ref, oa_ref, ob_ref, wout_ref, g_ref, wq_ref, gq_ref, km_ref, vm_ref, wo_ref, o_ref):
    x1 = (x_ref[...] + _dot(oa_ref[...].astype(BF16), wout_ref[:WIDTH, :])
          + _dot(ob_ref[...].astype(BF16), wout_ref[WIDTH:, :]))
    hb = _rms(x1, g_ref[...]).astype(BF16)
    qf = _dot(hb, wq_ref[...])
    scale = HEAD_DIM ** -0.5
    outs = []
    for h in range(N_HEADS):
        sl = slice(h * HEAD_DIM, (h + 1) * HEAD_DIM)
        qh = (_rms(qf[:, sl], gq_ref[...]) * scale).astype(BF16)
        logits = _dot_nt(qh, km_ref[:, sl])
        p = jnp.exp(logits - jnp.max(logits, axis=-1, keepdims=True))
        oh = _dot(p.astype(BF16), vm_ref[:, sl])
        outs.append((oh * (1.0 / jnp.sum(p, axis=-1, keepdims=True))).astype(BF16))
    o_ref[...] = x1 + _dot(jnp.concatenate(outs, axis=1), wo_ref[...])


def _xattn(x2, oa, ob, wout, gain, wq, gq, km, vm, wo, seq, n_mem, tm=1024):
    n, d = x2.shape
    per_b = seq // tm
    row = lambda w: pl.BlockSpec((tm, w), lambda i: (i, 0))
    full = lambda a: pl.BlockSpec(a.shape, lambda i: (0,) * a.ndim)
    memb = pl.BlockSpec((n_mem, WIDTH), lambda i: (i // per_b, 0))
    return pl.pallas_call(
        _xattn_kernel,
        out_shape=jax.ShapeDtypeStruct((n, d), F32),
        grid=(n // tm,),
        in_specs=[row(d), row(WIDTH), row(WIDTH), full(wout), full(gain), full(wq), full(gq),
                  memb, memb, full(wo)],
        out_specs=row(d),
        compiler_params=_params("parallel"),
        name="xattn",
    )(x2, oa, ob, wout, gain, wq, gq, km, vm, wo)


def _moe_kernel(x_ref, g_ref, wr_hi_ref, wr_lo_ref, br_ref, wgu_ref, wdn_ref, o_ref, hs_ref, gs_ref, ys_ref,
                *, tm, sub):
    lane = lax.broadcasted_iota(I32, (tm, LANES), 1)
    lane_f = lane.astype(F32)
    x = x_ref[...]
    h = _rms(x, g_ref[...])
    h_hi = h.astype(BF16)
    h_lo = (h - h_hi.astype(F32)).astype(BF16)
    lg = (_dot(h_hi, wr_hi_ref[...]) + _dot(h_lo, wr_hi_ref[...]) + _dot(h_hi, wr_lo_ref[...])
          + br_ref[...])
    first = lambda cond: jnp.min(jnp.where(cond, lane_f, 1e9), axis=-1, keepdims=True)
    gl = jnp.where(lane < N_GROUPS, lg, -jnp.inf)
    gmax = jnp.max(gl, axis=-1, keepdims=True)
    gsel = first(gl == gmax)
    g_w = 1.0 / jnp.sum(jnp.exp(gl - gmax), axis=-1, keepdims=True)
    grp_of_lane = ((lane - N_GROUPS) >> (EXP_PER_GROUP.bit_length() - 1)).astype(F32)
    in_grp = (lane >= N_GROUPS) & (lane < N_GROUPS + N_EXPERTS) & (grp_of_lane == gsel)
    el = jnp.where(in_grp, lg, -jnp.inf)
    v1 = jnp.max(el, axis=-1, keepdims=True)
    i1 = first(el == v1)
    el2 = jnp.where(lane_f == i1, -jnp.inf, el)
    v2 = jnp.max(el2, axis=-1, keepdims=True)
    i2 = first(el2 == v2)
    r = jnp.exp(v2 - v1)
    w1 = 1.0 / (1.0 + r)
    gate = jnp.where(lane_f == i1, w1 * g_w, jnp.where(lane_f == i2, r * w1 * g_w, 0.0))

    onehot = jnp.where(lane_f == gsel, 1.0, 0.0)
    t_row = lax.broadcasted_iota(I32, (tm, tm), 0)
    t_col = lax.broadcasted_iota(I32, (tm, tm), 1)
    before = _dot(jnp.where(t_row > t_col, 1.0, 0.0).astype(BF16), onehot.astype(BF16))
    counts = jnp.sum(onehot, axis=0, keepdims=True)
    lane1 = lax.broadcasted_iota(I32, (1, LANES), 1)
    ends = []
    run = jnp.zeros((1, 1), F32)
    for g in range(N_GROUPS - 1):
        run = run + jnp.sum(jnp.where(lane1 == g, counts, 0.0), axis=-1, keepdims=True)
        ends.append(run)
    start_of = sum(jnp.where(lane1 == g + 1, ends[g], 0.0) for g in range(N_GROUPS - 1))
    pos = jnp.sum(onehot * (before + start_of), axis=-1, keepdims=True)
    to_sorted_t = jnp.where(pos == t_col.astype(F32), 1.0, 0.0)
    to_sorted = to_sorted_t.T.astype(BF16)
    hs_ref[...] = _dot(to_sorted, h_hi).astype(BF16)
    g_hi, g_mid, g_lo = _split3(gate)
    gs_ref[...] = _dot(to_sorted, g_hi) + _dot(to_sorted, g_mid) + _dot(to_sorted, g_lo)

    bounds = [e_[0, 0].astype(I32) for e_ in ends]
    lane_s = lax.broadcasted_iota(I32, (sub, LANES), 1)

    def slab(j, carry):
        r0 = j * sub
        rows = pl.ds(pl.multiple_of(r0, sub), sub)
        g_first = sum((b <= r0).astype(I32) for b in bounds)
        g_last = sum((b <= r0 + sub - 1).astype(I32) for b in bounds)
        hs = hs_ref[rows, :]
        gs = gs_ref[rows, :]
        ys_ref[rows, :] = jnp.zeros((sub, ys_ref.shape[1]), F32)

        def group(g, c):
            gu = _dot(hs, wgu_ref[g])
            acts = []
            for e in range(EXP_PER_GROUP):
                ge = jnp.sum(jnp.where(lane_s == g * EXP_PER_GROUP + e + N_GROUPS, gs, 0.0), axis=-1, keepdims=True)
                up = gu[:, 2 * e * D_EXPERT:(2 * e + 1) * D_EXPERT]
                acts.append((up * jax.nn.sigmoid(up) * gu[:, (2 * e + 1) * D_EXPERT:(2 * e + 2) * D_EXPERT] * ge).astype(BF16))
            ys_ref[rows, :] += _dot(jnp.concatenate(acts, axis=1), wdn_ref[g])
            return c

        lax.fori_loop(g_first, g_last + 1, group, 0)
        return carry

    lax.fori_loop(0, tm // sub, slab, 0)
    o_ref[...] = x + _dot(to_sorted_t.astype(BF16), ys_ref[...].astype(BF16))


def _moe(x2, gain, wr_hi, wr_lo, br, wgu, wdn, tm=512, sub=128):
    n, d = x2.shape
    full = lambda a: pl.BlockSpec(a.shape, lambda i: (0,) * a.ndim)
    once = lambda a: pl.BlockSpec(a.shape, lambda i: (0,) * a.ndim, pipeline_mode=pl.Buffered(1))
    return pl.pallas_call(
        functools.partial(_moe_kernel, tm=tm, sub=sub),
        out_shape=jax.ShapeDtypeStruct((n, d), F32),
        grid=(n // tm,),
        in_specs=[pl.BlockSpec((tm, d), lambda i: (i, 0)), full(gain), full(wr_hi), full(wr_lo), full(br),
                  once(wgu), once(wdn)],
        out_specs=pl.BlockSpec((tm, d), lambda i: (i, 0)),
        scratch_shapes=[pltpu.VMEM((tm, d), BF16), pltpu.VMEM((tm, LANES), F32), pltpu.VMEM((tm, d), F32)],
        compiler_params=_params("parallel"),
        name="moe",
    )(x2, gain, wr_hi, wr_lo, br, wgu, wdn)


def _rope_tables(positions):
    pos = positions.reshape(-1).astype(F32)[:, None]

    def cs(dim):
        inv = ROPE_THETA ** (-jnp.arange(0, dim, 2, dtype=F32) / dim)
        ang = pos * inv
        return jnp.cos(ang), jnp.sin(ang)

    c, s = cs(HEAD_DIM)
    ci, si = cs(IDX_DIM)
    return jnp.concatenate([c, s], 1), jnp.concatenate([ci, si, jnp.zeros_like(ci), jnp.zeros_like(ci)], 1)


def _reorder_in_weight(w):
    a = w[:, :4 * WIDTH]
    o = 4 * WIDTH
    qlat = w[:, o:o + Q_LORA]; o += Q_LORA
    k = w[:, o:o + WIDTH]; o += WIDTH
    v = w[:, o:o + WIDTH]; o += WIDTH
    misc = w[:, o:]
    pad = jnp.zeros((w.shape[0], LANES - misc.shape[1]), w.dtype)
    return jnp.concatenate([a, k, v, qlat, misc, pad], axis=1).astype(BF16)


def _row(v):
    return v.reshape(1, -1).astype(F32)


def kernel(x, mem, positions, norm_mix, w_in, lb_raw, a_gnorm, b_qlat_gain, b_wqb, b_wqidx, b_qnorm,
           b_knorm, b_kidx_norm, w_out, norm_x, norm_mem, x_wq, x_wk, x_wv, x_wo, x_qnorm, x_knorm,
           norm_ffn, w_rg, b_rg, w_re, b_re, w_gu, w_dn):
    bsz, seq, d = x.shape
    n_mem = mem.shape[1]
    depth = w_in.shape[0]
    assert d == D_MODEL and seq % (2 * DSA_KEY_TILE) == 0 and n_mem % LANES == 0, (x.shape, mem.shape)
    x2 = x.reshape(bsz * seq, d)
    mem2 = mem.reshape(bsz * n_mem, d)
    tabs = _rope_tables(positions)
    pad_lanes = lambda v: jnp.pad(v, ((0, 0), (0, LANES - v.shape[1])))
    for l in range(depth):
        u_a, q, k, vt, qi, ki, wt = _proj_prep(
            x2, _row(norm_mix[l]), _reorder_in_weight(w_in[l]), tabs, b_wqb[l].astype(BF16),
            b_wqidx[l].astype(BF16), _row(b_qlat_gain[l]), _row(b_qnorm[l]), _row(b_knorm[l]),
            pad_lanes(_row(b_kidx_norm[l])))
        o_a = _hgrn(u_a, lb_raw.astype(F32), _row(a_gnorm[l]), l, bsz, seq)
        o_b = _dsa_attn(q, k, vt, qi, ki, wt, bsz, seq)
        km, vm = _mem_kv(mem2, _row(norm_mem[l]), x_wk[l].astype(BF16), x_wv[l].astype(BF16),
                         _row(x_knorm[l]), n_mem)
        x2 = _xattn(x2, o_a, o_b, w_out[l].astype(BF16), _row(norm_x[l]), x_wq[l].astype(BF16),
                    _row(x_qnorm[l]), km, vm, x_wo[l].astype(BF16), seq, n_mem)
        wr = pad_lanes(jnp.concatenate([w_rg[l], w_re[l].reshape(d, N_EXPERTS)], axis=1).astype(F32))
        wr_hi = wr.astype(BF16)
        wr_lo = (wr - wr_hi.astype(F32)).astype(BF16)
        br = pad_lanes(jnp.concatenate([b_rg[l], b_re[l].reshape(-1)]).reshape(1, -1).astype(F32))
        x2 = _moe(x2, _row(norm_ffn[l]), wr_hi, wr_lo, br,
                  w_gu[l].transpose(0, 2, 1, 3).reshape(N_GROUPS, d, EXP_PER_GROUP * 2 * D_EXPERT).astype(BF16),
                  w_dn[l].reshape(N_GROUPS, EXP_PER_GROUP * D_EXPERT, d).astype(BF16))
    return x2.reshape(bsz, seq, d)
```

```python
import functools

import numpy as np
import jax
import jax.numpy as jnp
from jax import lax
from jax.experimental import pallas as pl
from jax.experimental.pallas import tpu as pltpu

F32 = jnp.float32
BF16 = jnp.bfloat16
I32 = jnp.int32
I16 = jnp.int16

EPS = 1e-6
ROPE_THETA = 10000.0
CHUNK = 64
CHUNK_SHIFT = CHUNK.bit_length() - 1
SUB = 16
SUB_SHIFT = SUB.bit_length() - 1
N_HEADS = 4
HEAD_DIM = 128
WIDTH = N_HEADS * HEAD_DIM
Q_LORA = 256
D_MODEL = 1024
IDX_HEADS = 16
IDX_DIM = 64
ROT_I = IDX_DIM // 2
TOPK_MAX = 256
N_GROUPS = 4
EXP_PER_GROUP = 4
N_EXPERTS = N_GROUPS * EXP_PER_GROUP
D_EXPERT = 256
LANES = 128
INT_MIN = -2 ** 31
INT16_MIN, INT16_MAX = -2 ** 15, 2 ** 15 - 1
HALF_BITS = 16
PACK = 16
LOG2E = 1.4426950408889634
VT_ROWS = HEAD_DIM + PACK
DSA_KEY_TILE = 512
ATT_TILES = 4
SCORE_ROWS = 128
POOL_ROWS = 256
SURE_PASSES = 8

U_K = 4 * WIDTH
U_COLS = U_K + 2 * WIDTH + Q_LORA + LANES

VMEM_LIMIT = 56 * 1024 * 1024


def _rms(x, gain):
    return x * lax.rsqrt(jnp.mean(x * x, axis=-1, keepdims=True) + EPS) * gain


def _dot(a, b):
    return jnp.dot(a, b, preferred_element_type=F32)


def _rms_head(x, gain, width=HEAD_DIM):
    ones = (lax.broadcasted_iota(I32, (LANES, LANES), 0) < width).astype(BF16)
    ss = _dot((x * x).astype(BF16), ones)
    return x * lax.rsqrt(ss * (1.0 / width) + EPS) * gain


def _dot_nt(a, b):
    return lax.dot_general(a, b, (((1,), (1,)), ((), ())), preferred_element_type=F32)


def _params(*sem):
    return pltpu.CompilerParams(dimension_semantics=sem, vmem_limit_bytes=VMEM_LIMIT)


def _split3(x):
    hi = x.astype(BF16)
    r1 = x - hi.astype(F32)
    mid = r1.astype(BF16)
    lo = (r1 - mid.astype(F32)).astype(BF16)
    return hi, mid, lo


def _hgrn_kernel(q_ref, f_ref, i_ref, g_ref, lbraw_ref, gn_ref, e_ref, o_ref, st_ref,
                 *, layer, n_chunks, unroll):
    @pl.when(pl.program_id(2) == 0)
    def _():
        st_ref[...] = jnp.zeros_like(st_ref)

    lr = lbraw_ref[...]
    ex = jnp.exp(lr - jnp.max(lr, axis=0, keepdims=True))
    sm = ex / jnp.sum(ex, axis=0, keepdims=True)
    lb = jnp.zeros((1, HEAD_DIM), F32)
    for r in range(1, layer + 1):
        lb = lb + sm[r:r + 1, :]
    log_lb = jnp.log(lb)
    log_1mlb = jnp.log(1.0 - lb)

    row = lax.broadcasted_iota(I32, (CHUNK, HEAD_DIM), 0)
    col = lax.broadcasted_iota(I32, (CHUNK, HEAD_DIM), 1)
    row_l = row & (SUB - 1)
    row_b = row >> SUB_SHIFT
    col_b = col >> SUB_SHIFT
    tri = (lax.broadcasted_iota(I32, (CHUNK, CHUNK), 0)
           >= lax.broadcasted_iota(I32, (CHUNK, CHUNK), 1)).astype(BF16)
    n_sub = CHUNK // SUB
    zeros_c = jnp.zeros((CHUNK, HEAD_DIM), F32)

    def sub_bcast(t, s):
        t4 = t.reshape(n_sub, SUB, HEAD_DIM)
        return jnp.broadcast_to(t4[:, s:s + 1, :], (n_sub, SUB, HEAD_DIM)).reshape(CHUNK, HEAD_DIM)

    def gates(z):
        ls = jnp.minimum(z, 0.0) - jnp.log(1.0 + jnp.exp(-jnp.abs(z)))
        if layer == 0:
            return ls, (ls - z) * LOG2E
        cc = log_1mlb + ls
        log_f = jnp.maximum(log_lb, cc) + jnp.log(1.0 + jnp.exp(-jnp.abs(log_lb - cc)))
        return log_f, (cc - z) * LOG2E

    def diag_terms(q, b2, c2):
        pieces = []
        for s in range(SUB):
            d = jnp.where(row_l >= s, b2 - sub_bcast(c2, s), -jnp.inf)
            pieces.append((q * jnp.exp2(d)).astype(BF16))
        return jnp.concatenate(pieces, axis=1)

    def below_keys(b2, c2):
        kts = []
        for i in range(1, n_sub):
            r_i = b2[i * SUB:i * SUB + 1, :]
            kts.append(jnp.exp2(jnp.where(row < i * SUB, r_i - c2, -jnp.inf)))
            kts.append(zeros_c)
        return jnp.concatenate(kts, axis=0).astype(BF16)

    def below_scores(r):
        a = jnp.zeros((CHUNK, HEAD_DIM), F32)
        for i in range(1, n_sub):
            a = a + jnp.where(row_b == i, r[:, (i - 1) * LANES:i * LANES], 0.0)
        return a

    def chunks(it, carry):
        cs = range(unroll)
        rows = [pl.ds(pl.multiple_of((it * unroll + c) * CHUNK, CHUNK), CHUNK) for c in cs]
        qr = [q_ref[r, :] for r in rows]
        v = [i_ref[r, :] for r in rows]
        q = [x * jax.nn.sigmoid(x) for x in qr]
        gt = [gates(f_ref[r, :]) for r in rows]
        b2 = []
        for c in cs:
            hi, mid, lo = _split3(gt[c][0])
            b2.append((_dot(tri, hi) + _dot(tri, mid) + _dot(tri, lo)) * LOG2E)
        c2 = [b2[c] - gt[c][1] for c in cs]
        b_last = [x[CHUNK - 1:CHUNK, :] for x in b2]
        a2 = [_dot(diag_terms(q[c], b2[c], c2[c]), e_ref[...]) for c in cs]
        rr = [_dot_nt((q[c] * jnp.exp2(b2[c] - sub_bcast(b2[c], 0))).astype(BF16), below_keys(b2[c], c2[c]))
              for c in cs]
        upd = [_dot(v[c].T.astype(BF16), jnp.exp2(b_last[c] - c2[c]).astype(BF16)) for c in cs]
        qe = [(q[c] * jnp.exp2(b2[c])).astype(BF16) for c in cs]
        st = st_ref[...]
        o_inter = []
        for c in cs:
            o_inter.append(_dot_nt(qe[c], st.astype(BF16)))
            st = st * jnp.exp2(b_last[c]) + upd[c]
        st_ref[...] = st
        a = [jnp.where(col_b == row_b, a2[c], 0.0) + below_scores(rr[c]) for c in cs]
        o_intra = [_dot(a[c][:, :CHUNK].astype(BF16), v[c].astype(BF16)) for c in cs]
        for c in cs:
            g = g_ref[rows[c], :]
            o_ref[rows[c], :] = _rms(o_inter[c] + o_intra[c], gn_ref[...]) * (g * jax.nn.sigmoid(g))
        return carry

    lax.fori_loop(0, n_chunks // unroll, chunks, 0)


def _hgrn_selector():
    e = np.zeros((SUB * HEAD_DIM, LANES), np.float32)
    s_of_row = np.arange(SUB * HEAD_DIM) // HEAD_DIM
    cols = np.arange(LANES)
    e[:, :] = ((cols[None, :] % SUB) == s_of_row[:, None]) & (cols[None, :] < CHUNK)
    return jnp.asarray(e, BF16)


def _hgrn(u, lb_raw, gn, layer, bsz, seq, tb=1024):
    n = bsz * seq
    nb = seq // tb
    blk = lambda k: pl.BlockSpec((tb, HEAD_DIM), lambda b, h, j, k=k: (b * nb + j, h + N_HEADS * k))
    n_layers = lb_raw.shape[0]
    return pl.pallas_call(
        functools.partial(_hgrn_kernel, layer=layer, n_chunks=tb // CHUNK, unroll=16),
        out_shape=jax.ShapeDtypeStruct((n, WIDTH), F32),
        grid=(bsz, N_HEADS, nb),
        in_specs=[blk(0), blk(1), blk(2), blk(3),
                  pl.BlockSpec((n_layers, HEAD_DIM), lambda b, h, j: (0, h)),
                  pl.BlockSpec((1, HEAD_DIM), lambda b, h, j: (0, 0)),
                  pl.BlockSpec((SUB * HEAD_DIM, LANES), lambda b, h, j: (0, 0))],
        out_specs=pl.BlockSpec((tb, HEAD_DIM), lambda b, h, j: (b * nb + j, h)),
        scratch_shapes=[pltpu.VMEM((HEAD_DIM, HEAD_DIM), F32)],
        compiler_params=_params("parallel", "parallel", "arbitrary"),
        name="hgrn",
    )(u, u, u, u, lb_raw, gn, _hgrn_selector())


def _rope128(x, cos, sin_signed):
    return x * cos + pltpu.roll(x, HEAD_DIM // 2, 1) * sin_signed


def _rope64(x, cos, sin_a, sin_b):
    return x * cos + pltpu.roll(x, LANES - ROT_I, 1) * sin_a + pltpu.roll(x, ROT_I, 1) * sin_b


def _proj_prep_kernel(x_ref, g_ref, w_ref, ta_ref, tb_ref, wqb_ref, wqi_ref, gql_ref, gqn_ref, gkn_ref, gki_ref,
                      ua_out, q_out, k_out, vt_out, qi_out, ki_out, wt_out):
    hb = _rms(x_ref[...], g_ref[...]).astype(BF16)
    ub = _dot(hb, w_ref[:, U_K:])
    k, v = ub[:, :WIDTH], ub[:, WIDTH:2 * WIDTH]
    ql, misc = ub[:, 2 * WIDTH:2 * WIDTH + Q_LORA], ub[:, 2 * WIDTH + Q_LORA:]

    def project(c):
        ua_out[:, c * WIDTH:(c + 1) * WIDTH] = _dot(hb, w_ref[:, c * WIDTH:(c + 1) * WIDTH])

    ta, tb = ta_ref[...], tb_ref[...]
    lane = lax.broadcasted_iota(I32, ta.shape, 1)
    ta_swapped = pltpu.roll(ta, HEAD_DIM // 2, 1)
    cm = jnp.where(lane < HEAD_DIM // 2, ta, ta_swapped)
    sm = jnp.where(lane < HEAD_DIM // 2, -ta_swapped, ta)
    quarter = lane >> (ROT_I.bit_length() - 1)
    r32, r64, r96 = (pltpu.roll(tb, sh * ROT_I, 1) for sh in (1, 2, 3))
    ci = jnp.where(quarter == 0, tb, jnp.where(quarter == 1, r32, jnp.where(quarter == 2, r64, r96)))
    si = jnp.where(quarter == 0, r96, jnp.where(quarter == 1, tb, jnp.where(quarter == 2, r32, r64)))
    sia = jnp.where((lane & ROT_I) == 0, -si, 0.0)
    sib = jnp.where((lane & ROT_I) != 0, si, 0.0)
    cb = _rms(ql, gql_ref[...]).astype(BF16)
    scale = HEAD_DIM ** -0.5

    def prep_q():
        qf = _dot(cb, wqb_ref[...])
        for h in range(N_HEADS):
            sl = slice(h * HEAD_DIM, (h + 1) * HEAD_DIM)
            qh = _rope128(_rms_head(qf[:, sl], gqn_ref[...]), cm, sm)
            q_out[:, sl] = (qh * (scale * LOG2E)).astype(BF16)

    def prep_k():
        for h in range(N_HEADS):
            sl = slice(h * HEAD_DIM, (h + 1) * HEAD_DIM)
            kh = _rope128(_rms_head(k[:, sl], gkn_ref[...]), cm, sm)
            k_out[:, sl] = kh.astype(BF16)

    def prep_v():
        vt = v.T.astype(BF16)
        ones = jnp.ones((PACK, vt.shape[1]), BF16)
        for h in range(N_HEADS):
            vt_out[0, h * VT_ROWS:h * VT_ROWS + HEAD_DIM, :] = vt[h * HEAD_DIM:(h + 1) * HEAD_DIM, :]
            vt_out[0, h * VT_ROWS + HEAD_DIM:(h + 1) * VT_ROWS, :] = ones

    def prep_index(groups):
        qi = _dot(cb, wqi_ref[:, groups[0] * LANES:(groups[-1] + 1) * LANES])
        for n_, p in enumerate(groups):
            qi_out[:, p * LANES:(p + 1) * LANES] = _rope64(qi[:, n_ * LANES:(n_ + 1) * LANES], ci, sia, sib).astype(BF16)

    def prep_keys_index():
        kn = _rope64(_rms_head(misc, gki_ref[...], IDX_DIM), ci, sia, sib)
        ki_out[...] = (kn + pltpu.roll(kn, IDX_DIM, 1)).astype(BF16)
        wt = (misc * (IDX_HEADS ** -0.5 * IDX_DIM ** -0.5)).T
        wt_out[...] = wt[IDX_DIM:IDX_DIM + IDX_HEADS, :]

    prep_q()
    project(0)
    prep_v()
    prep_index(range(0, 4))
    project(1)
    prep_k()
    prep_index(range(4, 8))
    project(2)
    prep_keys_index()
    project(3)


def _proj_prep(x2, gain, w_bf, tabs, wqb, wqi, gql, gqn, gkn, gki, tm=DSA_KEY_TILE):
    n, d = x2.shape
    nt = n // tm
    row = lambda w: pl.BlockSpec((tm, w), lambda i: (i, 0))
    full = lambda a: pl.BlockSpec(a.shape, lambda i: (0,) * a.ndim)
    return pl.pallas_call(
        _proj_prep_kernel,
        out_shape=(jax.ShapeDtypeStruct((n, U_K), F32),
                   jax.ShapeDtypeStruct((n, WIDTH), BF16),
                   jax.ShapeDtypeStruct((n, WIDTH), BF16),
                   jax.ShapeDtypeStruct((nt, N_HEADS * VT_ROWS, tm), BF16),
                   jax.ShapeDtypeStruct((n, IDX_HEADS * IDX_DIM), BF16),
                   jax.ShapeDtypeStruct((n, LANES), BF16),
                   jax.ShapeDtypeStruct((IDX_HEADS, n), F32)),
        grid=(nt,),
        in_specs=[row(d), full(gain), pl.BlockSpec(w_bf.shape, lambda i: (0, 0), pipeline_mode=pl.Buffered(1)),
                  row(LANES), row(LANES),
                  full(wqb), full(wqi), full(gql), full(gqn), full(gkn), full(gki)],
        out_specs=(row(U_K), row(WIDTH), row(WIDTH),
                   pl.BlockSpec((1, N_HEADS * VT_ROWS, tm), lambda i: (i, 0, 0)),
                   row(IDX_HEADS * IDX_DIM), row(LANES),
                   pl.BlockSpec((IDX_HEADS, tm), lambda i: (0, i))),
        compiler_params=_params("parallel"),
        name="proj_prep",
    )(x2, gain, w_bf, *tabs, wqb, wqi, gql, gqn, gkn, gki)


def _sortable(x):
    bits = lax.bitcast_convert_type(x, I32)
    return bits ^ ((bits >> 31) & 0x7FFFFFFF)


def _dsa_attn_kernel(qi_ref, wt_ref, qin_ref, wtn_ref, q_ref, ki_ref, k_ref, vt_ref, o_ref,
                     qm_ref, hi_ref, lo_ref, kb_ref, pk_ref, lt_ref, acc0, acc1, acc2, acc3,
                     *, tq, kt, top_k, n_blocks):
    acc_refs = (acc0, acc1, acc2, acc3)
    blk = pl.program_id(1)
    cur = blk & 1
    nxt = 1 - cur
    q0 = blk * tq
    n_keys = q0 + tq
    sub = SCORE_ROWS
    key_iota = lax.broadcasted_iota(I32, (sub, tq), 0)
    lane_q = lax.broadcasted_iota(I32, (tq, LANES), 1)
    pooled_per_group = 2 * PACK

    def sweep(total, fn, carry, big=2):
        n_full = total // kt
        carry = lax.fori_loop(0, n_full // big,
                              lambda jp, c: fn(pl.multiple_of(jp * big * kt, big * kt), big * kt, c), carry)
        w = big // 2
        while w >= 1:
            first = (n_full // (2 * w)) * (2 * w)
            carry = lax.cond((n_full & w) != 0,
                             lambda c, first=first, w=w: fn(pl.multiple_of(first * kt, w * kt), w * kt, c),
                             lambda c: c, carry)
            w //= 2
        return lax.cond(total % kt != 0, lambda c: fn(pl.multiple_of(n_full * kt, kt), kt // 2, c),
                        lambda c: c, carry)

    def limit_of(q_start):
        qpos = q_start + lax.broadcasted_iota(I32, (1, tq), 1)
        return ((qpos >> CHUNK_SHIFT) + 1) << CHUNK_SHIFT

    def build_qm(src_ref):
        for h in range(IDX_HEADS):
            src = src_ref[:, (h // 2) * LANES:(h // 2 + 1) * LANES].astype(F32)
            keep = (lane_q < IDX_DIM) if h % 2 == 0 else (lane_q >= IDX_DIM)
            qm_ref[h * tq:(h + 1) * tq, :] = jnp.where(keep, src, 0.0).astype(BF16)

    def score_rows(start, size, slot, limit, w_ref):
        for part in range(size // sub):
            base = start + part * sub
            rows = pl.ds(pl.multiple_of(base, sub), sub)
            ki = ki_ref[rows, :]
            acc = jnp.zeros((sub, tq), F32)
            for h in range(IDX_HEADS):
                x = _dot_nt(ki, qm_ref[h * tq:(h + 1) * tq, :])
                acc = acc + w_ref[h:h + 1, :] * jnp.maximum(x, 0.0)
            key = jnp.where(base + key_iota < limit, _sortable(acc), INT_MIN)
            hi_ref[slot, rows, :] = (key >> HALF_BITS).astype(I16)
            lo_ref[slot, rows, :] = ((key & (2 ** HALF_BITS - 1)) + INT16_MIN).astype(I16)

    @pl.when(blk == 0)
    def _():
        build_qm(qi_ref)
        limit0 = limit_of(0)

        def body(start, size, carry):
            score_rows(start, size, 0, limit0, wt_ref)
            return carry
        sweep(n_keys, body, 0)

    one, zero = jnp.ones((), BF16), jnp.zeros((), BF16)
    neg_inf = jnp.full((), -jnp.inf, BF16)
    hi_rows = lambda rows: hi_ref[cur, rows, :]
    lo_rows = lambda rows: lo_ref[cur, rows, :]
    kb_rows = lambda rows: kb_ref[rows, :]

    def count16(get, trial, strict, total=None):
        t16 = jnp.broadcast_to(trial.astype(I16), (PACK, tq))

        def hits(start, size, c):
            x = get(pl.ds(start, size))
            parts = [jnp.zeros((PACK, tq), BF16) for _ in range(4)]
            for r in range(size // PACK):
                xr = x[r * PACK:(r + 1) * PACK, :]
                parts[r % 4] = parts[r % 4] + jnp.where((xr > t16) if strict else (xr >= t16), one, zero)
            return c + ((parts[0] + parts[1]) + (parts[2] + parts[3])).astype(F32)

        c = sweep(n_keys if total is None else total, hits, jnp.zeros((PACK, tq), F32))
        return jnp.sum(c, axis=0, keepdims=True)

    def bisect16(get, want, n_all, early_exit, total=None):
        def step(it, carry):
            lo, n_lo, n_up = carry
            trial = lo + lax.shift_left(jnp.int32(1), HALF_BITS - 1 - it)
            n = count16(get, trial, False, total)
            ok = n >= want
            return jnp.where(ok, trial, lo), jnp.where(ok, n, n_lo), jnp.where(ok, n_up, n)

        init = (jnp.full((1, tq), INT16_MIN, I32), n_all, jnp.zeros((1, tq), F32))
        if not early_exit:
            return lax.fori_loop(0, HALF_BITS, step, init)

        def unresolved(carry):
            it, _, n_lo, _ = carry
            return (it < HALF_BITS) & (jnp.max(jnp.where(n_lo != want, 1.0, 0.0)) > 0.0)

        def two_steps(carry):
            return (carry[0] + 2,) + step(carry[0] + 1, step(carry[0], carry[1:]))

        return lax.while_loop(unresolved, two_steps, (jnp.int32(SURE_PASSES),) + lax.fori_loop(0, SURE_PASSES, step, init))[1:]

    tau_hi, n_hi_ge, n_hi_gt = bisect16(hi_rows, float(top_k), jnp.full((1, tq), n_keys.astype(F32), F32), False)
    want_lo = top_k - n_hi_gt
    tau_hi_row = tau_hi.astype(I16)

    min16 = jnp.full((PACK, tq), INT16_MIN, I16)

    def bucket_rows(start, size, carry):
        rows = pl.ds(start, size)
        kb = jnp.where(hi_rows(rows) == jnp.broadcast_to(tau_hi_row, (size, tq)), lo_rows(rows),
                       jnp.full((), INT16_MIN, I16))
        kb_ref[rows, :] = kb
        for g in range(size // POOL_ROWS):
            top1, top2 = min16, min16
            for r in range(POOL_ROWS // PACK):
                x = kb[g * POOL_ROWS + r * PACK:g * POOL_ROWS + (r + 1) * PACK, :]
                below = x < top1
                runner = jnp.where(below, x, top1)
                top2 = jnp.where(runner > top2, runner, top2)
                top1 = jnp.where(below, top1, x)
            base = pl.multiple_of((start // POOL_ROWS + g) * pooled_per_group, pooled_per_group)
            pk_ref[pl.ds(base, PACK), :] = top1
            pk_ref[pl.ds(base + PACK, PACK), :] = top2
        return carry

    sweep(n_keys, bucket_rows, 0)
    n_groups = n_keys // POOL_ROWS
    pooled_total = (n_groups * pooled_per_group + kt // 2 - 1) // (kt // 2) * (kt // 2)

    def pad_pooled(g, carry):
        pk_ref[pl.ds(pl.multiple_of(g * pooled_per_group, pooled_per_group), pooled_per_group), :] = jnp.full(
            (pooled_per_group, tq), INT16_MIN, I16)
        return carry

    lax.fori_loop(n_groups, pooled_total // pooled_per_group, pad_pooled, 0)
    n_bucket = n_hi_ge - n_hi_gt
    pk_rows = lambda rows: pk_ref[rows, :]
    tau_p, _, _ = bisect16(pk_rows, want_lo, n_bucket, True, pooled_total)
    n_ge_p = jnp.where(tau_p == INT16_MIN, n_bucket, count16(kb_rows, tau_p, False))
    n_gt_p = count16(kb_rows, tau_p, True)
    pooled_ok = (n_ge_p == want_lo) | ((n_ge_p > want_lo) & (n_gt_p < want_lo))

    def full_search():
        t, n_ge, _ = bisect16(kb_rows, want_lo, n_bucket, True)
        return t, n_ge, count16(kb_rows, t, True)

    tau_lo, n_kb_ge, n_kb_gt = lax.cond(
        jnp.min(jnp.where(pooled_ok, 1.0, 0.0)) > 0.0,
        lambda: (tau_p, n_ge_p, n_gt_p), full_search)
    live = (tau_hi > INT16_MIN) | (tau_lo > INT16_MIN)
    excess = jnp.max(jnp.where(live, n_kb_ge - want_lo, 0.0))

    @pl.when(excess > 0)
    def _():
        need = want_lo - n_kb_gt
        live_f = jnp.where(live, 1.0, 0.0)
        step_rows = kt // 2
        tri = (lax.broadcasted_iota(I32, (step_rows, step_rows), 0)
               >= lax.broadcasted_iota(I32, (step_rows, step_rows), 1)).astype(BF16)

        def body(start, size, seen):
            for part in range(size // step_rows):
                rows = pl.ds(pl.multiple_of(start + part * step_rows, step_rows), step_rows)
                h32 = hi_rows(rows).astype(I32)
                eq = jnp.where(h32 == tau_hi, jnp.where(lo_rows(rows).astype(I32) == tau_lo, 1.0, 0.0), 0.0)
                rank = _dot(tri, eq.astype(BF16)) + seen
                dropped = eq * (1.0 - jnp.where(rank <= need, live_f, 0.0))
                hi_ref[cur, rows, :] = jnp.where(dropped > 0.0, INT16_MIN, h32).astype(I16)
                seen = rank[step_rows - 1:step_rows, :]
            return seen
        sweep(n_keys, body, jnp.zeros((1, tq), F32))

    tau_lo_row = jnp.where(live, tau_lo, INT16_MAX).astype(I16)

    def bias_rows(rows, size):
        h16 = hi_rows(rows)
        th = jnp.broadcast_to(tau_hi_row, (size, tq))
        in_bucket = jnp.where(lo_rows(rows) >= jnp.broadcast_to(tau_lo_row, (size, tq)), zero, neg_inf)
        return jnp.where(h16 > th, zero, jnp.where(h16 == th, in_bucket, neg_inf)).astype(F32)

    for a in acc_refs:
        a[...] = jnp.zeros_like(a)
    heads = [slice(h * HEAD_DIM, (h + 1) * HEAD_DIM) for h in range(N_HEADS)]
    limit_next = limit_of(q0 + tq)

    def attend(start, size, ms, score_next):
        segs = [(start + j * kt, kt) for j in range(size // kt)] if size >= kt else [(start, size)]
        for i, (s0, n) in enumerate(segs):
            rows = pl.ds(pl.multiple_of(s0, kt // 2), n)
            bias = bias_rows(rows, n)
            for h in range(N_HEADS):
                lt_ref[i, h, :n, :] = _dot_nt(k_ref[rows, heads[h]], q_ref[:, heads[h]]) + bias
        ms = list(ms)
        for i, (s0, n) in enumerate(segs):
            if score_next:
                score_rows(s0, n, nxt, limit_next, wtn_ref)
            alphas, ps = [], []
            for h in range(N_HEADS):
                lt = lt_ref[i, h, :n, :]
                m_new = jnp.maximum(ms[h], jnp.max(lt, axis=0, keepdims=True))
                m_safe = jnp.where(m_new == -jnp.inf, 0.0, m_new)
                ps.append(jnp.exp2(lt - m_safe).astype(BF16))
                alphas.append(jnp.exp2(ms[h] - m_safe))
                ms[h] = m_new
            for h in range(N_HEADS):
                acc_refs[h][...] = (acc_refs[h][...] * alphas[h]
                                    + _dot(vt_ref[s0 // kt, h * VT_ROWS:(h + 1) * VT_ROWS, :n], ps[h]))
        return tuple(ms)

    m_init = tuple(jnp.full((1, tq), -jnp.inf, F32) for _ in range(N_HEADS))

    @pl.when(blk < n_blocks - 1)
    def _():
        build_qm(qin_ref)
        sweep(n_keys, functools.partial(attend, score_next=True), m_init, big=ATT_TILES)
        score_rows(pl.multiple_of(n_keys, kt // 2), tq, nxt, limit_next, wtn_ref)

    @pl.when(blk == n_blocks - 1)
    def _():
        sweep(n_keys, functools.partial(attend, score_next=False), m_init, big=ATT_TILES)

    for h in range(N_HEADS):
        acc = acc_refs[h][...]
        out_t = acc[:HEAD_DIM, :] * (1.0 / acc[HEAD_DIM:HEAD_DIM + 1, :])
        o_ref[:, heads[h]] = out_t.T


def _dsa_attn(q, k, vt, qi, ki, wt, bsz, seq, tq=256, kt=DSA_KEY_TILE):
    n = bsz * seq
    nq = seq // tq
    top_k = min(TOPK_MAX, seq // 4)
    once = dict(pipeline_mode=pl.Buffered(1))
    nxt_blk = lambda b, i: b * nq + jnp.minimum(i + 1, nq - 1)
    return pl.pallas_call(
        functools.partial(_dsa_attn_kernel, tq=tq, kt=kt, top_k=top_k, n_blocks=nq),
        out_shape=jax.ShapeDtypeStruct((n, WIDTH), F32),
        grid=(bsz, nq),
        in_specs=[pl.BlockSpec((tq, IDX_HEADS * IDX_DIM), lambda b, i: (b * nq + i, 0)),
                  pl.BlockSpec((IDX_HEADS, tq), lambda b, i: (0, b * nq + i)),
                  pl.BlockSpec((tq, IDX_HEADS * IDX_DIM), lambda b, i: (nxt_blk(b, i), 0)),
                  pl.BlockSpec((IDX_HEADS, tq), lambda b, i: (0, nxt_blk(b, i))),
                  pl.BlockSpec((tq, WIDTH), lambda b, i: (b * nq + i, 0)),
                  pl.BlockSpec((seq, LANES), lambda b, i: (b, 0), **once),
                  pl.BlockSpec((seq, WIDTH), lambda b, i: (b, 0), **once),
                  pl.BlockSpec((seq // kt, N_HEADS * VT_ROWS, kt), lambda b, i: (b, 0, 0), **once)],
        out_specs=pl.BlockSpec((tq, WIDTH), lambda b, i: (b * nq + i, 0)),
        scratch_shapes=[pltpu.VMEM((IDX_HEADS * tq, LANES), BF16),
                        pltpu.VMEM((2, seq, tq), I16), pltpu.VMEM((2, seq, tq), I16), pltpu.VMEM((seq, tq), I16),
                        pltpu.VMEM((pl.cdiv(seq // POOL_ROWS * 2 * PACK, kt) * kt, tq), I16),
                        pltpu.VMEM((ATT_TILES, N_HEADS, kt, tq), F32)]
                       + [pltpu.VMEM((VT_ROWS, tq), F32) for _ in range(N_HEADS)],
        compiler_params=_params("arbitrary", "arbitrary"),
        name="dsa_attn",
    )(qi, wt, qi, wt, q, ki, k, vt)


def _mem_kv_kernel(mem_ref, g_ref, wk_ref, wv_ref, gk_ref, k_out, v_out):
    mb = _rms(mem_ref[...], g_ref[...]).astype(BF16)
    kf = _dot(mb, wk_ref[...])
    for h in range(N_HEADS):
        sl = slice(h * HEAD_DIM, (h + 1) * HEAD_DIM)
        k_out[:, sl] = _rms_head(kf[:, sl], gk_ref[...]).astype(BF16)
    v_out[...] = _dot(mb, wv_ref[...]).astype(BF16)


def _mem_kv(mem2, gain, wk, wv, gk, n_mem):
    n, d = mem2.shape
    full = lambda a: pl.BlockSpec(a.shape, lambda b: (0,) * a.ndim)
    return pl.pallas_call(
        _mem_kv_kernel,
        out_shape=(jax.ShapeDtypeStruct((n, WIDTH), BF16), jax.ShapeDtypeStruct((n, WIDTH), BF16)),
        grid=(n // n_mem,),
        in_specs=[pl.BlockSpec((n_mem, d), lambda b: (b, 0)), full(gain), full(wk), full(wv), full(gk)],
        out_specs=(pl.BlockSpec((n_mem, WIDTH), lambda b: (b, 0)),
                   pl.BlockSpec((n_mem, WIDTH), lambda b: (b, 0))),
        compiler_params=_params("parallel"),
        name="mem_kv",
    )(mem2, gain, wk, wv, gk)


def _xattn_kernel(x_ref, oa_ref, ob_ref, wout_ref, g_ref, wq_ref, gq_ref, km_ref, vm_ref, wo_ref, o_ref):
    x1 = (x_ref[...] + _dot(oa_ref[...].astype(BF16), wout_ref[:WIDTH, :])
          + _dot(ob_ref[...].astype(BF16), wout_ref[WIDTH:, :]))
    hb = _rms(x1, g_ref[...]).astype(BF16)
    qf = _dot(hb, wq_ref[...])
    scale = HEAD_DIM ** -0.5
    outs = []
    for h in range(N_HEADS):
        sl = slice(h * HEAD_DIM, (h + 1) * HEAD_DIM)
        qh = (_rms(qf[:, sl], gq_ref[...]) * scale).astype(BF16)
        logits = _dot_nt(qh, km_ref[:, sl])
        p = jnp.exp(logits - jnp.max(logits, axis=-1, keepdims=True))
        oh = _dot(p.astype(BF16), vm_ref[:, sl])
        outs.append((oh * (1.0 / jnp.sum(p, axis=-1, keepdims=True))).astype(BF16))
    o_ref[...] = x1 + _dot(jnp.concatenate(outs, axis=1), wo_ref[...])


def _xattn(x2, oa, ob, wout, gain, wq, gq, km, vm, wo, seq, n_mem, tm=1024):
    n, d = x2.shape
    per_b = seq // tm
    row = lambda w: pl.BlockSpec((tm, w), lambda i: (i, 0))
    full = lambda a: pl.BlockSpec(a.shape, lambda i: (0,) * a.ndim)
    memb = pl.BlockSpec((n_mem, WIDTH), lambda i: (i // per_b, 0))
    return pl.pallas_call(
        _xattn_kernel,
        out_shape=jax.ShapeDtypeStruct((n, d), F32),
        grid=(n // tm,),
        in_specs=[row(d), row(WIDTH), row(WIDTH), full(wout), full(gain), full(wq), full(gq),
                  memb, memb, full(wo)],
        out_specs=row(d),
        compiler_params=_params("parallel"),
        name="xattn",
    )(x2, oa, ob, wout, gain, wq, gq, km, vm, wo)


def _moe_kernel(x_ref, g_ref, wr_hi_ref, wr_lo_ref, br_ref, wgu_ref, wdn_ref, o_ref, hs_ref, gs_ref, ys_ref,
                *, tm, sub):
    lane = lax.broadcasted_iota(I32, (tm, LANES), 1)
    lane_f = lane.astype(F32)
    x = x_ref[...]
    h = _rms(x, g_ref[...])
    h_hi = h.astype(BF16)
    h_lo = (h - h_hi.astype(F32)).astype(BF16)
    lg = (_dot(h_hi, wr_hi_ref[...]) + _dot(h_lo, wr_hi_ref[...]) + _dot(h_hi, wr_lo_ref[...])
          + br_ref[...])
    first = lambda cond: jnp.min(jnp.where(cond, lane_f, 1e9), axis=-1, keepdims=True)
    gl = jnp.where(lane < N_GROUPS, lg, -jnp.inf)
    gmax = jnp.max(gl, axis=-1, keepdims=True)
    gsel = first(gl == gmax)
    g_w = 1.0 / jnp.sum(jnp.exp(gl - gmax), axis=-1, keepdims=True)
    grp_of_lane = ((lane - N_GROUPS) >> (EXP_PER_GROUP.bit_length() - 1)).astype(F32)
    in_grp = (lane >= N_GROUPS) & (lane < N_GROUPS + N_EXPERTS) & (grp_of_lane == gsel)
    el = jnp.where(in_grp, lg, -jnp.inf)
    v1 = jnp.max(el, axis=-1, keepdims=True)
    i1 = first(el == v1)
    el2 = jnp.where(lane_f == i1, -jnp.inf, el)
    v2 = jnp.max(el2, axis=-1, keepdims=True)
    i2 = first(el2 == v2)
    r = jnp.exp(v2 - v1)
    w1 = 1.0 / (1.0 + r)
    gate = jnp.where(lane_f == i1, w1 * g_w, jnp.where(lane_f == i2, r * w1 * g_w, 0.0))

    onehot = jnp.where(lane_f == gsel, 1.0, 0.0)
    t_row = lax.broadcasted_iota(I32, (tm, tm), 0)
    t_col = lax.broadcasted_iota(I32, (tm, tm), 1)
    before = _dot(jnp.where(t_row > t_col, 1.0, 0.0).astype(BF16), onehot.astype(BF16))
    counts = jnp.sum(onehot, axis=0, keepdims=True)
    lane1 = lax.broadcasted_iota(I32, (1, LANES), 1)
    ends = []
    run = jnp.zeros((1, 1), F32)
    for g in range(N_GROUPS - 1):
        run = run + jnp.sum(jnp.where(lane1 == g, counts, 0.0), axis=-1, keepdims=True)
        ends.append(run)
    start_of = sum(jnp.where(lane1 == g + 1, ends[g], 0.0) for g in range(N_GROUPS - 1))
    pos = jnp.sum(onehot * (before + start_of), axis=-1, keepdims=True)
    to_sorted_t = jnp.where(pos == t_col.astype(F32), 1.0, 0.0)
    to_sorted = to_sorted_t.T.astype(BF16)
    hs_ref[...] = _dot(to_sorted, h_hi).astype(BF16)
    g_hi, g_mid, g_lo = _split3(gate)
    gs_ref[...] = _dot(to_sorted, g_hi) + _dot(to_sorted, g_mid) + _dot(to_sorted, g_lo)

    bounds = [e_[0, 0].astype(I32) for e_ in ends]
    lane_s = lax.broadcasted_iota(I32, (sub, LANES), 1)

    def slab(j, carry):
        r0 = j * sub
        rows = pl.ds(pl.multiple_of(r0, sub), sub)
        g_first = sum((b <= r0).astype(I32) for b in bounds)
        g_last = sum((b <= r0 + sub - 1).astype(I32) for b in bounds)
        hs = hs_ref[rows, :]
        gs = gs_ref[rows, :]
        ys_ref[rows, :] = jnp.zeros((sub, ys_ref.shape[1]), F32)

        def group(g, c):
            gu = _dot(hs, wgu_ref[g])
            acts = []
            for e in range(EXP_PER_GROUP):
                ge = jnp.sum(jnp.where(lane_s == g * EXP_PER_GROUP + e + N_GROUPS, gs, 0.0), axis=-1, keepdims=True)
                up = gu[:, 2 * e * D_EXPERT:(2 * e + 1) * D_EXPERT]
                acts.append((up * jax.nn.sigmoid(up) * gu[:, (2 * e + 1) * D_EXPERT:(2 * e + 2) * D_EXPERT] * ge).astype(BF16))
            ys_ref[rows, :] += _dot(jnp.concatenate(acts, axis=1), wdn_ref[g])
            return c

        lax.fori_loop(g_first, g_last + 1, group, 0)
        return carry

    lax.fori_loop(0, tm // sub, slab, 0)
    o_ref[...] = x + _dot(to_sorted_t.astype(BF16), ys_ref[...].astype(BF16))


def _moe(x2, gain, wr_hi, wr_lo, br, wgu, wdn, tm=512, sub=128):
    n, d = x2.shape
    full = lambda a: pl.BlockSpec(a.shape, lambda i: (0,) * a.ndim)
    once = lambda a: pl.BlockSpec(a.shape, lambda i: (0,) * a.ndim, pipeline_mode=pl.Buffered(1))
    return pl.pallas_call(
        functools.partial(_moe_kernel, tm=tm, sub=sub),
        out_shape=jax.ShapeDtypeStruct((n, d), F32),
        grid=(n // tm,),
        in_specs=[pl.BlockSpec((tm, d), lambda i: (i, 0)), full(gain), full(wr_hi), full(wr_lo), full(br),
                  once(wgu), once(wdn)],
        out_specs=pl.BlockSpec((tm, d), lambda i: (i, 0)),
        scratch_shapes=[pltpu.VMEM((tm, d), BF16), pltpu.VMEM((tm, LANES), F32), pltpu.VMEM((tm, d), F32)],
        compiler_params=_params("parallel"),
        name="moe",
    )(x2, gain, wr_hi, wr_lo, br, wgu, wdn)


def _rope_tables(positions):
    pos = positions.reshape(-1).astype(F32)[:, None]

    def cs(dim):
        inv = ROPE_THETA ** (-jnp.arange(0, dim, 2, dtype=F32) / dim)
        ang = pos * inv
        return jnp.cos(ang), jnp.sin(ang)

    c, s = cs(HEAD_DIM)
    ci, si = cs(IDX_DIM)
    return jnp.concatenate([c, s], 1), jnp.concatenate([ci, si, jnp.zeros_like(ci), jnp.zeros_like(ci)], 1)


def _reorder_in_weight(w):
    a = w[:, :4 * WIDTH]
    o = 4 * WIDTH
    qlat = w[:, o:o + Q_LORA]; o += Q_LORA
    k = w[:, o:o + WIDTH]; o += WIDTH
    v = w[:, o:o + WIDTH]; o += WIDTH
    misc = w[:, o:]
    pad = jnp.zeros((w.shape[0], LANES - misc.shape[1]), w.dtype)
    return jnp.concatenate([a, k, v, qlat, misc, pad], axis=1).astype(BF16)


def _row(v):
    return v.reshape(1, -1).astype(F32)


def kernel(x, mem, positions, norm_mix, w_in, lb_raw, a_gnorm, b_qlat_gain, b_wqb, b_wqidx, b_qnorm,
           b_knorm, b_kidx_norm, w_out, norm_x, norm_mem, x_wq, x_wk, x_wv, x_wo, x_qnorm, x_knorm,
           norm_ffn, w_rg, b_rg, w_re, b_re, w_gu, w_dn):
    bsz, seq, d = x.shape
    n_mem = mem.shape[1]
    depth = w_in.shape[0]
    assert d == D_MODEL and seq % (2 * DSA_KEY_TILE) == 0 and n_mem % LANES == 0, (x.shape, mem.shape)
    x2 = x.reshape(bsz * seq, d)
    mem2 = mem.reshape(bsz * n_mem, d)
    tabs = _rope_tables(positions)
    pad_lanes = lambda v: jnp.pad(v, ((0, 0), (0, LANES - v.shape[1])))
    for l in range(depth):
        u_a, q, k, vt, qi, ki, wt = _proj_prep(
            x2, _row(norm_mix[l]), _reorder_in_weight(w_in[l]), tabs, b_wqb[l].astype(BF16),
            b_wqidx[l].astype(BF16), _row(b_qlat_gain[l]), _row(b_qnorm[l]), _row(b_knorm[l]),
            pad_lanes(_row(b_kidx_norm[l])))
        o_a = _hgrn(u_a, lb_raw.astype(F32), _row(a_gnorm[l]), l, bsz, seq)
        o_b = _dsa_attn(q, k, vt, qi, ki, wt, bsz, seq)
        km, vm = _mem_kv(mem2, _row(norm_mem[l]), x_wk[l].astype(BF16), x_wv[l].astype(BF16),
                         _row(x_knorm[l]), n_mem)
        x2 = _xattn(x2, o_a, o_b, w_out[l].astype(BF16), _row(norm_x[l]), x_wq[l].astype(BF16),
                    _row(x_qnorm[l]), km, vm, x_wo[l].astype(BF16), seq, n_mem)
        wr = pad_lanes(jnp.concatenate([w_rg[l], w_re[l].reshape(d, N_EXPERTS)], axis=1).astype(F32))
        wr_hi = wr.astype(BF16)
        wr_lo = (wr - wr_hi.astype(F32)).astype(BF16)
        br = pad_lanes(jnp.concatenate([b_rg[l], b_re[l].reshape(-1)]).reshape(1, -1).astype(F32))
        x2 = _moe(x2, _row(norm_ffn[l]), wr_hi, wr_lo, br,
                  w_gu[l].transpose(0, 2, 1, 3).reshape(N_GROUPS, d, EXP_PER_GROUP * 2 * D_EXPERT).astype(BF16),
                  w_dn[l].reshape(N_GROUPS, EXP_PER_GROUP * D_EXPERT, d).astype(BF16))
    return x2.reshape(bsz, seq, d)
```

```python
import functools

import numpy as np
import jax
import jax.numpy as jnp
from jax import lax
from jax.experimental import pallas as pl
from jax.experimental.pallas import tpu as pltpu

F32 = jnp.float32
BF16 = jnp.bfloat16
I32 = jnp.int32
I16 = jnp.int16

EPS = 1e-6
ROPE_THETA = 10000.0
CHUNK = 64
CHUNK_SHIFT = CHUNK.bit_length() - 1
SUB = 16
SUB_SHIFT = SUB.bit_length() - 1
N_HEADS = 4
HEAD_DIM = 128
WIDTH = N_HEADS * HEAD_DIM
Q_LORA = 256
D_MODEL = 1024
IDX_HEADS = 16
IDX_DIM = 64
ROT_I = IDX_DIM // 2
TOPK_MAX = 256
N_GROUPS = 4
EXP_PER_GROUP = 4
N_EXPERTS = N_GROUPS * EXP_PER_GROUP
D_EXPERT = 256
LANES = 128
INT_MIN = -2 ** 31
INT16_MIN, INT16_MAX = -2 ** 15, 2 ** 15 - 1
HALF_BITS = 16
PACK = 16
LOG2E = 1.4426950408889634
VT_ROWS = HEAD_DIM + PACK
DSA_KEY_TILE = 512
ATT_TILES = 4
SCORE_ROWS = 128
POOL_ROWS = 256
SURE_PASSES = 8

U_K = 4 * WIDTH
U_COLS = U_K + 2 * WIDTH + Q_LORA + LANES

VMEM_LIMIT = 56 * 1024 * 1024


def _rms(x, gain):
    return x * lax.rsqrt(jnp.mean(x * x, axis=-1, keepdims=True) + EPS) * gain


def _dot(a, b):
    return jnp.dot(a, b, preferred_element_type=F32)


def _rms_head(x, gain, width=HEAD_DIM):
    ones = (lax.broadcasted_iota(I32, (LANES, LANES), 0) < width).astype(BF16)
    ss = _dot((x * x).astype(BF16), ones)
    return x * lax.rsqrt(ss * (1.0 / width) + EPS) * gain


def _dot_nt(a, b):
    return lax.dot_general(a, b, (((1,), (1,)), ((), ())), preferred_element_type=F32)


def _params(*sem):
    return pltpu.CompilerParams(dimension_semantics=sem, vmem_limit_bytes=VMEM_LIMIT)


def _split3(x):
    hi = x.astype(BF16)
    r1 = x - hi.astype(F32)
    mid = r1.astype(BF16)
    lo = (r1 - mid.astype(F32)).astype(BF16)
    return hi, mid, lo


def _hgrn_kernel(q_ref, f_ref, i_ref, g_ref, lbraw_ref, gn_ref, e_ref, o_ref, st_ref,
                 *, layer, n_chunks, unroll):
    @pl.when(pl.program_id(2) == 0)
    def _():
        st_ref[...] = jnp.zeros_like(st_ref)

    lr = lbraw_ref[...]
    ex = jnp.exp(lr - jnp.max(lr, axis=0, keepdims=True))
    sm = ex / jnp.sum(ex, axis=0, keepdims=True)
    lb = jnp.zeros((1, HEAD_DIM), F32)
    for r in range(1, layer + 1):
        lb = lb + sm[r:r + 1, :]
    log_lb = jnp.log(lb)
    log_1mlb = jnp.log(1.0 - lb)

    row = lax.broadcasted_iota(I32, (CHUNK, HEAD_DIM), 0)
    col = lax.broadcasted_iota(I32, (CHUNK, HEAD_DIM), 1)
    row_l = row & (SUB - 1)
    row_b = row >> SUB_SHIFT
    col_b = col >> SUB_SHIFT
    tri = (lax.broadcasted_iota(I32, (CHUNK, CHUNK), 0)
           >= lax.broadcasted_iota(I32, (CHUNK, CHUNK), 1)).astype(BF16)
    n_sub = CHUNK // SUB
    zeros_c = jnp.zeros((CHUNK, HEAD_DIM), F32)

    def sub_bcast(t, s):
        t4 = t.reshape(n_sub, SUB, HEAD_DIM)
        return jnp.broadcast_to(t4[:, s:s + 1, :], (n_sub, SUB, HEAD_DIM)).reshape(CHUNK, HEAD_DIM)

    def gates(z):
        ls = jnp.minimum(z, 0.0) - jnp.log(1.0 + jnp.exp(-jnp.abs(z)))
        if layer == 0:
            return ls, (ls - z) * LOG2E
        cc = log_1mlb + ls
        log_f = jnp.maximum(log_lb, cc) + jnp.log(1.0 + jnp.exp(-jnp.abs(log_lb - cc)))
        return log_f, (cc - z) * LOG2E

    def diag_terms(q, b2, c2):
        pieces = []
        for s in range(SUB):
            d = jnp.where(row_l >= s, b2 - sub_bcast(c2, s), -jnp.inf)
            pieces.append((q * jnp.exp2(d)).astype(BF16))
        return jnp.concatenate(pieces, axis=1)

    def below_keys(b2, c2):
        kts = []
        for i in range(1, n_sub):
            r_i = b2[i * SUB:i * SUB + 1, :]
            kts.append(jnp.exp2(jnp.where(row < i * SUB, r_i - c2, -jnp.inf)))
            kts.append(zeros_c)
        return jnp.concatenate(kts, axis=0).astype(BF16)

    def below_scores(r):
        a = jnp.zeros((CHUNK, HEAD_DIM), F32)
        for i in range(1, n_sub):
            a = a + jnp.where(row_b == i, r[:, (i - 1) * LANES:i * LANES], 0.0)
        return a

    def chunks(it, carry):
        cs = range(unroll)
        rows = [pl.ds(pl.multiple_of((it * unroll + c) * CHUNK, CHUNK), CHUNK) for c in cs]
        qr = [q_ref[r, :] for r in rows]
        v = [i_ref[r, :] for r in rows]
        q = [x * jax.nn.sigmoid(x) for x in qr]
        gt = [gates(f_ref[r, :]) for r in rows]
        b2 = []
        for c in cs:
            hi, mid, lo = _split3(gt[c][0])
            b2.append((_dot(tri, hi) + _dot(tri, mid) + _dot(tri, lo)) * LOG2E)
        c2 = [b2[c] - gt[c][1] for c in cs]
        b_last = [x[CHUNK - 1:CHUNK, :] for x in b2]
        a2 = [_dot(diag_terms(q[c], b2[c], c2[c]), e_ref[...]) for c in cs]
        rr = [_dot_nt((q[c] * jnp.exp2(b2[c] - sub_bcast(b2[c], 0))).astype(BF16), below_keys(b2[c], c2[c]))
              for c in cs]
        upd = [_dot(v[c].T.astype(BF16), jnp.exp2(b_last[c] - c2[c]).astype(BF16)) for c in cs]
        qe = [(q[c] * jnp.exp2(b2[c])).astype(BF16) for c in cs]
        st = st_ref[...]
        o_inter = []
        for c in cs:
            o_inter.append(_dot_nt(qe[c], st.astype(BF16)))
            st = st * jnp.exp2(b_last[c]) + upd[c]
        st_ref[...] = st
        a = [jnp.where(col_b == row_b, a2[c], 0.0) + below_scores(rr[c]) for c in cs]
        o_intra = [_dot(a[c][:, :CHUNK].astype(BF16), v[c].astype(BF16)) for c in cs]
        for c in cs:
            g = g_ref[rows[c], :]
            o_ref[rows[c], :] = _rms(o_inter[c] + o_intra[c], gn_ref[...]) * (g * jax.nn.sigmoid(g))
        return carry

    lax.fori_loop(0, n_chunks // unroll, chunks, 0)


def _hgrn_selector():
    e = np.zeros((SUB * HEAD_DIM, LANES), np.float32)
    s_of_row = np.arange(SUB * HEAD_DIM) // HEAD_DIM
    cols = np.arange(LANES)
    e[:, :] = ((cols[None, :] % SUB) == s_of_row[:, None]) & (cols[None, :] < CHUNK)
    return jnp.asarray(e, BF16)


def _hgrn(u, lb_raw, gn, layer, bsz, seq, tb=2048):
    n = bsz * seq
    tb = min(tb, seq)
    nb = seq // tb
    blk = lambda k: pl.BlockSpec((tb, HEAD_DIM), lambda b, h, j, k=k: (b * nb + j, h + N_HEADS * k))
    n_layers = lb_raw.shape[0]
    return pl.pallas_call(
        functools.partial(_hgrn_kernel, layer=layer, n_chunks=tb // CHUNK, unroll=16),
        out_shape=jax.ShapeDtypeStruct((n, WIDTH), F32),
        grid=(bsz, N_HEADS, nb),
        in_specs=[blk(0), blk(1), blk(2), blk(3),
                  pl.BlockSpec((n_layers, HEAD_DIM), lambda b, h, j: (0, h)),
                  pl.BlockSpec((1, HEAD_DIM), lambda b, h, j: (0, 0)),
                  pl.BlockSpec((SUB * HEAD_DIM, LANES), lambda b, h, j: (0, 0))],
        out_specs=pl.BlockSpec((tb, HEAD_DIM), lambda b, h, j: (b * nb + j, h)),
        scratch_shapes=[pltpu.VMEM((HEAD_DIM, HEAD_DIM), F32)],
        compiler_params=_params("parallel", "parallel", "arbitrary"),
        name="hgrn",
    )(u, u, u, u, lb_raw, gn, _hgrn_selector())


def _rope128(x, cos, sin_signed):
    return x * cos + pltpu.roll(x, HEAD_DIM // 2, 1) * sin_signed


def _rope64(x, cos, sin_a, sin_b):
    return x * cos + pltpu.roll(x, LANES - ROT_I, 1) * sin_a + pltpu.roll(x, ROT_I, 1) * sin_b


def _proj_prep_kernel(x_ref, g_ref, w_ref, ta_ref, tb_ref, wqb_ref, wqi_ref, gql_ref, gqn_ref, gkn_ref, gki_ref,
                      ua_out, q_out, k_out, vt_out, qi_out, ki_out, wt_out):
    hb = _rms(x_ref[...], g_ref[...]).astype(BF16)
    ub = _dot(hb, w_ref[:, U_K:])
    k, v = ub[:, :WIDTH], ub[:, WIDTH:2 * WIDTH]
    ql, misc = ub[:, 2 * WIDTH:2 * WIDTH + Q_LORA], ub[:, 2 * WIDTH + Q_LORA:]

    def project(c):
        ua_out[:, c * WIDTH:(c + 1) * WIDTH] = _dot(hb, w_ref[:, c * WIDTH:(c + 1) * WIDTH])

    ta, tb = ta_ref[...], tb_ref[...]
    lane = lax.broadcasted_iota(I32, ta.shape, 1)
    ta_swapped = pltpu.roll(ta, HEAD_DIM // 2, 1)
    cm = jnp.where(lane < HEAD_DIM // 2, ta, ta_swapped)
    sm = jnp.where(lane < HEAD_DIM // 2, -ta_swapped, ta)
    quarter = lane >> (ROT_I.bit_length() - 1)
    r32, r64, r96 = (pltpu.roll(tb, sh * ROT_I, 1) for sh in (1, 2, 3))
    ci = jnp.where(quarter == 0, tb, jnp.where(quarter == 1, r32, jnp.where(quarter == 2, r64, r96)))
    si = jnp.where(quarter == 0, r96, jnp.where(quarter == 1, tb, jnp.where(quarter == 2, r32, r64)))
    sia = jnp.where((lane & ROT_I) == 0, -si, 0.0)
    sib = jnp.where((lane & ROT_I) != 0, si, 0.0)
    cb = _rms(ql, gql_ref[...]).astype(BF16)
    scale = HEAD_DIM ** -0.5

    def prep_q():
        qf = _dot(cb, wqb_ref[...])
        for h in range(N_HEADS):
            sl = slice(h * HEAD_DIM, (h + 1) * HEAD_DIM)
            qh = _rope128(_rms_head(qf[:, sl], gqn_ref[...]), cm, sm)
            q_out[:, sl] = (qh * (scale * LOG2E)).astype(BF16)

    def prep_k():
        for h in range(N_HEADS):
            sl = slice(h * HEAD_DIM, (h + 1) * HEAD_DIM)
            kh = _rope128(_rms_head(k[:, sl], gkn_ref[...]), cm, sm)
            k_out[:, sl] = kh.astype(BF16)

    def prep_v():
        vt = v.T.astype(BF16)
        ones = jnp.ones((PACK, vt.shape[1]), BF16)
        for h in range(N_HEADS):
            vt_out[0, h * VT_ROWS:h * VT_ROWS + HEAD_DIM, :] = vt[h * HEAD_DIM:(h + 1) * HEAD_DIM, :]
            vt_out[0, h * VT_ROWS + HEAD_DIM:(h + 1) * VT_ROWS, :] = ones

    def prep_index(groups):
        qi = _dot(cb, wqi_ref[:, groups[0] * LANES:(groups[-1] + 1) * LANES])
        for n_, p in enumerate(groups):
            qi_out[:, p * LANES:(p + 1) * LANES] = _rope64(qi[:, n_ * LANES:(n_ + 1) * LANES], ci, sia, sib).astype(BF16)

    def prep_keys_index():
        kn = _rope64(_rms_head(misc, gki_ref[...], IDX_DIM), ci, sia, sib)
        ki_out[...] = (kn + pltpu.roll(kn, IDX_DIM, 1)).astype(BF16)
        wt = (misc * (IDX_HEADS ** -0.5 * IDX_DIM ** -0.5)).T
        wt_out[...] = wt[IDX_DIM:IDX_DIM + IDX_HEADS, :]

    prep_q()
    project(0)
    prep_v()
    prep_index(range(0, 4))
    project(1)
    prep_k()
    prep_index(range(4, 8))
    project(2)
    prep_keys_index()
    project(3)


def _proj_prep(x2, gain, w_bf, tabs, wqb, wqi, gql, gqn, gkn, gki, tm=DSA_KEY_TILE):
    n, d = x2.shape
    nt = n // tm
    row = lambda w: pl.BlockSpec((tm, w), lambda i: (i, 0))
    full = lambda a: pl.BlockSpec(a.shape, lambda i: (0,) * a.ndim)
    return pl.pallas_call(
        _proj_prep_kernel,
        out_shape=(jax.ShapeDtypeStruct((n, U_K), F32),
                   jax.ShapeDtypeStruct((n, WIDTH), BF16),
                   jax.ShapeDtypeStruct((n, WIDTH), BF16),
                   jax.ShapeDtypeStruct((nt, N_HEADS * VT_ROWS, tm), BF16),
                   jax.ShapeDtypeStruct((n, IDX_HEADS * IDX_DIM), BF16),
                   jax.ShapeDtypeStruct((n, LANES), BF16),
                   jax.ShapeDtypeStruct((IDX_HEADS, n), F32)),
        grid=(nt,),
        in_specs=[row(d), full(gain), pl.BlockSpec(w_bf.shape, lambda i: (0, 0), pipeline_mode=pl.Buffered(1)),
                  row(LANES), row(LANES),
                  full(wqb), full(wqi), full(gql), full(gqn), full(gkn), full(gki)],
        out_specs=(row(U_K), row(WIDTH), row(WIDTH),
                   pl.BlockSpec((1, N_HEADS * VT_ROWS, tm), lambda i: (i, 0, 0)),
                   row(IDX_HEADS * IDX_DIM), row(LANES),
                   pl.BlockSpec((IDX_HEADS, tm), lambda i: (0, i))),
        compiler_params=_params("parallel"),
        name="proj_prep",
    )(x2, gain, w_bf, *tabs, wqb, wqi, gql, gqn, gkn, gki)


def _sortable(x):
    bits = lax.bitcast_convert_type(x, I32)
    return bits ^ ((bits >> 31) & 0x7FFFFFFF)


def _dsa_attn_kernel(qi_ref, wt_ref, qin_ref, wtn_ref, q_ref, ki_ref, k_ref, vt_ref, o_ref,
                     qm_ref, hi_ref, lo_ref, kb_ref, pk_ref, lt_ref, acc0, acc1, acc2, acc3,
                     *, tq, kt, top_k, n_blocks):
    acc_refs = (acc0, acc1, acc2, acc3)
    blk = pl.program_id(1)
    cur = blk & 1
    nxt = 1 - cur
    q0 = blk * tq
    n_keys = q0 + tq
    sub = SCORE_ROWS
    key_iota = lax.broadcasted_iota(I32, (sub, tq), 0)
    lane_q = lax.broadcasted_iota(I32, (tq, LANES), 1)
    pooled_per_group = 2 * PACK

    def sweep(total, fn, carry, big=2):
        n_full = total // kt
        carry = lax.fori_loop(0, n_full // big,
                              lambda jp, c: fn(pl.multiple_of(jp * big * kt, big * kt), big * kt, c), carry)
        w = big // 2
        while w >= 1:
            first = (n_full // (2 * w)) * (2 * w)
            carry = lax.cond((n_full & w) != 0,
                             lambda c, first=first, w=w: fn(pl.multiple_of(first * kt, w * kt), w * kt, c),
                             lambda c: c, carry)
            w //= 2
        return lax.cond(total % kt != 0, lambda c: fn(pl.multiple_of(n_full * kt, kt), kt // 2, c),
                        lambda c: c, carry)

    def limit_of(q_start):
        qpos = q_start + lax.broadcasted_iota(I32, (1, tq), 1)
        return ((qpos >> CHUNK_SHIFT) + 1) << CHUNK_SHIFT

    def build_qm(src_ref):
        for h in range(IDX_HEADS):
            src = src_ref[:, (h // 2) * LANES:(h // 2 + 1) * LANES].astype(F32)
            keep = (lane_q < IDX_DIM) if h % 2 == 0 else (lane_q >= IDX_DIM)
            qm_ref[h * tq:(h + 1) * tq, :] = jnp.where(keep, src, 0.0).astype(BF16)

    def score_rows(start, size, slot, limit, w_ref):
        for part in range(size // sub):
            base = start + part * sub
            rows = pl.ds(pl.multiple_of(base, sub), sub)
            ki = ki_ref[rows, :]
            acc = jnp.zeros((sub, tq), F32)
            for h in range(IDX_HEADS):
                x = _dot_nt(ki, qm_ref[h * tq:(h + 1) * tq, :])
                acc = acc + w_ref[h:h + 1, :] * jnp.maximum(x, 0.0)
            key = jnp.where(base + key_iota < limit, _sortable(acc), INT_MIN)
            hi_ref[slot, rows, :] = (key >> HALF_BITS).astype(I16)
            lo_ref[slot, rows, :] = ((key & (2 ** HALF_BITS - 1)) + INT16_MIN).astype(I16)

    @pl.when(blk == 0)
    def _():
        build_qm(qi_ref)
        limit0 = limit_of(0)

        def body(start, size, carry):
            score_rows(start, size, 0, limit0, wt_ref)
            return carry
        sweep(n_keys, body, 0)

    one, zero = jnp.ones((), BF16), jnp.zeros((), BF16)
    neg_inf = jnp.full((), -jnp.inf, BF16)
    hi_rows = lambda rows: hi_ref[cur, rows, :]
    lo_rows = lambda rows: lo_ref[cur, rows, :]
    kb_rows = lambda rows: kb_ref[rows, :]

    def count16(get, trial, strict, total=None):
        t16 = jnp.broadcast_to(trial.astype(I16), (PACK, tq))

        def hits(start, size, c):
            x = get(pl.ds(start, size))
            parts = [jnp.zeros((PACK, tq), BF16) for _ in range(4)]
            for r in range(size // PACK):
                xr = x[r * PACK:(r + 1) * PACK, :]
                parts[r % 4] = parts[r % 4] + jnp.where((xr > t16) if strict else (xr >= t16), one, zero)
            return c + ((parts[0] + parts[1]) + (parts[2] + parts[3])).astype(F32)

        c = sweep(n_keys if total is None else total, hits, jnp.zeros((PACK, tq), F32))
        return jnp.sum(c, axis=0, keepdims=True)

    def bisect16(get, want, n_all, early_exit, total=None):
        def step(it, carry):
            lo, n_lo, n_up = carry
            trial = lo + lax.shift_left(jnp.int32(1), HALF_BITS - 1 - it)
            n = count16(get, trial, False, total)
            ok = n >= want
            return jnp.where(ok, trial, lo), jnp.where(ok, n, n_lo), jnp.where(ok, n_up, n)

        init = (jnp.full((1, tq), INT16_MIN, I32), n_all, jnp.zeros((1, tq), F32))
        if not early_exit:
            return lax.fori_loop(0, HALF_BITS, step, init)

        def unresolved(carry):
            it, _, n_lo, _ = carry
            return (it < HALF_BITS) & (jnp.max(jnp.where(n_lo != want, 1.0, 0.0)) > 0.0)

        def two_steps(carry):
            return (carry[0] + 2,) + step(carry[0] + 1, step(carry[0], carry[1:]))

        return lax.while_loop(unresolved, two_steps, (jnp.int32(SURE_PASSES),) + lax.fori_loop(0, SURE_PASSES, step, init))[1:]

    tau_hi, n_hi_ge, n_hi_gt = bisect16(hi_rows, float(top_k), jnp.full((1, tq), n_keys.astype(F32), F32), False)
    want_lo = top_k - n_hi_gt
    tau_hi_row = tau_hi.astype(I16)

    min16 = jnp.full((PACK, tq), INT16_MIN, I16)

    def bucket_rows(start, size, carry):
        rows = pl.ds(start, size)
        kb = jnp.where(hi_rows(rows) == jnp.broadcast_to(tau_hi_row, (size, tq)), lo_rows(rows),
                       jnp.full((), INT16_MIN, I16))
        kb_ref[rows, :] = kb
        for g in range(size // POOL_ROWS):
            top1, top2 = min16, min16
            for r in range(POOL_ROWS // PACK):
                x = kb[g * POOL_ROWS + r * PACK:g * POOL_ROWS + (r + 1) * PACK, :]
                below = x < top1
                runner = jnp.where(below, x, top1)
                top2 = jnp.where(runner > top2, runner, top2)
                top1 = jnp.where(below, top1, x)
            base = pl.multiple_of((start // POOL_ROWS + g) * pooled_per_group, pooled_per_group)
            pk_ref[pl.ds(base, PACK), :] = top1
            pk_ref[pl.ds(base + PACK, PACK), :] = top2
        return carry

    sweep(n_keys, bucket_rows, 0)
    n_groups = n_keys // POOL_ROWS
    pooled_total = (n_groups * pooled_per_group + kt // 2 - 1) // (kt // 2) * (kt // 2)

    def pad_pooled(g, carry):
        pk_ref[pl.ds(pl.multiple_of(g * pooled_per_group, pooled_per_group), pooled_per_group), :] = jnp.full(
            (pooled_per_group, tq), INT16_MIN, I16)
        return carry

    lax.fori_loop(n_groups, pooled_total // pooled_per_group, pad_pooled, 0)
    n_bucket = n_hi_ge - n_hi_gt
    pk_rows = lambda rows: pk_ref[rows, :]
    tau_p, _, _ = bisect16(pk_rows, want_lo, n_bucket, True, pooled_total)
    n_ge_p = jnp.where(tau_p == INT16_MIN, n_bucket, count16(kb_rows, tau_p, False))
    n_gt_p = count16(kb_rows, tau_p, True)
    pooled_ok = (n_ge_p == want_lo) | ((n_ge_p > want_lo) & (n_gt_p < want_lo))

    def full_search():
        t, n_ge, _ = bisect16(kb_rows, want_lo, n_bucket, True)
        return t, n_ge, count16(kb_rows, t, True)

    tau_lo, n_kb_ge, n_kb_gt = lax.cond(
        jnp.min(jnp.where(pooled_ok, 1.0, 0.0)) > 0.0,
        lambda: (tau_p, n_ge_p, n_gt_p), full_search)
    live = (tau_hi > INT16_MIN) | (tau_lo > INT16_MIN)
    excess = jnp.max(jnp.where(live, n_kb_ge - want_lo, 0.0))

    @pl.when(excess > 0)
    def _():
        need = want_lo - n_kb_gt
        live_f = jnp.where(live, 1.0, 0.0)
        step_rows = kt // 2
        tri = (lax.broadcasted_iota(I32, (step_rows, step_rows), 0)
               >= lax.broadcasted_iota(I32, (step_rows, step_rows), 1)).astype(BF16)

        def body(start, size, seen):
            for part in range(size // step_rows):
                rows = pl.ds(pl.multiple_of(start + part * step_rows, step_rows), step_rows)
                h32 = hi_rows(rows).astype(I32)
                eq = jnp.where(h32 == tau_hi, jnp.where(lo_rows(rows).astype(I32) == tau_lo, 1.0, 0.0), 0.0)
                rank = _dot(tri, eq.astype(BF16)) + seen
                dropped = eq * (1.0 - jnp.where(rank <= need, live_f, 0.0))
                hi_ref[cur, rows, :] = jnp.where(dropped > 0.0, INT16_MIN, h32).astype(I16)
                seen = rank[step_rows - 1:step_rows, :]
            return seen
        sweep(n_keys, body, jnp.zeros((1, tq), F32))

    tau_lo_row = jnp.where(live, tau_lo, INT16_MAX).astype(I16)

    def bias_rows(rows, size):
        h16 = hi_rows(rows)
        th = jnp.broadcast_to(tau_hi_row, (size, tq))
        in_bucket = jnp.where(lo_rows(rows) >= jnp.broadcast_to(tau_lo_row, (size, tq)), zero, neg_inf)
        return jnp.where(h16 > th, zero, jnp.where(h16 == th, in_bucket, neg_inf)).astype(F32)

    for a in acc_refs:
        a[...] = jnp.zeros_like(a)
    heads = [slice(h * HEAD_DIM, (h + 1) * HEAD_DIM) for h in range(N_HEADS)]
    limit_next = limit_of(q0 + tq)

    def attend(start, size, ms, score_next):
        segs = [(start + j * kt, kt) for j in range(size // kt)] if size >= kt else [(start, size)]
        for i, (s0, n) in enumerate(segs):
            rows = pl.ds(pl.multiple_of(s0, kt // 2), n)
            bias = bias_rows(rows, n)
            for h in range(N_HEADS):
                lt_ref[i, h, :n, :] = _dot_nt(k_ref[rows, heads[h]], q_ref[:, heads[h]]) + bias
        ms = list(ms)
        for i, (s0, n) in enumerate(segs):
            if score_next:
                score_rows(s0, n, nxt, limit_next, wtn_ref)
            alphas, ps = [], []
            for h in range(N_HEADS):
                lt = lt_ref[i, h, :n, :]
                m_new = jnp.maximum(ms[h], jnp.max(lt, axis=0, keepdims=True))
                m_safe = jnp.where(m_new == -jnp.inf, 0.0, m_new)
                ps.append(jnp.exp2(lt - m_safe).astype(BF16))
                alphas.append(jnp.exp2(ms[h] - m_safe))
                ms[h] = m_new
            for h in range(N_HEADS):
                acc_refs[h][...] = (acc_refs[h][...] * alphas[h]
                                    + _dot(vt_ref[s0 // kt, h * VT_ROWS:(h + 1) * VT_ROWS, :n], ps[h]))
        return tuple(ms)

    m_init = tuple(jnp.full((1, tq), -jnp.inf, F32) for _ in range(N_HEADS))

    @pl.when(blk < n_blocks - 1)
    def _():
        build_qm(qin_ref)
        sweep(n_keys, functools.partial(attend, score_next=True), m_init, big=ATT_TILES)
        score_rows(pl.multiple_of(n_keys, kt // 2), tq, nxt, limit_next, wtn_ref)

    @pl.when(blk == n_blocks - 1)
    def _():
        sweep(n_keys, functools.partial(attend, score_next=False), m_init, big=ATT_TILES)

    for h in range(N_HEADS):
        acc = acc_refs[h][...]
        out_t = acc[:HEAD_DIM, :] * (1.0 / acc[HEAD_DIM:HEAD_DIM + 1, :])
        o_ref[:, heads[h]] = out_t.T


def _dsa_attn(q, k, vt, qi, ki, wt, bsz, seq, tq=256, kt=DSA_KEY_TILE):
    n = bsz * seq
    nq = seq // tq
    top_k = min(TOPK_MAX, seq // 4)
    once = dict(pipeline_mode=pl.Buffered(1))
    nxt_blk = lambda b, i: b * nq + jnp.minimum(i + 1, nq - 1)
    return pl.pallas_call(
        functools.partial(_dsa_attn_kernel, tq=tq, kt=kt, top_k=top_k, n_blocks=nq),
        out_shape=jax.ShapeDtypeStruct((n, WIDTH), F32),
        grid=(bsz, nq),
        in_specs=[pl.BlockSpec((tq, IDX_HEADS * IDX_DIM), lambda b, i: (b * nq + i, 0)),
                  pl.BlockSpec((IDX_HEADS, tq), lambda b, i: (0, b * nq + i)),
                  pl.BlockSpec((tq, IDX_HEADS * IDX_DIM), lambda b, i: (nxt_blk(b, i), 0)),
                  pl.BlockSpec((IDX_HEADS, tq), lambda b, i: (0, nxt_blk(b, i))),
                  pl.BlockSpec((tq, WIDTH), lambda b, i: (b * nq + i, 0)),
                  pl.BlockSpec((seq, LANES), lambda b, i: (b, 0), **once),
                  pl.BlockSpec((seq, WIDTH), lambda b, i: (b, 0), **once),
                  pl.BlockSpec((seq // kt, N_HEADS * VT_ROWS, kt), lambda b, i: (b, 0, 0), **once)],
        out_specs=pl.BlockSpec((tq, WIDTH), lambda b, i: (b * nq + i, 0)),
        scratch_shapes=[pltpu.VMEM((IDX_HEADS * tq, LANES), BF16),
                        pltpu.VMEM((2, seq, tq), I16), pltpu.VMEM((2, seq, tq), I16), pltpu.VMEM((seq, tq), I16),
                        pltpu.VMEM((pl.cdiv(seq // POOL_ROWS * 2 * PACK, kt) * kt, tq), I16),
                        pltpu.VMEM((ATT_TILES, N_HEADS, kt, tq), F32)]
                       + [pltpu.VMEM((VT_ROWS, tq), F32) for _ in range(N_HEADS)],
        compiler_params=_params("arbitrary", "arbitrary"),
        name="dsa_attn",
    )(qi, wt, qi, wt, q, ki, k, vt)


def _mem_kv_kernel(mem_ref, g_ref, wk_ref, wv_ref, gk_ref, k_out, v_out):
    mb = _rms(mem_ref[...], g_ref[...]).astype(BF16)
    kf = _dot(mb, wk_ref[...])
    for h in range(N_HEADS):
        sl = slice(h * HEAD_DIM, (h + 1) * HEAD_DIM)
        k_out[:, sl] = _rms_head(kf[:, sl], gk_ref[...]).astype(BF16)
    v_out[...] = _dot(mb, wv_ref[...]).astype(BF16)


def _mem_kv(mem2, gain, wk, wv, gk, n_mem):
    n, d = mem2.shape
    full = lambda a: pl.BlockSpec(a.shape, lambda b: (0,) * a.ndim)
    return pl.pallas_call(
        _mem_kv_kernel,
        out_shape=(jax.ShapeDtypeStruct((n, WIDTH), BF16), jax.ShapeDtypeStruct((n, WIDTH), BF16)),
        grid=(n // n_mem,),
        in_specs=[pl.BlockSpec((n_mem, d), lambda b: (b, 0)), full(gain), full(wk), full(wv), full(gk)],
        out_specs=(pl.BlockSpec((n_mem, WIDTH), lambda b: (b, 0)),
                   pl.BlockSpec((n_mem, WIDTH), lambda b: (b, 0))),
        compiler_params=_params("parallel"),
        name="mem_kv",
    )(mem2, gain, wk, wv, gk)


def _xattn_kernel(x_ref, oa_ref, ob_ref, wout_ref, g_ref, wq_ref, gq_ref, km_ref, vm_ref, wo_ref, o_ref):
    x1 = (x_ref[...] + _dot(oa_ref[...].astype(BF16), wout_ref[:WIDTH, :])
          + _dot(ob_ref[...].astype(BF16), wout_ref[WIDTH:, :]))
    hb = _rms(x1, g_ref[...]).astype(BF16)
    qf = _dot(hb, wq_ref[...])
    scale = HEAD_DIM ** -0.5
    outs = []
    for h in range(N_HEADS):
        sl = slice(h * HEAD_DIM, (h + 1) * HEAD_DIM)
        qh = (_rms(qf[:, sl], gq_ref[...]) * scale).astype(BF16)
        logits = _dot_nt(qh, km_ref[:, sl])
        p = jnp.exp(logits - jnp.max(logits, axis=-1, keepdims=True))
        oh = _dot(p.astype(BF16), vm_ref[:, sl])
        outs.append((oh * (1.0 / jnp.sum(p, axis=-1, keepdims=True))).astype(BF16))
    o_ref[...] = x1 + _dot(jnp.concatenate(outs, axis=1), wo_ref[...])


def _xattn(x2, oa, ob, wout, gain, wq, gq, km, vm, wo, seq, n_mem, tm=1024):
    n, d = x2.shape
    per_b = seq // tm
    row = lambda w: pl.BlockSpec((tm, w), lambda i: (i, 0))
    full = lambda a: pl.BlockSpec(a.shape, lambda i: (0,) * a.ndim)
    memb = pl.BlockSpec((n_mem, WIDTH), lambda i: (i // per_b, 0))
    return pl.pallas_call(
        _xattn_kernel,
        out_shape=jax.ShapeDtypeStruct((n, d), F32),
        grid=(n // tm,),
        in_specs=[row(d), row(WIDTH), row(WIDTH), full(wout), full(gain), full(wq), full(gq),
                  memb, memb, full(wo)],
        out_specs=row(d),
        compiler_params=_params("parallel"),
        name="xattn",
    )(x2, oa, ob, wout, gain, wq, gq, km, vm, wo)


def _moe_kernel(x_ref, g_ref, wr_hi_ref, wr_lo_ref, br_ref, wgu_ref, wdn_ref, o_ref, hs_ref, gs_ref, ys_ref,
                *, tm, sub):
    lane = lax.broadcasted_iota(I32, (tm, LANES), 1)
    lane_f = lane.astype(F32)
    x = x_ref[...]
    h = _rms(x, g_ref[...])
    h_hi = h.astype(BF16)
    h_lo = (h - h_hi.astype(F32)).astype(BF16)
    lg = (_dot(h_hi, wr_hi_ref[...]) + _dot(h_lo, wr_hi_ref[...]) + _dot(h_hi, wr_lo_ref[...])
          + br_ref[...])
    first = lambda cond: jnp.min(jnp.where(cond, lane_f, 1e9), axis=-1, keepdims=True)
    gl = jnp.where(lane < N_GROUPS, lg, -jnp.inf)
    gmax = jnp.max(gl, axis=-1, keepdims=True)
    gsel = first(gl == gmax)
    g_w = 1.0 / jnp.sum(jnp.exp(gl - gmax), axis=-1, keepdims=True)
    grp_of_lane = ((lane - N_GROUPS) >> (EXP_PER_GROUP.bit_length() - 1)).astype(F32)
    in_grp = (lane >= N_GROUPS) & (lane < N_GROUPS + N_EXPERTS) & (grp_of_lane == gsel)
    el = jnp.where(in_grp, lg, -jnp.inf)
    v1 = jnp.max(el, axis=-1, keepdims=True)
    i1 = first(el == v1)
    el2 = jnp.where(lane_f == i1, -jnp.inf, el)
    v2 = jnp.max(el2, axis=-1, keepdims=True)
    i2 = first(el2 == v2)
    r = jnp.exp(v2 - v1)
    w1 = 1.0 / (1.0 + r)
    gate = jnp.where(lane_f == i1, w1 * g_w, jnp.where(lane_f == i2, r * w1 * g_w, 0.0))

    onehot = jnp.where(lane_f == gsel, 1.0, 0.0)
    t_row = lax.broadcasted_iota(I32, (tm, tm), 0)
    t_col = lax.broadcasted_iota(I32, (tm, tm), 1)
    before = _dot(jnp.where(t_row > t_col, 1.0, 0.0).astype(BF16), onehot.astype(BF16))
    counts = jnp.sum(onehot, axis=0, keepdims=True)
    lane1 = lax.broadcasted_iota(I32, (1, LANES), 1)
    ends = []
    run = jnp.zeros((1, 1), F32)
    for g in range(N_GROUPS - 1):
        run = run + jnp.sum(jnp.where(lane1 == g, counts, 0.0), axis=-1, keepdims=True)
        ends.append(run)
    start_of = sum(jnp.where(lane1 == g + 1, ends[g], 0.0) for g in range(N_GROUPS - 1))
    pos = jnp.sum(onehot * (before + start_of), axis=-1, keepdims=True)
    to_sorted_t = jnp.where(pos == t_col.astype(F32), 1.0, 0.0)
    to_sorted = to_sorted_t.T.astype(BF16)
    hs_ref[...] = _dot(to_sorted, h_hi).astype(BF16)
    g_hi, g_mid, g_lo = _split3(gate)
    gs_ref[...] = _dot(to_sorted, g_hi) + _dot(to_sorted, g_mid) + _dot(to_sorted, g_lo)

    bounds = [e_[0, 0].astype(I32) for e_ in ends]
    lane_s = lax.broadcasted_iota(I32, (sub, LANES), 1)

    def slab(j, carry):
        r0 = j * sub
        rows = pl.ds(pl.multiple_of(r0, sub), sub)
        g_first = sum((b <= r0).astype(I32) for b in bounds)
        g_last = sum((b <= r0 + sub - 1).astype(I32) for b in bounds)
        hs = hs_ref[rows, :]
        gs = gs_ref[rows, :]
        ys_ref[rows, :] = jnp.zeros((sub, ys_ref.shape[1]), F32)

        def group(g, c):
            gu = _dot(hs, wgu_ref[g])
            acts = []
            for e in range(EXP_PER_GROUP):
                ge = jnp.sum(jnp.where(lane_s == g * EXP_PER_GROUP + e + N_GROUPS, gs, 0.0), axis=-1, keepdims=True)
                up = gu[:, 2 * e * D_EXPERT:(2 * e + 1) * D_EXPERT]
                acts.append((up * jax.nn.sigmoid(up) * gu[:, (2 * e + 1) * D_EXPERT:(2 * e + 2) * D_EXPERT] * ge).astype(BF16))
            ys_ref[rows, :] += _dot(jnp.concatenate(acts, axis=1), wdn_ref[g])
            return c

        lax.fori_loop(g_first, g_last + 1, group, 0)
        return carry

    lax.fori_loop(0, tm // sub, slab, 0)
    o_ref[...] = x + _dot(to_sorted_t.astype(BF16), ys_ref[...].astype(BF16))


def _moe(x2, gain, wr_hi, wr_lo, br, wgu, wdn, tm=512, sub=128):
    n, d = x2.shape
    full = lambda a: pl.BlockSpec(a.shape, lambda i: (0,) * a.ndim)
    once = lambda a: pl.BlockSpec(a.shape, lambda i: (0,) * a.ndim, pipeline_mode=pl.Buffered(1))
    return pl.pallas_call(
        functools.partial(_moe_kernel, tm=tm, sub=sub),
        out_shape=jax.ShapeDtypeStruct((n, d), F32),
        grid=(n // tm,),
        in_specs=[pl.BlockSpec((tm, d), lambda i: (i, 0)), full(gain), full(wr_hi), full(wr_lo), full(br),
                  once(wgu), once(wdn)],
        out_specs=pl.BlockSpec((tm, d), lambda i: (i, 0)),
        scratch_shapes=[pltpu.VMEM((tm, d), BF16), pltpu.VMEM((tm, LANES), F32), pltpu.VMEM((tm, d), F32)],
        compiler_params=_params("parallel"),
        name="moe",
    )(x2, gain, wr_hi, wr_lo, br, wgu, wdn)


def _rope_tables(positions):
    pos = positions.reshape(-1).astype(F32)[:, None]

    def cs(dim):
        inv = ROPE_THETA ** (-jnp.arange(0, dim, 2, dtype=F32) / dim)
        ang = pos * inv
        return jnp.cos(ang), jnp.sin(ang)

    c, s = cs(HEAD_DIM)
    ci, si = cs(IDX_DIM)
    return jnp.concatenate([c, s], 1), jnp.concatenate([ci, si, jnp.zeros_like(ci), jnp.zeros_like(ci)], 1)


def _reorder_in_weight(w):
    a = w[:, :4 * WIDTH]
    o = 4 * WIDTH
    qlat = w[:, o:o + Q_LORA]; o += Q_LORA
    k = w[:, o:o + WIDTH]; o += WIDTH
    v = w[:, o:o + WIDTH]; o += WIDTH
    misc = w[:, o:]
    pad = jnp.zeros((w.shape[0], LANES - misc.shape[1]), w.dtype)
    return jnp.concatenate([a, k, v, qlat, misc, pad], axis=1).astype(BF16)


def _row(v):
    return v.reshape(1, -1).astype(F32)


def kernel(x, mem, positions, norm_mix, w_in, lb_raw, a_gnorm, b_qlat_gain, b_wqb, b_wqidx, b_qnorm,
           b_knorm, b_kidx_norm, w_out, norm_x, norm_mem, x_wq, x_wk, x_wv, x_wo, x_qnorm, x_knorm,
           norm_ffn, w_rg, b_rg, w_re, b_re, w_gu, w_dn):
    bsz, seq, d = x.shape
    n_mem = mem.shape[1]
    depth = w_in.shape[0]
    assert d == D_MODEL and seq % (2 * DSA_KEY_TILE) == 0 and n_mem % LANES == 0, (x.shape, mem.shape)
    x2 = x.reshape(bsz * seq, d)
    mem2 = mem.reshape(bsz * n_mem, d)
    tabs = _rope_tables(positions)
    pad_lanes = lambda v: jnp.pad(v, ((0, 0), (0, LANES - v.shape[1])))
    for l in range(depth):
        u_a, q, k, vt, qi, ki, wt = _proj_prep(
            x2, _row(norm_mix[l]), _reorder_in_weight(w_in[l]), tabs, b_wqb[l].astype(BF16),
            b_wqidx[l].astype(BF16), _row(b_qlat_gain[l]), _row(b_qnorm[l]), _row(b_knorm[l]),
            pad_lanes(_row(b_kidx_norm[l])))
        o_a = _hgrn(u_a, lb_raw.astype(F32), _row(a_gnorm[l]), l, bsz, seq)
        o_b = _dsa_attn(q, k, vt, qi, ki, wt, bsz, seq)
        km, vm = _mem_kv(mem2, _row(norm_mem[l]), x_wk[l].astype(BF16), x_wv[l].astype(BF16),
                         _row(x_knorm[l]), n_mem)
        x2 = _xattn(x2, o_a, o_b, w_out[l].astype(BF16), _row(norm_x[l]), x_wq[l].astype(BF16),
                    _row(x_qnorm[l]), km, vm, x_wo[l].astype(BF16), seq, n_mem)
        wr = pad_lanes(jnp.concatenate([w_rg[l], w_re[l].reshape(d, N_EXPERTS)], axis=1).astype(F32))
        wr_hi = wr.astype(BF16)
        wr_lo = (wr - wr_hi.astype(F32)).astype(BF16)
        br = pad_lanes(jnp.concatenate([b_rg[l], b_re[l].reshape(-1)]).reshape(1, -1).astype(F32))
        x2 = _moe(x2, _row(norm_ffn[l]), wr_hi, wr_lo, br,
                  w_gu[l].transpose(0, 2, 1, 3).reshape(N_GROUPS, d, EXP_PER_GROUP * 2 * D_EXPERT).astype(BF16),
                  w_dn[l].reshape(N_GROUPS, EXP_PER_GROUP * D_EXPERT, d).astype(BF16))
    return x2.reshape(bsz, seq, d)
```

```python
import functools

import numpy as np
import jax
import jax.numpy as jnp
from jax import lax
from jax.experimental import pallas as pl
from jax.experimental.pallas import tpu as pltpu

F32 = jnp.float32
BF16 = jnp.bfloat16
I32 = jnp.int32
I16 = jnp.int16

EPS = 1e-6
ROPE_THETA = 10000.0
CHUNK = 64
CHUNK_SHIFT = CHUNK.bit_length() - 1
SUB = 16
SUB_SHIFT = SUB.bit_length() - 1
N_HEADS = 4
HEAD_DIM = 128
WIDTH = N_HEADS * HEAD_DIM
Q_LORA = 256
D_MODEL = 1024
IDX_HEADS = 16
IDX_DIM = 64
ROT_I = IDX_DIM // 2
TOPK_MAX = 256
N_GROUPS = 4
EXP_PER_GROUP = 4
N_EXPERTS = N_GROUPS * EXP_PER_GROUP
D_EXPERT = 256
LANES = 128
INT_MIN = -2 ** 31
INT16_MIN, INT16_MAX = -2 ** 15, 2 ** 15 - 1
HALF_BITS = 16
PACK = 16
LOG2E = 1.4426950408889634
VT_ROWS = HEAD_DIM + PACK
DSA_KEY_TILE = 512
ATT_TILES = 4
SCORE_ROWS = 128
POOL_ROWS = 256
SURE_PASSES = 8

U_K = 4 * WIDTH
U_COLS = U_K + 2 * WIDTH + Q_LORA + LANES

VMEM_LIMIT = 56 * 1024 * 1024


def _rms(x, gain):
    return x * lax.rsqrt(jnp.mean(x * x, axis=-1, keepdims=True) + EPS) * gain


def _dot(a, b):
    return jnp.dot(a, b, preferred_element_type=F32)


def _rms_head(x, gain, width=HEAD_DIM):
    ones = (lax.broadcasted_iota(I32, (LANES, LANES), 0) < width).astype(BF16)
    ss = _dot((x * x).astype(BF16), ones)
    return x * lax.rsqrt(ss * (1.0 / width) + EPS) * gain


def _dot_nt(a, b):
    return lax.dot_general(a, b, (((1,), (1,)), ((), ())), preferred_element_type=F32)


def _params(*sem):
    return pltpu.CompilerParams(dimension_semantics=sem, vmem_limit_bytes=VMEM_LIMIT)


def _split3(x):
    hi = x.astype(BF16)
    r1 = x - hi.astype(F32)
    mid = r1.astype(BF16)
    lo = (r1 - mid.astype(F32)).astype(BF16)
    return hi, mid, lo


def _hgrn_kernel(q_ref, f_ref, i_ref, g_ref, lbraw_ref, gn_ref, e_ref, o_ref, st_ref,
                 *, layer, n_chunks, unroll):
    @pl.when(pl.program_id(2) == 0)
    def _():
        st_ref[...] = jnp.zeros_like(st_ref)

    lr = lbraw_ref[...]
    ex = jnp.exp(lr - jnp.max(lr, axis=0, keepdims=True))
    sm = ex / jnp.sum(ex, axis=0, keepdims=True)
    lb = jnp.zeros((1, HEAD_DIM), F32)
    for r in range(1, layer + 1):
        lb = lb + sm[r:r + 1, :]
    log_lb = jnp.log(lb)
    log_1mlb = jnp.log(1.0 - lb)

    row = lax.broadcasted_iota(I32, (CHUNK, HEAD_DIM), 0)
    col = lax.broadcasted_iota(I32, (CHUNK, HEAD_DIM), 1)
    row_l = row & (SUB - 1)
    row_b = row >> SUB_SHIFT
    col_b = col >> SUB_SHIFT
    tri = (lax.broadcasted_iota(I32, (CHUNK, CHUNK), 0)
           >= lax.broadcasted_iota(I32, (CHUNK, CHUNK), 1)).astype(BF16)
    n_sub = CHUNK // SUB
    zeros_c = jnp.zeros((CHUNK, HEAD_DIM), F32)

    def sub_bcast(t, s):
        t4 = t.reshape(n_sub, SUB, HEAD_DIM)
        return jnp.broadcast_to(t4[:, s:s + 1, :], (n_sub, SUB, HEAD_DIM)).reshape(CHUNK, HEAD_DIM)

    def gates(z):
        ls = jnp.minimum(z, 0.0) - jnp.log(1.0 + jnp.exp(-jnp.abs(z)))
        if layer == 0:
            return ls, (ls - z) * LOG2E
        cc = log_1mlb + ls
        log_f = jnp.maximum(log_lb, cc) + jnp.log(1.0 + jnp.exp(-jnp.abs(log_lb - cc)))
        return log_f, (cc - z) * LOG2E

    def diag_terms(q, b2, c2):
        pieces = []
        for s in range(SUB):
            d = jnp.where(row_l >= s, b2 - sub_bcast(c2, s), -jnp.inf)
            pieces.append((q * jnp.exp2(d)).astype(BF16))
        return jnp.concatenate(pieces, axis=1)

    def below_keys(b2, c2):
        kts = []
        for i in range(1, n_sub):
            r_i = b2[i * SUB:i * SUB + 1, :]
            kts.append(jnp.exp2(jnp.where(row < i * SUB, r_i - c2, -jnp.inf)))
            kts.append(zeros_c)
        return jnp.concatenate(kts, axis=0).astype(BF16)

    def below_scores(r):
        a = jnp.zeros((CHUNK, HEAD_DIM), F32)
        for i in range(1, n_sub):
            a = a + jnp.where(row_b == i, r[:, (i - 1) * LANES:i * LANES], 0.0)
        return a

    def chunks(it, carry):
        cs = range(unroll)
        rows = [pl.ds(pl.multiple_of((it * unroll + c) * CHUNK, CHUNK), CHUNK) for c in cs]
        qr = [q_ref[r, :] for r in rows]
        v = [i_ref[r, :] for r in rows]
        q = [x * jax.nn.sigmoid(x) for x in qr]
        gt = [gates(f_ref[r, :]) for r in rows]
        b2 = []
        for c in cs:
            hi, mid, lo = _split3(gt[c][0])
            b2.append((_dot(tri, hi) + _dot(tri, mid) + _dot(tri, lo)) * LOG2E)
        c2 = [b2[c] - gt[c][1] for c in cs]
        b_last = [x[CHUNK - 1:CHUNK, :] for x in b2]
        a2 = [_dot(diag_terms(q[c], b2[c], c2[c]), e_ref[...]) for c in cs]
        rr = [_dot_nt((q[c] * jnp.exp2(b2[c] - sub_bcast(b2[c], 0))).astype(BF16), below_keys(b2[c], c2[c]))
              for c in cs]
        upd = [_dot(v[c].T.astype(BF16), jnp.exp2(b_last[c] - c2[c]).astype(BF16)) for c in cs]
        qe = [(q[c] * jnp.exp2(b2[c])).astype(BF16) for c in cs]
        st = st_ref[...]
        o_inter = []
        for c in cs:
            o_inter.append(_dot_nt(qe[c], st.astype(BF16)))
            st = st * jnp.exp2(b_last[c]) + upd[c]
        st_ref[...] = st
        a = [jnp.where(col_b == row_b, a2[c], 0.0) + below_scores(rr[c]) for c in cs]
        o_intra = [_dot(a[c][:, :CHUNK].astype(BF16), v[c].astype(BF16)) for c in cs]
        for c in cs:
            g = g_ref[rows[c], :]
            o_ref[rows[c], :] = _rms(o_inter[c] + o_intra[c], gn_ref[...]) * (g * jax.nn.sigmoid(g))
        return carry

    lax.fori_loop(0, n_chunks // unroll, chunks, 0)


def _hgrn_selector():
    e = np.zeros((SUB * HEAD_DIM, LANES), np.float32)
    s_of_row = np.arange(SUB * HEAD_DIM) // HEAD_DIM
    cols = np.arange(LANES)
    e[:, :] = ((cols[None, :] % SUB) == s_of_row[:, None]) & (cols[None, :] < CHUNK)
    return jnp.asarray(e, BF16)


def _hgrn(u, lb_raw, gn, layer, bsz, seq, tb=4096):
    n = bsz * seq
    tb = min(tb, seq)
    nb = seq // tb
    blk = lambda k: pl.BlockSpec((tb, HEAD_DIM), lambda b, h, j, k=k: (b * nb + j, h + N_HEADS * k))
    n_layers = lb_raw.shape[0]
    return pl.pallas_call(
        functools.partial(_hgrn_kernel, layer=layer, n_chunks=tb // CHUNK, unroll=16),
        out_shape=jax.ShapeDtypeStruct((n, WIDTH), F32),
        grid=(bsz, N_HEADS, nb),
        in_specs=[blk(0), blk(1), blk(2), blk(3),
                  pl.BlockSpec((n_layers, HEAD_DIM), lambda b, h, j: (0, h)),
                  pl.BlockSpec((1, HEAD_DIM), lambda b, h, j: (0, 0)),
                  pl.BlockSpec((SUB * HEAD_DIM, LANES), lambda b, h, j: (0, 0))],
        out_specs=pl.BlockSpec((tb, HEAD_DIM), lambda b, h, j: (b * nb + j, h)),
        scratch_shapes=[pltpu.VMEM((HEAD_DIM, HEAD_DIM), F32)],
        compiler_params=_params("parallel", "parallel", "arbitrary"),
        name="hgrn",
    )(u, u, u, u, lb_raw, gn, _hgrn_selector())


def _rope128(x, cos, sin_signed):
    return x * cos + pltpu.roll(x, HEAD_DIM // 2, 1) * sin_signed


def _rope64(x, cos, sin_a, sin_b):
    return x * cos + pltpu.roll(x, LANES - ROT_I, 1) * sin_a + pltpu.roll(x, ROT_I, 1) * sin_b


def _proj_prep_kernel(x_ref, g_ref, w_ref, ta_ref, tb_ref, wqb_ref, wqi_ref, gql_ref, gqn_ref, gkn_ref, gki_ref,
                      ua_out, q_out, k_out, vt_out, qi_out, ki_out, wt_out):
    hb = _rms(x_ref[...], g_ref[...]).astype(BF16)
    ub = _dot(hb, w_ref[:, U_K:])
    k, v = ub[:, :WIDTH], ub[:, WIDTH:2 * WIDTH]
    ql, misc = ub[:, 2 * WIDTH:2 * WIDTH + Q_LORA], ub[:, 2 * WIDTH + Q_LORA:]

    def project(c):
        ua_out[:, c * WIDTH:(c + 1) * WIDTH] = _dot(hb, w_ref[:, c * WIDTH:(c + 1) * WIDTH])

    ta, tb = ta_ref[...], tb_ref[...]
    lane = lax.broadcasted_iota(I32, ta.shape, 1)
    ta_swapped = pltpu.roll(ta, HEAD_DIM // 2, 1)
    cm = jnp.where(lane < HEAD_DIM // 2, ta, ta_swapped)
    sm = jnp.where(lane < HEAD_DIM // 2, -ta_swapped, ta)
    quarter = lane >> (ROT_I.bit_length() - 1)
    r32, r64, r96 = (pltpu.roll(tb, sh * ROT_I, 1) for sh in (1, 2, 3))
    ci = jnp.where(quarter == 0, tb, jnp.where(quarter == 1, r32, jnp.where(quarter == 2, r64, r96)))
    si = jnp.where(quarter == 0, r96, jnp.where(quarter == 1, tb, jnp.where(quarter == 2, r32, r64)))
    sia = jnp.where((lane & ROT_I) == 0, -si, 0.0)
    sib = jnp.where((lane & ROT_I) != 0, si, 0.0)
    cb = _rms(ql, gql_ref[...]).astype(BF16)
    scale = HEAD_DIM ** -0.5

    def prep_q():
        qf = _dot(cb, wqb_ref[...])
        for h in range(N_HEADS):
            sl = slice(h * HEAD_DIM, (h + 1) * HEAD_DIM)
            qh = _rope128(_rms_head(qf[:, sl], gqn_ref[...]), cm, sm)
            q_out[:, sl] = (qh * (scale * LOG2E)).astype(BF16)

    def prep_k():
        for h in range(N_HEADS):
            sl = slice(h * HEAD_DIM, (h + 1) * HEAD_DIM)
            kh = _rope128(_rms_head(k[:, sl], gkn_ref[...]), cm, sm)
            k_out[:, sl] = kh.astype(BF16)

    def prep_v():
        vt = v.T.astype(BF16)
        ones = jnp.ones((PACK, vt.shape[1]), BF16)
        for h in range(N_HEADS):
            vt_out[0, h * VT_ROWS:h * VT_ROWS + HEAD_DIM, :] = vt[h * HEAD_DIM:(h + 1) * HEAD_DIM, :]
            vt_out[0, h * VT_ROWS + HEAD_DIM:(h + 1) * VT_ROWS, :] = ones

    def prep_index(groups):
        qi = _dot(cb, wqi_ref[:, groups[0] * LANES:(groups[-1] + 1) * LANES])
        for n_, p in enumerate(groups):
            qi_out[:, p * LANES:(p + 1) * LANES] = _rope64(qi[:, n_ * LANES:(n_ + 1) * LANES], ci, sia, sib).astype(BF16)

    def prep_keys_index():
        kn = _rope64(_rms_head(misc, gki_ref[...], IDX_DIM), ci, sia, sib)
        ki_out[...] = (kn + pltpu.roll(kn, IDX_DIM, 1)).astype(BF16)
        wt = (misc * (IDX_HEADS ** -0.5 * IDX_DIM ** -0.5)).T
        wt_out[...] = wt[IDX_DIM:IDX_DIM + IDX_HEADS, :]

    prep_q()
    project(0)
    prep_v()
    prep_index(range(0, 4))
    project(1)
    prep_k()
    prep_index(range(4, 8))
    project(2)
    prep_keys_index()
    project(3)


def _proj_prep(x2, gain, w_bf, tabs, wqb, wqi, gql, gqn, gkn, gki, tm=DSA_KEY_TILE):
    n, d = x2.shape
    nt = n // tm
    row = lambda w: pl.BlockSpec((tm, w), lambda i: (i, 0))
    full = lambda a: pl.BlockSpec(a.shape, lambda i: (0,) * a.ndim)
    return pl.pallas_call(
        _proj_prep_kernel,
        out_shape=(jax.ShapeDtypeStruct((n, U_K), F32),
                   jax.ShapeDtypeStruct((n, WIDTH), BF16),
                   jax.ShapeDtypeStruct((n, WIDTH), BF16),
                   jax.ShapeDtypeStruct((nt, N_HEADS * VT_ROWS, tm), BF16),
                   jax.ShapeDtypeStruct((n, IDX_HEADS * IDX_DIM), BF16),
                   jax.ShapeDtypeStruct((n, LANES), BF16),
                   jax.ShapeDtypeStruct((IDX_HEADS, n), F32)),
        grid=(nt,),
        in_specs=[row(d), full(gain), pl.BlockSpec(w_bf.shape, lambda i: (0, 0), pipeline_mode=pl.Buffered(1)),
                  row(LANES), row(LANES),
                  full(wqb), full(wqi), full(gql), full(gqn), full(gkn), full(gki)],
        out_specs=(row(U_K), row(WIDTH), row(WIDTH),
                   pl.BlockSpec((1, N_HEADS * VT_ROWS, tm), lambda i: (i, 0, 0)),
                   row(IDX_HEADS * IDX_DIM), row(LANES),
                   pl.BlockSpec((IDX_HEADS, tm), lambda i: (0, i))),
        compiler_params=_params("parallel"),
        name="proj_prep",
    )(x2, gain, w_bf, *tabs, wqb, wqi, gql, gqn, gkn, gki)


def _sortable(x):
    bits = lax.bitcast_convert_type(x, I32)
    return bits ^ ((bits >> 31) & 0x7FFFFFFF)


def _dsa_attn_kernel(qi_ref, wt_ref, qin_ref, wtn_ref, q_ref, ki_ref, k_ref, vt_ref, o_ref,
                     qm_ref, hi_ref, lo_ref, kb_ref, pk_ref, lt_ref, acc0, acc1, acc2, acc3,
                     *, tq, kt, top_k, n_blocks):
    acc_refs = (acc0, acc1, acc2, acc3)
    blk = pl.program_id(1)
    cur = blk & 1
    nxt = 1 - cur
    q0 = blk * tq
    n_keys = q0 + tq
    sub = SCORE_ROWS
    key_iota = lax.broadcasted_iota(I32, (sub, tq), 0)
    lane_q = lax.broadcasted_iota(I32, (tq, LANES), 1)
    pooled_per_group = 2 * PACK

    def sweep(total, fn, carry, big=2):
        n_full = total // kt
        carry = lax.fori_loop(0, n_full // big,
                              lambda jp, c: fn(pl.multiple_of(jp * big * kt, big * kt), big * kt, c), carry)
        w = big // 2
        while w >= 1:
            first = (n_full // (2 * w)) * (2 * w)
            carry = lax.cond((n_full & w) != 0,
                             lambda c, first=first, w=w: fn(pl.multiple_of(first * kt, w * kt), w * kt, c),
                             lambda c: c, carry)
            w //= 2
        return lax.cond(total % kt != 0, lambda c: fn(pl.multiple_of(n_full * kt, kt), kt // 2, c),
                        lambda c: c, carry)

    def limit_of(q_start):
        qpos = q_start + lax.broadcasted_iota(I32, (1, tq), 1)
        return ((qpos >> CHUNK_SHIFT) + 1) << CHUNK_SHIFT

    def build_qm(src_ref):
        for h in range(IDX_HEADS):
            src = src_ref[:, (h // 2) * LANES:(h // 2 + 1) * LANES].astype(F32)
            keep = (lane_q < IDX_DIM) if h % 2 == 0 else (lane_q >= IDX_DIM)
            qm_ref[h * tq:(h + 1) * tq, :] = jnp.where(keep, src, 0.0).astype(BF16)

    def score_rows(start, size, slot, limit, w_ref):
        for part in range(size // sub):
            base = start + part * sub
            rows = pl.ds(pl.multiple_of(base, sub), sub)
            ki = ki_ref[rows, :]
            acc = jnp.zeros((sub, tq), F32)
            for h in range(IDX_HEADS):
                x = _dot_nt(ki, qm_ref[h * tq:(h + 1) * tq, :])
                acc = acc + w_ref[h:h + 1, :] * jnp.maximum(x, 0.0)
            key = jnp.where(base + key_iota < limit, _sortable(acc), INT_MIN)
            hi_ref[slot, rows, :] = (key >> HALF_BITS).astype(I16)
            lo_ref[slot, rows, :] = ((key & (2 ** HALF_BITS - 1)) + INT16_MIN).astype(I16)

    @pl.when(blk == 0)
    def _():
        build_qm(qi_ref)
        limit0 = limit_of(0)

        def body(start, size, carry):
            score_rows(start, size, 0, limit0, wt_ref)
            return carry
        sweep(n_keys, body, 0)

    one, zero = jnp.ones((), BF16), jnp.zeros((), BF16)
    neg_inf = jnp.full((), -jnp.inf, BF16)
    hi_rows = lambda rows: hi_ref[cur, rows, :]
    lo_rows = lambda rows: lo_ref[cur, rows, :]
    kb_rows = lambda rows: kb_ref[rows, :]

    def count16(get, trial, strict, total=None):
        t16 = jnp.broadcast_to(trial.astype(I16), (PACK, tq))

        def hits(start, size, c):
            x = get(pl.ds(start, size))
            parts = [jnp.zeros((PACK, tq), BF16) for _ in range(4)]
            for r in range(size // PACK):
                xr = x[r * PACK:(r + 1) * PACK, :]
                parts[r % 4] = parts[r % 4] + jnp.where((xr > t16) if strict else (xr >= t16), one, zero)
            return c + ((parts[0] + parts[1]) + (parts[2] + parts[3])).astype(F32)

        c = sweep(n_keys if total is None else total, hits, jnp.zeros((PACK, tq), F32))
        return jnp.sum(c, axis=0, keepdims=True)

    def bisect16(get, want, n_all, early_exit, total=None):
        def step(it, carry):
            lo, n_lo, n_up = carry
            trial = lo + lax.shift_left(jnp.int32(1), HALF_BITS - 1 - it)
            n = count16(get, trial, False, total)
            ok = n >= want
            return jnp.where(ok, trial, lo), jnp.where(ok, n, n_lo), jnp.where(ok, n_up, n)

        init = (jnp.full((1, tq), INT16_MIN, I32), n_all, jnp.zeros((1, tq), F32))
        if not early_exit:
            return lax.fori_loop(0, HALF_BITS, step, init)

        def unresolved(carry):
            it, _, n_lo, _ = carry
            return (it < HALF_BITS) & (jnp.max(jnp.where(n_lo != want, 1.0, 0.0)) > 0.0)

        def two_steps(carry):
            return (carry[0] + 2,) + step(carry[0] + 1, step(carry[0], carry[1:]))

        return lax.while_loop(unresolved, two_steps, (jnp.int32(SURE_PASSES),) + lax.fori_loop(0, SURE_PASSES, step, init))[1:]

    tau_hi, n_hi_ge, n_hi_gt = bisect16(hi_rows, float(top_k), jnp.full((1, tq), n_keys.astype(F32), F32), False)
    want_lo = top_k - n_hi_gt
    tau_hi_row = tau_hi.astype(I16)

    min16 = jnp.full((PACK, tq), INT16_MIN, I16)

    def bucket_rows(start, size, carry):
        rows = pl.ds(start, size)
        kb = jnp.where(hi_rows(rows) == jnp.broadcast_to(tau_hi_row, (size, tq)), lo_rows(rows),
                       jnp.full((), INT16_MIN, I16))
        kb_ref[rows, :] = kb
        for g in range(size // POOL_ROWS):
            top1, top2 = min16, min16
            for r in range(POOL_ROWS // PACK):
                x = kb[g * POOL_ROWS + r * PACK:g * POOL_ROWS + (r + 1) * PACK, :]
                below = x < top1
                runner = jnp.where(below, x, top1)
                top2 = jnp.where(runner > top2, runner, top2)
                top1 = jnp.where(below, top1, x)
            base = pl.multiple_of((start // POOL_ROWS + g) * pooled_per_group, pooled_per_group)
            pk_ref[pl.ds(base, PACK), :] = top1
            pk_ref[pl.ds(base + PACK, PACK), :] = top2
        return carry

    sweep(n_keys, bucket_rows, 0)
    n_groups = n_keys // POOL_ROWS
    pooled_total = (n_groups * pooled_per_group + kt // 2 - 1) // (kt // 2) * (kt // 2)

    def pad_pooled(g, carry):
        pk_ref[pl.ds(pl.multiple_of(g * pooled_per_group, pooled_per_group), pooled_per_group), :] = jnp.full(
            (pooled_per_group, tq), INT16_MIN, I16)
        return carry

    lax.fori_loop(n_groups, pooled_total // pooled_per_group, pad_pooled, 0)
    n_bucket = n_hi_ge - n_hi_gt
    pk_rows = lambda rows: pk_ref[rows, :]
    tau_p, _, _ = bisect16(pk_rows, want_lo, n_bucket, True, pooled_total)
    n_ge_p = jnp.where(tau_p == INT16_MIN, n_bucket, count16(kb_rows, tau_p, False))
    n_gt_p = count16(kb_rows, tau_p, True)
    pooled_ok = (n_ge_p == want_lo) | ((n_ge_p > want_lo) & (n_gt_p < want_lo))

    def full_search():
        t, n_ge, _ = bisect16(kb_rows, want_lo, n_bucket, True)
        return t, n_ge, count16(kb_rows, t, True)

    tau_lo, n_kb_ge, n_kb_gt = lax.cond(
        jnp.min(jnp.where(pooled_ok, 1.0, 0.0)) > 0.0,
        lambda: (tau_p, n_ge_p, n_gt_p), full_search)
    live = (tau_hi > INT16_MIN) | (tau_lo > INT16_MIN)
    excess = jnp.max(jnp.where(live, n_kb_ge - want_lo, 0.0))

    @pl.when(excess > 0)
    def _():
        need = want_lo - n_kb_gt
        live_f = jnp.where(live, 1.0, 0.0)
        step_rows = kt // 2
        tri = (lax.broadcasted_iota(I32, (step_rows, step_rows), 0)
               >= lax.broadcasted_iota(I32, (step_rows, step_rows), 1)).astype(BF16)

        def body(start, size, seen):
            for part in range(size // step_rows):
                rows = pl.ds(pl.multiple_of(start + part * step_rows, step_rows), step_rows)
                h32 = hi_rows(rows).astype(I32)
                eq = jnp.where(h32 == tau_hi, jnp.where(lo_rows(rows).astype(I32) == tau_lo, 1.0, 0.0), 0.0)
                rank = _dot(tri, eq.astype(BF16)) + seen
                dropped = eq * (1.0 - jnp.where(rank <= need, live_f, 0.0))
                hi_ref[cur, rows, :] = jnp.where(dropped > 0.0, INT16_MIN, h32).astype(I16)
                seen = rank[step_rows - 1:step_rows, :]
            return seen
        sweep(n_keys, body, jnp.zeros((1, tq), F32))

    tau_lo_row = jnp.where(live, tau_lo, INT16_MAX).astype(I16)

    def bias_rows(rows, size):
        h16 = hi_rows(rows)
        th = jnp.broadcast_to(tau_hi_row, (size, tq))
        in_bucket = jnp.where(lo_rows(rows) >= jnp.broadcast_to(tau_lo_row, (size, tq)), zero, neg_inf)
        return jnp.where(h16 > th, zero, jnp.where(h16 == th, in_bucket, neg_inf)).astype(F32)

    for a in acc_refs:
        a[...] = jnp.zeros_like(a)
    heads = [slice(h * HEAD_DIM, (h + 1) * HEAD_DIM) for h in range(N_HEADS)]
    limit_next = limit_of(q0 + tq)

    def attend(start, size, ms, score_next):
        segs = [(start + j * kt, kt) for j in range(size // kt)] if size >= kt else [(start, size)]
        for i, (s0, n) in enumerate(segs):
            rows = pl.ds(pl.multiple_of(s0, kt // 2), n)
            bias = bias_rows(rows, n)
            for h in range(N_HEADS):
                lt_ref[i, h, :n, :] = _dot_nt(k_ref[rows, heads[h]], q_ref[:, heads[h]]) + bias
        ms = list(ms)
        for i, (s0, n) in enumerate(segs):
            if score_next:
                score_rows(s0, n, nxt, limit_next, wtn_ref)
            alphas, ps = [], []
            for h in range(N_HEADS):
                lt = lt_ref[i, h, :n, :]
                m_new = jnp.maximum(ms[h], jnp.max(lt, axis=0, keepdims=True))
                m_safe = jnp.where(m_new == -jnp.inf, 0.0, m_new)
                ps.append(jnp.exp2(lt - m_safe).astype(BF16))
                alphas.append(jnp.exp2(ms[h] - m_safe))
                ms[h] = m_new
            for h in range(N_HEADS):
                acc_refs[h][...] = (acc_refs[h][...] * alphas[h]
                                    + _dot(vt_ref[s0 // kt, h * VT_ROWS:(h + 1) * VT_ROWS, :n], ps[h]))
        return tuple(ms)

    m_init = tuple(jnp.full((1, tq), -jnp.inf, F32) for _ in range(N_HEADS))

    @pl.when(blk < n_blocks - 1)
    def _():
        build_qm(qin_ref)
        sweep(n_keys, functools.partial(attend, score_next=True), m_init, big=ATT_TILES)
        score_rows(pl.multiple_of(n_keys, kt // 2), tq, nxt, limit_next, wtn_ref)

    @pl.when(blk == n_blocks - 1)
    def _():
        sweep(n_keys, functools.partial(attend, score_next=False), m_init, big=ATT_TILES)

    for h in range(N_HEADS):
        acc = acc_refs[h][...]
        out_t = acc[:HEAD_DIM, :] * (1.0 / acc[HEAD_DIM:HEAD_DIM + 1, :])
        o_ref[:, heads[h]] = out_t.T


def _dsa_attn(q, k, vt, qi, ki, wt, bsz, seq, tq=256, kt=DSA_KEY_TILE):
    n = bsz * seq
    nq = seq // tq
    top_k = min(TOPK_MAX, seq // 4)
    once = dict(pipeline_mode=pl.Buffered(1))
    nxt_blk = lambda b, i: b * nq + jnp.minimum(i + 1, nq - 1)
    return pl.pallas_call(
        functools.partial(_dsa_attn_kernel, tq=tq, kt=kt, top_k=top_k, n_blocks=nq),
        out_shape=jax.ShapeDtypeStruct((n, WIDTH), F32),
        grid=(bsz, nq),
        in_specs=[pl.BlockSpec((tq, IDX_HEADS * IDX_DIM), lambda b, i: (b * nq + i, 0)),
                  pl.BlockSpec((IDX_HEADS, tq), lambda b, i: (0, b * nq + i)),
                  pl.BlockSpec((tq, IDX_HEADS * IDX_DIM), lambda b, i: (nxt_blk(b, i), 0)),
                  pl.BlockSpec((IDX_HEADS, tq), lambda b, i: (0, nxt_blk(b, i))),
                  pl.BlockSpec((tq, WIDTH), lambda b, i: (b * nq + i, 0)),
                  pl.BlockSpec((seq, LANES), lambda b, i: (b, 0), **once),
                  pl.BlockSpec((seq, WIDTH), lambda b, i: (b, 0), **once),
                  pl.BlockSpec((seq // kt, N_HEADS * VT_ROWS, kt), lambda b, i: (b, 0, 0), **once)],
        out_specs=pl.BlockSpec((tq, WIDTH), lambda b, i: (b * nq + i, 0)),
        scratch_shapes=[pltpu.VMEM((IDX_HEADS * tq, LANES), BF16),
                        pltpu.VMEM((2, seq, tq), I16), pltpu.VMEM((2, seq, tq), I16), pltpu.VMEM((seq, tq), I16),
                        pltpu.VMEM((pl.cdiv(seq // POOL_ROWS * 2 * PACK, kt) * kt, tq), I16),
                        pltpu.VMEM((ATT_TILES, N_HEADS, kt, tq), F32)]
                       + [pltpu.VMEM((VT_ROWS, tq), F32) for _ in range(N_HEADS)],
        compiler_params=_params("arbitrary", "arbitrary"),
        name="dsa_attn",
    )(qi, wt, qi, wt, q, ki, k, vt)


def _mem_kv_kernel(mem_ref, g_ref, wk_ref, wv_ref, gk_ref, k_out, v_out):
    mb = _rms(mem_ref[...], g_ref[...]).astype(BF16)
    kf = _dot(mb, wk_ref[...])
    for h in range(N_HEADS):
        sl = slice(h * HEAD_DIM, (h + 1) * HEAD_DIM)
        k_out[:, sl] = _rms_head(kf[:, sl], gk_ref[...]).astype(BF16)
    v_out[...] = _dot(mb, wv_ref[...]).astype(BF16)


def _mem_kv(mem2, gain, wk, wv, gk, n_mem):
    n, d = mem2.shape
    full = lambda a: pl.BlockSpec(a.shape, lambda b: (0,) * a.ndim)
    return pl.pallas_call(
        _mem_kv_kernel,
        out_shape=(jax.ShapeDtypeStruct((n, WIDTH), BF16), jax.ShapeDtypeStruct((n, WIDTH), BF16)),
        grid=(n // n_mem,),
        in_specs=[pl.BlockSpec((n_mem, d), lambda b: (b, 0)), full(gain), full(wk), full(wv), full(gk)],
        out_specs=(pl.BlockSpec((n_mem, WIDTH), lambda b: (b, 0)),
                   pl.BlockSpec((n_mem, WIDTH), lambda b: (b, 0))),
        compiler_params=_params("parallel"),
        name="mem_kv",
    )(mem2, gain, wk, wv, gk)


def _xattn_kernel(x_ref, oa_ref, ob_ref, wout_ref, g_ref, wq_ref, gq_ref, km_ref, vm_ref, wo_ref, o_ref):
    x1 = (x_ref[...] + _dot(oa_ref[...].astype(BF16), wout_ref[:WIDTH, :])
          + _dot(ob_ref[...].astype(BF16), wout_ref[WIDTH:, :]))
    hb = _rms(x1, g_ref[...]).astype(BF16)
    qf = _dot(hb, wq_ref[...])
    scale = HEAD_DIM ** -0.5
    outs = []
    for h in range(N_HEADS):
        sl = slice(h * HEAD_DIM, (h + 1) * HEAD_DIM)
        qh = (_rms(qf[:, sl], gq_ref[...]) * scale).astype(BF16)
        logits = _dot_nt(qh, km_ref[:, sl])
        p = jnp.exp(logits - jnp.max(logits, axis=-1, keepdims=True))
        oh = _dot(p.astype(BF16), vm_ref[:, sl])
        outs.append((oh * (1.0 / jnp.sum(p, axis=-1, keepdims=True))).astype(BF16))
    o_ref[...] = x1 + _dot(jnp.concatenate(outs, axis=1), wo_ref[...])


def _xattn(x2, oa, ob, wout, gain, wq, gq, km, vm, wo, seq, n_mem, tm=1024):
    n, d = x2.shape
    per_b = seq // tm
    row = lambda w: pl.BlockSpec((tm, w), lambda i: (i, 0))
    full = lambda a: pl.BlockSpec(a.shape, lambda i: (0,) * a.ndim)
    memb = pl.BlockSpec((n_mem, WIDTH), lambda i: (i // per_b, 0))
    return pl.pallas_call(
        _xattn_kernel,
        out_shape=jax.ShapeDtypeStruct((n, d), F32),
        grid=(n // tm,),
        in_specs=[row(d), row(WIDTH), row(WIDTH), full(wout), full(gain), full(wq), full(gq),
                  memb, memb, full(wo)],
        out_specs=row(d),
        compiler_params=_params("parallel"),
        name="xattn",
    )(x2, oa, ob, wout, gain, wq, gq, km, vm, wo)


def _moe_kernel(x_ref, g_ref, wr_hi_ref, wr_lo_ref, br_ref, wgu_ref, wdn_ref, o_ref, hs_ref, gs_ref, ys_ref,
                *, tm, sub):
    lane = lax.broadcasted_iota(I32, (tm, LANES), 1)
    lane_f = lane.astype(F32)
    x = x_ref[...]
    h = _rms(x, g_ref[...])
    h_hi = h.astype(BF16)
    h_lo = (h - h_hi.astype(F32)).astype(BF16)
    lg = (_dot(h_hi, wr_hi_ref[...]) + _dot(h_lo, wr_hi_ref[...]) + _dot(h_hi, wr_lo_ref[...])
          + br_ref[...])
    first = lambda cond: jnp.min(jnp.where(cond, lane_f, 1e9), axis=-1, keepdims=True)
    gl = jnp.where(lane < N_GROUPS, lg, -jnp.inf)
    gmax = jnp.max(gl, axis=-1, keepdims=True)
    gsel = first(gl == gmax)
    g_w = 1.0 / jnp.sum(jnp.exp(gl - gmax), axis=-1, keepdims=True)
    grp_of_lane = ((lane - N_GROUPS) >> (EXP_PER_GROUP.bit_length() - 1)).astype(F32)
    in_grp = (lane >= N_GROUPS) & (lane < N_GROUPS + N_EXPERTS) & (grp_of_lane == gsel)
    el = jnp.where(in_grp, lg, -jnp.inf)
    v1 = jnp.max(el, axis=-1, keepdims=True)
    i1 = first(el == v1)
    el2 = jnp.where(lane_f == i1, -jnp.inf, el)
    v2 = jnp.max(el2, axis=-1, keepdims=True)
    i2 = first(el2 == v2)
    r = jnp.exp(v2 - v1)
    w1 = 1.0 / (1.0 + r)
    gate = jnp.where(lane_f == i1, w1 * g_w, jnp.where(lane_f == i2, r * w1 * g_w, 0.0))

    onehot = jnp.where(lane_f == gsel, 1.0, 0.0)
    t_row = lax.broadcasted_iota(I32, (tm, tm), 0)
    t_col = lax.broadcasted_iota(I32, (tm, tm), 1)
    before = _dot(jnp.where(t_row > t_col, 1.0, 0.0).astype(BF16), onehot.astype(BF16))
    counts = jnp.sum(onehot, axis=0, keepdims=True)
    lane1 = lax.broadcasted_iota(I32, (1, LANES), 1)
    ends = []
    run = jnp.zeros((1, 1), F32)
    for g in range(N_GROUPS - 1):
        run = run + jnp.sum(jnp.where(lane1 == g, counts, 0.0), axis=-1, keepdims=True)
        ends.append(run)
    start_of = sum(jnp.where(lane1 == g + 1, ends[g], 0.0) for g in range(N_GROUPS - 1))
    pos = jnp.sum(onehot * (before + start_of), axis=-1, keepdims=True)
    to_sorted_t = jnp.where(pos == t_col.astype(F32), 1.0, 0.0)
    to_sorted = to_sorted_t.T.astype(BF16)
    hs_ref[...] = _dot(to_sorted, h_hi).astype(BF16)
    g_hi, g_mid, g_lo = _split3(gate)
    gs_ref[...] = _dot(to_sorted, g_hi) + _dot(to_sorted, g_mid) + _dot(to_sorted, g_lo)

    bounds = [e_[0, 0].astype(I32) for e_ in ends]
    lane_s = lax.broadcasted_iota(I32, (sub, LANES), 1)

    def slab(j, carry):
        r0 = j * sub
        rows = pl.ds(pl.multiple_of(r0, sub), sub)
        g_first = sum((b <= r0).astype(I32) for b in bounds)
        g_last = sum((b <= r0 + sub - 1).astype(I32) for b in bounds)
        hs = hs_ref[rows, :]
        gs = gs_ref[rows, :]
        ys_ref[rows, :] = jnp.zeros((sub, ys_ref.shape[1]), F32)

        def group(g, c):
            gu = _dot(hs, wgu_ref[g])
            acts = []
            for e in range(EXP_PER_GROUP):
                ge = jnp.sum(jnp.where(lane_s == g * EXP_PER_GROUP + e + N_GROUPS, gs, 0.0), axis=-1, keepdims=True)
                up = gu[:, 2 * e * D_EXPERT:(2 * e + 1) * D_EXPERT]
                acts.append((up * jax.nn.sigmoid(up) * gu[:, (2 * e + 1) * D_EXPERT:(2 * e + 2) * D_EXPERT] * ge).astype(BF16))
            ys_ref[rows, :] += _dot(jnp.concatenate(acts, axis=1), wdn_ref[g])
            return c

        lax.fori_loop(g_first, g_last + 1, group, 0)
        return carry

    lax.fori_loop(0, tm // sub, slab, 0)
    o_ref[...] = x + _dot(to_sorted_t.astype(BF16), ys_ref[...].astype(BF16))


def _moe(x2, gain, wr_hi, wr_lo, br, wgu, wdn, tm=512, sub=128):
    n, d = x2.shape
    full = lambda a: pl.BlockSpec(a.shape, lambda i: (0,) * a.ndim)
    once = lambda a: pl.BlockSpec(a.shape, lambda i: (0,) * a.ndim, pipeline_mode=pl.Buffered(1))
    return pl.pallas_call(
        functools.partial(_moe_kernel, tm=tm, sub=sub),
        out_shape=jax.ShapeDtypeStruct((n, d), F32),
        grid=(n // tm,),
        in_specs=[pl.BlockSpec((tm, d), lambda i: (i, 0)), full(gain), full(wr_hi), full(wr_lo), full(br),
                  once(wgu), once(wdn)],
        out_specs=pl.BlockSpec((tm, d), lambda i: (i, 0)),
        scratch_shapes=[pltpu.VMEM((tm, d), BF16), pltpu.VMEM((tm, LANES), F32), pltpu.VMEM((tm, d), F32)],
        compiler_params=_params("parallel"),
        name="moe",
    )(x2, gain, wr_hi, wr_lo, br, wgu, wdn)


def _rope_tables(positions):
    pos = positions.reshape(-1).astype(F32)[:, None]

    def cs(dim):
        inv = ROPE_THETA ** (-jnp.arange(0, dim, 2, dtype=F32) / dim)
        ang = pos * inv
        return jnp.cos(ang), jnp.sin(ang)

    c, s = cs(HEAD_DIM)
    ci, si = cs(IDX_DIM)
    return jnp.concatenate([c, s], 1), jnp.concatenate([ci, si, jnp.zeros_like(ci), jnp.zeros_like(ci)], 1)


def _reorder_in_weight(w):
    a = w[:, :4 * WIDTH]
    o = 4 * WIDTH
    qlat = w[:, o:o + Q_LORA]; o += Q_LORA
    k = w[:, o:o + WIDTH]; o += WIDTH
    v = w[:, o:o + WIDTH]; o += WIDTH
    misc = w[:, o:]
    pad = jnp.zeros((w.shape[0], LANES - misc.shape[1]), w.dtype)
    return jnp.concatenate([a, k, v, qlat, misc, pad], axis=1).astype(BF16)


def _row(v):
    return v.reshape(1, -1).astype(F32)


def kernel(x, mem, positions, norm_mix, w_in, lb_raw, a_gnorm, b_qlat_gain, b_wqb, b_wqidx, b_qnorm,
           b_knorm, b_kidx_norm, w_out, norm_x, norm_mem, x_wq, x_wk, x_wv, x_wo, x_qnorm, x_knorm,
           norm_ffn, w_rg, b_rg, w_re, b_re, w_gu, w_dn):
    bsz, seq, d = x.shape
    n_mem = mem.shape[1]
    depth = w_in.shape[0]
    assert d == D_MODEL and seq % (2 * DSA_KEY_TILE) == 0 and n_mem % LANES == 0, (x.shape, mem.shape)
    x2 = x.reshape(bsz * seq, d)
    mem2 = mem.reshape(bsz * n_mem, d)
    tabs = _rope_tables(positions)
    pad_lanes = lambda v: jnp.pad(v, ((0, 0), (0, LANES - v.shape[1])))
    for l in range(depth):
        u_a, q, k, vt, qi, ki, wt = _proj_prep(
            x2, _row(norm_mix[l]), _reorder_in_weight(w_in[l]), tabs, b_wqb[l].astype(BF16),
            b_wqidx[l].astype(BF16), _row(b_qlat_gain[l]), _row(b_qnorm[l]), _row(b_knorm[l]),
            pad_lanes(_row(b_kidx_norm[l])))
        o_a = _hgrn(u_a, lb_raw.astype(F32), _row(a_gnorm[l]), l, bsz, seq)
        o_b = _dsa_attn(q, k, vt, qi, ki, wt, bsz, seq)
        km, vm = _mem_kv(mem2, _row(norm_mem[l]), x_wk[l].astype(BF16), x_wv[l].astype(BF16),
                         _row(x_knorm[l]), n_mem)
        x2 = _xattn(x2, o_a, o_b, w_out[l].astype(BF16), _row(norm_x[l]), x_wq[l].astype(BF16),
                    _row(x_qnorm[l]), km, vm, x_wo[l].astype(BF16), seq, n_mem)
        wr = pad_lanes(jnp.concatenate([w_rg[l], w_re[l].reshape(d, N_EXPERTS)], axis=1).astype(F32))
        wr_hi = wr.astype(BF16)
        wr_lo = (wr - wr_hi.astype(F32)).astype(BF16)
        br = pad_lanes(jnp.concatenate([b_rg[l], b_re[l].reshape(-1)]).reshape(1, -1).astype(F32))
        x2 = _moe(x2, _row(norm_ffn[l]), wr_hi, wr_lo, br,
                  w_gu[l].transpose(0, 2, 1, 3).reshape(N_GROUPS, d, EXP_PER_GROUP * 2 * D_EXPERT).astype(BF16),
                  w_dn[l].reshape(N_GROUPS, EXP_PER_GROUP * D_EXPERT, d).astype(BF16))
    return x2.reshape(bsz, seq, d)
```
